```python
import jax
import jax.numpy as jnp
from jax import lax
import numpy as np

D_MODEL = 2048
BATCH = 8
SEQ = 2048
DEPTH = 2

GRID_W = 64
CTX_LEN = 256
HEAD_DIM = 128
N_HEADS = 8
N_KV_HEADS = 2
KV_GROUP = N_HEADS // N_KV_HEADS
Q_W = N_HEADS * HEAD_DIM
KV_W = N_KV_HEADS * HEAD_DIM
Q_BLOCK = 128
ROPE_THETA = 10000.0
ROPE_AXIS_DIM = HEAD_DIM // 2
POOL_WINDOWS = (2, 4, 8, 16)
POOL_GROUPS = len(POOL_WINDOWS)
POOL_W = D_MODEL // 4
POOL_GC = POOL_W // POOL_GROUPS
SGU_W = D_MODEL // 4
SGU_GROUPS = 4
SGU_GC = SGU_W // SGU_GROUPS
SGU_CHUNK = 128
CONV_W = D_MODEL // 4
CONV_K = 3
N_BRANCH = 4
D_FF = -(-8 * D_MODEL // (3 * 256)) * 256
ALPHA = (2 * DEPTH) ** 0.25
BETA = (8 * DEPTH) ** -0.25
LN_EPS = 1e-5
RMS_EPS = 1e-6

OFF_K = Q_W
OFF_V = OFF_K + KV_W
OFF_POOL = OFF_V + KV_W
OFF_U = OFF_POOL + POOL_W
OFF_VG = OFF_U + SGU_W
OFF_CB = OFF_VG + SGU_W
OFF_CC = OFF_CB + CONV_W
OFF_CX = OFF_CC + CONV_W
IN_W = OFF_CX + CONV_W

kernel_name = 'hybrid_diffusion_parallel_mixer'


def layer_norm(x, g, b):
    xf = x.astype(jnp.float32)
    mu = jnp.mean(xf, axis=-1, keepdims=True)
    var = jnp.mean(jnp.square(xf - mu), axis=-1, keepdims=True)
    return ((xf - mu) * lax.rsqrt(var + LN_EPS) * g + b).astype(x.dtype)


def rms_norm(x, g):
    xf = x.astype(jnp.float32)
    ms = jnp.mean(jnp.square(xf), axis=-1, keepdims=True)
    return (xf * lax.rsqrt(ms + RMS_EPS) * g).astype(x.dtype)


def axial_rope_tables(n, dtype):
    rows = n // GRID_W
    row = jnp.repeat(jnp.arange(rows), GRID_W).astype(jnp.float32)
    col = jnp.tile(jnp.arange(GRID_W), rows).astype(jnp.float32)
    inv = ROPE_THETA ** (-jnp.arange(0, ROPE_AXIS_DIM, 2, dtype=jnp.float32) / ROPE_AXIS_DIM)
    ang_r = row[:, None] * inv
    ang_c = col[:, None] * inv
    f = lambda a: a[None, :, None, :].astype(dtype)
    return (f(jnp.cos(ang_r)), f(jnp.sin(ang_r)), f(jnp.cos(ang_c)), f(jnp.sin(ang_c)))


def rotate(x, cos, sin):
    h = x.shape[-1] // 2
    x1, x2 = x[..., :h], x[..., h:]
    return jnp.concatenate([x1 * cos - x2 * sin, x2 * cos + x1 * sin], axis=-1)


def apply_axial_rope(x, tables):
    cr, sr, cc, sc = tables
    return jnp.concatenate([rotate(x[..., :ROPE_AXIS_DIM], cr, sr),
                            rotate(x[..., ROPE_AXIS_DIM:], cc, sc)], axis=-1)


def block_attention(q, k, v):
    b, n, h, d = q.shape
    nb = n // Q_BLOCK
    qb = q.reshape(b, nb, Q_BLOCK, N_KV_HEADS, KV_GROUP, d).transpose(1, 0, 2, 3, 4, 5)
    scale = d ** -0.5

    def one_block(qi):
        s = jnp.einsum('bqkgd,bskd->bkgqs', qi, k, preferred_element_type=jnp.float32) * scale
        p = jax.nn.softmax(s, axis=-1).astype(v.dtype)
        return jnp.einsum('bkgqs,bskd->bqkgd', p, v)

    o = lax.map(one_block, qb)
    return o.transpose(1, 0, 2, 3, 4, 5).reshape(b, n, h * d)


def kv_heads(pkv, k_g):
    b, n, _ = pkv.shape
    k = rms_norm(pkv[..., :KV_W].reshape(b, n, N_KV_HEADS, HEAD_DIM), k_g)
    v = pkv[..., KV_W:].reshape(b, n, N_KV_HEADS, HEAD_DIM)
    return k, v


def multiscale_pool(z):
    b, n, ch = z.shape
    cs = jnp.cumsum(z.astype(jnp.float32), axis=1)
    cs = jnp.concatenate([jnp.zeros((b, 1, ch), jnp.float32), cs], axis=1)
    t = jnp.arange(n)
    means = []
    for g, w in enumerate(POOL_WINDOWS):
        left = w // 2
        right = w - 1 - left
        hi = jnp.clip(t + right + 1, 0, n)
        lo = jnp.clip(t - left, 0, n)
        seg = cs[:, :, g * POOL_GC:(g + 1) * POOL_GC]
        total = jnp.take(seg, hi, axis=1) - jnp.take(seg, lo, axis=1)
        means.append(total / (hi - lo).astype(jnp.float32)[None, :, None])
    return jnp.concatenate(means, axis=-1).astype(z.dtype) - z


def spatial_gating(u, v, ln_g, ln_b, w_s, b_s):
    b, n, _ = v.shape
    v = layer_norm(v, ln_g, ln_b)
    vc = v.reshape(b, n // SGU_CHUNK, SGU_CHUNK, SGU_GROUPS, SGU_GC)
    s = jnp.einsum('gpq,bnqgc->bnpgc', w_s, vc) + b_s.T[None, None, :, :, None]
    return u * s.reshape(b, n, SGU_W)


def short_conv(z, w):
    zp = jnp.pad(z, ((0, 0), (1, 1), (0, 0)))
    return zp[:, :-2] * w[0] + zp[:, 1:-1] * w[1] + zp[:, 2:] * w[2]


def local_branches(pr, pool_w, pool_scale, sgu_ln_g, sgu_ln_b, sgu_w, sgu_b, conv_w):
    b, n, _ = pr.shape
    d = multiscale_pool(pr[..., OFF_POOL:OFF_U])
    y_pool = jnp.einsum('bngc,gce->bnge', d.reshape(b, n, POOL_GROUPS, POOL_GC), pool_w).reshape(b, n, POOL_W) * pool_scale
    y_sgu = spatial_gating(jax.nn.gelu(pr[..., OFF_U:OFF_VG]), jax.nn.gelu(pr[..., OFF_VG:OFF_CB]),
                           sgu_ln_g, sgu_ln_b, sgu_w, sgu_b)
    y_conv = pr[..., OFF_CB:OFF_CC] * short_conv(pr[..., OFF_CC:OFF_CX] * pr[..., OFF_CX:IN_W], conv_w)
    return y_pool, y_sgu, y_conv


def merge_branches(h, branches, w_brs, w_gate, b_gate, w_o):
    terms = []
    for k in range(N_BRANCH):
        g = jax.nn.sigmoid(h @ w_gate[:, k * D_MODEL:(k + 1) * D_MODEL] + b_gate[k * D_MODEL:(k + 1) * D_MODEL])
        terms.append(g * (branches[k] @ w_brs[k]))
    merged = terms[0] + terms[1] + terms[2] + terms[3]
    return merged @ w_o


def hybrid_mixer(h_c, h_l, rope, ctx_out, w_in, q_g, k_g, pool_w, pool_scale, sgu_ln_g, sgu_ln_b,
                 sgu_w, sgu_b, conv_w, w_br_attn, w_br_pool, w_br_sgu, w_br_conv, w_gate, b_gate, w_o):
    b, n, _ = h_l.shape
    nc = h_c.shape[1]
    w_brs = (w_br_attn, w_br_pool, w_br_sgu, w_br_conv)
    p_l = h_l @ w_in
    q_l = apply_axial_rope(rms_norm(p_l[..., :Q_W].reshape(b, n, N_HEADS, HEAD_DIM), q_g), rope)
    k_l, v_l = kv_heads(p_l[..., OFF_K:OFF_POOL], k_g)
    k_l = apply_axial_rope(k_l, rope)
    if ctx_out:
        p_c = h_c @ w_in
        pkv_c = p_c[..., OFF_K:OFF_POOL]
    else:
        pkv_c = h_c @ w_in[:, OFF_K:OFF_POOL]
    k_c, v_c = kv_heads(pkv_c, k_g)
    a_l = block_attention(q_l, jnp.concatenate([k_c, k_l], axis=1), jnp.concatenate([v_c, v_l], axis=1))
    out_l = merge_branches(h_l, (a_l,) + local_branches(p_l, pool_w, pool_scale, sgu_ln_g, sgu_ln_b, sgu_w, sgu_b, conv_w),
                           w_brs, w_gate, b_gate, w_o)
    if not ctx_out:
        return None, out_l
    q_c = rms_norm(p_c[..., :Q_W].reshape(b, nc, N_HEADS, HEAD_DIM), q_g)
    a_c = block_attention(q_c, k_c, v_c)
    out_c = merge_branches(h_c, (a_c,) + local_branches(p_c, pool_w, pool_scale, sgu_ln_g, sgu_ln_b, sgu_w, sgu_b, conv_w),
                           w_brs, w_gate, b_gate, w_o)
    return out_c, out_l


def swiglu(h, w_g, w_u, w_d):
    return (jax.nn.silu(h @ w_g) * (h @ w_u)) @ w_d


def _fwd_setup_inputs(seed: int = 0) -> dict:
    key = jax.random.key(seed)
    ks = jax.random.split(key, 32)

    def nrm(k, shape, std):
        return jax.random.normal(k, shape, jnp.float32) * std

    def gain(k, shape):
        return 1.0 + nrm(k, shape, 0.02)

    L = DEPTH
    d = D_MODEL
    return {
        'x': nrm(ks[0], (BATCH, SEQ, d), 1.0),
        'c': nrm(ks[1], (BATCH, d), 1.0),
        'ctx': nrm(ks[2], (BATCH, CTX_LEN, d), 1.0),
        'c_ctx': nrm(ks[3], (d,), 1.0),
        'w_ada': nrm(ks[4], (L, d, 6 * d), 0.5 * d ** -0.5),
        'b_ada': nrm(ks[5], (L, 6 * d), 0.01),
        'w_in': nrm(ks[6], (L, d, IN_W), d ** -0.5),
        'q_norm_g': gain(ks[7], (L, HEAD_DIM)),
        'k_norm_g': gain(ks[8], (L, HEAD_DIM)),
        'pool_w': nrm(ks[9], (L, POOL_GROUPS, POOL_GC, POOL_GC), POOL_GC ** -0.5),
        'pool_scale': 1.0 + nrm(ks[10], (L, POOL_W), 0.1),
        'sgu_ln_g': gain(ks[11], (L, SGU_W)),
        'sgu_ln_b': nrm(ks[12], (L, SGU_W), 0.02),
        'sgu_w': nrm(ks[13], (L, SGU_GROUPS, SGU_CHUNK, SGU_CHUNK), SGU_CHUNK ** -0.5),
        'sgu_b': 1.0 + nrm(ks[14], (L, SGU_GROUPS, SGU_CHUNK), 0.02),
        'conv_w': nrm(ks[15], (L, CONV_K, CONV_W), CONV_K ** -0.5),
        'w_br_attn': nrm(ks[16], (L, Q_W, d), Q_W ** -0.5),
        'w_br_pool': nrm(ks[17], (L, POOL_W, d), POOL_W ** -0.5),
        'w_br_sgu': nrm(ks[18], (L, SGU_W, d), SGU_W ** -0.5),
        'w_br_conv': nrm(ks[19], (L, CONV_W, d), CONV_W ** -0.5),
        'w_gate': nrm(ks[20], (L, d, N_BRANCH * d), d ** -0.5),
        'b_gate': nrm(ks[21], (L, N_BRANCH * d), 0.01),
        'w_o': nrm(ks[22], (L, d, d), BETA * d ** -0.5),
        'ln1_g': gain(ks[23], (L, d)),
        'ln1_b': nrm(ks[24], (L, d), 0.02),
        'w_ff_gate': nrm(ks[25], (L, d, D_FF), d ** -0.5),
        'w_ff_up': nrm(ks[26], (L, d, D_FF), d ** -0.5),
        'w_ff_down': nrm(ks[27], (L, D_FF, d), BETA * D_FF ** -0.5),
        'ln2_g': gain(ks[28], (L, d)),
        'ln2_b': nrm(ks[29], (L, d), 0.02),
    }


def _fwd_reference(x, c, ctx, c_ctx, w_ada, b_ada, w_in, q_norm_g, k_norm_g, pool_w, pool_scale,
              sgu_ln_g, sgu_ln_b, sgu_w, sgu_b, conv_w, w_br_attn, w_br_pool, w_br_sgu, w_br_conv,
              w_gate, b_gate, w_o, ln1_g, ln1_b, w_ff_gate, w_ff_up, w_ff_down, ln2_g, ln2_b):
    rope = axial_rope_tables(x.shape[1], x.dtype)
    x_l, x_c = x, ctx
    for i in range(DEPTH):
        last = i == DEPTH - 1
        mod_l = jax.nn.silu(c) @ w_ada[i] + b_ada[i]
        mod_c = jax.nn.silu(c_ctx) @ w_ada[i] + b_ada[i]
        sh1_l, sc1_l, g1_l, sh2_l, sc2_l, g2_l = jnp.split(mod_l[:, None, :], 6, axis=-1)
        sh1_c, sc1_c, g1_c, sh2_c, sc2_c, g2_c = jnp.split(mod_c, 6, axis=-1)
        h_l = x_l * (1.0 + sc1_l) + sh1_l
        h_c = x_c * (1.0 + sc1_c) + sh1_c
        m_c, m_l = hybrid_mixer(h_c, h_l, rope, not last, w_in[i], q_norm_g[i], k_norm_g[i], pool_w[i],
                                pool_scale[i], sgu_ln_g[i], sgu_ln_b[i], sgu_w[i], sgu_b[i], conv_w[i],
                                w_br_attn[i], w_br_pool[i], w_br_sgu[i], w_br_conv[i], w_gate[i], b_gate[i], w_o[i])
        x_l = layer_norm(ALPHA * x_l + g1_l * m_l, ln1_g[i], ln1_b[i])
        h_l = x_l * (1.0 + sc2_l) + sh2_l
        x_l = layer_norm(ALPHA * x_l + g2_l * swiglu(h_l, w_ff_gate[i], w_ff_up[i], w_ff_down[i]), ln2_g[i], ln2_b[i])
        if not last:
            x_c = layer_norm(ALPHA * x_c + g1_c * m_c, ln1_g[i], ln1_b[i])
            h_c = x_c * (1.0 + sc2_c) + sh2_c
            x_c = layer_norm(ALPHA * x_c + g2_c * swiglu(h_c, w_ff_gate[i], w_ff_up[i], w_ff_down[i]), ln2_g[i], ln2_b[i])
    return x_l


import jax as _jax
import jax.numpy as _jnp

TWIN_FORMAT = 'train_step'
FWD_PARAMS = ['x', 'c', 'ctx', 'c_ctx', 'w_ada', 'b_ada', 'w_in', 'q_norm_g', 'k_norm_g', 'pool_w', 'pool_scale', 'sgu_ln_g', 'sgu_ln_b', 'sgu_w', 'sgu_b', 'conv_w', 'w_br_attn', 'w_br_pool', 'w_br_sgu', 'w_br_conv', 'w_gate', 'b_gate', 'w_o', 'ln1_g', 'ln1_b', 'w_ff_gate', 'w_ff_up', 'w_ff_down', 'ln2_g', 'ln2_b']
TWIN_WEIGHTS = ['c_ctx', 'w_ada', 'b_ada', 'w_in', 'q_norm_g', 'k_norm_g', 'pool_w', 'pool_scale', 'sgu_ln_g', 'sgu_ln_b', 'sgu_w', 'sgu_b', 'conv_w', 'w_br_attn', 'w_br_pool', 'w_br_sgu', 'w_br_conv', 'w_gate', 'b_gate', 'w_o', 'ln1_g', 'ln1_b', 'w_ff_gate', 'w_ff_up', 'w_ff_down', 'ln2_g', 'ln2_b']
TWIN_DIFF_INPUT = 'x'
TWIN_INPUTS = ['x', 'c', 'ctx', 'c_ctx', 'w_ada', 'b_ada', 'w_in', 'q_norm_g', 'k_norm_g', 'pool_w', 'pool_scale', 'sgu_ln_g', 'sgu_ln_b', 'sgu_w', 'sgu_b', 'conv_w', 'w_br_attn', 'w_br_pool', 'w_br_sgu', 'w_br_conv', 'w_gate', 'b_gate', 'w_o', 'ln1_g', 'ln1_b', 'w_ff_gate', 'w_ff_up', 'w_ff_down', 'ln2_g', 'ln2_b', 'loss_target', 'm_c_ctx', 'm_w_ada', 'm_b_ada', 'm_w_in', 'm_q_norm_g', 'm_k_norm_g', 'm_pool_w', 'm_pool_scale', 'm_sgu_ln_g', 'm_sgu_ln_b', 'm_sgu_w', 'm_sgu_b', 'm_conv_w', 'm_w_br_attn', 'm_w_br_pool', 'm_w_br_sgu', 'm_w_br_conv', 'm_w_gate', 'm_b_gate', 'm_w_o', 'm_ln1_g', 'm_ln1_b', 'm_w_ff_gate', 'm_w_ff_up', 'm_w_ff_down', 'm_ln2_g', 'm_ln2_b', 'v_c_ctx', 'v_w_ada', 'v_b_ada', 'v_w_in', 'v_q_norm_g', 'v_k_norm_g', 'v_pool_w', 'v_pool_scale', 'v_sgu_ln_g', 'v_sgu_ln_b', 'v_sgu_w', 'v_sgu_b', 'v_conv_w', 'v_w_br_attn', 'v_w_br_pool', 'v_w_br_sgu', 'v_w_br_conv', 'v_w_gate', 'v_b_gate', 'v_w_o', 'v_ln1_g', 'v_ln1_b', 'v_w_ff_gate', 'v_w_ff_up', 'v_w_ff_down', 'v_ln2_g', 'v_ln2_b']
TWIN_OUTPUTS = ['loss', 'grad_x', 'grad_c_ctx', 'grad_w_ada', 'grad_b_ada', 'grad_w_in', 'grad_q_norm_g', 'grad_k_norm_g', 'grad_pool_w', 'grad_pool_scale', 'grad_sgu_ln_g', 'grad_sgu_ln_b', 'grad_sgu_w', 'grad_sgu_b', 'grad_conv_w', 'grad_w_br_attn', 'grad_w_br_pool', 'grad_w_br_sgu', 'grad_w_br_conv', 'grad_w_gate', 'grad_b_gate', 'grad_w_o', 'grad_ln1_g', 'grad_ln1_b', 'grad_w_ff_gate', 'grad_w_ff_up', 'grad_w_ff_down', 'grad_ln2_g', 'grad_ln2_b', 'delta_c_ctx', 'delta_w_ada', 'delta_b_ada', 'delta_w_in', 'delta_q_norm_g', 'delta_k_norm_g', 'delta_pool_w', 'delta_pool_scale', 'delta_sgu_ln_g', 'delta_sgu_ln_b', 'delta_sgu_w', 'delta_sgu_b', 'delta_conv_w', 'delta_w_br_attn', 'delta_w_br_pool', 'delta_w_br_sgu', 'delta_w_br_conv', 'delta_w_gate', 'delta_b_gate', 'delta_w_o', 'delta_ln1_g', 'delta_ln1_b', 'delta_w_ff_gate', 'delta_w_ff_up', 'delta_w_ff_down', 'delta_ln2_g', 'delta_ln2_b', 'new_m_c_ctx', 'new_m_w_ada', 'new_m_b_ada', 'new_m_w_in', 'new_m_q_norm_g', 'new_m_k_norm_g', 'new_m_pool_w', 'new_m_pool_scale', 'new_m_sgu_ln_g', 'new_m_sgu_ln_b', 'new_m_sgu_w', 'new_m_sgu_b', 'new_m_conv_w', 'new_m_w_br_attn', 'new_m_w_br_pool', 'new_m_w_br_sgu', 'new_m_w_br_conv', 'new_m_w_gate', 'new_m_b_gate', 'new_m_w_o', 'new_m_ln1_g', 'new_m_ln1_b', 'new_m_w_ff_gate', 'new_m_w_ff_up', 'new_m_w_ff_down', 'new_m_ln2_g', 'new_m_ln2_b', 'new_v_c_ctx', 'new_v_w_ada', 'new_v_b_ada', 'new_v_w_in', 'new_v_q_norm_g', 'new_v_k_norm_g', 'new_v_pool_w', 'new_v_pool_scale', 'new_v_sgu_ln_g', 'new_v_sgu_ln_b', 'new_v_sgu_w', 'new_v_sgu_b', 'new_v_conv_w', 'new_v_w_br_attn', 'new_v_w_br_pool', 'new_v_w_br_sgu', 'new_v_w_br_conv', 'new_v_w_gate', 'new_v_b_gate', 'new_v_w_o', 'new_v_ln1_g', 'new_v_ln1_b', 'new_v_w_ff_gate', 'new_v_w_ff_up', 'new_v_w_ff_down', 'new_v_ln2_g', 'new_v_ln2_b']
TWIN_LEAF_KINDS = {'loss': 'loss', 'grad_x': 'grad_x', 'grad_c_ctx': 'grad_w', 'grad_w_ada': 'grad_w', 'grad_b_ada': 'grad_w', 'grad_w_in': 'grad_w', 'grad_q_norm_g': 'grad_w', 'grad_k_norm_g': 'grad_w', 'grad_pool_w': 'grad_w', 'grad_pool_scale': 'grad_w', 'grad_sgu_ln_g': 'grad_w', 'grad_sgu_ln_b': 'grad_w', 'grad_sgu_w': 'grad_w', 'grad_sgu_b': 'grad_w', 'grad_conv_w': 'grad_w', 'grad_w_br_attn': 'grad_w', 'grad_w_br_pool': 'grad_w', 'grad_w_br_sgu': 'grad_w', 'grad_w_br_conv': 'grad_w', 'grad_w_gate': 'grad_w', 'grad_b_gate': 'grad_w', 'grad_w_o': 'grad_w', 'grad_ln1_g': 'grad_w', 'grad_ln1_b': 'grad_w', 'grad_w_ff_gate': 'grad_w', 'grad_w_ff_up': 'grad_w', 'grad_w_ff_down': 'grad_w', 'grad_ln2_g': 'grad_w', 'grad_ln2_b': 'grad_w', 'delta_c_ctx': 'delta_w', 'delta_w_ada': 'delta_w', 'delta_b_ada': 'delta_w', 'delta_w_in': 'delta_w', 'delta_q_norm_g': 'delta_w', 'delta_k_norm_g': 'delta_w', 'delta_pool_w': 'delta_w', 'delta_pool_scale': 'delta_w', 'delta_sgu_ln_g': 'delta_w', 'delta_sgu_ln_b': 'delta_w', 'delta_sgu_w': 'delta_w', 'delta_sgu_b': 'delta_w', 'delta_conv_w': 'delta_w', 'delta_w_br_attn': 'delta_w', 'delta_w_br_pool': 'delta_w', 'delta_w_br_sgu': 'delta_w', 'delta_w_br_conv': 'delta_w', 'delta_w_gate': 'delta_w', 'delta_b_gate': 'delta_w', 'delta_w_o': 'delta_w', 'delta_ln1_g': 'delta_w', 'delta_ln1_b': 'delta_w', 'delta_w_ff_gate': 'delta_w', 'delta_w_ff_up': 'delta_w', 'delta_w_ff_down': 'delta_w', 'delta_ln2_g': 'delta_w', 'delta_ln2_b': 'delta_w', 'new_m_c_ctx': 'new_m', 'new_m_w_ada': 'new_m', 'new_m_b_ada': 'new_m', 'new_m_w_in': 'new_m', 'new_m_q_norm_g': 'new_m', 'new_m_k_norm_g': 'new_m', 'new_m_pool_w': 'new_m', 'new_m_pool_scale': 'new_m', 'new_m_sgu_ln_g': 'new_m', 'new_m_sgu_ln_b': 'new_m', 'new_m_sgu_w': 'new_m', 'new_m_sgu_b': 'new_m', 'new_m_conv_w': 'new_m', 'new_m_w_br_attn': 'new_m', 'new_m_w_br_pool': 'new_m', 'new_m_w_br_sgu': 'new_m', 'new_m_w_br_conv': 'new_m', 'new_m_w_gate': 'new_m', 'new_m_b_gate': 'new_m', 'new_m_w_o': 'new_m', 'new_m_ln1_g': 'new_m', 'new_m_ln1_b': 'new_m', 'new_m_w_ff_gate': 'new_m', 'new_m_w_ff_up': 'new_m', 'new_m_w_ff_down': 'new_m', 'new_m_ln2_g': 'new_m', 'new_m_ln2_b': 'new_m', 'new_v_c_ctx': 'new_v', 'new_v_w_ada': 'new_v', 'new_v_b_ada': 'new_v', 'new_v_w_in': 'new_v', 'new_v_q_norm_g': 'new_v', 'new_v_k_norm_g': 'new_v', 'new_v_pool_w': 'new_v', 'new_v_pool_scale': 'new_v', 'new_v_sgu_ln_g': 'new_v', 'new_v_sgu_ln_b': 'new_v', 'new_v_sgu_w': 'new_v', 'new_v_sgu_b': 'new_v', 'new_v_conv_w': 'new_v', 'new_v_w_br_attn': 'new_v', 'new_v_w_br_pool': 'new_v', 'new_v_w_br_sgu': 'new_v', 'new_v_w_br_conv': 'new_v', 'new_v_w_gate': 'new_v', 'new_v_b_gate': 'new_v', 'new_v_w_o': 'new_v', 'new_v_ln1_g': 'new_v', 'new_v_ln1_b': 'new_v', 'new_v_w_ff_gate': 'new_v', 'new_v_w_ff_up': 'new_v', 'new_v_w_ff_down': 'new_v', 'new_v_ln2_g': 'new_v', 'new_v_ln2_b': 'new_v'}


def _forward(args):
    return _fwd_reference(*[args[k] for k in FWD_PARAMS])


def _output_shape():
    out = _jax.eval_shape(lambda: _forward(_fwd_setup_inputs(0)))
    return out.shape, out.dtype

N_MICROBATCH = 1
ADAM_LR = 0.001
ADAM_B1 = 0.9
ADAM_B2 = 0.999
ADAM_EPS = 1e-08
ADAM_WD = 0.01
ADAM_STEP = 10
PER_EXAMPLE_BATCH_AXIS = {'x': 0, 'c': 0, 'ctx': 0, 'loss_target': 0}
SHARED_INPUTS = []
_WEIGHT_DTYPES = {'c_ctx': _jnp.float32, 'w_ada': _jnp.float32, 'b_ada': _jnp.float32, 'w_in': _jnp.float32, 'q_norm_g': _jnp.float32, 'k_norm_g': _jnp.float32, 'pool_w': _jnp.float32, 'pool_scale': _jnp.float32, 'sgu_ln_g': _jnp.float32, 'sgu_ln_b': _jnp.float32, 'sgu_w': _jnp.float32, 'sgu_b': _jnp.float32, 'conv_w': _jnp.float32, 'w_br_attn': _jnp.float32, 'w_br_pool': _jnp.float32, 'w_br_sgu': _jnp.float32, 'w_br_conv': _jnp.float32, 'w_gate': _jnp.float32, 'b_gate': _jnp.float32, 'w_o': _jnp.float32, 'ln1_g': _jnp.float32, 'ln1_b': _jnp.float32, 'w_ff_gate': _jnp.float32, 'w_ff_up': _jnp.float32, 'w_ff_down': _jnp.float32, 'ln2_g': _jnp.float32, 'ln2_b': _jnp.float32}
MOMENT_SCALE = {'c_ctx': 8.764238e-04, 'w_ada': 8.222578e-03, 'b_ada': 1.412256e-02, 'w_in': 6.795291e-03, 'q_norm_g': 1.240305e-03, 'k_norm_g': 1.237074e-03, 'pool_w': 6.766257e-03, 'pool_scale': 6.791369e-03, 'sgu_ln_g': 5.248590e-03, 'sgu_ln_b': 5.297259e-03, 'sgu_w': 5.213494e-03, 'sgu_b': 5.307000e-03, 'conv_w': 9.638163e-03, 'w_br_attn': 1.413528e-03, 'w_br_pool': 3.377929e-03, 'w_br_sgu': 3.845473e-03, 'w_br_conv': 4.735072e-03, 'w_gate': 1.439013e-03, 'b_gate': 1.355401e-03, 'w_o': 1.420651e-02, 'ln1_g': 2.818876e-01, 'ln1_b': 1.440006e-01, 'w_ff_gate': 2.988919e-03, 'w_ff_up': 2.898575e-03, 'w_ff_down': 9.621637e-03, 'ln2_g': 5.673959e+00, 'ln2_b': 2.456881e-01}


def _to_microbatches(a, axis):
    t = _jnp.moveaxis(a, axis, 0)
    t = t.reshape((N_MICROBATCH, t.shape[0] // N_MICROBATCH) + t.shape[1:])
    return _jnp.moveaxis(t, 1, axis + 1)


def setup_inputs(seed: int = 0) -> dict:
    inp = _fwd_setup_inputs(seed)
    key = _jax.random.fold_in(_jax.random.key(seed), 7919)
    shape, _ = _output_shape()
    out = dict(inp)
    out["loss_target"] = _jax.random.normal(_jax.random.fold_in(key, 0), shape, _jnp.float32)
    for i, name in enumerate(TWIN_WEIGHTS):
        w = inp[name].astype(_jnp.float32)
        if MOMENT_SCALE is None:
            s = _jnp.sqrt(_jnp.mean(_jnp.square(w)) + 1e-30)
        else:
            s = MOMENT_SCALE[name]
        km, kv = _jax.random.split(_jax.random.fold_in(key, i + 1))
        out[name] = w
        out["m_" + name] = s * _jax.random.normal(km, w.shape, _jnp.float32)
        out["v_" + name] = (s * s) * _jax.random.uniform(kv, w.shape, _jnp.float32, 0.5, 1.5)
    if N_MICROBATCH > 1:
        for name, axis in PER_EXAMPLE_BATCH_AXIS.items():
            out[name] = _to_microbatches(out[name], axis)
    return {'x': out['x'], 'c': out['c'], 'ctx': out['ctx'], 'c_ctx': out['c_ctx'], 'w_ada': out['w_ada'], 'b_ada': out['b_ada'], 'w_in': out['w_in'], 'q_norm_g': out['q_norm_g'], 'k_norm_g': out['k_norm_g'], 'pool_w': out['pool_w'], 'pool_scale': out['pool_scale'], 'sgu_ln_g': out['sgu_ln_g'], 'sgu_ln_b': out['sgu_ln_b'], 'sgu_w': out['sgu_w'], 'sgu_b': out['sgu_b'], 'conv_w': out['conv_w'], 'w_br_attn': out['w_br_attn'], 'w_br_pool': out['w_br_pool'], 'w_br_sgu': out['w_br_sgu'], 'w_br_conv': out['w_br_conv'], 'w_gate': out['w_gate'], 'b_gate': out['b_gate'], 'w_o': out['w_o'], 'ln1_g': out['ln1_g'], 'ln1_b': out['ln1_b'], 'w_ff_gate': out['w_ff_gate'], 'w_ff_up': out['w_ff_up'], 'w_ff_down': out['w_ff_down'], 'ln2_g': out['ln2_g'], 'ln2_b': out['ln2_b'], 'loss_target': out['loss_target'], 'm_c_ctx': out['m_c_ctx'], 'm_w_ada': out['m_w_ada'], 'm_b_ada': out['m_b_ada'], 'm_w_in': out['m_w_in'], 'm_q_norm_g': out['m_q_norm_g'], 'm_k_norm_g': out['m_k_norm_g'], 'm_pool_w': out['m_pool_w'], 'm_pool_scale': out['m_pool_scale'], 'm_sgu_ln_g': out['m_sgu_ln_g'], 'm_sgu_ln_b': out['m_sgu_ln_b'], 'm_sgu_w': out['m_sgu_w'], 'm_sgu_b': out['m_sgu_b'], 'm_conv_w': out['m_conv_w'], 'm_w_br_attn': out['m_w_br_attn'], 'm_w_br_pool': out['m_w_br_pool'], 'm_w_br_sgu': out['m_w_br_sgu'], 'm_w_br_conv': out['m_w_br_conv'], 'm_w_gate': out['m_w_gate'], 'm_b_gate': out['m_b_gate'], 'm_w_o': out['m_w_o'], 'm_ln1_g': out['m_ln1_g'], 'm_ln1_b': out['m_ln1_b'], 'm_w_ff_gate': out['m_w_ff_gate'], 'm_w_ff_up': out['m_w_ff_up'], 'm_w_ff_down': out['m_w_ff_down'], 'm_ln2_g': out['m_ln2_g'], 'm_ln2_b': out['m_ln2_b'], 'v_c_ctx': out['v_c_ctx'], 'v_w_ada': out['v_w_ada'], 'v_b_ada': out['v_b_ada'], 'v_w_in': out['v_w_in'], 'v_q_norm_g': out['v_q_norm_g'], 'v_k_norm_g': out['v_k_norm_g'], 'v_pool_w': out['v_pool_w'], 'v_pool_scale': out['v_pool_scale'], 'v_sgu_ln_g': out['v_sgu_ln_g'], 'v_sgu_ln_b': out['v_sgu_ln_b'], 'v_sgu_w': out['v_sgu_w'], 'v_sgu_b': out['v_sgu_b'], 'v_conv_w': out['v_conv_w'], 'v_w_br_attn': out['v_w_br_attn'], 'v_w_br_pool': out['v_w_br_pool'], 'v_w_br_sgu': out['v_w_br_sgu'], 'v_w_br_conv': out['v_w_br_conv'], 'v_w_gate': out['v_w_gate'], 'v_b_gate': out['v_b_gate'], 'v_w_o': out['v_w_o'], 'v_ln1_g': out['v_ln1_g'], 'v_ln1_b': out['v_ln1_b'], 'v_w_ff_gate': out['v_w_ff_gate'], 'v_w_ff_up': out['v_w_ff_up'], 'v_w_ff_down': out['v_w_ff_down'], 'v_ln2_g': out['v_ln2_g'], 'v_ln2_b': out['v_ln2_b']}


def _loss(weights, diff, rest, loss_target):
    with _jax.named_scope("forward"):
        args = {**rest, TWIN_DIFF_INPUT: diff, **{k: w.astype(_WEIGHT_DTYPES[k]) for k, w in weights.items()}}
        y = _forward(args)
    with _jax.named_scope("loss_head"):
        err = _jnp.square(y.astype(_jnp.float32) - loss_target)
        return 0.5 * _jnp.sum(_jnp.mean(err, axis=-1)) if err.ndim else 0.5 * err


def _adamw(w, g, m, v):
    m = ADAM_B1 * m + (1.0 - ADAM_B1) * g
    v = ADAM_B2 * v + (1.0 - ADAM_B2) * _jnp.square(g)
    m_hat = m / (1.0 - ADAM_B1 ** ADAM_STEP)
    v_hat = v / (1.0 - ADAM_B2 ** ADAM_STEP)
    delta = -ADAM_LR * (m_hat / (_jnp.sqrt(v_hat) + ADAM_EPS) + ADAM_WD * w)
    return delta, m, v


def reference(x, c, ctx, c_ctx, w_ada, b_ada, w_in, q_norm_g, k_norm_g, pool_w, pool_scale, sgu_ln_g, sgu_ln_b, sgu_w, sgu_b, conv_w, w_br_attn, w_br_pool, w_br_sgu, w_br_conv, w_gate, b_gate, w_o, ln1_g, ln1_b, w_ff_gate, w_ff_up, w_ff_down, ln2_g, ln2_b, loss_target, m_c_ctx, m_w_ada, m_b_ada, m_w_in, m_q_norm_g, m_k_norm_g, m_pool_w, m_pool_scale, m_sgu_ln_g, m_sgu_ln_b, m_sgu_w, m_sgu_b, m_conv_w, m_w_br_attn, m_w_br_pool, m_w_br_sgu, m_w_br_conv, m_w_gate, m_b_gate, m_w_o, m_ln1_g, m_ln1_b, m_w_ff_gate, m_w_ff_up, m_w_ff_down, m_ln2_g, m_ln2_b, v_c_ctx, v_w_ada, v_b_ada, v_w_in, v_q_norm_g, v_k_norm_g, v_pool_w, v_pool_scale, v_sgu_ln_g, v_sgu_ln_b, v_sgu_w, v_sgu_b, v_conv_w, v_w_br_attn, v_w_br_pool, v_w_br_sgu, v_w_br_conv, v_w_gate, v_b_gate, v_w_o, v_ln1_g, v_ln1_b, v_w_ff_gate, v_w_ff_up, v_w_ff_down, v_ln2_g, v_ln2_b):
    given = dict(x=x, c=c, ctx=ctx, c_ctx=c_ctx, w_ada=w_ada, b_ada=b_ada, w_in=w_in, q_norm_g=q_norm_g, k_norm_g=k_norm_g, pool_w=pool_w, pool_scale=pool_scale, sgu_ln_g=sgu_ln_g, sgu_ln_b=sgu_ln_b, sgu_w=sgu_w, sgu_b=sgu_b, conv_w=conv_w, w_br_attn=w_br_attn, w_br_pool=w_br_pool, w_br_sgu=w_br_sgu, w_br_conv=w_br_conv, w_gate=w_gate, b_gate=b_gate, w_o=w_o, ln1_g=ln1_g, ln1_b=ln1_b, w_ff_gate=w_ff_gate, w_ff_up=w_ff_up, w_ff_down=w_ff_down, ln2_g=ln2_g, ln2_b=ln2_b, loss_target=loss_target, m_c_ctx=m_c_ctx, m_w_ada=m_w_ada, m_b_ada=m_b_ada, m_w_in=m_w_in, m_q_norm_g=m_q_norm_g, m_k_norm_g=m_k_norm_g, m_pool_w=m_pool_w, m_pool_scale=m_pool_scale, m_sgu_ln_g=m_sgu_ln_g, m_sgu_ln_b=m_sgu_ln_b, m_sgu_w=m_sgu_w, m_sgu_b=m_sgu_b, m_conv_w=m_conv_w, m_w_br_attn=m_w_br_attn, m_w_br_pool=m_w_br_pool, m_w_br_sgu=m_w_br_sgu, m_w_br_conv=m_w_br_conv, m_w_gate=m_w_gate, m_b_gate=m_b_gate, m_w_o=m_w_o, m_ln1_g=m_ln1_g, m_ln1_b=m_ln1_b, m_w_ff_gate=m_w_ff_gate, m_w_ff_up=m_w_ff_up, m_w_ff_down=m_w_ff_down, m_ln2_g=m_ln2_g, m_ln2_b=m_ln2_b, v_c_ctx=v_c_ctx, v_w_ada=v_w_ada, v_b_ada=v_b_ada, v_w_in=v_w_in, v_q_norm_g=v_q_norm_g, v_k_norm_g=v_k_norm_g, v_pool_w=v_pool_w, v_pool_scale=v_pool_scale, v_sgu_ln_g=v_sgu_ln_g, v_sgu_ln_b=v_sgu_ln_b, v_sgu_w=v_sgu_w, v_sgu_b=v_sgu_b, v_conv_w=v_conv_w, v_w_br_attn=v_w_br_attn, v_w_br_pool=v_w_br_pool, v_w_br_sgu=v_w_br_sgu, v_w_br_conv=v_w_br_conv, v_w_gate=v_w_gate, v_b_gate=v_b_gate, v_w_o=v_w_o, v_ln1_g=v_ln1_g, v_ln1_b=v_ln1_b, v_w_ff_gate=v_w_ff_gate, v_w_ff_up=v_w_ff_up, v_w_ff_down=v_w_ff_down, v_ln2_g=v_ln2_g, v_ln2_b=v_ln2_b)
    weights = {n: given[n] for n in TWIN_WEIGHTS}
    shared = {n: given[n] for n in SHARED_INPUTS}
    per_example = {n: given[n] for n in ['x', 'c', 'ctx']}
    grad_fn = _jax.value_and_grad(_loss, argnums=(0, 1))

    def one_microbatch(ex, loss_target):
        ex = dict(ex)
        diff = ex.pop(TWIN_DIFF_INPUT)
        return grad_fn(weights, diff, {**shared, **ex}, loss_target)

    if N_MICROBATCH == 1:
        loss, (grad_w, grad_x) = one_microbatch(per_example, given["loss_target"])
    else:
        def body(carry, xs):
            loss_sum, grad_sum = carry
            l_k, (gw_k, gx_k) = one_microbatch(xs[0], xs[1])
            with _jax.named_scope("update"):
                return (loss_sum + l_k, _jax.tree.map(_jnp.add, grad_sum, gw_k)), gx_k

        init = (_jnp.zeros((), _jnp.float32), _jax.tree.map(_jnp.zeros_like, weights))
        (loss, grad_w), grad_x = _jax.lax.scan(body, init, (per_example, given["loss_target"]))
    with _jax.named_scope("update"):
        delta_w, new_m, new_v = {}, {}, {}
        for n in TWIN_WEIGHTS:
            delta_w[n], new_m[n], new_v[n] = _adamw(weights[n], grad_w[n], given["m_" + n], given["v_" + n])
    return (loss, grad_x, *[grad_w[n] for n in TWIN_WEIGHTS], *[delta_w[n] for n in TWIN_WEIGHTS],
            *[new_m[n] for n in TWIN_WEIGHTS], *[new_v[n] for n in TWIN_WEIGHTS])
```

```python
import functools
import math

import jax
import jax.numpy as jnp
from jax import lax
from jax.experimental import pallas as pl
from jax.experimental.pallas import tpu as pltpu

F32 = jnp.float32
BF16 = jnp.bfloat16

N_DEV = 8
MESH_AXES = ("x", "y", "c")
V7X_VMEM_LIMIT_BYTES = 56 * 1024 * 1024

GRID_W = 64
HEAD_DIM = 128
N_HEADS = 8
N_KV_HEADS = 2
KV_GROUP = N_HEADS // N_KV_HEADS
Q_W = N_HEADS * HEAD_DIM
KV_W = N_KV_HEADS * HEAD_DIM
ROPE_THETA = 10000.0
ROPE_AXIS_DIM = HEAD_DIM // 2
POOL_WINDOWS = (2, 4, 8, 16)
GC = 128
N_GROUPS = 4
BR_W = N_GROUPS * GC
SGU_CHUNK = 128
N_BRANCH = 4
LN_EPS = 1e-5
RMS_EPS = 1e-6
OFF_K = Q_W
OFF_V = OFF_K + KV_W
OFF_POOL = OFF_V + KV_W
OFF_U = OFF_POOL + BR_W
OFF_VG = OFF_U + BR_W
OFF_CB = OFF_VG + BR_W
OFF_CC = OFF_CB + BR_W
OFF_CX = OFF_CC + BR_W
IN_W = OFF_CX + BR_W
ATT_SCALE = HEAD_DIM ** -0.5

ADAM_LR = 0.001
ADAM_B1 = 0.9
ADAM_B2 = 0.999
ADAM_EPS = 1e-08
ADAM_WD = 0.01
ADAM_STEP = 10

_NT = (((1,), (1,)), ((), ()))
_NN = (((1,), (0,)), ((), ()))
_TN = (((0,), (0,)), ((), ()))
_DIMS = {"nt": _NT, "nn": _NN, "tn": _TN}


def _cparams(*sem):
    return pltpu.CompilerParams(dimension_semantics=sem, vmem_limit_bytes=V7X_VMEM_LIMIT_BYTES)


def _tile(dim, pref):
    best = None
    t = 128
    while t <= min(dim, pref):
        if dim % t == 0:
            best = t
        t += 128
    return best if best is not None else dim


def _dot(a, b, dims):
    return lax.dot_general(a.astype(BF16), b.astype(BF16), dims, preferred_element_type=F32)


def _mm(name, a, b, form, out_dtype, acc=None, tm=768, tn=512, tk=512):
    if form == "nt":
        (m, k), (n, k2) = a.shape, b.shape
    elif form == "nn":
        (m, k), (k2, n) = a.shape, b.shape
    else:
        (k, m), (k2, n) = a.shape, b.shape
    assert k == k2, (name, a.shape, b.shape)
    tm, tn, tk = _tile(m, tm), _tile(n, tn), _tile(k, tk)
    nk = k // tk
    a_spec = {"nt": pl.BlockSpec((tm, tk), lambda i, j, kk: (i, kk)),
              "nn": pl.BlockSpec((tm, tk), lambda i, j, kk: (i, kk)),
              "tn": pl.BlockSpec((tk, tm), lambda i, j, kk: (kk, i))}[form]
    b_spec = {"nt": pl.BlockSpec((tn, tk), lambda i, j, kk: (j, kk)),
              "nn": pl.BlockSpec((tk, tn), lambda i, j, kk: (kk, j)),
              "tn": pl.BlockSpec((tk, tn), lambda i, j, kk: (kk, j))}[form]
    o_spec = pl.BlockSpec((tm, tn), lambda i, j, kk: (i, j))
    dims = _DIMS[form]
    has_acc = acc is not None

    def body(*refs):
        if has_acc:
            a_ref, b_ref, c_ref, o_ref, acc_ref = refs
        else:
            a_ref, b_ref, o_ref, acc_ref = refs
        kk = pl.program_id(2)

        @pl.when(kk == 0)
        def _():
            acc_ref[...] = jnp.zeros_like(acc_ref)

        acc_ref[...] += _dot(a_ref[...], b_ref[...], dims)

        @pl.when(kk == nk - 1)
        def _():
            r = acc_ref[...]
            if has_acc:
                r = r + c_ref[...]
            o_ref[...] = r.astype(o_ref.dtype)

    in_specs = [a_spec, b_spec] + ([o_spec] if has_acc else [])
    args = (a, b) + ((acc,) if has_acc else ())
    return pl.pallas_call(
        body, name=name, grid=(m // tm, n // tn, nk),
        in_specs=in_specs, out_specs=o_spec,
        out_shape=jax.ShapeDtypeStruct((m, n), out_dtype),
        scratch_shapes=[pltpu.VMEM((tm, tn), F32)],
        compiler_params=_cparams("parallel", "parallel", "arbitrary"),
    )(*args)


def _rows(name, fn, n_rows, tm, nbc, row_ins, type_ins, row_outs, acc_outs):
    n_ri, n_ti, n_ro, n_ao = len(row_ins), len(type_ins), len(row_outs), len(acc_outs)

    def row_map(i, cb, roff):
        return (jnp.maximum(i - roff, 0), cb)

    def type_map(i):
        return (jnp.where(i >= nbc, 1, 0), 0, 0)

    in_specs, args = [], []
    for arr, cb, width, roff in row_ins:
        in_specs.append(pl.BlockSpec((tm, width), functools.partial(row_map, cb=cb, roff=roff)))
        args.append(arr)
    for arr in type_ins:
        in_specs.append(pl.BlockSpec((None, 1, arr.shape[-1]), type_map))
        args.append(arr)
    out_shape, out_specs = [], []
    for total, width, dtype, roff in row_outs:
        out_shape.append(jax.ShapeDtypeStruct((total, width), dtype))
        out_specs.append(pl.BlockSpec((tm, width), functools.partial(row_map, cb=0, roff=roff)))
    for width in acc_outs:
        out_shape.append(jax.ShapeDtypeStruct((2, 1, width), F32))
        out_specs.append(pl.BlockSpec((None, 1, width), type_map))
    n_in = n_ri + n_ti

    def body(*refs):
        i = pl.program_id(0)
        outs = fn(*[r[...] for r in refs[:n_in]])
        if not isinstance(outs, (tuple, list)):
            outs = (outs,)
        assert len(outs) == n_ro + n_ao, (name, len(outs))
        for r, o in zip(refs[n_in:n_in + n_ro], outs[:n_ro]):
            r[...] = o.astype(r.dtype)
        if n_ao:
            first = jnp.logical_or(i == 0, i == nbc)
            for r, o in zip(refs[n_in + n_ro:], outs[n_ro:]):
                o = jnp.broadcast_to(o.astype(F32), r.shape)

                @pl.when(first)
                def _(r=r, o=o):
                    r[...] = o

                @pl.when(jnp.logical_not(first))
                def _(r=r, o=o):
                    r[...] += o

    res = pl.pallas_call(
        body, name=name, grid=(n_rows // tm,),
        in_specs=in_specs, out_specs=out_specs, out_shape=out_shape,
        compiler_params=_cparams("arbitrary"),
    )(*args)
    return list(res)


def _vjp_fn(f, n_row, n_cot, keep=None):
    def g(*args):
        prim = args[:n_row] + args[n_row + n_cot:]
        cots = args[n_row:n_row + n_cot]
        out, vjp = jax.vjp(f, *prim)
        grads = vjp(tuple(cots) if isinstance(out, (tuple, list)) else cots[0])
        return grads if keep is None else tuple(grads[j] for j in keep)
    return g


def _typed(v):
    v = v.reshape(1, 1, -1)
    return jnp.concatenate([v, v], axis=0)


def _ln(x, g, b):
    mu = jnp.mean(x, axis=-1, keepdims=True)
    var = jnp.mean(jnp.square(x - mu), axis=-1, keepdims=True)
    return (x - mu) * lax.rsqrt(var + LN_EPS) * g + b


def _f_mod(x, sc, sh):
    return x * (1.0 + sc) + sh


def _make_f_ln(alpha, with_mod):
    def f(x, o, gate, lng, lnb, *mod):
        xn = _ln(alpha * x + gate * o, lng, lnb)
        if with_mod:
            sc, sh = mod
            return xn, xn * (1.0 + sc) + sh
        return xn
    return f


@jax.custom_vjp
def _rot(y):
    lane = lax.broadcasted_iota(jnp.int32, y.shape, 1)
    return jnp.where(lane % 64 < 32, pltpu.roll(y, 96, axis=1), pltpu.roll(y, 32, axis=1))


_rot.defvjp(lambda y: (_rot(y), None), lambda _, g: (_rot(g),))


def _f_prep(p, cos, sin, qg, kg):
    def head(xh, g):
        ms = jnp.mean(jnp.square(xh), axis=-1, keepdims=True)
        y = xh * lax.rsqrt(ms + RMS_EPS) * g
        return y * cos + _rot(y) * sin
    q = jnp.concatenate([head(p[:, h * HEAD_DIM:(h + 1) * HEAD_DIM], qg) for h in range(N_HEADS)], axis=1)
    k = jnp.concatenate([head(p[:, OFF_K + h * HEAD_DIM:OFF_K + (h + 1) * HEAD_DIM], kg)
                         for h in range(N_KV_HEADS)], axis=1)
    return q, k, p[:, OFF_V:OFF_POOL]


def _f_gate(g, t0, t1, t2, t3, b):
    d = t0.shape[-1]
    ts = (t0, t1, t2, t3)
    terms = [jax.nn.sigmoid(g[:, k * d:(k + 1) * d] + b[:, k * d:(k + 1) * d]) * ts[k] for k in range(N_BRANCH)]
    return terms[0] + terms[1] + terms[2] + terms[3]


def _f_swiglu(a, b):
    return jax.nn.silu(a) * b


def _softmax(s):
    e = jnp.exp(s - jnp.max(s, axis=-1, keepdims=True))
    return e / jnp.sum(e, axis=-1, keepdims=True)


def _attn_fwd(name, q, k, v, rc, ctx_queries, tq=256):
    r = q.shape[0]
    assert rc % tq == 0 and r % tq == 0
    nqc = rc // tq

    def body(q_ref, k_ref, v_ref, o_ref):
        qi = pl.program_id(1)

        def attend(nk):
            s = _dot(q_ref[...], k_ref[0:nk, :], _NT) * ATT_SCALE
            p = _softmax(s)
            o_ref[...] = _dot(p, v_ref[0:nk, :], _NN).astype(o_ref.dtype)

        @pl.when(qi < nqc)
        def _():
            if ctx_queries:
                attend(rc)
            else:
                o_ref[...] = jnp.zeros_like(o_ref)

        @pl.when(qi >= nqc)
        def _():
            attend(r)

    return pl.pallas_call(
        body, name=name, grid=(N_HEADS, r // tq),
        in_specs=[pl.BlockSpec((tq, HEAD_DIM), lambda h, i: (i, h)),
                  pl.BlockSpec((r, HEAD_DIM), lambda h, i: (0, h // KV_GROUP)),
                  pl.BlockSpec((r, HEAD_DIM), lambda h, i: (0, h // KV_GROUP))],
        out_specs=pl.BlockSpec((tq, HEAD_DIM), lambda h, i: (i, h)),
        out_shape=jax.ShapeDtypeStruct((r, Q_W), BF16),
        compiler_params=_cparams("parallel", "parallel"),
    )(q, k, v)


def _attn_bwd(name, q, k, v, do, rc, ctx_queries, tq=256):
    r = q.shape[0]
    nqc = rc // tq

    def body(q_ref, k_ref, v_ref, do_ref, dq_ref, dk_ref, dv_ref):
        g, qi = pl.program_id(1), pl.program_id(2)

        @pl.when(jnp.logical_and(g == 0, qi == 0))
        def _():
            dk_ref[...] = jnp.zeros_like(dk_ref)
            dv_ref[...] = jnp.zeros_like(dv_ref)

        def grad(nk):
            qb, kb, vb = q_ref[...], k_ref[0:nk, :], v_ref[0:nk, :]
            dob = do_ref[...].astype(BF16)
            p = _softmax(_dot(qb, kb, _NT) * ATT_SCALE)
            dv_ref[0:nk, :] += _dot(p, dob, _TN)
            dp = _dot(dob, vb, _NT)
            ds = p * (dp - jnp.sum(dp * p, axis=-1, keepdims=True)) * ATT_SCALE
            dq_ref[...] = _dot(ds, kb, _NN)
            dk_ref[0:nk, :] += _dot(ds, qb, _TN)

        @pl.when(qi < nqc)
        def _():
            if ctx_queries:
                grad(rc)
            else:
                dq_ref[...] = jnp.zeros_like(dq_ref)

        @pl.when(qi >= nqc)
        def _():
            grad(r)

    def qmap(kv, g, i):
        return (i, kv * KV_GROUP + g)

    def kvmap(kv, g, i):
        return (0, kv)

    return pl.pallas_call(
        body, name=name, grid=(N_KV_HEADS, KV_GROUP, r // tq),
        in_specs=[pl.BlockSpec((tq, HEAD_DIM), qmap), pl.BlockSpec((r, HEAD_DIM), kvmap),
                  pl.BlockSpec((r, HEAD_DIM), kvmap), pl.BlockSpec((tq, HEAD_DIM), qmap)],
        out_specs=[pl.BlockSpec((tq, HEAD_DIM), qmap), pl.BlockSpec((r, HEAD_DIM), kvmap),
                   pl.BlockSpec((r, HEAD_DIM), kvmap)],
        out_shape=[jax.ShapeDtypeStruct((r, Q_W), F32), jax.ShapeDtypeStruct((r, KV_W), F32),
                   jax.ShapeDtypeStruct((r, KV_W), F32)],
        compiler_params=_cparams("arbitrary", "arbitrary", "arbitrary"),
    )(q, k, v, do)


def _segments(shape, rc):
    t = lax.broadcasted_iota(jnp.int32, shape, 0)
    lo = jnp.where(t < rc, 0, rc)
    hi = jnp.where(t < rc, rc, shape[0])
    return t, lo, hi


def _shifted(x, o, t, lo, hi):
    n = x.shape[0]
    sh = pltpu.roll(x, (-o) % n, axis=0)
    return jnp.where(jnp.logical_and(t + o >= lo, t + o < hi), sh, 0.0)


def _winsum(x, left, right, t, lo, hi):
    acc = x
    for o in range(-left, right + 1):
        if o != 0:
            acc = acc + _shifted(x, o, t, lo, hi)
    return acc


def _pool_parts(z, g, t, lo, hi):
    w = POOL_WINDOWS[g]
    left = w // 2
    right = w - 1 - left
    count = (jnp.minimum(t + right + 1, hi) - jnp.maximum(t - left, lo)).astype(F32)
    return _winsum(z, left, right, t, lo, hi) / count - z, count, left, right


def _pool_fwd(name, p, pool_w, pool_scale, rc):
    r = p.shape[0]

    def body(z_ref, w_ref, s_ref, y_ref):
        t, lo, hi = _segments((r, GC), rc)
        for g in range(N_GROUPS):
            cols = slice(g * GC, (g + 1) * GC)
            d, _, _, _ = _pool_parts(z_ref[:, cols], g, t, lo, hi)
            y_ref[:, cols] = (_dot(d, w_ref[g], _NN) * s_ref[:, cols]).astype(y_ref.dtype)

    return pl.pallas_call(
        body, name=name, grid=(1,),
        in_specs=[pl.BlockSpec((r, BR_W), lambda i: (0, OFF_POOL // BR_W)),
                  pl.BlockSpec((N_GROUPS, GC, GC), lambda i: (0, 0, 0)),
                  pl.BlockSpec((1, BR_W), lambda i: (0, 0))],
        out_specs=pl.BlockSpec((r, BR_W), lambda i: (0, 0)),
        out_shape=jax.ShapeDtypeStruct((r, BR_W), BF16),
        compiler_params=_cparams("arbitrary"),
    )(p, pool_w, pool_scale.reshape(1, BR_W))


def _pool_bwd(name, p, pool_w, pool_scale, dy, rc):
    r = p.shape[0]

    def body(z_ref, w_ref, s_ref, dy_ref, dz_ref, dw_ref, ds_ref):
        t, lo, hi = _segments((r, GC), rc)
        for g in range(N_GROUPS):
            cols = slice(g * GC, (g + 1) * GC)
            d, count, left, right = _pool_parts(z_ref[:, cols], g, t, lo, hi)
            dyg = dy_ref[:, cols]
            ds_ref[:, cols] = jnp.sum(dyg * _dot(d, w_ref[g], _NN), axis=0, keepdims=True)
            dlin = dyg * s_ref[:, cols]
            dw_ref[g] = _dot(d, dlin, _TN)
            dd = _dot(dlin, w_ref[g], _NT)
            dz_ref[:, cols] = (_winsum(dd / count, right, left, t, lo, hi) - dd).astype(dz_ref.dtype)

    return pl.pallas_call(
        body, name=name, grid=(1,),
        in_specs=[pl.BlockSpec((r, BR_W), lambda i: (0, OFF_POOL // BR_W)),
                  pl.BlockSpec((N_GROUPS, GC, GC), lambda i: (0, 0, 0)),
                  pl.BlockSpec((1, BR_W), lambda i: (0, 0)),
                  pl.BlockSpec((r, BR_W), lambda i: (0, 0))],
        out_specs=[pl.BlockSpec((r, BR_W), lambda i: (0, 0)),
                   pl.BlockSpec((N_GROUPS, GC, GC), lambda i: (0, 0, 0)),
                   pl.BlockSpec((1, BR_W), lambda i: (0, 0))],
        out_shape=[jax.ShapeDtypeStruct((r, BR_W), BF16), jax.ShapeDtypeStruct((N_GROUPS, GC, GC), F32),
                   jax.ShapeDtypeStruct((1, BR_W), F32)],
        compiler_params=_cparams("arbitrary"),
    )(p, pool_w, pool_scale.reshape(1, BR_W), dy)


def _f_sgu_v(pvg, lng, lnb):
    return _ln(jax.nn.gelu(pvg), lng, lnb)


def _sgu_fwd(name, p, ln_g, ln_b, sgu_w, sgu_b):
    r = p.shape[0]

    def body(pu_ref, pv_ref, g_ref, b_ref, w_ref, sb_ref, y_ref):
        vn = _f_sgu_v(pv_ref[...], g_ref[...], b_ref[...])
        u = jax.nn.gelu(pu_ref[...])
        for g in range(N_GROUPS):
            cols = slice(g * GC, (g + 1) * GC)
            s = _dot(w_ref[g], vn[:, cols], _NN) + sb_ref[g]
            y_ref[:, cols] = (u[:, cols] * s).astype(y_ref.dtype)

    return pl.pallas_call(
        body, name=name, grid=(r // SGU_CHUNK,),
        in_specs=[pl.BlockSpec((SGU_CHUNK, BR_W), lambda i: (i, OFF_U // BR_W)),
                  pl.BlockSpec((SGU_CHUNK, BR_W), lambda i: (i, OFF_VG // BR_W)),
                  pl.BlockSpec((1, BR_W), lambda i: (0, 0)), pl.BlockSpec((1, BR_W), lambda i: (0, 0)),
                  pl.BlockSpec((N_GROUPS, GC, GC), lambda i: (0, 0, 0)),
                  pl.BlockSpec((N_GROUPS, SGU_CHUNK, 1), lambda i: (0, 0, 0))],
        out_specs=pl.BlockSpec((SGU_CHUNK, BR_W), lambda i: (i, 0)),
        out_shape=jax.ShapeDtypeStruct((r, BR_W), BF16),
        compiler_params=_cparams("parallel"),
    )(p, p, ln_g.reshape(1, BR_W), ln_b.reshape(1, BR_W), sgu_w, sgu_b.reshape(N_GROUPS, SGU_CHUNK, 1))


def _sgu_bwd(name, p, ln_g, ln_b, sgu_w, sgu_b, dy):
    r = p.shape[0]

    def body(pu_ref, pv_ref, g_ref, b_ref, w_ref, sb_ref, dy_ref, dp_ref, dg_ref, db_ref, dw_ref, dsb_ref):
        i = pl.program_id(0)

        @pl.when(i == 0)
        def _():
            for ref in (dg_ref, db_ref, dw_ref, dsb_ref):
                ref[...] = jnp.zeros_like(ref)

        vn, vjp_v = jax.vjp(_f_sgu_v, pv_ref[...], g_ref[...], b_ref[...])
        u, vjp_u = jax.vjp(jax.nn.gelu, pu_ref[...])
        dy = dy_ref[...]
        du, dvn = [], []
        for g in range(N_GROUPS):
            cols = slice(g * GC, (g + 1) * GC)
            s = _dot(w_ref[g], vn[:, cols], _NN) + sb_ref[g]
            du.append(dy[:, cols] * s)
            ds = dy[:, cols] * u[:, cols]
            dsb_ref[g] += jnp.sum(ds, axis=1, keepdims=True)
            dw_ref[g] += _dot(ds, vn[:, cols], _NT)
            dvn.append(_dot(w_ref[g], ds, _TN))
        (dpu,) = vjp_u(jnp.concatenate(du, axis=1))
        dpv, dg, db = vjp_v(jnp.concatenate(dvn, axis=1))
        dp_ref[:, 0:BR_W] = dpu.astype(dp_ref.dtype)
        dp_ref[:, BR_W:2 * BR_W] = dpv.astype(dp_ref.dtype)
        dg_ref[...] += dg
        db_ref[...] += db

    vec = pl.BlockSpec((1, BR_W), lambda i: (0, 0))
    wsp = pl.BlockSpec((N_GROUPS, GC, GC), lambda i: (0, 0, 0))
    bsp = pl.BlockSpec((N_GROUPS, SGU_CHUNK, 1), lambda i: (0, 0, 0))
    return pl.pallas_call(
        body, name=name, grid=(r // SGU_CHUNK,),
        in_specs=[pl.BlockSpec((SGU_CHUNK, BR_W), lambda i: (i, OFF_U // BR_W)),
                  pl.BlockSpec((SGU_CHUNK, BR_W), lambda i: (i, OFF_VG // BR_W)),
                  vec, vec, wsp, bsp, pl.BlockSpec((SGU_CHUNK, BR_W), lambda i: (i, 0))],
        out_specs=[pl.BlockSpec((SGU_CHUNK, 2 * BR_W), lambda i: (i, 0)), vec, vec, wsp, bsp],
        out_shape=[jax.ShapeDtypeStruct((r, 2 * BR_W), BF16), jax.ShapeDtypeStruct((1, BR_W), F32),
                   jax.ShapeDtypeStruct((1, BR_W), F32), jax.ShapeDtypeStruct((N_GROUPS, GC, GC), F32),
                   jax.ShapeDtypeStruct((N_GROUPS, SGU_CHUNK, 1), F32)],
        compiler_params=_cparams("arbitrary"),
    )(p, p, ln_g.reshape(1, BR_W), ln_b.reshape(1, BR_W), sgu_w, sgu_b.reshape(N_GROUPS, SGU_CHUNK, 1), dy)


def _conv_w8(conv_w):
    return jnp.concatenate([conv_w, jnp.zeros((8 - conv_w.shape[0], conv_w.shape[1]), F32)], axis=0)


def _conv_fwd(name, p, conv_w, rc):
    r = p.shape[0]

    def body(cb_ref, cc_ref, cx_ref, w_ref, y_ref):
        t, lo, hi = _segments((r, GC), rc)
        z = cc_ref[...] * cx_ref[...]
        w = w_ref[...]
        c = _shifted(z, -1, t, lo, hi) * w[0:1] + z * w[1:2] + _shifted(z, 1, t, lo, hi) * w[2:3]
        y_ref[...] = (cb_ref[...] * c).astype(y_ref.dtype)

    nb = OFF_CB // GC
    return pl.pallas_call(
        body, name=name, grid=(N_GROUPS,),
        in_specs=[pl.BlockSpec((r, GC), lambda j: (0, nb + j)),
                  pl.BlockSpec((r, GC), lambda j: (0, nb + N_GROUPS + j)),
                  pl.BlockSpec((r, GC), lambda j: (0, nb + 2 * N_GROUPS + j)),
                  pl.BlockSpec((8, GC), lambda j: (0, j))],
        out_specs=pl.BlockSpec((r, GC), lambda j: (0, j)),
        out_shape=jax.ShapeDtypeStruct((r, BR_W), BF16),
        compiler_params=_cparams("parallel"),
    )(p, p, p, _conv_w8(conv_w))


def _conv_bwd(name, p, conv_w, dy, rc):
    r = p.shape[0]

    def body(cb_ref, cc_ref, cx_ref, w_ref, dy_ref, dcb_ref, dcc_ref, dcx_ref, dw_ref):
        t, lo, hi = _segments((r, GC), rc)
        cc, cx, w, dy = cc_ref[...], cx_ref[...], w_ref[...], dy_ref[...]
        z = cc * cx
        zp, zn = _shifted(z, -1, t, lo, hi), _shifted(z, 1, t, lo, hi)
        dcb_ref[...] = (dy * (zp * w[0:1] + z * w[1:2] + zn * w[2:3])).astype(dcb_ref.dtype)
        dc = dy * cb_ref[...]
        dw_ref[...] = jnp.concatenate(
            [jnp.sum(dc * zp, axis=0, keepdims=True), jnp.sum(dc * z, axis=0, keepdims=True),
             jnp.sum(dc * zn, axis=0, keepdims=True), jnp.zeros((5, GC), F32)], axis=0)
        dz = dc * w[1:2] + _shifted(dc, 1, t, lo, hi) * w[0:1] + _shifted(dc, -1, t, lo, hi) * w[2:3]
        dcc_ref[...] = (dz * cx).astype(dcc_ref.dtype)
        dcx_ref[...] = (dz * cc).astype(dcx_ref.dtype)

    nb = OFF_CB // GC
    return pl.pallas_call(
        body, name=name, grid=(N_GROUPS,),
        in_specs=[pl.BlockSpec((r, GC), lambda j: (0, nb + j)),
                  pl.BlockSpec((r, GC), lambda j: (0, nb + N_GROUPS + j)),
                  pl.BlockSpec((r, GC), lambda j: (0, nb + 2 * N_GROUPS + j)),
                  pl.BlockSpec((8, GC), lambda j: (0, j)),
                  pl.BlockSpec((r, GC), lambda j: (0, j))],
        out_specs=[pl.BlockSpec((r, GC), lambda j: (0, j))] * 3 + [pl.BlockSpec((8, GC), lambda j: (0, j))],
        out_shape=[jax.ShapeDtypeStruct((r, BR_W), BF16)] * 3 + [jax.ShapeDtypeStruct((8, BR_W), F32)],
        compiler_params=_cparams("parallel"),
    )(p, p, p, _conv_w8(conv_w), dy)


def _rope_tables(rc, n):
    rows = n // GRID_W
    row = jnp.repeat(jnp.arange(rows), GRID_W).astype(F32)
    col = jnp.tile(jnp.arange(GRID_W), rows).astype(F32)
    inv = ROPE_THETA ** (-jnp.arange(0, ROPE_AXIS_DIM, 2, dtype=F32) / ROPE_AXIS_DIM)
    ang_r, ang_c = row[:, None] * inv, col[:, None] * inv
    cos = jnp.concatenate([jnp.cos(ang_r), jnp.cos(ang_r), jnp.cos(ang_c), jnp.cos(ang_c)], axis=1)
    sin = jnp.concatenate([-jnp.sin(ang_r), jnp.sin(ang_r), -jnp.sin(ang_c), jnp.sin(ang_c)], axis=1)
    cos = jnp.concatenate([jnp.ones((rc, HEAD_DIM), F32), cos], axis=0)
    sin = jnp.concatenate([jnp.zeros((rc, HEAD_DIM), F32), sin], axis=0)
    return cos, sin


MOD_NAMES = ("sh1", "sc1", "g1", "sh2", "sc2", "g2")


def _local_step(xin, target, mod, wg, sp, rc, alpha):
    r, d = xin.shape
    n_layers = mod.shape[0]
    tm_n, tm_w = 256, 128
    nbc_n, nbc_w = rc // tm_n, rc // tm_w
    cos, sin = _rope_tables(rc, r - rc)
    mp = mod.reshape(n_layers, 2, 6, 1, d)
    mods = [{nm: mp[i, :, j] for j, nm in enumerate(MOD_NAMES)} for i in range(n_layers)]
    f_ln_mod, f_ln_last = _make_f_ln(alpha, True), _make_f_ln(alpha, False)

    def whole(arr, roff=0):
        return (arr, 0, arr.shape[1], roff)

    (hb,) = _rows("mod_in", _f_mod, r, tm_n, nbc_n, [whole(xin)], [mods[0]["sc1"], mods[0]["sh1"]],
                  [(r, d, BF16, 0)], [])
    saved = []
    x = xin
    for i in range(n_layers):
        last = i == n_layers - 1
        w, s, m = wg[i], sp[i], mods[i]
        sv = {"x": x, "hb": hb}
        p = _mm(f"l{i}_in", hb, w["in_t"], "nt", F32)
        gpre = _mm(f"l{i}_gate", hb, w["gate_t"], "nt", F32)
        q, k, v = _rows(f"l{i}_prep", _f_prep, r, tm_n, nbc_n,
                        [(p, 0, OFF_POOL, 0), whole(cos), whole(sin)], [_typed(s["q_norm_g"]), _typed(s["k_norm_g"])],
                        [(r, Q_W, BF16, 0), (r, KV_W, BF16, 0), (r, KV_W, BF16, 0)], [])
        ys = [_attn_fwd(f"l{i}_attn", q, k, v, rc, not last),
              _pool_fwd(f"l{i}_pool", p, s["pool_w"], s["pool_scale"], rc),
              _sgu_fwd(f"l{i}_sgu", p, s["sgu_ln_g"], s["sgu_ln_b"], s["sgu_w"], s["sgu_b"]),
              _conv_fwd(f"l{i}_conv", p, s["conv_w"], rc)]
        ts = [_mm(f"l{i}_br{kk}", ys[kk], w["br_t"][kk], "nt", F32) for kk in range(N_BRANCH)]
        (mg,) = _rows(f"l{i}_merge", _f_gate, r, tm_w, nbc_w, [whole(gpre)] + [whole(t) for t in ts],
                      [_typed(s["b_gate"])], [(r, d, BF16, 0)], [])
        o = _mm(f"l{i}_o", mg, w["o"], "nn", F32)
        x1, h2b = _rows(f"l{i}_ln1", f_ln_mod, r, tm_n, nbc_n, [whole(x), whole(o)],
                        [m["g1"], _typed(s["ln1_g"]), _typed(s["ln1_b"]), m["sc2"], m["sh2"]],
                        [(r, d, F32, 0), (r, d, BF16, 0)], [])
        af = _mm(f"l{i}_ffg", h2b, w["ffg_t"], "nt", F32)
        bf = _mm(f"l{i}_ffu", h2b, w["ffu_t"], "nt", F32)
        (f,) = _rows(f"l{i}_swiglu", _f_swiglu, r, tm_w, nbc_w, [whole(af), whole(bf)], [],
                     [(r, af.shape[1], BF16, 0)], [])
        o2 = _mm(f"l{i}_ffd", f, w["ffd"], "nn", F32)
        if last:
            (x2,) = _rows(f"l{i}_ln2", f_ln_last, r, tm_n, nbc_n, [whole(x1), whole(o2)],
                          [m["g2"], _typed(s["ln2_g"]), _typed(s["ln2_b"])], [(r, d, F32, 0)], [])
            hb = None
        else:
            nx = mods[i + 1]
            x2, hb = _rows(f"l{i}_ln2", f_ln_mod, r, tm_n, nbc_n, [whole(x1), whole(o2)],
                           [m["g2"], _typed(s["ln2_g"]), _typed(s["ln2_b"]), nx["sc1"], nx["sh1"]],
                           [(r, d, F32, 0), (r, d, BF16, 0)], [])
        sv.update(p=p, gpre=gpre, q=q, k=k, v=v, ys=ys, ts=ts, mg=mg, o=o, x1=x1, h2b=h2b, af=af, bf=bf, f=f, o2=o2)
        saved.append(sv)
        x = x2

    lat = jnp.concatenate([jnp.zeros((1, 1, 128), F32), jnp.ones((1, 1, 128), F32)], axis=0)

    def f_loss(xb, tb, msk):
        diff = (xb - tb) * msk[:, 0:1]
        part = jnp.sum(jnp.mean(jnp.square(diff), axis=-1, keepdims=True), axis=0, keepdims=True)
        return diff * (1.0 / d), jnp.broadcast_to(part, (1, 128))

    dx_direct, loss_acc = _rows("loss", f_loss, r, tm_n, nbc_n, [whole(x), whole(target, nbc_n)], [lat],
                                [(r, d, F32, 0)], [128])
    loss = 0.5 * loss_acc[1, 0, 0]

    dmods = [dict() for _ in range(n_layers)]
    dwg = [dict() for _ in range(n_layers)]
    dsp = [dict() for _ in range(n_layers)]
    dh = None
    for i in reversed(range(n_layers)):
        last = i == n_layers - 1
        w, s, m, sv = wg[i], sp[i], mods[i], saved[i]
        dm, dw, ds = dmods[i], dwg[i], dsp[i]
        ln2 = [m["g2"], _typed(s["ln2_g"]), _typed(s["ln2_b"])]
        if last:
            res = _rows(f"l{i}_ln2_bwd", _vjp_fn(f_ln_last, 2, 1), r, tm_n, nbc_n,
                        [whole(sv["x1"]), whole(sv["o2"]), whole(dx_direct)], ln2,
                        [(r, d, F32, 0), (r, d, BF16, 0)], [d, d, d])
            dx1, do2, dm["g2"], dlg, dlb = res
        else:
            nx = mods[i + 1]
            res = _rows(f"l{i}_ln2_bwd", _vjp_fn(f_ln_mod, 2, 2), r, tm_n, nbc_n,
                        [whole(sv["x1"]), whole(sv["o2"]), whole(dx_direct), whole(dh)],
                        ln2 + [nx["sc1"], nx["sh1"]],
                        [(r, d, F32, 0), (r, d, BF16, 0)], [d, d, d, d, d])
            dx1, do2, dm["g2"], dlg, dlb, dmods[i + 1]["sc1"], dmods[i + 1]["sh1"] = res
        ds["ln2_g"], ds["ln2_b"] = dlg, dlb
        df = _mm(f"l{i}_dF", do2, w["ffd"], "nt", F32)
        dw["ffd"] = _mm(f"l{i}_dWffd", sv["f"], do2, "tn", BF16)
        dab, dbb = _rows(f"l{i}_swiglu_bwd", _vjp_fn(_f_swiglu, 2, 1), r, tm_w, nbc_w,
                         [whole(sv["af"]), whole(sv["bf"]), whole(df)], [],
                         [(r, df.shape[1], BF16, 0), (r, df.shape[1], BF16, 0)], [])
        dh2 = _mm(f"l{i}_dh2a", dab, w["ffg_t"], "nn", F32)
        dh2 = _mm(f"l{i}_dh2b", dbb, w["ffu_t"], "nn", F32, acc=dh2)
        dw["ffg_t"] = _mm(f"l{i}_dWffg", dab, sv["h2b"], "tn", BF16)
        dw["ffu_t"] = _mm(f"l{i}_dWffu", dbb, sv["h2b"], "tn", BF16)
        res = _rows(f"l{i}_ln1_bwd", _vjp_fn(f_ln_mod, 2, 2), r, tm_n, nbc_n,
                    [whole(sv["x"]), whole(sv["o"]), whole(dx1), whole(dh2)],
                    [m["g1"], _typed(s["ln1_g"]), _typed(s["ln1_b"]), m["sc2"], m["sh2"]],
                    [(r, d, F32, 0), (r, d, BF16, 0)], [d, d, d, d, d])
        dx_direct, do, dm["g1"], ds["ln1_g"], ds["ln1_b"], dm["sc2"], dm["sh2"] = res
        dmg = _mm(f"l{i}_dMg", do, w["o"], "nt", F32)
        dw["o"] = _mm(f"l{i}_dWo", sv["mg"], do, "tn", BF16)
        res = _rows(f"l{i}_merge_bwd", _vjp_fn(_f_gate, 5, 1), r, tm_w, nbc_w,
                    [whole(sv["gpre"])] + [whole(t) for t in sv["ts"]] + [whole(dmg)], [_typed(s["b_gate"])],
                    [(r, N_BRANCH * d, BF16, 0)] + [(r, d, BF16, 0)] * N_BRANCH, [N_BRANCH * d])
        dgb, dts, ds["b_gate"] = res[0], res[1:1 + N_BRANCH], res[1 + N_BRANCH]
        dys = [_mm(f"l{i}_dY{kk}", dts[kk], w["br_t"][kk], "nn", F32) for kk in range(N_BRANCH)]
        dw["br_t"] = [_mm(f"l{i}_dWbr{kk}", dts[kk], sv["ys"][kk], "tn", BF16) for kk in range(N_BRANCH)]
        dq, dk, dv = _attn_bwd(f"l{i}_attn_bwd", sv["q"], sv["k"], sv["v"], dys[0], rc, not last)
        res = _rows(f"l{i}_prep_bwd", _vjp_fn(_f_prep, 3, 3, keep=(0, 3, 4)), r, tm_n, nbc_n,
                    [(sv["p"], 0, OFF_POOL, 0), whole(cos), whole(sin), whole(dq), whole(dk), whole(dv)],
                    [_typed(s["q_norm_g"]), _typed(s["k_norm_g"])],
                    [(r, OFF_POOL, BF16, 0)], [HEAD_DIM, HEAD_DIM])
        dp_qkv, ds["q_norm_g"], ds["k_norm_g"] = res
        dp_pool, ds["pool_w"], ds["pool_scale"] = _pool_bwd(f"l{i}_pool_bwd", sv["p"], s["pool_w"], s["pool_scale"],
                                                            dys[1], rc)
        dp_sgu, ds["sgu_ln_g"], ds["sgu_ln_b"], ds["sgu_w"], ds["sgu_b"] = _sgu_bwd(
            f"l{i}_sgu_bwd", sv["p"], s["sgu_ln_g"], s["sgu_ln_b"], s["sgu_w"], s["sgu_b"], dys[2])
        dp_cb, dp_cc, dp_cx, dcw = _conv_bwd(f"l{i}_conv_bwd", sv["p"], s["conv_w"], dys[3], rc)
        ds["conv_w"] = dcw[0:3]
        dpb = jnp.concatenate([dp_qkv, dp_pool, dp_sgu, dp_cb, dp_cc, dp_cx], axis=1)
        dh = _mm(f"l{i}_dhb_a", dpb, w["in_t"], "nn", F32)
        dh = _mm(f"l{i}_dhb_b", dgb, w["gate_t"], "nn", F32, acc=dh)
        dw["in_t"] = _mm(f"l{i}_dWin", dpb, sv["hb"], "tn", BF16)
        dw["gate_t"] = _mm(f"l{i}_dWgate", dgb, sv["hb"], "tn", BF16)

    def f_mod_bwd(xb, ddir, dhb, sc, sh):
        _, vjp = jax.vjp(_f_mod, xb, sc, sh)
        dxb, dsc, dsh = vjp(dhb)
        return dxb + ddir, dsc, dsh

    grad_x, dmods[0]["sc1"], dmods[0]["sh1"] = _rows(
        "mod_in_bwd", f_mod_bwd, r, tm_n, nbc_n, [whole(xin), whole(dx_direct), whole(dh)],
        [mods[0]["sc1"], mods[0]["sh1"]], [(r - rc, d, F32, nbc_n)], [d, d])
    dmod = jnp.stack([jnp.concatenate([dmods[i][nm][:, 0, :] for nm in MOD_NAMES], axis=-1)
                      for i in range(n_layers)])
    for ds in dsp:
        for nm in ("ln1_g", "ln1_b", "ln2_g", "ln2_b", "b_gate", "q_norm_g", "k_norm_g"):
            ds[nm] = ds[nm][0, 0] + ds[nm][1, 0]
        ds["pool_scale"] = ds["pool_scale"].reshape(-1)
        ds["sgu_ln_g"] = ds["sgu_ln_g"].reshape(-1)
        ds["sgu_ln_b"] = ds["sgu_ln_b"].reshape(-1)
        ds["sgu_b"] = ds["sgu_b"].reshape(N_GROUPS, SGU_CHUNK)
    return loss, grad_x, dmod, dwg, dsp


def _exchange(name, srcs, scatter):
    n_items = len(srcs)
    out_shapes = [jax.ShapeDtypeStruct(s.shape if scatter else (N_DEV,) + s.shape, s.dtype) for s in srcs]

    def body(*refs):
        src, out = refs[:n_items], refs[n_items:2 * n_items]
        send_sems, recv_sems, loc_sems = refs[2 * n_items:]
        mx, my, mc = [lax.axis_index(a) for a in MESH_AXES]
        me = 4 * mx + 2 * my + mc
        peers = []
        for kk in range(1, N_DEV):
            px = 1 - mx if kk & 4 else mx
            py = 1 - my if kk & 2 else my
            pc = 1 - mc if kk & 1 else mc
            peers.append(((px, py, pc), 4 * px + 2 * py + pc))
        local, sent = [], []
        for a in range(n_items):
            loc = pltpu.make_async_copy(src[a].at[me] if scatter else src[a], out[a].at[me], loc_sems.at[a])
            loc.start()
            local.append(loc)
            for j, (peer, peer_l) in enumerate(peers):
                cp = pltpu.make_async_remote_copy(
                    src_ref=src[a].at[peer_l] if scatter else src[a], dst_ref=out[a].at[me],
                    send_sem=send_sems.at[a * (N_DEV - 1) + j], recv_sem=recv_sems.at[a * (N_DEV - 1) + j],
                    device_id=peer, device_id_type=pl.DeviceIdType.MESH)
                cp.start()
                sent.append(cp)
        for a in range(n_items):
            for j, (peer, peer_l) in enumerate(peers):
                pltpu.make_async_remote_copy(
                    src_ref=src[a].at[peer_l] if scatter else src[a], dst_ref=out[a].at[peer_l],
                    send_sem=send_sems.at[a * (N_DEV - 1) + j], recv_sem=recv_sems.at[a * (N_DEV - 1) + j],
                    device_id=peer, device_id_type=pl.DeviceIdType.MESH).wait_recv()
        for cp in sent:
            cp.wait_send()
        for loc in local:
            loc.wait()

    any_spec = pl.BlockSpec(memory_space=pl.ANY)
    res = pl.pallas_call(
        body, name=name,
        in_specs=[any_spec] * n_items, out_specs=[any_spec] * n_items, out_shape=out_shapes,
        scratch_shapes=[pltpu.SemaphoreType.DMA((n_items * (N_DEV - 1),)),
                        pltpu.SemaphoreType.DMA((n_items * (N_DEV - 1),)),
                        pltpu.SemaphoreType.DMA((n_items,))],
    )(*srcs)
    return list(res)


def _row_tile(n, pref, mult):
    best = None
    t = mult
    while t <= min(n, pref):
        if n % t == 0:
            best = t
        t += mult
    return best if best is not None else n


def _sum8(name, slabs):
    _, n, k = slabs.shape
    tr = _row_tile(n, 128, 16)

    def body(s_ref, o_ref):
        acc = s_ref[0].astype(F32)
        for j in range(1, N_DEV):
            acc = acc + s_ref[j].astype(F32)
        o_ref[...] = acc

    return pl.pallas_call(
        body, name=name, grid=(n // tr,),
        in_specs=[pl.BlockSpec((N_DEV, tr, k), lambda i: (0, i, 0))],
        out_specs=pl.BlockSpec((tr, k), lambda i: (i, 0)),
        out_shape=jax.ShapeDtypeStruct((n, k), F32),
        compiler_params=_cparams("parallel"),
    )(slabs)


def _adamw(name, w, g, m, v):
    n, k = w.shape
    tr = _row_tile(n, 256, 8)

    def body(w_ref, g_ref, m_ref, v_ref, d_ref, m2_ref, v2_ref):
        gv = g_ref[...]
        m2 = ADAM_B1 * m_ref[...] + (1.0 - ADAM_B1) * gv
        v2 = ADAM_B2 * v_ref[...] + (1.0 - ADAM_B2) * jnp.square(gv)
        m_hat = m2 / (1.0 - ADAM_B1 ** ADAM_STEP)
        v_hat = v2 / (1.0 - ADAM_B2 ** ADAM_STEP)
        d_ref[...] = -ADAM_LR * (m_hat / (jnp.sqrt(v_hat) + ADAM_EPS) + ADAM_WD * w_ref[...])
        m2_ref[...] = m2
        v2_ref[...] = v2

    spec = pl.BlockSpec((tr, k), lambda i: (i, 0))
    return pl.pallas_call(
        body, name=name, grid=(n // tr,),
        in_specs=[spec] * 4, out_specs=[spec] * 3,
        out_shape=[jax.ShapeDtypeStruct((n, k), F32)] * 3,
        compiler_params=_cparams("parallel"),
    )(w, g, m, v)


def _pack(arrs):
    flat = jnp.concatenate([a.reshape(-1).astype(F32) for a in arrs])
    pad = (-flat.shape[0]) % 2048
    if pad:
        flat = jnp.concatenate([flat, jnp.zeros((pad,), F32)])
    return flat.reshape(-1, 128)


def _unpack(packed, shapes):
    flat = packed.reshape(-1)
    out, off = [], 0
    for shp in shapes:
        size = math.prod(shp)
        out.append(flat[off:off + size].reshape(shp))
        off += size
    return out


WEIGHT_NAMES = ("c_ctx", "w_ada", "b_ada", "w_in", "q_norm_g", "k_norm_g", "pool_w", "pool_scale", "sgu_ln_g",
                "sgu_ln_b", "sgu_w", "sgu_b", "conv_w", "w_br_attn", "w_br_pool", "w_br_sgu", "w_br_conv", "w_gate",
                "b_gate", "w_o", "ln1_g", "ln1_b", "w_ff_gate", "w_ff_up", "w_ff_down", "ln2_g", "ln2_b")
COL_SHARDED = {"w_in": "in_t", "w_gate": "gate_t", "w_ff_gate": "ffg_t", "w_ff_up": "ffu_t",
               "w_br_attn": ("br_t", 0), "w_br_pool": ("br_t", 1), "w_br_sgu": ("br_t", 2), "w_br_conv": ("br_t", 3)}
ROW_SHARDED = {"w_o": "o", "w_ff_down": "ffd"}
LAYER_SMALL = ("q_norm_g", "k_norm_g", "pool_w", "pool_scale", "sgu_ln_g", "sgu_ln_b", "sgu_w", "sgu_b", "b_gate",
               "ln1_g", "ln1_b", "ln2_g", "ln2_b")
SMALL_ORDER = ("c_ctx", "b_ada") + LAYER_SMALL + ("conv_w",)


def _wg_get(tree, key):
    return tree[key[0]][key[1]] if isinstance(key, tuple) else tree[key]


def _train_step(a):
    n_layers, d = a["w_in"].shape[0], a["x"].shape[-1]
    rc = a["ctx"].shape[1]
    alpha = (2 * n_layers) ** 0.25
    mx, my, mc = [lax.axis_index(ax) for ax in MESH_AXES]
    me = 4 * mx + 2 * my + mc
    ada_w = a["w_ada"].shape[-1]
    cw_loc = a["conv_w"].shape[-1]

    n_c, n_cw = d, n_layers * 3 * cw_loc
    got = _exchange("gather_cond", [_pack([a["c"], a["conv_w"]])], False)[0].reshape(N_DEV, -1)
    c_all = got[:, :n_c]
    conv_w = got[:, n_c:n_c + n_cw].reshape(N_DEV, n_layers, 3, cw_loc).transpose(1, 2, 0, 3).reshape(n_layers, 3, -1)
    cond = jnp.concatenate([c_all, a["c_ctx"][None], jnp.zeros((16 - N_DEV - 1, d), F32)], axis=0)
    sil, sil_vjp = jax.vjp(jax.nn.silu, cond)
    sil = sil.astype(BF16)

    mod_cols = jnp.concatenate([_mm(f"ada{i}", sil, a["w_ada"][i], "nn", F32) for i in range(n_layers)], axis=0)
    got = _exchange("gather_mod", [mod_cols], False)[0]
    mod_all = got.reshape(N_DEV, n_layers, 16, ada_w).transpose(1, 2, 0, 3).reshape(n_layers, 16, -1)
    mod_all = mod_all + a["b_ada"][:, None, :]
    mod = jnp.stack([mod_all[:, N_DEV], lax.dynamic_index_in_dim(mod_all, me, axis=1, keepdims=False)], axis=1)

    wg = []
    for i in range(n_layers):
        items, keys = [], []
        for nm, key in COL_SHARDED.items():
            items.append(a[nm][i].T.astype(BF16))
            keys.append(key)
        for nm, key in ROW_SHARDED.items():
            items.append(a[nm][i].astype(BF16))
            keys.append(key)
        got = _exchange(f"gather_w{i}", items, False)
        tree = {"br_t": [None] * N_BRANCH}
        for key, g in zip(keys, got):
            g = g.reshape(-1, g.shape[-1])
            if isinstance(key, tuple):
                tree[key[0]][key[1]] = g
            else:
                tree[key] = g
        wg.append(tree)
    sp = [{nm: a[nm][i] for nm in LAYER_SMALL} for i in range(n_layers)]
    for i in range(n_layers):
        sp[i]["conv_w"] = conv_w[i]

    xin = jnp.concatenate([a["ctx"][0], a["x"][0]], axis=0)
    loss_l, grad_x, dmod, dwg, dsp = _local_step(xin, a["loss_target"][0], mod, wg, sp, rc, alpha)
    loss = lax.psum(loss_l, MESH_AXES)

    grads = {}
    big = {nm: [] for nm in list(COL_SHARDED) + list(ROW_SHARDED)}
    for i in reversed(range(n_layers)):
        nms = list(COL_SHARDED) + list(ROW_SHARDED)
        keys = [COL_SHARDED.get(nm, ROW_SHARDED.get(nm)) for nm in nms]
        items = [_wg_get(dwg[i], key) for key in keys]
        items = [it.reshape(N_DEV, it.shape[0] // N_DEV, it.shape[1]) for it in items]
        got = _exchange(f"scatter_g{i}", items, True)
        for nm, slabs in zip(nms, got):
            tot = _sum8(f"sum_{nm}{i}", slabs)
            big[nm].insert(0, tot.T if nm in COL_SHARDED else tot)
    for nm, per_layer in big.items():
        grads[nm] = jnp.stack(per_layer)

    small_parts = [dmod[:, 0], dmod[:, 1]]
    small_shapes = [dmod[:, 0].shape, dmod[:, 1].shape]
    for nm in LAYER_SMALL + ("conv_w",):
        part = jnp.stack([dsp[i][nm] for i in range(n_layers)])
        small_parts.append(part)
        small_shapes.append(part.shape)
    got_lat, got_small = _exchange("gather_small", [_pack([dmod[:, 1]]), _pack(small_parts)], False)
    tot = _unpack(_sum8("sum_small", got_small), small_shapes)
    dmod_c, dmod_lat_sum = tot[0], tot[1]
    for nm, g in zip(LAYER_SMALL + ("conv_w",), tot[2:]):
        grads[nm] = g
    grads["conv_w"] = lax.dynamic_slice_in_dim(grads["conv_w"], me * cw_loc, cw_loc, axis=2)
    grads["b_ada"] = dmod_c + dmod_lat_sum
    dmod_lat_all = got_lat.reshape(N_DEV, -1)[:, :n_layers * 6 * d].reshape(N_DEV, n_layers, 6 * d)
    dm_rows = jnp.concatenate([dmod_lat_all.transpose(1, 0, 2), dmod_c[:, None, :],
                               jnp.zeros((n_layers, 16 - N_DEV - 1, 6 * d), F32)], axis=1)
    dm_cols = lax.dynamic_slice_in_dim(dm_rows, me * ada_w, ada_w, axis=2).astype(BF16)
    grads["w_ada"] = jnp.stack([_mm(f"dWada{i}", sil, dm_cols[i], "tn", F32) for i in range(n_layers)])
    dsil = None
    for i in range(n_layers):
        dsil = _mm(f"dsil{i}", dm_cols[i], a["w_ada"][i], "nt", F32, acc=dsil)
    got = _exchange("gather_dsil", [dsil], False)[0]
    dsil = _sum8("sum_dsil", got)
    grads["c_ctx"] = sil_vjp(dsil)[0][N_DEV]

    delta, new_m, new_v = {}, {}, {}
    for nm in WEIGHT_NAMES:
        if nm in SMALL_ORDER:
            continue
        shp = a[nm].shape
        view = (-1, shp[-1])
        res = _adamw(f"adamw_{nm}", a[nm].reshape(view), grads[nm].reshape(view), a["m_" + nm].reshape(view),
                     a["v_" + nm].reshape(view))
        delta[nm], new_m[nm], new_v[nm] = [t.reshape(shp) for t in res]
    shapes = [a[nm].shape for nm in SMALL_ORDER]
    res = _adamw("adamw_small", _pack([a[nm] for nm in SMALL_ORDER]), _pack([grads[nm] for nm in SMALL_ORDER]),
                 _pack([a["m_" + nm] for nm in SMALL_ORDER]), _pack([a["v_" + nm] for nm in SMALL_ORDER]))
    for tree, packed in zip((delta, new_m, new_v), res):
        for nm, t in zip(SMALL_ORDER, _unpack(packed, shapes)):
            tree[nm] = t
    return (loss, grad_x[None], *[grads[nm] for nm in WEIGHT_NAMES], *[delta[nm] for nm in WEIGHT_NAMES],
            *[new_m[nm] for nm in WEIGHT_NAMES], *[new_v[nm] for nm in WEIGHT_NAMES])


def kernel(x, c, ctx, c_ctx, w_ada, b_ada, w_in, q_norm_g, k_norm_g, pool_w, pool_scale, sgu_ln_g, sgu_ln_b, sgu_w, sgu_b, conv_w, w_br_attn, w_br_pool, w_br_sgu, w_br_conv, w_gate, b_gate, w_o, ln1_g, ln1_b, w_ff_gate, w_ff_up, w_ff_down, ln2_g, ln2_b, loss_target, m_c_ctx, m_w_ada, m_b_ada, m_w_in, m_q_norm_g, m_k_norm_g, m_pool_w, m_pool_scale, m_sgu_ln_g, m_sgu_ln_b, m_sgu_w, m_sgu_b, m_conv_w, m_w_br_attn, m_w_br_pool, m_w_br_sgu, m_w_br_conv, m_w_gate, m_b_gate, m_w_o, m_ln1_g, m_ln1_b, m_w_ff_gate, m_w_ff_up, m_w_ff_down, m_ln2_g, m_ln2_b, v_c_ctx, v_w_ada, v_b_ada, v_w_in, v_q_norm_g, v_k_norm_g, v_pool_w, v_pool_scale, v_sgu_ln_g, v_sgu_ln_b, v_sgu_w, v_sgu_b, v_conv_w, v_w_br_attn, v_w_br_pool, v_w_br_sgu, v_w_br_conv, v_w_gate, v_b_gate, v_w_o, v_ln1_g, v_ln1_b, v_w_ff_gate, v_w_ff_up, v_w_ff_down, v_ln2_g, v_ln2_b):
    names = list(WEIGHT_NAMES)
    args = dict(zip(
        ["x", "c", "ctx"] + names + ["loss_target"] + ["m_" + n for n in names] + ["v_" + n for n in names],
        (x, c, ctx, c_ctx, w_ada, b_ada, w_in, q_norm_g, k_norm_g, pool_w, pool_scale, sgu_ln_g, sgu_ln_b, sgu_w, sgu_b, conv_w, w_br_attn, w_br_pool, w_br_sgu, w_br_conv, w_gate, b_gate, w_o, ln1_g, ln1_b, w_ff_gate, w_ff_up, w_ff_down, ln2_g, ln2_b, loss_target, m_c_ctx, m_w_ada, m_b_ada, m_w_in, m_q_norm_g, m_k_norm_g, m_pool_w, m_pool_scale, m_sgu_ln_g, m_sgu_ln_b, m_sgu_w, m_sgu_b, m_conv_w, m_w_br_attn, m_w_br_pool, m_w_br_sgu, m_w_br_conv, m_w_gate, m_b_gate, m_w_o, m_ln1_g, m_ln1_b, m_w_ff_gate, m_w_ff_up, m_w_ff_down, m_ln2_g, m_ln2_b, v_c_ctx, v_w_ada, v_b_ada, v_w_in, v_q_norm_g, v_k_norm_g, v_pool_w, v_pool_scale, v_sgu_ln_g, v_sgu_ln_b, v_sgu_w, v_sgu_b, v_conv_w, v_w_br_attn, v_w_br_pool, v_w_br_sgu, v_w_br_conv, v_w_gate, v_b_gate, v_w_o, v_ln1_g, v_ln1_b, v_w_ff_gate, v_w_ff_up, v_w_ff_down, v_ln2_g, v_ln2_b)))
    return _train_step(args)
```

```python
import functools
import math

import jax
import jax.numpy as jnp
from jax import lax
from jax.experimental import pallas as pl
from jax.experimental.pallas import tpu as pltpu

F32 = jnp.float32
BF16 = jnp.bfloat16

N_DEV = 8
MESH_AXES = ("x", "y", "c")
V7X_VMEM_LIMIT_BYTES = 56 * 1024 * 1024

GRID_W = 64
HEAD_DIM = 128
N_HEADS = 8
N_KV_HEADS = 2
KV_GROUP = N_HEADS // N_KV_HEADS
Q_W = N_HEADS * HEAD_DIM
KV_W = N_KV_HEADS * HEAD_DIM
ROPE_THETA = 10000.0
ROPE_AXIS_DIM = HEAD_DIM // 2
POOL_WINDOWS = (2, 4, 8, 16)
GC = 128
N_GROUPS = 4
BR_W = N_GROUPS * GC
SGU_CHUNK = 128
N_BRANCH = 4
LN_EPS = 1e-5
RMS_EPS = 1e-6
OFF_K = Q_W
OFF_V = OFF_K + KV_W
OFF_POOL = OFF_V + KV_W
OFF_U = OFF_POOL + BR_W
OFF_VG = OFF_U + BR_W
OFF_CB = OFF_VG + BR_W
OFF_CC = OFF_CB + BR_W
OFF_CX = OFF_CC + BR_W
IN_W = OFF_CX + BR_W
ATT_SCALE = HEAD_DIM ** -0.5

ADAM_LR = 0.001
ADAM_B1 = 0.9
ADAM_B2 = 0.999
ADAM_EPS = 1e-08
ADAM_WD = 0.01
ADAM_STEP = 10

_NT = (((1,), (1,)), ((), ()))
_NN = (((1,), (0,)), ((), ()))
_TN = (((0,), (0,)), ((), ()))
_DIMS = {"nt": _NT, "nn": _NN, "tn": _TN}


def _cparams(*sem):
    return pltpu.CompilerParams(dimension_semantics=sem, vmem_limit_bytes=V7X_VMEM_LIMIT_BYTES)


def _tile(dim, pref):
    best = None
    t = 128
    while t <= min(dim, pref):
        if dim % t == 0:
            best = t
        t += 128
    return best if best is not None else dim


def _dot(a, b, dims):
    return lax.dot_general(a.astype(BF16), b.astype(BF16), dims, preferred_element_type=F32)


class _Rider:
    def __init__(self, inputs, out_shapes, scratch, start, finish):
        self.inputs, self.out_shapes, self.scratch = list(inputs), list(out_shapes), list(scratch)
        self.start, self.finish = start, finish


def _pcall(name, body, grid, in_specs, out_specs, out_shape, args, sem, scratch=(), rider=None):
    in_specs, out_specs, out_shape, scratch = list(in_specs), list(out_specs), list(out_shape), list(scratch)
    if rider is None:
        res = pl.pallas_call(body, name=name, grid=grid, in_specs=in_specs, out_specs=out_specs, out_shape=out_shape,
                             scratch_shapes=scratch, compiler_params=_cparams(*sem))(*args)
        return list(res), []
    n_in, n_out, n_scr = len(in_specs), len(out_specs), len(scratch)
    r_in, r_out = len(rider.inputs), len(rider.out_shapes)

    def wrapped(*refs):
        ins, refs = refs[:n_in], refs[n_in:]
        r_ins, refs = refs[:r_in], refs[r_in:]
        outs, refs = refs[:n_out], refs[n_out:]
        r_outs, refs = refs[:r_out], refs[r_out:]
        scr, r_scr = refs[:n_scr], refs[n_scr:]
        first = functools.reduce(jnp.logical_and, [pl.program_id(ax) == 0 for ax in range(len(grid))])
        last = functools.reduce(jnp.logical_and, [pl.program_id(ax) == grid[ax] - 1 for ax in range(len(grid))])

        @pl.when(first)
        def _():
            rider.start(r_ins, r_outs, r_scr)

        body(*ins, *outs, *scr)

        @pl.when(last)
        def _():
            rider.finish(r_ins, r_outs, r_scr)

    any_spec = pl.BlockSpec(memory_space=pl.ANY)
    res = pl.pallas_call(
        wrapped, name=name, grid=grid,
        in_specs=in_specs + [any_spec] * r_in, out_specs=out_specs + [any_spec] * r_out,
        out_shape=out_shape + rider.out_shapes, scratch_shapes=scratch + rider.scratch,
        compiler_params=_cparams(*(["arbitrary"] * len(grid))),
    )(*args, *rider.inputs)
    return list(res[:n_out]), list(res[n_out:])


V7X_MM_VMEM_BUDGET = 40 * 1024 * 1024


def _mm_plan(form, m, n, k, a_size, b_size, o_size, has_acc):
    tk = k if k <= 2816 else _tile(k, 2816)
    nk = k // tk
    tn = n if (form == "tn" and n <= 2048) else _tile(n, 512)
    for tm in sorted({m} | {t for t in range(128, m, 128) if m % t == 0}, reverse=True):
        need = 2 * (tm * tk * a_size + tn * tk * b_size + tm * tn * o_size) + tm * tn * 4 * (2 if nk > 1 else 1)
        need += 2 * tm * tn * 4 if has_acc else 0
        if need <= V7X_MM_VMEM_BUDGET:
            return tm, tn, tk
    return _tile(m, 128), tn, tk


def _mm(name, a, b, form, out_dtype, acc=None, rider=None):
    if form == "nt":
        (m, k), (n, k2) = a.shape, b.shape
    elif form == "nn":
        (m, k), (k2, n) = a.shape, b.shape
    else:
        (k, m), (k2, n) = a.shape, b.shape
    assert k == k2, (name, a.shape, b.shape)
    has_acc = acc is not None
    tm, tn, tk = _mm_plan(form, m, n, k, a.dtype.itemsize, b.dtype.itemsize, jnp.dtype(out_dtype).itemsize, has_acc)
    nk = k // tk
    a_spec = {"nt": pl.BlockSpec((tm, tk), lambda i, j, kk: (i, kk)),
              "nn": pl.BlockSpec((tm, tk), lambda i, j, kk: (i, kk)),
              "tn": pl.BlockSpec((tk, tm), lambda i, j, kk: (kk, i))}[form]
    b_spec = {"nt": pl.BlockSpec((tn, tk), lambda i, j, kk: (j, kk)),
              "nn": pl.BlockSpec((tk, tn), lambda i, j, kk: (kk, j)),
              "tn": pl.BlockSpec((tk, tn), lambda i, j, kk: (kk, j))}[form]
    o_spec = pl.BlockSpec((tm, tn), lambda i, j, kk: (i, j))
    dims = _DIMS[form]

    def body(*refs):
        a_ref, b_ref = refs[0], refs[1]
        c_ref = refs[2] if has_acc else None
        o_ref = refs[3] if has_acc else refs[2]

        def finish(r):
            if has_acc:
                r = r + c_ref[...]
            o_ref[...] = r.astype(o_ref.dtype)

        if nk == 1:
            finish(_dot(a_ref[...], b_ref[...], dims))
            return
        acc_ref = refs[-1]
        kk = pl.program_id(2)

        @pl.when(kk == 0)
        def _():
            acc_ref[...] = _dot(a_ref[...], b_ref[...], dims)

        @pl.when(kk > 0)
        def _():
            acc_ref[...] += _dot(a_ref[...], b_ref[...], dims)

        @pl.when(kk == nk - 1)
        def _():
            finish(acc_ref[...])

    in_specs = [a_spec, b_spec] + ([o_spec] if has_acc else [])
    args = (a, b) + ((acc,) if has_acc else ())
    outs, r_outs = _pcall(name, body, (m // tm, n // tn, nk), in_specs, [o_spec],
                          [jax.ShapeDtypeStruct((m, n), out_dtype)], args, ("parallel", "parallel", "arbitrary"),
                          scratch=[pltpu.VMEM((tm, tn), F32)] if nk > 1 else [], rider=rider)
    return outs[0] if rider is None else (outs[0], r_outs)


def _rows(name, fn, n_rows, tm, nbc, row_ins, type_ins, row_outs, acc_outs, rider=None):
    n_ri, n_ti, n_ro, n_ao = len(row_ins), len(type_ins), len(row_outs), len(acc_outs)

    def row_map(i, cb, roff):
        return (jnp.maximum(i - roff, 0), cb)

    def type_map(i):
        return (jnp.where(i >= nbc, 1, 0), 0, 0)

    in_specs, args = [], []
    for arr, cb, width, roff in row_ins:
        in_specs.append(pl.BlockSpec((tm, width), functools.partial(row_map, cb=cb, roff=roff)))
        args.append(arr)
    for arr in type_ins:
        in_specs.append(pl.BlockSpec((None, 1, arr.shape[-1]), type_map))
        args.append(arr)
    out_shape, out_specs = [], []
    for total, width, dtype, roff in row_outs:
        out_shape.append(jax.ShapeDtypeStruct((total, width), dtype))
        out_specs.append(pl.BlockSpec((tm, width), functools.partial(row_map, cb=0, roff=roff)))
    for width in acc_outs:
        out_shape.append(jax.ShapeDtypeStruct((2, 1, width), F32))
        out_specs.append(pl.BlockSpec((None, 1, width), type_map))
    n_in = n_ri + n_ti

    def body(*refs):
        i = pl.program_id(0)
        outs = fn(*[r[...] for r in refs[:n_in]])
        if not isinstance(outs, (tuple, list)):
            outs = (outs,)
        assert len(outs) == n_ro + n_ao, (name, len(outs))
        for r, o in zip(refs[n_in:n_in + n_ro], outs[:n_ro]):
            r[...] = o.astype(r.dtype)
        if n_ao:
            first = jnp.logical_or(i == 0, i == nbc)
            for r, o in zip(refs[n_in + n_ro:], outs[n_ro:]):
                o = jnp.broadcast_to(o.astype(F32), r.shape)

                @pl.when(first)
                def _(r=r, o=o):
                    r[...] = o

                @pl.when(jnp.logical_not(first))
                def _(r=r, o=o):
                    r[...] += o

    outs, r_outs = _pcall(name, body, (n_rows // tm,), in_specs, out_specs, out_shape, args, ("arbitrary",),
                          rider=rider)
    return outs if rider is None else (outs, r_outs)


def _vjp_fn(f, n_row, n_cot, keep=None):
    def g(*args):
        prim = args[:n_row] + args[n_row + n_cot:]
        cots = args[n_row:n_row + n_cot]
        out, vjp = jax.vjp(f, *prim)
        grads = vjp(tuple(cots) if isinstance(out, (tuple, list)) else cots[0])
        return grads if keep is None else tuple(grads[j] for j in keep)
    return g


def _typed(v):
    v = v.reshape(1, 1, -1)
    return jnp.concatenate([v, v], axis=0)


def _ln(x, g, b):
    mu = jnp.mean(x, axis=-1, keepdims=True)
    var = jnp.mean(jnp.square(x - mu), axis=-1, keepdims=True)
    return (x - mu) * lax.rsqrt(var + LN_EPS) * g + b


def _f_mod(x, sc, sh):
    return x * (1.0 + sc) + sh


def _make_f_ln(alpha, with_mod):
    def f(x, o, gate, lng, lnb, *mod):
        xn = _ln(alpha * x + gate * o, lng, lnb)
        if with_mod:
            sc, sh = mod
            return xn, xn * (1.0 + sc) + sh
        return xn
    return f


@jax.custom_vjp
def _rot(y):
    lane = lax.broadcasted_iota(jnp.int32, y.shape, 1)
    return jnp.where(lane % 64 < 32, pltpu.roll(y, 96, axis=1), pltpu.roll(y, 32, axis=1))


_rot.defvjp(lambda y: (_rot(y), None), lambda _, g: (_rot(g),))


def _f_prep(p, cos, sin, qg, kg):
    def head(xh, g):
        ms = jnp.mean(jnp.square(xh), axis=-1, keepdims=True)
        y = xh * lax.rsqrt(ms + RMS_EPS) * g
        return y * cos + _rot(y) * sin
    q = jnp.concatenate([head(p[:, h * HEAD_DIM:(h + 1) * HEAD_DIM], qg) for h in range(N_HEADS)], axis=1)
    k = jnp.concatenate([head(p[:, OFF_K + h * HEAD_DIM:OFF_K + (h + 1) * HEAD_DIM], kg)
                         for h in range(N_KV_HEADS)], axis=1)
    return q, k, p[:, OFF_V:OFF_POOL]


def _f_gate(g, t0, t1, t2, t3, b):
    d = t0.shape[-1]
    ts = (t0, t1, t2, t3)
    terms = [jax.nn.sigmoid(g[:, k * d:(k + 1) * d] + b[:, k * d:(k + 1) * d]) * ts[k] for k in range(N_BRANCH)]
    return terms[0] + terms[1] + terms[2] + terms[3]


def _f_swiglu(a, b):
    return jax.nn.silu(a) * b


def _softmax(s):
    e = jnp.exp(s - jnp.max(s, axis=-1, keepdims=True))
    return e / jnp.sum(e, axis=-1, keepdims=True)


def _attn_fwd(name, q, k, v, rc, ctx_queries, tq=256, rider=None):
    r = q.shape[0]
    assert rc % tq == 0 and r % tq == 0
    nqc = rc // tq

    def body(q_ref, k_ref, v_ref, o_ref):
        qi = pl.program_id(1)

        def attend(nk):
            s = _dot(q_ref[...], k_ref[0:nk, :], _NT) * ATT_SCALE
            p = _softmax(s)
            o_ref[...] = _dot(p, v_ref[0:nk, :], _NN).astype(o_ref.dtype)

        @pl.when(qi < nqc)
        def _():
            if ctx_queries:
                attend(rc)
            else:
                o_ref[...] = jnp.zeros_like(o_ref)

        @pl.when(qi >= nqc)
        def _():
            attend(r)

    outs, r_outs = _pcall(
        name, body, (N_HEADS, r // tq),
        [pl.BlockSpec((tq, HEAD_DIM), lambda h, i: (i, h)),
         pl.BlockSpec((r, HEAD_DIM), lambda h, i: (0, h // KV_GROUP)),
         pl.BlockSpec((r, HEAD_DIM), lambda h, i: (0, h // KV_GROUP))],
        [pl.BlockSpec((tq, HEAD_DIM), lambda h, i: (i, h))],
        [jax.ShapeDtypeStruct((r, Q_W), BF16)], (q, k, v), ("parallel", "parallel"), rider=rider)
    return outs[0] if rider is None else (outs[0], r_outs)


def _attn_bwd(name, q, k, v, do, rc, ctx_queries, tq=256, rider=None):
    r = q.shape[0]
    nqc = rc // tq

    def body(q_ref, k_ref, v_ref, do_ref, dq_ref, dk_ref, dv_ref):
        g, qi = pl.program_id(1), pl.program_id(2)

        @pl.when(jnp.logical_and(g == 0, qi == 0))
        def _():
            dk_ref[...] = jnp.zeros_like(dk_ref)
            dv_ref[...] = jnp.zeros_like(dv_ref)

        def grad(nk):
            qb, kb, vb = q_ref[...], k_ref[0:nk, :], v_ref[0:nk, :]
            dob = do_ref[...].astype(BF16)
            p = _softmax(_dot(qb, kb, _NT) * ATT_SCALE)
            dv_ref[0:nk, :] += _dot(p, dob, _TN)
            dp = _dot(dob, vb, _NT)
            ds = p * (dp - jnp.sum(dp * p, axis=-1, keepdims=True)) * ATT_SCALE
            dq_ref[...] = _dot(ds, kb, _NN)
            dk_ref[0:nk, :] += _dot(ds, qb, _TN)

        @pl.when(qi < nqc)
        def _():
            if ctx_queries:
                grad(rc)
            else:
                dq_ref[...] = jnp.zeros_like(dq_ref)

        @pl.when(qi >= nqc)
        def _():
            grad(r)

    def qmap(kv, g, i):
        return (i, kv * KV_GROUP + g)

    def kvmap(kv, g, i):
        return (0, kv)

    outs, r_outs = _pcall(
        name, body, (N_KV_HEADS, KV_GROUP, r // tq),
        [pl.BlockSpec((tq, HEAD_DIM), qmap), pl.BlockSpec((r, HEAD_DIM), kvmap),
         pl.BlockSpec((r, HEAD_DIM), kvmap), pl.BlockSpec((tq, HEAD_DIM), qmap)],
        [pl.BlockSpec((tq, HEAD_DIM), qmap), pl.BlockSpec((r, HEAD_DIM), kvmap), pl.BlockSpec((r, HEAD_DIM), kvmap)],
        [jax.ShapeDtypeStruct((r, Q_W), F32), jax.ShapeDtypeStruct((r, KV_W), F32),
         jax.ShapeDtypeStruct((r, KV_W), F32)],
        (q, k, v, do), ("arbitrary", "arbitrary", "arbitrary"), rider=rider)
    return outs if rider is None else (outs, r_outs)


def _segments(shape, rc):
    t = lax.broadcasted_iota(jnp.int32, shape, 0)
    lo = jnp.where(t < rc, 0, rc)
    hi = jnp.where(t < rc, rc, shape[0])
    return t, lo, hi


def _shifted(x, o, t, lo, hi):
    n = x.shape[0]
    sh = pltpu.roll(x, (-o) % n, axis=0)
    return jnp.where(jnp.logical_and(t + o >= lo, t + o < hi), sh, 0.0)


def _winsum(x, left, right, t, lo, hi):
    acc = x
    for o in range(-left, right + 1):
        if o != 0:
            acc = acc + _shifted(x, o, t, lo, hi)
    return acc


def _pool_parts(z, g, t, lo, hi):
    w = POOL_WINDOWS[g]
    left = w // 2
    right = w - 1 - left
    count = (jnp.minimum(t + right + 1, hi) - jnp.maximum(t - left, lo)).astype(F32)
    return _winsum(z, left, right, t, lo, hi) / count - z, count, left, right


def _pool_fwd(name, p, pool_w, pool_scale, rc):
    r = p.shape[0]

    def body(z_ref, w_ref, s_ref, y_ref):
        t, lo, hi = _segments((r, GC), rc)
        for g in range(N_GROUPS):
            cols = slice(g * GC, (g + 1) * GC)
            d, _, _, _ = _pool_parts(z_ref[:, cols], g, t, lo, hi)
            y_ref[:, cols] = (_dot(d, w_ref[g], _NN) * s_ref[:, cols]).astype(y_ref.dtype)

    return pl.pallas_call(
        body, name=name, grid=(1,),
        in_specs=[pl.BlockSpec((r, BR_W), lambda i: (0, OFF_POOL // BR_W)),
                  pl.BlockSpec((N_GROUPS, GC, GC), lambda i: (0, 0, 0)),
                  pl.BlockSpec((1, BR_W), lambda i: (0, 0))],
        out_specs=pl.BlockSpec((r, BR_W), lambda i: (0, 0)),
        out_shape=jax.ShapeDtypeStruct((r, BR_W), BF16),
        compiler_params=_cparams("arbitrary"),
    )(p, pool_w, pool_scale.reshape(1, BR_W))


def _pool_bwd(name, p, pool_w, pool_scale, dy, rc):
    r = p.shape[0]

    def body(z_ref, w_ref, s_ref, dy_ref, dz_ref, dw_ref, ds_ref):
        t, lo, hi = _segments((r, GC), rc)
        for g in range(N_GROUPS):
            cols = slice(g * GC, (g + 1) * GC)
            d, count, left, right = _pool_parts(z_ref[:, cols], g, t, lo, hi)
            dyg = dy_ref[:, cols]
            ds_ref[:, cols] = jnp.sum(dyg * _dot(d, w_ref[g], _NN), axis=0, keepdims=True)
            dlin = dyg * s_ref[:, cols]
            dw_ref[g] = _dot(d, dlin, _TN)
            dd = _dot(dlin, w_ref[g], _NT)
            dz_ref[:, cols] = (_winsum(dd / count, right, left, t, lo, hi) - dd).astype(dz_ref.dtype)

    return pl.pallas_call(
        body, name=name, grid=(1,),
        in_specs=[pl.BlockSpec((r, BR_W), lambda i: (0, OFF_POOL // BR_W)),
                  pl.BlockSpec((N_GROUPS, GC, GC), lambda i: (0, 0, 0)),
                  pl.BlockSpec((1, BR_W), lambda i: (0, 0)),
                  pl.BlockSpec((r, BR_W), lambda i: (0, 0))],
        out_specs=[pl.BlockSpec((r, BR_W), lambda i: (0, 0)),
                   pl.BlockSpec((N_GROUPS, GC, GC), lambda i: (0, 0, 0)),
                   pl.BlockSpec((1, BR_W), lambda i: (0, 0))],
        out_shape=[jax.ShapeDtypeStruct((r, BR_W), BF16), jax.ShapeDtypeStruct((N_GROUPS, GC, GC), F32),
                   jax.ShapeDtypeStruct((1, BR_W), F32)],
        compiler_params=_cparams("arbitrary"),
    )(p, pool_w, pool_scale.reshape(1, BR_W), dy)


def _f_sgu_v(pvg, lng, lnb):
    return _ln(jax.nn.gelu(pvg), lng, lnb)


def _sgu_fwd(name, p, ln_g, ln_b, sgu_w, sgu_b):
    r = p.shape[0]

    def body(pu_ref, pv_ref, g_ref, b_ref, w_ref, sb_ref, y_ref):
        vn = _f_sgu_v(pv_ref[...], g_ref[...], b_ref[...])
        u = jax.nn.gelu(pu_ref[...])
        for g in range(N_GROUPS):
            cols = slice(g * GC, (g + 1) * GC)
            s = _dot(w_ref[g], vn[:, cols], _NN) + sb_ref[g]
            y_ref[:, cols] = (u[:, cols] * s).astype(y_ref.dtype)

    return pl.pallas_call(
        body, name=name, grid=(r // SGU_CHUNK,),
        in_specs=[pl.BlockSpec((SGU_CHUNK, BR_W), lambda i: (i, OFF_U // BR_W)),
                  pl.BlockSpec((SGU_CHUNK, BR_W), lambda i: (i, OFF_VG // BR_W)),
                  pl.BlockSpec((1, BR_W), lambda i: (0, 0)), pl.BlockSpec((1, BR_W), lambda i: (0, 0)),
                  pl.BlockSpec((N_GROUPS, GC, GC), lambda i: (0, 0, 0)),
                  pl.BlockSpec((N_GROUPS, SGU_CHUNK, 1), lambda i: (0, 0, 0))],
        out_specs=pl.BlockSpec((SGU_CHUNK, BR_W), lambda i: (i, 0)),
        out_shape=jax.ShapeDtypeStruct((r, BR_W), BF16),
        compiler_params=_cparams("parallel"),
    )(p, p, ln_g.reshape(1, BR_W), ln_b.reshape(1, BR_W), sgu_w, sgu_b.reshape(N_GROUPS, SGU_CHUNK, 1))


def _sgu_bwd(name, p, ln_g, ln_b, sgu_w, sgu_b, dy):
    r = p.shape[0]

    def body(pu_ref, pv_ref, g_ref, b_ref, w_ref, sb_ref, dy_ref, dp_ref, dg_ref, db_ref, dw_ref, dsb_ref):
        i = pl.program_id(0)

        @pl.when(i == 0)
        def _():
            for ref in (dg_ref, db_ref, dw_ref, dsb_ref):
                ref[...] = jnp.zeros_like(ref)

        vn, vjp_v = jax.vjp(_f_sgu_v, pv_ref[...], g_ref[...], b_ref[...])
        u, vjp_u = jax.vjp(jax.nn.gelu, pu_ref[...])
        dy = dy_ref[...]
        du, dvn = [], []
        for g in range(N_GROUPS):
            cols = slice(g * GC, (g + 1) * GC)
            s = _dot(w_ref[g], vn[:, cols], _NN) + sb_ref[g]
            du.append(dy[:, cols] * s)
            ds = dy[:, cols] * u[:, cols]
            dsb_ref[g] += jnp.sum(ds, axis=1, keepdims=True)
            dw_ref[g] += _dot(ds, vn[:, cols], _NT)
            dvn.append(_dot(w_ref[g], ds, _TN))
        (dpu,) = vjp_u(jnp.concatenate(du, axis=1))
        dpv, dg, db = vjp_v(jnp.concatenate(dvn, axis=1))
        dp_ref[:, 0:BR_W] = dpu.astype(dp_ref.dtype)
        dp_ref[:, BR_W:2 * BR_W] = dpv.astype(dp_ref.dtype)
        dg_ref[...] += dg
        db_ref[...] += db

    vec = pl.BlockSpec((1, BR_W), lambda i: (0, 0))
    wsp = pl.BlockSpec((N_GROUPS, GC, GC), lambda i: (0, 0, 0))
    bsp = pl.BlockSpec((N_GROUPS, SGU_CHUNK, 1), lambda i: (0, 0, 0))
    return pl.pallas_call(
        body, name=name, grid=(r // SGU_CHUNK,),
        in_specs=[pl.BlockSpec((SGU_CHUNK, BR_W), lambda i: (i, OFF_U // BR_W)),
                  pl.BlockSpec((SGU_CHUNK, BR_W), lambda i: (i, OFF_VG // BR_W)),
                  vec, vec, wsp, bsp, pl.BlockSpec((SGU_CHUNK, BR_W), lambda i: (i, 0))],
        out_specs=[pl.BlockSpec((SGU_CHUNK, 2 * BR_W), lambda i: (i, 0)), vec, vec, wsp, bsp],
        out_shape=[jax.ShapeDtypeStruct((r, 2 * BR_W), BF16), jax.ShapeDtypeStruct((1, BR_W), F32),
                   jax.ShapeDtypeStruct((1, BR_W), F32), jax.ShapeDtypeStruct((N_GROUPS, GC, GC), F32),
                   jax.ShapeDtypeStruct((N_GROUPS, SGU_CHUNK, 1), F32)],
        compiler_params=_cparams("arbitrary"),
    )(p, p, ln_g.reshape(1, BR_W), ln_b.reshape(1, BR_W), sgu_w, sgu_b.reshape(N_GROUPS, SGU_CHUNK, 1), dy)


def _conv_w8(conv_w):
    return jnp.concatenate([conv_w, jnp.zeros((8 - conv_w.shape[0], conv_w.shape[1]), F32)], axis=0)


def _conv_fwd(name, p, conv_w, rc):
    r = p.shape[0]

    def body(cb_ref, cc_ref, cx_ref, w_ref, y_ref):
        t, lo, hi = _segments((r, GC), rc)
        z = cc_ref[...] * cx_ref[...]
        w = w_ref[...]
        c = _shifted(z, -1, t, lo, hi) * w[0:1] + z * w[1:2] + _shifted(z, 1, t, lo, hi) * w[2:3]
        y_ref[...] = (cb_ref[...] * c).astype(y_ref.dtype)

    nb = OFF_CB // GC
    return pl.pallas_call(
        body, name=name, grid=(N_GROUPS,),
        in_specs=[pl.BlockSpec((r, GC), lambda j: (0, nb + j)),
                  pl.BlockSpec((r, GC), lambda j: (0, nb + N_GROUPS + j)),
                  pl.BlockSpec((r, GC), lambda j: (0, nb + 2 * N_GROUPS + j)),
                  pl.BlockSpec((8, GC), lambda j: (0, j))],
        out_specs=pl.BlockSpec((r, GC), lambda j: (0, j)),
        out_shape=jax.ShapeDtypeStruct((r, BR_W), BF16),
        compiler_params=_cparams("parallel"),
    )(p, p, p, _conv_w8(conv_w))


def _conv_bwd(name, p, conv_w, dy, rc):
    r = p.shape[0]

    def body(cb_ref, cc_ref, cx_ref, w_ref, dy_ref, dcb_ref, dcc_ref, dcx_ref, dw_ref):
        t, lo, hi = _segments((r, GC), rc)
        cc, cx, w, dy = cc_ref[...], cx_ref[...], w_ref[...], dy_ref[...]
        z = cc * cx
        zp, zn = _shifted(z, -1, t, lo, hi), _shifted(z, 1, t, lo, hi)
        dcb_ref[...] = (dy * (zp * w[0:1] + z * w[1:2] + zn * w[2:3])).astype(dcb_ref.dtype)
        dc = dy * cb_ref[...]
        dw_ref[...] = jnp.concatenate(
            [jnp.sum(dc * zp, axis=0, keepdims=True), jnp.sum(dc * z, axis=0, keepdims=True),
             jnp.sum(dc * zn, axis=0, keepdims=True), jnp.zeros((5, GC), F32)], axis=0)
        dz = dc * w[1:2] + _shifted(dc, 1, t, lo, hi) * w[0:1] + _shifted(dc, -1, t, lo, hi) * w[2:3]
        dcc_ref[...] = (dz * cx).astype(dcc_ref.dtype)
        dcx_ref[...] = (dz * cc).astype(dcx_ref.dtype)

    nb = OFF_CB // GC
    return pl.pallas_call(
        body, name=name, grid=(N_GROUPS,),
        in_specs=[pl.BlockSpec((r, GC), lambda j: (0, nb + j)),
                  pl.BlockSpec((r, GC), lambda j: (0, nb + N_GROUPS + j)),
                  pl.BlockSpec((r, GC), lambda j: (0, nb + 2 * N_GROUPS + j)),
                  pl.BlockSpec((8, GC), lambda j: (0, j)),
                  pl.BlockSpec((r, GC), lambda j: (0, j))],
        out_specs=[pl.BlockSpec((r, GC), lambda j: (0, j))] * 3 + [pl.BlockSpec((8, GC), lambda j: (0, j))],
        out_shape=[jax.ShapeDtypeStruct((r, BR_W), BF16)] * 3 + [jax.ShapeDtypeStruct((8, BR_W), F32)],
        compiler_params=_cparams("parallel"),
    )(p, p, p, _conv_w8(conv_w), dy)


def _rope_tables(rc, n):
    rows = n // GRID_W
    row = jnp.repeat(jnp.arange(rows), GRID_W).astype(F32)
    col = jnp.tile(jnp.arange(GRID_W), rows).astype(F32)
    inv = ROPE_THETA ** (-jnp.arange(0, ROPE_AXIS_DIM, 2, dtype=F32) / ROPE_AXIS_DIM)
    ang_r, ang_c = row[:, None] * inv, col[:, None] * inv
    cos = jnp.concatenate([jnp.cos(ang_r), jnp.cos(ang_r), jnp.cos(ang_c), jnp.cos(ang_c)], axis=1)
    sin = jnp.concatenate([-jnp.sin(ang_r), jnp.sin(ang_r), -jnp.sin(ang_c), jnp.sin(ang_c)], axis=1)
    cos = jnp.concatenate([jnp.ones((rc, HEAD_DIM), F32), cos], axis=0)
    sin = jnp.concatenate([jnp.zeros((rc, HEAD_DIM), F32), sin], axis=0)
    return cos, sin


MOD_NAMES = ("sh1", "sc1", "g1", "sh2", "sc2", "g2")


def _local_step(xin, target, mod, comm, sp, rc, alpha):
    def carrying(fn):
        def call(name, *args, **kw):
            rider = comm.rider(name)
            if rider is None:
                return fn(name, *args, **kw)
            res, r_outs = fn(name, *args, rider=rider, **kw)
            comm.deliver(name, r_outs)
            return res
        return call

    mm, rows, attn_fwd, attn_bwd = carrying(_mm), carrying(_rows), carrying(_attn_fwd), carrying(_attn_bwd)
    r, d = xin.shape
    n_layers = mod.shape[0]
    tm_n, tm_w = 256, 128
    nbc_n, nbc_w = rc // tm_n, rc // tm_w
    cos, sin = _rope_tables(rc, r - rc)
    mp = mod.reshape(n_layers, 2, 6, 1, d)
    mods = [{nm: mp[i, :, j] for j, nm in enumerate(MOD_NAMES)} for i in range(n_layers)]
    f_ln_mod, f_ln_last = _make_f_ln(alpha, True), _make_f_ln(alpha, False)

    def whole(arr, roff=0):
        return (arr, 0, arr.shape[1], roff)

    (hb,) = rows("mod_in", _f_mod, r, tm_n, nbc_n, [whole(xin)], [mods[0]["sc1"], mods[0]["sh1"]],
                 [(r, d, BF16, 0)], [])
    saved = []
    x = xin
    for i in range(n_layers):
        last = i == n_layers - 1
        w, s, m = functools.partial(comm.weight, i), sp[i], mods[i]
        sv = {"x": x, "hb": hb}
        p = mm(f"l{i}_in", hb, w("in_t"), "nt", F32)
        gpre = mm(f"l{i}_gate", hb, w("gate_t"), "nt", F32)
        q, k, v = rows(f"l{i}_prep", _f_prep, r, tm_n, nbc_n,
                       [(p, 0, OFF_POOL, 0), whole(cos), whole(sin)], [_typed(s["q_norm_g"]), _typed(s["k_norm_g"])],
                       [(r, Q_W, BF16, 0), (r, KV_W, BF16, 0), (r, KV_W, BF16, 0)], [])
        ys = [attn_fwd(f"l{i}_attn", q, k, v, rc, not last),
              _pool_fwd(f"l{i}_pool", p, s["pool_w"], s["pool_scale"], rc),
              _sgu_fwd(f"l{i}_sgu", p, s["sgu_ln_g"], s["sgu_ln_b"], s["sgu_w"], s["sgu_b"]),
              _conv_fwd(f"l{i}_conv", p, s["conv_w"], rc)]
        ts = [mm(f"l{i}_br{kk}", ys[kk], w(f"br{kk}"), "nt", F32) for kk in range(N_BRANCH)]
        (mg,) = rows(f"l{i}_merge", _f_gate, r, tm_w, nbc_w, [whole(gpre)] + [whole(t) for t in ts],
                     [_typed(s["b_gate"])], [(r, d, BF16, 0)], [])
        o = mm(f"l{i}_o", mg, w("o"), "nn", F32)
        x1, h2b = rows(f"l{i}_ln1", f_ln_mod, r, tm_n, nbc_n, [whole(x), whole(o)],
                       [m["g1"], _typed(s["ln1_g"]), _typed(s["ln1_b"]), m["sc2"], m["sh2"]],
                       [(r, d, F32, 0), (r, d, BF16, 0)], [])
        af = mm(f"l{i}_ffg", h2b, w("ffg_t"), "nt", F32)
        bf = mm(f"l{i}_ffu", h2b, w("ffu_t"), "nt", F32)
        (f,) = rows(f"l{i}_swiglu", _f_swiglu, r, tm_w, nbc_w, [whole(af), whole(bf)], [],
                    [(r, af.shape[1], BF16, 0)], [])
        o2 = mm(f"l{i}_ffd", f, w("ffd"), "nn", F32)
        if last:
            (x2,) = rows(f"l{i}_ln2", f_ln_last, r, tm_n, nbc_n, [whole(x1), whole(o2)],
                         [m["g2"], _typed(s["ln2_g"]), _typed(s["ln2_b"])], [(r, d, F32, 0)], [])
            hb = None
        else:
            nx = mods[i + 1]
            x2, hb = rows(f"l{i}_ln2", f_ln_mod, r, tm_n, nbc_n, [whole(x1), whole(o2)],
                          [m["g2"], _typed(s["ln2_g"]), _typed(s["ln2_b"]), nx["sc1"], nx["sh1"]],
                          [(r, d, F32, 0), (r, d, BF16, 0)], [])
        sv.update(p=p, gpre=gpre, q=q, k=k, v=v, ys=ys, ts=ts, mg=mg, o=o, x1=x1, h2b=h2b, af=af, bf=bf, f=f, o2=o2)
        saved.append(sv)
        x = x2

    lat = jnp.concatenate([jnp.zeros((1, 1, 128), F32), jnp.ones((1, 1, 128), F32)], axis=0)

    def f_loss(xb, tb, msk):
        diff = (xb - tb) * msk[:, 0:1]
        part = jnp.sum(jnp.mean(jnp.square(diff), axis=-1, keepdims=True), axis=0, keepdims=True)
        return diff * (1.0 / d), jnp.broadcast_to(part, (1, 128))

    dx_direct, loss_acc = rows("loss", f_loss, r, tm_n, nbc_n, [whole(x), whole(target, nbc_n)], [lat],
                               [(r, d, F32, 0)], [128])
    loss = 0.5 * loss_acc[1, 0, 0]

    dmods = [dict() for _ in range(n_layers)]
    dsp = [dict() for _ in range(n_layers)]
    dh = None
    for i in reversed(range(n_layers)):
        last = i == n_layers - 1
        w, s, m, sv = functools.partial(comm.weight, i), sp[i], mods[i], saved[i]
        dm, dw, ds = dmods[i], {}, dsp[i]
        ln2 = [m["g2"], _typed(s["ln2_g"]), _typed(s["ln2_b"])]
        if last:
            res = rows(f"l{i}_ln2_bwd", _vjp_fn(f_ln_last, 2, 1), r, tm_n, nbc_n,
                       [whole(sv["x1"]), whole(sv["o2"]), whole(dx_direct)], ln2,
                       [(r, d, F32, 0), (r, d, BF16, 0)], [d, d, d])
            dx1, do2, dm["g2"], dlg, dlb = res
        else:
            nx = mods[i + 1]
            res = rows(f"l{i}_ln2_bwd", _vjp_fn(f_ln_mod, 2, 2), r, tm_n, nbc_n,
                       [whole(sv["x1"]), whole(sv["o2"]), whole(dx_direct), whole(dh)],
                       ln2 + [nx["sc1"], nx["sh1"]],
                       [(r, d, F32, 0), (r, d, BF16, 0)], [d, d, d, d, d])
            dx1, do2, dm["g2"], dlg, dlb, dmods[i + 1]["sc1"], dmods[i + 1]["sh1"] = res
        ds["ln2_g"], ds["ln2_b"] = dlg, dlb
        df = mm(f"l{i}_dF", do2, w("ffd"), "nt", F32)
        dw["ffd"] = mm(f"l{i}_dWffd", sv["f"], do2, "tn", BF16)
        dab, dbb = rows(f"l{i}_swiglu_bwd", _vjp_fn(_f_swiglu, 2, 1), r, tm_w, nbc_w,
                        [whole(sv["af"]), whole(sv["bf"]), whole(df)], [],
                        [(r, df.shape[1], BF16, 0), (r, df.shape[1], BF16, 0)], [])
        dh2 = mm(f"l{i}_dh2a", dab, w("ffg_t"), "nn", F32)
        dh2 = mm(f"l{i}_dh2b", dbb, w("ffu_t"), "nn", F32, acc=dh2)
        dw["ffg_t"] = mm(f"l{i}_dWffg", dab, sv["h2b"], "tn", BF16)
        dw["ffu_t"] = mm(f"l{i}_dWffu", dbb, sv["h2b"], "tn", BF16)
        comm.grads(i, {key: dw[key] for key in ("ffd", "ffg_t", "ffu_t")})
        res = rows(f"l{i}_ln1_bwd", _vjp_fn(f_ln_mod, 2, 2), r, tm_n, nbc_n,
                   [whole(sv["x"]), whole(sv["o"]), whole(dx1), whole(dh2)],
                   [m["g1"], _typed(s["ln1_g"]), _typed(s["ln1_b"]), m["sc2"], m["sh2"]],
                   [(r, d, F32, 0), (r, d, BF16, 0)], [d, d, d, d, d])
        dx_direct, do, dm["g1"], ds["ln1_g"], ds["ln1_b"], dm["sc2"], dm["sh2"] = res
        dmg = mm(f"l{i}_dMg", do, w("o"), "nt", F32)
        dw["o"] = mm(f"l{i}_dWo", sv["mg"], do, "tn", BF16)
        res = rows(f"l{i}_merge_bwd", _vjp_fn(_f_gate, 5, 1), r, tm_w, nbc_w,
                   [whole(sv["gpre"])] + [whole(t) for t in sv["ts"]] + [whole(dmg)], [_typed(s["b_gate"])],
                   [(r, N_BRANCH * d, BF16, 0)] + [(r, d, BF16, 0)] * N_BRANCH, [N_BRANCH * d])
        dgb, dts, ds["b_gate"] = res[0], res[1:1 + N_BRANCH], res[1 + N_BRANCH]
        dys = [mm(f"l{i}_dY{kk}", dts[kk], w(f"br{kk}"), "nn", F32) for kk in range(N_BRANCH)]
        for kk in range(N_BRANCH):
            dw[f"br{kk}"] = mm(f"l{i}_dWbr{kk}", dts[kk], sv["ys"][kk], "tn", BF16)
        comm.grads(i, {key: dw[key] for key in ("o", "br0", "br1", "br2", "br3")})
        dq, dk, dv = attn_bwd(f"l{i}_attn_bwd", sv["q"], sv["k"], sv["v"], dys[0], rc, not last)
        res = rows(f"l{i}_prep_bwd", _vjp_fn(_f_prep, 3, 3, keep=(0, 3, 4)), r, tm_n, nbc_n,
                   [(sv["p"], 0, OFF_POOL, 0), whole(cos), whole(sin), whole(dq), whole(dk), whole(dv)],
                   [_typed(s["q_norm_g"]), _typed(s["k_norm_g"])],
                   [(r, OFF_POOL, BF16, 0)], [HEAD_DIM, HEAD_DIM])
        dp_qkv, ds["q_norm_g"], ds["k_norm_g"] = res
        dp_pool, ds["pool_w"], ds["pool_scale"] = _pool_bwd(f"l{i}_pool_bwd", sv["p"], s["pool_w"], s["pool_scale"],
                                                            dys[1], rc)
        dp_sgu, ds["sgu_ln_g"], ds["sgu_ln_b"], ds["sgu_w"], ds["sgu_b"] = _sgu_bwd(
            f"l{i}_sgu_bwd", sv["p"], s["sgu_ln_g"], s["sgu_ln_b"], s["sgu_w"], s["sgu_b"], dys[2])
        dp_cb, dp_cc, dp_cx, dcw = _conv_bwd(f"l{i}_conv_bwd", sv["p"], s["conv_w"], dys[3], rc)
        ds["conv_w"] = dcw[0:3]
        dpb = jnp.concatenate([dp_qkv, dp_pool, dp_sgu, dp_cb, dp_cc, dp_cx], axis=1)
        dh = mm(f"l{i}_dhb_a", dpb, w("in_t"), "nn", F32)
        dh = mm(f"l{i}_dhb_b", dgb, w("gate_t"), "nn", F32, acc=dh)
        dw["in_t"] = mm(f"l{i}_dWin", dpb, sv["hb"], "tn", BF16)
        dw["gate_t"] = mm(f"l{i}_dWgate", dgb, sv["hb"], "tn", BF16)
        comm.grads(i, {key: dw[key] for key in ("in_t", "gate_t")})

    def f_mod_bwd(xb, ddir, dhb, sc, sh):
        _, vjp = jax.vjp(_f_mod, xb, sc, sh)
        dxb, dsc, dsh = vjp(dhb)
        return dxb + ddir, dsc, dsh

    grad_x, dmods[0]["sc1"], dmods[0]["sh1"] = rows(
        "mod_in_bwd", f_mod_bwd, r, tm_n, nbc_n, [whole(xin), whole(dx_direct), whole(dh)],
        [mods[0]["sc1"], mods[0]["sh1"]], [(r - rc, d, F32, nbc_n)], [d, d])
    dmod = jnp.stack([jnp.concatenate([dmods[i][nm][:, 0, :] for nm in MOD_NAMES], axis=-1)
                      for i in range(n_layers)])
    for ds in dsp:
        for nm in ("ln1_g", "ln1_b", "ln2_g", "ln2_b", "b_gate", "q_norm_g", "k_norm_g"):
            ds[nm] = ds[nm][0, 0] + ds[nm][1, 0]
        ds["pool_scale"] = ds["pool_scale"].reshape(-1)
        ds["sgu_ln_g"] = ds["sgu_ln_g"].reshape(-1)
        ds["sgu_ln_b"] = ds["sgu_ln_b"].reshape(-1)
        ds["sgu_b"] = ds["sgu_b"].reshape(N_GROUPS, SGU_CHUNK)
    return loss, grad_x, dmod, dsp


def _exchange(name, srcs, scatter):
    n_items = len(srcs)
    out_shapes = [jax.ShapeDtypeStruct(s.shape if scatter else (N_DEV,) + s.shape, s.dtype) for s in srcs]

    def body(*refs):
        src, out = refs[:n_items], refs[n_items:2 * n_items]
        send_sems, recv_sems, loc_sems = refs[2 * n_items:]
        mx, my, mc = [lax.axis_index(a) for a in MESH_AXES]
        me = 4 * mx + 2 * my + mc
        peers = []
        for kk in range(1, N_DEV):
            px = 1 - mx if kk & 4 else mx
            py = 1 - my if kk & 2 else my
            pc = 1 - mc if kk & 1 else mc
            peers.append(((px, py, pc), 4 * px + 2 * py + pc))
        local, sent = [], []
        for a in range(n_items):
            loc = pltpu.make_async_copy(src[a].at[me] if scatter else src[a], out[a].at[me], loc_sems.at[a])
            loc.start()
            local.append(loc)
            for j, (peer, peer_l) in enumerate(peers):
                cp = pltpu.make_async_remote_copy(
                    src_ref=src[a].at[peer_l] if scatter else src[a], dst_ref=out[a].at[me],
                    send_sem=send_sems.at[a * (N_DEV - 1) + j], recv_sem=recv_sems.at[a * (N_DEV - 1) + j],
                    device_id=peer, device_id_type=pl.DeviceIdType.MESH)
                cp.start()
                sent.append(cp)
        for a in range(n_items):
            for j, (peer, peer_l) in enumerate(peers):
                pltpu.make_async_remote_copy(
                    src_ref=src[a].at[peer_l] if scatter else src[a], dst_ref=out[a].at[peer_l],
                    send_sem=send_sems.at[a * (N_DEV - 1) + j], recv_sem=recv_sems.at[a * (N_DEV - 1) + j],
                    device_id=peer, device_id_type=pl.DeviceIdType.MESH).wait_recv()
        for cp in sent:
            cp.wait_send()
        for loc in local:
            loc.wait()

    any_spec = pl.BlockSpec(memory_space=pl.ANY)
    res = pl.pallas_call(
        body, name=name,
        in_specs=[any_spec] * n_items, out_specs=[any_spec] * n_items, out_shape=out_shapes,
        scratch_shapes=[pltpu.SemaphoreType.DMA((n_items * (N_DEV - 1),)),
                        pltpu.SemaphoreType.DMA((n_items * (N_DEV - 1),)),
                        pltpu.SemaphoreType.DMA((n_items,))],
    )(*srcs)
    return list(res)


def _mesh_place():
    mx, my, mc = [lax.axis_index(a) for a in MESH_AXES]
    chips = [(1 - mx, my), (mx, 1 - my), (1 - mx, 1 - my)]

    def lid(px, py, pc):
        return 4 * px + 2 * py + pc

    return (mx, my, mc), (mx, my, 1 - mc), chips, lid


def _rcopy(src, dst, send_sems, recv_sems, k, to):
    return pltpu.make_async_remote_copy(src_ref=src, dst_ref=dst, send_sem=send_sems.at[k], recv_sem=recv_sems.at[k],
                                        device_id=to, device_id_type=pl.DeviceIdType.MESH)


def _sem_scratch(*sizes):
    return [pltpu.SemaphoreType.DMA((s,)) for s in sizes]


def _gather_rider(srcs):
    n = len(srcs)

    def start(ins, outs, sems):
        send, recv, loc = sems
        (mx, my, mc), sib, chips, lid = _mesh_place()
        me = lid(mx, my, mc)
        for a in range(n):
            pltpu.make_async_copy(ins[a], outs[a].at[me], loc.at[a]).start()
            _rcopy(ins[a], outs[a].at[me], send, recv, 7 * a, sib).start()
            for j, chip in enumerate(chips):
                _rcopy(ins[a], outs[a].at[me], send, recv, 7 * a + 1 + j, (*chip, mc)).start()

    def finish(ins, outs, sems):
        send, recv, loc = sems
        (mx, my, mc), sib, chips, lid = _mesh_place()
        me = lid(mx, my, mc)
        for a in range(n):
            for j, chip in enumerate(chips):
                blk = outs[a].at[lid(*chip, mc)]
                _rcopy(ins[a], blk, send, recv, 7 * a + 1 + j, (*chip, mc)).wait_recv()
                _rcopy(blk, blk, send, recv, 7 * a + 4 + j, sib).start()
        for a in range(n):
            _rcopy(ins[a], outs[a].at[lid(*sib)], send, recv, 7 * a, sib).wait_recv()
            for j, chip in enumerate(chips):
                _rcopy(ins[a], outs[a].at[lid(*chip, 1 - mc)], send, recv, 7 * a + 4 + j, sib).wait_recv()
            for t in range(7):
                _rcopy(ins[a], outs[a].at[me], send, recv, 7 * a + t, sib).wait_send()
            pltpu.make_async_copy(ins[a], outs[a].at[me], loc.at[a]).wait()

    out_shapes = [jax.ShapeDtypeStruct((N_DEV,) + s.shape, s.dtype) for s in srcs]
    return _Rider(srcs, out_shapes, _sem_scratch(7 * n, 7 * n, n), start, finish)


def _sibling_rider(parts):
    n = len(parts)

    def slabs(lid, sib, chips, mc):
        return [lid(*sib)] + [lid(*chip, 1 - mc) for chip in chips]

    def start(ins, outs, sems):
        send, recv = sems
        (mx, my, mc), sib, chips, lid = _mesh_place()
        for a in range(n):
            for t, slab in enumerate(slabs(lid, sib, chips, mc)):
                _rcopy(ins[a].at[slab], outs[a].at[t], send, recv, 4 * a + t, sib).start()

    def finish(ins, outs, sems):
        send, recv = sems
        (mx, my, mc), sib, chips, lid = _mesh_place()
        for a in range(n):
            for t in range(4):
                cp = _rcopy(ins[a].at[0], outs[a].at[t], send, recv, 4 * a + t, sib)
                cp.wait_recv()
                cp.wait_send()

    out_shapes = [jax.ShapeDtypeStruct((4,) + p.shape[1:], p.dtype) for p in parts]
    return _Rider(parts, out_shapes, _sem_scratch(4 * n, 4 * n), start, finish)


def _chips_rider(pairs):
    n = len(pairs)

    def start(ins, outs, sems):
        send, recv = sems
        (mx, my, mc), sib, chips, lid = _mesh_place()
        for a in range(n):
            for j, chip in enumerate(chips):
                _rcopy(ins[a].at[j], outs[a].at[j], send, recv, 3 * a + j, (*chip, mc)).start()

    def finish(ins, outs, sems):
        send, recv = sems
        (mx, my, mc), sib, chips, lid = _mesh_place()
        for a in range(n):
            for j, chip in enumerate(chips):
                cp = _rcopy(ins[a].at[j], outs[a].at[j], send, recv, 3 * a + j, (*chip, mc))
                cp.wait_recv()
                cp.wait_send()

    out_shapes = [jax.ShapeDtypeStruct(p.shape, p.dtype) for p in pairs]
    return _Rider(pairs, out_shapes, _sem_scratch(3 * n, 3 * n), start, finish)


def _run_rider(name, rider):
    n_in, n_out = len(rider.inputs), len(rider.out_shapes)

    def body(*refs):
        ins, outs, sems = refs[:n_in], refs[n_in:n_in + n_out], refs[n_in + n_out:]
        rider.start(ins, outs, sems)
        rider.finish(ins, outs, sems)

    any_spec = pl.BlockSpec(memory_space=pl.ANY)
    res = pl.pallas_call(body, name=name, in_specs=[any_spec] * n_in, out_specs=[any_spec] * n_out,
                         out_shape=rider.out_shapes, scratch_shapes=rider.scratch)(*rider.inputs)
    return list(res)


def _slab_ids():
    (mx, my, mc), _, chips, lid = _mesh_place()
    return jnp.stack([lid(*chip, mc) for chip in chips] + [lid(mx, my, mc)]).astype(jnp.int32)


def _pair_sum(name, part, rsib, ids):
    _, n, k = part.shape
    tr = _row_tile(n, 128, 16)

    def body(ids_ref, p_ref, r_ref, o_ref):
        o_ref[...] = (p_ref[...].astype(F32) + r_ref[...].astype(F32)).astype(o_ref.dtype)

    grid_spec = pltpu.PrefetchScalarGridSpec(
        num_scalar_prefetch=1, grid=(3, n // tr),
        in_specs=[pl.BlockSpec((None, tr, k), lambda j, i, ids: (ids[j], i, 0)),
                  pl.BlockSpec((None, tr, k), lambda j, i, ids: (1 + j, i, 0))],
        out_specs=pl.BlockSpec((None, tr, k), lambda j, i, ids: (j, i, 0)))
    return pl.pallas_call(body, name=name, grid_spec=grid_spec, out_shape=jax.ShapeDtypeStruct((3, n, k), part.dtype),
                          compiler_params=_cparams("parallel", "parallel"))(ids, part, rsib)


def _sum5(name, part, rsib, rici, ids):
    _, n, k = part.shape
    tr = _row_tile(n, 128, 16)

    def body(ids_ref, p_ref, r_ref, c_ref, o_ref):
        acc = p_ref[...].astype(F32) + r_ref[...].astype(F32)
        for j in range(3):
            acc = acc + c_ref[j].astype(F32)
        o_ref[...] = acc

    grid_spec = pltpu.PrefetchScalarGridSpec(
        num_scalar_prefetch=1, grid=(n // tr,),
        in_specs=[pl.BlockSpec((None, tr, k), lambda i, ids: (ids[3], i, 0)),
                  pl.BlockSpec((None, tr, k), lambda i, ids: (0, i, 0)),
                  pl.BlockSpec((3, tr, k), lambda i, ids: (0, i, 0))],
        out_specs=pl.BlockSpec((tr, k), lambda i, ids: (i, 0)))
    return pl.pallas_call(body, name=name, grid_spec=grid_spec, out_shape=jax.ShapeDtypeStruct((n, k), F32),
                          compiler_params=_cparams("parallel"))(ids, part, rsib, rici)


W_KEYS = ("in_t", "gate_t", "br0", "br1", "br2", "br3", "o", "ffg_t", "ffu_t", "ffd")


class _Comm:
    def __init__(self, wsrc):
        self.wsrc = wsrc
        nl = self.n_layers = len(wsrc)
        self.w = [dict() for _ in range(nl)]
        self.part, self.rsib, self.pair, self.rici = {}, {}, {}, {}
        self.ids = _slab_ids()
        brs = ["br0", "br1", "br2", "br3"]
        gather, scatter = {}, {}
        for i in range(nl):
            gather[f"l{i}_in"] = [(i, k) for k in brs + ["o"]]
            gather[f"l{i}_gate"] = [(i, "ffg_t")]
            gather[f"l{i}_attn"] = [(i, "ffu_t")]
            gather[f"l{i}_ffg"] = [(i, "ffd")]
            if i + 1 < nl:
                gather[f"l{i}_ffu"] = [(i + 1, "in_t")]
                gather[f"l{i}_ffd"] = [(i + 1, "gate_t")]
            scatter[f"l{i}_merge_bwd"] = [(i, "ffd")]
            scatter[f"l{i}_attn_bwd"] = [(i, "ffg_t")]
            scatter[f"l{i}_dhb_b"] = [(i, "ffu_t")]
            scatter[f"l{i}_dWgate"] = [(i, k) for k in ["o"] + brs]
            scatter[f"l{i - 1}_dF" if i else "adamw_w_ada"] = [(i, "in_t")]
            scatter[f"l{i - 1}_dh2a" if i else "adamw_w_ff_gate"] = [(i, "gate_t")]
        self.gather, self.scatter = gather, scatter

    def begin(self):
        self._deliver_gather([(0, "in_t"), (0, "gate_t")],
                             _run_rider("gather_w_first", _gather_rider([self.wsrc[0]["in_t"], self.wsrc[0]["gate_t"]])))

    def _deliver_gather(self, items, outs):
        for (i, k), o in zip(items, outs):
            self.w[i][k] = o.reshape(-1, o.shape[-1])

    def rider(self, name):
        if name in self.gather:
            return _gather_rider([self.wsrc[i][k] for i, k in self.gather[name]])
        if name in self.scatter:
            return _chips_rider([self.pair[it] for it in self.scatter[name]])
        return None

    def deliver(self, name, outs):
        if name in self.gather:
            self._deliver_gather(self.gather.pop(name), outs)
        else:
            for it, o in zip(self.scatter.pop(name), outs):
                self.rici[it] = o

    def weight(self, i, k):
        return self.w[i][k]

    def grads(self, i, group):
        keys = list(group)
        parts = [group[k].reshape(N_DEV, group[k].shape[0] // N_DEV, group[k].shape[1]) for k in keys]
        got = _run_rider(f"sibling_l{i}_{keys[0]}", _sibling_rider(parts))
        for k, p, r in zip(keys, parts, got):
            self.part[(i, k)], self.rsib[(i, k)] = p, r
            self.pair[(i, k)] = _pair_sum(f"pair_l{i}_{k}", p, r, self.ids)

    def total(self, i, k):
        if (i, k) not in self.rici:
            name = next(nm for nm, items in self.scatter.items() if (i, k) in items)
            self.deliver(name, _run_rider(f"chips_{name}", self.rider(name)))
        return _sum5(f"sum_l{i}_{k}", self.part[(i, k)], self.rsib[(i, k)], self.rici[(i, k)], self.ids)


def _row_tile(n, pref, mult):
    best = None
    t = mult
    while t <= min(n, pref):
        if n % t == 0:
            best = t
        t += mult
    return best if best is not None else n


def _sum8(name, slabs):
    _, n, k = slabs.shape
    tr = _row_tile(n, 128, 16)

    def body(s_ref, o_ref):
        acc = s_ref[0].astype(F32)
        for j in range(1, N_DEV):
            acc = acc + s_ref[j].astype(F32)
        o_ref[...] = acc

    return pl.pallas_call(
        body, name=name, grid=(n // tr,),
        in_specs=[pl.BlockSpec((N_DEV, tr, k), lambda i: (0, i, 0))],
        out_specs=pl.BlockSpec((tr, k), lambda i: (i, 0)),
        out_shape=jax.ShapeDtypeStruct((n, k), F32),
        compiler_params=_cparams("parallel"),
    )(slabs)


def _adamw(name, w, g, m, v, rider=None):
    n, k = w.shape
    tr = _row_tile(n, 256, 8)

    def body(w_ref, g_ref, m_ref, v_ref, d_ref, m2_ref, v2_ref):
        gv = g_ref[...]
        m2 = ADAM_B1 * m_ref[...] + (1.0 - ADAM_B1) * gv
        v2 = ADAM_B2 * v_ref[...] + (1.0 - ADAM_B2) * jnp.square(gv)
        m_hat = m2 / (1.0 - ADAM_B1 ** ADAM_STEP)
        v_hat = v2 / (1.0 - ADAM_B2 ** ADAM_STEP)
        d_ref[...] = -ADAM_LR * (m_hat / (jnp.sqrt(v_hat) + ADAM_EPS) + ADAM_WD * w_ref[...])
        m2_ref[...] = m2
        v2_ref[...] = v2

    spec = pl.BlockSpec((tr, k), lambda i: (i, 0))
    outs, r_outs = _pcall(name, body, (n // tr,), [spec] * 4, [spec] * 3, [jax.ShapeDtypeStruct((n, k), F32)] * 3,
                          (w, g, m, v), ("parallel",), rider=rider)
    return outs if rider is None else (outs, r_outs)


def _pack(arrs):
    flat = jnp.concatenate([a.reshape(-1).astype(F32) for a in arrs])
    pad = (-flat.shape[0]) % 2048
    if pad:
        flat = jnp.concatenate([flat, jnp.zeros((pad,), F32)])
    return flat.reshape(-1, 128)


def _unpack(packed, shapes):
    flat = packed.reshape(-1)
    out, off = [], 0
    for shp in shapes:
        size = math.prod(shp)
        out.append(flat[off:off + size].reshape(shp))
        off += size
    return out


WEIGHT_NAMES = ("c_ctx", "w_ada", "b_ada", "w_in", "q_norm_g", "k_norm_g", "pool_w", "pool_scale", "sgu_ln_g",
                "sgu_ln_b", "sgu_w", "sgu_b", "conv_w", "w_br_attn", "w_br_pool", "w_br_sgu", "w_br_conv", "w_gate",
                "b_gate", "w_o", "ln1_g", "ln1_b", "w_ff_gate", "w_ff_up", "w_ff_down", "ln2_g", "ln2_b")
COL_SHARDED = {"w_in": "in_t", "w_gate": "gate_t", "w_ff_gate": "ffg_t", "w_ff_up": "ffu_t",
               "w_br_attn": "br0", "w_br_pool": "br1", "w_br_sgu": "br2", "w_br_conv": "br3"}
ROW_SHARDED = {"w_o": "o", "w_ff_down": "ffd"}
LAYER_SMALL = ("q_norm_g", "k_norm_g", "pool_w", "pool_scale", "sgu_ln_g", "sgu_ln_b", "sgu_w", "sgu_b", "b_gate",
               "ln1_g", "ln1_b", "ln2_g", "ln2_b")
SMALL_ORDER = ("c_ctx", "b_ada") + LAYER_SMALL + ("conv_w",)


def _train_step(a):
    n_layers, d = a["w_in"].shape[0], a["x"].shape[-1]
    rc = a["ctx"].shape[1]
    alpha = (2 * n_layers) ** 0.25
    mx, my, mc = [lax.axis_index(ax) for ax in MESH_AXES]
    me = 4 * mx + 2 * my + mc
    ada_w = a["w_ada"].shape[-1]
    cw_loc = a["conv_w"].shape[-1]

    n_c, n_cw = d, n_layers * 3 * cw_loc
    got = _exchange("gather_cond", [_pack([a["c"], a["conv_w"]])], False)[0].reshape(N_DEV, -1)
    c_all = got[:, :n_c]
    conv_w = got[:, n_c:n_c + n_cw].reshape(N_DEV, n_layers, 3, cw_loc).transpose(1, 2, 0, 3).reshape(n_layers, 3, -1)
    cond = jnp.concatenate([c_all, a["c_ctx"][None], jnp.zeros((16 - N_DEV - 1, d), F32)], axis=0)
    sil, sil_vjp = jax.vjp(jax.nn.silu, cond)
    sil = sil.astype(BF16)

    mod_cols = jnp.concatenate([_mm(f"ada{i}", sil, a["w_ada"][i], "nn", F32) for i in range(n_layers)], axis=0)
    got = _exchange("gather_mod", [mod_cols], False)[0]
    mod_all = got.reshape(N_DEV, n_layers, 16, ada_w).transpose(1, 2, 0, 3).reshape(n_layers, 16, -1)
    mod_all = mod_all + a["b_ada"][:, None, :]
    mod = jnp.stack([mod_all[:, N_DEV], lax.dynamic_index_in_dim(mod_all, me, axis=1, keepdims=False)], axis=1)

    comm = _Comm([{**{key: a[nm][i].T.astype(BF16) for nm, key in COL_SHARDED.items()},
                   **{key: a[nm][i].astype(BF16) for nm, key in ROW_SHARDED.items()}} for i in range(n_layers)])
    comm.begin()
    sp = [{nm: a[nm][i] for nm in LAYER_SMALL} for i in range(n_layers)]
    for i in range(n_layers):
        sp[i]["conv_w"] = conv_w[i]

    xin = jnp.concatenate([a["ctx"][0], a["x"][0]], axis=0)
    loss_l, grad_x, dmod, dsp = _local_step(xin, a["loss_target"][0], mod, comm, sp, rc, alpha)
    loss = lax.psum(loss_l, MESH_AXES)
    grads = {}

    def big_grad(nm):
        key = COL_SHARDED.get(nm, ROW_SHARDED.get(nm))
        per_layer = [comm.total(i, key) for i in range(n_layers)]
        return jnp.stack([t.T for t in per_layer] if nm in COL_SHARDED else per_layer)

    small_parts = [dmod[:, 0], dmod[:, 1]]
    small_shapes = [dmod[:, 0].shape, dmod[:, 1].shape]
    for nm in LAYER_SMALL + ("conv_w",):
        part = jnp.stack([dsp[i][nm] for i in range(n_layers)])
        small_parts.append(part)
        small_shapes.append(part.shape)
    got_lat, got_small = _exchange("gather_small", [_pack([dmod[:, 1]]), _pack(small_parts)], False)
    tot = _unpack(_sum8("sum_small", got_small), small_shapes)
    dmod_c, dmod_lat_sum = tot[0], tot[1]
    for nm, g in zip(LAYER_SMALL + ("conv_w",), tot[2:]):
        grads[nm] = g
    grads["conv_w"] = lax.dynamic_slice_in_dim(grads["conv_w"], me * cw_loc, cw_loc, axis=2)
    grads["b_ada"] = dmod_c + dmod_lat_sum
    dmod_lat_all = got_lat.reshape(N_DEV, -1)[:, :n_layers * 6 * d].reshape(N_DEV, n_layers, 6 * d)
    dm_rows = jnp.concatenate([dmod_lat_all.transpose(1, 0, 2), dmod_c[:, None, :],
                               jnp.zeros((n_layers, 16 - N_DEV - 1, 6 * d), F32)], axis=1)
    dm_cols = lax.dynamic_slice_in_dim(dm_rows, me * ada_w, ada_w, axis=2).astype(BF16)
    grads["w_ada"] = jnp.stack([_mm(f"dWada{i}", sil, dm_cols[i], "tn", F32) for i in range(n_layers)])
    dsil = None
    for i in range(n_layers):
        dsil = _mm(f"dsil{i}", dm_cols[i], a["w_ada"][i], "nt", F32, acc=dsil)
    got = _exchange("gather_dsil", [dsil], False)[0]
    dsil = _sum8("sum_dsil", got)
    grads["c_ctx"] = sil_vjp(dsil)[0][N_DEV]

    delta, new_m, new_v = {}, {}, {}
    order = ["w_ada", "w_ff_gate"] + [nm for nm in WEIGHT_NAMES
                                      if nm not in SMALL_ORDER + ("w_ada", "w_ff_gate", "w_in", "w_gate")] + ["w_in", "w_gate"]
    for nm in order:
        if nm != "w_ada":
            grads[nm] = big_grad(nm)
        shp = a[nm].shape
        view = (-1, shp[-1])
        name = f"adamw_{nm}"
        rider = comm.rider(name)
        res = _adamw(name, a[nm].reshape(view), grads[nm].reshape(view), a["m_" + nm].reshape(view),
                     a["v_" + nm].reshape(view), rider=rider)
        if rider is not None:
            res, r_outs = res
            comm.deliver(name, r_outs)
        delta[nm], new_m[nm], new_v[nm] = [t.reshape(shp) for t in res]
    shapes = [a[nm].shape for nm in SMALL_ORDER]
    res = _adamw("adamw_small", _pack([a[nm] for nm in SMALL_ORDER]), _pack([grads[nm] for nm in SMALL_ORDER]),
                 _pack([a["m_" + nm] for nm in SMALL_ORDER]), _pack([a["v_" + nm] for nm in SMALL_ORDER]))
    for tree, packed in zip((delta, new_m, new_v), res):
        for nm, t in zip(SMALL_ORDER, _unpack(packed, shapes)):
            tree[nm] = t
    return (loss, grad_x[None], *[grads[nm] for nm in WEIGHT_NAMES], *[delta[nm] for nm in WEIGHT_NAMES],
            *[new_m[nm] for nm in WEIGHT_NAMES], *[new_v[nm] for nm in WEIGHT_NAMES])


def kernel(x, c, ctx, c_ctx, w_ada, b_ada, w_in, q_norm_g, k_norm_g, pool_w, pool_scale, sgu_ln_g, sgu_ln_b, sgu_w, sgu_b, conv_w, w_br_attn, w_br_pool, w_br_sgu, w_br_conv, w_gate, b_gate, w_o, ln1_g, ln1_b, w_ff_gate, w_ff_up, w_ff_down, ln2_g, ln2_b, loss_target, m_c_ctx, m_w_ada, m_b_ada, m_w_in, m_q_norm_g, m_k_norm_g, m_pool_w, m_pool_scale, m_sgu_ln_g, m_sgu_ln_b, m_sgu_w, m_sgu_b, m_conv_w, m_w_br_attn, m_w_br_pool, m_w_br_sgu, m_w_br_conv, m_w_gate, m_b_gate, m_w_o, m_ln1_g, m_ln1_b, m_w_ff_gate, m_w_ff_up, m_w_ff_down, m_ln2_g, m_ln2_b, v_c_ctx, v_w_ada, v_b_ada, v_w_in, v_q_norm_g, v_k_norm_g, v_pool_w, v_pool_scale, v_sgu_ln_g, v_sgu_ln_b, v_sgu_w, v_sgu_b, v_conv_w, v_w_br_attn, v_w_br_pool, v_w_br_sgu, v_w_br_conv, v_w_gate, v_b_gate, v_w_o, v_ln1_g, v_ln1_b, v_w_ff_gate, v_w_ff_up, v_w_ff_down, v_ln2_g, v_ln2_b):
    names = list(WEIGHT_NAMES)
    args = dict(zip(
        ["x", "c", "ctx"] + names + ["loss_target"] + ["m_" + n for n in names] + ["v_" + n for n in names],
        (x, c, ctx, c_ctx, w_ada, b_ada, w_in, q_norm_g, k_norm_g, pool_w, pool_scale, sgu_ln_g, sgu_ln_b, sgu_w, sgu_b, conv_w, w_br_attn, w_br_pool, w_br_sgu, w_br_conv, w_gate, b_gate, w_o, ln1_g, ln1_b, w_ff_gate, w_ff_up, w_ff_down, ln2_g, ln2_b, loss_target, m_c_ctx, m_w_ada, m_b_ada, m_w_in, m_q_norm_g, m_k_norm_g, m_pool_w, m_pool_scale, m_sgu_ln_g, m_sgu_ln_b, m_sgu_w, m_sgu_b, m_conv_w, m_w_br_attn, m_w_br_pool, m_w_br_sgu, m_w_br_conv, m_w_gate, m_b_gate, m_w_o, m_ln1_g, m_ln1_b, m_w_ff_gate, m_w_ff_up, m_w_ff_down, m_ln2_g, m_ln2_b, v_c_ctx, v_w_ada, v_b_ada, v_w_in, v_q_norm_g, v_k_norm_g, v_pool_w, v_pool_scale, v_sgu_ln_g, v_sgu_ln_b, v_sgu_w, v_sgu_b, v_conv_w, v_w_br_attn, v_w_br_pool, v_w_br_sgu, v_w_br_conv, v_w_gate, v_b_gate, v_w_o, v_ln1_g, v_ln1_b, v_w_ff_gate, v_w_ff_up, v_w_ff_down, v_ln2_g, v_ln2_b)))
    return _train_step(args)
```

```python
import functools
import math

import jax
import jax.numpy as jnp
from jax import lax
from jax.experimental import pallas as pl
from jax.experimental.pallas import tpu as pltpu

F32 = jnp.float32
BF16 = jnp.bfloat16

N_DEV = 8
MESH_AXES = ("x", "y", "c")
V7X_VMEM_LIMIT_BYTES = 56 * 1024 * 1024

GRID_W = 64
HEAD_DIM = 128
N_HEADS = 8
N_KV_HEADS = 2
KV_GROUP = N_HEADS // N_KV_HEADS
Q_W = N_HEADS * HEAD_DIM
KV_W = N_KV_HEADS * HEAD_DIM
ROPE_THETA = 10000.0
ROPE_AXIS_DIM = HEAD_DIM // 2
POOL_WINDOWS = (2, 4, 8, 16)
GC = 128
N_GROUPS = 4
BR_W = N_GROUPS * GC
SGU_CHUNK = 128
N_BRANCH = 4
LN_EPS = 1e-5
RMS_EPS = 1e-6
OFF_K = Q_W
OFF_V = OFF_K + KV_W
OFF_POOL = OFF_V + KV_W
OFF_U = OFF_POOL + BR_W
OFF_VG = OFF_U + BR_W
OFF_CB = OFF_VG + BR_W
OFF_CC = OFF_CB + BR_W
OFF_CX = OFF_CC + BR_W
IN_W = OFF_CX + BR_W
ATT_SCALE = HEAD_DIM ** -0.5

ADAM_LR = 0.001
ADAM_B1 = 0.9
ADAM_B2 = 0.999
ADAM_EPS = 1e-08
ADAM_WD = 0.01
ADAM_STEP = 10

_NT = (((1,), (1,)), ((), ()))
_NN = (((1,), (0,)), ((), ()))
_TN = (((0,), (0,)), ((), ()))
_DIMS = {"nt": _NT, "nn": _NN, "tn": _TN}


def _cparams(*sem):
    return pltpu.CompilerParams(dimension_semantics=sem, vmem_limit_bytes=V7X_VMEM_LIMIT_BYTES)


def _tile(dim, pref):
    best = None
    t = 128
    while t <= min(dim, pref):
        if dim % t == 0:
            best = t
        t += 128
    return best if best is not None else dim


def _dot(a, b, dims):
    return lax.dot_general(a.astype(BF16), b.astype(BF16), dims, preferred_element_type=F32)


class _Rider:
    def __init__(self, inputs, out_shapes, scratch, start, finish, aliases=None):
        self.inputs, self.out_shapes, self.scratch = list(inputs), list(out_shapes), list(scratch)
        self.start, self.finish = start, finish
        self.aliases = dict(aliases or {})


def _compose(riders):
    inputs, outs, scratch, aliases, spans = [], [], [], {}, []
    for rd in riders:
        i0, o0, s0 = len(inputs), len(outs), len(scratch)
        aliases.update({i0 + p: o0 + q for p, q in rd.aliases.items()})
        inputs += rd.inputs
        outs += rd.out_shapes
        scratch += rd.scratch
        spans.append((slice(i0, len(inputs)), slice(o0, len(outs)), slice(s0, len(scratch))))

    def start(ins, os, sems):
        for rd, (si, so, ss) in zip(riders, spans):
            rd.start(ins[si], os[so], sems[ss])

    def finish(ins, os, sems):
        for rd, (si, so, ss) in zip(riders, spans):
            rd.finish(ins[si], os[so], sems[ss])

    return _Rider(inputs, outs, scratch, start, finish, aliases)


def _pcall(name, body, grid, in_specs, out_specs, out_shape, args, sem, scratch=(), rider=None):
    in_specs, out_specs, out_shape, scratch = list(in_specs), list(out_specs), list(out_shape), list(scratch)
    if rider is None:
        res = pl.pallas_call(body, name=name, grid=grid, in_specs=in_specs, out_specs=out_specs, out_shape=out_shape,
                             scratch_shapes=scratch, compiler_params=_cparams(*sem))(*args)
        return list(res), []
    n_in, n_out, n_scr = len(in_specs), len(out_specs), len(scratch)
    r_in, r_out = len(rider.inputs), len(rider.out_shapes)

    def wrapped(*refs):
        ins, refs = refs[:n_in], refs[n_in:]
        r_ins, refs = refs[:r_in], refs[r_in:]
        outs, refs = refs[:n_out], refs[n_out:]
        r_outs, refs = refs[:r_out], refs[r_out:]
        scr, r_scr = refs[:n_scr], refs[n_scr:]
        first = functools.reduce(jnp.logical_and, [pl.program_id(ax) == 0 for ax in range(len(grid))])
        last = functools.reduce(jnp.logical_and, [pl.program_id(ax) == grid[ax] - 1 for ax in range(len(grid))])

        @pl.when(first)
        def _():
            rider.start(r_ins, r_outs, r_scr)

        body(*ins, *outs, *scr)

        @pl.when(last)
        def _():
            rider.finish(r_ins, r_outs, r_scr)

    any_spec = pl.BlockSpec(memory_space=pl.ANY)
    res = pl.pallas_call(
        wrapped, name=name, grid=grid,
        in_specs=in_specs + [any_spec] * r_in, out_specs=out_specs + [any_spec] * r_out,
        out_shape=out_shape + rider.out_shapes, scratch_shapes=scratch + rider.scratch,
        input_output_aliases={n_in + p: n_out + q for p, q in rider.aliases.items()},
        compiler_params=_cparams(*(["arbitrary"] * len(grid))),
    )(*args, *rider.inputs)
    return list(res[:n_out]), list(res[n_out:])


V7X_MM_VMEM_BUDGET = 40 * 1024 * 1024


def _mm_plan(form, m, n, k, a_size, b_size, o_size, has_acc):
    tk = k if k <= 2816 else _tile(k, 2816)
    nk = k // tk
    tn = n if (form == "tn" and n <= 2048) else _tile(n, 512)
    for tm in sorted({m} | {t for t in range(128, m, 128) if m % t == 0}, reverse=True):
        need = 2 * (tm * tk * a_size + tn * tk * b_size + tm * tn * o_size) + tm * tn * 4 * (2 if nk > 1 else 1)
        need += 2 * tm * tn * 4 if has_acc else 0
        if need <= V7X_MM_VMEM_BUDGET:
            return tm, tn, tk
    return _tile(m, 128), tn, tk


def _mm(name, a, b, form, out_dtype, acc=None, rider=None):
    if form == "nt":
        (m, k), (n, k2) = a.shape, b.shape
    elif form == "nn":
        (m, k), (k2, n) = a.shape, b.shape
    else:
        (k, m), (k2, n) = a.shape, b.shape
    assert k == k2, (name, a.shape, b.shape)
    has_acc = acc is not None
    tm, tn, tk = _mm_plan(form, m, n, k, a.dtype.itemsize, b.dtype.itemsize, jnp.dtype(out_dtype).itemsize, has_acc)
    nk = k // tk
    a_spec = {"nt": pl.BlockSpec((tm, tk), lambda i, j, kk: (i, kk)),
              "nn": pl.BlockSpec((tm, tk), lambda i, j, kk: (i, kk)),
              "tn": pl.BlockSpec((tk, tm), lambda i, j, kk: (kk, i))}[form]
    b_spec = {"nt": pl.BlockSpec((tn, tk), lambda i, j, kk: (j, kk)),
              "nn": pl.BlockSpec((tk, tn), lambda i, j, kk: (kk, j)),
              "tn": pl.BlockSpec((tk, tn), lambda i, j, kk: (kk, j))}[form]
    o_spec = pl.BlockSpec((tm, tn), lambda i, j, kk: (i, j))
    dims = _DIMS[form]

    def body(*refs):
        a_ref, b_ref = refs[0], refs[1]
        c_ref = refs[2] if has_acc else None
        o_ref = refs[3] if has_acc else refs[2]

        def finish(r):
            if has_acc:
                r = r + c_ref[...]
            o_ref[...] = r.astype(o_ref.dtype)

        if nk == 1:
            finish(_dot(a_ref[...], b_ref[...], dims))
            return
        acc_ref = refs[-1]
        kk = pl.program_id(2)

        @pl.when(kk == 0)
        def _():
            acc_ref[...] = _dot(a_ref[...], b_ref[...], dims)

        @pl.when(kk > 0)
        def _():
            acc_ref[...] += _dot(a_ref[...], b_ref[...], dims)

        @pl.when(kk == nk - 1)
        def _():
            finish(acc_ref[...])

    in_specs = [a_spec, b_spec] + ([o_spec] if has_acc else [])
    args = (a, b) + ((acc,) if has_acc else ())
    outs, r_outs = _pcall(name, body, (m // tm, n // tn, nk), in_specs, [o_spec],
                          [jax.ShapeDtypeStruct((m, n), out_dtype)], args, ("parallel", "parallel", "arbitrary"),
                          scratch=[pltpu.VMEM((tm, tn), F32)] if nk > 1 else [], rider=rider)
    return outs[0] if rider is None else (outs[0], r_outs)


def _rows(name, fn, n_rows, tm, nbc, row_ins, type_ins, row_outs, acc_outs, rider=None):
    n_ri, n_ti, n_ro, n_ao = len(row_ins), len(type_ins), len(row_outs), len(acc_outs)

    def row_map(i, cb, roff):
        return (jnp.maximum(i - roff, 0), cb)

    def type_map(i):
        return (jnp.where(i >= nbc, 1, 0), 0, 0)

    in_specs, args = [], []
    for arr, cb, width, roff in row_ins:
        in_specs.append(pl.BlockSpec((tm, width), functools.partial(row_map, cb=cb, roff=roff)))
        args.append(arr)
    for arr in type_ins:
        in_specs.append(pl.BlockSpec((None, 1, arr.shape[-1]), type_map))
        args.append(arr)
    out_shape, out_specs = [], []
    for total, width, dtype, roff in row_outs:
        out_shape.append(jax.ShapeDtypeStruct((total, width), dtype))
        out_specs.append(pl.BlockSpec((tm, width), functools.partial(row_map, cb=0, roff=roff)))
    for width in acc_outs:
        out_shape.append(jax.ShapeDtypeStruct((2, 1, width), F32))
        out_specs.append(pl.BlockSpec((None, 1, width), type_map))
    n_in = n_ri + n_ti

    def body(*refs):
        i = pl.program_id(0)
        outs = fn(*[r[...] for r in refs[:n_in]])
        if not isinstance(outs, (tuple, list)):
            outs = (outs,)
        assert len(outs) == n_ro + n_ao, (name, len(outs))
        for r, o in zip(refs[n_in:n_in + n_ro], outs[:n_ro]):
            r[...] = o.astype(r.dtype)
        if n_ao:
            first = jnp.logical_or(i == 0, i == nbc)
            for r, o in zip(refs[n_in + n_ro:], outs[n_ro:]):
                o = jnp.broadcast_to(o.astype(F32), r.shape)

                @pl.when(first)
                def _(r=r, o=o):
                    r[...] = o

                @pl.when(jnp.logical_not(first))
                def _(r=r, o=o):
                    r[...] += o

    outs, r_outs = _pcall(name, body, (n_rows // tm,), in_specs, out_specs, out_shape, args, ("arbitrary",),
                          rider=rider)
    return outs if rider is None else (outs, r_outs)


def _vjp_fn(f, n_row, n_cot, keep=None):
    def g(*args):
        prim = args[:n_row] + args[n_row + n_cot:]
        cots = args[n_row:n_row + n_cot]
        out, vjp = jax.vjp(f, *prim)
        grads = vjp(tuple(cots) if isinstance(out, (tuple, list)) else cots[0])
        return grads if keep is None else tuple(grads[j] for j in keep)
    return g


def _typed(v):
    v = v.reshape(1, 1, -1)
    return jnp.concatenate([v, v], axis=0)


def _ln(x, g, b):
    mu = jnp.mean(x, axis=-1, keepdims=True)
    var = jnp.mean(jnp.square(x - mu), axis=-1, keepdims=True)
    return (x - mu) * lax.rsqrt(var + LN_EPS) * g + b


def _f_mod(x, sc, sh):
    return x * (1.0 + sc) + sh


def _make_f_ln(alpha, with_mod):
    def f(x, o, gate, lng, lnb, *mod):
        xn = _ln(alpha * x + gate * o, lng, lnb)
        if with_mod:
            sc, sh = mod
            return xn, xn * (1.0 + sc) + sh
        return xn
    return f


@jax.custom_vjp
def _rot(y):
    lane = lax.broadcasted_iota(jnp.int32, y.shape, 1)
    return jnp.where(lane % 64 < 32, pltpu.roll(y, 96, axis=1), pltpu.roll(y, 32, axis=1))


_rot.defvjp(lambda y: (_rot(y), None), lambda _, g: (_rot(g),))


def _f_prep(p, cos, sin, qg, kg):
    def head(xh, g):
        ms = jnp.mean(jnp.square(xh), axis=-1, keepdims=True)
        y = xh * lax.rsqrt(ms + RMS_EPS) * g
        return y * cos + _rot(y) * sin
    q = jnp.concatenate([head(p[:, h * HEAD_DIM:(h + 1) * HEAD_DIM], qg) for h in range(N_HEADS)], axis=1)
    k = jnp.concatenate([head(p[:, OFF_K + h * HEAD_DIM:OFF_K + (h + 1) * HEAD_DIM], kg)
                         for h in range(N_KV_HEADS)], axis=1)
    return q, k, p[:, OFF_V:OFF_POOL]


def _f_gate(g, t0, t1, t2, t3, b):
    d = t0.shape[-1]
    ts = (t0, t1, t2, t3)
    terms = [jax.nn.sigmoid(g[:, k * d:(k + 1) * d] + b[:, k * d:(k + 1) * d]) * ts[k] for k in range(N_BRANCH)]
    return terms[0] + terms[1] + terms[2] + terms[3]


def _f_swiglu(a, b):
    return jax.nn.silu(a) * b


def _softmax(s):
    e = jnp.exp(s - jnp.max(s, axis=-1, keepdims=True))
    return e / jnp.sum(e, axis=-1, keepdims=True)


def _attn_fwd(name, q, k, v, rc, ctx_queries, tq=256, rider=None):
    r = q.shape[0]
    assert rc % tq == 0 and r % tq == 0
    nqc = rc // tq

    def body(q_ref, k_ref, v_ref, o_ref):
        qi = pl.program_id(1)

        def attend(nk):
            s = _dot(q_ref[...], k_ref[0:nk, :], _NT) * ATT_SCALE
            p = _softmax(s)
            o_ref[...] = _dot(p, v_ref[0:nk, :], _NN).astype(o_ref.dtype)

        @pl.when(qi < nqc)
        def _():
            if ctx_queries:
                attend(rc)
            else:
                o_ref[...] = jnp.zeros_like(o_ref)

        @pl.when(qi >= nqc)
        def _():
            attend(r)

    outs, r_outs = _pcall(
        name, body, (N_HEADS, r // tq),
        [pl.BlockSpec((tq, HEAD_DIM), lambda h, i: (i, h)),
         pl.BlockSpec((r, HEAD_DIM), lambda h, i: (0, h // KV_GROUP)),
         pl.BlockSpec((r, HEAD_DIM), lambda h, i: (0, h // KV_GROUP))],
        [pl.BlockSpec((tq, HEAD_DIM), lambda h, i: (i, h))],
        [jax.ShapeDtypeStruct((r, Q_W), BF16)], (q, k, v), ("parallel", "parallel"), rider=rider)
    return outs[0] if rider is None else (outs[0], r_outs)


def _attn_bwd(name, q, k, v, do, rc, ctx_queries, tq=256, rider=None):
    r = q.shape[0]
    nqc = rc // tq

    def body(q_ref, k_ref, v_ref, do_ref, dq_ref, dk_ref, dv_ref):
        g, qi = pl.program_id(1), pl.program_id(2)

        @pl.when(jnp.logical_and(g == 0, qi == 0))
        def _():
            dk_ref[...] = jnp.zeros_like(dk_ref)
            dv_ref[...] = jnp.zeros_like(dv_ref)

        def grad(nk):
            qb, kb, vb = q_ref[...], k_ref[0:nk, :], v_ref[0:nk, :]
            dob = do_ref[...].astype(BF16)
            p = _softmax(_dot(qb, kb, _NT) * ATT_SCALE)
            dv_ref[0:nk, :] += _dot(p, dob, _TN)
            dp = _dot(dob, vb, _NT)
            ds = p * (dp - jnp.sum(dp * p, axis=-1, keepdims=True)) * ATT_SCALE
            dq_ref[...] = _dot(ds, kb, _NN)
            dk_ref[0:nk, :] += _dot(ds, qb, _TN)

        @pl.when(qi < nqc)
        def _():
            if ctx_queries:
                grad(rc)
            else:
                dq_ref[...] = jnp.zeros_like(dq_ref)

        @pl.when(qi >= nqc)
        def _():
            grad(r)

    def qmap(kv, g, i):
        return (i, kv * KV_GROUP + g)

    def kvmap(kv, g, i):
        return (0, kv)

    outs, r_outs = _pcall(
        name, body, (N_KV_HEADS, KV_GROUP, r // tq),
        [pl.BlockSpec((tq, HEAD_DIM), qmap), pl.BlockSpec((r, HEAD_DIM), kvmap),
         pl.BlockSpec((r, HEAD_DIM), kvmap), pl.BlockSpec((tq, HEAD_DIM), qmap)],
        [pl.BlockSpec((tq, HEAD_DIM), qmap), pl.BlockSpec((r, HEAD_DIM), kvmap), pl.BlockSpec((r, HEAD_DIM), kvmap)],
        [jax.ShapeDtypeStruct((r, Q_W), F32), jax.ShapeDtypeStruct((r, KV_W), F32),
         jax.ShapeDtypeStruct((r, KV_W), F32)],
        (q, k, v, do), ("arbitrary", "arbitrary", "arbitrary"), rider=rider)
    return outs if rider is None else (outs, r_outs)


def _segments(shape, rc):
    t = lax.broadcasted_iota(jnp.int32, shape, 0)
    lo = jnp.where(t < rc, 0, rc)
    hi = jnp.where(t < rc, rc, shape[0])
    return t, lo, hi


def _shifted(x, o, t, lo, hi):
    n = x.shape[0]
    sh = pltpu.roll(x, (-o) % n, axis=0)
    return jnp.where(jnp.logical_and(t + o >= lo, t + o < hi), sh, 0.0)


def _winsum(x, left, right, t, lo, hi):
    acc = x
    for o in range(-left, right + 1):
        if o != 0:
            acc = acc + _shifted(x, o, t, lo, hi)
    return acc


def _pool_parts(z, g, t, lo, hi):
    w = POOL_WINDOWS[g]
    left = w // 2
    right = w - 1 - left
    count = (jnp.minimum(t + right + 1, hi) - jnp.maximum(t - left, lo)).astype(F32)
    return _winsum(z, left, right, t, lo, hi) / count - z, count, left, right


def _pool_fwd(name, p, pool_w, pool_scale, rc):
    r = p.shape[0]

    def body(z_ref, w_ref, s_ref, y_ref):
        t, lo, hi = _segments((r, GC), rc)
        for g in range(N_GROUPS):
            cols = slice(g * GC, (g + 1) * GC)
            d, _, _, _ = _pool_parts(z_ref[:, cols], g, t, lo, hi)
            y_ref[:, cols] = (_dot(d, w_ref[g], _NN) * s_ref[:, cols]).astype(y_ref.dtype)

    return pl.pallas_call(
        body, name=name, grid=(1,),
        in_specs=[pl.BlockSpec((r, BR_W), lambda i: (0, OFF_POOL // BR_W)),
                  pl.BlockSpec((N_GROUPS, GC, GC), lambda i: (0, 0, 0)),
                  pl.BlockSpec((1, BR_W), lambda i: (0, 0))],
        out_specs=pl.BlockSpec((r, BR_W), lambda i: (0, 0)),
        out_shape=jax.ShapeDtypeStruct((r, BR_W), BF16),
        compiler_params=_cparams("arbitrary"),
    )(p, pool_w, pool_scale.reshape(1, BR_W))


def _pool_bwd(name, p, pool_w, pool_scale, dy, rc):
    r = p.shape[0]

    def body(z_ref, w_ref, s_ref, dy_ref, dz_ref, dw_ref, ds_ref):
        t, lo, hi = _segments((r, GC), rc)
        for g in range(N_GROUPS):
            cols = slice(g * GC, (g + 1) * GC)
            d, count, left, right = _pool_parts(z_ref[:, cols], g, t, lo, hi)
            dyg = dy_ref[:, cols]
            ds_ref[:, cols] = jnp.sum(dyg * _dot(d, w_ref[g], _NN), axis=0, keepdims=True)
            dlin = dyg * s_ref[:, cols]
            dw_ref[g] = _dot(d, dlin, _TN)
            dd = _dot(dlin, w_ref[g], _NT)
            dz_ref[:, cols] = (_winsum(dd / count, right, left, t, lo, hi) - dd).astype(dz_ref.dtype)

    return pl.pallas_call(
        body, name=name, grid=(1,),
        in_specs=[pl.BlockSpec((r, BR_W), lambda i: (0, OFF_POOL // BR_W)),
                  pl.BlockSpec((N_GROUPS, GC, GC), lambda i: (0, 0, 0)),
                  pl.BlockSpec((1, BR_W), lambda i: (0, 0)),
                  pl.BlockSpec((r, BR_W), lambda i: (0, 0))],
        out_specs=[pl.BlockSpec((r, BR_W), lambda i: (0, 0)),
                   pl.BlockSpec((N_GROUPS, GC, GC), lambda i: (0, 0, 0)),
                   pl.BlockSpec((1, BR_W), lambda i: (0, 0))],
        out_shape=[jax.ShapeDtypeStruct((r, BR_W), BF16), jax.ShapeDtypeStruct((N_GROUPS, GC, GC), F32),
                   jax.ShapeDtypeStruct((1, BR_W), F32)],
        compiler_params=_cparams("arbitrary"),
    )(p, pool_w, pool_scale.reshape(1, BR_W), dy)


def _f_sgu_v(pvg, lng, lnb):
    return _ln(jax.nn.gelu(pvg), lng, lnb)


def _sgu_fwd(name, p, ln_g, ln_b, sgu_w, sgu_b):
    r = p.shape[0]

    def body(pu_ref, pv_ref, g_ref, b_ref, w_ref, sb_ref, y_ref):
        vn = _f_sgu_v(pv_ref[...], g_ref[...], b_ref[...])
        u = jax.nn.gelu(pu_ref[...])
        for g in range(N_GROUPS):
            cols = slice(g * GC, (g + 1) * GC)
            s = _dot(w_ref[g], vn[:, cols], _NN) + sb_ref[g]
            y_ref[:, cols] = (u[:, cols] * s).astype(y_ref.dtype)

    return pl.pallas_call(
        body, name=name, grid=(r // SGU_CHUNK,),
        in_specs=[pl.BlockSpec((SGU_CHUNK, BR_W), lambda i: (i, OFF_U // BR_W)),
                  pl.BlockSpec((SGU_CHUNK, BR_W), lambda i: (i, OFF_VG // BR_W)),
                  pl.BlockSpec((1, BR_W), lambda i: (0, 0)), pl.BlockSpec((1, BR_W), lambda i: (0, 0)),
                  pl.BlockSpec((N_GROUPS, GC, GC), lambda i: (0, 0, 0)),
                  pl.BlockSpec((N_GROUPS, SGU_CHUNK, 1), lambda i: (0, 0, 0))],
        out_specs=pl.BlockSpec((SGU_CHUNK, BR_W), lambda i: (i, 0)),
        out_shape=jax.ShapeDtypeStruct((r, BR_W), BF16),
        compiler_params=_cparams("parallel"),
    )(p, p, ln_g.reshape(1, BR_W), ln_b.reshape(1, BR_W), sgu_w, sgu_b.reshape(N_GROUPS, SGU_CHUNK, 1))


def _sgu_bwd(name, p, ln_g, ln_b, sgu_w, sgu_b, dy):
    r = p.shape[0]

    def body(pu_ref, pv_ref, g_ref, b_ref, w_ref, sb_ref, dy_ref, dp_ref, dg_ref, db_ref, dw_ref, dsb_ref):
        i = pl.program_id(0)

        @pl.when(i == 0)
        def _():
            for ref in (dg_ref, db_ref, dw_ref, dsb_ref):
                ref[...] = jnp.zeros_like(ref)

        vn, vjp_v = jax.vjp(_f_sgu_v, pv_ref[...], g_ref[...], b_ref[...])
        u, vjp_u = jax.vjp(jax.nn.gelu, pu_ref[...])
        dy = dy_ref[...]
        du, dvn = [], []
        for g in range(N_GROUPS):
            cols = slice(g * GC, (g + 1) * GC)
            s = _dot(w_ref[g], vn[:, cols], _NN) + sb_ref[g]
            du.append(dy[:, cols] * s)
            ds = dy[:, cols] * u[:, cols]
            dsb_ref[g] += jnp.sum(ds, axis=1, keepdims=True)
            dw_ref[g] += _dot(ds, vn[:, cols], _NT)
            dvn.append(_dot(w_ref[g], ds, _TN))
        (dpu,) = vjp_u(jnp.concatenate(du, axis=1))
        dpv, dg, db = vjp_v(jnp.concatenate(dvn, axis=1))
        dp_ref[:, 0:BR_W] = dpu.astype(dp_ref.dtype)
        dp_ref[:, BR_W:2 * BR_W] = dpv.astype(dp_ref.dtype)
        dg_ref[...] += dg
        db_ref[...] += db

    vec = pl.BlockSpec((1, BR_W), lambda i: (0, 0))
    wsp = pl.BlockSpec((N_GROUPS, GC, GC), lambda i: (0, 0, 0))
    bsp = pl.BlockSpec((N_GROUPS, SGU_CHUNK, 1), lambda i: (0, 0, 0))
    return pl.pallas_call(
        body, name=name, grid=(r // SGU_CHUNK,),
        in_specs=[pl.BlockSpec((SGU_CHUNK, BR_W), lambda i: (i, OFF_U // BR_W)),
                  pl.BlockSpec((SGU_CHUNK, BR_W), lambda i: (i, OFF_VG // BR_W)),
                  vec, vec, wsp, bsp, pl.BlockSpec((SGU_CHUNK, BR_W), lambda i: (i, 0))],
        out_specs=[pl.BlockSpec((SGU_CHUNK, 2 * BR_W), lambda i: (i, 0)), vec, vec, wsp, bsp],
        out_shape=[jax.ShapeDtypeStruct((r, 2 * BR_W), BF16), jax.ShapeDtypeStruct((1, BR_W), F32),
                   jax.ShapeDtypeStruct((1, BR_W), F32), jax.ShapeDtypeStruct((N_GROUPS, GC, GC), F32),
                   jax.ShapeDtypeStruct((N_GROUPS, SGU_CHUNK, 1), F32)],
        compiler_params=_cparams("arbitrary"),
    )(p, p, ln_g.reshape(1, BR_W), ln_b.reshape(1, BR_W), sgu_w, sgu_b.reshape(N_GROUPS, SGU_CHUNK, 1), dy)


def _conv_w8(conv_w):
    return jnp.concatenate([conv_w, jnp.zeros((8 - conv_w.shape[0], conv_w.shape[1]), F32)], axis=0)


def _conv_fwd(name, p, conv_w, rc):
    r = p.shape[0]

    def body(cb_ref, cc_ref, cx_ref, w_ref, y_ref):
        t, lo, hi = _segments((r, GC), rc)
        z = cc_ref[...] * cx_ref[...]
        w = w_ref[...]
        c = _shifted(z, -1, t, lo, hi) * w[0:1] + z * w[1:2] + _shifted(z, 1, t, lo, hi) * w[2:3]
        y_ref[...] = (cb_ref[...] * c).astype(y_ref.dtype)

    nb = OFF_CB // GC
    return pl.pallas_call(
        body, name=name, grid=(N_GROUPS,),
        in_specs=[pl.BlockSpec((r, GC), lambda j: (0, nb + j)),
                  pl.BlockSpec((r, GC), lambda j: (0, nb + N_GROUPS + j)),
                  pl.BlockSpec((r, GC), lambda j: (0, nb + 2 * N_GROUPS + j)),
                  pl.BlockSpec((8, GC), lambda j: (0, j))],
        out_specs=pl.BlockSpec((r, GC), lambda j: (0, j)),
        out_shape=jax.ShapeDtypeStruct((r, BR_W), BF16),
        compiler_params=_cparams("parallel"),
    )(p, p, p, _conv_w8(conv_w))


def _conv_bwd(name, p, conv_w, dy, rc):
    r = p.shape[0]

    def body(cb_ref, cc_ref, cx_ref, w_ref, dy_ref, dcb_ref, dcc_ref, dcx_ref, dw_ref):
        t, lo, hi = _segments((r, GC), rc)
        cc, cx, w, dy = cc_ref[...], cx_ref[...], w_ref[...], dy_ref[...]
        z = cc * cx
        zp, zn = _shifted(z, -1, t, lo, hi), _shifted(z, 1, t, lo, hi)
        dcb_ref[...] = (dy * (zp * w[0:1] + z * w[1:2] + zn * w[2:3])).astype(dcb_ref.dtype)
        dc = dy * cb_ref[...]
        dw_ref[...] = jnp.concatenate(
            [jnp.sum(dc * zp, axis=0, keepdims=True), jnp.sum(dc * z, axis=0, keepdims=True),
             jnp.sum(dc * zn, axis=0, keepdims=True), jnp.zeros((5, GC), F32)], axis=0)
        dz = dc * w[1:2] + _shifted(dc, 1, t, lo, hi) * w[0:1] + _shifted(dc, -1, t, lo, hi) * w[2:3]
        dcc_ref[...] = (dz * cx).astype(dcc_ref.dtype)
        dcx_ref[...] = (dz * cc).astype(dcx_ref.dtype)

    nb = OFF_CB // GC
    return pl.pallas_call(
        body, name=name, grid=(N_GROUPS,),
        in_specs=[pl.BlockSpec((r, GC), lambda j: (0, nb + j)),
                  pl.BlockSpec((r, GC), lambda j: (0, nb + N_GROUPS + j)),
                  pl.BlockSpec((r, GC), lambda j: (0, nb + 2 * N_GROUPS + j)),
                  pl.BlockSpec((8, GC), lambda j: (0, j)),
                  pl.BlockSpec((r, GC), lambda j: (0, j))],
        out_specs=[pl.BlockSpec((r, GC), lambda j: (0, j))] * 3 + [pl.BlockSpec((8, GC), lambda j: (0, j))],
        out_shape=[jax.ShapeDtypeStruct((r, BR_W), BF16)] * 3 + [jax.ShapeDtypeStruct((8, BR_W), F32)],
        compiler_params=_cparams("parallel"),
    )(p, p, p, _conv_w8(conv_w), dy)


def _rope_tables(rc, n):
    rows = n // GRID_W
    row = jnp.repeat(jnp.arange(rows), GRID_W).astype(F32)
    col = jnp.tile(jnp.arange(GRID_W), rows).astype(F32)
    inv = ROPE_THETA ** (-jnp.arange(0, ROPE_AXIS_DIM, 2, dtype=F32) / ROPE_AXIS_DIM)
    ang_r, ang_c = row[:, None] * inv, col[:, None] * inv
    cos = jnp.concatenate([jnp.cos(ang_r), jnp.cos(ang_r), jnp.cos(ang_c), jnp.cos(ang_c)], axis=1)
    sin = jnp.concatenate([-jnp.sin(ang_r), jnp.sin(ang_r), -jnp.sin(ang_c), jnp.sin(ang_c)], axis=1)
    cos = jnp.concatenate([jnp.ones((rc, HEAD_DIM), F32), cos], axis=0)
    sin = jnp.concatenate([jnp.zeros((rc, HEAD_DIM), F32), sin], axis=0)
    return cos, sin


MOD_NAMES = ("sh1", "sc1", "g1", "sh2", "sc2", "g2")


def _local_step(xin, target, mod, comm, sp, rc, alpha):
    def carrying(fn):
        def call(name, *args, **kw):
            rider = comm.rider(name)
            if rider is None:
                return fn(name, *args, **kw)
            res, r_outs = fn(name, *args, rider=rider, **kw)
            comm.deliver(name, r_outs)
            return res
        return call

    mm, rows, attn_fwd, attn_bwd = carrying(_mm), carrying(_rows), carrying(_attn_fwd), carrying(_attn_bwd)
    r, d = xin.shape
    n_layers = mod.shape[0]
    tm_n, tm_w = 256, 128
    nbc_n, nbc_w = rc // tm_n, rc // tm_w
    cos, sin = _rope_tables(rc, r - rc)
    mp = mod.reshape(n_layers, 2, 6, 1, d)
    mods = [{nm: mp[i, :, j] for j, nm in enumerate(MOD_NAMES)} for i in range(n_layers)]
    f_ln_mod, f_ln_last = _make_f_ln(alpha, True), _make_f_ln(alpha, False)

    def whole(arr, roff=0):
        return (arr, 0, arr.shape[1], roff)

    (hb,) = rows("mod_in", _f_mod, r, tm_n, nbc_n, [whole(xin)], [mods[0]["sc1"], mods[0]["sh1"]],
                 [(r, d, BF16, 0)], [])
    saved = []
    x = xin
    for i in range(n_layers):
        last = i == n_layers - 1
        w, s, m = functools.partial(comm.weight, i), sp[i], mods[i]
        sv = {"x": x, "hb": hb}
        p = mm(f"l{i}_in", hb, w("in_t"), "nt", F32)
        gpre = mm(f"l{i}_gate", hb, w("gate_t"), "nt", F32)
        q, k, v = rows(f"l{i}_prep", _f_prep, r, tm_n, nbc_n,
                       [(p, 0, OFF_POOL, 0), whole(cos), whole(sin)], [_typed(s["q_norm_g"]), _typed(s["k_norm_g"])],
                       [(r, Q_W, BF16, 0), (r, KV_W, BF16, 0), (r, KV_W, BF16, 0)], [])
        ys = [attn_fwd(f"l{i}_attn", q, k, v, rc, not last),
              _pool_fwd(f"l{i}_pool", p, s["pool_w"], s["pool_scale"], rc),
              _sgu_fwd(f"l{i}_sgu", p, s["sgu_ln_g"], s["sgu_ln_b"], s["sgu_w"], s["sgu_b"]),
              _conv_fwd(f"l{i}_conv", p, s["conv_w"], rc)]
        ts = [mm(f"l{i}_br{kk}", ys[kk], w(f"br{kk}"), "nt", F32) for kk in range(N_BRANCH)]
        (mg,) = rows(f"l{i}_merge", _f_gate, r, tm_w, nbc_w, [whole(gpre)] + [whole(t) for t in ts],
                     [_typed(s["b_gate"])], [(r, d, BF16, 0)], [])
        o = mm(f"l{i}_o", mg, w("o"), "nn", F32)
        x1, h2b = rows(f"l{i}_ln1", f_ln_mod, r, tm_n, nbc_n, [whole(x), whole(o)],
                       [m["g1"], _typed(s["ln1_g"]), _typed(s["ln1_b"]), m["sc2"], m["sh2"]],
                       [(r, d, F32, 0), (r, d, BF16, 0)], [])
        af = mm(f"l{i}_ffg", h2b, w("ffg_t"), "nt", F32)
        bf = mm(f"l{i}_ffu", h2b, w("ffu_t"), "nt", F32)
        (f,) = rows(f"l{i}_swiglu", _f_swiglu, r, tm_w, nbc_w, [whole(af), whole(bf)], [],
                    [(r, af.shape[1], BF16, 0)], [])
        o2 = mm(f"l{i}_ffd", f, w("ffd"), "nn", F32)
        if last:
            (x2,) = rows(f"l{i}_ln2", f_ln_last, r, tm_n, nbc_n, [whole(x1), whole(o2)],
                         [m["g2"], _typed(s["ln2_g"]), _typed(s["ln2_b"])], [(r, d, F32, 0)], [])
            hb = None
        else:
            nx = mods[i + 1]
            x2, hb = rows(f"l{i}_ln2", f_ln_mod, r, tm_n, nbc_n, [whole(x1), whole(o2)],
                          [m["g2"], _typed(s["ln2_g"]), _typed(s["ln2_b"]), nx["sc1"], nx["sh1"]],
                          [(r, d, F32, 0), (r, d, BF16, 0)], [])
        sv.update(p=p, gpre=gpre, q=q, k=k, v=v, ys=ys, ts=ts, mg=mg, o=o, x1=x1, h2b=h2b, af=af, bf=bf, f=f, o2=o2)
        saved.append(sv)
        x = x2

    lat = jnp.concatenate([jnp.zeros((1, 1, 128), F32), jnp.ones((1, 1, 128), F32)], axis=0)

    def f_loss(xb, tb, msk):
        diff = (xb - tb) * msk[:, 0:1]
        part = jnp.sum(jnp.mean(jnp.square(diff), axis=-1, keepdims=True), axis=0, keepdims=True)
        return diff * (1.0 / d), jnp.broadcast_to(part, (1, 128))

    dx_direct, loss_acc = rows("loss", f_loss, r, tm_n, nbc_n, [whole(x), whole(target, nbc_n)], [lat],
                               [(r, d, F32, 0)], [128])
    loss = 0.5 * loss_acc[1, 0, 0]

    dmods = [dict() for _ in range(n_layers)]
    dsp = [dict() for _ in range(n_layers)]
    dh = None
    for i in reversed(range(n_layers)):
        last = i == n_layers - 1
        w, s, m, sv = functools.partial(comm.weight, i), sp[i], mods[i], saved[i]
        dm, dw, ds = dmods[i], {}, dsp[i]
        ln2 = [m["g2"], _typed(s["ln2_g"]), _typed(s["ln2_b"])]
        if last:
            res = rows(f"l{i}_ln2_bwd", _vjp_fn(f_ln_last, 2, 1), r, tm_n, nbc_n,
                       [whole(sv["x1"]), whole(sv["o2"]), whole(dx_direct)], ln2,
                       [(r, d, F32, 0), (r, d, BF16, 0)], [d, d, d])
            dx1, do2, dm["g2"], dlg, dlb = res
        else:
            nx = mods[i + 1]
            res = rows(f"l{i}_ln2_bwd", _vjp_fn(f_ln_mod, 2, 2), r, tm_n, nbc_n,
                       [whole(sv["x1"]), whole(sv["o2"]), whole(dx_direct), whole(dh)],
                       ln2 + [nx["sc1"], nx["sh1"]],
                       [(r, d, F32, 0), (r, d, BF16, 0)], [d, d, d, d, d])
            dx1, do2, dm["g2"], dlg, dlb, dmods[i + 1]["sc1"], dmods[i + 1]["sh1"] = res
        ds["ln2_g"], ds["ln2_b"] = dlg, dlb
        df = mm(f"l{i}_dF", do2, w("ffd"), "nt", F32)
        comm.grads(i, {"ffd": mm(f"l{i}_dWffd", sv["f"], do2, "tn", BF16)})
        dab, dbb = rows(f"l{i}_swiglu_bwd", _vjp_fn(_f_swiglu, 2, 1), r, tm_w, nbc_w,
                        [whole(sv["af"]), whole(sv["bf"]), whole(df)], [],
                        [(r, df.shape[1], BF16, 0), (r, df.shape[1], BF16, 0)], [])
        comm.grads(i, {"ffg_t": mm(f"l{i}_dWffg", dab, sv["h2b"], "tn", BF16)})
        comm.grads(i, {"ffu_t": mm(f"l{i}_dWffu", dbb, sv["h2b"], "tn", BF16)})
        dh2 = mm(f"l{i}_dh2a", dab, w("ffg_t"), "nn", F32)
        dh2 = mm(f"l{i}_dh2b", dbb, w("ffu_t"), "nn", F32, acc=dh2)
        res = rows(f"l{i}_ln1_bwd", _vjp_fn(f_ln_mod, 2, 2), r, tm_n, nbc_n,
                   [whole(sv["x"]), whole(sv["o"]), whole(dx1), whole(dh2)],
                   [m["g1"], _typed(s["ln1_g"]), _typed(s["ln1_b"]), m["sc2"], m["sh2"]],
                   [(r, d, F32, 0), (r, d, BF16, 0)], [d, d, d, d, d])
        dx_direct, do, dm["g1"], ds["ln1_g"], ds["ln1_b"], dm["sc2"], dm["sh2"] = res
        dmg = mm(f"l{i}_dMg", do, w("o"), "nt", F32)
        comm.grads(i, {"o": mm(f"l{i}_dWo", sv["mg"], do, "tn", BF16)})
        res = rows(f"l{i}_merge_bwd", _vjp_fn(_f_gate, 5, 1), r, tm_w, nbc_w,
                   [whole(sv["gpre"])] + [whole(t) for t in sv["ts"]] + [whole(dmg)], [_typed(s["b_gate"])],
                   [(r, N_BRANCH * d, BF16, 0)] + [(r, d, BF16, 0)] * N_BRANCH, [N_BRANCH * d])
        dgb, dts, ds["b_gate"] = res[0], res[1:1 + N_BRANCH], res[1 + N_BRANCH]
        dys = [mm(f"l{i}_dY{kk}", dts[kk], w(f"br{kk}"), "nn", F32) for kk in range(N_BRANCH)]
        for kk in range(N_BRANCH):
            comm.grads(i, {f"br{kk}": mm(f"l{i}_dWbr{kk}", dts[kk], sv["ys"][kk], "tn", BF16)})
        dq, dk, dv = attn_bwd(f"l{i}_attn_bwd", sv["q"], sv["k"], sv["v"], dys[0], rc, not last)
        res = rows(f"l{i}_prep_bwd", _vjp_fn(_f_prep, 3, 3, keep=(0, 3, 4)), r, tm_n, nbc_n,
                   [(sv["p"], 0, OFF_POOL, 0), whole(cos), whole(sin), whole(dq), whole(dk), whole(dv)],
                   [_typed(s["q_norm_g"]), _typed(s["k_norm_g"])],
                   [(r, OFF_POOL, BF16, 0)], [HEAD_DIM, HEAD_DIM])
        dp_qkv, ds["q_norm_g"], ds["k_norm_g"] = res
        dp_pool, ds["pool_w"], ds["pool_scale"] = _pool_bwd(f"l{i}_pool_bwd", sv["p"], s["pool_w"], s["pool_scale"],
                                                            dys[1], rc)
        dp_sgu, ds["sgu_ln_g"], ds["sgu_ln_b"], ds["sgu_w"], ds["sgu_b"] = _sgu_bwd(
            f"l{i}_sgu_bwd", sv["p"], s["sgu_ln_g"], s["sgu_ln_b"], s["sgu_w"], s["sgu_b"], dys[2])
        dp_cb, dp_cc, dp_cx, dcw = _conv_bwd(f"l{i}_conv_bwd", sv["p"], s["conv_w"], dys[3], rc)
        ds["conv_w"] = dcw[0:3]
        dpb = jnp.concatenate([dp_qkv, dp_pool, dp_sgu, dp_cb, dp_cc, dp_cx], axis=1)
        comm.grads(i, {"in_t": mm(f"l{i}_dWin", dpb, sv["hb"], "tn", BF16)})
        comm.grads(i, {"gate_t": mm(f"l{i}_dWgate", dgb, sv["hb"], "tn", BF16)})
        dh = mm(f"l{i}_dhb_a", dpb, w("in_t"), "nn", F32)
        dh = mm(f"l{i}_dhb_b", dgb, w("gate_t"), "nn", F32, acc=dh)

    def f_mod_bwd(xb, ddir, dhb, sc, sh):
        _, vjp = jax.vjp(_f_mod, xb, sc, sh)
        dxb, dsc, dsh = vjp(dhb)
        return dxb + ddir, dsc, dsh

    grad_x, dmods[0]["sc1"], dmods[0]["sh1"] = rows(
        "mod_in_bwd", f_mod_bwd, r, tm_n, nbc_n, [whole(xin), whole(dx_direct), whole(dh)],
        [mods[0]["sc1"], mods[0]["sh1"]], [(r - rc, d, F32, nbc_n)], [d, d])
    dmod = jnp.stack([jnp.concatenate([dmods[i][nm][:, 0, :] for nm in MOD_NAMES], axis=-1)
                      for i in range(n_layers)])
    for ds in dsp:
        for nm in ("ln1_g", "ln1_b", "ln2_g", "ln2_b", "b_gate", "q_norm_g", "k_norm_g"):
            ds[nm] = ds[nm][0, 0] + ds[nm][1, 0]
        ds["pool_scale"] = ds["pool_scale"].reshape(-1)
        ds["sgu_ln_g"] = ds["sgu_ln_g"].reshape(-1)
        ds["sgu_ln_b"] = ds["sgu_ln_b"].reshape(-1)
        ds["sgu_b"] = ds["sgu_b"].reshape(N_GROUPS, SGU_CHUNK)
    return loss, grad_x, dmod, dsp


def _exchange(name, srcs, scatter):
    n_items = len(srcs)
    out_shapes = [jax.ShapeDtypeStruct(s.shape if scatter else (N_DEV,) + s.shape, s.dtype) for s in srcs]

    def body(*refs):
        src, out = refs[:n_items], refs[n_items:2 * n_items]
        send_sems, recv_sems, loc_sems = refs[2 * n_items:]
        mx, my, mc = [lax.axis_index(a) for a in MESH_AXES]
        me = 4 * mx + 2 * my + mc
        peers = []
        for kk in range(1, N_DEV):
            px = 1 - mx if kk & 4 else mx
            py = 1 - my if kk & 2 else my
            pc = 1 - mc if kk & 1 else mc
            peers.append(((px, py, pc), 4 * px + 2 * py + pc))
        local, sent = [], []
        for a in range(n_items):
            loc = pltpu.make_async_copy(src[a].at[me] if scatter else src[a], out[a].at[me], loc_sems.at[a])
            loc.start()
            local.append(loc)
            for j, (peer, peer_l) in enumerate(peers):
                cp = pltpu.make_async_remote_copy(
                    src_ref=src[a].at[peer_l] if scatter else src[a], dst_ref=out[a].at[me],
                    send_sem=send_sems.at[a * (N_DEV - 1) + j], recv_sem=recv_sems.at[a * (N_DEV - 1) + j],
                    device_id=peer, device_id_type=pl.DeviceIdType.MESH)
                cp.start()
                sent.append(cp)
        for a in range(n_items):
            for j, (peer, peer_l) in enumerate(peers):
                pltpu.make_async_remote_copy(
                    src_ref=src[a].at[peer_l] if scatter else src[a], dst_ref=out[a].at[peer_l],
                    send_sem=send_sems.at[a * (N_DEV - 1) + j], recv_sem=recv_sems.at[a * (N_DEV - 1) + j],
                    device_id=peer, device_id_type=pl.DeviceIdType.MESH).wait_recv()
        for cp in sent:
            cp.wait_send()
        for loc in local:
            loc.wait()

    any_spec = pl.BlockSpec(memory_space=pl.ANY)
    res = pl.pallas_call(
        body, name=name,
        in_specs=[any_spec] * n_items, out_specs=[any_spec] * n_items, out_shape=out_shapes,
        scratch_shapes=[pltpu.SemaphoreType.DMA((n_items * (N_DEV - 1),)),
                        pltpu.SemaphoreType.DMA((n_items * (N_DEV - 1),)),
                        pltpu.SemaphoreType.DMA((n_items,))],
    )(*srcs)
    return list(res)


def _mesh_place():
    mx, my, mc = [lax.axis_index(a) for a in MESH_AXES]
    chips = [(1 - mx, my), (mx, 1 - my), (1 - mx, 1 - my)]

    def lid(px, py, pc):
        return 4 * px + 2 * py + pc

    return (mx, my, mc), (mx, my, 1 - mc), chips, lid


def _rcopy(src, dst, send_sems, recv_sems, k, to):
    return pltpu.make_async_remote_copy(src_ref=src, dst_ref=dst, send_sem=send_sems.at[k], recv_sem=recv_sems.at[k],
                                        device_id=to, device_id_type=pl.DeviceIdType.MESH)


def _sem_scratch(*sizes):
    return [pltpu.SemaphoreType.DMA((s,)) for s in sizes]


def _gather_rider(src, rows, buf=None):
    r0, r1 = rows
    win = pl.ds(r0, r1 - r0)

    def start(ins, outs, sems):
        send, recv, loc = sems
        (mx, my, mc), sib, chips, lid = _mesh_place()
        mine, dst = ins[0].at[win], outs[0].at[lid(mx, my, mc), win]
        pltpu.make_async_copy(mine, dst, loc.at[0]).start()
        _rcopy(mine, dst, send, recv, 0, sib).start()
        for j, chip in enumerate(chips):
            _rcopy(mine, dst, send, recv, 1 + j, (*chip, mc)).start()

    def finish(ins, outs, sems):
        send, recv, loc = sems
        (mx, my, mc), sib, chips, lid = _mesh_place()
        mine, dst = ins[0].at[win], outs[0].at[lid(mx, my, mc), win]
        for j, chip in enumerate(chips):
            blk = outs[0].at[lid(*chip, mc), win]
            _rcopy(mine, blk, send, recv, 1 + j, (*chip, mc)).wait_recv()
            _rcopy(blk, blk, send, recv, 4 + j, sib).start()
        _rcopy(mine, outs[0].at[lid(*sib), win], send, recv, 0, sib).wait_recv()
        for j, chip in enumerate(chips):
            _rcopy(mine, outs[0].at[lid(*chip, 1 - mc), win], send, recv, 4 + j, sib).wait_recv()
        for t in range(7):
            _rcopy(mine, dst, send, recv, t, sib).wait_send()
        pltpu.make_async_copy(mine, dst, loc.at[0]).wait()

    out_shape = jax.ShapeDtypeStruct((N_DEV,) + src.shape, src.dtype)
    if buf is None:
        return _Rider([src], [out_shape], _sem_scratch(7, 7, 1), start, finish)
    return _Rider([src, buf], [out_shape], _sem_scratch(7, 7, 1), start, finish, aliases={1: 0})


def _sibling_rider(part):
    def start(ins, outs, sems):
        send, recv = sems
        (mx, my, mc), sib, chips, lid = _mesh_place()
        for t, slab in enumerate([lid(*sib)] + [lid(*chip, 1 - mc) for chip in chips]):
            _rcopy(ins[0].at[slab], outs[0].at[t], send, recv, t, sib).start()

    def finish(ins, outs, sems):
        send, recv = sems
        _, sib, _, _ = _mesh_place()
        for t in range(4):
            cp = _rcopy(ins[0].at[0], outs[0].at[t], send, recv, t, sib)
            cp.wait_recv()
            cp.wait_send()

    return _Rider([part], [jax.ShapeDtypeStruct((4,) + part.shape[1:], part.dtype)], _sem_scratch(4, 4), start, finish)


def _chips_rider(pair, rows, buf=None):
    r0, r1 = rows
    win = pl.ds(r0, r1 - r0)

    def start(ins, outs, sems):
        send, recv = sems
        (mx, my, mc), sib, chips, lid = _mesh_place()
        for j, chip in enumerate(chips):
            _rcopy(ins[0].at[j, win], outs[0].at[j, win], send, recv, j, (*chip, mc)).start()

    def finish(ins, outs, sems):
        send, recv = sems
        (mx, my, mc), sib, chips, lid = _mesh_place()
        for j, chip in enumerate(chips):
            cp = _rcopy(ins[0].at[j, win], outs[0].at[j, win], send, recv, j, (*chip, mc))
            cp.wait_recv()
            cp.wait_send()

    out_shape = jax.ShapeDtypeStruct(pair.shape, pair.dtype)
    if buf is None:
        return _Rider([pair], [out_shape], _sem_scratch(3, 3), start, finish)
    return _Rider([pair, buf], [out_shape], _sem_scratch(3, 3), start, finish, aliases={1: 0})


def _run_rider(name, rider):
    n_in, n_out = len(rider.inputs), len(rider.out_shapes)

    def body(*refs):
        ins, outs, sems = refs[:n_in], refs[n_in:n_in + n_out], refs[n_in + n_out:]
        rider.start(ins, outs, sems)
        rider.finish(ins, outs, sems)

    any_spec = pl.BlockSpec(memory_space=pl.ANY)
    res = pl.pallas_call(body, name=name, in_specs=[any_spec] * n_in, out_specs=[any_spec] * n_out,
                         out_shape=rider.out_shapes, scratch_shapes=rider.scratch,
                         input_output_aliases=rider.aliases)(*rider.inputs)
    return list(res)


def _slab_ids():
    (mx, my, mc), _, chips, lid = _mesh_place()
    return jnp.stack([lid(*chip, mc) for chip in chips] + [lid(mx, my, mc)]).astype(jnp.int32)


def _pair_sum(name, part, rsib, ids):
    _, n, k = part.shape
    tr = _row_tile(n, 512, 16)

    def body(ids_ref, p_ref, r_ref, o_ref):
        o_ref[...] = (p_ref[...].astype(F32) + r_ref[...].astype(F32)).astype(o_ref.dtype)

    grid_spec = pltpu.PrefetchScalarGridSpec(
        num_scalar_prefetch=1, grid=(3, n // tr),
        in_specs=[pl.BlockSpec((None, tr, k), lambda j, i, ids: (ids[j], i, 0)),
                  pl.BlockSpec((None, tr, k), lambda j, i, ids: (1 + j, i, 0))],
        out_specs=pl.BlockSpec((None, tr, k), lambda j, i, ids: (j, i, 0)))
    return pl.pallas_call(body, name=name, grid_spec=grid_spec, out_shape=jax.ShapeDtypeStruct((3, n, k), part.dtype),
                          compiler_params=_cparams("parallel", "parallel"))(ids, part, rsib)


def _sum5(name, part, rsib, rici, ids):
    _, n, k = part.shape
    tr = _row_tile(n, 512, 16)

    def body(ids_ref, p_ref, r_ref, c_ref, o_ref):
        acc = p_ref[...].astype(F32) + r_ref[...].astype(F32)
        for j in range(3):
            acc = acc + c_ref[j].astype(F32)
        o_ref[...] = acc

    grid_spec = pltpu.PrefetchScalarGridSpec(
        num_scalar_prefetch=1, grid=(n // tr,),
        in_specs=[pl.BlockSpec((None, tr, k), lambda i, ids: (ids[3], i, 0)),
                  pl.BlockSpec((None, tr, k), lambda i, ids: (0, i, 0)),
                  pl.BlockSpec((3, tr, k), lambda i, ids: (0, i, 0))],
        out_specs=pl.BlockSpec((tr, k), lambda i, ids: (i, 0)))
    return pl.pallas_call(body, name=name, grid_spec=grid_spec, out_shape=jax.ShapeDtypeStruct((n, k), F32),
                          compiler_params=_cparams("parallel"))(ids, part, rsib, rici)


W_KEYS = ("in_t", "gate_t", "br0", "br1", "br2", "br3", "o", "ffg_t", "ffu_t", "ffd")


CARRIER_US = {"mod_in": 12, "in": 55, "gate": 95, "prep": 19, "attn": 118, "br0": 15, "merge": 50, "o": 25, "ln1": 27,
              "ffg": 66, "ffu": 66, "swiglu": 42, "ffd": 73, "ln2": 27, "loss": 20, "ln2_bwd": 44, "dF": 70,
              "dWffd": 64, "swiglu_bwd": 66, "dh2a": 75, "dh2b": 75, "dWffg": 64, "dWffu": 64, "ln1_bwd": 44,
              "dMg": 25, "dWo": 25, "merge_bwd": 80, "dY0": 14, "dWbr0": 15, "attn_bwd": 195, "prep_bwd": 28,
              "dWin": 54, "dWgate": 95, "dhb_a": 64, "dhb_b": 115, "mod_in_bwd": 24}
ICI_US_PER_MIB = 45.0
D2D_US_PER_MIB = 6.8
MIN_CHUNK_US = 10.0


class _Comm:
    def __init__(self, wsrc):
        self.wsrc = wsrc
        self.n_layers = len(wsrc)
        self.queue = []
        self.riding = {}
        self.buf, self.left = {}, {}
        self.part, self.rsib, self.pair = {}, {}, {}
        self.ids = _slab_ids()
        for i in range(self.n_layers):
            for k in W_KEYS:
                self._push_chunks("gather", ("w", i, k), wsrc[i][k].shape, wsrc[i][k].dtype)

    def _push_chunks(self, kind, item, shape, dtype):
        n, k = shape[-2], shape[-1]
        us = n * k * jnp.dtype(dtype).itemsize / 2 ** 20 * ICI_US_PER_MIB
        pieces = max(1, int(us // MIN_CHUNK_US))
        while n % (16 * pieces):
            pieces -= 1
        step = n // pieces
        self.left[item] = pieces
        for c in range(pieces):
            self.queue.append(dict(kind=kind, item=item, rows=(c * step, (c + 1) * step), us=us / pieces))

    @staticmethod
    def _merge(units, u):
        v = units[-1] if units else None
        if not (v and v["item"] == u["item"] and v["kind"] == u["kind"] and u["rows"] and v["rows"][1] == u["rows"][0]):
            return False
        v.update(rows=(v["rows"][0], u["rows"][1]), us=v["us"] + u["us"], count=v.get("count", 1) + u.get("count", 1))
        return True

    def _unit_rider(self, u):
        item = u["item"]
        if u["kind"] == "gather":
            src = self.wsrc[item[1]][item[2]] if item[0] == "w" else self.part[item]
            return _gather_rider(src, u["rows"], self.buf.get(item))
        if u["kind"] == "sibling":
            return _sibling_rider(self.part[item])
        return _chips_rider(self.pair[item], u["rows"], self.buf.get(item))

    def _done(self, u, out):
        item = u["item"]
        if u["kind"] == "sibling":
            self.rsib[item] = out
            self.pair[item] = _pair_sum(f"pair_l{item[1]}_{item[2]}", self.part[item], out, self.ids)
            self._push_chunks("chips", item, self.pair[item].shape, self.pair[item].dtype)
            return
        self.buf[item] = out
        self.left[item] -= u.get("count", 1)

    def _send(self, name, units, call):
        outs = call(_compose([self._unit_rider(u) for u in units]))
        for u, o in zip(units, outs):
            self._done(u, o)

    def rider(self, name, budget_us=None):
        budget = CARRIER_US.get(name.split("_", 1)[1] if name[0] == "l" and name[1].isdigit() else name, 0) \
            if budget_us is None else budget_us
        units, used = [], 0.0
        while self.queue and used + self.queue[0]["us"] <= 1.15 * budget:
            u = self.queue[0]
            if not self._merge(units, u):
                if any(v["item"] == u["item"] for v in units):
                    break
                units.append(dict(u))
            used += u["us"]
            del self.queue[0]
        if not units:
            return None
        self.riding[name] = units
        return _compose([self._unit_rider(u) for u in units])

    def deliver(self, name, outs):
        for u, o in zip(self.riding.pop(name), outs):
            self._done(u, o)

    def _flush(self, item, kinds):
        hits = [p for p, u in enumerate(self.queue) if u["item"] == item and u["kind"] in kinds]
        if not hits:
            return
        prefix = self.queue[:hits[-1] + 1]
        del self.queue[:hits[-1] + 1]
        units = []
        for u in prefix:
            if not self._merge(units, u):
                units.append(dict(u))
        tag = "_".join(str(t) for t in item) + "_" + kinds[0]
        batches = [[]]
        for u in units:
            if any(v["item"] == u["item"] for v in batches[-1]):
                batches.append([])
            batches[-1].append(u)
        for b, batch in enumerate(batches):
            self._send(None, batch, functools.partial(_run_rider, f"alone_{tag}_{b}"))

    def begin(self):
        self._flush(("w", 0, "in_t"), ("gather",))

    def weight(self, i, k):
        item = ("w", i, k)
        self._flush(item, ("gather",))
        o = self.buf[item]
        return o.reshape(-1, o.shape[-1])

    def grads(self, i, group):
        for k, g in group.items():
            item = ("g", i, k)
            self.part[item] = g.reshape(N_DEV, g.shape[0] // N_DEV, g.shape[1])
            us = g.size // N_DEV * g.dtype.itemsize / 2 ** 20 * D2D_US_PER_MIB
            self.queue.append(dict(kind="sibling", item=item, rows=None, us=us))

    def total(self, i, k):
        item = ("g", i, k)
        self._flush(item, ("sibling",))
        self._flush(item, ("chips",))
        return _sum5(f"sum_l{i}_{k}", self.part[item], self.rsib[item], self.buf[item], self.ids)

    def gather_small(self, name, arr):
        item = ("s", name)
        self.part[item] = arr
        self._push_chunks("gather", item, arr.shape, arr.dtype)

    def gathered(self, name):
        item = ("s", name)
        self._flush(item, ("gather",))
        return self.buf[item]


def _row_tile(n, pref, mult):
    best = None
    t = mult
    while t <= min(n, pref):
        if n % t == 0:
            best = t
        t += mult
    return best if best is not None else n


def _sum8(name, slabs):
    _, n, k = slabs.shape
    tr = _row_tile(n, 128, 16)

    def body(s_ref, o_ref):
        acc = s_ref[0].astype(F32)
        for j in range(1, N_DEV):
            acc = acc + s_ref[j].astype(F32)
        o_ref[...] = acc

    return pl.pallas_call(
        body, name=name, grid=(n // tr,),
        in_specs=[pl.BlockSpec((N_DEV, tr, k), lambda i: (0, i, 0))],
        out_specs=pl.BlockSpec((tr, k), lambda i: (i, 0)),
        out_shape=jax.ShapeDtypeStruct((n, k), F32),
        compiler_params=_cparams("parallel"),
    )(slabs)


def _adamw(name, w, g, m, v, rider=None):
    n, k = w.shape
    tr = _row_tile(n, 256, 8)

    def body(w_ref, g_ref, m_ref, v_ref, d_ref, m2_ref, v2_ref):
        gv = g_ref[...]
        m2 = ADAM_B1 * m_ref[...] + (1.0 - ADAM_B1) * gv
        v2 = ADAM_B2 * v_ref[...] + (1.0 - ADAM_B2) * jnp.square(gv)
        m_hat = m2 / (1.0 - ADAM_B1 ** ADAM_STEP)
        v_hat = v2 / (1.0 - ADAM_B2 ** ADAM_STEP)
        d_ref[...] = -ADAM_LR * (m_hat / (jnp.sqrt(v_hat) + ADAM_EPS) + ADAM_WD * w_ref[...])
        m2_ref[...] = m2
        v2_ref[...] = v2

    spec = pl.BlockSpec((tr, k), lambda i: (i, 0))
    outs, r_outs = _pcall(name, body, (n // tr,), [spec] * 4, [spec] * 3, [jax.ShapeDtypeStruct((n, k), F32)] * 3,
                          (w, g, m, v), ("parallel",), rider=rider)
    return outs if rider is None else (outs, r_outs)


def _pack(arrs):
    flat = jnp.concatenate([a.reshape(-1).astype(F32) for a in arrs])
    pad = (-flat.shape[0]) % 2048
    if pad:
        flat = jnp.concatenate([flat, jnp.zeros((pad,), F32)])
    return flat.reshape(-1, 128)


def _unpack(packed, shapes):
    flat = packed.reshape(-1)
    out, off = [], 0
    for shp in shapes:
        size = math.prod(shp)
        out.append(flat[off:off + size].reshape(shp))
        off += size
    return out


WEIGHT_NAMES = ("c_ctx", "w_ada", "b_ada", "w_in", "q_norm_g", "k_norm_g", "pool_w", "pool_scale", "sgu_ln_g",
                "sgu_ln_b", "sgu_w", "sgu_b", "conv_w", "w_br_attn", "w_br_pool", "w_br_sgu", "w_br_conv", "w_gate",
                "b_gate", "w_o", "ln1_g", "ln1_b", "w_ff_gate", "w_ff_up", "w_ff_down", "ln2_g", "ln2_b")
COL_SHARDED = {"w_in": "in_t", "w_gate": "gate_t", "w_ff_gate": "ffg_t", "w_ff_up": "ffu_t",
               "w_br_attn": "br0", "w_br_pool": "br1", "w_br_sgu": "br2", "w_br_conv": "br3"}
ROW_SHARDED = {"w_o": "o", "w_ff_down": "ffd"}
LAYER_SMALL = ("q_norm_g", "k_norm_g", "pool_w", "pool_scale", "sgu_ln_g", "sgu_ln_b", "sgu_w", "sgu_b", "b_gate",
               "ln1_g", "ln1_b", "ln2_g", "ln2_b")
SMALL_ORDER = ("c_ctx", "b_ada") + LAYER_SMALL + ("conv_w",)


def _train_step(a):
    n_layers, d = a["w_in"].shape[0], a["x"].shape[-1]
    rc = a["ctx"].shape[1]
    alpha = (2 * n_layers) ** 0.25
    mx, my, mc = [lax.axis_index(ax) for ax in MESH_AXES]
    me = 4 * mx + 2 * my + mc
    ada_w = a["w_ada"].shape[-1]
    cw_loc = a["conv_w"].shape[-1]

    n_c, n_cw = d, n_layers * 3 * cw_loc
    got = _exchange("gather_cond", [_pack([a["c"], a["conv_w"]])], False)[0].reshape(N_DEV, -1)
    c_all = got[:, :n_c]
    conv_w = got[:, n_c:n_c + n_cw].reshape(N_DEV, n_layers, 3, cw_loc).transpose(1, 2, 0, 3).reshape(n_layers, 3, -1)
    cond = jnp.concatenate([c_all, a["c_ctx"][None], jnp.zeros((16 - N_DEV - 1, d), F32)], axis=0)
    sil, sil_vjp = jax.vjp(jax.nn.silu, cond)
    sil = sil.astype(BF16)

    mod_cols = jnp.concatenate([_mm(f"ada{i}", sil, a["w_ada"][i], "nn", F32) for i in range(n_layers)], axis=0)
    got = _exchange("gather_mod", [mod_cols], False)[0]
    mod_all = got.reshape(N_DEV, n_layers, 16, ada_w).transpose(1, 2, 0, 3).reshape(n_layers, 16, -1)
    mod_all = mod_all + a["b_ada"][:, None, :]
    mod = jnp.stack([mod_all[:, N_DEV], lax.dynamic_index_in_dim(mod_all, me, axis=1, keepdims=False)], axis=1)

    comm = _Comm([{**{key: a[nm][i].T.astype(BF16) for nm, key in COL_SHARDED.items()},
                   **{key: a[nm][i].astype(BF16) for nm, key in ROW_SHARDED.items()}} for i in range(n_layers)])
    comm.begin()
    sp = [{nm: a[nm][i] for nm in LAYER_SMALL} for i in range(n_layers)]
    for i in range(n_layers):
        sp[i]["conv_w"] = conv_w[i]

    xin = jnp.concatenate([a["ctx"][0], a["x"][0]], axis=0)
    loss_l, grad_x, dmod, dsp = _local_step(xin, a["loss_target"][0], mod, comm, sp, rc, alpha)
    loss = lax.psum(loss_l, MESH_AXES)
    grads = {}

    def big_grad(nm):
        key = COL_SHARDED.get(nm, ROW_SHARDED.get(nm))
        per_layer = [comm.total(i, key) for i in range(n_layers)]
        return jnp.stack([t.T for t in per_layer] if nm in COL_SHARDED else per_layer)

    small_parts = [dmod[:, 0], dmod[:, 1]]
    small_shapes = [dmod[:, 0].shape, dmod[:, 1].shape]
    for nm in LAYER_SMALL + ("conv_w",):
        part = jnp.stack([dsp[i][nm] for i in range(n_layers)])
        small_parts.append(part)
        small_shapes.append(part.shape)
    comm.gather_small("lat", _pack([dmod[:, 1]]))
    comm.gather_small("small", _pack(small_parts))

    delta, new_m, new_v = {}, {}, {}

    def adamw(nm):
        shp = a[nm].shape
        view = (-1, shp[-1])
        name = f"adamw_{nm}"
        rider = comm.rider(name, budget_us=a[nm].size * 28 / 3.3e6)
        res = _adamw(name, a[nm].reshape(view), grads[nm].reshape(view), a["m_" + nm].reshape(view),
                     a["v_" + nm].reshape(view), rider=rider)
        if rider is not None:
            res, r_outs = res
            comm.deliver(name, r_outs)
        delta[nm], new_m[nm], new_v[nm] = [t.reshape(shp) for t in res]

    for nm in ("w_ff_down", "w_ff_gate", "w_ff_up", "w_o", "w_br_attn", "w_br_pool", "w_br_sgu", "w_br_conv"):
        grads[nm] = big_grad(nm)
        adamw(nm)

    got_lat, got_small = comm.gathered("lat"), comm.gathered("small")
    tot = _unpack(_sum8("sum_small", got_small), small_shapes)
    dmod_c, dmod_lat_sum = tot[0], tot[1]
    for nm, g in zip(LAYER_SMALL + ("conv_w",), tot[2:]):
        grads[nm] = g
    grads["conv_w"] = lax.dynamic_slice_in_dim(grads["conv_w"], me * cw_loc, cw_loc, axis=2)
    grads["b_ada"] = dmod_c + dmod_lat_sum
    dmod_lat_all = got_lat.reshape(N_DEV, -1)[:, :n_layers * 6 * d].reshape(N_DEV, n_layers, 6 * d)
    dm_rows = jnp.concatenate([dmod_lat_all.transpose(1, 0, 2), dmod_c[:, None, :],
                               jnp.zeros((n_layers, 16 - N_DEV - 1, 6 * d), F32)], axis=1)
    dm_cols = lax.dynamic_slice_in_dim(dm_rows, me * ada_w, ada_w, axis=2).astype(BF16)
    grads["w_ada"] = jnp.stack([_mm(f"dWada{i}", sil, dm_cols[i], "tn", F32) for i in range(n_layers)])
    dsil = None
    for i in range(n_layers):
        dsil = _mm(f"dsil{i}", dm_cols[i], a["w_ada"][i], "nt", F32, acc=dsil)
    got = _exchange("gather_dsil", [dsil], False)[0]
    dsil = _sum8("sum_dsil", got)
    grads["c_ctx"] = sil_vjp(dsil)[0][N_DEV]

    adamw("w_ada")
    for nm in ("w_in", "w_gate"):
        grads[nm] = big_grad(nm)
        adamw(nm)
    shapes = [a[nm].shape for nm in SMALL_ORDER]
    res = _adamw("adamw_small", _pack([a[nm] for nm in SMALL_ORDER]), _pack([grads[nm] for nm in SMALL_ORDER]),
                 _pack([a["m_" + nm] for nm in SMALL_ORDER]), _pack([a["v_" + nm] for nm in SMALL_ORDER]))
    for tree, packed in zip((delta, new_m, new_v), res):
        for nm, t in zip(SMALL_ORDER, _unpack(packed, shapes)):
            tree[nm] = t
    return (loss, grad_x[None], *[grads[nm] for nm in WEIGHT_NAMES], *[delta[nm] for nm in WEIGHT_NAMES],
            *[new_m[nm] for nm in WEIGHT_NAMES], *[new_v[nm] for nm in WEIGHT_NAMES])


def kernel(x, c, ctx, c_ctx, w_ada, b_ada, w_in, q_norm_g, k_norm_g, pool_w, pool_scale, sgu_ln_g, sgu_ln_b, sgu_w, sgu_b, conv_w, w_br_attn, w_br_pool, w_br_sgu, w_br_conv, w_gate, b_gate, w_o, ln1_g, ln1_b, w_ff_gate, w_ff_up, w_ff_down, ln2_g, ln2_b, loss_target, m_c_ctx, m_w_ada, m_b_ada, m_w_in, m_q_norm_g, m_k_norm_g, m_pool_w, m_pool_scale, m_sgu_ln_g, m_sgu_ln_b, m_sgu_w, m_sgu_b, m_conv_w, m_w_br_attn, m_w_br_pool, m_w_br_sgu, m_w_br_conv, m_w_gate, m_b_gate, m_w_o, m_ln1_g, m_ln1_b, m_w_ff_gate, m_w_ff_up, m_w_ff_down, m_ln2_g, m_ln2_b, v_c_ctx, v_w_ada, v_b_ada, v_w_in, v_q_norm_g, v_k_norm_g, v_pool_w, v_pool_scale, v_sgu_ln_g, v_sgu_ln_b, v_sgu_w, v_sgu_b, v_conv_w, v_w_br_attn, v_w_br_pool, v_w_br_sgu, v_w_br_conv, v_w_gate, v_b_gate, v_w_o, v_ln1_g, v_ln1_b, v_w_ff_gate, v_w_ff_up, v_w_ff_down, v_ln2_g, v_ln2_b):
    names = list(WEIGHT_NAMES)
    args = dict(zip(
        ["x", "c", "ctx"] + names + ["loss_target"] + ["m_" + n for n in names] + ["v_" + n for n in names],
        (x, c, ctx, c_ctx, w_ada, b_ada, w_in, q_norm_g, k_norm_g, pool_w, pool_scale, sgu_ln_g, sgu_ln_b, sgu_w, sgu_b, conv_w, w_br_attn, w_br_pool, w_br_sgu, w_br_conv, w_gate, b_gate, w_o, ln1_g, ln1_b, w_ff_gate, w_ff_up, w_ff_down, ln2_g, ln2_b, loss_target, m_c_ctx, m_w_ada, m_b_ada, m_w_in, m_q_norm_g, m_k_norm_g, m_pool_w, m_pool_scale, m_sgu_ln_g, m_sgu_ln_b, m_sgu_w, m_sgu_b, m_conv_w, m_w_br_attn, m_w_br_pool, m_w_br_sgu, m_w_br_conv, m_w_gate, m_b_gate, m_w_o, m_ln1_g, m_ln1_b, m_w_ff_gate, m_w_ff_up, m_w_ff_down, m_ln2_g, m_ln2_b, v_c_ctx, v_w_ada, v_b_ada, v_w_in, v_q_norm_g, v_k_norm_g, v_pool_w, v_pool_scale, v_sgu_ln_g, v_sgu_ln_b, v_sgu_w, v_sgu_b, v_conv_w, v_w_br_attn, v_w_br_pool, v_w_br_sgu, v_w_br_conv, v_w_gate, v_b_gate, v_w_o, v_ln1_g, v_ln1_b, v_w_ff_gate, v_w_ff_up, v_w_ff_down, v_ln2_g, v_ln2_b)))
    return _train_step(args)
```

```python
import functools
import math

import jax
import jax.numpy as jnp
from jax import lax
from jax.experimental import pallas as pl
from jax.experimental.pallas import tpu as pltpu

F32 = jnp.float32
BF16 = jnp.bfloat16

N_DEV = 8
MESH_AXES = ("x", "y", "c")
V7X_VMEM_LIMIT_BYTES = 56 * 1024 * 1024

GRID_W = 64
HEAD_DIM = 128
N_HEADS = 8
N_KV_HEADS = 2
KV_GROUP = N_HEADS // N_KV_HEADS
Q_W = N_HEADS * HEAD_DIM
KV_W = N_KV_HEADS * HEAD_DIM
ROPE_THETA = 10000.0
ROPE_AXIS_DIM = HEAD_DIM // 2
POOL_WINDOWS = (2, 4, 8, 16)
GC = 128
N_GROUPS = 4
BR_W = N_GROUPS * GC
SGU_CHUNK = 128
N_BRANCH = 4
LN_EPS = 1e-5
RMS_EPS = 1e-6
OFF_K = Q_W
OFF_V = OFF_K + KV_W
OFF_POOL = OFF_V + KV_W
OFF_U = OFF_POOL + BR_W
OFF_VG = OFF_U + BR_W
OFF_CB = OFF_VG + BR_W
OFF_CC = OFF_CB + BR_W
OFF_CX = OFF_CC + BR_W
IN_W = OFF_CX + BR_W
ATT_SCALE = HEAD_DIM ** -0.5

ADAM_LR = 0.001
ADAM_B1 = 0.9
ADAM_B2 = 0.999
ADAM_EPS = 1e-08
ADAM_WD = 0.01
ADAM_STEP = 10

_NT = (((1,), (1,)), ((), ()))
_NN = (((1,), (0,)), ((), ()))
_TN = (((0,), (0,)), ((), ()))
_DIMS = {"nt": _NT, "nn": _NN, "tn": _TN}


def _cparams(*sem):
    return pltpu.CompilerParams(dimension_semantics=sem, vmem_limit_bytes=V7X_VMEM_LIMIT_BYTES)


def _tile(dim, pref):
    best = None
    t = 128
    while t <= min(dim, pref):
        if dim % t == 0:
            best = t
        t += 128
    return best if best is not None else dim


def _dot(a, b, dims):
    return lax.dot_general(a.astype(BF16), b.astype(BF16), dims, preferred_element_type=F32)


class _Rider:
    def __init__(self, inputs, out_shapes, scratch, start, finish, aliases=None):
        self.inputs, self.out_shapes, self.scratch = list(inputs), list(out_shapes), list(scratch)
        self.start, self.finish = start, finish
        self.aliases = dict(aliases or {})


def _compose(riders):
    inputs, outs, scratch, aliases, spans = [], [], [], {}, []
    for rd in riders:
        i0, o0, s0 = len(inputs), len(outs), len(scratch)
        aliases.update({i0 + p: o0 + q for p, q in rd.aliases.items()})
        inputs += rd.inputs
        outs += rd.out_shapes
        scratch += rd.scratch
        spans.append((slice(i0, len(inputs)), slice(o0, len(outs)), slice(s0, len(scratch))))

    def start(ins, os, sems):
        for rd, (si, so, ss) in zip(riders, spans):
            rd.start(ins[si], os[so], sems[ss])

    def finish(ins, os, sems):
        for rd, (si, so, ss) in zip(riders, spans):
            rd.finish(ins[si], os[so], sems[ss])

    return _Rider(inputs, outs, scratch, start, finish, aliases)


def _pcall(name, body, grid, in_specs, out_specs, out_shape, args, sem, scratch=(), rider=None):
    in_specs, out_specs, out_shape, scratch = list(in_specs), list(out_specs), list(out_shape), list(scratch)
    if rider is None:
        res = pl.pallas_call(body, name=name, grid=grid, in_specs=in_specs, out_specs=out_specs, out_shape=out_shape,
                             scratch_shapes=scratch, compiler_params=_cparams(*sem))(*args)
        return list(res), []
    n_in, n_out, n_scr = len(in_specs), len(out_specs), len(scratch)
    r_in, r_out = len(rider.inputs), len(rider.out_shapes)

    def wrapped(*refs):
        ins, refs = refs[:n_in], refs[n_in:]
        r_ins, refs = refs[:r_in], refs[r_in:]
        outs, refs = refs[:n_out], refs[n_out:]
        r_outs, refs = refs[:r_out], refs[r_out:]
        scr, r_scr = refs[:n_scr], refs[n_scr:]
        first = functools.reduce(jnp.logical_and, [pl.program_id(ax) == 0 for ax in range(len(grid))])
        last = functools.reduce(jnp.logical_and, [pl.program_id(ax) == grid[ax] - 1 for ax in range(len(grid))])

        @pl.when(first)
        def _():
            rider.start(r_ins, r_outs, r_scr)

        body(*ins, *outs, *scr)

        @pl.when(last)
        def _():
            rider.finish(r_ins, r_outs, r_scr)

    any_spec = pl.BlockSpec(memory_space=pl.ANY)
    res = pl.pallas_call(
        wrapped, name=name, grid=grid,
        in_specs=in_specs + [any_spec] * r_in, out_specs=out_specs + [any_spec] * r_out,
        out_shape=out_shape + rider.out_shapes, scratch_shapes=scratch + rider.scratch,
        input_output_aliases={n_in + p: n_out + q for p, q in rider.aliases.items()},
        compiler_params=_cparams(*(["arbitrary"] * len(grid))),
    )(*args, *rider.inputs)
    return list(res[:n_out]), list(res[n_out:])


V7X_MM_VMEM_BUDGET = 40 * 1024 * 1024


def _mm_plan(form, m, n, k, a_size, b_size, o_size, has_acc):
    tk = k if k <= 2816 else _tile(k, 2816)
    nk = k // tk
    tn = n if (form == "tn" and n <= 2048) else _tile(n, 512)
    for tm in sorted({m} | {t for t in range(128, m, 128) if m % t == 0}, reverse=True):
        need = 2 * (tm * tk * a_size + tn * tk * b_size + tm * tn * o_size) + tm * tn * 4 * (2 if nk > 1 else 1)
        need += 2 * tm * tn * 4 if has_acc else 0
        if need <= V7X_MM_VMEM_BUDGET:
            return tm, tn, tk
    return _tile(m, 128), tn, tk


def _mm(name, a, b, form, out_dtype, acc=None, rider=None):
    if form == "nt":
        (m, k), (n, k2) = a.shape, b.shape
    elif form == "nn":
        (m, k), (k2, n) = a.shape, b.shape
    else:
        (k, m), (k2, n) = a.shape, b.shape
    assert k == k2, (name, a.shape, b.shape)
    has_acc = acc is not None
    tm, tn, tk = _mm_plan(form, m, n, k, a.dtype.itemsize, b.dtype.itemsize, jnp.dtype(out_dtype).itemsize, has_acc)
    nk = k // tk
    a_spec = {"nt": pl.BlockSpec((tm, tk), lambda i, j, kk: (i, kk)),
              "nn": pl.BlockSpec((tm, tk), lambda i, j, kk: (i, kk)),
              "tn": pl.BlockSpec((tk, tm), lambda i, j, kk: (kk, i))}[form]
    b_spec = {"nt": pl.BlockSpec((tn, tk), lambda i, j, kk: (j, kk)),
              "nn": pl.BlockSpec((tk, tn), lambda i, j, kk: (kk, j)),
              "tn": pl.BlockSpec((tk, tn), lambda i, j, kk: (kk, j))}[form]
    o_spec = pl.BlockSpec((tm, tn), lambda i, j, kk: (i, j))
    dims = _DIMS[form]

    def body(*refs):
        a_ref, b_ref = refs[0], refs[1]
        c_ref = refs[2] if has_acc else None
        o_ref = refs[3] if has_acc else refs[2]

        def finish(r):
            if has_acc:
                r = r + c_ref[...]
            o_ref[...] = r.astype(o_ref.dtype)

        if nk == 1:
            finish(_dot(a_ref[...], b_ref[...], dims))
            return
        acc_ref = refs[-1]
        kk = pl.program_id(2)

        @pl.when(kk == 0)
        def _():
            acc_ref[...] = _dot(a_ref[...], b_ref[...], dims)

        @pl.when(kk > 0)
        def _():
            acc_ref[...] += _dot(a_ref[...], b_ref[...], dims)

        @pl.when(kk == nk - 1)
        def _():
            finish(acc_ref[...])

    in_specs = [a_spec, b_spec] + ([o_spec] if has_acc else [])
    args = (a, b) + ((acc,) if has_acc else ())
    outs, r_outs = _pcall(name, body, (m // tm, n // tn, nk), in_specs, [o_spec],
                          [jax.ShapeDtypeStruct((m, n), out_dtype)], args, ("parallel", "parallel", "arbitrary"),
                          scratch=[pltpu.VMEM((tm, tn), F32)] if nk > 1 else [], rider=rider)
    return outs[0] if rider is None else (outs[0], r_outs)


def _rows(name, fn, n_rows, tm, nbc, row_ins, type_ins, row_outs, acc_outs, rider=None):
    n_ri, n_ti, n_ro, n_ao = len(row_ins), len(type_ins), len(row_outs), len(acc_outs)

    def row_map(i, cb, roff):
        return (jnp.maximum(i - roff, 0), cb)

    def type_map(i):
        return (jnp.where(i >= nbc, 1, 0), 0, 0)

    in_specs, args = [], []
    for arr, cb, width, roff in row_ins:
        in_specs.append(pl.BlockSpec((tm, width), functools.partial(row_map, cb=cb, roff=roff)))
        args.append(arr)
    for arr in type_ins:
        in_specs.append(pl.BlockSpec((None, 1, arr.shape[-1]), type_map))
        args.append(arr)
    out_shape, out_specs = [], []
    for total, width, dtype, roff in row_outs:
        out_shape.append(jax.ShapeDtypeStruct((total, width), dtype))
        out_specs.append(pl.BlockSpec((tm, width), functools.partial(row_map, cb=0, roff=roff)))
    for width in acc_outs:
        out_shape.append(jax.ShapeDtypeStruct((2, 1, width), F32))
        out_specs.append(pl.BlockSpec((None, 1, width), type_map))
    n_in = n_ri + n_ti

    def body(*refs):
        i = pl.program_id(0)
        outs = fn(*[r[...] for r in refs[:n_in]])
        if not isinstance(outs, (tuple, list)):
            outs = (outs,)
        assert len(outs) == n_ro + n_ao, (name, len(outs))
        for r, o in zip(refs[n_in:n_in + n_ro], outs[:n_ro]):
            r[...] = o.astype(r.dtype)
        if n_ao:
            first = jnp.logical_or(i == 0, i == nbc)
            for r, o in zip(refs[n_in + n_ro:], outs[n_ro:]):
                o = jnp.broadcast_to(o.astype(F32), r.shape)

                @pl.when(first)
                def _(r=r, o=o):
                    r[...] = o

                @pl.when(jnp.logical_not(first))
                def _(r=r, o=o):
                    r[...] += o

    outs, r_outs = _pcall(name, body, (n_rows // tm,), in_specs, out_specs, out_shape, args, ("arbitrary",),
                          rider=rider)
    return outs if rider is None else (outs, r_outs)


def _vjp_fn(f, n_row, n_cot, keep=None):
    def g(*args):
        prim = args[:n_row] + args[n_row + n_cot:]
        cots = args[n_row:n_row + n_cot]
        out, vjp = jax.vjp(f, *prim)
        grads = vjp(tuple(cots) if isinstance(out, (tuple, list)) else cots[0])
        return grads if keep is None else tuple(grads[j] for j in keep)
    return g


def _typed(v):
    v = v.reshape(1, 1, -1)
    return jnp.concatenate([v, v], axis=0)


def _ln(x, g, b):
    mu = jnp.mean(x, axis=-1, keepdims=True)
    var = jnp.mean(jnp.square(x - mu), axis=-1, keepdims=True)
    return (x - mu) * lax.rsqrt(var + LN_EPS) * g + b


def _f_mod(x, sc, sh):
    return x * (1.0 + sc) + sh


def _make_f_ln(alpha, with_mod):
    def f(x, o, gate, lng, lnb, *mod):
        xn = _ln(alpha * x + gate * o, lng, lnb)
        if with_mod:
            sc, sh = mod
            return xn, xn * (1.0 + sc) + sh
        return xn
    return f


@jax.custom_vjp
def _rot(y):
    lane = lax.broadcasted_iota(jnp.int32, y.shape, 1)
    return jnp.where(lane % 64 < 32, pltpu.roll(y, 96, axis=1), pltpu.roll(y, 32, axis=1))


_rot.defvjp(lambda y: (_rot(y), None), lambda _, g: (_rot(g),))


def _f_prep(p, cos, sin, qg, kg):
    def head(xh, g):
        ms = jnp.mean(jnp.square(xh), axis=-1, keepdims=True)
        y = xh * lax.rsqrt(ms + RMS_EPS) * g
        return y * cos + _rot(y) * sin
    q = jnp.concatenate([head(p[:, h * HEAD_DIM:(h + 1) * HEAD_DIM], qg) for h in range(N_HEADS)], axis=1)
    k = jnp.concatenate([head(p[:, OFF_K + h * HEAD_DIM:OFF_K + (h + 1) * HEAD_DIM], kg)
                         for h in range(N_KV_HEADS)], axis=1)
    return q, k, p[:, OFF_V:OFF_POOL]


def _f_gate(g, t0, t1, t2, t3, b):
    d = t0.shape[-1]
    ts = (t0, t1, t2, t3)
    terms = [jax.nn.sigmoid(g[:, k * d:(k + 1) * d] + b[:, k * d:(k + 1) * d]) * ts[k] for k in range(N_BRANCH)]
    return terms[0] + terms[1] + terms[2] + terms[3]


def _f_swiglu(a, b):
    return jax.nn.silu(a) * b


def _softmax(s):
    e = jnp.exp(s - jnp.max(s, axis=-1, keepdims=True))
    return e / jnp.sum(e, axis=-1, keepdims=True)


def _attn_fwd(name, q, k, v, rc, ctx_queries, tq=256, rider=None):
    r = q.shape[0]
    assert rc % tq == 0 and r % tq == 0
    nqc = rc // tq

    def body(q_ref, k_ref, v_ref, o_ref):
        qi = pl.program_id(1)

        def attend(nk):
            s = _dot(q_ref[...], k_ref[0:nk, :], _NT) * ATT_SCALE
            p = _softmax(s)
            o_ref[...] = _dot(p, v_ref[0:nk, :], _NN).astype(o_ref.dtype)

        @pl.when(qi < nqc)
        def _():
            if ctx_queries:
                attend(rc)
            else:
                o_ref[...] = jnp.zeros_like(o_ref)

        @pl.when(qi >= nqc)
        def _():
            attend(r)

    outs, r_outs = _pcall(
        name, body, (N_HEADS, r // tq),
        [pl.BlockSpec((tq, HEAD_DIM), lambda h, i: (i, h)),
         pl.BlockSpec((r, HEAD_DIM), lambda h, i: (0, h // KV_GROUP)),
         pl.BlockSpec((r, HEAD_DIM), lambda h, i: (0, h // KV_GROUP))],
        [pl.BlockSpec((tq, HEAD_DIM), lambda h, i: (i, h))],
        [jax.ShapeDtypeStruct((r, Q_W), BF16)], (q, k, v), ("parallel", "parallel"), rider=rider)
    return outs[0] if rider is None else (outs[0], r_outs)


def _attn_bwd(name, q, k, v, do, rc, ctx_queries, tq=256, rider=None):
    r = q.shape[0]
    nqc = rc // tq

    def body(q_ref, k_ref, v_ref, do_ref, dq_ref, dk_ref, dv_ref):
        g, qi = pl.program_id(1), pl.program_id(2)

        @pl.when(jnp.logical_and(g == 0, qi == 0))
        def _():
            dk_ref[...] = jnp.zeros_like(dk_ref)
            dv_ref[...] = jnp.zeros_like(dv_ref)

        def grad(nk):
            qb, kb, vb = q_ref[...], k_ref[0:nk, :], v_ref[0:nk, :]
            dob = do_ref[...].astype(BF16)
            p = _softmax(_dot(qb, kb, _NT) * ATT_SCALE)
            dv_ref[0:nk, :] += _dot(p, dob, _TN)
            dp = _dot(dob, vb, _NT)
            ds = p * (dp - jnp.sum(dp * p, axis=-1, keepdims=True)) * ATT_SCALE
            dq_ref[...] = _dot(ds, kb, _NN)
            dk_ref[0:nk, :] += _dot(ds, qb, _TN)

        @pl.when(qi < nqc)
        def _():
            if ctx_queries:
                grad(rc)
            else:
                dq_ref[...] = jnp.zeros_like(dq_ref)

        @pl.when(qi >= nqc)
        def _():
            grad(r)

    def qmap(kv, g, i):
        return (i, kv * KV_GROUP + g)

    def kvmap(kv, g, i):
        return (0, kv)

    outs, r_outs = _pcall(
        name, body, (N_KV_HEADS, KV_GROUP, r // tq),
        [pl.BlockSpec((tq, HEAD_DIM), qmap), pl.BlockSpec((r, HEAD_DIM), kvmap),
         pl.BlockSpec((r, HEAD_DIM), kvmap), pl.BlockSpec((tq, HEAD_DIM), qmap)],
        [pl.BlockSpec((tq, HEAD_DIM), qmap), pl.BlockSpec((r, HEAD_DIM), kvmap), pl.BlockSpec((r, HEAD_DIM), kvmap)],
        [jax.ShapeDtypeStruct((r, Q_W), F32), jax.ShapeDtypeStruct((r, KV_W), F32),
         jax.ShapeDtypeStruct((r, KV_W), F32)],
        (q, k, v, do), ("arbitrary", "arbitrary", "arbitrary"), rider=rider)
    return outs if rider is None else (outs, r_outs)


def _segments(shape, rc):
    t = lax.broadcasted_iota(jnp.int32, shape, 0)
    lo = jnp.where(t < rc, 0, rc)
    hi = jnp.where(t < rc, rc, shape[0])
    return t, lo, hi


def _shifted(x, o, t, lo, hi):
    n = x.shape[0]
    sh = pltpu.roll(x, (-o) % n, axis=0)
    return jnp.where(jnp.logical_and(t + o >= lo, t + o < hi), sh, 0.0)


def _winsum(x, left, right, t, lo, hi):
    acc = x
    for o in range(-left, right + 1):
        if o != 0:
            acc = acc + _shifted(x, o, t, lo, hi)
    return acc


def _pool_parts(z, g, t, lo, hi):
    w = POOL_WINDOWS[g]
    left = w // 2
    right = w - 1 - left
    count = (jnp.minimum(t + right + 1, hi) - jnp.maximum(t - left, lo)).astype(F32)
    return _winsum(z, left, right, t, lo, hi) / count - z, count, left, right


def _pool_fwd(name, p, pool_w, pool_scale, rc):
    r = p.shape[0]

    def body(z_ref, w_ref, s_ref, y_ref):
        t, lo, hi = _segments((r, GC), rc)
        for g in range(N_GROUPS):
            cols = slice(g * GC, (g + 1) * GC)
            d, _, _, _ = _pool_parts(z_ref[:, cols], g, t, lo, hi)
            y_ref[:, cols] = (_dot(d, w_ref[g], _NN) * s_ref[:, cols]).astype(y_ref.dtype)

    return pl.pallas_call(
        body, name=name, grid=(1,),
        in_specs=[pl.BlockSpec((r, BR_W), lambda i: (0, OFF_POOL // BR_W)),
                  pl.BlockSpec((N_GROUPS, GC, GC), lambda i: (0, 0, 0)),
                  pl.BlockSpec((1, BR_W), lambda i: (0, 0))],
        out_specs=pl.BlockSpec((r, BR_W), lambda i: (0, 0)),
        out_shape=jax.ShapeDtypeStruct((r, BR_W), BF16),
        compiler_params=_cparams("arbitrary"),
    )(p, pool_w, pool_scale.reshape(1, BR_W))


def _pool_bwd(name, p, pool_w, pool_scale, dy, rc):
    r = p.shape[0]

    def body(z_ref, w_ref, s_ref, dy_ref, dz_ref, dw_ref, ds_ref):
        t, lo, hi = _segments((r, GC), rc)
        for g in range(N_GROUPS):
            cols = slice(g * GC, (g + 1) * GC)
            d, count, left, right = _pool_parts(z_ref[:, cols], g, t, lo, hi)
            dyg = dy_ref[:, cols]
            ds_ref[:, cols] = jnp.sum(dyg * _dot(d, w_ref[g], _NN), axis=0, keepdims=True)
            dlin = dyg * s_ref[:, cols]
            dw_ref[g] = _dot(d, dlin, _TN)
            dd = _dot(dlin, w_ref[g], _NT)
            dz_ref[:, cols] = (_winsum(dd / count, right, left, t, lo, hi) - dd).astype(dz_ref.dtype)

    return pl.pallas_call(
        body, name=name, grid=(1,),
        in_specs=[pl.BlockSpec((r, BR_W), lambda i: (0, OFF_POOL // BR_W)),
                  pl.BlockSpec((N_GROUPS, GC, GC), lambda i: (0, 0, 0)),
                  pl.BlockSpec((1, BR_W), lambda i: (0, 0)),
                  pl.BlockSpec((r, BR_W), lambda i: (0, 0))],
        out_specs=[pl.BlockSpec((r, BR_W), lambda i: (0, 0)),
                   pl.BlockSpec((N_GROUPS, GC, GC), lambda i: (0, 0, 0)),
                   pl.BlockSpec((1, BR_W), lambda i: (0, 0))],
        out_shape=[jax.ShapeDtypeStruct((r, BR_W), BF16), jax.ShapeDtypeStruct((N_GROUPS, GC, GC), F32),
                   jax.ShapeDtypeStruct((1, BR_W), F32)],
        compiler_params=_cparams("arbitrary"),
    )(p, pool_w, pool_scale.reshape(1, BR_W), dy)


def _f_sgu_v(pvg, lng, lnb):
    return _ln(jax.nn.gelu(pvg), lng, lnb)


def _sgu_fwd(name, p, ln_g, ln_b, sgu_w, sgu_b):
    r = p.shape[0]

    def body(pu_ref, pv_ref, g_ref, b_ref, w_ref, sb_ref, y_ref):
        vn = _f_sgu_v(pv_ref[...], g_ref[...], b_ref[...])
        u = jax.nn.gelu(pu_ref[...])
        for g in range(N_GROUPS):
            cols = slice(g * GC, (g + 1) * GC)
            s = _dot(w_ref[g], vn[:, cols], _NN) + sb_ref[g]
            y_ref[:, cols] = (u[:, cols] * s).astype(y_ref.dtype)

    return pl.pallas_call(
        body, name=name, grid=(r // SGU_CHUNK,),
        in_specs=[pl.BlockSpec((SGU_CHUNK, BR_W), lambda i: (i, OFF_U // BR_W)),
                  pl.BlockSpec((SGU_CHUNK, BR_W), lambda i: (i, OFF_VG // BR_W)),
                  pl.BlockSpec((1, BR_W), lambda i: (0, 0)), pl.BlockSpec((1, BR_W), lambda i: (0, 0)),
                  pl.BlockSpec((N_GROUPS, GC, GC), lambda i: (0, 0, 0)),
                  pl.BlockSpec((N_GROUPS, SGU_CHUNK, 1), lambda i: (0, 0, 0))],
        out_specs=pl.BlockSpec((SGU_CHUNK, BR_W), lambda i: (i, 0)),
        out_shape=jax.ShapeDtypeStruct((r, BR_W), BF16),
        compiler_params=_cparams("parallel"),
    )(p, p, ln_g.reshape(1, BR_W), ln_b.reshape(1, BR_W), sgu_w, sgu_b.reshape(N_GROUPS, SGU_CHUNK, 1))


def _sgu_bwd(name, p, ln_g, ln_b, sgu_w, sgu_b, dy):
    r = p.shape[0]

    def body(pu_ref, pv_ref, g_ref, b_ref, w_ref, sb_ref, dy_ref, dp_ref, dg_ref, db_ref, dw_ref, dsb_ref):
        i = pl.program_id(0)

        @pl.when(i == 0)
        def _():
            for ref in (dg_ref, db_ref, dw_ref, dsb_ref):
                ref[...] = jnp.zeros_like(ref)

        vn, vjp_v = jax.vjp(_f_sgu_v, pv_ref[...], g_ref[...], b_ref[...])
        u, vjp_u = jax.vjp(jax.nn.gelu, pu_ref[...])
        dy = dy_ref[...]
        du, dvn = [], []
        for g in range(N_GROUPS):
            cols = slice(g * GC, (g + 1) * GC)
            s = _dot(w_ref[g], vn[:, cols], _NN) + sb_ref[g]
            du.append(dy[:, cols] * s)
            ds = dy[:, cols] * u[:, cols]
            dsb_ref[g] += jnp.sum(ds, axis=1, keepdims=True)
            dw_ref[g] += _dot(ds, vn[:, cols], _NT)
            dvn.append(_dot(w_ref[g], ds, _TN))
        (dpu,) = vjp_u(jnp.concatenate(du, axis=1))
        dpv, dg, db = vjp_v(jnp.concatenate(dvn, axis=1))
        dp_ref[:, 0:BR_W] = dpu.astype(dp_ref.dtype)
        dp_ref[:, BR_W:2 * BR_W] = dpv.astype(dp_ref.dtype)
        dg_ref[...] += dg
        db_ref[...] += db

    vec = pl.BlockSpec((1, BR_W), lambda i: (0, 0))
    wsp = pl.BlockSpec((N_GROUPS, GC, GC), lambda i: (0, 0, 0))
    bsp = pl.BlockSpec((N_GROUPS, SGU_CHUNK, 1), lambda i: (0, 0, 0))
    return pl.pallas_call(
        body, name=name, grid=(r // SGU_CHUNK,),
        in_specs=[pl.BlockSpec((SGU_CHUNK, BR_W), lambda i: (i, OFF_U // BR_W)),
                  pl.BlockSpec((SGU_CHUNK, BR_W), lambda i: (i, OFF_VG // BR_W)),
                  vec, vec, wsp, bsp, pl.BlockSpec((SGU_CHUNK, BR_W), lambda i: (i, 0))],
        out_specs=[pl.BlockSpec((SGU_CHUNK, 2 * BR_W), lambda i: (i, 0)), vec, vec, wsp, bsp],
        out_shape=[jax.ShapeDtypeStruct((r, 2 * BR_W), BF16), jax.ShapeDtypeStruct((1, BR_W), F32),
                   jax.ShapeDtypeStruct((1, BR_W), F32), jax.ShapeDtypeStruct((N_GROUPS, GC, GC), F32),
                   jax.ShapeDtypeStruct((N_GROUPS, SGU_CHUNK, 1), F32)],
        compiler_params=_cparams("arbitrary"),
    )(p, p, ln_g.reshape(1, BR_W), ln_b.reshape(1, BR_W), sgu_w, sgu_b.reshape(N_GROUPS, SGU_CHUNK, 1), dy)


def _conv_w8(conv_w):
    return jnp.concatenate([conv_w, jnp.zeros((8 - conv_w.shape[0], conv_w.shape[1]), F32)], axis=0)


def _conv_fwd(name, p, conv_w, rc):
    r = p.shape[0]

    def body(cb_ref, cc_ref, cx_ref, w_ref, y_ref):
        t, lo, hi = _segments((r, GC), rc)
        z = cc_ref[...] * cx_ref[...]
        w = w_ref[...]
        c = _shifted(z, -1, t, lo, hi) * w[0:1] + z * w[1:2] + _shifted(z, 1, t, lo, hi) * w[2:3]
        y_ref[...] = (cb_ref[...] * c).astype(y_ref.dtype)

    nb = OFF_CB // GC
    return pl.pallas_call(
        body, name=name, grid=(N_GROUPS,),
        in_specs=[pl.BlockSpec((r, GC), lambda j: (0, nb + j)),
                  pl.BlockSpec((r, GC), lambda j: (0, nb + N_GROUPS + j)),
                  pl.BlockSpec((r, GC), lambda j: (0, nb + 2 * N_GROUPS + j)),
                  pl.BlockSpec((8, GC), lambda j: (0, j))],
        out_specs=pl.BlockSpec((r, GC), lambda j: (0, j)),
        out_shape=jax.ShapeDtypeStruct((r, BR_W), BF16),
        compiler_params=_cparams("parallel"),
    )(p, p, p, _conv_w8(conv_w))


def _conv_bwd(name, p, conv_w, dy, rc):
    r = p.shape[0]

    def body(cb_ref, cc_ref, cx_ref, w_ref, dy_ref, dcb_ref, dcc_ref, dcx_ref, dw_ref):
        t, lo, hi = _segments((r, GC), rc)
        cc, cx, w, dy = cc_ref[...], cx_ref[...], w_ref[...], dy_ref[...]
        z = cc * cx
        zp, zn = _shifted(z, -1, t, lo, hi), _shifted(z, 1, t, lo, hi)
        dcb_ref[...] = (dy * (zp * w[0:1] + z * w[1:2] + zn * w[2:3])).astype(dcb_ref.dtype)
        dc = dy * cb_ref[...]
        dw_ref[...] = jnp.concatenate(
            [jnp.sum(dc * zp, axis=0, keepdims=True), jnp.sum(dc * z, axis=0, keepdims=True),
             jnp.sum(dc * zn, axis=0, keepdims=True), jnp.zeros((5, GC), F32)], axis=0)
        dz = dc * w[1:2] + _shifted(dc, 1, t, lo, hi) * w[0:1] + _shifted(dc, -1, t, lo, hi) * w[2:3]
        dcc_ref[...] = (dz * cx).astype(dcc_ref.dtype)
        dcx_ref[...] = (dz * cc).astype(dcx_ref.dtype)

    nb = OFF_CB // GC
    return pl.pallas_call(
        body, name=name, grid=(N_GROUPS,),
        in_specs=[pl.BlockSpec((r, GC), lambda j: (0, nb + j)),
                  pl.BlockSpec((r, GC), lambda j: (0, nb + N_GROUPS + j)),
                  pl.BlockSpec((r, GC), lambda j: (0, nb + 2 * N_GROUPS + j)),
                  pl.BlockSpec((8, GC), lambda j: (0, j)),
                  pl.BlockSpec((r, GC), lambda j: (0, j))],
        out_specs=[pl.BlockSpec((r, GC), lambda j: (0, j))] * 3 + [pl.BlockSpec((8, GC), lambda j: (0, j))],
        out_shape=[jax.ShapeDtypeStruct((r, BR_W), BF16)] * 3 + [jax.ShapeDtypeStruct((8, BR_W), F32)],
        compiler_params=_cparams("parallel"),
    )(p, p, p, _conv_w8(conv_w), dy)


def _rope_tables(rc, n):
    rows = n // GRID_W
    row = jnp.repeat(jnp.arange(rows), GRID_W).astype(F32)
    col = jnp.tile(jnp.arange(GRID_W), rows).astype(F32)
    inv = ROPE_THETA ** (-jnp.arange(0, ROPE_AXIS_DIM, 2, dtype=F32) / ROPE_AXIS_DIM)
    ang_r, ang_c = row[:, None] * inv, col[:, None] * inv
    cos = jnp.concatenate([jnp.cos(ang_r), jnp.cos(ang_r), jnp.cos(ang_c), jnp.cos(ang_c)], axis=1)
    sin = jnp.concatenate([-jnp.sin(ang_r), jnp.sin(ang_r), -jnp.sin(ang_c), jnp.sin(ang_c)], axis=1)
    cos = jnp.concatenate([jnp.ones((rc, HEAD_DIM), F32), cos], axis=0)
    sin = jnp.concatenate([jnp.zeros((rc, HEAD_DIM), F32), sin], axis=0)
    return cos, sin


MOD_NAMES = ("sh1", "sc1", "g1", "sh2", "sc2", "g2")


def _local_step(xin, target, mod, comm, sp, rc, alpha):
    def carrying(fn):
        def call(name, *args, **kw):
            rider = comm.rider(name)
            if rider is None:
                return fn(name, *args, **kw)
            res, r_outs = fn(name, *args, rider=rider, **kw)
            comm.deliver(name, r_outs)
            return res
        return call

    mm, rows, attn_fwd, attn_bwd = carrying(_mm), carrying(_rows), carrying(_attn_fwd), carrying(_attn_bwd)
    r, d = xin.shape
    n_layers = mod.shape[0]
    tm_n, tm_w = 256, 128
    nbc_n, nbc_w = rc // tm_n, rc // tm_w
    cos, sin = _rope_tables(rc, r - rc)
    mp = mod.reshape(n_layers, 2, 6, 1, d)
    mods = [{nm: mp[i, :, j] for j, nm in enumerate(MOD_NAMES)} for i in range(n_layers)]
    f_ln_mod, f_ln_last = _make_f_ln(alpha, True), _make_f_ln(alpha, False)

    def whole(arr, roff=0):
        return (arr, 0, arr.shape[1], roff)

    (hb,) = rows("mod_in", _f_mod, r, tm_n, nbc_n, [whole(xin)], [mods[0]["sc1"], mods[0]["sh1"]],
                 [(r, d, BF16, 0)], [])
    saved = []
    x = xin
    for i in range(n_layers):
        last = i == n_layers - 1
        w, s, m = functools.partial(comm.weight, i), sp[i], mods[i]
        sv = {"x": x, "hb": hb}
        p = mm(f"l{i}_in", hb, w("in_t"), "nt", F32)
        gpre = mm(f"l{i}_gate", hb, w("gate_t"), "nt", F32)
        q, k, v = rows(f"l{i}_prep", _f_prep, r, tm_n, nbc_n,
                       [(p, 0, OFF_POOL, 0), whole(cos), whole(sin)], [_typed(s["q_norm_g"]), _typed(s["k_norm_g"])],
                       [(r, Q_W, BF16, 0), (r, KV_W, BF16, 0), (r, KV_W, BF16, 0)], [])
        ys = [attn_fwd(f"l{i}_attn", q, k, v, rc, not last),
              _pool_fwd(f"l{i}_pool", p, s["pool_w"], s["pool_scale"], rc),
              _sgu_fwd(f"l{i}_sgu", p, s["sgu_ln_g"], s["sgu_ln_b"], s["sgu_w"], s["sgu_b"]),
              _conv_fwd(f"l{i}_conv", p, s["conv_w"], rc)]
        ts = [mm(f"l{i}_br{kk}", ys[kk], w(f"br{kk}"), "nt", F32) for kk in range(N_BRANCH)]
        (mg,) = rows(f"l{i}_merge", _f_gate, r, tm_w, nbc_w, [whole(gpre)] + [whole(t) for t in ts],
                     [_typed(s["b_gate"])], [(r, d, BF16, 0)], [])
        o = mm(f"l{i}_o", mg, w("o"), "nn", F32)
        x1, h2b = rows(f"l{i}_ln1", f_ln_mod, r, tm_n, nbc_n, [whole(x), whole(o)],
                       [m["g1"], _typed(s["ln1_g"]), _typed(s["ln1_b"]), m["sc2"], m["sh2"]],
                       [(r, d, F32, 0), (r, d, BF16, 0)], [])
        af = mm(f"l{i}_ffg", h2b, w("ffg_t"), "nt", F32)
        bf = mm(f"l{i}_ffu", h2b, w("ffu_t"), "nt", F32)
        (f,) = rows(f"l{i}_swiglu", _f_swiglu, r, tm_w, nbc_w, [whole(af), whole(bf)], [],
                    [(r, af.shape[1], BF16, 0)], [])
        o2 = mm(f"l{i}_ffd", f, w("ffd"), "nn", F32)
        if last:
            (x2,) = rows(f"l{i}_ln2", f_ln_last, r, tm_n, nbc_n, [whole(x1), whole(o2)],
                         [m["g2"], _typed(s["ln2_g"]), _typed(s["ln2_b"])], [(r, d, F32, 0)], [])
            hb = None
        else:
            nx = mods[i + 1]
            x2, hb = rows(f"l{i}_ln2", f_ln_mod, r, tm_n, nbc_n, [whole(x1), whole(o2)],
                          [m["g2"], _typed(s["ln2_g"]), _typed(s["ln2_b"]), nx["sc1"], nx["sh1"]],
                          [(r, d, F32, 0), (r, d, BF16, 0)], [])
        sv.update(p=p, gpre=gpre, q=q, k=k, v=v, ys=ys, ts=ts, mg=mg, o=o, x1=x1, h2b=h2b, af=af, bf=bf, f=f, o2=o2)
        saved.append(sv)
        x = x2

    lat = jnp.concatenate([jnp.zeros((1, 1, 128), F32), jnp.ones((1, 1, 128), F32)], axis=0)

    def f_loss(xb, tb, msk):
        diff = (xb - tb) * msk[:, 0:1]
        part = jnp.sum(jnp.mean(jnp.square(diff), axis=-1, keepdims=True), axis=0, keepdims=True)
        return diff * (1.0 / d), jnp.broadcast_to(part, (1, 128))

    dx_direct, loss_acc = rows("loss", f_loss, r, tm_n, nbc_n, [whole(x), whole(target, nbc_n)], [lat],
                               [(r, d, F32, 0)], [128])
    loss = 0.5 * loss_acc[1, 0, 0]

    dmods = [dict() for _ in range(n_layers)]
    dsp = [dict() for _ in range(n_layers)]
    dh = None
    for i in reversed(range(n_layers)):
        last = i == n_layers - 1
        w, s, m, sv = functools.partial(comm.weight, i), sp[i], mods[i], saved[i]
        dm, dw, ds = dmods[i], {}, dsp[i]
        ln2 = [m["g2"], _typed(s["ln2_g"]), _typed(s["ln2_b"])]
        if last:
            res = rows(f"l{i}_ln2_bwd", _vjp_fn(f_ln_last, 2, 1), r, tm_n, nbc_n,
                       [whole(sv["x1"]), whole(sv["o2"]), whole(dx_direct)], ln2,
                       [(r, d, F32, 0), (r, d, BF16, 0)], [d, d, d])
            dx1, do2, dm["g2"], dlg, dlb = res
        else:
            nx = mods[i + 1]
            res = rows(f"l{i}_ln2_bwd", _vjp_fn(f_ln_mod, 2, 2), r, tm_n, nbc_n,
                       [whole(sv["x1"]), whole(sv["o2"]), whole(dx_direct), whole(dh)],
                       ln2 + [nx["sc1"], nx["sh1"]],
                       [(r, d, F32, 0), (r, d, BF16, 0)], [d, d, d, d, d])
            dx1, do2, dm["g2"], dlg, dlb, dmods[i + 1]["sc1"], dmods[i + 1]["sh1"] = res
        ds["ln2_g"], ds["ln2_b"] = dlg, dlb
        df = mm(f"l{i}_dF", do2, w("ffd"), "nt", F32)
        comm.grads(i, {"ffd": mm(f"l{i}_dWffd", sv["f"], do2, "tn", BF16)})
        dab, dbb = rows(f"l{i}_swiglu_bwd", _vjp_fn(_f_swiglu, 2, 1), r, tm_w, nbc_w,
                        [whole(sv["af"]), whole(sv["bf"]), whole(df)], [],
                        [(r, df.shape[1], BF16, 0), (r, df.shape[1], BF16, 0)], [])
        comm.grads(i, {"ffg_t": mm(f"l{i}_dWffg", dab, sv["h2b"], "tn", BF16)})
        comm.grads(i, {"ffu_t": mm(f"l{i}_dWffu", dbb, sv["h2b"], "tn", BF16)})
        dh2 = mm(f"l{i}_dh2a", dab, w("ffg_t"), "nn", F32)
        dh2 = mm(f"l{i}_dh2b", dbb, w("ffu_t"), "nn", F32, acc=dh2)
        res = rows(f"l{i}_ln1_bwd", _vjp_fn(f_ln_mod, 2, 2), r, tm_n, nbc_n,
                   [whole(sv["x"]), whole(sv["o"]), whole(dx1), whole(dh2)],
                   [m["g1"], _typed(s["ln1_g"]), _typed(s["ln1_b"]), m["sc2"], m["sh2"]],
                   [(r, d, F32, 0), (r, d, BF16, 0)], [d, d, d, d, d])
        dx_direct, do, dm["g1"], ds["ln1_g"], ds["ln1_b"], dm["sc2"], dm["sh2"] = res
        dmg = mm(f"l{i}_dMg", do, w("o"), "nt", F32)
        comm.grads(i, {"o": mm(f"l{i}_dWo", sv["mg"], do, "tn", BF16)})
        res = rows(f"l{i}_merge_bwd", _vjp_fn(_f_gate, 5, 1), r, tm_w, nbc_w,
                   [whole(sv["gpre"])] + [whole(t) for t in sv["ts"]] + [whole(dmg)], [_typed(s["b_gate"])],
                   [(r, N_BRANCH * d, BF16, 0)] + [(r, d, BF16, 0)] * N_BRANCH, [N_BRANCH * d])
        dgb, dts, ds["b_gate"] = res[0], res[1:1 + N_BRANCH], res[1 + N_BRANCH]
        dys = [mm(f"l{i}_dY{kk}", dts[kk], w(f"br{kk}"), "nn", F32) for kk in range(N_BRANCH)]
        for kk in range(N_BRANCH):
            comm.grads(i, {f"br{kk}": mm(f"l{i}_dWbr{kk}", dts[kk], sv["ys"][kk], "tn", BF16)})
        dq, dk, dv = attn_bwd(f"l{i}_attn_bwd", sv["q"], sv["k"], sv["v"], dys[0], rc, not last)
        res = rows(f"l{i}_prep_bwd", _vjp_fn(_f_prep, 3, 3, keep=(0, 3, 4)), r, tm_n, nbc_n,
                   [(sv["p"], 0, OFF_POOL, 0), whole(cos), whole(sin), whole(dq), whole(dk), whole(dv)],
                   [_typed(s["q_norm_g"]), _typed(s["k_norm_g"])],
                   [(r, OFF_POOL, BF16, 0)], [HEAD_DIM, HEAD_DIM])
        dp_qkv, ds["q_norm_g"], ds["k_norm_g"] = res
        dp_pool, ds["pool_w"], ds["pool_scale"] = _pool_bwd(f"l{i}_pool_bwd", sv["p"], s["pool_w"], s["pool_scale"],
                                                            dys[1], rc)
        dp_sgu, ds["sgu_ln_g"], ds["sgu_ln_b"], ds["sgu_w"], ds["sgu_b"] = _sgu_bwd(
            f"l{i}_sgu_bwd", sv["p"], s["sgu_ln_g"], s["sgu_ln_b"], s["sgu_w"], s["sgu_b"], dys[2])
        dp_cb, dp_cc, dp_cx, dcw = _conv_bwd(f"l{i}_conv_bwd", sv["p"], s["conv_w"], dys[3], rc)
        ds["conv_w"] = dcw[0:3]
        dpb = jnp.concatenate([dp_qkv, dp_pool, dp_sgu, dp_cb, dp_cc, dp_cx], axis=1)
        comm.grads(i, {"in_t": mm(f"l{i}_dWin", dpb, sv["hb"], "tn", BF16)})
        comm.grads(i, {"gate_t": mm(f"l{i}_dWgate", dgb, sv["hb"], "tn", BF16)})
        dh = mm(f"l{i}_dhb_a", dpb, w("in_t"), "nn", F32)
        dh = mm(f"l{i}_dhb_b", dgb, w("gate_t"), "nn", F32, acc=dh)

    def f_mod_bwd(xb, ddir, dhb, sc, sh):
        _, vjp = jax.vjp(_f_mod, xb, sc, sh)
        dxb, dsc, dsh = vjp(dhb)
        return dxb + ddir, dsc, dsh

    grad_x, dmods[0]["sc1"], dmods[0]["sh1"] = rows(
        "mod_in_bwd", f_mod_bwd, r, tm_n, nbc_n, [whole(xin), whole(dx_direct), whole(dh)],
        [mods[0]["sc1"], mods[0]["sh1"]], [(r - rc, d, F32, nbc_n)], [d, d])
    dmod = jnp.stack([jnp.concatenate([dmods[i][nm][:, 0, :] for nm in MOD_NAMES], axis=-1)
                      for i in range(n_layers)])
    for ds in dsp:
        for nm in ("ln1_g", "ln1_b", "ln2_g", "ln2_b", "b_gate", "q_norm_g", "k_norm_g"):
            ds[nm] = ds[nm][0, 0] + ds[nm][1, 0]
        ds["pool_scale"] = ds["pool_scale"].reshape(-1)
        ds["sgu_ln_g"] = ds["sgu_ln_g"].reshape(-1)
        ds["sgu_ln_b"] = ds["sgu_ln_b"].reshape(-1)
        ds["sgu_b"] = ds["sgu_b"].reshape(N_GROUPS, SGU_CHUNK)
    return loss, grad_x, dmod, dsp


def _exchange(name, srcs, scatter):
    n_items = len(srcs)
    out_shapes = [jax.ShapeDtypeStruct(s.shape if scatter else (N_DEV,) + s.shape, s.dtype) for s in srcs]

    def body(*refs):
        src, out = refs[:n_items], refs[n_items:2 * n_items]
        send_sems, recv_sems, loc_sems = refs[2 * n_items:]
        mx, my, mc = [lax.axis_index(a) for a in MESH_AXES]
        me = 4 * mx + 2 * my + mc
        peers = []
        for kk in range(1, N_DEV):
            px = 1 - mx if kk & 4 else mx
            py = 1 - my if kk & 2 else my
            pc = 1 - mc if kk & 1 else mc
            peers.append(((px, py, pc), 4 * px + 2 * py + pc))
        local, sent = [], []
        for a in range(n_items):
            loc = pltpu.make_async_copy(src[a].at[me] if scatter else src[a], out[a].at[me], loc_sems.at[a])
            loc.start()
            local.append(loc)
            for j, (peer, peer_l) in enumerate(peers):
                cp = pltpu.make_async_remote_copy(
                    src_ref=src[a].at[peer_l] if scatter else src[a], dst_ref=out[a].at[me],
                    send_sem=send_sems.at[a * (N_DEV - 1) + j], recv_sem=recv_sems.at[a * (N_DEV - 1) + j],
                    device_id=peer, device_id_type=pl.DeviceIdType.MESH)
                cp.start()
                sent.append(cp)
        for a in range(n_items):
            for j, (peer, peer_l) in enumerate(peers):
                pltpu.make_async_remote_copy(
                    src_ref=src[a].at[peer_l] if scatter else src[a], dst_ref=out[a].at[peer_l],
                    send_sem=send_sems.at[a * (N_DEV - 1) + j], recv_sem=recv_sems.at[a * (N_DEV - 1) + j],
                    device_id=peer, device_id_type=pl.DeviceIdType.MESH).wait_recv()
        for cp in sent:
            cp.wait_send()
        for loc in local:
            loc.wait()

    any_spec = pl.BlockSpec(memory_space=pl.ANY)
    res = pl.pallas_call(
        body, name=name,
        in_specs=[any_spec] * n_items, out_specs=[any_spec] * n_items, out_shape=out_shapes,
        scratch_shapes=[pltpu.SemaphoreType.DMA((n_items * (N_DEV - 1),)),
                        pltpu.SemaphoreType.DMA((n_items * (N_DEV - 1),)),
                        pltpu.SemaphoreType.DMA((n_items,))],
    )(*srcs)
    return list(res)


def _mesh_place():
    mx, my, mc = [lax.axis_index(a) for a in MESH_AXES]
    chips = [(1 - mx, my), (mx, 1 - my), (1 - mx, 1 - my)]

    def lid(px, py, pc):
        return 4 * px + 2 * py + pc

    return (mx, my, mc), (mx, my, 1 - mc), chips, lid


def _rcopy(src, dst, send_sems, recv_sems, k, to):
    return pltpu.make_async_remote_copy(src_ref=src, dst_ref=dst, send_sem=send_sems.at[k], recv_sem=recv_sems.at[k],
                                        device_id=to, device_id_type=pl.DeviceIdType.MESH)


def _sem_scratch(*sizes):
    return [pltpu.SemaphoreType.DMA((s,)) for s in sizes]


def _gather_rider(src, rows, buf=None):
    r0, r1 = rows
    win = pl.ds(r0, r1 - r0)

    def start(ins, outs, sems):
        send, recv, loc = sems
        (mx, my, mc), sib, chips, lid = _mesh_place()
        mine, dst = ins[0].at[win], outs[0].at[lid(mx, my, mc), win]
        pltpu.make_async_copy(mine, dst, loc.at[0]).start()
        _rcopy(mine, dst, send, recv, 0, sib).start()
        for j, chip in enumerate(chips):
            _rcopy(mine, dst, send, recv, 1 + j, (*chip, mc)).start()

    def finish(ins, outs, sems):
        send, recv, loc = sems
        (mx, my, mc), sib, chips, lid = _mesh_place()
        mine, dst = ins[0].at[win], outs[0].at[lid(mx, my, mc), win]
        for j, chip in enumerate(chips):
            blk = outs[0].at[lid(*chip, mc), win]
            _rcopy(mine, blk, send, recv, 1 + j, (*chip, mc)).wait_recv()
            _rcopy(blk, blk, send, recv, 4 + j, sib).start()
        _rcopy(mine, outs[0].at[lid(*sib), win], send, recv, 0, sib).wait_recv()
        for j, chip in enumerate(chips):
            _rcopy(mine, outs[0].at[lid(*chip, 1 - mc), win], send, recv, 4 + j, sib).wait_recv()
        for t in range(7):
            _rcopy(mine, dst, send, recv, t, sib).wait_send()
        pltpu.make_async_copy(mine, dst, loc.at[0]).wait()

    out_shape = jax.ShapeDtypeStruct((N_DEV,) + src.shape, src.dtype)
    if buf is None:
        return _Rider([src], [out_shape], _sem_scratch(7, 7, 1), start, finish)
    return _Rider([src, buf], [out_shape], _sem_scratch(7, 7, 1), start, finish, aliases={1: 0})


def _sibling_rider(part):
    def start(ins, outs, sems):
        send, recv = sems
        (mx, my, mc), sib, chips, lid = _mesh_place()
        for t, slab in enumerate([lid(*sib)] + [lid(*chip, 1 - mc) for chip in chips]):
            _rcopy(ins[0].at[slab], outs[0].at[t], send, recv, t, sib).start()

    def finish(ins, outs, sems):
        send, recv = sems
        _, sib, _, _ = _mesh_place()
        for t in range(4):
            cp = _rcopy(ins[0].at[0], outs[0].at[t], send, recv, t, sib)
            cp.wait_recv()
            cp.wait_send()

    return _Rider([part], [jax.ShapeDtypeStruct((4,) + part.shape[1:], part.dtype)], _sem_scratch(4, 4), start, finish)


def _chips_rider(pair, rows, buf=None):
    r0, r1 = rows
    win = pl.ds(r0, r1 - r0)

    def start(ins, outs, sems):
        send, recv = sems
        (mx, my, mc), sib, chips, lid = _mesh_place()
        for j, chip in enumerate(chips):
            _rcopy(ins[0].at[j, win], outs[0].at[j, win], send, recv, j, (*chip, mc)).start()

    def finish(ins, outs, sems):
        send, recv = sems
        (mx, my, mc), sib, chips, lid = _mesh_place()
        for j, chip in enumerate(chips):
            cp = _rcopy(ins[0].at[j, win], outs[0].at[j, win], send, recv, j, (*chip, mc))
            cp.wait_recv()
            cp.wait_send()

    out_shape = jax.ShapeDtypeStruct(pair.shape, pair.dtype)
    if buf is None:
        return _Rider([pair], [out_shape], _sem_scratch(3, 3), start, finish)
    return _Rider([pair, buf], [out_shape], _sem_scratch(3, 3), start, finish, aliases={1: 0})


def _run_rider(name, rider):
    n_in, n_out = len(rider.inputs), len(rider.out_shapes)

    def body(*refs):
        ins, outs, sems = refs[:n_in], refs[n_in:n_in + n_out], refs[n_in + n_out:]
        rider.start(ins, outs, sems)
        rider.finish(ins, outs, sems)

    any_spec = pl.BlockSpec(memory_space=pl.ANY)
    res = pl.pallas_call(body, name=name, in_specs=[any_spec] * n_in, out_specs=[any_spec] * n_out,
                         out_shape=rider.out_shapes, scratch_shapes=rider.scratch,
                         input_output_aliases=rider.aliases)(*rider.inputs)
    return list(res)


def _slab_ids():
    (mx, my, mc), _, chips, lid = _mesh_place()
    return jnp.stack([lid(*chip, mc) for chip in chips] + [lid(mx, my, mc)]).astype(jnp.int32)


def _pair_sum(name, part, rsib, ids):
    _, n, k = part.shape
    tr = _row_tile(n, 512, 16)

    def body(ids_ref, p_ref, r_ref, o_ref):
        o_ref[...] = (p_ref[...].astype(F32) + r_ref[...].astype(F32)).astype(o_ref.dtype)

    grid_spec = pltpu.PrefetchScalarGridSpec(
        num_scalar_prefetch=1, grid=(3, n // tr),
        in_specs=[pl.BlockSpec((None, tr, k), lambda j, i, ids: (ids[j], i, 0)),
                  pl.BlockSpec((None, tr, k), lambda j, i, ids: (1 + j, i, 0))],
        out_specs=pl.BlockSpec((None, tr, k), lambda j, i, ids: (j, i, 0)))
    return pl.pallas_call(body, name=name, grid_spec=grid_spec, out_shape=jax.ShapeDtypeStruct((3, n, k), part.dtype),
                          compiler_params=_cparams("parallel", "parallel"))(ids, part, rsib)


def _sum5(name, part, rsib, rici, ids):
    _, n, k = part.shape
    tr = _row_tile(n, 512, 16)

    def body(ids_ref, p_ref, r_ref, c_ref, o_ref):
        acc = p_ref[...].astype(F32) + r_ref[...].astype(F32)
        for j in range(3):
            acc = acc + c_ref[j].astype(F32)
        o_ref[...] = acc

    grid_spec = pltpu.PrefetchScalarGridSpec(
        num_scalar_prefetch=1, grid=(n // tr,),
        in_specs=[pl.BlockSpec((None, tr, k), lambda i, ids: (ids[3], i, 0)),
                  pl.BlockSpec((None, tr, k), lambda i, ids: (0, i, 0)),
                  pl.BlockSpec((3, tr, k), lambda i, ids: (0, i, 0))],
        out_specs=pl.BlockSpec((tr, k), lambda i, ids: (i, 0)))
    return pl.pallas_call(body, name=name, grid_spec=grid_spec, out_shape=jax.ShapeDtypeStruct((n, k), F32),
                          compiler_params=_cparams("parallel"))(ids, part, rsib, rici)


W_KEYS = ("in_t", "gate_t", "br0", "br1", "br2", "br3", "o", "ffg_t", "ffu_t", "ffd")


CARRIER_US = {"mod_in": 12, "in": 55, "gate": 95, "prep": 19, "attn": 118, "br0": 15, "merge": 50, "o": 25, "ln1": 27,
              "ffg": 66, "ffu": 66, "swiglu": 42, "ffd": 73, "ln2": 27, "loss": 20, "ln2_bwd": 44, "dF": 70,
              "dWffd": 64, "swiglu_bwd": 66, "dh2a": 75, "dh2b": 75, "dWffg": 64, "dWffu": 64, "ln1_bwd": 44,
              "dMg": 25, "dWo": 25, "merge_bwd": 80, "dY0": 14, "dWbr0": 15, "attn_bwd": 195, "prep_bwd": 28,
              "dWin": 54, "dWgate": 95, "dhb_a": 64, "dhb_b": 115, "mod_in_bwd": 24}
ICI_US_PER_MIB = 45.0
D2D_US_PER_MIB = 6.8
MIN_CHUNK_US = 10.0
MIN_CARRIER_US = 25.0
MIN_CARRIER_BWD_US = 55.0


class _Comm:
    def __init__(self, wsrc):
        self.wsrc = wsrc
        self.n_layers = len(wsrc)
        self.queue = []
        self.riding = {}
        self.buf, self.left = {}, {}
        self.part, self.rsib, self.pair = {}, {}, {}
        self.ids = _slab_ids()
        for i in range(self.n_layers):
            for k in W_KEYS:
                self._push_chunks("gather", ("w", i, k), wsrc[i][k].shape, wsrc[i][k].dtype)

    def _push_chunks(self, kind, item, shape, dtype):
        n, k = shape[-2], shape[-1]
        us = n * k * jnp.dtype(dtype).itemsize / 2 ** 20 * ICI_US_PER_MIB
        pieces = max(1, int(us // MIN_CHUNK_US))
        while n % (16 * pieces):
            pieces -= 1
        step = n // pieces
        self.left[item] = pieces
        for c in range(pieces):
            self.queue.append(dict(kind=kind, item=item, rows=(c * step, (c + 1) * step), us=us / pieces))

    @staticmethod
    def _merge(units, u):
        v = units[-1] if units else None
        if not (v and v["item"] == u["item"] and v["kind"] == u["kind"] and u["rows"] and v["rows"][1] == u["rows"][0]):
            return False
        v.update(rows=(v["rows"][0], u["rows"][1]), us=v["us"] + u["us"], count=v.get("count", 1) + u.get("count", 1))
        return True

    def _unit_rider(self, u):
        item = u["item"]
        if u["kind"] == "gather":
            src = self.wsrc[item[1]][item[2]] if item[0] == "w" else self.part[item]
            return _gather_rider(src, u["rows"], self.buf.get(item))
        if u["kind"] == "sibling":
            return _sibling_rider(self.part[item])
        return _chips_rider(self.pair[item], u["rows"], self.buf.get(item))

    def _done(self, u, out):
        item = u["item"]
        if u["kind"] == "sibling":
            self.rsib[item] = out
            self.pair[item] = _pair_sum(f"pair_l{item[1]}_{item[2]}", self.part[item], out, self.ids)
            self._push_chunks("chips", item, self.pair[item].shape, self.pair[item].dtype)
            return
        self.buf[item] = out
        self.left[item] -= u.get("count", 1)

    def _send(self, name, units, call):
        outs = call(_compose([self._unit_rider(u) for u in units]))
        for u, o in zip(units, outs):
            self._done(u, o)

    def rider(self, name, budget_us=None):
        budget = CARRIER_US.get(name.split("_", 1)[1] if name[0] == "l" and name[1].isdigit() else name, 0) \
            if budget_us is None else budget_us
        forward = bool(self.queue) and self.queue[0]["item"][0] == "w"
        if budget < (MIN_CARRIER_US if forward or budget_us is not None else MIN_CARRIER_BWD_US):
            return None
        units, used = [], 0.0
        while self.queue and used + self.queue[0]["us"] <= budget:
            u = self.queue[0]
            if not self._merge(units, u):
                if any(v["item"] == u["item"] for v in units):
                    break
                units.append(dict(u))
            used += u["us"]
            del self.queue[0]
        if not units:
            return None
        self.riding[name] = units
        return _compose([self._unit_rider(u) for u in units])

    def deliver(self, name, outs):
        for u, o in zip(self.riding.pop(name), outs):
            self._done(u, o)

    def _flush(self, item, kinds):
        hits = [p for p, u in enumerate(self.queue) if u["item"] == item and u["kind"] in kinds]
        if not hits:
            return
        prefix = self.queue[:hits[-1] + 1]
        del self.queue[:hits[-1] + 1]
        units = []
        for u in prefix:
            if not self._merge(units, u):
                units.append(dict(u))
        tag = "_".join(str(t) for t in item) + "_" + kinds[0]
        batches = [[]]
        for u in units:
            if any(v["item"] == u["item"] for v in batches[-1]):
                batches.append([])
            batches[-1].append(u)
        for b, batch in enumerate(batches):
            self._send(None, batch, functools.partial(_run_rider, f"alone_{tag}_{b}"))

    def begin(self):
        self._flush(("w", 0, "in_t"), ("gather",))

    def weight(self, i, k):
        item = ("w", i, k)
        self._flush(item, ("gather",))
        o = self.buf[item]
        return o.reshape(-1, o.shape[-1])

    def grads(self, i, group):
        for k, g in group.items():
            item = ("g", i, k)
            self.part[item] = g.reshape(N_DEV, g.shape[0] // N_DEV, g.shape[1])
            us = g.size // N_DEV * g.dtype.itemsize / 2 ** 20 * D2D_US_PER_MIB
            self.queue.append(dict(kind="sibling", item=item, rows=None, us=us))

    def total(self, i, k):
        item = ("g", i, k)
        self._flush(item, ("sibling",))
        self._flush(item, ("chips",))
        return _sum5(f"sum_l{i}_{k}", self.part[item], self.rsib[item], self.buf[item], self.ids)

    def gather_small(self, name, arr):
        item = ("s", name)
        self.part[item] = arr
        waiting, self.queue = self.queue, []
        self._push_chunks("gather", item, arr.shape, arr.dtype)
        self.queue += waiting

    def gathered(self, name):
        item = ("s", name)
        self._flush(item, ("gather",))
        return self.buf[item]


def _row_tile(n, pref, mult):
    best = None
    t = mult
    while t <= min(n, pref):
        if n % t == 0:
            best = t
        t += mult
    return best if best is not None else n


def _sum8(name, slabs):
    _, n, k = slabs.shape
    tr = _row_tile(n, 128, 16)

    def body(s_ref, o_ref):
        acc = s_ref[0].astype(F32)
        for j in range(1, N_DEV):
            acc = acc + s_ref[j].astype(F32)
        o_ref[...] = acc

    return pl.pallas_call(
        body, name=name, grid=(n // tr,),
        in_specs=[pl.BlockSpec((N_DEV, tr, k), lambda i: (0, i, 0))],
        out_specs=pl.BlockSpec((tr, k), lambda i: (i, 0)),
        out_shape=jax.ShapeDtypeStruct((n, k), F32),
        compiler_params=_cparams("parallel"),
    )(slabs)


def _adamw(name, w, g, m, v, rider=None):
    n, k = w.shape[-2:]
    tr = _row_tile(n, 256, 8)

    def body(w_ref, g_ref, m_ref, v_ref, d_ref, m2_ref, v2_ref):
        gv = g_ref[...]
        m2 = ADAM_B1 * m_ref[...] + (1.0 - ADAM_B1) * gv
        v2 = ADAM_B2 * v_ref[...] + (1.0 - ADAM_B2) * jnp.square(gv)
        m_hat = m2 / (1.0 - ADAM_B1 ** ADAM_STEP)
        v_hat = v2 / (1.0 - ADAM_B2 ** ADAM_STEP)
        d_ref[...] = -ADAM_LR * (m_hat / (jnp.sqrt(v_hat) + ADAM_EPS) + ADAM_WD * w_ref[...])
        m2_ref[...] = m2
        v2_ref[...] = v2

    if w.ndim == 2:
        grid, spec = (n // tr,), pl.BlockSpec((tr, k), lambda i: (i, 0))
    else:
        grid, spec = (w.shape[0], n // tr), pl.BlockSpec((None, tr, k), lambda l, i: (l, i, 0))
    outs, r_outs = _pcall(name, body, grid, [spec] * 4, [spec] * 3, [jax.ShapeDtypeStruct(w.shape, F32)] * 3,
                          (w, g, m, v), ("parallel",) * len(grid), rider=rider)
    return outs if rider is None else (outs, r_outs)


def _pack(arrs):
    flat = jnp.concatenate([a.reshape(-1).astype(F32) for a in arrs])
    pad = (-flat.shape[0]) % 2048
    if pad:
        flat = jnp.concatenate([flat, jnp.zeros((pad,), F32)])
    return flat.reshape(-1, 128)


def _unpack(packed, shapes):
    flat = packed.reshape(-1)
    out, off = [], 0
    for shp in shapes:
        size = math.prod(shp)
        out.append(flat[off:off + size].reshape(shp))
        off += size
    return out


WEIGHT_NAMES = ("c_ctx", "w_ada", "b_ada", "w_in", "q_norm_g", "k_norm_g", "pool_w", "pool_scale", "sgu_ln_g",
                "sgu_ln_b", "sgu_w", "sgu_b", "conv_w", "w_br_attn", "w_br_pool", "w_br_sgu", "w_br_conv", "w_gate",
                "b_gate", "w_o", "ln1_g", "ln1_b", "w_ff_gate", "w_ff_up", "w_ff_down", "ln2_g", "ln2_b")
COL_SHARDED = {"w_in": "in_t", "w_gate": "gate_t", "w_ff_gate": "ffg_t", "w_ff_up": "ffu_t",
               "w_br_attn": "br0", "w_br_pool": "br1", "w_br_sgu": "br2", "w_br_conv": "br3"}
ROW_SHARDED = {"w_o": "o", "w_ff_down": "ffd"}
LAYER_SMALL = ("q_norm_g", "k_norm_g", "pool_w", "pool_scale", "sgu_ln_g", "sgu_ln_b", "sgu_w", "sgu_b", "b_gate",
               "ln1_g", "ln1_b", "ln2_g", "ln2_b")
SMALL_ORDER = ("c_ctx", "b_ada") + LAYER_SMALL + ("conv_w",)


def _train_step(a):
    n_layers, d = a["w_in"].shape[0], a["x"].shape[-1]
    rc = a["ctx"].shape[1]
    alpha = (2 * n_layers) ** 0.25
    mx, my, mc = [lax.axis_index(ax) for ax in MESH_AXES]
    me = 4 * mx + 2 * my + mc
    ada_w = a["w_ada"].shape[-1]
    cw_loc = a["conv_w"].shape[-1]

    n_c, n_cw = d, n_layers * 3 * cw_loc
    got = _exchange("gather_cond", [_pack([a["c"], a["conv_w"]])], False)[0].reshape(N_DEV, -1)
    c_all = got[:, :n_c]
    conv_w = got[:, n_c:n_c + n_cw].reshape(N_DEV, n_layers, 3, cw_loc).transpose(1, 2, 0, 3).reshape(n_layers, 3, -1)
    cond = jnp.concatenate([c_all, a["c_ctx"][None], jnp.zeros((16 - N_DEV - 1, d), F32)], axis=0)
    sil, sil_vjp = jax.vjp(jax.nn.silu, cond)
    sil = sil.astype(BF16)

    mod_cols = jnp.concatenate([_mm(f"ada{i}", sil, a["w_ada"][i], "nn", F32) for i in range(n_layers)], axis=0)
    got = _exchange("gather_mod", [mod_cols], False)[0]
    mod_all = got.reshape(N_DEV, n_layers, 16, ada_w).transpose(1, 2, 0, 3).reshape(n_layers, 16, -1)
    mod_all = mod_all + a["b_ada"][:, None, :]
    mod = jnp.stack([mod_all[:, N_DEV], lax.dynamic_index_in_dim(mod_all, me, axis=1, keepdims=False)], axis=1)

    comm = _Comm([{**{key: a[nm][i].T.astype(BF16) for nm, key in COL_SHARDED.items()},
                   **{key: a[nm][i].astype(BF16) for nm, key in ROW_SHARDED.items()}} for i in range(n_layers)])
    comm.begin()
    sp = [{nm: a[nm][i] for nm in LAYER_SMALL} for i in range(n_layers)]
    for i in range(n_layers):
        sp[i]["conv_w"] = conv_w[i]

    xin = jnp.concatenate([a["ctx"][0], a["x"][0]], axis=0)
    loss_l, grad_x, dmod, dsp = _local_step(xin, a["loss_target"][0], mod, comm, sp, rc, alpha)
    loss = lax.psum(loss_l, MESH_AXES)
    grads = {}

    def big_grad(nm):
        key = COL_SHARDED.get(nm, ROW_SHARDED.get(nm))
        per_layer = [comm.total(i, key) for i in range(n_layers)]
        return jnp.stack([t.T for t in per_layer] if nm in COL_SHARDED else per_layer)

    small_parts = [dmod[:, 0], dmod[:, 1]]
    small_shapes = [dmod[:, 0].shape, dmod[:, 1].shape]
    for nm in LAYER_SMALL + ("conv_w",):
        part = jnp.stack([dsp[i][nm] for i in range(n_layers)])
        small_parts.append(part)
        small_shapes.append(part.shape)
    comm.gather_small("lat", _pack([dmod[:, 1]]))
    comm.gather_small("small", _pack(small_parts))

    delta, new_m, new_v = {}, {}, {}

    def adamw(nm):
        name = f"adamw_{nm}"
        rider = comm.rider(name, budget_us=a[nm].size * 28 / 3.3e6)
        res = _adamw(name, a[nm], grads[nm], a["m_" + nm], a["v_" + nm], rider=rider)
        if rider is not None:
            res, r_outs = res
            comm.deliver(name, r_outs)
        delta[nm], new_m[nm], new_v[nm] = res

    for nm in ("w_ff_down", "w_ff_gate", "w_ff_up", "w_o", "w_br_attn", "w_br_pool", "w_br_sgu", "w_br_conv"):
        grads[nm] = big_grad(nm)
        adamw(nm)

    got_lat, got_small = comm.gathered("lat"), comm.gathered("small")
    tot = _unpack(_sum8("sum_small", got_small), small_shapes)
    dmod_c, dmod_lat_sum = tot[0], tot[1]
    for nm, g in zip(LAYER_SMALL + ("conv_w",), tot[2:]):
        grads[nm] = g
    grads["conv_w"] = lax.dynamic_slice_in_dim(grads["conv_w"], me * cw_loc, cw_loc, axis=2)
    grads["b_ada"] = dmod_c + dmod_lat_sum
    dmod_lat_all = got_lat.reshape(N_DEV, -1)[:, :n_layers * 6 * d].reshape(N_DEV, n_layers, 6 * d)
    dm_rows = jnp.concatenate([dmod_lat_all.transpose(1, 0, 2), dmod_c[:, None, :],
                               jnp.zeros((n_layers, 16 - N_DEV - 1, 6 * d), F32)], axis=1)
    dm_cols = lax.dynamic_slice_in_dim(dm_rows, me * ada_w, ada_w, axis=2).astype(BF16)
    grads["w_ada"] = jnp.stack([_mm(f"dWada{i}", sil, dm_cols[i], "tn", F32) for i in range(n_layers)])
    dsil = None
    for i in range(n_layers):
        dsil = _mm(f"dsil{i}", dm_cols[i], a["w_ada"][i], "nt", F32, acc=dsil)
    got = _exchange("gather_dsil", [dsil], False)[0]
    dsil = _sum8("sum_dsil", got)
    grads["c_ctx"] = sil_vjp(dsil)[0][N_DEV]

    adamw("w_ada")
    for nm in ("w_in", "w_gate"):
        grads[nm] = big_grad(nm)
        adamw(nm)
    shapes = [a[nm].shape for nm in SMALL_ORDER]
    res = _adamw("adamw_small", _pack([a[nm] for nm in SMALL_ORDER]), _pack([grads[nm] for nm in SMALL_ORDER]),
                 _pack([a["m_" + nm] for nm in SMALL_ORDER]), _pack([a["v_" + nm] for nm in SMALL_ORDER]))
    for tree, packed in zip((delta, new_m, new_v), res):
        for nm, t in zip(SMALL_ORDER, _unpack(packed, shapes)):
            tree[nm] = t
    return (loss, grad_x[None], *[grads[nm] for nm in WEIGHT_NAMES], *[delta[nm] for nm in WEIGHT_NAMES],
            *[new_m[nm] for nm in WEIGHT_NAMES], *[new_v[nm] for nm in WEIGHT_NAMES])


def kernel(x, c, ctx, c_ctx, w_ada, b_ada, w_in, q_norm_g, k_norm_g, pool_w, pool_scale, sgu_ln_g, sgu_ln_b, sgu_w, sgu_b, conv_w, w_br_attn, w_br_pool, w_br_sgu, w_br_conv, w_gate, b_gate, w_o, ln1_g, ln1_b, w_ff_gate, w_ff_up, w_ff_down, ln2_g, ln2_b, loss_target, m_c_ctx, m_w_ada, m_b_ada, m_w_in, m_q_norm_g, m_k_norm_g, m_pool_w, m_pool_scale, m_sgu_ln_g, m_sgu_ln_b, m_sgu_w, m_sgu_b, m_conv_w, m_w_br_attn, m_w_br_pool, m_w_br_sgu, m_w_br_conv, m_w_gate, m_b_gate, m_w_o, m_ln1_g, m_ln1_b, m_w_ff_gate, m_w_ff_up, m_w_ff_down, m_ln2_g, m_ln2_b, v_c_ctx, v_w_ada, v_b_ada, v_w_in, v_q_norm_g, v_k_norm_g, v_pool_w, v_pool_scale, v_sgu_ln_g, v_sgu_ln_b, v_sgu_w, v_sgu_b, v_conv_w, v_w_br_attn, v_w_br_pool, v_w_br_sgu, v_w_br_conv, v_w_gate, v_b_gate, v_w_o, v_ln1_g, v_ln1_b, v_w_ff_gate, v_w_ff_up, v_w_ff_down, v_ln2_g, v_ln2_b):
    names = list(WEIGHT_NAMES)
    args = dict(zip(
        ["x", "c", "ctx"] + names + ["loss_target"] + ["m_" + n for n in names] + ["v_" + n for n in names],
        (x, c, ctx, c_ctx, w_ada, b_ada, w_in, q_norm_g, k_norm_g, pool_w, pool_scale, sgu_ln_g, sgu_ln_b, sgu_w, sgu_b, conv_w, w_br_attn, w_br_pool, w_br_sgu, w_br_conv, w_gate, b_gate, w_o, ln1_g, ln1_b, w_ff_gate, w_ff_up, w_ff_down, ln2_g, ln2_b, loss_target, m_c_ctx, m_w_ada, m_b_ada, m_w_in, m_q_norm_g, m_k_norm_g, m_pool_w, m_pool_scale, m_sgu_ln_g, m_sgu_ln_b, m_sgu_w, m_sgu_b, m_conv_w, m_w_br_attn, m_w_br_pool, m_w_br_sgu, m_w_br_conv, m_w_gate, m_b_gate, m_w_o, m_ln1_g, m_ln1_b, m_w_ff_gate, m_w_ff_up, m_w_ff_down, m_ln2_g, m_ln2_b, v_c_ctx, v_w_ada, v_b_ada, v_w_in, v_q_norm_g, v_k_norm_g, v_pool_w, v_pool_scale, v_sgu_ln_g, v_sgu_ln_b, v_sgu_w, v_sgu_b, v_conv_w, v_w_br_attn, v_w_br_pool, v_w_br_sgu, v_w_br_conv, v_w_gate, v_b_gate, v_w_o, v_ln1_g, v_ln1_b, v_w_ff_gate, v_w_ff_up, v_w_ff_down, v_ln2_g, v_ln2_b)))
    return _train_step(args)
```

```python
import functools
import math

import jax
import jax.numpy as jnp
from jax import lax
from jax.experimental import pallas as pl
from jax.experimental.pallas import tpu as pltpu

F32 = jnp.float32
BF16 = jnp.bfloat16

N_DEV = 8
MESH_AXES = ("x", "y", "c")
V7X_VMEM_LIMIT_BYTES = 56 * 1024 * 1024

GRID_W = 64
HEAD_DIM = 128
N_HEADS = 8
N_KV_HEADS = 2
KV_GROUP = N_HEADS // N_KV_HEADS
Q_W = N_HEADS * HEAD_DIM
KV_W = N_KV_HEADS * HEAD_DIM
ROPE_THETA = 10000.0
ROPE_AXIS_DIM = HEAD_DIM // 2
POOL_WINDOWS = (2, 4, 8, 16)
GC = 128
N_GROUPS = 4
BR_W = N_GROUPS * GC
SGU_CHUNK = 128
N_BRANCH = 4
LN_EPS = 1e-5
RMS_EPS = 1e-6
OFF_K = Q_W
OFF_V = OFF_K + KV_W
OFF_POOL = OFF_V + KV_W
OFF_U = OFF_POOL + BR_W
OFF_VG = OFF_U + BR_W
OFF_CB = OFF_VG + BR_W
OFF_CC = OFF_CB + BR_W
OFF_CX = OFF_CC + BR_W
IN_W = OFF_CX + BR_W
ATT_SCALE = HEAD_DIM ** -0.5

ADAM_LR = 0.001
ADAM_B1 = 0.9
ADAM_B2 = 0.999
ADAM_EPS = 1e-08
ADAM_WD = 0.01
ADAM_STEP = 10

_NT = (((1,), (1,)), ((), ()))
_NN = (((1,), (0,)), ((), ()))
_TN = (((0,), (0,)), ((), ()))
_DIMS = {"nt": _NT, "nn": _NN, "tn": _TN}


def _cparams(*sem):
    return pltpu.CompilerParams(dimension_semantics=sem, vmem_limit_bytes=V7X_VMEM_LIMIT_BYTES)


def _tile(dim, pref):
    best = None
    t = 128
    while t <= min(dim, pref):
        if dim % t == 0:
            best = t
        t += 128
    return best if best is not None else dim


def _dot(a, b, dims):
    return lax.dot_general(a.astype(BF16), b.astype(BF16), dims, preferred_element_type=F32)


class _Rider:
    def __init__(self, inputs, out_shapes, scratch, start, finish, aliases=None):
        self.inputs, self.out_shapes, self.scratch = list(inputs), list(out_shapes), list(scratch)
        self.start, self.finish = start, finish
        self.aliases = dict(aliases or {})


def _compose(riders):
    inputs, outs, scratch, aliases, spans = [], [], [], {}, []
    for rd in riders:
        i0, o0, s0 = len(inputs), len(outs), len(scratch)
        aliases.update({i0 + p: o0 + q for p, q in rd.aliases.items()})
        inputs += rd.inputs
        outs += rd.out_shapes
        scratch += rd.scratch
        spans.append((slice(i0, len(inputs)), slice(o0, len(outs)), slice(s0, len(scratch))))

    def start(ins, os, sems):
        for rd, (si, so, ss) in zip(riders, spans):
            rd.start(ins[si], os[so], sems[ss])

    def finish(ins, os, sems):
        for rd, (si, so, ss) in zip(riders, spans):
            rd.finish(ins[si], os[so], sems[ss])

    return _Rider(inputs, outs, scratch, start, finish, aliases)


def _pcall(name, body, grid, in_specs, out_specs, out_shape, args, sem, scratch=(), rider=None, prefetch=None,
           aliases=None):
    in_specs, out_specs, out_shape, scratch = list(in_specs), list(out_specs), list(out_shape), list(scratch)
    n_pre = 0 if prefetch is None else 1
    n_in, n_out, n_scr = len(in_specs), len(out_specs), len(scratch)
    r_in, r_out = (len(rider.inputs), len(rider.out_shapes)) if rider is not None else (0, 0)
    any_spec = pl.BlockSpec(memory_space=pl.ANY)
    io_aliases = {n_pre + p: q for p, q in (aliases or {}).items()}
    if rider is not None:
        io_aliases.update({n_pre + n_in + p: n_out + q for p, q in rider.aliases.items()})
        in_specs, out_specs = in_specs + [any_spec] * r_in, out_specs + [any_spec] * r_out
        out_shape, scratch = out_shape + rider.out_shapes, scratch + rider.scratch
        args, sem = (*args, *rider.inputs), ["arbitrary"] * len(grid)

    def wrapped(*refs):
        pre, refs = refs[:n_pre], refs[n_pre:]
        if rider is None:
            return body(*pre, *refs)
        ins, refs = refs[:n_in], refs[n_in:]
        r_ins, refs = refs[:r_in], refs[r_in:]
        outs, refs = refs[:n_out], refs[n_out:]
        r_outs, refs = refs[:r_out], refs[r_out:]
        scr, r_scr = refs[:n_scr], refs[n_scr:]
        first = functools.reduce(jnp.logical_and, [pl.program_id(ax) == 0 for ax in range(len(grid))])
        last = functools.reduce(jnp.logical_and, [pl.program_id(ax) == grid[ax] - 1 for ax in range(len(grid))])

        @pl.when(first)
        def _():
            rider.start(r_ins, r_outs, r_scr)

        body(*pre, *ins, *outs, *scr)

        @pl.when(last)
        def _():
            rider.finish(r_ins, r_outs, r_scr)

    if prefetch is None:
        res = pl.pallas_call(wrapped, name=name, grid=grid, in_specs=in_specs, out_specs=out_specs, out_shape=out_shape,
                             scratch_shapes=scratch, input_output_aliases=io_aliases,
                             compiler_params=_cparams(*sem))(*args)
    else:
        grid_spec = pltpu.PrefetchScalarGridSpec(num_scalar_prefetch=1, grid=grid, in_specs=in_specs,
                                                 out_specs=out_specs, scratch_shapes=scratch)
        res = pl.pallas_call(wrapped, name=name, grid_spec=grid_spec, out_shape=out_shape,
                             input_output_aliases=io_aliases, compiler_params=_cparams(*sem))(prefetch, *args)
    return list(res[:n_out]), list(res[n_out:])


V7X_MM_VMEM_BUDGET = 40 * 1024 * 1024


def _mm_plan(form, m, n, k, a_size, b_size, o_size, has_acc):
    tk = k if k <= 2816 else _tile(k, 2816)
    nk = k // tk
    tn = n if (form == "tn" and n <= 2048) else _tile(n, 512)
    for tm in sorted({m} | {t for t in range(128, m, 128) if m % t == 0}, reverse=True):
        need = 2 * (tm * tk * a_size + tn * tk * b_size + tm * tn * o_size) + tm * tn * 4 * (2 if nk > 1 else 1)
        need += 2 * tm * tn * 4 if has_acc else 0
        if need <= V7X_MM_VMEM_BUDGET:
            return tm, tn, tk
    return _tile(m, 128), tn, tk


def _mm(name, a, b, form, out_dtype, acc=None, rider=None):
    if form == "nt":
        (m, k), (n, k2) = a.shape, b.shape
    elif form == "nn":
        (m, k), (k2, n) = a.shape, b.shape
    else:
        (k, m), (k2, n) = a.shape, b.shape
    assert k == k2, (name, a.shape, b.shape)
    has_acc = acc is not None
    tm, tn, tk = _mm_plan(form, m, n, k, a.dtype.itemsize, b.dtype.itemsize, jnp.dtype(out_dtype).itemsize, has_acc)
    nk = k // tk
    a_spec = {"nt": pl.BlockSpec((tm, tk), lambda i, j, kk: (i, kk)),
              "nn": pl.BlockSpec((tm, tk), lambda i, j, kk: (i, kk)),
              "tn": pl.BlockSpec((tk, tm), lambda i, j, kk: (kk, i))}[form]
    b_spec = {"nt": pl.BlockSpec((tn, tk), lambda i, j, kk: (j, kk)),
              "nn": pl.BlockSpec((tk, tn), lambda i, j, kk: (kk, j)),
              "tn": pl.BlockSpec((tk, tn), lambda i, j, kk: (kk, j))}[form]
    o_spec = pl.BlockSpec((tm, tn), lambda i, j, kk: (i, j))
    dims = _DIMS[form]

    def body(*refs):
        a_ref, b_ref = refs[0], refs[1]
        c_ref = refs[2] if has_acc else None
        o_ref = refs[3] if has_acc else refs[2]

        def finish(r):
            if has_acc:
                r = r + c_ref[...]
            o_ref[...] = r.astype(o_ref.dtype)

        if nk == 1:
            finish(_dot(a_ref[...], b_ref[...], dims))
            return
        acc_ref = refs[-1]
        kk = pl.program_id(2)

        @pl.when(kk == 0)
        def _():
            acc_ref[...] = _dot(a_ref[...], b_ref[...], dims)

        @pl.when(kk > 0)
        def _():
            acc_ref[...] += _dot(a_ref[...], b_ref[...], dims)

        @pl.when(kk == nk - 1)
        def _():
            finish(acc_ref[...])

    in_specs = [a_spec, b_spec] + ([o_spec] if has_acc else [])
    args = (a, b) + ((acc,) if has_acc else ())
    outs, r_outs = _pcall(name, body, (m // tm, n // tn, nk), in_specs, [o_spec],
                          [jax.ShapeDtypeStruct((m, n), out_dtype)], args, ("parallel", "parallel", "arbitrary"),
                          scratch=[pltpu.VMEM((tm, tn), F32)] if nk > 1 else [], rider=rider)
    return outs[0] if rider is None else (outs[0], r_outs)


def _rows(name, fn, n_rows, tm, nbc, row_ins, type_ins, row_outs, acc_outs, rider=None):
    n_ri, n_ti, n_ro, n_ao = len(row_ins), len(type_ins), len(row_outs), len(acc_outs)

    def row_map(i, cb, roff):
        return (jnp.maximum(i - roff, 0), cb)

    def type_map(i):
        return (jnp.where(i >= nbc, 1, 0), 0, 0)

    in_specs, args = [], []
    for arr, cb, width, roff in row_ins:
        in_specs.append(pl.BlockSpec((tm, width), functools.partial(row_map, cb=cb, roff=roff)))
        args.append(arr)
    for arr in type_ins:
        in_specs.append(pl.BlockSpec((None, 1, arr.shape[-1]), type_map))
        args.append(arr)
    out_shape, out_specs = [], []
    for total, width, dtype, roff in row_outs:
        out_shape.append(jax.ShapeDtypeStruct((total, width), dtype))
        out_specs.append(pl.BlockSpec((tm, width), functools.partial(row_map, cb=0, roff=roff)))
    for width in acc_outs:
        out_shape.append(jax.ShapeDtypeStruct((2, 1, width), F32))
        out_specs.append(pl.BlockSpec((None, 1, width), type_map))
    n_in = n_ri + n_ti

    def body(*refs):
        i = pl.program_id(0)
        outs = fn(*[r[...] for r in refs[:n_in]])
        if not isinstance(outs, (tuple, list)):
            outs = (outs,)
        assert len(outs) == n_ro + n_ao, (name, len(outs))
        for r, o in zip(refs[n_in:n_in + n_ro], outs[:n_ro]):
            r[...] = o.astype(r.dtype)
        if n_ao:
            first = jnp.logical_or(i == 0, i == nbc)
            for r, o in zip(refs[n_in + n_ro:], outs[n_ro:]):
                o = jnp.broadcast_to(o.astype(F32), r.shape)

                @pl.when(first)
                def _(r=r, o=o):
                    r[...] = o

                @pl.when(jnp.logical_not(first))
                def _(r=r, o=o):
                    r[...] += o

    outs, r_outs = _pcall(name, body, (n_rows // tm,), in_specs, out_specs, out_shape, args, ("arbitrary",),
                          rider=rider)
    return outs if rider is None else (outs, r_outs)


def _vjp_fn(f, n_row, n_cot, keep=None):
    def g(*args):
        prim = args[:n_row] + args[n_row + n_cot:]
        cots = args[n_row:n_row + n_cot]
        out, vjp = jax.vjp(f, *prim)
        grads = vjp(tuple(cots) if isinstance(out, (tuple, list)) else cots[0])
        return grads if keep is None else tuple(grads[j] for j in keep)
    return g


def _typed(v):
    v = v.reshape(1, 1, -1)
    return jnp.concatenate([v, v], axis=0)


def _ln(x, g, b):
    mu = jnp.mean(x, axis=-1, keepdims=True)
    var = jnp.mean(jnp.square(x - mu), axis=-1, keepdims=True)
    return (x - mu) * lax.rsqrt(var + LN_EPS) * g + b


def _f_mod(x, sc, sh):
    return x * (1.0 + sc) + sh


def _make_f_ln(alpha, with_mod):
    def f(x, o, gate, lng, lnb, *mod):
        xn = _ln(alpha * x + gate * o, lng, lnb)
        if with_mod:
            sc, sh = mod
            return xn, xn * (1.0 + sc) + sh
        return xn
    return f


@jax.custom_vjp
def _rot(y):
    lane = lax.broadcasted_iota(jnp.int32, y.shape, 1)
    return jnp.where(lane % 64 < 32, pltpu.roll(y, 96, axis=1), pltpu.roll(y, 32, axis=1))


_rot.defvjp(lambda y: (_rot(y), None), lambda _, g: (_rot(g),))


def _f_prep(p, cos, sin, qg, kg):
    def head(xh, g):
        ms = jnp.mean(jnp.square(xh), axis=-1, keepdims=True)
        y = xh * lax.rsqrt(ms + RMS_EPS) * g
        return y * cos + _rot(y) * sin
    q = jnp.concatenate([head(p[:, h * HEAD_DIM:(h + 1) * HEAD_DIM], qg) for h in range(N_HEADS)], axis=1)
    k = jnp.concatenate([head(p[:, OFF_K + h * HEAD_DIM:OFF_K + (h + 1) * HEAD_DIM], kg)
                         for h in range(N_KV_HEADS)], axis=1)
    return q, k, p[:, OFF_V:OFF_POOL]


def _f_gate(g, t0, t1, t2, t3, b):
    d = t0.shape[-1]
    ts = (t0, t1, t2, t3)
    terms = [jax.nn.sigmoid(g[:, k * d:(k + 1) * d] + b[:, k * d:(k + 1) * d]) * ts[k] for k in range(N_BRANCH)]
    return terms[0] + terms[1] + terms[2] + terms[3]


def _f_swiglu(a, b):
    return jax.nn.silu(a) * b


def _softmax(s):
    e = jnp.exp(s - jnp.max(s, axis=-1, keepdims=True))
    return e / jnp.sum(e, axis=-1, keepdims=True)


def _attn_fwd(name, q, k, v, rc, ctx_queries, tq=256, rider=None):
    r = q.shape[0]
    assert rc % tq == 0 and r % tq == 0
    nqc = rc // tq

    def body(q_ref, k_ref, v_ref, o_ref):
        qi = pl.program_id(1)

        def attend(nk):
            s = _dot(q_ref[...], k_ref[0:nk, :], _NT) * ATT_SCALE
            p = _softmax(s)
            o_ref[...] = _dot(p, v_ref[0:nk, :], _NN).astype(o_ref.dtype)

        @pl.when(qi < nqc)
        def _():
            if ctx_queries:
                attend(rc)
            else:
                o_ref[...] = jnp.zeros_like(o_ref)

        @pl.when(qi >= nqc)
        def _():
            attend(r)

    outs, r_outs = _pcall(
        name, body, (N_HEADS, r // tq),
        [pl.BlockSpec((tq, HEAD_DIM), lambda h, i: (i, h)),
         pl.BlockSpec((r, HEAD_DIM), lambda h, i: (0, h // KV_GROUP)),
         pl.BlockSpec((r, HEAD_DIM), lambda h, i: (0, h // KV_GROUP))],
        [pl.BlockSpec((tq, HEAD_DIM), lambda h, i: (i, h))],
        [jax.ShapeDtypeStruct((r, Q_W), BF16)], (q, k, v), ("parallel", "parallel"), rider=rider)
    return outs[0] if rider is None else (outs[0], r_outs)


def _attn_bwd(name, q, k, v, do, rc, ctx_queries, tq=256, rider=None):
    r = q.shape[0]
    nqc = rc // tq

    def body(q_ref, k_ref, v_ref, do_ref, dq_ref, dk_ref, dv_ref):
        g, qi = pl.program_id(1), pl.program_id(2)

        @pl.when(jnp.logical_and(g == 0, qi == 0))
        def _():
            dk_ref[...] = jnp.zeros_like(dk_ref)
            dv_ref[...] = jnp.zeros_like(dv_ref)

        def grad(nk):
            qb, kb, vb = q_ref[...], k_ref[0:nk, :], v_ref[0:nk, :]
            dob = do_ref[...].astype(BF16)
            p = _softmax(_dot(qb, kb, _NT) * ATT_SCALE)
            dv_ref[0:nk, :] += _dot(p, dob, _TN)
            dp = _dot(dob, vb, _NT)
            ds = p * (dp - jnp.sum(dp * p, axis=-1, keepdims=True)) * ATT_SCALE
            dq_ref[...] = _dot(ds, kb, _NN)
            dk_ref[0:nk, :] += _dot(ds, qb, _TN)

        @pl.when(qi < nqc)
        def _():
            if ctx_queries:
                grad(rc)
            else:
                dq_ref[...] = jnp.zeros_like(dq_ref)

        @pl.when(qi >= nqc)
        def _():
            grad(r)

    def qmap(kv, g, i):
        return (i, kv * KV_GROUP + g)

    def kvmap(kv, g, i):
        return (0, kv)

    outs, r_outs = _pcall(
        name, body, (N_KV_HEADS, KV_GROUP, r // tq),
        [pl.BlockSpec((tq, HEAD_DIM), qmap), pl.BlockSpec((r, HEAD_DIM), kvmap),
         pl.BlockSpec((r, HEAD_DIM), kvmap), pl.BlockSpec((tq, HEAD_DIM), qmap)],
        [pl.BlockSpec((tq, HEAD_DIM), qmap), pl.BlockSpec((r, HEAD_DIM), kvmap), pl.BlockSpec((r, HEAD_DIM), kvmap)],
        [jax.ShapeDtypeStruct((r, Q_W), F32), jax.ShapeDtypeStruct((r, KV_W), F32),
         jax.ShapeDtypeStruct((r, KV_W), F32)],
        (q, k, v, do), ("arbitrary", "arbitrary", "arbitrary"), rider=rider)
    return outs if rider is None else (outs, r_outs)


def _segments(shape, rc):
    t = lax.broadcasted_iota(jnp.int32, shape, 0)
    lo = jnp.where(t < rc, 0, rc)
    hi = jnp.where(t < rc, rc, shape[0])
    return t, lo, hi


def _shifted(x, o, t, lo, hi):
    n = x.shape[0]
    sh = pltpu.roll(x, (-o) % n, axis=0)
    return jnp.where(jnp.logical_and(t + o >= lo, t + o < hi), sh, 0.0)


def _winsum(x, left, right, t, lo, hi):
    acc = x
    for o in range(-left, right + 1):
        if o != 0:
            acc = acc + _shifted(x, o, t, lo, hi)
    return acc


def _pool_parts(z, g, t, lo, hi):
    w = POOL_WINDOWS[g]
    left = w // 2
    right = w - 1 - left
    count = (jnp.minimum(t + right + 1, hi) - jnp.maximum(t - left, lo)).astype(F32)
    return _winsum(z, left, right, t, lo, hi) / count - z, count, left, right


def _pool_fwd(name, p, pool_w, pool_scale, rc):
    r = p.shape[0]

    def body(z_ref, w_ref, s_ref, y_ref):
        t, lo, hi = _segments((r, GC), rc)
        for g in range(N_GROUPS):
            cols = slice(g * GC, (g + 1) * GC)
            d, _, _, _ = _pool_parts(z_ref[:, cols], g, t, lo, hi)
            y_ref[:, cols] = (_dot(d, w_ref[g], _NN) * s_ref[:, cols]).astype(y_ref.dtype)

    return pl.pallas_call(
        body, name=name, grid=(1,),
        in_specs=[pl.BlockSpec((r, BR_W), lambda i: (0, OFF_POOL // BR_W)),
                  pl.BlockSpec((N_GROUPS, GC, GC), lambda i: (0, 0, 0)),
                  pl.BlockSpec((1, BR_W), lambda i: (0, 0))],
        out_specs=pl.BlockSpec((r, BR_W), lambda i: (0, 0)),
        out_shape=jax.ShapeDtypeStruct((r, BR_W), BF16),
        compiler_params=_cparams("arbitrary"),
    )(p, pool_w, pool_scale.reshape(1, BR_W))


def _pool_bwd(name, p, pool_w, pool_scale, dy, rc):
    r = p.shape[0]

    def body(z_ref, w_ref, s_ref, dy_ref, dz_ref, dw_ref, ds_ref):
        t, lo, hi = _segments((r, GC), rc)
        for g in range(N_GROUPS):
            cols = slice(g * GC, (g + 1) * GC)
            d, count, left, right = _pool_parts(z_ref[:, cols], g, t, lo, hi)
            dyg = dy_ref[:, cols]
            ds_ref[:, cols] = jnp.sum(dyg * _dot(d, w_ref[g], _NN), axis=0, keepdims=True)
            dlin = dyg * s_ref[:, cols]
            dw_ref[g] = _dot(d, dlin, _TN)
            dd = _dot(dlin, w_ref[g], _NT)
            dz_ref[:, cols] = (_winsum(dd / count, right, left, t, lo, hi) - dd).astype(dz_ref.dtype)

    return pl.pallas_call(
        body, name=name, grid=(1,),
        in_specs=[pl.BlockSpec((r, BR_W), lambda i: (0, OFF_POOL // BR_W)),
                  pl.BlockSpec((N_GROUPS, GC, GC), lambda i: (0, 0, 0)),
                  pl.BlockSpec((1, BR_W), lambda i: (0, 0)),
                  pl.BlockSpec((r, BR_W), lambda i: (0, 0))],
        out_specs=[pl.BlockSpec((r, BR_W), lambda i: (0, 0)),
                   pl.BlockSpec((N_GROUPS, GC, GC), lambda i: (0, 0, 0)),
                   pl.BlockSpec((1, BR_W), lambda i: (0, 0))],
        out_shape=[jax.ShapeDtypeStruct((r, BR_W), BF16), jax.ShapeDtypeStruct((N_GROUPS, GC, GC), F32),
                   jax.ShapeDtypeStruct((1, BR_W), F32)],
        compiler_params=_cparams("arbitrary"),
    )(p, pool_w, pool_scale.reshape(1, BR_W), dy)


def _f_sgu_v(pvg, lng, lnb):
    return _ln(jax.nn.gelu(pvg), lng, lnb)


def _sgu_fwd(name, p, ln_g, ln_b, sgu_w, sgu_b):
    r = p.shape[0]

    def body(pu_ref, pv_ref, g_ref, b_ref, w_ref, sb_ref, y_ref):
        vn = _f_sgu_v(pv_ref[...], g_ref[...], b_ref[...])
        u = jax.nn.gelu(pu_ref[...])
        for g in range(N_GROUPS):
            cols = slice(g * GC, (g + 1) * GC)
            s = _dot(w_ref[g], vn[:, cols], _NN) + sb_ref[g]
            y_ref[:, cols] = (u[:, cols] * s).astype(y_ref.dtype)

    return pl.pallas_call(
        body, name=name, grid=(r // SGU_CHUNK,),
        in_specs=[pl.BlockSpec((SGU_CHUNK, BR_W), lambda i: (i, OFF_U // BR_W)),
                  pl.BlockSpec((SGU_CHUNK, BR_W), lambda i: (i, OFF_VG // BR_W)),
                  pl.BlockSpec((1, BR_W), lambda i: (0, 0)), pl.BlockSpec((1, BR_W), lambda i: (0, 0)),
                  pl.BlockSpec((N_GROUPS, GC, GC), lambda i: (0, 0, 0)),
                  pl.BlockSpec((N_GROUPS, SGU_CHUNK, 1), lambda i: (0, 0, 0))],
        out_specs=pl.BlockSpec((SGU_CHUNK, BR_W), lambda i: (i, 0)),
        out_shape=jax.ShapeDtypeStruct((r, BR_W), BF16),
        compiler_params=_cparams("parallel"),
    )(p, p, ln_g.reshape(1, BR_W), ln_b.reshape(1, BR_W), sgu_w, sgu_b.reshape(N_GROUPS, SGU_CHUNK, 1))


def _sgu_bwd(name, p, ln_g, ln_b, sgu_w, sgu_b, dy):
    r = p.shape[0]

    def body(pu_ref, pv_ref, g_ref, b_ref, w_ref, sb_ref, dy_ref, dp_ref, dg_ref, db_ref, dw_ref, dsb_ref):
        i = pl.program_id(0)

        @pl.when(i == 0)
        def _():
            for ref in (dg_ref, db_ref, dw_ref, dsb_ref):
                ref[...] = jnp.zeros_like(ref)

        vn, vjp_v = jax.vjp(_f_sgu_v, pv_ref[...], g_ref[...], b_ref[...])
        u, vjp_u = jax.vjp(jax.nn.gelu, pu_ref[...])
        dy = dy_ref[...]
        du, dvn = [], []
        for g in range(N_GROUPS):
            cols = slice(g * GC, (g + 1) * GC)
            s = _dot(w_ref[g], vn[:, cols], _NN) + sb_ref[g]
            du.append(dy[:, cols] * s)
            ds = dy[:, cols] * u[:, cols]
            dsb_ref[g] += jnp.sum(ds, axis=1, keepdims=True)
            dw_ref[g] += _dot(ds, vn[:, cols], _NT)
            dvn.append(_dot(w_ref[g], ds, _TN))
        (dpu,) = vjp_u(jnp.concatenate(du, axis=1))
        dpv, dg, db = vjp_v(jnp.concatenate(dvn, axis=1))
        dp_ref[:, 0:BR_W] = dpu.astype(dp_ref.dtype)
        dp_ref[:, BR_W:2 * BR_W] = dpv.astype(dp_ref.dtype)
        dg_ref[...] += dg
        db_ref[...] += db

    vec = pl.BlockSpec((1, BR_W), lambda i: (0, 0))
    wsp = pl.BlockSpec((N_GROUPS, GC, GC), lambda i: (0, 0, 0))
    bsp = pl.BlockSpec((N_GROUPS, SGU_CHUNK, 1), lambda i: (0, 0, 0))
    return pl.pallas_call(
        body, name=name, grid=(r // SGU_CHUNK,),
        in_specs=[pl.BlockSpec((SGU_CHUNK, BR_W), lambda i: (i, OFF_U // BR_W)),
                  pl.BlockSpec((SGU_CHUNK, BR_W), lambda i: (i, OFF_VG // BR_W)),
                  vec, vec, wsp, bsp, pl.BlockSpec((SGU_CHUNK, BR_W), lambda i: (i, 0))],
        out_specs=[pl.BlockSpec((SGU_CHUNK, 2 * BR_W), lambda i: (i, 0)), vec, vec, wsp, bsp],
        out_shape=[jax.ShapeDtypeStruct((r, 2 * BR_W), BF16), jax.ShapeDtypeStruct((1, BR_W), F32),
                   jax.ShapeDtypeStruct((1, BR_W), F32), jax.ShapeDtypeStruct((N_GROUPS, GC, GC), F32),
                   jax.ShapeDtypeStruct((N_GROUPS, SGU_CHUNK, 1), F32)],
        compiler_params=_cparams("arbitrary"),
    )(p, p, ln_g.reshape(1, BR_W), ln_b.reshape(1, BR_W), sgu_w, sgu_b.reshape(N_GROUPS, SGU_CHUNK, 1), dy)


def _conv_w8(conv_w):
    return jnp.concatenate([conv_w, jnp.zeros((8 - conv_w.shape[0], conv_w.shape[1]), F32)], axis=0)


def _conv_fwd(name, p, conv_w, rc):
    r = p.shape[0]

    def body(cb_ref, cc_ref, cx_ref, w_ref, y_ref):
        t, lo, hi = _segments((r, GC), rc)
        z = cc_ref[...] * cx_ref[...]
        w = w_ref[...]
        c = _shifted(z, -1, t, lo, hi) * w[0:1] + z * w[1:2] + _shifted(z, 1, t, lo, hi) * w[2:3]
        y_ref[...] = (cb_ref[...] * c).astype(y_ref.dtype)

    nb = OFF_CB // GC
    return pl.pallas_call(
        body, name=name, grid=(N_GROUPS,),
        in_specs=[pl.BlockSpec((r, GC), lambda j: (0, nb + j)),
                  pl.BlockSpec((r, GC), lambda j: (0, nb + N_GROUPS + j)),
                  pl.BlockSpec((r, GC), lambda j: (0, nb + 2 * N_GROUPS + j)),
                  pl.BlockSpec((8, GC), lambda j: (0, j))],
        out_specs=pl.BlockSpec((r, GC), lambda j: (0, j)),
        out_shape=jax.ShapeDtypeStruct((r, BR_W), BF16),
        compiler_params=_cparams("parallel"),
    )(p, p, p, _conv_w8(conv_w))


def _conv_bwd(name, p, conv_w, dy, rc):
    r = p.shape[0]

    def body(cb_ref, cc_ref, cx_ref, w_ref, dy_ref, dcb_ref, dcc_ref, dcx_ref, dw_ref):
        t, lo, hi = _segments((r, GC), rc)
        cc, cx, w, dy = cc_ref[...], cx_ref[...], w_ref[...], dy_ref[...]
        z = cc * cx
        zp, zn = _shifted(z, -1, t, lo, hi), _shifted(z, 1, t, lo, hi)
        dcb_ref[...] = (dy * (zp * w[0:1] + z * w[1:2] + zn * w[2:3])).astype(dcb_ref.dtype)
        dc = dy * cb_ref[...]
        dw_ref[...] = jnp.concatenate(
            [jnp.sum(dc * zp, axis=0, keepdims=True), jnp.sum(dc * z, axis=0, keepdims=True),
             jnp.sum(dc * zn, axis=0, keepdims=True), jnp.zeros((5, GC), F32)], axis=0)
        dz = dc * w[1:2] + _shifted(dc, 1, t, lo, hi) * w[0:1] + _shifted(dc, -1, t, lo, hi) * w[2:3]
        dcc_ref[...] = (dz * cx).astype(dcc_ref.dtype)
        dcx_ref[...] = (dz * cc).astype(dcx_ref.dtype)

    nb = OFF_CB // GC
    return pl.pallas_call(
        body, name=name, grid=(N_GROUPS,),
        in_specs=[pl.BlockSpec((r, GC), lambda j: (0, nb + j)),
                  pl.BlockSpec((r, GC), lambda j: (0, nb + N_GROUPS + j)),
                  pl.BlockSpec((r, GC), lambda j: (0, nb + 2 * N_GROUPS + j)),
                  pl.BlockSpec((8, GC), lambda j: (0, j)),
                  pl.BlockSpec((r, GC), lambda j: (0, j))],
        out_specs=[pl.BlockSpec((r, GC), lambda j: (0, j))] * 3 + [pl.BlockSpec((8, GC), lambda j: (0, j))],
        out_shape=[jax.ShapeDtypeStruct((r, BR_W), BF16)] * 3 + [jax.ShapeDtypeStruct((8, BR_W), F32)],
        compiler_params=_cparams("parallel"),
    )(p, p, p, _conv_w8(conv_w), dy)


def _rope_tables(rc, n):
    rows = n // GRID_W
    row = jnp.repeat(jnp.arange(rows), GRID_W).astype(F32)
    col = jnp.tile(jnp.arange(GRID_W), rows).astype(F32)
    inv = ROPE_THETA ** (-jnp.arange(0, ROPE_AXIS_DIM, 2, dtype=F32) / ROPE_AXIS_DIM)
    ang_r, ang_c = row[:, None] * inv, col[:, None] * inv
    cos = jnp.concatenate([jnp.cos(ang_r), jnp.cos(ang_r), jnp.cos(ang_c), jnp.cos(ang_c)], axis=1)
    sin = jnp.concatenate([-jnp.sin(ang_r), jnp.sin(ang_r), -jnp.sin(ang_c), jnp.sin(ang_c)], axis=1)
    cos = jnp.concatenate([jnp.ones((rc, HEAD_DIM), F32), cos], axis=0)
    sin = jnp.concatenate([jnp.zeros((rc, HEAD_DIM), F32), sin], axis=0)
    return cos, sin


MOD_NAMES = ("sh1", "sc1", "g1", "sh2", "sc2", "g2")


def _local_step(xin, target, mod, comm, sp, rc, alpha):
    def carrying(fn):
        def call(name, *args, **kw):
            rider = comm.rider(name)
            if rider is None:
                return fn(name, *args, **kw)
            res, r_outs = fn(name, *args, rider=rider, **kw)
            comm.deliver(name, r_outs)
            return res
        return call

    mm, rows, attn_fwd, attn_bwd = carrying(_mm), carrying(_rows), carrying(_attn_fwd), carrying(_attn_bwd)
    r, d = xin.shape
    n_layers = mod.shape[0]
    tm_n, tm_w = 256, 128
    nbc_n, nbc_w = rc // tm_n, rc // tm_w
    cos, sin = _rope_tables(rc, r - rc)
    mp = mod.reshape(n_layers, 2, 6, 1, d)
    mods = [{nm: mp[i, :, j] for j, nm in enumerate(MOD_NAMES)} for i in range(n_layers)]
    f_ln_mod, f_ln_last = _make_f_ln(alpha, True), _make_f_ln(alpha, False)

    def whole(arr, roff=0):
        return (arr, 0, arr.shape[1], roff)

    (hb,) = rows("mod_in", _f_mod, r, tm_n, nbc_n, [whole(xin)], [mods[0]["sc1"], mods[0]["sh1"]],
                 [(r, d, BF16, 0)], [])
    saved = []
    x = xin
    for i in range(n_layers):
        last = i == n_layers - 1
        w, s, m = functools.partial(comm.weight, i), sp[i], mods[i]
        sv = {"x": x, "hb": hb}
        p = mm(f"l{i}_in", hb, w("in_t"), "nt", F32)
        gpre = mm(f"l{i}_gate", hb, w("gate_t"), "nt", F32)
        q, k, v = rows(f"l{i}_prep", _f_prep, r, tm_n, nbc_n,
                       [(p, 0, OFF_POOL, 0), whole(cos), whole(sin)], [_typed(s["q_norm_g"]), _typed(s["k_norm_g"])],
                       [(r, Q_W, BF16, 0), (r, KV_W, BF16, 0), (r, KV_W, BF16, 0)], [])
        ys = [attn_fwd(f"l{i}_attn", q, k, v, rc, not last),
              _pool_fwd(f"l{i}_pool", p, s["pool_w"], s["pool_scale"], rc),
              _sgu_fwd(f"l{i}_sgu", p, s["sgu_ln_g"], s["sgu_ln_b"], s["sgu_w"], s["sgu_b"]),
              _conv_fwd(f"l{i}_conv", p, s["conv_w"], rc)]
        ts = [mm(f"l{i}_br{kk}", ys[kk], w(f"br{kk}"), "nt", F32) for kk in range(N_BRANCH)]
        (mg,) = rows(f"l{i}_merge", _f_gate, r, tm_w, nbc_w, [whole(gpre)] + [whole(t) for t in ts],
                     [_typed(s["b_gate"])], [(r, d, BF16, 0)], [])
        o = mm(f"l{i}_o", mg, w("o"), "nn", F32)
        x1, h2b = rows(f"l{i}_ln1", f_ln_mod, r, tm_n, nbc_n, [whole(x), whole(o)],
                       [m["g1"], _typed(s["ln1_g"]), _typed(s["ln1_b"]), m["sc2"], m["sh2"]],
                       [(r, d, F32, 0), (r, d, BF16, 0)], [])
        af = mm(f"l{i}_ffg", h2b, w("ffg_t"), "nt", F32)
        bf = mm(f"l{i}_ffu", h2b, w("ffu_t"), "nt", F32)
        (f,) = rows(f"l{i}_swiglu", _f_swiglu, r, tm_w, nbc_w, [whole(af), whole(bf)], [],
                    [(r, af.shape[1], BF16, 0)], [])
        o2 = mm(f"l{i}_ffd", f, w("ffd"), "nn", F32)
        if last:
            (x2,) = rows(f"l{i}_ln2", f_ln_last, r, tm_n, nbc_n, [whole(x1), whole(o2)],
                         [m["g2"], _typed(s["ln2_g"]), _typed(s["ln2_b"])], [(r, d, F32, 0)], [])
            hb = None
        else:
            nx = mods[i + 1]
            x2, hb = rows(f"l{i}_ln2", f_ln_mod, r, tm_n, nbc_n, [whole(x1), whole(o2)],
                          [m["g2"], _typed(s["ln2_g"]), _typed(s["ln2_b"]), nx["sc1"], nx["sh1"]],
                          [(r, d, F32, 0), (r, d, BF16, 0)], [])
        sv.update(p=p, gpre=gpre, q=q, k=k, v=v, ys=ys, ts=ts, mg=mg, o=o, x1=x1, h2b=h2b, af=af, bf=bf, f=f, o2=o2)
        saved.append(sv)
        x = x2

    lat = jnp.concatenate([jnp.zeros((1, 1, 128), F32), jnp.ones((1, 1, 128), F32)], axis=0)

    def f_loss(xb, tb, msk):
        diff = (xb - tb) * msk[:, 0:1]
        part = jnp.sum(jnp.mean(jnp.square(diff), axis=-1, keepdims=True), axis=0, keepdims=True)
        return diff * (1.0 / d), jnp.broadcast_to(part, (1, 128))

    dx_direct, loss_acc = rows("loss", f_loss, r, tm_n, nbc_n, [whole(x), whole(target, nbc_n)], [lat],
                               [(r, d, F32, 0)], [128])
    loss = 0.5 * loss_acc[1, 0, 0]

    dmods = [dict() for _ in range(n_layers)]
    dsp = [dict() for _ in range(n_layers)]
    dh = None
    for i in reversed(range(n_layers)):
        last = i == n_layers - 1
        w, s, m, sv = functools.partial(comm.weight, i), sp[i], mods[i], saved[i]
        dm, dw, ds = dmods[i], {}, dsp[i]
        ln2 = [m["g2"], _typed(s["ln2_g"]), _typed(s["ln2_b"])]
        if last:
            res = rows(f"l{i}_ln2_bwd", _vjp_fn(f_ln_last, 2, 1), r, tm_n, nbc_n,
                       [whole(sv["x1"]), whole(sv["o2"]), whole(dx_direct)], ln2,
                       [(r, d, F32, 0), (r, d, BF16, 0)], [d, d, d])
            dx1, do2, dm["g2"], dlg, dlb = res
        else:
            nx = mods[i + 1]
            res = rows(f"l{i}_ln2_bwd", _vjp_fn(f_ln_mod, 2, 2), r, tm_n, nbc_n,
                       [whole(sv["x1"]), whole(sv["o2"]), whole(dx_direct), whole(dh)],
                       ln2 + [nx["sc1"], nx["sh1"]],
                       [(r, d, F32, 0), (r, d, BF16, 0)], [d, d, d, d, d])
            dx1, do2, dm["g2"], dlg, dlb, dmods[i + 1]["sc1"], dmods[i + 1]["sh1"] = res
        ds["ln2_g"], ds["ln2_b"] = dlg, dlb
        df = mm(f"l{i}_dF", do2, w("ffd"), "nt", F32)
        comm.grads(i, {"ffd": mm(f"l{i}_dWffd", sv["f"], do2, "tn", BF16)})
        dab, dbb = rows(f"l{i}_swiglu_bwd", _vjp_fn(_f_swiglu, 2, 1), r, tm_w, nbc_w,
                        [whole(sv["af"]), whole(sv["bf"]), whole(df)], [],
                        [(r, df.shape[1], BF16, 0), (r, df.shape[1], BF16, 0)], [])
        comm.grads(i, {"ffg_t": mm(f"l{i}_dWffg", dab, sv["h2b"], "tn", BF16)})
        comm.grads(i, {"ffu_t": mm(f"l{i}_dWffu", dbb, sv["h2b"], "tn", BF16)})
        dh2 = mm(f"l{i}_dh2a", dab, w("ffg_t"), "nn", F32)
        dh2 = mm(f"l{i}_dh2b", dbb, w("ffu_t"), "nn", F32, acc=dh2)
        res = rows(f"l{i}_ln1_bwd", _vjp_fn(f_ln_mod, 2, 2), r, tm_n, nbc_n,
                   [whole(sv["x"]), whole(sv["o"]), whole(dx1), whole(dh2)],
                   [m["g1"], _typed(s["ln1_g"]), _typed(s["ln1_b"]), m["sc2"], m["sh2"]],
                   [(r, d, F32, 0), (r, d, BF16, 0)], [d, d, d, d, d])
        dx_direct, do, dm["g1"], ds["ln1_g"], ds["ln1_b"], dm["sc2"], dm["sh2"] = res
        dmg = mm(f"l{i}_dMg", do, w("o"), "nt", F32)
        comm.grads(i, {"o": mm(f"l{i}_dWo", sv["mg"], do, "tn", BF16)})
        res = rows(f"l{i}_merge_bwd", _vjp_fn(_f_gate, 5, 1), r, tm_w, nbc_w,
                   [whole(sv["gpre"])] + [whole(t) for t in sv["ts"]] + [whole(dmg)], [_typed(s["b_gate"])],
                   [(r, N_BRANCH * d, BF16, 0)] + [(r, d, BF16, 0)] * N_BRANCH, [N_BRANCH * d])
        dgb, dts, ds["b_gate"] = res[0], res[1:1 + N_BRANCH], res[1 + N_BRANCH]
        dys = [mm(f"l{i}_dY{kk}", dts[kk], w(f"br{kk}"), "nn", F32) for kk in range(N_BRANCH)]
        for kk in range(N_BRANCH):
            comm.grads(i, {f"br{kk}": mm(f"l{i}_dWbr{kk}", dts[kk], sv["ys"][kk], "tn", BF16)})
        dq, dk, dv = attn_bwd(f"l{i}_attn_bwd", sv["q"], sv["k"], sv["v"], dys[0], rc, not last)
        res = rows(f"l{i}_prep_bwd", _vjp_fn(_f_prep, 3, 3, keep=(0, 3, 4)), r, tm_n, nbc_n,
                   [(sv["p"], 0, OFF_POOL, 0), whole(cos), whole(sin), whole(dq), whole(dk), whole(dv)],
                   [_typed(s["q_norm_g"]), _typed(s["k_norm_g"])],
                   [(r, OFF_POOL, BF16, 0)], [HEAD_DIM, HEAD_DIM])
        dp_qkv, ds["q_norm_g"], ds["k_norm_g"] = res
        dp_pool, ds["pool_w"], ds["pool_scale"] = _pool_bwd(f"l{i}_pool_bwd", sv["p"], s["pool_w"], s["pool_scale"],
                                                            dys[1], rc)
        dp_sgu, ds["sgu_ln_g"], ds["sgu_ln_b"], ds["sgu_w"], ds["sgu_b"] = _sgu_bwd(
            f"l{i}_sgu_bwd", sv["p"], s["sgu_ln_g"], s["sgu_ln_b"], s["sgu_w"], s["sgu_b"], dys[2])
        dp_cb, dp_cc, dp_cx, dcw = _conv_bwd(f"l{i}_conv_bwd", sv["p"], s["conv_w"], dys[3], rc)
        ds["conv_w"] = dcw[0:3]
        dpb = jnp.concatenate([dp_qkv, dp_pool, dp_sgu, dp_cb, dp_cc, dp_cx], axis=1)
        comm.grads(i, {"in_t": mm(f"l{i}_dWin", dpb, sv["hb"], "tn", BF16)})
        comm.grads(i, {"gate_t": mm(f"l{i}_dWgate", dgb, sv["hb"], "tn", BF16)})
        dh = mm(f"l{i}_dhb_a", dpb, w("in_t"), "nn", F32)
        dh = mm(f"l{i}_dhb_b", dgb, w("gate_t"), "nn", F32, acc=dh)

    def f_mod_bwd(xb, ddir, dhb, sc, sh):
        _, vjp = jax.vjp(_f_mod, xb, sc, sh)
        dxb, dsc, dsh = vjp(dhb)
        return dxb + ddir, dsc, dsh

    grad_x, dmods[0]["sc1"], dmods[0]["sh1"] = rows(
        "mod_in_bwd", f_mod_bwd, r, tm_n, nbc_n, [whole(xin), whole(dx_direct), whole(dh)],
        [mods[0]["sc1"], mods[0]["sh1"]], [(r - rc, d, F32, nbc_n)], [d, d])
    dmod = jnp.stack([jnp.concatenate([dmods[i][nm][:, 0, :] for nm in MOD_NAMES], axis=-1)
                      for i in range(n_layers)])
    for ds in dsp:
        for nm in ("ln1_g", "ln1_b", "ln2_g", "ln2_b", "b_gate", "q_norm_g", "k_norm_g"):
            ds[nm] = ds[nm][0, 0] + ds[nm][1, 0]
        ds["pool_scale"] = ds["pool_scale"].reshape(-1)
        ds["sgu_ln_g"] = ds["sgu_ln_g"].reshape(-1)
        ds["sgu_ln_b"] = ds["sgu_ln_b"].reshape(-1)
        ds["sgu_b"] = ds["sgu_b"].reshape(N_GROUPS, SGU_CHUNK)
    return loss, grad_x, dmod, dsp


def _exchange(name, srcs, scatter):
    n_items = len(srcs)
    out_shapes = [jax.ShapeDtypeStruct(s.shape if scatter else (N_DEV,) + s.shape, s.dtype) for s in srcs]

    def body(*refs):
        src, out = refs[:n_items], refs[n_items:2 * n_items]
        send_sems, recv_sems, loc_sems = refs[2 * n_items:]
        mx, my, mc = [lax.axis_index(a) for a in MESH_AXES]
        me = 4 * mx + 2 * my + mc
        peers = []
        for kk in range(1, N_DEV):
            px = 1 - mx if kk & 4 else mx
            py = 1 - my if kk & 2 else my
            pc = 1 - mc if kk & 1 else mc
            peers.append(((px, py, pc), 4 * px + 2 * py + pc))
        local, sent = [], []
        for a in range(n_items):
            loc = pltpu.make_async_copy(src[a].at[me] if scatter else src[a], out[a].at[me], loc_sems.at[a])
            loc.start()
            local.append(loc)
            for j, (peer, peer_l) in enumerate(peers):
                cp = pltpu.make_async_remote_copy(
                    src_ref=src[a].at[peer_l] if scatter else src[a], dst_ref=out[a].at[me],
                    send_sem=send_sems.at[a * (N_DEV - 1) + j], recv_sem=recv_sems.at[a * (N_DEV - 1) + j],
                    device_id=peer, device_id_type=pl.DeviceIdType.MESH)
                cp.start()
                sent.append(cp)
        for a in range(n_items):
            for j, (peer, peer_l) in enumerate(peers):
                pltpu.make_async_remote_copy(
                    src_ref=src[a].at[peer_l] if scatter else src[a], dst_ref=out[a].at[peer_l],
                    send_sem=send_sems.at[a * (N_DEV - 1) + j], recv_sem=recv_sems.at[a * (N_DEV - 1) + j],
                    device_id=peer, device_id_type=pl.DeviceIdType.MESH).wait_recv()
        for cp in sent:
            cp.wait_send()
        for loc in local:
            loc.wait()

    any_spec = pl.BlockSpec(memory_space=pl.ANY)
    res = pl.pallas_call(
        body, name=name,
        in_specs=[any_spec] * n_items, out_specs=[any_spec] * n_items, out_shape=out_shapes,
        scratch_shapes=[pltpu.SemaphoreType.DMA((n_items * (N_DEV - 1),)),
                        pltpu.SemaphoreType.DMA((n_items * (N_DEV - 1),)),
                        pltpu.SemaphoreType.DMA((n_items,))],
    )(*srcs)
    return list(res)


def _mesh_place():
    mx, my, mc = [lax.axis_index(a) for a in MESH_AXES]
    chips = [(1 - mx, my), (mx, 1 - my), (1 - mx, 1 - my)]

    def lid(px, py, pc):
        return 4 * px + 2 * py + pc

    return (mx, my, mc), (mx, my, 1 - mc), chips, lid


def _rcopy(src, dst, send_sems, recv_sems, k, to):
    return pltpu.make_async_remote_copy(src_ref=src, dst_ref=dst, send_sem=send_sems.at[k], recv_sem=recv_sems.at[k],
                                        device_id=to, device_id_type=pl.DeviceIdType.MESH)


def _sem_scratch(*sizes):
    return [pltpu.SemaphoreType.DMA((s,)) for s in sizes]


def _gather_rider(src, rows, buf=None):
    r0, r1 = rows
    win = pl.ds(r0, r1 - r0)

    def start(ins, outs, sems):
        send, recv, loc = sems
        (mx, my, mc), sib, chips, lid = _mesh_place()
        mine, dst = ins[0].at[win], outs[0].at[lid(mx, my, mc), win]
        pltpu.make_async_copy(mine, dst, loc.at[0]).start()
        _rcopy(mine, dst, send, recv, 0, sib).start()
        for j, chip in enumerate(chips):
            _rcopy(mine, dst, send, recv, 1 + j, (*chip, mc)).start()

    def finish(ins, outs, sems):
        send, recv, loc = sems
        (mx, my, mc), sib, chips, lid = _mesh_place()
        mine, dst = ins[0].at[win], outs[0].at[lid(mx, my, mc), win]
        for j, chip in enumerate(chips):
            blk = outs[0].at[lid(*chip, mc), win]
            _rcopy(mine, blk, send, recv, 1 + j, (*chip, mc)).wait_recv()
            _rcopy(blk, blk, send, recv, 4 + j, sib).start()
        _rcopy(mine, outs[0].at[lid(*sib), win], send, recv, 0, sib).wait_recv()
        for j, chip in enumerate(chips):
            _rcopy(mine, outs[0].at[lid(*chip, 1 - mc), win], send, recv, 4 + j, sib).wait_recv()
        for t in range(7):
            _rcopy(mine, dst, send, recv, t, sib).wait_send()
        pltpu.make_async_copy(mine, dst, loc.at[0]).wait()

    out_shape = jax.ShapeDtypeStruct((N_DEV,) + src.shape, src.dtype)
    if buf is None:
        return _Rider([src], [out_shape], _sem_scratch(7, 7, 1), start, finish)
    return _Rider([src, buf], [out_shape], _sem_scratch(7, 7, 1), start, finish, aliases={1: 0})


def _sibling_rider(part):
    def start(ins, outs, sems):
        send, recv = sems
        (mx, my, mc), sib, chips, lid = _mesh_place()
        for t, slab in enumerate([lid(*sib)] + [lid(*chip, 1 - mc) for chip in chips]):
            _rcopy(ins[0].at[slab], outs[0].at[t], send, recv, t, sib).start()

    def finish(ins, outs, sems):
        send, recv = sems
        _, sib, _, _ = _mesh_place()
        for t in range(4):
            cp = _rcopy(ins[0].at[0], outs[0].at[t], send, recv, t, sib)
            cp.wait_recv()
            cp.wait_send()

    return _Rider([part], [jax.ShapeDtypeStruct((4,) + part.shape[1:], part.dtype)], _sem_scratch(4, 4), start, finish)


def _chips_rider(pair, rows, buf=None):
    r0, r1 = rows
    win = pl.ds(r0, r1 - r0)

    def start(ins, outs, sems):
        send, recv = sems
        (mx, my, mc), sib, chips, lid = _mesh_place()
        for j, chip in enumerate(chips):
            _rcopy(ins[0].at[j, win], outs[0].at[j, win], send, recv, j, (*chip, mc)).start()

    def finish(ins, outs, sems):
        send, recv = sems
        (mx, my, mc), sib, chips, lid = _mesh_place()
        for j, chip in enumerate(chips):
            cp = _rcopy(ins[0].at[j, win], outs[0].at[j, win], send, recv, j, (*chip, mc))
            cp.wait_recv()
            cp.wait_send()

    out_shape = jax.ShapeDtypeStruct(pair.shape, pair.dtype)
    if buf is None:
        return _Rider([pair], [out_shape], _sem_scratch(3, 3), start, finish)
    return _Rider([pair, buf], [out_shape], _sem_scratch(3, 3), start, finish, aliases={1: 0})


def _run_rider(name, rider):
    n_in, n_out = len(rider.inputs), len(rider.out_shapes)

    def body(*refs):
        ins, outs, sems = refs[:n_in], refs[n_in:n_in + n_out], refs[n_in + n_out:]
        rider.start(ins, outs, sems)
        rider.finish(ins, outs, sems)

    any_spec = pl.BlockSpec(memory_space=pl.ANY)
    res = pl.pallas_call(body, name=name, in_specs=[any_spec] * n_in, out_specs=[any_spec] * n_out,
                         out_shape=rider.out_shapes, scratch_shapes=rider.scratch,
                         input_output_aliases=rider.aliases)(*rider.inputs)
    return list(res)


def _slab_ids():
    (mx, my, mc), _, chips, lid = _mesh_place()
    return jnp.stack([lid(*chip, mc) for chip in chips] + [lid(mx, my, mc)]).astype(jnp.int32)


def _pair_sum(name, part, rsib, ids):
    _, n, k = part.shape
    tr = _row_tile(n, 512, 16)

    def body(ids_ref, p_ref, r_ref, o_ref):
        o_ref[...] = (p_ref[...].astype(F32) + r_ref[...].astype(F32)).astype(o_ref.dtype)

    grid_spec = pltpu.PrefetchScalarGridSpec(
        num_scalar_prefetch=1, grid=(3, n // tr),
        in_specs=[pl.BlockSpec((None, tr, k), lambda j, i, ids: (ids[j], i, 0)),
                  pl.BlockSpec((None, tr, k), lambda j, i, ids: (1 + j, i, 0))],
        out_specs=pl.BlockSpec((None, tr, k), lambda j, i, ids: (j, i, 0)))
    return pl.pallas_call(body, name=name, grid_spec=grid_spec, out_shape=jax.ShapeDtypeStruct((3, n, k), part.dtype),
                          compiler_params=_cparams("parallel", "parallel"))(ids, part, rsib)


def _sum5(name, part, rsib, rici, ids, layer, stacked, rider=None):
    _, n, k = part.shape
    tr = _row_tile(n, 512, 16)
    first = isinstance(stacked, int)

    def body(ids_ref, p_ref, r_ref, c_ref, *rest):
        acc = p_ref[...].astype(F32) + r_ref[...].astype(F32)
        for j in range(3):
            acc = acc + c_ref[j].astype(F32)
        rest[-1][...] = acc

    in_specs = [pl.BlockSpec((None, tr, k), lambda i, ids: (ids[3], i, 0)),
                pl.BlockSpec((None, tr, k), lambda i, ids: (0, i, 0)),
                pl.BlockSpec((3, tr, k), lambda i, ids: (0, i, 0))] + ([] if first else [pl.BlockSpec(memory_space=pl.ANY)])
    n_layers = stacked if first else stacked.shape[0]
    outs, r_outs = _pcall(name, body, (n // tr,), in_specs, [pl.BlockSpec((None, tr, k), lambda i, ids: (layer, i, 0))],
                          [jax.ShapeDtypeStruct((n_layers, n, k), F32)],
                          (part, rsib, rici) + (() if first else (stacked,)), ("parallel",), rider=rider, prefetch=ids,
                          aliases={} if first else {3: 0})
    return outs[0] if rider is None else (outs[0], r_outs)


W_KEYS = ("in_t", "gate_t", "br0", "br1", "br2", "br3", "o", "ffg_t", "ffu_t", "ffd")
SUMS_TRANSPOSED_LATER = ("gate_t", "br0", "br1", "br2", "br3")


CARRIER_US = {"mod_in": 12, "in": 55, "gate": 95, "prep": 19, "attn": 118, "br0": 15, "merge": 50, "o": 25, "ln1": 27,
              "ffg": 66, "ffu": 66, "swiglu": 42, "ffd": 73, "ln2": 27, "loss": 20, "ln2_bwd": 44, "dF": 70,
              "dWffd": 64, "swiglu_bwd": 66, "dh2a": 75, "dh2b": 75, "dWffg": 64, "dWffu": 64, "ln1_bwd": 44,
              "dMg": 25, "dWo": 25, "merge_bwd": 80, "dY0": 14, "dWbr0": 15, "attn_bwd": 195, "prep_bwd": 28,
              "dWin": 54, "dWgate": 95, "dhb_a": 64, "dhb_b": 115}
ICI_US_PER_MIB = 45.0
D2D_US_PER_MIB = 6.8
MIN_CHUNK_US = 10.0
CARRIER_FILL = 1.15


class _Comm:
    def __init__(self, wsrc):
        self.wsrc = wsrc
        self.n_layers = len(wsrc)
        self.queue = []
        self.riding = {}
        self.buf, self.left = {}, {}
        self.part, self.rsib, self.pair = {}, {}, {}
        self.ids = _slab_ids()
        for i in range(self.n_layers):
            for k in W_KEYS:
                self._push_chunks("gather", ("w", i, k), wsrc[i][k].shape, wsrc[i][k].dtype)

    def _push_chunks(self, kind, item, shape, dtype):
        n, k = shape[-2], shape[-1]
        us = n * k * jnp.dtype(dtype).itemsize / 2 ** 20 * ICI_US_PER_MIB
        pieces = max(1, int(us // MIN_CHUNK_US))
        while n % (16 * pieces):
            pieces -= 1
        step = n // pieces
        self.left[item] = pieces
        for c in range(pieces):
            self.queue.append(dict(kind=kind, item=item, rows=(c * step, (c + 1) * step), us=us / pieces))

    @staticmethod
    def _merge(units, u):
        v = units[-1] if units else None
        if not (v and v["item"] == u["item"] and v["kind"] == u["kind"] and u["rows"] and v["rows"][1] == u["rows"][0]):
            return False
        v.update(rows=(v["rows"][0], u["rows"][1]), us=v["us"] + u["us"], count=v.get("count", 1) + u.get("count", 1))
        return True

    def _unit_rider(self, u):
        item = u["item"]
        if u["kind"] == "gather":
            src = self.wsrc[item[1]][item[2]] if item[0] == "w" else self.part[item]
            return _gather_rider(src, u["rows"], self.buf.get(item))
        if u["kind"] == "sibling":
            return _sibling_rider(self.part[item])
        return _chips_rider(self.pair[item], u["rows"], self.buf.get(item))

    def _done(self, u, out):
        item = u["item"]
        if u["kind"] == "sibling":
            self.rsib[item] = out
            self.pair[item] = _pair_sum(f"pair_l{item[1]}_{item[2]}", self.part[item], out, self.ids)
            self._push_chunks("chips", item, self.pair[item].shape, self.pair[item].dtype)
            return
        self.buf[item] = out
        self.left[item] -= u.get("count", 1)

    def _send(self, name, units, call):
        outs = call(_compose([self._unit_rider(u) for u in units]))
        for u, o in zip(units, outs):
            self._done(u, o)

    def rider(self, name, budget_us=None):
        budget = CARRIER_US.get(name.split("_", 1)[1] if name[0] == "l" and name[1].isdigit() else name, 0) \
            if budget_us is None else budget_us
        units, used = [], 0.0
        while self.queue and used + self.queue[0]["us"] <= CARRIER_FILL * budget:
            u = self.queue[0]
            if not self._merge(units, u):
                if any(v["item"] == u["item"] for v in units):
                    break
                units.append(dict(u))
            used += u["us"]
            del self.queue[0]
        if not units:
            return None
        self.riding[name] = units
        return _compose([self._unit_rider(u) for u in units])

    def deliver(self, name, outs):
        for u, o in zip(self.riding.pop(name), outs):
            self._done(u, o)

    def _flush(self, item, kinds):
        hits = [p for p, u in enumerate(self.queue) if u["item"] == item and u["kind"] in kinds]
        if not hits:
            return
        prefix = self.queue[:hits[-1] + 1]
        del self.queue[:hits[-1] + 1]
        units = []
        for u in prefix:
            if not self._merge(units, u):
                units.append(dict(u))
        tag = "_".join(str(t) for t in item) + "_" + kinds[0]
        batches = [[]]
        for u in units:
            if any(v["item"] == u["item"] for v in batches[-1]):
                batches.append([])
            batches[-1].append(u)
        for b, batch in enumerate(batches):
            self._send(None, batch, functools.partial(_run_rider, f"alone_{tag}_{b}"))

    def begin(self):
        self._flush(("w", 0, "in_t"), ("gather",))

    def weight(self, i, k):
        item = ("w", i, k)
        self._flush(item, ("gather",))
        o = self.buf[item]
        return o.reshape(-1, o.shape[-1])

    def grads(self, i, group):
        for k, g in group.items():
            item = ("g", i, k)
            self.part[item] = g.reshape(N_DEV, g.shape[0] // N_DEV, g.shape[1])
            us = g.size // N_DEV * g.dtype.itemsize / 2 ** 20 * D2D_US_PER_MIB
            self.queue.append(dict(kind="sibling", item=item, rows=None, us=us))

    def total(self, k):
        out = self.n_layers
        for i in range(self.n_layers):
            item = ("g", i, k)
            self._flush(item, ("sibling",))
            self._flush(item, ("chips",))
            name = f"sum_l{i}_{k}"
            rider = self.rider(name, budget_us=self.part[item][0].size / 1.06e5) if k in SUMS_TRANSPOSED_LATER else None
            out = _sum5(name, self.part[item], self.rsib[item], self.buf[item], self.ids, i, out, rider=rider)
            if rider is not None:
                out, r_outs = out
                self.deliver(name, r_outs)
        return out

    def gather_small(self, name, arr):
        item = ("s", name)
        self.part[item] = arr
        waiting, self.queue = self.queue, []
        self._push_chunks("gather", item, arr.shape, arr.dtype)
        self.queue += waiting

    def gathered(self, name):
        item = ("s", name)
        self._flush(item, ("gather",))
        return self.buf[item]


def _row_tile(n, pref, mult):
    best = None
    t = mult
    while t <= min(n, pref):
        if n % t == 0:
            best = t
        t += mult
    return best if best is not None else n


def _sum8(name, slabs):
    _, n, k = slabs.shape
    tr = _row_tile(n, 128, 16)

    def body(s_ref, o_ref):
        acc = s_ref[0].astype(F32)
        for j in range(1, N_DEV):
            acc = acc + s_ref[j].astype(F32)
        o_ref[...] = acc

    return pl.pallas_call(
        body, name=name, grid=(n // tr,),
        in_specs=[pl.BlockSpec((N_DEV, tr, k), lambda i: (0, i, 0))],
        out_specs=pl.BlockSpec((tr, k), lambda i: (i, 0)),
        out_shape=jax.ShapeDtypeStruct((n, k), F32),
        compiler_params=_cparams("parallel"),
    )(slabs)


def _adamw(name, w, g, m, v, rider=None):
    n, k = w.shape[-2:]
    tr = _row_tile(n, 256, 8)

    def body(w_ref, g_ref, m_ref, v_ref, d_ref, m2_ref, v2_ref):
        gv = g_ref[...]
        m2 = ADAM_B1 * m_ref[...] + (1.0 - ADAM_B1) * gv
        v2 = ADAM_B2 * v_ref[...] + (1.0 - ADAM_B2) * jnp.square(gv)
        m_hat = m2 / (1.0 - ADAM_B1 ** ADAM_STEP)
        v_hat = v2 / (1.0 - ADAM_B2 ** ADAM_STEP)
        d_ref[...] = -ADAM_LR * (m_hat / (jnp.sqrt(v_hat) + ADAM_EPS) + ADAM_WD * w_ref[...])
        m2_ref[...] = m2
        v2_ref[...] = v2

    if w.ndim == 2:
        grid, spec = (n // tr,), pl.BlockSpec((tr, k), lambda i: (i, 0))
    else:
        grid, spec = (w.shape[0], n // tr), pl.BlockSpec((None, tr, k), lambda l, i: (l, i, 0))
    outs, r_outs = _pcall(name, body, grid, [spec] * 4, [spec] * 3, [jax.ShapeDtypeStruct(w.shape, F32)] * 3,
                          (w, g, m, v), ("parallel",) * len(grid), rider=rider)
    return outs if rider is None else (outs, r_outs)


def _pack(arrs):
    flat = jnp.concatenate([a.reshape(-1).astype(F32) for a in arrs])
    pad = (-flat.shape[0]) % 2048
    if pad:
        flat = jnp.concatenate([flat, jnp.zeros((pad,), F32)])
    return flat.reshape(-1, 128)


def _unpack(packed, shapes):
    flat = packed.reshape(-1)
    out, off = [], 0
    for shp in shapes:
        size = math.prod(shp)
        out.append(flat[off:off + size].reshape(shp))
        off += size
    return out


WEIGHT_NAMES = ("c_ctx", "w_ada", "b_ada", "w_in", "q_norm_g", "k_norm_g", "pool_w", "pool_scale", "sgu_ln_g",
                "sgu_ln_b", "sgu_w", "sgu_b", "conv_w", "w_br_attn", "w_br_pool", "w_br_sgu", "w_br_conv", "w_gate",
                "b_gate", "w_o", "ln1_g", "ln1_b", "w_ff_gate", "w_ff_up", "w_ff_down", "ln2_g", "ln2_b")
COL_SHARDED = {"w_in": "in_t", "w_gate": "gate_t", "w_ff_gate": "ffg_t", "w_ff_up": "ffu_t",
               "w_br_attn": "br0", "w_br_pool": "br1", "w_br_sgu": "br2", "w_br_conv": "br3"}
ROW_SHARDED = {"w_o": "o", "w_ff_down": "ffd"}
LAYER_SMALL = ("q_norm_g", "k_norm_g", "pool_w", "pool_scale", "sgu_ln_g", "sgu_ln_b", "sgu_w", "sgu_b", "b_gate",
               "ln1_g", "ln1_b", "ln2_g", "ln2_b")
SMALL_ORDER = ("c_ctx", "b_ada") + LAYER_SMALL + ("conv_w",)


def _train_step(a):
    n_layers, d = a["w_in"].shape[0], a["x"].shape[-1]
    rc = a["ctx"].shape[1]
    alpha = (2 * n_layers) ** 0.25
    mx, my, mc = [lax.axis_index(ax) for ax in MESH_AXES]
    me = 4 * mx + 2 * my + mc
    ada_w = a["w_ada"].shape[-1]
    cw_loc = a["conv_w"].shape[-1]

    n_c, n_cw = d, n_layers * 3 * cw_loc
    got = _exchange("gather_cond", [_pack([a["c"], a["conv_w"]])], False)[0].reshape(N_DEV, -1)
    c_all = got[:, :n_c]
    conv_w = got[:, n_c:n_c + n_cw].reshape(N_DEV, n_layers, 3, cw_loc).transpose(1, 2, 0, 3).reshape(n_layers, 3, -1)
    cond = jnp.concatenate([c_all, a["c_ctx"][None], jnp.zeros((16 - N_DEV - 1, d), F32)], axis=0)
    sil, sil_vjp = jax.vjp(jax.nn.silu, cond)
    sil = sil.astype(BF16)

    mod_cols = jnp.concatenate([_mm(f"ada{i}", sil, a["w_ada"][i], "nn", F32) for i in range(n_layers)], axis=0)
    got = _exchange("gather_mod", [mod_cols], False)[0]
    mod_all = got.reshape(N_DEV, n_layers, 16, ada_w).transpose(1, 2, 0, 3).reshape(n_layers, 16, -1)
    mod_all = mod_all + a["b_ada"][:, None, :]
    mod = jnp.stack([mod_all[:, N_DEV], lax.dynamic_index_in_dim(mod_all, me, axis=1, keepdims=False)], axis=1)

    comm = _Comm([{**{key: jnp.swapaxes(a[nm], 1, 2)[i].astype(BF16) for nm, key in COL_SHARDED.items()},
                   **{key: a[nm][i].astype(BF16) for nm, key in ROW_SHARDED.items()}} for i in range(n_layers)])
    comm.begin()
    sp = [{nm: a[nm][i] for nm in LAYER_SMALL} for i in range(n_layers)]
    for i in range(n_layers):
        sp[i]["conv_w"] = conv_w[i]

    xin = jnp.concatenate([a["ctx"][0], a["x"][0]], axis=0)
    loss_l, grad_x, dmod, dsp = _local_step(xin, a["loss_target"][0], mod, comm, sp, rc, alpha)
    loss = lax.psum(loss_l, MESH_AXES)
    grads = {}

    def transposed_home(nm):
        return nm in COL_SHARDED and a[nm].shape[-1] % 128 != 0

    small_parts = [dmod[:, 0], dmod[:, 1]]
    small_shapes = [dmod[:, 0].shape, dmod[:, 1].shape]
    for nm in LAYER_SMALL + ("conv_w",):
        part = jnp.stack([dsp[i][nm] for i in range(n_layers)])
        small_parts.append(part)
        small_shapes.append(part.shape)
    comm.gather_small("lat", _pack([dmod[:, 1]]))
    comm.gather_small("small", _pack(small_parts))

    delta, new_m, new_v = {}, {}, {}

    def adamw(nm):
        name = f"adamw_{nm}"
        there = transposed_home(nm)
        view = (lambda t: jnp.swapaxes(t, 1, 2)) if there else (lambda t: t)
        if nm in COL_SHARDED:
            g = comm.total(COL_SHARDED[nm])
            grads[nm] = jnp.swapaxes(g, 1, 2)
            g = g if there else grads[nm]
        elif nm in ROW_SHARDED:
            g = grads[nm] = comm.total(ROW_SHARDED[nm])
        else:
            g = grads[nm]
        res = _adamw(name, view(a[nm]), g, view(a["m_" + nm]), view(a["v_" + nm]))
        delta[nm], new_m[nm], new_v[nm] = [view(t) for t in res]

    for nm in ("w_ff_down", "w_ff_gate", "w_ff_up", "w_o", "w_br_attn", "w_br_pool", "w_br_sgu", "w_br_conv"):
        adamw(nm)

    got_lat, got_small = comm.gathered("lat"), comm.gathered("small")
    tot = _unpack(_sum8("sum_small", got_small), small_shapes)
    dmod_c, dmod_lat_sum = tot[0], tot[1]
    for nm, g in zip(LAYER_SMALL + ("conv_w",), tot[2:]):
        grads[nm] = g
    grads["conv_w"] = lax.dynamic_slice_in_dim(grads["conv_w"], me * cw_loc, cw_loc, axis=2)
    grads["b_ada"] = dmod_c + dmod_lat_sum
    dmod_lat_all = got_lat.reshape(N_DEV, -1)[:, :n_layers * 6 * d].reshape(N_DEV, n_layers, 6 * d)
    dm_rows = jnp.concatenate([dmod_lat_all.transpose(1, 0, 2), dmod_c[:, None, :],
                               jnp.zeros((n_layers, 16 - N_DEV - 1, 6 * d), F32)], axis=1)
    dm_cols = lax.dynamic_slice_in_dim(dm_rows, me * ada_w, ada_w, axis=2).astype(BF16)
    grads["w_ada"] = jnp.stack([_mm(f"dWada{i}", sil, dm_cols[i], "tn", F32) for i in range(n_layers)])
    dsil = None
    for i in range(n_layers):
        dsil = _mm(f"dsil{i}", dm_cols[i], a["w_ada"][i], "nt", F32, acc=dsil)
    got = _exchange("gather_dsil", [dsil], False)[0]
    dsil = _sum8("sum_dsil", got)
    grads["c_ctx"] = sil_vjp(dsil)[0][N_DEV]

    for nm in ("w_ada", "w_in", "w_gate"):
        adamw(nm)
    shapes = [a[nm].shape for nm in SMALL_ORDER]
    res = _adamw("adamw_small", _pack([a[nm] for nm in SMALL_ORDER]), _pack([grads[nm] for nm in SMALL_ORDER]),
                 _pack([a["m_" + nm] for nm in SMALL_ORDER]), _pack([a["v_" + nm] for nm in SMALL_ORDER]))
    for tree, packed in zip((delta, new_m, new_v), res):
        for nm, t in zip(SMALL_ORDER, _unpack(packed, shapes)):
            tree[nm] = t
    return (loss, grad_x[None], *[grads[nm] for nm in WEIGHT_NAMES], *[delta[nm] for nm in WEIGHT_NAMES],
            *[new_m[nm] for nm in WEIGHT_NAMES], *[new_v[nm] for nm in WEIGHT_NAMES])


def kernel(x, c, ctx, c_ctx, w_ada, b_ada, w_in, q_norm_g, k_norm_g, pool_w, pool_scale, sgu_ln_g, sgu_ln_b, sgu_w, sgu_b, conv_w, w_br_attn, w_br_pool, w_br_sgu, w_br_conv, w_gate, b_gate, w_o, ln1_g, ln1_b, w_ff_gate, w_ff_up, w_ff_down, ln2_g, ln2_b, loss_target, m_c_ctx, m_w_ada, m_b_ada, m_w_in, m_q_norm_g, m_k_norm_g, m_pool_w, m_pool_scale, m_sgu_ln_g, m_sgu_ln_b, m_sgu_w, m_sgu_b, m_conv_w, m_w_br_attn, m_w_br_pool, m_w_br_sgu, m_w_br_conv, m_w_gate, m_b_gate, m_w_o, m_ln1_g, m_ln1_b, m_w_ff_gate, m_w_ff_up, m_w_ff_down, m_ln2_g, m_ln2_b, v_c_ctx, v_w_ada, v_b_ada, v_w_in, v_q_norm_g, v_k_norm_g, v_pool_w, v_pool_scale, v_sgu_ln_g, v_sgu_ln_b, v_sgu_w, v_sgu_b, v_conv_w, v_w_br_attn, v_w_br_pool, v_w_br_sgu, v_w_br_conv, v_w_gate, v_b_gate, v_w_o, v_ln1_g, v_ln1_b, v_w_ff_gate, v_w_ff_up, v_w_ff_down, v_ln2_g, v_ln2_b):
    names = list(WEIGHT_NAMES)
    args = dict(zip(
        ["x", "c", "ctx"] + names + ["loss_target"] + ["m_" + n for n in names] + ["v_" + n for n in names],
        (x, c, ctx, c_ctx, w_ada, b_ada, w_in, q_norm_g, k_norm_g, pool_w, pool_scale, sgu_ln_g, sgu_ln_b, sgu_w, sgu_b, conv_w, w_br_attn, w_br_pool, w_br_sgu, w_br_conv, w_gate, b_gate, w_o, ln1_g, ln1_b, w_ff_gate, w_ff_up, w_ff_down, ln2_g, ln2_b, loss_target, m_c_ctx, m_w_ada, m_b_ada, m_w_in, m_q_norm_g, m_k_norm_g, m_pool_w, m_pool_scale, m_sgu_ln_g, m_sgu_ln_b, m_sgu_w, m_sgu_b, m_conv_w, m_w_br_attn, m_w_br_pool, m_w_br_sgu, m_w_br_conv, m_w_gate, m_b_gate, m_w_o, m_ln1_g, m_ln1_b, m_w_ff_gate, m_w_ff_up, m_w_ff_down, m_ln2_g, m_ln2_b, v_c_ctx, v_w_ada, v_b_ada, v_w_in, v_q_norm_g, v_k_norm_g, v_pool_w, v_pool_scale, v_sgu_ln_g, v_sgu_ln_b, v_sgu_w, v_sgu_b, v_conv_w, v_w_br_attn, v_w_br_pool, v_w_br_sgu, v_w_br_conv, v_w_gate, v_b_gate, v_w_o, v_ln1_g, v_ln1_b, v_w_ff_gate, v_w_ff_up, v_w_ff_down, v_ln2_g, v_ln2_b)))
    return _train_step(args)
```

```python
import functools
import math

import jax
import jax.numpy as jnp
from jax import lax
from jax.experimental import pallas as pl
from jax.experimental.pallas import tpu as pltpu

F32 = jnp.float32
BF16 = jnp.bfloat16

N_DEV = 8
MESH_AXES = ("x", "y", "c")
V7X_VMEM_LIMIT_BYTES = 56 * 1024 * 1024

GRID_W = 64
HEAD_DIM = 128
N_HEADS = 8
N_KV_HEADS = 2
KV_GROUP = N_HEADS // N_KV_HEADS
Q_W = N_HEADS * HEAD_DIM
KV_W = N_KV_HEADS * HEAD_DIM
ROPE_THETA = 10000.0
ROPE_AXIS_DIM = HEAD_DIM // 2
POOL_WINDOWS = (2, 4, 8, 16)
GC = 128
N_GROUPS = 4
BR_W = N_GROUPS * GC
SGU_CHUNK = 128
N_BRANCH = 4
LN_EPS = 1e-5
RMS_EPS = 1e-6
OFF_K = Q_W
OFF_V = OFF_K + KV_W
OFF_POOL = OFF_V + KV_W
OFF_U = OFF_POOL + BR_W
OFF_VG = OFF_U + BR_W
OFF_CB = OFF_VG + BR_W
OFF_CC = OFF_CB + BR_W
OFF_CX = OFF_CC + BR_W
IN_W = OFF_CX + BR_W
ATT_SCALE = HEAD_DIM ** -0.5

ADAM_LR = 0.001
ADAM_B1 = 0.9
ADAM_B2 = 0.999
ADAM_EPS = 1e-08
ADAM_WD = 0.01
ADAM_STEP = 10

_NT = (((1,), (1,)), ((), ()))
_NN = (((1,), (0,)), ((), ()))
_TN = (((0,), (0,)), ((), ()))
_DIMS = {"nt": _NT, "nn": _NN, "tn": _TN}


def _cparams(*sem):
    return pltpu.CompilerParams(dimension_semantics=sem, vmem_limit_bytes=V7X_VMEM_LIMIT_BYTES)


def _tile(dim, pref):
    best = None
    t = 128
    while t <= min(dim, pref):
        if dim % t == 0:
            best = t
        t += 128
    return best if best is not None else dim


def _dot(a, b, dims):
    return lax.dot_general(a.astype(BF16), b.astype(BF16), dims, preferred_element_type=F32)


class _Rider:
    def __init__(self, inputs, out_shapes, scratch, start, finish, aliases=None):
        self.inputs, self.out_shapes, self.scratch = list(inputs), list(out_shapes), list(scratch)
        self.start, self.finish = start, finish
        self.aliases = dict(aliases or {})


def _compose(riders):
    inputs, outs, scratch, aliases, spans = [], [], [], {}, []
    for rd in riders:
        i0, o0, s0 = len(inputs), len(outs), len(scratch)
        aliases.update({i0 + p: o0 + q for p, q in rd.aliases.items()})
        inputs += rd.inputs
        outs += rd.out_shapes
        scratch += rd.scratch
        spans.append((slice(i0, len(inputs)), slice(o0, len(outs)), slice(s0, len(scratch))))

    def start(ins, os, sems):
        for rd, (si, so, ss) in zip(riders, spans):
            rd.start(ins[si], os[so], sems[ss])

    def finish(ins, os, sems):
        for rd, (si, so, ss) in zip(riders, spans):
            rd.finish(ins[si], os[so], sems[ss])

    return _Rider(inputs, outs, scratch, start, finish, aliases)


def _pcall(name, body, grid, in_specs, out_specs, out_shape, args, sem, scratch=(), rider=None, prefetch=None,
           aliases=None):
    in_specs, out_specs, out_shape, scratch = list(in_specs), list(out_specs), list(out_shape), list(scratch)
    n_pre = 0 if prefetch is None else 1
    n_in, n_out, n_scr = len(in_specs), len(out_specs), len(scratch)
    r_in, r_out = (len(rider.inputs), len(rider.out_shapes)) if rider is not None else (0, 0)
    any_spec = pl.BlockSpec(memory_space=pl.ANY)
    io_aliases = {n_pre + p: q for p, q in (aliases or {}).items()}
    if rider is not None:
        io_aliases.update({n_pre + n_in + p: n_out + q for p, q in rider.aliases.items()})
        in_specs, out_specs = in_specs + [any_spec] * r_in, out_specs + [any_spec] * r_out
        out_shape, scratch = out_shape + rider.out_shapes, scratch + rider.scratch
        args, sem = (*args, *rider.inputs), ["arbitrary"] * len(grid)

    def wrapped(*refs):
        pre, refs = refs[:n_pre], refs[n_pre:]
        if rider is None:
            return body(*pre, *refs)
        ins, refs = refs[:n_in], refs[n_in:]
        r_ins, refs = refs[:r_in], refs[r_in:]
        outs, refs = refs[:n_out], refs[n_out:]
        r_outs, refs = refs[:r_out], refs[r_out:]
        scr, r_scr = refs[:n_scr], refs[n_scr:]
        first = functools.reduce(jnp.logical_and, [pl.program_id(ax) == 0 for ax in range(len(grid))])
        last = functools.reduce(jnp.logical_and, [pl.program_id(ax) == grid[ax] - 1 for ax in range(len(grid))])

        @pl.when(first)
        def _():
            rider.start(r_ins, r_outs, r_scr)

        body(*pre, *ins, *outs, *scr)

        @pl.when(last)
        def _():
            rider.finish(r_ins, r_outs, r_scr)

    if prefetch is None:
        res = pl.pallas_call(wrapped, name=name, grid=grid, in_specs=in_specs, out_specs=out_specs, out_shape=out_shape,
                             scratch_shapes=scratch, input_output_aliases=io_aliases,
                             compiler_params=_cparams(*sem))(*args)
    else:
        grid_spec = pltpu.PrefetchScalarGridSpec(num_scalar_prefetch=1, grid=grid, in_specs=in_specs,
                                                 out_specs=out_specs, scratch_shapes=scratch)
        res = pl.pallas_call(wrapped, name=name, grid_spec=grid_spec, out_shape=out_shape,
                             input_output_aliases=io_aliases, compiler_params=_cparams(*sem))(prefetch, *args)
    return list(res[:n_out]), list(res[n_out:])


V7X_MM_VMEM_BUDGET = 40 * 1024 * 1024


def _mm_plan(form, m, n, k, a_size, b_size, o_size, has_acc):
    tk = k if k <= 2816 else _tile(k, 2816)
    nk = k // tk
    tn = n if (form == "tn" and n <= 2048) else _tile(n, 512)
    for tm in sorted({m} | {t for t in range(128, m, 128) if m % t == 0}, reverse=True):
        need = 2 * (tm * tk * a_size + tn * tk * b_size + tm * tn * o_size) + tm * tn * 4 * (2 if nk > 1 else 1)
        need += 2 * tm * tn * 4 if has_acc else 0
        if need <= V7X_MM_VMEM_BUDGET:
            return tm, tn, tk
    return _tile(m, 128), tn, tk


def _mm(name, a, b, form, out_dtype, acc=None, rider=None):
    if form == "nt":
        (m, k), (n, k2) = a.shape, b.shape
    elif form == "nn":
        (m, k), (k2, n) = a.shape, b.shape
    else:
        (k, m), (k2, n) = a.shape, b.shape
    assert k == k2, (name, a.shape, b.shape)
    has_acc = acc is not None
    tm, tn, tk = _mm_plan(form, m, n, k, a.dtype.itemsize, b.dtype.itemsize, jnp.dtype(out_dtype).itemsize, has_acc)
    nk = k // tk
    a_spec = {"nt": pl.BlockSpec((tm, tk), lambda i, j, kk: (i, kk)),
              "nn": pl.BlockSpec((tm, tk), lambda i, j, kk: (i, kk)),
              "tn": pl.BlockSpec((tk, tm), lambda i, j, kk: (kk, i))}[form]
    b_spec = {"nt": pl.BlockSpec((tn, tk), lambda i, j, kk: (j, kk)),
              "nn": pl.BlockSpec((tk, tn), lambda i, j, kk: (kk, j)),
              "tn": pl.BlockSpec((tk, tn), lambda i, j, kk: (kk, j))}[form]
    o_spec = pl.BlockSpec((tm, tn), lambda i, j, kk: (i, j))
    dims = _DIMS[form]

    def body(*refs):
        a_ref, b_ref = refs[0], refs[1]
        c_ref = refs[2] if has_acc else None
        o_ref = refs[3] if has_acc else refs[2]

        def finish(r):
            if has_acc:
                r = r + c_ref[...]
            o_ref[...] = r.astype(o_ref.dtype)

        if nk == 1:
            finish(_dot(a_ref[...], b_ref[...], dims))
            return
        acc_ref = refs[-1]
        kk = pl.program_id(2)

        @pl.when(kk == 0)
        def _():
            acc_ref[...] = _dot(a_ref[...], b_ref[...], dims)

        @pl.when(kk > 0)
        def _():
            acc_ref[...] += _dot(a_ref[...], b_ref[...], dims)

        @pl.when(kk == nk - 1)
        def _():
            finish(acc_ref[...])

    in_specs = [a_spec, b_spec] + ([o_spec] if has_acc else [])
    args = (a, b) + ((acc,) if has_acc else ())
    outs, r_outs = _pcall(name, body, (m // tm, n // tn, nk), in_specs, [o_spec],
                          [jax.ShapeDtypeStruct((m, n), out_dtype)], args, ("parallel", "parallel", "arbitrary"),
                          scratch=[pltpu.VMEM((tm, tn), F32)] if nk > 1 else [], rider=rider)
    return outs[0] if rider is None else (outs[0], r_outs)


def _mm_fused(name, a, bs, epilogue, tile_ins, out_dtypes, rider=None):
    (m, k), (n, _) = a.shape, bs[0].shape
    tn = _tile(n, 512)
    tm = None
    for cand in sorted({m} | {t for t in range(128, m, 128) if m % t == 0}, reverse=True):
        per_tile = sum(t.dtype.itemsize for t in tile_ins) + sum(jnp.dtype(d).itemsize for d in out_dtypes)
        need = 2 * (cand * k * a.dtype.itemsize + len(bs) * tn * k * bs[0].dtype.itemsize + cand * tn * per_tile)
        need += (len(bs) + 2) * cand * tn * 4
        if need <= V7X_MM_VMEM_BUDGET:
            tm = cand
            break
    assert tm is not None, name
    n_b, n_t = len(bs), len(tile_ins)
    tile = pl.BlockSpec((tm, tn), lambda i, j: (i, j))

    def body(a_ref, *refs):
        prods = [_dot(a_ref[...], r[...], _NT) for r in refs[:n_b]]
        outs = epilogue(prods, [r[...] for r in refs[n_b:n_b + n_t]])
        for r, o in zip(refs[n_b + n_t:], outs):
            r[...] = o.astype(r.dtype)

    outs, r_outs = _pcall(name, body, (m // tm, n // tn),
                          [pl.BlockSpec((tm, k), lambda i, j: (i, 0))] + [pl.BlockSpec((tn, k), lambda i, j: (j, 0))] * n_b
                          + [tile] * n_t, [tile] * len(out_dtypes),
                          [jax.ShapeDtypeStruct((m, n), d) for d in out_dtypes], (a, *bs, *tile_ins),
                          ("parallel", "parallel"), rider=rider)
    return outs if rider is None else (outs, r_outs)


def _rows(name, fn, n_rows, tm, nbc, row_ins, type_ins, row_outs, acc_outs, rider=None):
    n_ri, n_ti, n_ro, n_ao = len(row_ins), len(type_ins), len(row_outs), len(acc_outs)

    def row_map(i, cb, roff):
        return (jnp.maximum(i - roff, 0), cb)

    def type_map(i):
        return (jnp.where(i >= nbc, 1, 0), 0, 0)

    in_specs, args = [], []
    for arr, cb, width, roff in row_ins:
        in_specs.append(pl.BlockSpec((tm, width), functools.partial(row_map, cb=cb, roff=roff)))
        args.append(arr)
    for arr in type_ins:
        in_specs.append(pl.BlockSpec((None, 1, arr.shape[-1]), type_map))
        args.append(arr)
    out_shape, out_specs = [], []
    for total, width, dtype, roff in row_outs:
        out_shape.append(jax.ShapeDtypeStruct((total, width), dtype))
        out_specs.append(pl.BlockSpec((tm, width), functools.partial(row_map, cb=0, roff=roff)))
    for width in acc_outs:
        out_shape.append(jax.ShapeDtypeStruct((2, 1, width), F32))
        out_specs.append(pl.BlockSpec((None, 1, width), type_map))
    n_in = n_ri + n_ti

    def body(*refs):
        i = pl.program_id(0)
        outs = fn(*[r[...] for r in refs[:n_in]])
        if not isinstance(outs, (tuple, list)):
            outs = (outs,)
        assert len(outs) == n_ro + n_ao, (name, len(outs))
        for r, o in zip(refs[n_in:n_in + n_ro], outs[:n_ro]):
            r[...] = o.astype(r.dtype)
        if n_ao:
            first = jnp.logical_or(i == 0, i == nbc)
            for r, o in zip(refs[n_in + n_ro:], outs[n_ro:]):
                o = jnp.broadcast_to(o.astype(F32), r.shape)

                @pl.when(first)
                def _(r=r, o=o):
                    r[...] = o

                @pl.when(jnp.logical_not(first))
                def _(r=r, o=o):
                    r[...] += o

    outs, r_outs = _pcall(name, body, (n_rows // tm,), in_specs, out_specs, out_shape, args, ("arbitrary",),
                          rider=rider)
    return outs if rider is None else (outs, r_outs)


def _vjp_fn(f, n_row, n_cot, keep=None):
    def g(*args):
        prim = args[:n_row] + args[n_row + n_cot:]
        cots = args[n_row:n_row + n_cot]
        out, vjp = jax.vjp(f, *prim)
        grads = vjp(tuple(cots) if isinstance(out, (tuple, list)) else cots[0])
        return grads if keep is None else tuple(grads[j] for j in keep)
    return g


def _typed(v):
    v = v.reshape(1, 1, -1)
    return jnp.concatenate([v, v], axis=0)


def _ln(x, g, b):
    mu = jnp.mean(x, axis=-1, keepdims=True)
    var = jnp.mean(jnp.square(x - mu), axis=-1, keepdims=True)
    return (x - mu) * lax.rsqrt(var + LN_EPS) * g + b


def _f_mod(x, sc, sh):
    return x * (1.0 + sc) + sh


def _make_f_ln(alpha, with_mod):
    def f(x, o, gate, lng, lnb, *mod):
        xn = _ln(alpha * x + gate * o, lng, lnb)
        if with_mod:
            sc, sh = mod
            return xn, xn * (1.0 + sc) + sh
        return xn
    return f


@jax.custom_vjp
def _rot(y):
    lane = lax.broadcasted_iota(jnp.int32, y.shape, 1)
    return jnp.where(lane % 64 < 32, pltpu.roll(y, 96, axis=1), pltpu.roll(y, 32, axis=1))


_rot.defvjp(lambda y: (_rot(y), None), lambda _, g: (_rot(g),))


def _f_prep(p, cos, sin, qg, kg):
    def head(xh, g):
        ms = jnp.mean(jnp.square(xh), axis=-1, keepdims=True)
        y = xh * lax.rsqrt(ms + RMS_EPS) * g
        return y * cos + _rot(y) * sin
    q = jnp.concatenate([head(p[:, h * HEAD_DIM:(h + 1) * HEAD_DIM], qg) for h in range(N_HEADS)], axis=1)
    k = jnp.concatenate([head(p[:, OFF_K + h * HEAD_DIM:OFF_K + (h + 1) * HEAD_DIM], kg)
                         for h in range(N_KV_HEADS)], axis=1)
    return q, k, p[:, OFF_V:OFF_POOL]


def _f_gate(g, t0, t1, t2, t3, b):
    d = t0.shape[-1]
    ts = (t0, t1, t2, t3)
    terms = [jax.nn.sigmoid(g[:, k * d:(k + 1) * d] + b[:, k * d:(k + 1) * d]) * ts[k] for k in range(N_BRANCH)]
    return terms[0] + terms[1] + terms[2] + terms[3]


def _f_swiglu(a, b):
    return jax.nn.silu(a) * b


def _softmax(raw):
    e = jnp.exp2((raw - jnp.max(raw, axis=-1, keepdims=True)) * (ATT_SCALE * math.log2(math.e)))
    return e / jnp.sum(e, axis=-1, keepdims=True)


def _attn_fwd(name, q, k, v, rc, ctx_queries, tq=256, rider=None):
    r = q.shape[0]
    assert rc % tq == 0 and r % tq == 0
    nqc = rc // tq

    def body(q_ref, k_ref, v_ref, o_ref):
        qi = pl.program_id(1)

        def attend(nk):
            p = _softmax(_dot(q_ref[...], k_ref[0:nk, :], _NT))
            o_ref[...] = _dot(p, v_ref[0:nk, :], _NN).astype(o_ref.dtype)

        @pl.when(qi < nqc)
        def _():
            if ctx_queries:
                attend(rc)
            else:
                o_ref[...] = jnp.zeros_like(o_ref)

        @pl.when(qi >= nqc)
        def _():
            attend(r)

    outs, r_outs = _pcall(
        name, body, (N_HEADS, r // tq),
        [pl.BlockSpec((tq, HEAD_DIM), lambda h, i: (i, h)),
         pl.BlockSpec((r, HEAD_DIM), lambda h, i: (0, h // KV_GROUP)),
         pl.BlockSpec((r, HEAD_DIM), lambda h, i: (0, h // KV_GROUP))],
        [pl.BlockSpec((tq, HEAD_DIM), lambda h, i: (i, h))],
        [jax.ShapeDtypeStruct((r, Q_W), BF16)], (q, k, v), ("parallel", "parallel"), rider=rider)
    return outs[0] if rider is None else (outs[0], r_outs)


def _attn_bwd(name, q, k, v, do, rc, ctx_queries, tq=256, rider=None):
    r = q.shape[0]
    nqc = rc // tq

    def body(q_ref, k_ref, v_ref, do_ref, dq_ref, dk_ref, dv_ref):
        g, qi = pl.program_id(1), pl.program_id(2)

        @pl.when(jnp.logical_and(g == 0, qi == 0))
        def _():
            dk_ref[...] = jnp.zeros_like(dk_ref)
            dv_ref[...] = jnp.zeros_like(dv_ref)

        def grad(nk):
            qb, kb, vb = q_ref[...], k_ref[0:nk, :], v_ref[0:nk, :]
            dob = do_ref[...].astype(BF16)
            p = _softmax(_dot(qb, kb, _NT))
            dv_ref[0:nk, :] += _dot(p, dob, _TN)
            dp = _dot(dob, vb, _NT)
            ds = p * (dp - jnp.sum(dp * p, axis=-1, keepdims=True)) * ATT_SCALE
            dq_ref[...] = _dot(ds, kb, _NN)
            dk_ref[0:nk, :] += _dot(ds, qb, _TN)

        @pl.when(qi < nqc)
        def _():
            if ctx_queries:
                grad(rc)
            else:
                dq_ref[...] = jnp.zeros_like(dq_ref)

        @pl.when(qi >= nqc)
        def _():
            grad(r)

    def qmap(kv, g, i):
        return (i, kv * KV_GROUP + g)

    def kvmap(kv, g, i):
        return (0, kv)

    outs, r_outs = _pcall(
        name, body, (N_KV_HEADS, KV_GROUP, r // tq),
        [pl.BlockSpec((tq, HEAD_DIM), qmap), pl.BlockSpec((r, HEAD_DIM), kvmap),
         pl.BlockSpec((r, HEAD_DIM), kvmap), pl.BlockSpec((tq, HEAD_DIM), qmap)],
        [pl.BlockSpec((tq, HEAD_DIM), qmap), pl.BlockSpec((r, HEAD_DIM), kvmap), pl.BlockSpec((r, HEAD_DIM), kvmap)],
        [jax.ShapeDtypeStruct((r, Q_W), F32), jax.ShapeDtypeStruct((r, KV_W), F32),
         jax.ShapeDtypeStruct((r, KV_W), F32)],
        (q, k, v, do), ("arbitrary", "arbitrary", "arbitrary"), rider=rider)
    return outs if rider is None else (outs, r_outs)


def _segments(shape, rc):
    t = lax.broadcasted_iota(jnp.int32, shape, 0)
    lo = jnp.where(t < rc, 0, rc)
    hi = jnp.where(t < rc, rc, shape[0])
    return t, lo, hi


def _shifted(x, o, t, lo, hi):
    n = x.shape[0]
    sh = pltpu.roll(x, (-o) % n, axis=0)
    return jnp.where(jnp.logical_and(t + o >= lo, t + o < hi), sh, 0.0)


def _winsum(x, left, right, t, lo, hi):
    acc = x
    for o in range(-left, right + 1):
        if o != 0:
            acc = acc + _shifted(x, o, t, lo, hi)
    return acc


def _pool_parts(z, g, t, lo, hi):
    w = POOL_WINDOWS[g]
    left = w // 2
    right = w - 1 - left
    count = (jnp.minimum(t + right + 1, hi) - jnp.maximum(t - left, lo)).astype(F32)
    return _winsum(z, left, right, t, lo, hi) / count - z, count, left, right


def _pool_fwd(name, p, pool_w, pool_scale, rc):
    r = p.shape[0]

    def body(z_ref, w_ref, s_ref, y_ref):
        t, lo, hi = _segments((r, GC), rc)
        for g in range(N_GROUPS):
            cols = slice(g * GC, (g + 1) * GC)
            d, _, _, _ = _pool_parts(z_ref[:, cols], g, t, lo, hi)
            y_ref[:, cols] = (_dot(d, w_ref[g], _NN) * s_ref[:, cols]).astype(y_ref.dtype)

    return pl.pallas_call(
        body, name=name, grid=(1,),
        in_specs=[pl.BlockSpec((r, BR_W), lambda i: (0, OFF_POOL // BR_W)),
                  pl.BlockSpec((N_GROUPS, GC, GC), lambda i: (0, 0, 0)),
                  pl.BlockSpec((1, BR_W), lambda i: (0, 0))],
        out_specs=pl.BlockSpec((r, BR_W), lambda i: (0, 0)),
        out_shape=jax.ShapeDtypeStruct((r, BR_W), BF16),
        compiler_params=_cparams("arbitrary"),
    )(p, pool_w, pool_scale.reshape(1, BR_W))


def _pool_bwd(name, p, pool_w, pool_scale, dy, rc):
    r = p.shape[0]

    def body(z_ref, w_ref, s_ref, dy_ref, dz_ref, dw_ref, ds_ref):
        t, lo, hi = _segments((r, GC), rc)
        for g in range(N_GROUPS):
            cols = slice(g * GC, (g + 1) * GC)
            d, count, left, right = _pool_parts(z_ref[:, cols], g, t, lo, hi)
            dyg = dy_ref[:, cols]
            ds_ref[:, cols] = jnp.sum(dyg * _dot(d, w_ref[g], _NN), axis=0, keepdims=True)
            dlin = dyg * s_ref[:, cols]
            dw_ref[g] = _dot(d, dlin, _TN)
            dd = _dot(dlin, w_ref[g], _NT)
            dz_ref[:, cols] = (_winsum(dd / count, right, left, t, lo, hi) - dd).astype(dz_ref.dtype)

    return pl.pallas_call(
        body, name=name, grid=(1,),
        in_specs=[pl.BlockSpec((r, BR_W), lambda i: (0, OFF_POOL // BR_W)),
                  pl.BlockSpec((N_GROUPS, GC, GC), lambda i: (0, 0, 0)),
                  pl.BlockSpec((1, BR_W), lambda i: (0, 0)),
                  pl.BlockSpec((r, BR_W), lambda i: (0, 0))],
        out_specs=[pl.BlockSpec((r, BR_W), lambda i: (0, 0)),
                   pl.BlockSpec((N_GROUPS, GC, GC), lambda i: (0, 0, 0)),
                   pl.BlockSpec((1, BR_W), lambda i: (0, 0))],
        out_shape=[jax.ShapeDtypeStruct((r, BR_W), BF16), jax.ShapeDtypeStruct((N_GROUPS, GC, GC), F32),
                   jax.ShapeDtypeStruct((1, BR_W), F32)],
        compiler_params=_cparams("arbitrary"),
    )(p, pool_w, pool_scale.reshape(1, BR_W), dy)


def _f_sgu_v(pvg, lng, lnb):
    return _ln(jax.nn.gelu(pvg), lng, lnb)


def _sgu_fwd(name, p, ln_g, ln_b, sgu_w, sgu_b):
    r = p.shape[0]

    def body(pu_ref, pv_ref, g_ref, b_ref, w_ref, sb_ref, y_ref):
        vn = _f_sgu_v(pv_ref[...], g_ref[...], b_ref[...])
        u = jax.nn.gelu(pu_ref[...])
        for g in range(N_GROUPS):
            cols = slice(g * GC, (g + 1) * GC)
            s = _dot(w_ref[g], vn[:, cols], _NN) + sb_ref[g]
            y_ref[:, cols] = (u[:, cols] * s).astype(y_ref.dtype)

    return pl.pallas_call(
        body, name=name, grid=(r // SGU_CHUNK,),
        in_specs=[pl.BlockSpec((SGU_CHUNK, BR_W), lambda i: (i, OFF_U // BR_W)),
                  pl.BlockSpec((SGU_CHUNK, BR_W), lambda i: (i, OFF_VG // BR_W)),
                  pl.BlockSpec((1, BR_W), lambda i: (0, 0)), pl.BlockSpec((1, BR_W), lambda i: (0, 0)),
                  pl.BlockSpec((N_GROUPS, GC, GC), lambda i: (0, 0, 0)),
                  pl.BlockSpec((N_GROUPS, SGU_CHUNK, 1), lambda i: (0, 0, 0))],
        out_specs=pl.BlockSpec((SGU_CHUNK, BR_W), lambda i: (i, 0)),
        out_shape=jax.ShapeDtypeStruct((r, BR_W), BF16),
        compiler_params=_cparams("parallel"),
    )(p, p, ln_g.reshape(1, BR_W), ln_b.reshape(1, BR_W), sgu_w, sgu_b.reshape(N_GROUPS, SGU_CHUNK, 1))


def _sgu_bwd(name, p, ln_g, ln_b, sgu_w, sgu_b, dy):
    r = p.shape[0]

    def body(pu_ref, pv_ref, g_ref, b_ref, w_ref, sb_ref, dy_ref, dp_ref, dg_ref, db_ref, dw_ref, dsb_ref):
        i = pl.program_id(0)

        @pl.when(i == 0)
        def _():
            for ref in (dg_ref, db_ref, dw_ref, dsb_ref):
                ref[...] = jnp.zeros_like(ref)

        vn, vjp_v = jax.vjp(_f_sgu_v, pv_ref[...], g_ref[...], b_ref[...])
        u, vjp_u = jax.vjp(jax.nn.gelu, pu_ref[...])
        dy = dy_ref[...]
        du, dvn = [], []
        for g in range(N_GROUPS):
            cols = slice(g * GC, (g + 1) * GC)
            s = _dot(w_ref[g], vn[:, cols], _NN) + sb_ref[g]
            du.append(dy[:, cols] * s)
            ds = dy[:, cols] * u[:, cols]
            dsb_ref[g] += jnp.sum(ds, axis=1, keepdims=True)
            dw_ref[g] += _dot(ds, vn[:, cols], _NT)
            dvn.append(_dot(w_ref[g], ds, _TN))
        (dpu,) = vjp_u(jnp.concatenate(du, axis=1))
        dpv, dg, db = vjp_v(jnp.concatenate(dvn, axis=1))
        dp_ref[:, 0:BR_W] = dpu.astype(dp_ref.dtype)
        dp_ref[:, BR_W:2 * BR_W] = dpv.astype(dp_ref.dtype)
        dg_ref[...] += dg
        db_ref[...] += db

    vec = pl.BlockSpec((1, BR_W), lambda i: (0, 0))
    wsp = pl.BlockSpec((N_GROUPS, GC, GC), lambda i: (0, 0, 0))
    bsp = pl.BlockSpec((N_GROUPS, SGU_CHUNK, 1), lambda i: (0, 0, 0))
    return pl.pallas_call(
        body, name=name, grid=(r // SGU_CHUNK,),
        in_specs=[pl.BlockSpec((SGU_CHUNK, BR_W), lambda i: (i, OFF_U // BR_W)),
                  pl.BlockSpec((SGU_CHUNK, BR_W), lambda i: (i, OFF_VG // BR_W)),
                  vec, vec, wsp, bsp, pl.BlockSpec((SGU_CHUNK, BR_W), lambda i: (i, 0))],
        out_specs=[pl.BlockSpec((SGU_CHUNK, 2 * BR_W), lambda i: (i, 0)), vec, vec, wsp, bsp],
        out_shape=[jax.ShapeDtypeStruct((r, 2 * BR_W), BF16), jax.ShapeDtypeStruct((1, BR_W), F32),
                   jax.ShapeDtypeStruct((1, BR_W), F32), jax.ShapeDtypeStruct((N_GROUPS, GC, GC), F32),
                   jax.ShapeDtypeStruct((N_GROUPS, SGU_CHUNK, 1), F32)],
        compiler_params=_cparams("arbitrary"),
    )(p, p, ln_g.reshape(1, BR_W), ln_b.reshape(1, BR_W), sgu_w, sgu_b.reshape(N_GROUPS, SGU_CHUNK, 1), dy)


def _conv_w8(conv_w):
    return jnp.concatenate([conv_w, jnp.zeros((8 - conv_w.shape[0], conv_w.shape[1]), F32)], axis=0)


def _conv_fwd(name, p, conv_w, rc):
    r = p.shape[0]

    def body(cb_ref, cc_ref, cx_ref, w_ref, y_ref):
        t, lo, hi = _segments((r, GC), rc)
        z = cc_ref[...] * cx_ref[...]
        w = w_ref[...]
        c = _shifted(z, -1, t, lo, hi) * w[0:1] + z * w[1:2] + _shifted(z, 1, t, lo, hi) * w[2:3]
        y_ref[...] = (cb_ref[...] * c).astype(y_ref.dtype)

    nb = OFF_CB // GC
    return pl.pallas_call(
        body, name=name, grid=(N_GROUPS,),
        in_specs=[pl.BlockSpec((r, GC), lambda j: (0, nb + j)),
                  pl.BlockSpec((r, GC), lambda j: (0, nb + N_GROUPS + j)),
                  pl.BlockSpec((r, GC), lambda j: (0, nb + 2 * N_GROUPS + j)),
                  pl.BlockSpec((8, GC), lambda j: (0, j))],
        out_specs=pl.BlockSpec((r, GC), lambda j: (0, j)),
        out_shape=jax.ShapeDtypeStruct((r, BR_W), BF16),
        compiler_params=_cparams("parallel"),
    )(p, p, p, _conv_w8(conv_w))


def _conv_bwd(name, p, conv_w, dy, rc):
    r = p.shape[0]

    def body(cb_ref, cc_ref, cx_ref, w_ref, dy_ref, dcb_ref, dcc_ref, dcx_ref, dw_ref):
        t, lo, hi = _segments((r, GC), rc)
        cc, cx, w, dy = cc_ref[...], cx_ref[...], w_ref[...], dy_ref[...]
        z = cc * cx
        zp, zn = _shifted(z, -1, t, lo, hi), _shifted(z, 1, t, lo, hi)
        dcb_ref[...] = (dy * (zp * w[0:1] + z * w[1:2] + zn * w[2:3])).astype(dcb_ref.dtype)
        dc = dy * cb_ref[...]
        dw_ref[...] = jnp.concatenate(
            [jnp.sum(dc * zp, axis=0, keepdims=True), jnp.sum(dc * z, axis=0, keepdims=True),
             jnp.sum(dc * zn, axis=0, keepdims=True), jnp.zeros((5, GC), F32)], axis=0)
        dz = dc * w[1:2] + _shifted(dc, 1, t, lo, hi) * w[0:1] + _shifted(dc, -1, t, lo, hi) * w[2:3]
        dcc_ref[...] = (dz * cx).astype(dcc_ref.dtype)
        dcx_ref[...] = (dz * cc).astype(dcx_ref.dtype)

    nb = OFF_CB // GC
    return pl.pallas_call(
        body, name=name, grid=(N_GROUPS,),
        in_specs=[pl.BlockSpec((r, GC), lambda j: (0, nb + j)),
                  pl.BlockSpec((r, GC), lambda j: (0, nb + N_GROUPS + j)),
                  pl.BlockSpec((r, GC), lambda j: (0, nb + 2 * N_GROUPS + j)),
                  pl.BlockSpec((8, GC), lambda j: (0, j)),
                  pl.BlockSpec((r, GC), lambda j: (0, j))],
        out_specs=[pl.BlockSpec((r, GC), lambda j: (0, j))] * 3 + [pl.BlockSpec((8, GC), lambda j: (0, j))],
        out_shape=[jax.ShapeDtypeStruct((r, BR_W), BF16)] * 3 + [jax.ShapeDtypeStruct((8, BR_W), F32)],
        compiler_params=_cparams("parallel"),
    )(p, p, p, _conv_w8(conv_w), dy)


def _rope_tables(rc, n):
    rows = n // GRID_W
    row = jnp.repeat(jnp.arange(rows), GRID_W).astype(F32)
    col = jnp.tile(jnp.arange(GRID_W), rows).astype(F32)
    inv = ROPE_THETA ** (-jnp.arange(0, ROPE_AXIS_DIM, 2, dtype=F32) / ROPE_AXIS_DIM)
    ang_r, ang_c = row[:, None] * inv, col[:, None] * inv
    cos = jnp.concatenate([jnp.cos(ang_r), jnp.cos(ang_r), jnp.cos(ang_c), jnp.cos(ang_c)], axis=1)
    sin = jnp.concatenate([-jnp.sin(ang_r), jnp.sin(ang_r), -jnp.sin(ang_c), jnp.sin(ang_c)], axis=1)
    cos = jnp.concatenate([jnp.ones((rc, HEAD_DIM), F32), cos], axis=0)
    sin = jnp.concatenate([jnp.zeros((rc, HEAD_DIM), F32), sin], axis=0)
    return cos, sin


MOD_NAMES = ("sh1", "sc1", "g1", "sh2", "sc2", "g2")


def _local_step(xin, target, mod, comm, sp, rc, alpha):
    def carrying(fn):
        def call(name, *args, **kw):
            rider = comm.rider(name)
            if rider is None:
                return fn(name, *args, **kw)
            res, r_outs = fn(name, *args, rider=rider, **kw)
            comm.deliver(name, r_outs)
            return res
        return call

    mm, rows, attn_fwd, attn_bwd = carrying(_mm), carrying(_rows), carrying(_attn_fwd), carrying(_attn_bwd)
    mm_fused = carrying(_mm_fused)
    r, d = xin.shape
    n_layers = mod.shape[0]
    tm_n, tm_w = 256, 128
    nbc_n, nbc_w = rc // tm_n, rc // tm_w
    cos, sin = _rope_tables(rc, r - rc)
    mp = mod.reshape(n_layers, 2, 6, 1, d)
    mods = [{nm: mp[i, :, j] for j, nm in enumerate(MOD_NAMES)} for i in range(n_layers)]
    f_ln_mod, f_ln_last = _make_f_ln(alpha, True), _make_f_ln(alpha, False)

    def whole(arr, roff=0):
        return (arr, 0, arr.shape[1], roff)

    (hb,) = rows("mod_in", _f_mod, r, tm_n, nbc_n, [whole(xin)], [mods[0]["sc1"], mods[0]["sh1"]],
                 [(r, d, BF16, 0)], [])
    saved = []
    x = xin
    for i in range(n_layers):
        last = i == n_layers - 1
        w, s, m = functools.partial(comm.weight, i), sp[i], mods[i]
        sv = {"x": x, "hb": hb}
        p = mm(f"l{i}_in", hb, w("in_t"), "nt", F32)
        q, k, v = rows(f"l{i}_prep", _f_prep, r, tm_n, nbc_n,
                       [(p, 0, OFF_POOL, 0), whole(cos), whole(sin)], [_typed(s["q_norm_g"]), _typed(s["k_norm_g"])],
                       [(r, Q_W, BF16, 0), (r, KV_W, BF16, 0), (r, KV_W, BF16, 0)], [])
        ys = [attn_fwd(f"l{i}_attn", q, k, v, rc, not last),
              _pool_fwd(f"l{i}_pool", p, s["pool_w"], s["pool_scale"], rc),
              _sgu_fwd(f"l{i}_sgu", p, s["sgu_ln_g"], s["sgu_ln_b"], s["sgu_w"], s["sgu_b"]),
              _conv_fwd(f"l{i}_conv", p, s["conv_w"], rc)]
        ts = [mm(f"l{i}_br{kk}", ys[kk], w(f"br{kk}"), "nt", F32) for kk in range(N_BRANCH)]
        gpre = mm(f"l{i}_gate", hb, w("gate_t"), "nt", F32)
        (mg,) = rows(f"l{i}_merge", _f_gate, r, tm_w, nbc_w, [whole(gpre)] + [whole(t) for t in ts],
                     [_typed(s["b_gate"])], [(r, d, BF16, 0)], [])
        o = mm(f"l{i}_o", mg, w("o"), "nn", F32)
        x1, h2b = rows(f"l{i}_ln1", f_ln_mod, r, tm_n, nbc_n, [whole(x), whole(o)],
                       [m["g1"], _typed(s["ln1_g"]), _typed(s["ln1_b"]), m["sc2"], m["sh2"]],
                       [(r, d, F32, 0), (r, d, BF16, 0)], [])
        af, bf, f = mm_fused(f"l{i}_ffgu", h2b, [w("ffg_t"), w("ffu_t")],
                             lambda prods, _: (prods[0], prods[1], _f_swiglu(prods[0], prods[1])), [], [F32, F32, BF16])
        o2 = mm(f"l{i}_ffd", f, w("ffd"), "nn", F32)
        if last:
            (x2,) = rows(f"l{i}_ln2", f_ln_last, r, tm_n, nbc_n, [whole(x1), whole(o2)],
                         [m["g2"], _typed(s["ln2_g"]), _typed(s["ln2_b"])], [(r, d, F32, 0)], [])
            hb = None
        else:
            nx = mods[i + 1]
            x2, hb = rows(f"l{i}_ln2", f_ln_mod, r, tm_n, nbc_n, [whole(x1), whole(o2)],
                          [m["g2"], _typed(s["ln2_g"]), _typed(s["ln2_b"]), nx["sc1"], nx["sh1"]],
                          [(r, d, F32, 0), (r, d, BF16, 0)], [])
        sv.update(p=p, gpre=gpre, q=q, k=k, v=v, ys=ys, ts=ts, mg=mg, o=o, x1=x1, h2b=h2b, af=af, bf=bf, f=f, o2=o2)
        saved.append(sv)
        x = x2

    lat = jnp.concatenate([jnp.zeros((1, 1, 128), F32), jnp.ones((1, 1, 128), F32)], axis=0)

    def f_loss(xb, tb, msk):
        diff = (xb - tb) * msk[:, 0:1]
        part = jnp.sum(jnp.mean(jnp.square(diff), axis=-1, keepdims=True), axis=0, keepdims=True)
        return diff * (1.0 / d), jnp.broadcast_to(part, (1, 128))

    dx_direct, loss_acc = rows("loss", f_loss, r, tm_n, nbc_n, [whole(x), whole(target, nbc_n)], [lat],
                               [(r, d, F32, 0)], [128])
    loss = 0.5 * loss_acc[1, 0, 0]

    dmods = [dict() for _ in range(n_layers)]
    dsp = [dict() for _ in range(n_layers)]
    dh = None
    for i in reversed(range(n_layers)):
        last = i == n_layers - 1
        w, s, m, sv = functools.partial(comm.weight, i), sp[i], mods[i], saved[i]
        dm, dw, ds = dmods[i], {}, dsp[i]
        ln2 = [m["g2"], _typed(s["ln2_g"]), _typed(s["ln2_b"])]
        if last:
            res = rows(f"l{i}_ln2_bwd", _vjp_fn(f_ln_last, 2, 1), r, tm_n, nbc_n,
                       [whole(sv["x1"]), whole(sv["o2"]), whole(dx_direct)], ln2,
                       [(r, d, F32, 0), (r, d, BF16, 0)], [d, d, d])
            dx1, do2, dm["g2"], dlg, dlb = res
        else:
            nx = mods[i + 1]
            res = rows(f"l{i}_ln2_bwd", _vjp_fn(f_ln_mod, 2, 2), r, tm_n, nbc_n,
                       [whole(sv["x1"]), whole(sv["o2"]), whole(dx_direct), whole(dh)],
                       ln2 + [nx["sc1"], nx["sh1"]],
                       [(r, d, F32, 0), (r, d, BF16, 0)], [d, d, d, d, d])
            dx1, do2, dm["g2"], dlg, dlb, dmods[i + 1]["sc1"], dmods[i + 1]["sh1"] = res
        ds["ln2_g"], ds["ln2_b"] = dlg, dlb
        dab, dbb = mm_fused(f"l{i}_dF", do2, [w("ffd")],
                            lambda prods, tiles: jax.vjp(_f_swiglu, *tiles)[1](prods[0]), [sv["af"], sv["bf"]],
                            [BF16, BF16])
        comm.grads(i, {"ffd": mm(f"l{i}_dWffd", sv["f"], do2, "tn", BF16)})
        comm.grads(i, {"ffg_t": mm(f"l{i}_dWffg", dab, sv["h2b"], "tn", BF16)})
        comm.grads(i, {"ffu_t": mm(f"l{i}_dWffu", dbb, sv["h2b"], "tn", BF16)})
        dh2 = mm(f"l{i}_dh2a", dab, w("ffg_t"), "nn", F32)
        dh2 = mm(f"l{i}_dh2b", dbb, w("ffu_t"), "nn", F32, acc=dh2)
        res = rows(f"l{i}_ln1_bwd", _vjp_fn(f_ln_mod, 2, 2), r, tm_n, nbc_n,
                   [whole(sv["x"]), whole(sv["o"]), whole(dx1), whole(dh2)],
                   [m["g1"], _typed(s["ln1_g"]), _typed(s["ln1_b"]), m["sc2"], m["sh2"]],
                   [(r, d, F32, 0), (r, d, BF16, 0)], [d, d, d, d, d])
        dx_direct, do, dm["g1"], ds["ln1_g"], ds["ln1_b"], dm["sc2"], dm["sh2"] = res
        dmg = mm(f"l{i}_dMg", do, w("o"), "nt", F32)
        comm.grads(i, {"o": mm(f"l{i}_dWo", sv["mg"], do, "tn", BF16)})
        res = rows(f"l{i}_merge_bwd", _vjp_fn(_f_gate, 5, 1), r, tm_w, nbc_w,
                   [whole(sv["gpre"])] + [whole(t) for t in sv["ts"]] + [whole(dmg)], [_typed(s["b_gate"])],
                   [(r, N_BRANCH * d, BF16, 0)] + [(r, d, BF16, 0)] * N_BRANCH, [N_BRANCH * d])
        dgb, dts, ds["b_gate"] = res[0], res[1:1 + N_BRANCH], res[1 + N_BRANCH]
        comm.grads(i, {"gate_t": mm(f"l{i}_dWgate", dgb, sv["hb"], "tn", BF16)})
        dys = [mm(f"l{i}_dY{kk}", dts[kk], w(f"br{kk}"), "nn", F32) for kk in range(N_BRANCH)]
        for kk in range(N_BRANCH):
            comm.grads(i, {f"br{kk}": mm(f"l{i}_dWbr{kk}", dts[kk], sv["ys"][kk], "tn", BF16)})
        dq, dk, dv = attn_bwd(f"l{i}_attn_bwd", sv["q"], sv["k"], sv["v"], dys[0], rc, not last)
        res = rows(f"l{i}_prep_bwd", _vjp_fn(_f_prep, 3, 3, keep=(0, 3, 4)), r, tm_n, nbc_n,
                   [(sv["p"], 0, OFF_POOL, 0), whole(cos), whole(sin), whole(dq), whole(dk), whole(dv)],
                   [_typed(s["q_norm_g"]), _typed(s["k_norm_g"])],
                   [(r, OFF_POOL, BF16, 0)], [HEAD_DIM, HEAD_DIM])
        dp_qkv, ds["q_norm_g"], ds["k_norm_g"] = res
        dp_pool, ds["pool_w"], ds["pool_scale"] = _pool_bwd(f"l{i}_pool_bwd", sv["p"], s["pool_w"], s["pool_scale"],
                                                            dys[1], rc)
        dp_sgu, ds["sgu_ln_g"], ds["sgu_ln_b"], ds["sgu_w"], ds["sgu_b"] = _sgu_bwd(
            f"l{i}_sgu_bwd", sv["p"], s["sgu_ln_g"], s["sgu_ln_b"], s["sgu_w"], s["sgu_b"], dys[2])
        dp_cb, dp_cc, dp_cx, dcw = _conv_bwd(f"l{i}_conv_bwd", sv["p"], s["conv_w"], dys[3], rc)
        ds["conv_w"] = dcw[0:3]
        dpb = jnp.concatenate([dp_qkv, dp_pool, dp_sgu, dp_cb, dp_cc, dp_cx], axis=1)
        comm.grads(i, {"in_t": mm(f"l{i}_dWin", dpb, sv["hb"], "tn", BF16)})
        dh = mm(f"l{i}_dhb_a", dpb, w("in_t"), "nn", F32)
        dh = mm(f"l{i}_dhb_b", dgb, w("gate_t"), "nn", F32, acc=dh)

    def f_mod_bwd(xb, ddir, dhb, sc, sh):
        _, vjp = jax.vjp(_f_mod, xb, sc, sh)
        dxb, dsc, dsh = vjp(dhb)
        return dxb + ddir, dsc, dsh

    grad_x, dmods[0]["sc1"], dmods[0]["sh1"] = rows(
        "mod_in_bwd", f_mod_bwd, r, tm_n, nbc_n, [whole(xin), whole(dx_direct), whole(dh)],
        [mods[0]["sc1"], mods[0]["sh1"]], [(r - rc, d, F32, nbc_n)], [d, d])
    dmod = jnp.stack([jnp.concatenate([dmods[i][nm][:, 0, :] for nm in MOD_NAMES], axis=-1)
                      for i in range(n_layers)])
    for ds in dsp:
        for nm in ("ln1_g", "ln1_b", "ln2_g", "ln2_b", "b_gate", "q_norm_g", "k_norm_g"):
            ds[nm] = ds[nm][0, 0] + ds[nm][1, 0]
        ds["pool_scale"] = ds["pool_scale"].reshape(-1)
        ds["sgu_ln_g"] = ds["sgu_ln_g"].reshape(-1)
        ds["sgu_ln_b"] = ds["sgu_ln_b"].reshape(-1)
        ds["sgu_b"] = ds["sgu_b"].reshape(N_GROUPS, SGU_CHUNK)
    return loss, grad_x, dmod, dsp


def _exchange(name, srcs, scatter):
    n_items = len(srcs)
    out_shapes = [jax.ShapeDtypeStruct(s.shape if scatter else (N_DEV,) + s.shape, s.dtype) for s in srcs]

    def body(*refs):
        src, out = refs[:n_items], refs[n_items:2 * n_items]
        send_sems, recv_sems, loc_sems = refs[2 * n_items:]
        mx, my, mc = [lax.axis_index(a) for a in MESH_AXES]
        me = 4 * mx + 2 * my + mc
        peers = []
        for kk in range(1, N_DEV):
            px = 1 - mx if kk & 4 else mx
            py = 1 - my if kk & 2 else my
            pc = 1 - mc if kk & 1 else mc
            peers.append(((px, py, pc), 4 * px + 2 * py + pc))
        local, sent = [], []
        for a in range(n_items):
            loc = pltpu.make_async_copy(src[a].at[me] if scatter else src[a], out[a].at[me], loc_sems.at[a])
            loc.start()
            local.append(loc)
            for j, (peer, peer_l) in enumerate(peers):
                cp = pltpu.make_async_remote_copy(
                    src_ref=src[a].at[peer_l] if scatter else src[a], dst_ref=out[a].at[me],
                    send_sem=send_sems.at[a * (N_DEV - 1) + j], recv_sem=recv_sems.at[a * (N_DEV - 1) + j],
                    device_id=peer, device_id_type=pl.DeviceIdType.MESH)
                cp.start()
                sent.append(cp)
        for a in range(n_items):
            for j, (peer, peer_l) in enumerate(peers):
                pltpu.make_async_remote_copy(
                    src_ref=src[a].at[peer_l] if scatter else src[a], dst_ref=out[a].at[peer_l],
                    send_sem=send_sems.at[a * (N_DEV - 1) + j], recv_sem=recv_sems.at[a * (N_DEV - 1) + j],
                    device_id=peer, device_id_type=pl.DeviceIdType.MESH).wait_recv()
        for cp in sent:
            cp.wait_send()
        for loc in local:
            loc.wait()

    any_spec = pl.BlockSpec(memory_space=pl.ANY)
    res = pl.pallas_call(
        body, name=name,
        in_specs=[any_spec] * n_items, out_specs=[any_spec] * n_items, out_shape=out_shapes,
        scratch_shapes=[pltpu.SemaphoreType.DMA((n_items * (N_DEV - 1),)),
                        pltpu.SemaphoreType.DMA((n_items * (N_DEV - 1),)),
                        pltpu.SemaphoreType.DMA((n_items,))],
    )(*srcs)
    return list(res)


def _mesh_place():
    mx, my, mc = [lax.axis_index(a) for a in MESH_AXES]
    chips = [(1 - mx, my), (mx, 1 - my), (1 - mx, 1 - my)]

    def lid(px, py, pc):
        return 4 * px + 2 * py + pc

    return (mx, my, mc), (mx, my, 1 - mc), chips, lid


def _rcopy(src, dst, send_sems, recv_sems, k, to):
    return pltpu.make_async_remote_copy(src_ref=src, dst_ref=dst, send_sem=send_sems.at[k], recv_sem=recv_sems.at[k],
                                        device_id=to, device_id_type=pl.DeviceIdType.MESH)


def _sem_scratch(*sizes):
    return [pltpu.SemaphoreType.DMA((s,)) for s in sizes]


def _gather_rider(src, rows, buf=None):
    r0, r1 = rows
    win = pl.ds(r0, r1 - r0)

    def start(ins, outs, sems):
        send, recv, loc = sems
        (mx, my, mc), sib, chips, lid = _mesh_place()
        mine, dst = ins[0].at[win], outs[0].at[lid(mx, my, mc), win]
        pltpu.make_async_copy(mine, dst, loc.at[0]).start()
        _rcopy(mine, dst, send, recv, 0, sib).start()
        for j, chip in enumerate(chips):
            _rcopy(mine, dst, send, recv, 1 + j, (*chip, mc)).start()

    def finish(ins, outs, sems):
        send, recv, loc = sems
        (mx, my, mc), sib, chips, lid = _mesh_place()
        mine, dst = ins[0].at[win], outs[0].at[lid(mx, my, mc), win]
        for j, chip in enumerate(chips):
            blk = outs[0].at[lid(*chip, mc), win]
            _rcopy(mine, blk, send, recv, 1 + j, (*chip, mc)).wait_recv()
            _rcopy(blk, blk, send, recv, 4 + j, sib).start()
        _rcopy(mine, outs[0].at[lid(*sib), win], send, recv, 0, sib).wait_recv()
        for j, chip in enumerate(chips):
            _rcopy(mine, outs[0].at[lid(*chip, 1 - mc), win], send, recv, 4 + j, sib).wait_recv()
        for t in range(7):
            _rcopy(mine, dst, send, recv, t, sib).wait_send()
        pltpu.make_async_copy(mine, dst, loc.at[0]).wait()

    out_shape = jax.ShapeDtypeStruct((N_DEV,) + src.shape, src.dtype)
    if buf is None:
        return _Rider([src], [out_shape], _sem_scratch(7, 7, 1), start, finish)
    return _Rider([src, buf], [out_shape], _sem_scratch(7, 7, 1), start, finish, aliases={1: 0})


def _sibling_rider(part):
    def start(ins, outs, sems):
        send, recv = sems
        (mx, my, mc), sib, chips, lid = _mesh_place()
        for t, slab in enumerate([lid(*sib)] + [lid(*chip, 1 - mc) for chip in chips]):
            _rcopy(ins[0].at[slab], outs[0].at[t], send, recv, t, sib).start()

    def finish(ins, outs, sems):
        send, recv = sems
        _, sib, _, _ = _mesh_place()
        for t in range(4):
            cp = _rcopy(ins[0].at[0], outs[0].at[t], send, recv, t, sib)
            cp.wait_recv()
            cp.wait_send()

    return _Rider([part], [jax.ShapeDtypeStruct((4,) + part.shape[1:], part.dtype)], _sem_scratch(4, 4), start, finish)


def _chips_rider(pair, rows, buf=None):
    r0, r1 = rows
    win = pl.ds(r0, r1 - r0)

    def start(ins, outs, sems):
        send, recv = sems
        (mx, my, mc), sib, chips, lid = _mesh_place()
        for j, chip in enumerate(chips):
            _rcopy(ins[0].at[j, win], outs[0].at[j, win], send, recv, j, (*chip, mc)).start()

    def finish(ins, outs, sems):
        send, recv = sems
        (mx, my, mc), sib, chips, lid = _mesh_place()
        for j, chip in enumerate(chips):
            cp = _rcopy(ins[0].at[j, win], outs[0].at[j, win], send, recv, j, (*chip, mc))
            cp.wait_recv()
            cp.wait_send()

    out_shape = jax.ShapeDtypeStruct(pair.shape, pair.dtype)
    if buf is None:
        return _Rider([pair], [out_shape], _sem_scratch(3, 3), start, finish)
    return _Rider([pair, buf], [out_shape], _sem_scratch(3, 3), start, finish, aliases={1: 0})


def _run_rider(name, rider):
    n_in, n_out = len(rider.inputs), len(rider.out_shapes)

    def body(*refs):
        ins, outs, sems = refs[:n_in], refs[n_in:n_in + n_out], refs[n_in + n_out:]
        rider.start(ins, outs, sems)
        rider.finish(ins, outs, sems)

    any_spec = pl.BlockSpec(memory_space=pl.ANY)
    res = pl.pallas_call(body, name=name, in_specs=[any_spec] * n_in, out_specs=[any_spec] * n_out,
                         out_shape=rider.out_shapes, scratch_shapes=rider.scratch,
                         input_output_aliases=rider.aliases)(*rider.inputs)
    return list(res)


def _slab_ids():
    (mx, my, mc), _, chips, lid = _mesh_place()
    return jnp.stack([lid(*chip, mc) for chip in chips] + [lid(mx, my, mc)]).astype(jnp.int32)


def _pair_sum(name, part, rsib, ids):
    _, n, k = part.shape
    tr = _row_tile(n, 512, 16)

    def body(ids_ref, p_ref, r_ref, o_ref):
        o_ref[...] = (p_ref[...].astype(F32) + r_ref[...].astype(F32)).astype(o_ref.dtype)

    grid_spec = pltpu.PrefetchScalarGridSpec(
        num_scalar_prefetch=1, grid=(3, n // tr),
        in_specs=[pl.BlockSpec((None, tr, k), lambda j, i, ids: (ids[j], i, 0)),
                  pl.BlockSpec((None, tr, k), lambda j, i, ids: (1 + j, i, 0))],
        out_specs=pl.BlockSpec((None, tr, k), lambda j, i, ids: (j, i, 0)))
    return pl.pallas_call(body, name=name, grid_spec=grid_spec, out_shape=jax.ShapeDtypeStruct((3, n, k), part.dtype),
                          compiler_params=_cparams("parallel", "parallel"))(ids, part, rsib)


def _sum5(name, part, rsib, rici, ids, layer, stacked, rider=None):
    _, n, k = part.shape
    tr = _row_tile(n, 512, 16)
    first = isinstance(stacked, int)

    def body(ids_ref, p_ref, r_ref, c_ref, *rest):
        acc = p_ref[...].astype(F32) + r_ref[...].astype(F32)
        for j in range(3):
            acc = acc + c_ref[j].astype(F32)
        rest[-1][...] = acc

    in_specs = [pl.BlockSpec((None, tr, k), lambda i, ids: (ids[3], i, 0)),
                pl.BlockSpec((None, tr, k), lambda i, ids: (0, i, 0)),
                pl.BlockSpec((3, tr, k), lambda i, ids: (0, i, 0))] + ([] if first else [pl.BlockSpec(memory_space=pl.ANY)])
    n_layers = stacked if first else stacked.shape[0]
    outs, r_outs = _pcall(name, body, (n // tr,), in_specs, [pl.BlockSpec((None, tr, k), lambda i, ids: (layer, i, 0))],
                          [jax.ShapeDtypeStruct((n_layers, n, k), F32)],
                          (part, rsib, rici) + (() if first else (stacked,)), ("parallel",), rider=rider, prefetch=ids,
                          aliases={} if first else {3: 0})
    return outs[0] if rider is None else (outs[0], r_outs)


W_KEYS = ("in_t", "br0", "br1", "br2", "br3", "gate_t", "o", "ffg_t", "ffu_t", "ffd")
SUMS_TRANSPOSED_LATER = ("gate_t", "br0", "br1", "br2", "br3")


CARRIER_US = {"mod_in": 12, "in": 55, "gate": 95, "prep": 19, "attn": 112, "br0": 15, "merge": 50, "o": 25, "ln1": 27,
              "ffgu": 135, "ffd": 73, "ln2": 27, "loss": 20, "ln2_bwd": 44, "dF": 80,
              "dWffd": 64, "dh2a": 75, "dh2b": 75, "dWffg": 64, "dWffu": 64, "ln1_bwd": 44,
              "dMg": 25, "dWo": 25, "merge_bwd": 80, "dY0": 14, "dWbr0": 15, "attn_bwd": 195, "prep_bwd": 28,
              "dWin": 54, "dWgate": 95, "dhb_a": 64, "dhb_b": 115}
ICI_US_PER_MIB = 45.0
D2D_US_PER_MIB = 6.8
MIN_CHUNK_US = 10.0
CARRIER_FILL = 1.15


class _Comm:
    def __init__(self, wsrc):
        self.wsrc = wsrc
        self.n_layers = len(wsrc)
        self.queue = []
        self.riding = {}
        self.buf, self.left = {}, {}
        self.part, self.rsib, self.pair = {}, {}, {}
        self.ids = _slab_ids()
        for i in range(self.n_layers):
            for k in W_KEYS:
                self._push_chunks("gather", ("w", i, k), wsrc[i][k].shape, wsrc[i][k].dtype)

    def _push_chunks(self, kind, item, shape, dtype):
        n, k = shape[-2], shape[-1]
        us = n * k * jnp.dtype(dtype).itemsize / 2 ** 20 * ICI_US_PER_MIB
        pieces = max(1, int(us // MIN_CHUNK_US))
        while n % (16 * pieces):
            pieces -= 1
        step = n // pieces
        self.left[item] = pieces
        for c in range(pieces):
            self.queue.append(dict(kind=kind, item=item, rows=(c * step, (c + 1) * step), us=us / pieces))

    @staticmethod
    def _merge(units, u):
        v = units[-1] if units else None
        if not (v and v["item"] == u["item"] and v["kind"] == u["kind"] and u["rows"] and v["rows"][1] == u["rows"][0]):
            return False
        v.update(rows=(v["rows"][0], u["rows"][1]), us=v["us"] + u["us"], count=v.get("count", 1) + u.get("count", 1))
        return True

    def _unit_rider(self, u):
        item = u["item"]
        if u["kind"] == "gather":
            src = self.wsrc[item[1]][item[2]] if item[0] == "w" else self.part[item]
            return _gather_rider(src, u["rows"], self.buf.get(item))
        if u["kind"] == "sibling":
            return _sibling_rider(self.part[item])
        return _chips_rider(self.pair[item], u["rows"], self.buf.get(item))

    def _done(self, u, out):
        item = u["item"]
        if u["kind"] == "sibling":
            self.rsib[item] = out
            self.pair[item] = _pair_sum(f"pair_l{item[1]}_{item[2]}", self.part[item], out, self.ids)
            self._push_chunks("chips", item, self.pair[item].shape, self.pair[item].dtype)
            return
        self.buf[item] = out
        self.left[item] -= u.get("count", 1)

    def _send(self, name, units, call):
        outs = call(_compose([self._unit_rider(u) for u in units]))
        for u, o in zip(units, outs):
            self._done(u, o)

    def rider(self, name, budget_us=None):
        budget = CARRIER_US.get(name.split("_", 1)[1] if name[0] == "l" and name[1].isdigit() else name, 0) \
            if budget_us is None else budget_us
        units, used = [], 0.0
        while self.queue and used + self.queue[0]["us"] <= CARRIER_FILL * budget:
            u = self.queue[0]
            if not self._merge(units, u):
                if any(v["item"] == u["item"] for v in units):
                    break
                units.append(dict(u))
            used += u["us"]
            del self.queue[0]
        if not units:
            return None
        self.riding[name] = units
        return _compose([self._unit_rider(u) for u in units])

    def deliver(self, name, outs):
        for u, o in zip(self.riding.pop(name), outs):
            self._done(u, o)

    def _flush(self, item, kinds):
        hits = [p for p, u in enumerate(self.queue) if u["item"] == item and u["kind"] in kinds]
        if not hits:
            return
        prefix = self.queue[:hits[-1] + 1]
        del self.queue[:hits[-1] + 1]
        units = []
        for u in prefix:
            if not self._merge(units, u):
                units.append(dict(u))
        tag = "_".join(str(t) for t in item) + "_" + kinds[0]
        batches = [[]]
        for u in units:
            if any(v["item"] == u["item"] for v in batches[-1]):
                batches.append([])
            batches[-1].append(u)
        for b, batch in enumerate(batches):
            self._send(None, batch, functools.partial(_run_rider, f"alone_{tag}_{b}"))

    def begin(self):
        self._flush(("w", 0, "in_t"), ("gather",))

    def weight(self, i, k):
        item = ("w", i, k)
        self._flush(item, ("gather",))
        o = self.buf[item]
        return o.reshape(-1, o.shape[-1])

    def grads(self, i, group):
        for k, g in group.items():
            item = ("g", i, k)
            self.part[item] = g.reshape(N_DEV, g.shape[0] // N_DEV, g.shape[1])
            us = g.size // N_DEV * g.dtype.itemsize / 2 ** 20 * D2D_US_PER_MIB
            self.queue.append(dict(kind="sibling", item=item, rows=None, us=us))

    def total(self, k):
        out = self.n_layers
        for i in range(self.n_layers):
            item = ("g", i, k)
            self._flush(item, ("sibling",))
            self._flush(item, ("chips",))
            name = f"sum_l{i}_{k}"
            rider = self.rider(name, budget_us=self.part[item][0].size / 1.06e5) if k in SUMS_TRANSPOSED_LATER else None
            out = _sum5(name, self.part[item], self.rsib[item], self.buf[item], self.ids, i, out, rider=rider)
            if rider is not None:
                out, r_outs = out
                self.deliver(name, r_outs)
        return out

    def gather_small(self, name, arr):
        item = ("s", name)
        self.part[item] = arr
        waiting, self.queue = self.queue, []
        self._push_chunks("gather", item, arr.shape, arr.dtype)
        self.queue += waiting

    def gathered(self, name):
        item = ("s", name)
        self._flush(item, ("gather",))
        return self.buf[item]


def _row_tile(n, pref, mult):
    best = None
    t = mult
    while t <= min(n, pref):
        if n % t == 0:
            best = t
        t += mult
    return best if best is not None else n


def _sum8(name, slabs):
    _, n, k = slabs.shape
    tr = _row_tile(n, 128, 16)

    def body(s_ref, o_ref):
        acc = s_ref[0].astype(F32)
        for j in range(1, N_DEV):
            acc = acc + s_ref[j].astype(F32)
        o_ref[...] = acc

    return pl.pallas_call(
        body, name=name, grid=(n // tr,),
        in_specs=[pl.BlockSpec((N_DEV, tr, k), lambda i: (0, i, 0))],
        out_specs=pl.BlockSpec((tr, k), lambda i: (i, 0)),
        out_shape=jax.ShapeDtypeStruct((n, k), F32),
        compiler_params=_cparams("parallel"),
    )(slabs)


def _adamw(name, w, g, m, v, rider=None):
    n, k = w.shape[-2:]
    tr = _row_tile(n, 256, 8)

    def body(w_ref, g_ref, m_ref, v_ref, d_ref, m2_ref, v2_ref):
        gv = g_ref[...]
        m2 = ADAM_B1 * m_ref[...] + (1.0 - ADAM_B1) * gv
        v2 = ADAM_B2 * v_ref[...] + (1.0 - ADAM_B2) * jnp.square(gv)
        m_hat = m2 / (1.0 - ADAM_B1 ** ADAM_STEP)
        v_hat = v2 / (1.0 - ADAM_B2 ** ADAM_STEP)
        d_ref[...] = -ADAM_LR * (m_hat / (jnp.sqrt(v_hat) + ADAM_EPS) + ADAM_WD * w_ref[...])
        m2_ref[...] = m2
        v2_ref[...] = v2

    if w.ndim == 2:
        grid, spec = (n // tr,), pl.BlockSpec((tr, k), lambda i: (i, 0))
    else:
        grid, spec = (w.shape[0], n // tr), pl.BlockSpec((None, tr, k), lambda l, i: (l, i, 0))
    outs, r_outs = _pcall(name, body, grid, [spec] * 4, [spec] * 3, [jax.ShapeDtypeStruct(w.shape, F32)] * 3,
                          (w, g, m, v), ("parallel",) * len(grid), rider=rider)
    return outs if rider is None else (outs, r_outs)


def _pack(arrs):
    flat = jnp.concatenate([a.reshape(-1).astype(F32) for a in arrs])
    pad = (-flat.shape[0]) % 2048
    if pad:
        flat = jnp.concatenate([flat, jnp.zeros((pad,), F32)])
    return flat.reshape(-1, 128)


def _unpack(packed, shapes):
    flat = packed.reshape(-1)
    out, off = [], 0
    for shp in shapes:
        size = math.prod(shp)
        out.append(flat[off:off + size].reshape(shp))
        off += size
    return out


WEIGHT_NAMES = ("c_ctx", "w_ada", "b_ada", "w_in", "q_norm_g", "k_norm_g", "pool_w", "pool_scale", "sgu_ln_g",
                "sgu_ln_b", "sgu_w", "sgu_b", "conv_w", "w_br_attn", "w_br_pool", "w_br_sgu", "w_br_conv", "w_gate",
                "b_gate", "w_o", "ln1_g", "ln1_b", "w_ff_gate", "w_ff_up", "w_ff_down", "ln2_g", "ln2_b")
COL_SHARDED = {"w_in": "in_t", "w_gate": "gate_t", "w_ff_gate": "ffg_t", "w_ff_up": "ffu_t",
               "w_br_attn": "br0", "w_br_pool": "br1", "w_br_sgu": "br2", "w_br_conv": "br3"}
ROW_SHARDED = {"w_o": "o", "w_ff_down": "ffd"}
LAYER_SMALL = ("q_norm_g", "k_norm_g", "pool_w", "pool_scale", "sgu_ln_g", "sgu_ln_b", "sgu_w", "sgu_b", "b_gate",
               "ln1_g", "ln1_b", "ln2_g", "ln2_b")
SMALL_ORDER = ("c_ctx", "b_ada") + LAYER_SMALL + ("conv_w",)


def _train_step(a):
    n_layers, d = a["w_in"].shape[0], a["x"].shape[-1]
    rc = a["ctx"].shape[1]
    alpha = (2 * n_layers) ** 0.25
    mx, my, mc = [lax.axis_index(ax) for ax in MESH_AXES]
    me = 4 * mx + 2 * my + mc
    ada_w = a["w_ada"].shape[-1]
    cw_loc = a["conv_w"].shape[-1]

    n_c, n_cw = d, n_layers * 3 * cw_loc
    got = _exchange("gather_cond", [_pack([a["c"], a["conv_w"]])], False)[0].reshape(N_DEV, -1)
    c_all = got[:, :n_c]
    conv_w = got[:, n_c:n_c + n_cw].reshape(N_DEV, n_layers, 3, cw_loc).transpose(1, 2, 0, 3).reshape(n_layers, 3, -1)
    cond = jnp.concatenate([c_all, a["c_ctx"][None], jnp.zeros((16 - N_DEV - 1, d), F32)], axis=0)
    sil, sil_vjp = jax.vjp(jax.nn.silu, cond)
    sil = sil.astype(BF16)

    mod_cols = jnp.concatenate([_mm(f"ada{i}", sil, a["w_ada"][i], "nn", F32) for i in range(n_layers)], axis=0)
    got = _exchange("gather_mod", [mod_cols], False)[0]
    mod_all = got.reshape(N_DEV, n_layers, 16, ada_w).transpose(1, 2, 0, 3).reshape(n_layers, 16, -1)
    mod_all = mod_all + a["b_ada"][:, None, :]
    mod = jnp.stack([mod_all[:, N_DEV], lax.dynamic_index_in_dim(mod_all, me, axis=1, keepdims=False)], axis=1)

    comm = _Comm([{**{key: jnp.swapaxes(a[nm], 1, 2)[i].astype(BF16) for nm, key in COL_SHARDED.items()},
                   **{key: a[nm][i].astype(BF16) for nm, key in ROW_SHARDED.items()}} for i in range(n_layers)])
    comm.begin()
    sp = [{nm: a[nm][i] for nm in LAYER_SMALL} for i in range(n_layers)]
    for i in range(n_layers):
        sp[i]["conv_w"] = conv_w[i]

    xin = jnp.concatenate([a["ctx"][0], a["x"][0]], axis=0)
    loss_l, grad_x, dmod, dsp = _local_step(xin, a["loss_target"][0], mod, comm, sp, rc, alpha)
    loss = lax.psum(loss_l, MESH_AXES)
    grads = {}

    def transposed_home(nm):
        return nm in COL_SHARDED and a[nm].shape[-1] % 128 != 0

    small_parts = [dmod[:, 0], dmod[:, 1]]
    small_shapes = [dmod[:, 0].shape, dmod[:, 1].shape]
    for nm in LAYER_SMALL + ("conv_w",):
        part = jnp.stack([dsp[i][nm] for i in range(n_layers)])
        small_parts.append(part)
        small_shapes.append(part.shape)
    comm.gather_small("lat", _pack([dmod[:, 1]]))
    comm.gather_small("small", _pack(small_parts))

    delta, new_m, new_v = {}, {}, {}

    def adamw(nm):
        name = f"adamw_{nm}"
        there = transposed_home(nm)
        view = (lambda t: jnp.swapaxes(t, 1, 2)) if there else (lambda t: t)
        if nm in COL_SHARDED:
            g = comm.total(COL_SHARDED[nm])
            grads[nm] = jnp.swapaxes(g, 1, 2)
            g = g if there else grads[nm]
        elif nm in ROW_SHARDED:
            g = grads[nm] = comm.total(ROW_SHARDED[nm])
        else:
            g = grads[nm]
        res = _adamw(name, view(a[nm]), g, view(a["m_" + nm]), view(a["v_" + nm]))
        delta[nm], new_m[nm], new_v[nm] = [view(t) for t in res]

    for nm in ("w_ff_down", "w_ff_gate", "w_ff_up", "w_o", "w_br_attn", "w_br_pool", "w_br_sgu", "w_br_conv"):
        adamw(nm)

    got_lat, got_small = comm.gathered("lat"), comm.gathered("small")
    tot = _unpack(_sum8("sum_small", got_small), small_shapes)
    dmod_c, dmod_lat_sum = tot[0], tot[1]
    for nm, g in zip(LAYER_SMALL + ("conv_w",), tot[2:]):
        grads[nm] = g
    grads["conv_w"] = lax.dynamic_slice_in_dim(grads["conv_w"], me * cw_loc, cw_loc, axis=2)
    grads["b_ada"] = dmod_c + dmod_lat_sum
    dmod_lat_all = got_lat.reshape(N_DEV, -1)[:, :n_layers * 6 * d].reshape(N_DEV, n_layers, 6 * d)
    dm_rows = jnp.concatenate([dmod_lat_all.transpose(1, 0, 2), dmod_c[:, None, :],
                               jnp.zeros((n_layers, 16 - N_DEV - 1, 6 * d), F32)], axis=1)
    dm_cols = lax.dynamic_slice_in_dim(dm_rows, me * ada_w, ada_w, axis=2).astype(BF16)
    grads["w_ada"] = jnp.stack([_mm(f"dWada{i}", sil, dm_cols[i], "tn", F32) for i in range(n_layers)])
    dsil = None
    for i in range(n_layers):
        dsil = _mm(f"dsil{i}", dm_cols[i], a["w_ada"][i], "nt", F32, acc=dsil)
    got = _exchange("gather_dsil", [dsil], False)[0]
    dsil = _sum8("sum_dsil", got)
    grads["c_ctx"] = sil_vjp(dsil)[0][N_DEV]

    for nm in ("w_ada", "w_in", "w_gate"):
        adamw(nm)
    shapes = [a[nm].shape for nm in SMALL_ORDER]
    res = _adamw("adamw_small", _pack([a[nm] for nm in SMALL_ORDER]), _pack([grads[nm] for nm in SMALL_ORDER]),
                 _pack([a["m_" + nm] for nm in SMALL_ORDER]), _pack([a["v_" + nm] for nm in SMALL_ORDER]))
    for tree, packed in zip((delta, new_m, new_v), res):
        for nm, t in zip(SMALL_ORDER, _unpack(packed, shapes)):
            tree[nm] = t
    return (loss, grad_x[None], *[grads[nm] for nm in WEIGHT_NAMES], *[delta[nm] for nm in WEIGHT_NAMES],
            *[new_m[nm] for nm in WEIGHT_NAMES], *[new_v[nm] for nm in WEIGHT_NAMES])


def kernel(x, c, ctx, c_ctx, w_ada, b_ada, w_in, q_norm_g, k_norm_g, pool_w, pool_scale, sgu_ln_g, sgu_ln_b, sgu_w, sgu_b, conv_w, w_br_attn, w_br_pool, w_br_sgu, w_br_conv, w_gate, b_gate, w_o, ln1_g, ln1_b, w_ff_gate, w_ff_up, w_ff_down, ln2_g, ln2_b, loss_target, m_c_ctx, m_w_ada, m_b_ada, m_w_in, m_q_norm_g, m_k_norm_g, m_pool_w, m_pool_scale, m_sgu_ln_g, m_sgu_ln_b, m_sgu_w, m_sgu_b, m_conv_w, m_w_br_attn, m_w_br_pool, m_w_br_sgu, m_w_br_conv, m_w_gate, m_b_gate, m_w_o, m_ln1_g, m_ln1_b, m_w_ff_gate, m_w_ff_up, m_w_ff_down, m_ln2_g, m_ln2_b, v_c_ctx, v_w_ada, v_b_ada, v_w_in, v_q_norm_g, v_k_norm_g, v_pool_w, v_pool_scale, v_sgu_ln_g, v_sgu_ln_b, v_sgu_w, v_sgu_b, v_conv_w, v_w_br_attn, v_w_br_pool, v_w_br_sgu, v_w_br_conv, v_w_gate, v_b_gate, v_w_o, v_ln1_g, v_ln1_b, v_w_ff_gate, v_w_ff_up, v_w_ff_down, v_ln2_g, v_ln2_b):
    names = list(WEIGHT_NAMES)
    args = dict(zip(
        ["x", "c", "ctx"] + names + ["loss_target"] + ["m_" + n for n in names] + ["v_" + n for n in names],
        (x, c, ctx, c_ctx, w_ada, b_ada, w_in, q_norm_g, k_norm_g, pool_w, pool_scale, sgu_ln_g, sgu_ln_b, sgu_w, sgu_b, conv_w, w_br_attn, w_br_pool, w_br_sgu, w_br_conv, w_gate, b_gate, w_o, ln1_g, ln1_b, w_ff_gate, w_ff_up, w_ff_down, ln2_g, ln2_b, loss_target, m_c_ctx, m_w_ada, m_b_ada, m_w_in, m_q_norm_g, m_k_norm_g, m_pool_w, m_pool_scale, m_sgu_ln_g, m_sgu_ln_b, m_sgu_w, m_sgu_b, m_conv_w, m_w_br_attn, m_w_br_pool, m_w_br_sgu, m_w_br_conv, m_w_gate, m_b_gate, m_w_o, m_ln1_g, m_ln1_b, m_w_ff_gate, m_w_ff_up, m_w_ff_down, m_ln2_g, m_ln2_b, v_c_ctx, v_w_ada, v_b_ada, v_w_in, v_q_norm_g, v_k_norm_g, v_pool_w, v_pool_scale, v_sgu_ln_g, v_sgu_ln_b, v_sgu_w, v_sgu_b, v_conv_w, v_w_br_attn, v_w_br_pool, v_w_br_sgu, v_w_br_conv, v_w_gate, v_b_gate, v_w_o, v_ln1_g, v_ln1_b, v_w_ff_gate, v_w_ff_up, v_w_ff_down, v_ln2_g, v_ln2_b)))
    return _train_step(args)
```

```python
import functools
import math

import jax
import jax.numpy as jnp
from jax import lax
from jax.experimental import pallas as pl
from jax.experimental.pallas import tpu as pltpu

F32 = jnp.float32
BF16 = jnp.bfloat16

N_DEV = 8
MESH_AXES = ("x", "y", "c")
V7X_VMEM_LIMIT_BYTES = 56 * 1024 * 1024

GRID_W = 64
HEAD_DIM = 128
N_HEADS = 8
N_KV_HEADS = 2
KV_GROUP = N_HEADS // N_KV_HEADS
Q_W = N_HEADS * HEAD_DIM
KV_W = N_KV_HEADS * HEAD_DIM
ROPE_THETA = 10000.0
ROPE_AXIS_DIM = HEAD_DIM // 2
POOL_WINDOWS = (2, 4, 8, 16)
GC = 128
N_GROUPS = 4
BR_W = N_GROUPS * GC
SGU_CHUNK = 128
N_BRANCH = 4
LN_EPS = 1e-5
RMS_EPS = 1e-6
OFF_K = Q_W
OFF_V = OFF_K + KV_W
OFF_POOL = OFF_V + KV_W
OFF_U = OFF_POOL + BR_W
OFF_VG = OFF_U + BR_W
OFF_CB = OFF_VG + BR_W
OFF_CC = OFF_CB + BR_W
OFF_CX = OFF_CC + BR_W
IN_W = OFF_CX + BR_W
ATT_SCALE = HEAD_DIM ** -0.5

ADAM_LR = 0.001
ADAM_B1 = 0.9
ADAM_B2 = 0.999
ADAM_EPS = 1e-08
ADAM_WD = 0.01
ADAM_STEP = 10

_NT = (((1,), (1,)), ((), ()))
_NN = (((1,), (0,)), ((), ()))
_TN = (((0,), (0,)), ((), ()))
_DIMS = {"nt": _NT, "nn": _NN, "tn": _TN}


def _cparams(*sem):
    return pltpu.CompilerParams(dimension_semantics=sem, vmem_limit_bytes=V7X_VMEM_LIMIT_BYTES)


def _tile(dim, pref):
    best = None
    t = 128
    while t <= min(dim, pref):
        if dim % t == 0:
            best = t
        t += 128
    return best if best is not None else dim


def _dot(a, b, dims):
    return lax.dot_general(a.astype(BF16), b.astype(BF16), dims, preferred_element_type=F32)


class _Rider:
    def __init__(self, inputs, out_shapes, scratch, start, finish, aliases=None):
        self.inputs, self.out_shapes, self.scratch = list(inputs), list(out_shapes), list(scratch)
        self.start, self.finish = start, finish
        self.aliases = dict(aliases or {})


def _compose(riders):
    inputs, outs, scratch, aliases, spans = [], [], [], {}, []
    for rd in riders:
        i0, o0, s0 = len(inputs), len(outs), len(scratch)
        aliases.update({i0 + p: o0 + q for p, q in rd.aliases.items()})
        inputs += rd.inputs
        outs += rd.out_shapes
        scratch += rd.scratch
        spans.append((slice(i0, len(inputs)), slice(o0, len(outs)), slice(s0, len(scratch))))

    def start(ins, os, sems):
        for rd, (si, so, ss) in zip(riders, spans):
            rd.start(ins[si], os[so], sems[ss])

    def finish(ins, os, sems):
        for rd, (si, so, ss) in zip(riders, spans):
            rd.finish(ins[si], os[so], sems[ss])

    return _Rider(inputs, outs, scratch, start, finish, aliases)


def _pcall(name, body, grid, in_specs, out_specs, out_shape, args, sem, scratch=(), rider=None, prefetch=None,
           aliases=None):
    in_specs, out_specs, out_shape, scratch = list(in_specs), list(out_specs), list(out_shape), list(scratch)
    n_pre = 0 if prefetch is None else 1
    n_in, n_out, n_scr = len(in_specs), len(out_specs), len(scratch)
    r_in, r_out = (len(rider.inputs), len(rider.out_shapes)) if rider is not None else (0, 0)
    any_spec = pl.BlockSpec(memory_space=pl.ANY)
    io_aliases = {n_pre + p: q for p, q in (aliases or {}).items()}
    if rider is not None:
        io_aliases.update({n_pre + n_in + p: n_out + q for p, q in rider.aliases.items()})
        in_specs, out_specs = in_specs + [any_spec] * r_in, out_specs + [any_spec] * r_out
        out_shape, scratch = out_shape + rider.out_shapes, scratch + rider.scratch
        args, sem = (*args, *rider.inputs), ["arbitrary"] * len(grid)

    def wrapped(*refs):
        pre, refs = refs[:n_pre], refs[n_pre:]
        if rider is None:
            return body(*pre, *refs)
        ins, refs = refs[:n_in], refs[n_in:]
        r_ins, refs = refs[:r_in], refs[r_in:]
        outs, refs = refs[:n_out], refs[n_out:]
        r_outs, refs = refs[:r_out], refs[r_out:]
        scr, r_scr = refs[:n_scr], refs[n_scr:]
        first = functools.reduce(jnp.logical_and, [pl.program_id(ax) == 0 for ax in range(len(grid))])
        last = functools.reduce(jnp.logical_and, [pl.program_id(ax) == grid[ax] - 1 for ax in range(len(grid))])

        @pl.when(first)
        def _():
            rider.start(r_ins, r_outs, r_scr)

        body(*pre, *ins, *outs, *scr)

        @pl.when(last)
        def _():
            rider.finish(r_ins, r_outs, r_scr)

    if prefetch is None:
        res = pl.pallas_call(wrapped, name=name, grid=grid, in_specs=in_specs, out_specs=out_specs, out_shape=out_shape,
                             scratch_shapes=scratch, input_output_aliases=io_aliases,
                             compiler_params=_cparams(*sem))(*args)
    else:
        grid_spec = pltpu.PrefetchScalarGridSpec(num_scalar_prefetch=1, grid=grid, in_specs=in_specs,
                                                 out_specs=out_specs, scratch_shapes=scratch)
        res = pl.pallas_call(wrapped, name=name, grid_spec=grid_spec, out_shape=out_shape,
                             input_output_aliases=io_aliases, compiler_params=_cparams(*sem))(prefetch, *args)
    return list(res[:n_out]), list(res[n_out:])


V7X_MM_VMEM_BUDGET = 40 * 1024 * 1024


def _mm_plan(form, m, n, k, a_size, b_size, o_size, has_acc):
    tk = k if k <= 2816 else _tile(k, 2816)
    nk = k // tk
    tn = n if (form == "tn" and n <= 2048) else _tile(n, 512)
    for tm in sorted({m} | {t for t in range(128, m, 128) if m % t == 0}, reverse=True):
        need = 2 * (tm * tk * a_size + tn * tk * b_size + tm * tn * o_size) + tm * tn * 4 * (2 if nk > 1 else 1)
        need += 2 * tm * tn * 4 if has_acc else 0
        if need <= V7X_MM_VMEM_BUDGET:
            return tm, tn, tk
    return _tile(m, 128), tn, tk


def _mm(name, a, b, form, out_dtype, acc=None, rider=None):
    if form == "nt":
        (m, k), (n, k2) = a.shape, b.shape
    elif form == "nn":
        (m, k), (k2, n) = a.shape, b.shape
    else:
        (k, m), (k2, n) = a.shape, b.shape
    assert k == k2, (name, a.shape, b.shape)
    has_acc = acc is not None
    tm, tn, tk = _mm_plan(form, m, n, k, a.dtype.itemsize, b.dtype.itemsize, jnp.dtype(out_dtype).itemsize, has_acc)
    nk = k // tk
    a_spec = {"nt": pl.BlockSpec((tm, tk), lambda i, j, kk: (i, kk)),
              "nn": pl.BlockSpec((tm, tk), lambda i, j, kk: (i, kk)),
              "tn": pl.BlockSpec((tk, tm), lambda i, j, kk: (kk, i))}[form]
    b_spec = {"nt": pl.BlockSpec((tn, tk), lambda i, j, kk: (j, kk)),
              "nn": pl.BlockSpec((tk, tn), lambda i, j, kk: (kk, j)),
              "tn": pl.BlockSpec((tk, tn), lambda i, j, kk: (kk, j))}[form]
    o_spec = pl.BlockSpec((tm, tn), lambda i, j, kk: (i, j))
    dims = _DIMS[form]

    def body(*refs):
        a_ref, b_ref = refs[0], refs[1]
        c_ref = refs[2] if has_acc else None
        o_ref = refs[3] if has_acc else refs[2]

        def finish(r):
            if has_acc:
                r = r + c_ref[...]
            o_ref[...] = r.astype(o_ref.dtype)

        if nk == 1:
            finish(_dot(a_ref[...], b_ref[...], dims))
            return
        acc_ref = refs[-1]
        kk = pl.program_id(2)

        @pl.when(kk == 0)
        def _():
            acc_ref[...] = _dot(a_ref[...], b_ref[...], dims)

        @pl.when(kk > 0)
        def _():
            acc_ref[...] += _dot(a_ref[...], b_ref[...], dims)

        @pl.when(kk == nk - 1)
        def _():
            finish(acc_ref[...])

    in_specs = [a_spec, b_spec] + ([o_spec] if has_acc else [])
    args = (a, b) + ((acc,) if has_acc else ())
    outs, r_outs = _pcall(name, body, (m // tm, n // tn, nk), in_specs, [o_spec],
                          [jax.ShapeDtypeStruct((m, n), out_dtype)], args, ("parallel", "parallel", "arbitrary"),
                          scratch=[pltpu.VMEM((tm, tn), F32)] if nk > 1 else [], rider=rider)
    return outs[0] if rider is None else (outs[0], r_outs)


def _mm_fused(name, a, bs, epilogue, tile_ins, out_dtypes, rider=None):
    (m, k), (n, _) = a.shape, bs[0].shape
    tn = _tile(n, 512)
    tm = None
    for cand in sorted({m} | {t for t in range(128, m, 128) if m % t == 0}, reverse=True):
        per_tile = sum(t.dtype.itemsize for t in tile_ins) + sum(jnp.dtype(d).itemsize for d in out_dtypes)
        need = 2 * (cand * k * a.dtype.itemsize + len(bs) * tn * k * bs[0].dtype.itemsize + cand * tn * per_tile)
        need += (len(bs) + 2) * cand * tn * 4
        if need <= V7X_MM_VMEM_BUDGET:
            tm = cand
            break
    assert tm is not None, name
    n_b, n_t = len(bs), len(tile_ins)
    tile = pl.BlockSpec((tm, tn), lambda i, j: (i, j))

    def body(a_ref, *refs):
        prods = [_dot(a_ref[...], r[...], _NT) for r in refs[:n_b]]
        outs = epilogue(prods, [r[...].astype(F32) for r in refs[n_b:n_b + n_t]])
        for r, o in zip(refs[n_b + n_t:], outs):
            r[...] = o.astype(r.dtype)

    outs, r_outs = _pcall(name, body, (m // tm, n // tn),
                          [pl.BlockSpec((tm, k), lambda i, j: (i, 0))] + [pl.BlockSpec((tn, k), lambda i, j: (j, 0))] * n_b
                          + [tile] * n_t, [tile] * len(out_dtypes),
                          [jax.ShapeDtypeStruct((m, n), d) for d in out_dtypes], (a, *bs, *tile_ins),
                          ("parallel", "parallel"), rider=rider)
    return outs if rider is None else (outs, r_outs)


def _rows(name, fn, n_rows, tm, nbc, row_ins, type_ins, row_outs, acc_outs, rider=None):
    n_ri, n_ti, n_ro, n_ao = len(row_ins), len(type_ins), len(row_outs), len(acc_outs)

    def row_map(i, cb, roff):
        return (jnp.maximum(i - roff, 0), cb)

    def type_map(i):
        return (jnp.where(i >= nbc, 1, 0), 0, 0)

    in_specs, args = [], []
    for arr, cb, width, roff in row_ins:
        in_specs.append(pl.BlockSpec((tm, width), functools.partial(row_map, cb=cb, roff=roff)))
        args.append(arr)
    for arr in type_ins:
        in_specs.append(pl.BlockSpec((None, 1, arr.shape[-1]), type_map))
        args.append(arr)
    out_shape, out_specs = [], []
    for total, width, dtype, roff in row_outs:
        out_shape.append(jax.ShapeDtypeStruct((total, width), dtype))
        out_specs.append(pl.BlockSpec((tm, width), functools.partial(row_map, cb=0, roff=roff)))
    for width in acc_outs:
        out_shape.append(jax.ShapeDtypeStruct((2, 1, width), F32))
        out_specs.append(pl.BlockSpec((None, 1, width), type_map))
    n_in = n_ri + n_ti

    def body(*refs):
        i = pl.program_id(0)
        outs = fn(*[r[...].astype(F32) for r in refs[:n_in]])
        if not isinstance(outs, (tuple, list)):
            outs = (outs,)
        assert len(outs) == n_ro + n_ao, (name, len(outs))
        for r, o in zip(refs[n_in:n_in + n_ro], outs[:n_ro]):
            r[...] = o.astype(r.dtype)
        if n_ao:
            first = jnp.logical_or(i == 0, i == nbc)
            for r, o in zip(refs[n_in + n_ro:], outs[n_ro:]):
                o = jnp.broadcast_to(o.astype(F32), r.shape)

                @pl.when(first)
                def _(r=r, o=o):
                    r[...] = o

                @pl.when(jnp.logical_not(first))
                def _(r=r, o=o):
                    r[...] += o

    outs, r_outs = _pcall(name, body, (n_rows // tm,), in_specs, out_specs, out_shape, args, ("arbitrary",),
                          rider=rider)
    return outs if rider is None else (outs, r_outs)


def _vjp_fn(f, n_row, n_cot, keep=None):
    def g(*args):
        prim = args[:n_row] + args[n_row + n_cot:]
        cots = args[n_row:n_row + n_cot]
        out, vjp = jax.vjp(f, *prim)
        grads = vjp(tuple(cots) if isinstance(out, (tuple, list)) else cots[0])
        return grads if keep is None else tuple(grads[j] for j in keep)
    return g


def _typed(v):
    v = v.reshape(1, 1, -1)
    return jnp.concatenate([v, v], axis=0)


def _ln(x, g, b):
    mu = jnp.mean(x, axis=-1, keepdims=True)
    var = jnp.mean(jnp.square(x - mu), axis=-1, keepdims=True)
    return (x - mu) * lax.rsqrt(var + LN_EPS) * g + b


def _f_mod(x, sc, sh):
    return x * (1.0 + sc) + sh


def _make_f_ln(alpha, with_mod):
    def f(x, o, gate, lng, lnb, *mod):
        xn = _ln(alpha * x + gate * o, lng, lnb)
        if with_mod:
            sc, sh = mod
            return xn, xn * (1.0 + sc) + sh
        return xn
    return f


@jax.custom_vjp
def _rot(y):
    lane = lax.broadcasted_iota(jnp.int32, y.shape, 1)
    return jnp.where(lane % 64 < 32, pltpu.roll(y, 96, axis=1), pltpu.roll(y, 32, axis=1))


_rot.defvjp(lambda y: (_rot(y), None), lambda _, g: (_rot(g),))


def _f_prep(p, cos, sin, qg, kg):
    def head(xh, g):
        ms = jnp.mean(jnp.square(xh), axis=-1, keepdims=True)
        y = xh * lax.rsqrt(ms + RMS_EPS) * g
        return y * cos + _rot(y) * sin
    q = jnp.concatenate([head(p[:, h * HEAD_DIM:(h + 1) * HEAD_DIM], qg) for h in range(N_HEADS)], axis=1)
    k = jnp.concatenate([head(p[:, OFF_K + h * HEAD_DIM:OFF_K + (h + 1) * HEAD_DIM], kg)
                         for h in range(N_KV_HEADS)], axis=1)
    return q, k, p[:, OFF_V:OFF_POOL]


def _f_gate(g, t0, t1, t2, t3, b):
    d = t0.shape[-1]
    ts = (t0, t1, t2, t3)
    terms = [jax.nn.sigmoid(g[:, k * d:(k + 1) * d] + b[:, k * d:(k + 1) * d]) * ts[k] for k in range(N_BRANCH)]
    return terms[0] + terms[1] + terms[2] + terms[3]


def _f_swiglu(a, b):
    return jax.nn.silu(a) * b


def _softmax(raw):
    e = jnp.exp2((raw - jnp.max(raw, axis=-1, keepdims=True)) * (ATT_SCALE * math.log2(math.e)))
    return e / jnp.sum(e, axis=-1, keepdims=True)


def _attn_fwd(name, q, k, v, rc, ctx_queries, tq=256, rider=None):
    r = q.shape[0]
    assert rc % tq == 0 and r % tq == 0
    nqc = rc // tq

    def body(q_ref, k_ref, v_ref, o_ref):
        qi = pl.program_id(1)

        def attend(nk):
            p = _softmax(_dot(q_ref[...], k_ref[0:nk, :], _NT))
            o_ref[...] = _dot(p, v_ref[0:nk, :], _NN).astype(o_ref.dtype)

        @pl.when(qi < nqc)
        def _():
            if ctx_queries:
                attend(rc)
            else:
                o_ref[...] = jnp.zeros_like(o_ref)

        @pl.when(qi >= nqc)
        def _():
            attend(r)

    outs, r_outs = _pcall(
        name, body, (N_HEADS, r // tq),
        [pl.BlockSpec((tq, HEAD_DIM), lambda h, i: (i, h)),
         pl.BlockSpec((r, HEAD_DIM), lambda h, i: (0, h // KV_GROUP)),
         pl.BlockSpec((r, HEAD_DIM), lambda h, i: (0, h // KV_GROUP))],
        [pl.BlockSpec((tq, HEAD_DIM), lambda h, i: (i, h))],
        [jax.ShapeDtypeStruct((r, Q_W), BF16)], (q, k, v), ("parallel", "parallel"), rider=rider)
    return outs[0] if rider is None else (outs[0], r_outs)


def _attn_bwd(name, q, k, v, do, rc, ctx_queries, tq=256, rider=None):
    r = q.shape[0]
    nqc = rc // tq

    def body(q_ref, k_ref, v_ref, do_ref, dq_ref, dk_ref, dv_ref):
        g, qi = pl.program_id(1), pl.program_id(2)

        @pl.when(jnp.logical_and(g == 0, qi == 0))
        def _():
            dk_ref[...] = jnp.zeros_like(dk_ref)
            dv_ref[...] = jnp.zeros_like(dv_ref)

        def grad(nk):
            qb, kb, vb = q_ref[...], k_ref[0:nk, :], v_ref[0:nk, :]
            dob = do_ref[...].astype(BF16)
            p = _softmax(_dot(qb, kb, _NT))
            dv_ref[0:nk, :] += _dot(p, dob, _TN)
            dp = _dot(dob, vb, _NT)
            ds = p * (dp - jnp.sum(dp * p, axis=-1, keepdims=True)) * ATT_SCALE
            dq_ref[...] = _dot(ds, kb, _NN)
            dk_ref[0:nk, :] += _dot(ds, qb, _TN)

        @pl.when(qi < nqc)
        def _():
            if ctx_queries:
                grad(rc)
            else:
                dq_ref[...] = jnp.zeros_like(dq_ref)

        @pl.when(qi >= nqc)
        def _():
            grad(r)

    def qmap(kv, g, i):
        return (i, kv * KV_GROUP + g)

    def kvmap(kv, g, i):
        return (0, kv)

    outs, r_outs = _pcall(
        name, body, (N_KV_HEADS, KV_GROUP, r // tq),
        [pl.BlockSpec((tq, HEAD_DIM), qmap), pl.BlockSpec((r, HEAD_DIM), kvmap),
         pl.BlockSpec((r, HEAD_DIM), kvmap), pl.BlockSpec((tq, HEAD_DIM), qmap)],
        [pl.BlockSpec((tq, HEAD_DIM), qmap), pl.BlockSpec((r, HEAD_DIM), kvmap), pl.BlockSpec((r, HEAD_DIM), kvmap)],
        [jax.ShapeDtypeStruct((r, Q_W), F32), jax.ShapeDtypeStruct((r, KV_W), F32),
         jax.ShapeDtypeStruct((r, KV_W), F32)],
        (q, k, v, do), ("arbitrary", "arbitrary", "arbitrary"), rider=rider)
    return outs if rider is None else (outs, r_outs)


def _segments(shape, rc):
    t = lax.broadcasted_iota(jnp.int32, shape, 0)
    lo = jnp.where(t < rc, 0, rc)
    hi = jnp.where(t < rc, rc, shape[0])
    return t, lo, hi


def _shifted(x, o, t, lo, hi):
    n = x.shape[0]
    sh = pltpu.roll(x, (-o) % n, axis=0)
    return jnp.where(jnp.logical_and(t + o >= lo, t + o < hi), sh, 0.0)


def _winsum(x, left, right, t, lo, hi):
    acc = x
    for o in range(-left, right + 1):
        if o != 0:
            acc = acc + _shifted(x, o, t, lo, hi)
    return acc


def _pool_parts(z, g, t, lo, hi):
    w = POOL_WINDOWS[g]
    left = w // 2
    right = w - 1 - left
    count = (jnp.minimum(t + right + 1, hi) - jnp.maximum(t - left, lo)).astype(F32)
    return _winsum(z, left, right, t, lo, hi) / count - z, count, left, right


def _pool_fwd(name, p, pool_w, pool_scale, rc):
    r = p.shape[0]

    def body(z_ref, w_ref, s_ref, y_ref):
        t, lo, hi = _segments((r, GC), rc)
        for g in range(N_GROUPS):
            cols = slice(g * GC, (g + 1) * GC)
            d, _, _, _ = _pool_parts(z_ref[:, cols], g, t, lo, hi)
            y_ref[:, cols] = (_dot(d, w_ref[g], _NN) * s_ref[:, cols]).astype(y_ref.dtype)

    return pl.pallas_call(
        body, name=name, grid=(1,),
        in_specs=[pl.BlockSpec((r, BR_W), lambda i: (0, OFF_POOL // BR_W)),
                  pl.BlockSpec((N_GROUPS, GC, GC), lambda i: (0, 0, 0)),
                  pl.BlockSpec((1, BR_W), lambda i: (0, 0))],
        out_specs=pl.BlockSpec((r, BR_W), lambda i: (0, 0)),
        out_shape=jax.ShapeDtypeStruct((r, BR_W), BF16),
        compiler_params=_cparams("arbitrary"),
    )(p, pool_w, pool_scale.reshape(1, BR_W))


def _pool_bwd(name, p, pool_w, pool_scale, dy, rc):
    r = p.shape[0]

    def body(z_ref, w_ref, s_ref, dy_ref, dz_ref, dw_ref, ds_ref):
        t, lo, hi = _segments((r, GC), rc)
        for g in range(N_GROUPS):
            cols = slice(g * GC, (g + 1) * GC)
            d, count, left, right = _pool_parts(z_ref[:, cols], g, t, lo, hi)
            dyg = dy_ref[:, cols]
            ds_ref[:, cols] = jnp.sum(dyg * _dot(d, w_ref[g], _NN), axis=0, keepdims=True)
            dlin = dyg * s_ref[:, cols]
            dw_ref[g] = _dot(d, dlin, _TN)
            dd = _dot(dlin, w_ref[g], _NT)
            dz_ref[:, cols] = (_winsum(dd / count, right, left, t, lo, hi) - dd).astype(dz_ref.dtype)

    return pl.pallas_call(
        body, name=name, grid=(1,),
        in_specs=[pl.BlockSpec((r, BR_W), lambda i: (0, OFF_POOL // BR_W)),
                  pl.BlockSpec((N_GROUPS, GC, GC), lambda i: (0, 0, 0)),
                  pl.BlockSpec((1, BR_W), lambda i: (0, 0)),
                  pl.BlockSpec((r, BR_W), lambda i: (0, 0))],
        out_specs=[pl.BlockSpec((r, BR_W), lambda i: (0, 0)),
                   pl.BlockSpec((N_GROUPS, GC, GC), lambda i: (0, 0, 0)),
                   pl.BlockSpec((1, BR_W), lambda i: (0, 0))],
        out_shape=[jax.ShapeDtypeStruct((r, BR_W), BF16), jax.ShapeDtypeStruct((N_GROUPS, GC, GC), F32),
                   jax.ShapeDtypeStruct((1, BR_W), F32)],
        compiler_params=_cparams("arbitrary"),
    )(p, pool_w, pool_scale.reshape(1, BR_W), dy)


def _f_sgu_v(pvg, lng, lnb):
    return _ln(jax.nn.gelu(pvg), lng, lnb)


def _sgu_fwd(name, p, ln_g, ln_b, sgu_w, sgu_b):
    r = p.shape[0]

    def body(pu_ref, pv_ref, g_ref, b_ref, w_ref, sb_ref, y_ref):
        vn = _f_sgu_v(pv_ref[...], g_ref[...], b_ref[...])
        u = jax.nn.gelu(pu_ref[...])
        for g in range(N_GROUPS):
            cols = slice(g * GC, (g + 1) * GC)
            s = _dot(w_ref[g], vn[:, cols], _NN) + sb_ref[g]
            y_ref[:, cols] = (u[:, cols] * s).astype(y_ref.dtype)

    return pl.pallas_call(
        body, name=name, grid=(r // SGU_CHUNK,),
        in_specs=[pl.BlockSpec((SGU_CHUNK, BR_W), lambda i: (i, OFF_U // BR_W)),
                  pl.BlockSpec((SGU_CHUNK, BR_W), lambda i: (i, OFF_VG // BR_W)),
                  pl.BlockSpec((1, BR_W), lambda i: (0, 0)), pl.BlockSpec((1, BR_W), lambda i: (0, 0)),
                  pl.BlockSpec((N_GROUPS, GC, GC), lambda i: (0, 0, 0)),
                  pl.BlockSpec((N_GROUPS, SGU_CHUNK, 1), lambda i: (0, 0, 0))],
        out_specs=pl.BlockSpec((SGU_CHUNK, BR_W), lambda i: (i, 0)),
        out_shape=jax.ShapeDtypeStruct((r, BR_W), BF16),
        compiler_params=_cparams("parallel"),
    )(p, p, ln_g.reshape(1, BR_W), ln_b.reshape(1, BR_W), sgu_w, sgu_b.reshape(N_GROUPS, SGU_CHUNK, 1))


def _sgu_bwd(name, p, ln_g, ln_b, sgu_w, sgu_b, dy):
    r = p.shape[0]

    def body(pu_ref, pv_ref, g_ref, b_ref, w_ref, sb_ref, dy_ref, dp_ref, dg_ref, db_ref, dw_ref, dsb_ref):
        i = pl.program_id(0)

        @pl.when(i == 0)
        def _():
            for ref in (dg_ref, db_ref, dw_ref, dsb_ref):
                ref[...] = jnp.zeros_like(ref)

        vn, vjp_v = jax.vjp(_f_sgu_v, pv_ref[...], g_ref[...], b_ref[...])
        u, vjp_u = jax.vjp(jax.nn.gelu, pu_ref[...])
        dy = dy_ref[...]
        du, dvn = [], []
        for g in range(N_GROUPS):
            cols = slice(g * GC, (g + 1) * GC)
            s = _dot(w_ref[g], vn[:, cols], _NN) + sb_ref[g]
            du.append(dy[:, cols] * s)
            ds = dy[:, cols] * u[:, cols]
            dsb_ref[g] += jnp.sum(ds, axis=1, keepdims=True)
            dw_ref[g] += _dot(ds, vn[:, cols], _NT)
            dvn.append(_dot(w_ref[g], ds, _TN))
        (dpu,) = vjp_u(jnp.concatenate(du, axis=1))
        dpv, dg, db = vjp_v(jnp.concatenate(dvn, axis=1))
        dp_ref[:, 0:BR_W] = dpu.astype(dp_ref.dtype)
        dp_ref[:, BR_W:2 * BR_W] = dpv.astype(dp_ref.dtype)
        dg_ref[...] += dg
        db_ref[...] += db

    vec = pl.BlockSpec((1, BR_W), lambda i: (0, 0))
    wsp = pl.BlockSpec((N_GROUPS, GC, GC), lambda i: (0, 0, 0))
    bsp = pl.BlockSpec((N_GROUPS, SGU_CHUNK, 1), lambda i: (0, 0, 0))
    return pl.pallas_call(
        body, name=name, grid=(r // SGU_CHUNK,),
        in_specs=[pl.BlockSpec((SGU_CHUNK, BR_W), lambda i: (i, OFF_U // BR_W)),
                  pl.BlockSpec((SGU_CHUNK, BR_W), lambda i: (i, OFF_VG // BR_W)),
                  vec, vec, wsp, bsp, pl.BlockSpec((SGU_CHUNK, BR_W), lambda i: (i, 0))],
        out_specs=[pl.BlockSpec((SGU_CHUNK, 2 * BR_W), lambda i: (i, 0)), vec, vec, wsp, bsp],
        out_shape=[jax.ShapeDtypeStruct((r, 2 * BR_W), BF16), jax.ShapeDtypeStruct((1, BR_W), F32),
                   jax.ShapeDtypeStruct((1, BR_W), F32), jax.ShapeDtypeStruct((N_GROUPS, GC, GC), F32),
                   jax.ShapeDtypeStruct((N_GROUPS, SGU_CHUNK, 1), F32)],
        compiler_params=_cparams("arbitrary"),
    )(p, p, ln_g.reshape(1, BR_W), ln_b.reshape(1, BR_W), sgu_w, sgu_b.reshape(N_GROUPS, SGU_CHUNK, 1), dy)


def _conv_w8(conv_w):
    return jnp.concatenate([conv_w, jnp.zeros((8 - conv_w.shape[0], conv_w.shape[1]), F32)], axis=0)


def _conv_fwd(name, p, conv_w, rc):
    r = p.shape[0]

    def body(cb_ref, cc_ref, cx_ref, w_ref, y_ref):
        t, lo, hi = _segments((r, GC), rc)
        z = cc_ref[...] * cx_ref[...]
        w = w_ref[...]
        c = _shifted(z, -1, t, lo, hi) * w[0:1] + z * w[1:2] + _shifted(z, 1, t, lo, hi) * w[2:3]
        y_ref[...] = (cb_ref[...] * c).astype(y_ref.dtype)

    nb = OFF_CB // GC
    return pl.pallas_call(
        body, name=name, grid=(N_GROUPS,),
        in_specs=[pl.BlockSpec((r, GC), lambda j: (0, nb + j)),
                  pl.BlockSpec((r, GC), lambda j: (0, nb + N_GROUPS + j)),
                  pl.BlockSpec((r, GC), lambda j: (0, nb + 2 * N_GROUPS + j)),
                  pl.BlockSpec((8, GC), lambda j: (0, j))],
        out_specs=pl.BlockSpec((r, GC), lambda j: (0, j)),
        out_shape=jax.ShapeDtypeStruct((r, BR_W), BF16),
        compiler_params=_cparams("parallel"),
    )(p, p, p, _conv_w8(conv_w))


def _conv_bwd(name, p, conv_w, dy, rc):
    r = p.shape[0]

    def body(cb_ref, cc_ref, cx_ref, w_ref, dy_ref, dcb_ref, dcc_ref, dcx_ref, dw_ref):
        t, lo, hi = _segments((r, GC), rc)
        cc, cx, w, dy = cc_ref[...], cx_ref[...], w_ref[...], dy_ref[...]
        z = cc * cx
        zp, zn = _shifted(z, -1, t, lo, hi), _shifted(z, 1, t, lo, hi)
        dcb_ref[...] = (dy * (zp * w[0:1] + z * w[1:2] + zn * w[2:3])).astype(dcb_ref.dtype)
        dc = dy * cb_ref[...]
        dw_ref[...] = jnp.concatenate(
            [jnp.sum(dc * zp, axis=0, keepdims=True), jnp.sum(dc * z, axis=0, keepdims=True),
             jnp.sum(dc * zn, axis=0, keepdims=True), jnp.zeros((5, GC), F32)], axis=0)
        dz = dc * w[1:2] + _shifted(dc, 1, t, lo, hi) * w[0:1] + _shifted(dc, -1, t, lo, hi) * w[2:3]
        dcc_ref[...] = (dz * cx).astype(dcc_ref.dtype)
        dcx_ref[...] = (dz * cc).astype(dcx_ref.dtype)

    nb = OFF_CB // GC
    return pl.pallas_call(
        body, name=name, grid=(N_GROUPS,),
        in_specs=[pl.BlockSpec((r, GC), lambda j: (0, nb + j)),
                  pl.BlockSpec((r, GC), lambda j: (0, nb + N_GROUPS + j)),
                  pl.BlockSpec((r, GC), lambda j: (0, nb + 2 * N_GROUPS + j)),
                  pl.BlockSpec((8, GC), lambda j: (0, j)),
                  pl.BlockSpec((r, GC), lambda j: (0, j))],
        out_specs=[pl.BlockSpec((r, GC), lambda j: (0, j))] * 3 + [pl.BlockSpec((8, GC), lambda j: (0, j))],
        out_shape=[jax.ShapeDtypeStruct((r, BR_W), BF16)] * 3 + [jax.ShapeDtypeStruct((8, BR_W), F32)],
        compiler_params=_cparams("parallel"),
    )(p, p, p, _conv_w8(conv_w), dy)


def _rope_tables(rc, n):
    rows = n // GRID_W
    row = jnp.repeat(jnp.arange(rows), GRID_W).astype(F32)
    col = jnp.tile(jnp.arange(GRID_W), rows).astype(F32)
    inv = ROPE_THETA ** (-jnp.arange(0, ROPE_AXIS_DIM, 2, dtype=F32) / ROPE_AXIS_DIM)
    ang_r, ang_c = row[:, None] * inv, col[:, None] * inv
    cos = jnp.concatenate([jnp.cos(ang_r), jnp.cos(ang_r), jnp.cos(ang_c), jnp.cos(ang_c)], axis=1)
    sin = jnp.concatenate([-jnp.sin(ang_r), jnp.sin(ang_r), -jnp.sin(ang_c), jnp.sin(ang_c)], axis=1)
    cos = jnp.concatenate([jnp.ones((rc, HEAD_DIM), F32), cos], axis=0)
    sin = jnp.concatenate([jnp.zeros((rc, HEAD_DIM), F32), sin], axis=0)
    return cos, sin


MOD_NAMES = ("sh1", "sc1", "g1", "sh2", "sc2", "g2")


def _local_step(xin, target, mod, comm, sp, rc, alpha):
    def carrying(fn):
        def call(name, *args, **kw):
            rider = comm.rider(name)
            if rider is None:
                return fn(name, *args, **kw)
            res, r_outs = fn(name, *args, rider=rider, **kw)
            comm.deliver(name, r_outs)
            return res
        return call

    mm, rows, attn_fwd, attn_bwd = carrying(_mm), carrying(_rows), carrying(_attn_fwd), carrying(_attn_bwd)
    mm_fused = carrying(_mm_fused)
    r, d = xin.shape
    n_layers = mod.shape[0]
    tm_n, tm_w = 256, 128
    nbc_n, nbc_w = rc // tm_n, rc // tm_w
    cos, sin = _rope_tables(rc, r - rc)
    mp = mod.reshape(n_layers, 2, 6, 1, d)
    mods = [{nm: mp[i, :, j] for j, nm in enumerate(MOD_NAMES)} for i in range(n_layers)]
    f_ln_mod, f_ln_last = _make_f_ln(alpha, True), _make_f_ln(alpha, False)

    def whole(arr, roff=0):
        return (arr, 0, arr.shape[1], roff)

    (hb,) = rows("mod_in", _f_mod, r, tm_n, nbc_n, [whole(xin)], [mods[0]["sc1"], mods[0]["sh1"]],
                 [(r, d, BF16, 0)], [])
    saved = []
    x = xin
    for i in range(n_layers):
        last = i == n_layers - 1
        w, s, m = functools.partial(comm.weight, i), sp[i], mods[i]
        sv = {"x": x, "hb": hb}
        p = mm(f"l{i}_in", hb, w("in_t"), "nt", F32)
        q, k, v = rows(f"l{i}_prep", _f_prep, r, tm_n, nbc_n,
                       [(p, 0, OFF_POOL, 0), whole(cos), whole(sin)], [_typed(s["q_norm_g"]), _typed(s["k_norm_g"])],
                       [(r, Q_W, BF16, 0), (r, KV_W, BF16, 0), (r, KV_W, BF16, 0)], [])
        ys = [attn_fwd(f"l{i}_attn", q, k, v, rc, not last),
              _pool_fwd(f"l{i}_pool", p, s["pool_w"], s["pool_scale"], rc),
              _sgu_fwd(f"l{i}_sgu", p, s["sgu_ln_g"], s["sgu_ln_b"], s["sgu_w"], s["sgu_b"]),
              _conv_fwd(f"l{i}_conv", p, s["conv_w"], rc)]
        ts = [mm(f"l{i}_br{kk}", ys[kk], w(f"br{kk}"), "nt", BF16) for kk in range(N_BRANCH)]
        gpre = mm(f"l{i}_gate", hb, w("gate_t"), "nt", BF16)
        (mg,) = rows(f"l{i}_merge", _f_gate, r, tm_w, nbc_w, [whole(gpre)] + [whole(t) for t in ts],
                     [_typed(s["b_gate"])], [(r, d, BF16, 0)], [])
        o = mm(f"l{i}_o", mg, w("o"), "nn", F32)
        x1, h2b = rows(f"l{i}_ln1", f_ln_mod, r, tm_n, nbc_n, [whole(x), whole(o)],
                       [m["g1"], _typed(s["ln1_g"]), _typed(s["ln1_b"]), m["sc2"], m["sh2"]],
                       [(r, d, F32, 0), (r, d, BF16, 0)], [])
        af, bf, f = mm_fused(f"l{i}_ffgu", h2b, [w("ffg_t"), w("ffu_t")],
                             lambda prods, _: (prods[0], prods[1], _f_swiglu(prods[0], prods[1])), [], [BF16, BF16, BF16])
        o2 = mm(f"l{i}_ffd", f, w("ffd"), "nn", F32)
        if last:
            (x2,) = rows(f"l{i}_ln2", f_ln_last, r, tm_n, nbc_n, [whole(x1), whole(o2)],
                         [m["g2"], _typed(s["ln2_g"]), _typed(s["ln2_b"])], [(r, d, F32, 0)], [])
            hb = None
        else:
            nx = mods[i + 1]
            x2, hb = rows(f"l{i}_ln2", f_ln_mod, r, tm_n, nbc_n, [whole(x1), whole(o2)],
                          [m["g2"], _typed(s["ln2_g"]), _typed(s["ln2_b"]), nx["sc1"], nx["sh1"]],
                          [(r, d, F32, 0), (r, d, BF16, 0)], [])
        sv.update(p=p, gpre=gpre, q=q, k=k, v=v, ys=ys, ts=ts, mg=mg, o=o, x1=x1, h2b=h2b, af=af, bf=bf, f=f, o2=o2)
        saved.append(sv)
        x = x2

    lat = jnp.concatenate([jnp.zeros((1, 1, 128), F32), jnp.ones((1, 1, 128), F32)], axis=0)

    def f_loss(xb, tb, msk):
        diff = (xb - tb) * msk[:, 0:1]
        part = jnp.sum(jnp.mean(jnp.square(diff), axis=-1, keepdims=True), axis=0, keepdims=True)
        return diff * (1.0 / d), jnp.broadcast_to(part, (1, 128))

    dx_direct, loss_acc = rows("loss", f_loss, r, tm_n, nbc_n, [whole(x), whole(target, nbc_n)], [lat],
                               [(r, d, F32, 0)], [128])
    loss = 0.5 * loss_acc[1, 0, 0]

    dmods = [dict() for _ in range(n_layers)]
    dsp = [dict() for _ in range(n_layers)]
    dh = None

    def small_done(j):
        ds = dict(dsp[j])
        for nm in ("ln1_g", "ln1_b", "ln2_g", "ln2_b", "b_gate", "q_norm_g", "k_norm_g"):
            ds[nm] = ds[nm][0, 0] + ds[nm][1, 0]
        for nm in ("pool_scale", "sgu_ln_g", "sgu_ln_b"):
            ds[nm] = ds[nm].reshape(-1)
        ds["sgu_b"] = ds["sgu_b"].reshape(N_GROUPS, SGU_CHUNK)
        comm.small_ready(j, jnp.concatenate([dmods[j][nm][:, 0, :] for nm in MOD_NAMES], axis=-1), ds)

    for i in reversed(range(n_layers)):
        last = i == n_layers - 1
        w, s, m, sv = functools.partial(comm.weight, i), sp[i], mods[i], saved[i]
        dm, dw, ds = dmods[i], {}, dsp[i]
        ln2 = [m["g2"], _typed(s["ln2_g"]), _typed(s["ln2_b"])]
        if last:
            res = rows(f"l{i}_ln2_bwd", _vjp_fn(f_ln_last, 2, 1), r, tm_n, nbc_n,
                       [whole(sv["x1"]), whole(sv["o2"]), whole(dx_direct)], ln2,
                       [(r, d, F32, 0), (r, d, BF16, 0)], [d, d, d])
            dx1, do2, dm["g2"], dlg, dlb = res
        else:
            nx = mods[i + 1]
            res = rows(f"l{i}_ln2_bwd", _vjp_fn(f_ln_mod, 2, 2), r, tm_n, nbc_n,
                       [whole(sv["x1"]), whole(sv["o2"]), whole(dx_direct), whole(dh)],
                       ln2 + [nx["sc1"], nx["sh1"]],
                       [(r, d, F32, 0), (r, d, BF16, 0)], [d, d, d, d, d])
            dx1, do2, dm["g2"], dlg, dlb, dmods[i + 1]["sc1"], dmods[i + 1]["sh1"] = res
            small_done(i + 1)
        ds["ln2_g"], ds["ln2_b"] = dlg, dlb
        dab, dbb = mm_fused(f"l{i}_dF", do2, [w("ffd")],
                            lambda prods, tiles: jax.vjp(_f_swiglu, *tiles)[1](prods[0]), [sv["af"], sv["bf"]],
                            [BF16, BF16])
        comm.grads(i, {"ffd": mm(f"l{i}_dWffd", sv["f"], do2, "tn", BF16)})
        comm.grads(i, {"ffg_t": mm(f"l{i}_dWffg", dab, sv["h2b"], "tn", BF16)})
        comm.grads(i, {"ffu_t": mm(f"l{i}_dWffu", dbb, sv["h2b"], "tn", BF16)})
        dh2 = mm(f"l{i}_dh2a", dab, w("ffg_t"), "nn", F32)
        dh2 = mm(f"l{i}_dh2b", dbb, w("ffu_t"), "nn", F32, acc=dh2)
        res = rows(f"l{i}_ln1_bwd", _vjp_fn(f_ln_mod, 2, 2), r, tm_n, nbc_n,
                   [whole(sv["x"]), whole(sv["o"]), whole(dx1), whole(dh2)],
                   [m["g1"], _typed(s["ln1_g"]), _typed(s["ln1_b"]), m["sc2"], m["sh2"]],
                   [(r, d, F32, 0), (r, d, BF16, 0)], [d, d, d, d, d])
        dx_direct, do, dm["g1"], ds["ln1_g"], ds["ln1_b"], dm["sc2"], dm["sh2"] = res
        dmg = mm(f"l{i}_dMg", do, w("o"), "nt", F32)
        comm.grads(i, {"o": mm(f"l{i}_dWo", sv["mg"], do, "tn", BF16)})
        res = rows(f"l{i}_merge_bwd", _vjp_fn(_f_gate, 5, 1), r, tm_w, nbc_w,
                   [whole(sv["gpre"])] + [whole(t) for t in sv["ts"]] + [whole(dmg)], [_typed(s["b_gate"])],
                   [(r, N_BRANCH * d, BF16, 0)] + [(r, d, BF16, 0)] * N_BRANCH, [N_BRANCH * d])
        dgb, dts, ds["b_gate"] = res[0], res[1:1 + N_BRANCH], res[1 + N_BRANCH]
        comm.grads(i, {"gate_t": mm(f"l{i}_dWgate", dgb, sv["hb"], "tn", BF16)})
        dys = [mm(f"l{i}_dY{kk}", dts[kk], w(f"br{kk}"), "nn", F32) for kk in range(N_BRANCH)]
        for kk in range(N_BRANCH):
            comm.grads(i, {f"br{kk}": mm(f"l{i}_dWbr{kk}", dts[kk], sv["ys"][kk], "tn", BF16)})
        dq, dk, dv = attn_bwd(f"l{i}_attn_bwd", sv["q"], sv["k"], sv["v"], dys[0], rc, not last)
        res = rows(f"l{i}_prep_bwd", _vjp_fn(_f_prep, 3, 3, keep=(0, 3, 4)), r, tm_n, nbc_n,
                   [(sv["p"], 0, OFF_POOL, 0), whole(cos), whole(sin), whole(dq), whole(dk), whole(dv)],
                   [_typed(s["q_norm_g"]), _typed(s["k_norm_g"])],
                   [(r, OFF_POOL, BF16, 0)], [HEAD_DIM, HEAD_DIM])
        dp_qkv, ds["q_norm_g"], ds["k_norm_g"] = res
        dp_pool, ds["pool_w"], ds["pool_scale"] = _pool_bwd(f"l{i}_pool_bwd", sv["p"], s["pool_w"], s["pool_scale"],
                                                            dys[1], rc)
        dp_sgu, ds["sgu_ln_g"], ds["sgu_ln_b"], ds["sgu_w"], ds["sgu_b"] = _sgu_bwd(
            f"l{i}_sgu_bwd", sv["p"], s["sgu_ln_g"], s["sgu_ln_b"], s["sgu_w"], s["sgu_b"], dys[2])
        dp_cb, dp_cc, dp_cx, dcw = _conv_bwd(f"l{i}_conv_bwd", sv["p"], s["conv_w"], dys[3], rc)
        ds["conv_w"] = dcw[0:3]
        dpb = jnp.concatenate([dp_qkv, dp_pool, dp_sgu, dp_cb, dp_cc, dp_cx], axis=1)
        comm.grads(i, {"in_t": mm(f"l{i}_dWin", dpb, sv["hb"], "tn", BF16)})
        dh = mm(f"l{i}_dhb_a", dpb, w("in_t"), "nn", F32)
        dh = mm(f"l{i}_dhb_b", dgb, w("gate_t"), "nn", F32, acc=dh)

    def f_mod_bwd(xb, ddir, dhb, sc, sh):
        _, vjp = jax.vjp(_f_mod, xb, sc, sh)
        dxb, dsc, dsh = vjp(dhb)
        return dxb + ddir, dsc, dsh

    grad_x, dmods[0]["sc1"], dmods[0]["sh1"] = rows(
        "mod_in_bwd", f_mod_bwd, r, tm_n, nbc_n, [whole(xin), whole(dx_direct), whole(dh)],
        [mods[0]["sc1"], mods[0]["sh1"]], [(r - rc, d, F32, nbc_n)], [d, d])
    small_done(0)
    return loss, grad_x


def _exchange(name, srcs, scatter):
    n_items = len(srcs)
    out_shapes = [jax.ShapeDtypeStruct(s.shape if scatter else (N_DEV,) + s.shape, s.dtype) for s in srcs]

    def body(*refs):
        src, out = refs[:n_items], refs[n_items:2 * n_items]
        send_sems, recv_sems, loc_sems = refs[2 * n_items:]
        mx, my, mc = [lax.axis_index(a) for a in MESH_AXES]
        me = 4 * mx + 2 * my + mc
        peers = []
        for kk in range(1, N_DEV):
            px = 1 - mx if kk & 4 else mx
            py = 1 - my if kk & 2 else my
            pc = 1 - mc if kk & 1 else mc
            peers.append(((px, py, pc), 4 * px + 2 * py + pc))
        local, sent = [], []
        for a in range(n_items):
            loc = pltpu.make_async_copy(src[a].at[me] if scatter else src[a], out[a].at[me], loc_sems.at[a])
            loc.start()
            local.append(loc)
            for j, (peer, peer_l) in enumerate(peers):
                cp = pltpu.make_async_remote_copy(
                    src_ref=src[a].at[peer_l] if scatter else src[a], dst_ref=out[a].at[me],
                    send_sem=send_sems.at[a * (N_DEV - 1) + j], recv_sem=recv_sems.at[a * (N_DEV - 1) + j],
                    device_id=peer, device_id_type=pl.DeviceIdType.MESH)
                cp.start()
                sent.append(cp)
        for a in range(n_items):
            for j, (peer, peer_l) in enumerate(peers):
                pltpu.make_async_remote_copy(
                    src_ref=src[a].at[peer_l] if scatter else src[a], dst_ref=out[a].at[peer_l],
                    send_sem=send_sems.at[a * (N_DEV - 1) + j], recv_sem=recv_sems.at[a * (N_DEV - 1) + j],
                    device_id=peer, device_id_type=pl.DeviceIdType.MESH).wait_recv()
        for cp in sent:
            cp.wait_send()
        for loc in local:
            loc.wait()

    any_spec = pl.BlockSpec(memory_space=pl.ANY)
    res = pl.pallas_call(
        body, name=name,
        in_specs=[any_spec] * n_items, out_specs=[any_spec] * n_items, out_shape=out_shapes,
        scratch_shapes=[pltpu.SemaphoreType.DMA((n_items * (N_DEV - 1),)),
                        pltpu.SemaphoreType.DMA((n_items * (N_DEV - 1),)),
                        pltpu.SemaphoreType.DMA((n_items,))],
    )(*srcs)
    return list(res)


def _mesh_place():
    mx, my, mc = [lax.axis_index(a) for a in MESH_AXES]
    chips = [(1 - mx, my), (mx, 1 - my), (1 - mx, 1 - my)]

    def lid(px, py, pc):
        return 4 * px + 2 * py + pc

    return (mx, my, mc), (mx, my, 1 - mc), chips, lid


def _rcopy(src, dst, send_sems, recv_sems, k, to):
    return pltpu.make_async_remote_copy(src_ref=src, dst_ref=dst, send_sem=send_sems.at[k], recv_sem=recv_sems.at[k],
                                        device_id=to, device_id_type=pl.DeviceIdType.MESH)


def _sem_scratch(*sizes):
    return [pltpu.SemaphoreType.DMA((s,)) for s in sizes]


def _gather_rider(src, rows, buf=None):
    r0, r1 = rows
    win = pl.ds(r0, r1 - r0)

    def start(ins, outs, sems):
        send, recv, loc = sems
        (mx, my, mc), sib, chips, lid = _mesh_place()
        mine, dst = ins[0].at[win], outs[0].at[lid(mx, my, mc), win]
        pltpu.make_async_copy(mine, dst, loc.at[0]).start()
        _rcopy(mine, dst, send, recv, 0, sib).start()
        for j, chip in enumerate(chips):
            _rcopy(mine, dst, send, recv, 1 + j, (*chip, mc)).start()

    def finish(ins, outs, sems):
        send, recv, loc = sems
        (mx, my, mc), sib, chips, lid = _mesh_place()
        mine, dst = ins[0].at[win], outs[0].at[lid(mx, my, mc), win]
        for j, chip in enumerate(chips):
            blk = outs[0].at[lid(*chip, mc), win]
            _rcopy(mine, blk, send, recv, 1 + j, (*chip, mc)).wait_recv()
            _rcopy(blk, blk, send, recv, 4 + j, sib).start()
        _rcopy(mine, outs[0].at[lid(*sib), win], send, recv, 0, sib).wait_recv()
        for j, chip in enumerate(chips):
            _rcopy(mine, outs[0].at[lid(*chip, 1 - mc), win], send, recv, 4 + j, sib).wait_recv()
        for t in range(7):
            _rcopy(mine, dst, send, recv, t, sib).wait_send()
        pltpu.make_async_copy(mine, dst, loc.at[0]).wait()

    out_shape = jax.ShapeDtypeStruct((N_DEV,) + src.shape, src.dtype)
    if buf is None:
        return _Rider([src], [out_shape], _sem_scratch(7, 7, 1), start, finish)
    return _Rider([src, buf], [out_shape], _sem_scratch(7, 7, 1), start, finish, aliases={1: 0})


def _sibling_rider(part):
    def start(ins, outs, sems):
        send, recv = sems
        (mx, my, mc), sib, chips, lid = _mesh_place()
        for t, slab in enumerate([lid(*sib)] + [lid(*chip, 1 - mc) for chip in chips]):
            _rcopy(ins[0].at[slab], outs[0].at[t], send, recv, t, sib).start()

    def finish(ins, outs, sems):
        send, recv = sems
        _, sib, _, _ = _mesh_place()
        for t in range(4):
            cp = _rcopy(ins[0].at[0], outs[0].at[t], send, recv, t, sib)
            cp.wait_recv()
            cp.wait_send()

    return _Rider([part], [jax.ShapeDtypeStruct((4,) + part.shape[1:], part.dtype)], _sem_scratch(4, 4), start, finish)


def _chips_rider(pair, rows, buf=None):
    r0, r1 = rows
    win = pl.ds(r0, r1 - r0)

    def start(ins, outs, sems):
        send, recv = sems
        (mx, my, mc), sib, chips, lid = _mesh_place()
        for j, chip in enumerate(chips):
            _rcopy(ins[0].at[j, win], outs[0].at[j, win], send, recv, j, (*chip, mc)).start()

    def finish(ins, outs, sems):
        send, recv = sems
        (mx, my, mc), sib, chips, lid = _mesh_place()
        for j, chip in enumerate(chips):
            cp = _rcopy(ins[0].at[j, win], outs[0].at[j, win], send, recv, j, (*chip, mc))
            cp.wait_recv()
            cp.wait_send()

    out_shape = jax.ShapeDtypeStruct(pair.shape, pair.dtype)
    if buf is None:
        return _Rider([pair], [out_shape], _sem_scratch(3, 3), start, finish)
    return _Rider([pair, buf], [out_shape], _sem_scratch(3, 3), start, finish, aliases={1: 0})


def _run_rider(name, rider):
    n_in, n_out = len(rider.inputs), len(rider.out_shapes)

    def body(*refs):
        ins, outs, sems = refs[:n_in], refs[n_in:n_in + n_out], refs[n_in + n_out:]
        rider.start(ins, outs, sems)
        rider.finish(ins, outs, sems)

    any_spec = pl.BlockSpec(memory_space=pl.ANY)
    res = pl.pallas_call(body, name=name, in_specs=[any_spec] * n_in, out_specs=[any_spec] * n_out,
                         out_shape=rider.out_shapes, scratch_shapes=rider.scratch,
                         input_output_aliases=rider.aliases)(*rider.inputs)
    return list(res)


def _slab_ids():
    (mx, my, mc), _, chips, lid = _mesh_place()
    return jnp.stack([lid(*chip, mc) for chip in chips] + [lid(mx, my, mc)]).astype(jnp.int32)


def _pair_sum(name, part, rsib, ids):
    _, n, k = part.shape
    tr = _row_tile(n, 512, 16)

    def body(ids_ref, p_ref, r_ref, o_ref):
        o_ref[...] = (p_ref[...].astype(F32) + r_ref[...].astype(F32)).astype(o_ref.dtype)

    grid_spec = pltpu.PrefetchScalarGridSpec(
        num_scalar_prefetch=1, grid=(3, n // tr),
        in_specs=[pl.BlockSpec((None, tr, k), lambda j, i, ids: (ids[j], i, 0)),
                  pl.BlockSpec((None, tr, k), lambda j, i, ids: (1 + j, i, 0))],
        out_specs=pl.BlockSpec((None, tr, k), lambda j, i, ids: (j, i, 0)))
    return pl.pallas_call(body, name=name, grid_spec=grid_spec, out_shape=jax.ShapeDtypeStruct((3, n, k), part.dtype),
                          compiler_params=_cparams("parallel", "parallel"))(ids, part, rsib)


def _sum5(name, part, rsib, rici, ids, layer, stacked, rider=None):
    _, n, k = part.shape
    tr = _row_tile(n, 512, 16)
    first = isinstance(stacked, int)

    def body(ids_ref, p_ref, r_ref, c_ref, *rest):
        acc = p_ref[...].astype(F32) + r_ref[...].astype(F32)
        for j in range(3):
            acc = acc + c_ref[j].astype(F32)
        rest[-1][...] = acc

    in_specs = [pl.BlockSpec((None, tr, k), lambda i, ids: (ids[3], i, 0)),
                pl.BlockSpec((None, tr, k), lambda i, ids: (0, i, 0)),
                pl.BlockSpec((3, tr, k), lambda i, ids: (0, i, 0))] + ([] if first else [pl.BlockSpec(memory_space=pl.ANY)])
    n_layers = stacked if first else stacked.shape[0]
    outs, r_outs = _pcall(name, body, (n // tr,), in_specs, [pl.BlockSpec((None, tr, k), lambda i, ids: (layer, i, 0))],
                          [jax.ShapeDtypeStruct((n_layers, n, k), F32)],
                          (part, rsib, rici) + (() if first else (stacked,)), ("parallel",), rider=rider, prefetch=ids,
                          aliases={} if first else {3: 0})
    return outs[0] if rider is None else (outs[0], r_outs)


W_KEYS = ("in_t", "br0", "br1", "br2", "br3", "gate_t", "o", "ffg_t", "ffu_t", "ffd")
SUMS_TRANSPOSED_LATER = ("gate_t", "br0", "br1", "br2", "br3")


CARRIER_US = {"mod_in": 12, "in": 55, "gate": 95, "prep": 19, "attn": 112, "br0": 15, "merge": 50, "o": 25, "ln1": 27,
              "ffgu": 135, "ffd": 73, "ln2": 27, "loss": 20, "ln2_bwd": 44, "dF": 80,
              "dWffd": 64, "dh2a": 75, "dh2b": 75, "dWffg": 64, "dWffu": 64, "ln1_bwd": 44,
              "dMg": 25, "dWo": 25, "merge_bwd": 80, "dY0": 14, "dWbr0": 15, "attn_bwd": 195, "prep_bwd": 28,
              "dWin": 54, "dWgate": 95, "dhb_a": 64, "dhb_b": 115}
ICI_US_PER_MIB = 45.0
D2D_US_PER_MIB = 6.8
MIN_CHUNK_US = 10.0
CARRIER_FILL = 1.15


class _Comm:
    def __init__(self, wsrc):
        self.wsrc = wsrc
        self.n_layers = len(wsrc)
        self.queue = []
        self.riding = {}
        self.buf, self.left = {}, {}
        self.part, self.rsib, self.pair = {}, {}, {}
        self.ids = _slab_ids()
        for i in range(self.n_layers):
            for k in W_KEYS:
                self._push_chunks("gather", ("w", i, k), wsrc[i][k].shape, wsrc[i][k].dtype)

    def _push_chunks(self, kind, item, shape, dtype):
        n, k = shape[-2], shape[-1]
        us = n * k * jnp.dtype(dtype).itemsize / 2 ** 20 * ICI_US_PER_MIB
        pieces = max(1, int(us // MIN_CHUNK_US))
        while n % (16 * pieces):
            pieces -= 1
        step = n // pieces
        self.left[item] = pieces
        for c in range(pieces):
            self.queue.append(dict(kind=kind, item=item, rows=(c * step, (c + 1) * step), us=us / pieces))

    @staticmethod
    def _merge(units, u):
        v = units[-1] if units else None
        if not (v and v["item"] == u["item"] and v["kind"] == u["kind"] and u["rows"] and v["rows"][1] == u["rows"][0]):
            return False
        v.update(rows=(v["rows"][0], u["rows"][1]), us=v["us"] + u["us"], count=v.get("count", 1) + u.get("count", 1))
        return True

    def _unit_rider(self, u):
        item = u["item"]
        if u["kind"] == "gather":
            src = self.wsrc[item[1]][item[2]] if item[0] == "w" else self.part[item]
            return _gather_rider(src, u["rows"], self.buf.get(item))
        if u["kind"] == "sibling":
            return _sibling_rider(self.part[item])
        return _chips_rider(self.pair[item], u["rows"], self.buf.get(item))

    def _done(self, u, out):
        item = u["item"]
        if u["kind"] == "sibling":
            self.rsib[item] = out
            self.pair[item] = _pair_sum(f"pair_l{item[1]}_{item[2]}", self.part[item], out, self.ids)
            self._push_chunks("chips", item, self.pair[item].shape, self.pair[item].dtype)
            return
        self.buf[item] = out
        self.left[item] -= u.get("count", 1)

    def _send(self, name, units, call):
        outs = call(_compose([self._unit_rider(u) for u in units]))
        for u, o in zip(units, outs):
            self._done(u, o)

    def rider(self, name, budget_us=None):
        budget = CARRIER_US.get(name.split("_", 1)[1] if name[0] == "l" and name[1].isdigit() else name, 0) \
            if budget_us is None else budget_us
        units, used = [], 0.0
        while self.queue and used + self.queue[0]["us"] <= CARRIER_FILL * budget:
            u = self.queue[0]
            if not self._merge(units, u):
                if any(v["item"] == u["item"] for v in units):
                    break
                units.append(dict(u))
            used += u["us"]
            del self.queue[0]
        if not units:
            return None
        self.riding[name] = units
        return _compose([self._unit_rider(u) for u in units])

    def deliver(self, name, outs):
        for u, o in zip(self.riding.pop(name), outs):
            self._done(u, o)

    def _flush(self, item, kinds):
        hits = [p for p, u in enumerate(self.queue) if u["item"] == item and u["kind"] in kinds]
        if not hits:
            return
        prefix = self.queue[:hits[-1] + 1]
        del self.queue[:hits[-1] + 1]
        units = []
        for u in prefix:
            if not self._merge(units, u):
                units.append(dict(u))
        tag = "_".join(str(t) for t in item) + "_" + kinds[0]
        batches = [[]]
        for u in units:
            if any(v["item"] == u["item"] for v in batches[-1]):
                batches.append([])
            batches[-1].append(u)
        for b, batch in enumerate(batches):
            self._send(None, batch, functools.partial(_run_rider, f"alone_{tag}_{b}"))

    def begin(self):
        self._flush(("w", 0, "in_t"), ("gather",))

    def weight(self, i, k):
        item = ("w", i, k)
        self._flush(item, ("gather",))
        o = self.buf[item]
        return o.reshape(-1, o.shape[-1])

    def grads(self, i, group):
        for k, g in group.items():
            item = ("g", i, k)
            self.part[item] = g.reshape(N_DEV, g.shape[0] // N_DEV, g.shape[1])
            us = g.size // N_DEV * g.dtype.itemsize / 2 ** 20 * D2D_US_PER_MIB
            self.queue.insert(0, dict(kind="sibling", item=item, rows=None, us=us))

    def total(self, k):
        out = self.n_layers
        for i in range(self.n_layers):
            item = ("g", i, k)
            self._flush(item, ("sibling",))
            self._flush(item, ("chips",))
            name = f"sum_l{i}_{k}"
            rider = self.rider(name, budget_us=self.part[item][0].size / 1.06e5) if k in SUMS_TRANSPOSED_LATER else None
            out = _sum5(name, self.part[item], self.rsib[item], self.buf[item], self.ids, i, out, rider=rider)
            if rider is not None:
                out, r_outs = out
                self.deliver(name, r_outs)
        return out

    def small_ready(self, i, dmod, ds):
        parts = [dmod[0], dmod[1]] + [ds[nm] for nm in LAYER_SMALL + ("conv_w",)]
        self.small_shapes = [p.shape for p in parts]
        self.gather_small(f"lat{i}", _pack([dmod[1]]))
        self.gather_small(f"small{i}", _pack(parts))

    def gather_small(self, name, arr):
        item = ("s", name)
        self.part[item] = arr
        waiting, self.queue = self.queue, []
        self._push_chunks("gather", item, arr.shape, arr.dtype)
        self.queue += waiting

    def gathered(self, name):
        item = ("s", name)
        self._flush(item, ("gather",))
        return self.buf[item]


def _row_tile(n, pref, mult):
    best = None
    t = mult
    while t <= min(n, pref):
        if n % t == 0:
            best = t
        t += mult
    return best if best is not None else n


def _sum8(name, slabs):
    _, n, k = slabs.shape
    tr = _row_tile(n, 128, 16)

    def body(s_ref, o_ref):
        acc = s_ref[0].astype(F32)
        for j in range(1, N_DEV):
            acc = acc + s_ref[j].astype(F32)
        o_ref[...] = acc

    return pl.pallas_call(
        body, name=name, grid=(n // tr,),
        in_specs=[pl.BlockSpec((N_DEV, tr, k), lambda i: (0, i, 0))],
        out_specs=pl.BlockSpec((tr, k), lambda i: (i, 0)),
        out_shape=jax.ShapeDtypeStruct((n, k), F32),
        compiler_params=_cparams("parallel"),
    )(slabs)


def _adamw(name, w, g, m, v, rider=None):
    n, k = w.shape[-2:]
    tr = _row_tile(n, 256, 8)

    def body(w_ref, g_ref, m_ref, v_ref, d_ref, m2_ref, v2_ref):
        gv = g_ref[...]
        m2 = ADAM_B1 * m_ref[...] + (1.0 - ADAM_B1) * gv
        v2 = ADAM_B2 * v_ref[...] + (1.0 - ADAM_B2) * jnp.square(gv)
        m_hat = m2 / (1.0 - ADAM_B1 ** ADAM_STEP)
        v_hat = v2 / (1.0 - ADAM_B2 ** ADAM_STEP)
        d_ref[...] = -ADAM_LR * (m_hat / (jnp.sqrt(v_hat) + ADAM_EPS) + ADAM_WD * w_ref[...])
        m2_ref[...] = m2
        v2_ref[...] = v2

    if w.ndim == 2:
        grid, spec = (n // tr,), pl.BlockSpec((tr, k), lambda i: (i, 0))
    else:
        grid, spec = (w.shape[0], n // tr), pl.BlockSpec((None, tr, k), lambda l, i: (l, i, 0))
    outs, r_outs = _pcall(name, body, grid, [spec] * 4, [spec] * 3, [jax.ShapeDtypeStruct(w.shape, F32)] * 3,
                          (w, g, m, v), ("parallel",) * len(grid), rider=rider)
    return outs if rider is None else (outs, r_outs)


def _pack(arrs):
    flat = jnp.concatenate([a.reshape(-1).astype(F32) for a in arrs])
    pad = (-flat.shape[0]) % 2048
    if pad:
        flat = jnp.concatenate([flat, jnp.zeros((pad,), F32)])
    return flat.reshape(-1, 128)


def _unpack(packed, shapes):
    flat = packed.reshape(-1)
    out, off = [], 0
    for shp in shapes:
        size = math.prod(shp)
        out.append(flat[off:off + size].reshape(shp))
        off += size
    return out


WEIGHT_NAMES = ("c_ctx", "w_ada", "b_ada", "w_in", "q_norm_g", "k_norm_g", "pool_w", "pool_scale", "sgu_ln_g",
                "sgu_ln_b", "sgu_w", "sgu_b", "conv_w", "w_br_attn", "w_br_pool", "w_br_sgu", "w_br_conv", "w_gate",
                "b_gate", "w_o", "ln1_g", "ln1_b", "w_ff_gate", "w_ff_up", "w_ff_down", "ln2_g", "ln2_b")
COL_SHARDED = {"w_in": "in_t", "w_gate": "gate_t", "w_ff_gate": "ffg_t", "w_ff_up": "ffu_t",
               "w_br_attn": "br0", "w_br_pool": "br1", "w_br_sgu": "br2", "w_br_conv": "br3"}
ROW_SHARDED = {"w_o": "o", "w_ff_down": "ffd"}
LAYER_SMALL = ("q_norm_g", "k_norm_g", "pool_w", "pool_scale", "sgu_ln_g", "sgu_ln_b", "sgu_w", "sgu_b", "b_gate",
               "ln1_g", "ln1_b", "ln2_g", "ln2_b")
SMALL_ORDER = ("c_ctx", "b_ada") + LAYER_SMALL + ("conv_w",)


def _train_step(a):
    n_layers, d = a["w_in"].shape[0], a["x"].shape[-1]
    rc = a["ctx"].shape[1]
    alpha = (2 * n_layers) ** 0.25
    mx, my, mc = [lax.axis_index(ax) for ax in MESH_AXES]
    me = 4 * mx + 2 * my + mc
    ada_w = a["w_ada"].shape[-1]
    cw_loc = a["conv_w"].shape[-1]

    n_c, n_cw = d, n_layers * 3 * cw_loc
    got = _exchange("gather_cond", [_pack([a["c"], a["conv_w"]])], False)[0].reshape(N_DEV, -1)
    c_all = got[:, :n_c]
    conv_w = got[:, n_c:n_c + n_cw].reshape(N_DEV, n_layers, 3, cw_loc).transpose(1, 2, 0, 3).reshape(n_layers, 3, -1)
    cond = jnp.concatenate([c_all, a["c_ctx"][None], jnp.zeros((16 - N_DEV - 1, d), F32)], axis=0)
    sil, sil_vjp = jax.vjp(jax.nn.silu, cond)
    sil = sil.astype(BF16)

    mod_cols = jnp.concatenate([_mm(f"ada{i}", sil, a["w_ada"][i], "nn", F32) for i in range(n_layers)], axis=0)
    got = _exchange("gather_mod", [mod_cols], False)[0]
    mod_all = got.reshape(N_DEV, n_layers, 16, ada_w).transpose(1, 2, 0, 3).reshape(n_layers, 16, -1)
    mod_all = mod_all + a["b_ada"][:, None, :]
    mod = jnp.stack([mod_all[:, N_DEV], lax.dynamic_index_in_dim(mod_all, me, axis=1, keepdims=False)], axis=1)

    comm = _Comm([{**{key: jnp.swapaxes(a[nm], 1, 2)[i].astype(BF16) for nm, key in COL_SHARDED.items()},
                   **{key: a[nm][i].astype(BF16) for nm, key in ROW_SHARDED.items()}} for i in range(n_layers)])
    comm.begin()
    sp = [{nm: a[nm][i] for nm in LAYER_SMALL} for i in range(n_layers)]
    for i in range(n_layers):
        sp[i]["conv_w"] = conv_w[i]

    xin = jnp.concatenate([a["ctx"][0], a["x"][0]], axis=0)
    loss_l, grad_x = _local_step(xin, a["loss_target"][0], mod, comm, sp, rc, alpha)
    loss = lax.psum(loss_l, MESH_AXES)
    grads = {}

    def transposed_home(nm):
        return nm in COL_SHARDED and a[nm].shape[-1] % 128 != 0

    delta, new_m, new_v = {}, {}, {}

    def adamw(nm):
        name = f"adamw_{nm}"
        there = transposed_home(nm)
        view = (lambda t: jnp.swapaxes(t, 1, 2)) if there else (lambda t: t)
        if nm in COL_SHARDED:
            g = comm.total(COL_SHARDED[nm])
            grads[nm] = jnp.swapaxes(g, 1, 2)
            g = g if there else grads[nm]
        elif nm in ROW_SHARDED:
            g = grads[nm] = comm.total(ROW_SHARDED[nm])
        else:
            g = grads[nm]
        res = _adamw(name, view(a[nm]), g, view(a["m_" + nm]), view(a["v_" + nm]))
        delta[nm], new_m[nm], new_v[nm] = [view(t) for t in res]

    for nm in ("w_ff_down", "w_ff_gate", "w_ff_up", "w_o", "w_br_attn", "w_br_pool", "w_br_sgu", "w_br_conv"):
        adamw(nm)

    tots = [_unpack(_sum8(f"sum_small{i}", comm.gathered(f"small{i}")), comm.small_shapes) for i in range(n_layers)]
    dmod_c, dmod_lat_sum = jnp.stack([t[0] for t in tots]), jnp.stack([t[1] for t in tots])
    for j, nm in enumerate(LAYER_SMALL + ("conv_w",)):
        grads[nm] = jnp.stack([t[2 + j] for t in tots])
    grads["conv_w"] = lax.dynamic_slice_in_dim(grads["conv_w"], me * cw_loc, cw_loc, axis=2)
    grads["b_ada"] = dmod_c + dmod_lat_sum
    dmod_lat_all = jnp.stack([comm.gathered(f"lat{i}").reshape(N_DEV, -1)[:, :6 * d] for i in range(n_layers)])
    dm_rows = jnp.concatenate([dmod_lat_all, dmod_c[:, None, :],
                               jnp.zeros((n_layers, 16 - N_DEV - 1, 6 * d), F32)], axis=1)
    dm_cols = lax.dynamic_slice_in_dim(dm_rows, me * ada_w, ada_w, axis=2).astype(BF16)
    grads["w_ada"] = jnp.stack([_mm(f"dWada{i}", sil, dm_cols[i], "tn", F32) for i in range(n_layers)])
    dsil = None
    for i in range(n_layers):
        dsil = _mm(f"dsil{i}", dm_cols[i], a["w_ada"][i], "nt", F32, acc=dsil)
    got = _exchange("gather_dsil", [dsil], False)[0]
    dsil = _sum8("sum_dsil", got)
    grads["c_ctx"] = sil_vjp(dsil)[0][N_DEV]

    for nm in ("w_ada", "w_in", "w_gate"):
        adamw(nm)
    shapes = [a[nm].shape for nm in SMALL_ORDER]
    res = _adamw("adamw_small", _pack([a[nm] for nm in SMALL_ORDER]), _pack([grads[nm] for nm in SMALL_ORDER]),
                 _pack([a["m_" + nm] for nm in SMALL_ORDER]), _pack([a["v_" + nm] for nm in SMALL_ORDER]))
    for tree, packed in zip((delta, new_m, new_v), res):
        for nm, t in zip(SMALL_ORDER, _unpack(packed, shapes)):
            tree[nm] = t
    return (loss, grad_x[None], *[grads[nm] for nm in WEIGHT_NAMES], *[delta[nm] for nm in WEIGHT_NAMES],
            *[new_m[nm] for nm in WEIGHT_NAMES], *[new_v[nm] for nm in WEIGHT_NAMES])


def kernel(x, c, ctx, c_ctx, w_ada, b_ada, w_in, q_norm_g, k_norm_g, pool_w, pool_scale, sgu_ln_g, sgu_ln_b, sgu_w, sgu_b, conv_w, w_br_attn, w_br_pool, w_br_sgu, w_br_conv, w_gate, b_gate, w_o, ln1_g, ln1_b, w_ff_gate, w_ff_up, w_ff_down, ln2_g, ln2_b, loss_target, m_c_ctx, m_w_ada, m_b_ada, m_w_in, m_q_norm_g, m_k_norm_g, m_pool_w, m_pool_scale, m_sgu_ln_g, m_sgu_ln_b, m_sgu_w, m_sgu_b, m_conv_w, m_w_br_attn, m_w_br_pool, m_w_br_sgu, m_w_br_conv, m_w_gate, m_b_gate, m_w_o, m_ln1_g, m_ln1_b, m_w_ff_gate, m_w_ff_up, m_w_ff_down, m_ln2_g, m_ln2_b, v_c_ctx, v_w_ada, v_b_ada, v_w_in, v_q_norm_g, v_k_norm_g, v_pool_w, v_pool_scale, v_sgu_ln_g, v_sgu_ln_b, v_sgu_w, v_sgu_b, v_conv_w, v_w_br_attn, v_w_br_pool, v_w_br_sgu, v_w_br_conv, v_w_gate, v_b_gate, v_w_o, v_ln1_g, v_ln1_b, v_w_ff_gate, v_w_ff_up, v_w_ff_down, v_ln2_g, v_ln2_b):
    names = list(WEIGHT_NAMES)
    args = dict(zip(
        ["x", "c", "ctx"] + names + ["loss_target"] + ["m_" + n for n in names] + ["v_" + n for n in names],
        (x, c, ctx, c_ctx, w_ada, b_ada, w_in, q_norm_g, k_norm_g, pool_w, pool_scale, sgu_ln_g, sgu_ln_b, sgu_w, sgu_b, conv_w, w_br_attn, w_br_pool, w_br_sgu, w_br_conv, w_gate, b_gate, w_o, ln1_g, ln1_b, w_ff_gate, w_ff_up, w_ff_down, ln2_g, ln2_b, loss_target, m_c_ctx, m_w_ada, m_b_ada, m_w_in, m_q_norm_g, m_k_norm_g, m_pool_w, m_pool_scale, m_sgu_ln_g, m_sgu_ln_b, m_sgu_w, m_sgu_b, m_conv_w, m_w_br_attn, m_w_br_pool, m_w_br_sgu, m_w_br_conv, m_w_gate, m_b_gate, m_w_o, m_ln1_g, m_ln1_b, m_w_ff_gate, m_w_ff_up, m_w_ff_down, m_ln2_g, m_ln2_b, v_c_ctx, v_w_ada, v_b_ada, v_w_in, v_q_norm_g, v_k_norm_g, v_pool_w, v_pool_scale, v_sgu_ln_g, v_sgu_ln_b, v_sgu_w, v_sgu_b, v_conv_w, v_w_br_attn, v_w_br_pool, v_w_br_sgu, v_w_br_conv, v_w_gate, v_b_gate, v_w_o, v_ln1_g, v_ln1_b, v_w_ff_gate, v_w_ff_up, v_w_ff_down, v_ln2_g, v_ln2_b)))
    return _train_step(args)
```

```python
import functools
import math

import jax
import jax.numpy as jnp
from jax import lax
from jax.experimental import pallas as pl
from jax.experimental.pallas import tpu as pltpu

F32 = jnp.float32
BF16 = jnp.bfloat16

N_DEV = 8
MESH_AXES = ("x", "y", "c")
V7X_VMEM_LIMIT_BYTES = 56 * 1024 * 1024

GRID_W = 64
HEAD_DIM = 128
N_HEADS = 8
N_KV_HEADS = 2
KV_GROUP = N_HEADS // N_KV_HEADS
Q_W = N_HEADS * HEAD_DIM
KV_W = N_KV_HEADS * HEAD_DIM
ROPE_THETA = 10000.0
ROPE_AXIS_DIM = HEAD_DIM // 2
POOL_WINDOWS = (2, 4, 8, 16)
GC = 128
N_GROUPS = 4
BR_W = N_GROUPS * GC
SGU_CHUNK = 128
N_BRANCH = 4
LN_EPS = 1e-5
RMS_EPS = 1e-6
OFF_K = Q_W
OFF_V = OFF_K + KV_W
OFF_POOL = OFF_V + KV_W
OFF_U = OFF_POOL + BR_W
OFF_VG = OFF_U + BR_W
OFF_CB = OFF_VG + BR_W
OFF_CC = OFF_CB + BR_W
OFF_CX = OFF_CC + BR_W
IN_W = OFF_CX + BR_W
ATT_SCALE = HEAD_DIM ** -0.5

ADAM_LR = 0.001
ADAM_B1 = 0.9
ADAM_B2 = 0.999
ADAM_EPS = 1e-08
ADAM_WD = 0.01
ADAM_STEP = 10

_NT = (((1,), (1,)), ((), ()))
_NN = (((1,), (0,)), ((), ()))
_TN = (((0,), (0,)), ((), ()))
_DIMS = {"nt": _NT, "nn": _NN, "tn": _TN}


def _cparams(*sem):
    return pltpu.CompilerParams(dimension_semantics=sem, vmem_limit_bytes=V7X_VMEM_LIMIT_BYTES)


def _tile(dim, pref):
    best = None
    t = 128
    while t <= min(dim, pref):
        if dim % t == 0:
            best = t
        t += 128
    return best if best is not None else dim


def _dot(a, b, dims):
    return lax.dot_general(a.astype(BF16), b.astype(BF16), dims, preferred_element_type=F32)


class _Rider:
    def __init__(self, inputs, out_shapes, scratch, start, finish, aliases=None):
        self.inputs, self.out_shapes, self.scratch = list(inputs), list(out_shapes), list(scratch)
        self.start, self.finish = start, finish
        self.aliases = dict(aliases or {})


def _compose(riders):
    inputs, outs, scratch, aliases, spans = [], [], [], {}, []
    for rd in riders:
        i0, o0, s0 = len(inputs), len(outs), len(scratch)
        aliases.update({i0 + p: o0 + q for p, q in rd.aliases.items()})
        inputs += rd.inputs
        outs += rd.out_shapes
        scratch += rd.scratch
        spans.append((slice(i0, len(inputs)), slice(o0, len(outs)), slice(s0, len(scratch))))

    def start(ins, os, sems):
        for rd, (si, so, ss) in zip(riders, spans):
            rd.start(ins[si], os[so], sems[ss])

    def finish(ins, os, sems):
        for rd, (si, so, ss) in zip(riders, spans):
            rd.finish(ins[si], os[so], sems[ss])

    return _Rider(inputs, outs, scratch, start, finish, aliases)


def _pcall(name, body, grid, in_specs, out_specs, out_shape, args, sem, scratch=(), rider=None, prefetch=None,
           aliases=None):
    in_specs, out_specs, out_shape, scratch = list(in_specs), list(out_specs), list(out_shape), list(scratch)
    n_pre = 0 if prefetch is None else 1
    n_in, n_out, n_scr = len(in_specs), len(out_specs), len(scratch)
    r_in, r_out = (len(rider.inputs), len(rider.out_shapes)) if rider is not None else (0, 0)
    any_spec = pl.BlockSpec(memory_space=pl.ANY)
    io_aliases = {n_pre + p: q for p, q in (aliases or {}).items()}
    if rider is not None:
        io_aliases.update({n_pre + n_in + p: n_out + q for p, q in rider.aliases.items()})
        in_specs, out_specs = in_specs + [any_spec] * r_in, out_specs + [any_spec] * r_out
        out_shape, scratch = out_shape + rider.out_shapes, scratch + rider.scratch
        args, sem = (*args, *rider.inputs), ["arbitrary"] * len(grid)

    def wrapped(*refs):
        pre, refs = refs[:n_pre], refs[n_pre:]
        if rider is None:
            return body(*pre, *refs)
        ins, refs = refs[:n_in], refs[n_in:]
        r_ins, refs = refs[:r_in], refs[r_in:]
        outs, refs = refs[:n_out], refs[n_out:]
        r_outs, refs = refs[:r_out], refs[r_out:]
        scr, r_scr = refs[:n_scr], refs[n_scr:]
        first = functools.reduce(jnp.logical_and, [pl.program_id(ax) == 0 for ax in range(len(grid))])
        last = functools.reduce(jnp.logical_and, [pl.program_id(ax) == grid[ax] - 1 for ax in range(len(grid))])

        @pl.when(first)
        def _():
            rider.start(r_ins, r_outs, r_scr)

        body(*pre, *ins, *outs, *scr)

        @pl.when(last)
        def _():
            rider.finish(r_ins, r_outs, r_scr)

    if prefetch is None:
        res = pl.pallas_call(wrapped, name=name, grid=grid, in_specs=in_specs, out_specs=out_specs, out_shape=out_shape,
                             scratch_shapes=scratch, input_output_aliases=io_aliases,
                             compiler_params=_cparams(*sem))(*args)
    else:
        grid_spec = pltpu.PrefetchScalarGridSpec(num_scalar_prefetch=1, grid=grid, in_specs=in_specs,
                                                 out_specs=out_specs, scratch_shapes=scratch)
        res = pl.pallas_call(wrapped, name=name, grid_spec=grid_spec, out_shape=out_shape,
                             input_output_aliases=io_aliases, compiler_params=_cparams(*sem))(prefetch, *args)
    return list(res[:n_out]), list(res[n_out:])


V7X_MM_VMEM_BUDGET = 40 * 1024 * 1024


def _mm_plan(form, m, n, k, a_size, b_size, o_size, has_acc):
    tk = k if k <= 2816 else _tile(k, 2816)
    nk = k // tk
    tn = n if (form == "tn" and n <= 2048) else _tile(n, 512)
    for tm in sorted({m} | {t for t in range(128, m, 128) if m % t == 0}, reverse=True):
        need = 2 * (tm * tk * a_size + tn * tk * b_size + tm * tn * o_size) + tm * tn * 4 * (2 if nk > 1 else 1)
        need += 2 * tm * tn * 4 if has_acc else 0
        if need <= V7X_MM_VMEM_BUDGET:
            return tm, tn, tk
    return _tile(m, 128), tn, tk


def _mm(name, a, b, form, out_dtype, acc=None, rider=None):
    if form == "nt":
        (m, k), (n, k2) = a.shape, b.shape
    elif form == "nn":
        (m, k), (k2, n) = a.shape, b.shape
    else:
        (k, m), (k2, n) = a.shape, b.shape
    assert k == k2, (name, a.shape, b.shape)
    has_acc = acc is not None
    tm, tn, tk = _mm_plan(form, m, n, k, a.dtype.itemsize, b.dtype.itemsize, jnp.dtype(out_dtype).itemsize, has_acc)
    nk = k // tk
    a_spec = {"nt": pl.BlockSpec((tm, tk), lambda i, j, kk: (i, kk)),
              "nn": pl.BlockSpec((tm, tk), lambda i, j, kk: (i, kk)),
              "tn": pl.BlockSpec((tk, tm), lambda i, j, kk: (kk, i))}[form]
    b_spec = {"nt": pl.BlockSpec((tn, tk), lambda i, j, kk: (j, kk)),
              "nn": pl.BlockSpec((tk, tn), lambda i, j, kk: (kk, j)),
              "tn": pl.BlockSpec((tk, tn), lambda i, j, kk: (kk, j))}[form]
    o_spec = pl.BlockSpec((tm, tn), lambda i, j, kk: (i, j))
    dims = _DIMS[form]

    def body(*refs):
        a_ref, b_ref = refs[0], refs[1]
        c_ref = refs[2] if has_acc else None
        o_ref = refs[3] if has_acc else refs[2]

        def finish(r):
            if has_acc:
                r = r + c_ref[...]
            o_ref[...] = r.astype(o_ref.dtype)

        if nk == 1:
            finish(_dot(a_ref[...], b_ref[...], dims))
            return
        acc_ref = refs[-1]
        kk = pl.program_id(2)

        @pl.when(kk == 0)
        def _():
            acc_ref[...] = _dot(a_ref[...], b_ref[...], dims)

        @pl.when(kk > 0)
        def _():
            acc_ref[...] += _dot(a_ref[...], b_ref[...], dims)

        @pl.when(kk == nk - 1)
        def _():
            finish(acc_ref[...])

    in_specs = [a_spec, b_spec] + ([o_spec] if has_acc else [])
    args = (a, b) + ((acc,) if has_acc else ())
    outs, r_outs = _pcall(name, body, (m // tm, n // tn, nk), in_specs, [o_spec],
                          [jax.ShapeDtypeStruct((m, n), out_dtype)], args, ("parallel", "parallel", "arbitrary"),
                          scratch=[pltpu.VMEM((tm, tn), F32)] if nk > 1 else [], rider=rider)
    return outs[0] if rider is None else (outs[0], r_outs)


def _mm_fused(name, a, bs, epilogue, tile_ins, out_dtypes, rider=None):
    (m, k), (n, _) = a.shape, bs[0].shape
    tn = _tile(n, 512)
    tm = None
    for cand in sorted({m} | {t for t in range(128, m, 128) if m % t == 0}, reverse=True):
        per_tile = sum(t.dtype.itemsize for t in tile_ins) + sum(jnp.dtype(d).itemsize for d in out_dtypes)
        need = 2 * (cand * k * a.dtype.itemsize + len(bs) * tn * k * bs[0].dtype.itemsize + cand * tn * per_tile)
        need += (len(bs) + 2) * cand * tn * 4
        if need <= V7X_MM_VMEM_BUDGET:
            tm = cand
            break
    assert tm is not None, name
    n_b, n_t = len(bs), len(tile_ins)
    tile = pl.BlockSpec((tm, tn), lambda i, j: (i, j))

    def body(a_ref, *refs):
        prods = [_dot(a_ref[...], r[...], _NT) for r in refs[:n_b]]
        outs = epilogue(prods, [r[...].astype(F32) for r in refs[n_b:n_b + n_t]])
        for r, o in zip(refs[n_b + n_t:], outs):
            r[...] = o.astype(r.dtype)

    outs, r_outs = _pcall(name, body, (m // tm, n // tn),
                          [pl.BlockSpec((tm, k), lambda i, j: (i, 0))] + [pl.BlockSpec((tn, k), lambda i, j: (j, 0))] * n_b
                          + [tile] * n_t, [tile] * len(out_dtypes),
                          [jax.ShapeDtypeStruct((m, n), d) for d in out_dtypes], (a, *bs, *tile_ins),
                          ("parallel", "parallel"), rider=rider)
    return outs if rider is None else (outs, r_outs)


def _rows(name, fn, n_rows, tm, nbc, row_ins, type_ins, row_outs, acc_outs, rider=None):
    n_ri, n_ti, n_ro, n_ao = len(row_ins), len(type_ins), len(row_outs), len(acc_outs)

    def row_map(i, cb, roff):
        return (jnp.maximum(i - roff, 0), cb)

    def type_map(i):
        return (jnp.where(i >= nbc, 1, 0), 0, 0)

    in_specs, args = [], []
    for arr, cb, width, roff in row_ins:
        in_specs.append(pl.BlockSpec((tm, width), functools.partial(row_map, cb=cb, roff=roff)))
        args.append(arr)
    for arr in type_ins:
        in_specs.append(pl.BlockSpec((None, 1, arr.shape[-1]), type_map))
        args.append(arr)
    out_shape, out_specs = [], []
    for total, width, dtype, roff in row_outs:
        out_shape.append(jax.ShapeDtypeStruct((total, width), dtype))
        out_specs.append(pl.BlockSpec((tm, width), functools.partial(row_map, cb=0, roff=roff)))
    for width in acc_outs:
        out_shape.append(jax.ShapeDtypeStruct((2, 1, width), F32))
        out_specs.append(pl.BlockSpec((None, 1, width), type_map))
    n_in = n_ri + n_ti

    def body(*refs):
        i = pl.program_id(0)
        outs = fn(*[r[...].astype(F32) for r in refs[:n_in]])
        if not isinstance(outs, (tuple, list)):
            outs = (outs,)
        assert len(outs) == n_ro + n_ao, (name, len(outs))
        for r, o in zip(refs[n_in:n_in + n_ro], outs[:n_ro]):
            r[...] = o.astype(r.dtype)
        if n_ao:
            first = jnp.logical_or(i == 0, i == nbc)
            for r, o in zip(refs[n_in + n_ro:], outs[n_ro:]):
                o = jnp.broadcast_to(o.astype(F32), r.shape)

                @pl.when(first)
                def _(r=r, o=o):
                    r[...] = o

                @pl.when(jnp.logical_not(first))
                def _(r=r, o=o):
                    r[...] += o

    outs, r_outs = _pcall(name, body, (n_rows // tm,), in_specs, out_specs, out_shape, args, ("arbitrary",),
                          rider=rider)
    return outs if rider is None else (outs, r_outs)


def _vjp_fn(f, n_row, n_cot, keep=None):
    def g(*args):
        prim = args[:n_row] + args[n_row + n_cot:]
        cots = args[n_row:n_row + n_cot]
        out, vjp = jax.vjp(f, *prim)
        grads = vjp(tuple(cots) if isinstance(out, (tuple, list)) else cots[0])
        return grads if keep is None else tuple(grads[j] for j in keep)
    return g


def _typed(v):
    v = v.reshape(1, 1, -1)
    return jnp.concatenate([v, v], axis=0)


def _ln(x, g, b):
    mu = jnp.mean(x, axis=-1, keepdims=True)
    var = jnp.mean(jnp.square(x - mu), axis=-1, keepdims=True)
    return (x - mu) * lax.rsqrt(var + LN_EPS) * g + b


def _f_mod(x, sc, sh):
    return x * (1.0 + sc) + sh


def _make_f_ln(alpha, with_mod):
    def f(x, o, gate, lng, lnb, *mod):
        xn = _ln(alpha * x + gate * o, lng, lnb)
        if with_mod:
            sc, sh = mod
            return xn, xn * (1.0 + sc) + sh
        return xn
    return f


@jax.custom_vjp
def _rot(y):
    lane = lax.broadcasted_iota(jnp.int32, y.shape, 1)
    return jnp.where(lane % 64 < 32, pltpu.roll(y, 96, axis=1), pltpu.roll(y, 32, axis=1))


_rot.defvjp(lambda y: (_rot(y), None), lambda _, g: (_rot(g),))


def _f_prep(p, cos, sin, qg, kg):
    def head(xh, g):
        ms = jnp.mean(jnp.square(xh), axis=-1, keepdims=True)
        y = xh * lax.rsqrt(ms + RMS_EPS) * g
        return y * cos + _rot(y) * sin
    q = jnp.concatenate([head(p[:, h * HEAD_DIM:(h + 1) * HEAD_DIM], qg) for h in range(N_HEADS)], axis=1)
    k = jnp.concatenate([head(p[:, OFF_K + h * HEAD_DIM:OFF_K + (h + 1) * HEAD_DIM], kg)
                         for h in range(N_KV_HEADS)], axis=1)
    return q, k, p[:, OFF_V:OFF_POOL]


def _f_gate(g, t0, t1, t2, t3, b):
    d = t0.shape[-1]
    ts = (t0, t1, t2, t3)
    terms = [jax.nn.sigmoid(g[:, k * d:(k + 1) * d] + b[:, k * d:(k + 1) * d]) * ts[k] for k in range(N_BRANCH)]
    return terms[0] + terms[1] + terms[2] + terms[3]


def _f_swiglu(a, b):
    return jax.nn.silu(a) * b


def _softmax(raw):
    e = jnp.exp2((raw - jnp.max(raw, axis=-1, keepdims=True)) * (ATT_SCALE * math.log2(math.e)))
    return e / jnp.sum(e, axis=-1, keepdims=True)


def _attn_fwd(name, q, k, v, rc, ctx_queries, tq=256, rider=None):
    r = q.shape[0]
    assert rc % tq == 0 and r % tq == 0
    nqc = rc // tq

    def body(q_ref, k_ref, v_ref, o_ref):
        qi = pl.program_id(1)

        def attend(nk):
            raw = _dot(q_ref[...], k_ref[0:nk, :], _NT)
            e = jnp.exp2((raw - jnp.max(raw, axis=-1, keepdims=True)) * (ATT_SCALE * math.log2(math.e)))
            o = _dot(e, v_ref[0:nk, :], _NN) / jnp.sum(e, axis=-1, keepdims=True)
            o_ref[...] = o.astype(o_ref.dtype)

        @pl.when(qi < nqc)
        def _():
            if ctx_queries:
                attend(rc)
            else:
                o_ref[...] = jnp.zeros_like(o_ref)

        @pl.when(qi >= nqc)
        def _():
            attend(r)

    outs, r_outs = _pcall(
        name, body, (N_HEADS, r // tq),
        [pl.BlockSpec((tq, HEAD_DIM), lambda h, i: (i, h)),
         pl.BlockSpec((r, HEAD_DIM), lambda h, i: (0, h // KV_GROUP)),
         pl.BlockSpec((r, HEAD_DIM), lambda h, i: (0, h // KV_GROUP))],
        [pl.BlockSpec((tq, HEAD_DIM), lambda h, i: (i, h))],
        [jax.ShapeDtypeStruct((r, Q_W), BF16)], (q, k, v), ("parallel", "parallel"), rider=rider)
    return outs[0] if rider is None else (outs[0], r_outs)


def _attn_bwd(name, q, k, v, do, rc, ctx_queries, tq=256, rider=None):
    r = q.shape[0]
    nqc = rc // tq

    def body(q_ref, k_ref, v_ref, do_ref, dq_ref, dk_ref, dv_ref):
        g, qi = pl.program_id(1), pl.program_id(2)

        @pl.when(jnp.logical_and(g == 0, qi == 0))
        def _():
            dk_ref[...] = jnp.zeros_like(dk_ref)
            dv_ref[...] = jnp.zeros_like(dv_ref)

        def grad(nk):
            qb, kb, vb = q_ref[...], k_ref[0:nk, :], v_ref[0:nk, :]
            dob = do_ref[...].astype(BF16)
            p = _softmax(_dot(qb, kb, _NT))
            dv_ref[0:nk, :] += _dot(p, dob, _TN)
            dp = _dot(dob, vb, _NT)
            ds = p * (dp - jnp.sum(dp * p, axis=-1, keepdims=True)) * ATT_SCALE
            dq_ref[...] = _dot(ds, kb, _NN)
            dk_ref[0:nk, :] += _dot(ds, qb, _TN)

        @pl.when(qi < nqc)
        def _():
            if ctx_queries:
                grad(rc)
            else:
                dq_ref[...] = jnp.zeros_like(dq_ref)

        @pl.when(qi >= nqc)
        def _():
            grad(r)

    def qmap(kv, g, i):
        return (i, kv * KV_GROUP + g)

    def kvmap(kv, g, i):
        return (0, kv)

    outs, r_outs = _pcall(
        name, body, (N_KV_HEADS, KV_GROUP, r // tq),
        [pl.BlockSpec((tq, HEAD_DIM), qmap), pl.BlockSpec((r, HEAD_DIM), kvmap),
         pl.BlockSpec((r, HEAD_DIM), kvmap), pl.BlockSpec((tq, HEAD_DIM), qmap)],
        [pl.BlockSpec((tq, HEAD_DIM), qmap), pl.BlockSpec((r, HEAD_DIM), kvmap), pl.BlockSpec((r, HEAD_DIM), kvmap)],
        [jax.ShapeDtypeStruct((r, Q_W), F32), jax.ShapeDtypeStruct((r, KV_W), F32),
         jax.ShapeDtypeStruct((r, KV_W), F32)],
        (q, k, v, do), ("arbitrary", "arbitrary", "arbitrary"), rider=rider)
    return outs if rider is None else (outs, r_outs)


def _segments(shape, rc):
    t = lax.broadcasted_iota(jnp.int32, shape, 0)
    lo = jnp.where(t < rc, 0, rc)
    hi = jnp.where(t < rc, rc, shape[0])
    return t, lo, hi


def _shifted(x, o, t, lo, hi):
    n = x.shape[0]
    sh = pltpu.roll(x, (-o) % n, axis=0)
    return jnp.where(jnp.logical_and(t + o >= lo, t + o < hi), sh, 0.0)


def _winsum(x, left, right, t, lo, hi):
    acc = x
    for o in range(-left, right + 1):
        if o != 0:
            acc = acc + _shifted(x, o, t, lo, hi)
    return acc


def _pool_parts(z, g, t, lo, hi):
    w = POOL_WINDOWS[g]
    left = w // 2
    right = w - 1 - left
    count = (jnp.minimum(t + right + 1, hi) - jnp.maximum(t - left, lo)).astype(F32)
    return _winsum(z, left, right, t, lo, hi) / count - z, count, left, right


def _pool_fwd(name, p, pool_w, pool_scale, rc):
    r = p.shape[0]

    def body(z_ref, w_ref, s_ref, y_ref):
        t, lo, hi = _segments((r, GC), rc)
        for g in range(N_GROUPS):
            cols = slice(g * GC, (g + 1) * GC)
            d, _, _, _ = _pool_parts(z_ref[:, cols], g, t, lo, hi)
            y_ref[:, cols] = (_dot(d, w_ref[g], _NN) * s_ref[:, cols]).astype(y_ref.dtype)

    return pl.pallas_call(
        body, name=name, grid=(1,),
        in_specs=[pl.BlockSpec((r, BR_W), lambda i: (0, OFF_POOL // BR_W)),
                  pl.BlockSpec((N_GROUPS, GC, GC), lambda i: (0, 0, 0)),
                  pl.BlockSpec((1, BR_W), lambda i: (0, 0))],
        out_specs=pl.BlockSpec((r, BR_W), lambda i: (0, 0)),
        out_shape=jax.ShapeDtypeStruct((r, BR_W), BF16),
        compiler_params=_cparams("arbitrary"),
    )(p, pool_w, pool_scale.reshape(1, BR_W))


def _pool_bwd(name, p, pool_w, pool_scale, dy, rc):
    r = p.shape[0]

    def body(z_ref, w_ref, s_ref, dy_ref, dz_ref, dw_ref, ds_ref):
        t, lo, hi = _segments((r, GC), rc)
        for g in range(N_GROUPS):
            cols = slice(g * GC, (g + 1) * GC)
            d, count, left, right = _pool_parts(z_ref[:, cols], g, t, lo, hi)
            dyg = dy_ref[:, cols]
            ds_ref[:, cols] = jnp.sum(dyg * _dot(d, w_ref[g], _NN), axis=0, keepdims=True)
            dlin = dyg * s_ref[:, cols]
            dw_ref[g] = _dot(d, dlin, _TN)
            dd = _dot(dlin, w_ref[g], _NT)
            dz_ref[:, cols] = (_winsum(dd / count, right, left, t, lo, hi) - dd).astype(dz_ref.dtype)

    return pl.pallas_call(
        body, name=name, grid=(1,),
        in_specs=[pl.BlockSpec((r, BR_W), lambda i: (0, OFF_POOL // BR_W)),
                  pl.BlockSpec((N_GROUPS, GC, GC), lambda i: (0, 0, 0)),
                  pl.BlockSpec((1, BR_W), lambda i: (0, 0)),
                  pl.BlockSpec((r, BR_W), lambda i: (0, 0))],
        out_specs=[pl.BlockSpec((r, BR_W), lambda i: (0, 0)),
                   pl.BlockSpec((N_GROUPS, GC, GC), lambda i: (0, 0, 0)),
                   pl.BlockSpec((1, BR_W), lambda i: (0, 0))],
        out_shape=[jax.ShapeDtypeStruct((r, BR_W), BF16), jax.ShapeDtypeStruct((N_GROUPS, GC, GC), F32),
                   jax.ShapeDtypeStruct((1, BR_W), F32)],
        compiler_params=_cparams("arbitrary"),
    )(p, pool_w, pool_scale.reshape(1, BR_W), dy)


def _f_sgu_v(pvg, lng, lnb):
    return _ln(jax.nn.gelu(pvg), lng, lnb)


def _sgu_fwd(name, p, ln_g, ln_b, sgu_w, sgu_b):
    r = p.shape[0]

    def body(pu_ref, pv_ref, g_ref, b_ref, w_ref, sb_ref, y_ref):
        vn = _f_sgu_v(pv_ref[...], g_ref[...], b_ref[...])
        u = jax.nn.gelu(pu_ref[...])
        for g in range(N_GROUPS):
            cols = slice(g * GC, (g + 1) * GC)
            s = _dot(w_ref[g], vn[:, cols], _NN) + sb_ref[g]
            y_ref[:, cols] = (u[:, cols] * s).astype(y_ref.dtype)

    return pl.pallas_call(
        body, name=name, grid=(r // SGU_CHUNK,),
        in_specs=[pl.BlockSpec((SGU_CHUNK, BR_W), lambda i: (i, OFF_U // BR_W)),
                  pl.BlockSpec((SGU_CHUNK, BR_W), lambda i: (i, OFF_VG // BR_W)),
                  pl.BlockSpec((1, BR_W), lambda i: (0, 0)), pl.BlockSpec((1, BR_W), lambda i: (0, 0)),
                  pl.BlockSpec((N_GROUPS, GC, GC), lambda i: (0, 0, 0)),
                  pl.BlockSpec((N_GROUPS, SGU_CHUNK, 1), lambda i: (0, 0, 0))],
        out_specs=pl.BlockSpec((SGU_CHUNK, BR_W), lambda i: (i, 0)),
        out_shape=jax.ShapeDtypeStruct((r, BR_W), BF16),
        compiler_params=_cparams("parallel"),
    )(p, p, ln_g.reshape(1, BR_W), ln_b.reshape(1, BR_W), sgu_w, sgu_b.reshape(N_GROUPS, SGU_CHUNK, 1))


def _sgu_bwd(name, p, ln_g, ln_b, sgu_w, sgu_b, dy):
    r = p.shape[0]

    def body(pu_ref, pv_ref, g_ref, b_ref, w_ref, sb_ref, dy_ref, dp_ref, dg_ref, db_ref, dw_ref, dsb_ref):
        i = pl.program_id(0)

        @pl.when(i == 0)
        def _():
            for ref in (dg_ref, db_ref, dw_ref, dsb_ref):
                ref[...] = jnp.zeros_like(ref)

        vn, vjp_v = jax.vjp(_f_sgu_v, pv_ref[...], g_ref[...], b_ref[...])
        u, vjp_u = jax.vjp(jax.nn.gelu, pu_ref[...])
        dy = dy_ref[...]
        du, dvn = [], []
        for g in range(N_GROUPS):
            cols = slice(g * GC, (g + 1) * GC)
            s = _dot(w_ref[g], vn[:, cols], _NN) + sb_ref[g]
            du.append(dy[:, cols] * s)
            ds = dy[:, cols] * u[:, cols]
            dsb_ref[g] += jnp.sum(ds, axis=1, keepdims=True)
            dw_ref[g] += _dot(ds, vn[:, cols], _NT)
            dvn.append(_dot(w_ref[g], ds, _TN))
        (dpu,) = vjp_u(jnp.concatenate(du, axis=1))
        dpv, dg, db = vjp_v(jnp.concatenate(dvn, axis=1))
        dp_ref[:, 0:BR_W] = dpu.astype(dp_ref.dtype)
        dp_ref[:, BR_W:2 * BR_W] = dpv.astype(dp_ref.dtype)
        dg_ref[...] += dg
        db_ref[...] += db

    vec = pl.BlockSpec((1, BR_W), lambda i: (0, 0))
    wsp = pl.BlockSpec((N_GROUPS, GC, GC), lambda i: (0, 0, 0))
    bsp = pl.BlockSpec((N_GROUPS, SGU_CHUNK, 1), lambda i: (0, 0, 0))
    return pl.pallas_call(
        body, name=name, grid=(r // SGU_CHUNK,),
        in_specs=[pl.BlockSpec((SGU_CHUNK, BR_W), lambda i: (i, OFF_U // BR_W)),
                  pl.BlockSpec((SGU_CHUNK, BR_W), lambda i: (i, OFF_VG // BR_W)),
                  vec, vec, wsp, bsp, pl.BlockSpec((SGU_CHUNK, BR_W), lambda i: (i, 0))],
        out_specs=[pl.BlockSpec((SGU_CHUNK, 2 * BR_W), lambda i: (i, 0)), vec, vec, wsp, bsp],
        out_shape=[jax.ShapeDtypeStruct((r, 2 * BR_W), BF16), jax.ShapeDtypeStruct((1, BR_W), F32),
                   jax.ShapeDtypeStruct((1, BR_W), F32), jax.ShapeDtypeStruct((N_GROUPS, GC, GC), F32),
                   jax.ShapeDtypeStruct((N_GROUPS, SGU_CHUNK, 1), F32)],
        compiler_params=_cparams("arbitrary"),
    )(p, p, ln_g.reshape(1, BR_W), ln_b.reshape(1, BR_W), sgu_w, sgu_b.reshape(N_GROUPS, SGU_CHUNK, 1), dy)


def _conv_w8(conv_w):
    return jnp.concatenate([conv_w, jnp.zeros((8 - conv_w.shape[0], conv_w.shape[1]), F32)], axis=0)


def _conv_fwd(name, p, conv_w, rc):
    r = p.shape[0]

    def body(cb_ref, cc_ref, cx_ref, w_ref, y_ref):
        t, lo, hi = _segments((r, GC), rc)
        z = cc_ref[...] * cx_ref[...]
        w = w_ref[...]
        c = _shifted(z, -1, t, lo, hi) * w[0:1] + z * w[1:2] + _shifted(z, 1, t, lo, hi) * w[2:3]
        y_ref[...] = (cb_ref[...] * c).astype(y_ref.dtype)

    nb = OFF_CB // GC
    return pl.pallas_call(
        body, name=name, grid=(N_GROUPS,),
        in_specs=[pl.BlockSpec((r, GC), lambda j: (0, nb + j)),
                  pl.BlockSpec((r, GC), lambda j: (0, nb + N_GROUPS + j)),
                  pl.BlockSpec((r, GC), lambda j: (0, nb + 2 * N_GROUPS + j)),
                  pl.BlockSpec((8, GC), lambda j: (0, j))],
        out_specs=pl.BlockSpec((r, GC), lambda j: (0, j)),
        out_shape=jax.ShapeDtypeStruct((r, BR_W), BF16),
        compiler_params=_cparams("parallel"),
    )(p, p, p, _conv_w8(conv_w))


def _conv_bwd(name, p, conv_w, dy, rc):
    r = p.shape[0]

    def body(cb_ref, cc_ref, cx_ref, w_ref, dy_ref, dcb_ref, dcc_ref, dcx_ref, dw_ref):
        t, lo, hi = _segments((r, GC), rc)
        cc, cx, w, dy = cc_ref[...], cx_ref[...], w_ref[...], dy_ref[...]
        z = cc * cx
        zp, zn = _shifted(z, -1, t, lo, hi), _shifted(z, 1, t, lo, hi)
        dcb_ref[...] = (dy * (zp * w[0:1] + z * w[1:2] + zn * w[2:3])).astype(dcb_ref.dtype)
        dc = dy * cb_ref[...]
        dw_ref[...] = jnp.concatenate(
            [jnp.sum(dc * zp, axis=0, keepdims=True), jnp.sum(dc * z, axis=0, keepdims=True),
             jnp.sum(dc * zn, axis=0, keepdims=True), jnp.zeros((5, GC), F32)], axis=0)
        dz = dc * w[1:2] + _shifted(dc, 1, t, lo, hi) * w[0:1] + _shifted(dc, -1, t, lo, hi) * w[2:3]
        dcc_ref[...] = (dz * cx).astype(dcc_ref.dtype)
        dcx_ref[...] = (dz * cc).astype(dcx_ref.dtype)

    nb = OFF_CB // GC
    return pl.pallas_call(
        body, name=name, grid=(N_GROUPS,),
        in_specs=[pl.BlockSpec((r, GC), lambda j: (0, nb + j)),
                  pl.BlockSpec((r, GC), lambda j: (0, nb + N_GROUPS + j)),
                  pl.BlockSpec((r, GC), lambda j: (0, nb + 2 * N_GROUPS + j)),
                  pl.BlockSpec((8, GC), lambda j: (0, j)),
                  pl.BlockSpec((r, GC), lambda j: (0, j))],
        out_specs=[pl.BlockSpec((r, GC), lambda j: (0, j))] * 3 + [pl.BlockSpec((8, GC), lambda j: (0, j))],
        out_shape=[jax.ShapeDtypeStruct((r, BR_W), BF16)] * 3 + [jax.ShapeDtypeStruct((8, BR_W), F32)],
        compiler_params=_cparams("parallel"),
    )(p, p, p, _conv_w8(conv_w), dy)


def _rope_tables(rc, n):
    rows = n // GRID_W
    row = jnp.repeat(jnp.arange(rows), GRID_W).astype(F32)
    col = jnp.tile(jnp.arange(GRID_W), rows).astype(F32)
    inv = ROPE_THETA ** (-jnp.arange(0, ROPE_AXIS_DIM, 2, dtype=F32) / ROPE_AXIS_DIM)
    ang_r, ang_c = row[:, None] * inv, col[:, None] * inv
    cos = jnp.concatenate([jnp.cos(ang_r), jnp.cos(ang_r), jnp.cos(ang_c), jnp.cos(ang_c)], axis=1)
    sin = jnp.concatenate([-jnp.sin(ang_r), jnp.sin(ang_r), -jnp.sin(ang_c), jnp.sin(ang_c)], axis=1)
    cos = jnp.concatenate([jnp.ones((rc, HEAD_DIM), F32), cos], axis=0)
    sin = jnp.concatenate([jnp.zeros((rc, HEAD_DIM), F32), sin], axis=0)
    return cos, sin


MOD_NAMES = ("sh1", "sc1", "g1", "sh2", "sc2", "g2")


def _local_step(xin, target, mod, comm, sp, rc, alpha):
    def carrying(fn):
        def call(name, *args, **kw):
            rider = comm.rider(name)
            if rider is None:
                return fn(name, *args, **kw)
            res, r_outs = fn(name, *args, rider=rider, **kw)
            comm.deliver(name, r_outs)
            return res
        return call

    mm, rows, attn_fwd, attn_bwd = carrying(_mm), carrying(_rows), carrying(_attn_fwd), carrying(_attn_bwd)
    mm_fused = carrying(_mm_fused)
    r, d = xin.shape
    n_layers = mod.shape[0]
    tm_n, tm_w = 256, 128
    nbc_n, nbc_w = rc // tm_n, rc // tm_w
    cos, sin = _rope_tables(rc, r - rc)
    mp = mod.reshape(n_layers, 2, 6, 1, d)
    mods = [{nm: mp[i, :, j] for j, nm in enumerate(MOD_NAMES)} for i in range(n_layers)]
    f_ln_mod, f_ln_last = _make_f_ln(alpha, True), _make_f_ln(alpha, False)

    def whole(arr, roff=0):
        return (arr, 0, arr.shape[1], roff)

    (hb,) = rows("mod_in", _f_mod, r, tm_n, nbc_n, [whole(xin)], [mods[0]["sc1"], mods[0]["sh1"]],
                 [(r, d, BF16, 0)], [])
    saved = []
    x = xin
    for i in range(n_layers):
        last = i == n_layers - 1
        w, s, m = functools.partial(comm.weight, i), sp[i], mods[i]
        sv = {"x": x, "hb": hb}
        p = mm(f"l{i}_in", hb, w("in_t"), "nt", F32)
        q, k, v = rows(f"l{i}_prep", _f_prep, r, tm_n, nbc_n,
                       [(p, 0, OFF_POOL, 0), whole(cos), whole(sin)], [_typed(s["q_norm_g"]), _typed(s["k_norm_g"])],
                       [(r, Q_W, BF16, 0), (r, KV_W, BF16, 0), (r, KV_W, BF16, 0)], [])
        ys = [attn_fwd(f"l{i}_attn", q, k, v, rc, not last),
              _pool_fwd(f"l{i}_pool", p, s["pool_w"], s["pool_scale"], rc),
              _sgu_fwd(f"l{i}_sgu", p, s["sgu_ln_g"], s["sgu_ln_b"], s["sgu_w"], s["sgu_b"]),
              _conv_fwd(f"l{i}_conv", p, s["conv_w"], rc)]
        ts = [mm(f"l{i}_br{kk}", ys[kk], w(f"br{kk}"), "nt", BF16) for kk in range(N_BRANCH)]
        gpre = mm(f"l{i}_gate", hb, w("gate_t"), "nt", BF16)
        (mg,) = rows(f"l{i}_merge", _f_gate, r, tm_w, nbc_w, [whole(gpre)] + [whole(t) for t in ts],
                     [_typed(s["b_gate"])], [(r, d, BF16, 0)], [])
        o = mm(f"l{i}_o", mg, w("o"), "nn", F32)
        x1, h2b = rows(f"l{i}_ln1", f_ln_mod, r, tm_n, nbc_n, [whole(x), whole(o)],
                       [m["g1"], _typed(s["ln1_g"]), _typed(s["ln1_b"]), m["sc2"], m["sh2"]],
                       [(r, d, F32, 0), (r, d, BF16, 0)], [])
        af, bf, f = mm_fused(f"l{i}_ffgu", h2b, [w("ffg_t"), w("ffu_t")],
                             lambda prods, _: (prods[0], prods[1], _f_swiglu(prods[0], prods[1])), [], [BF16, BF16, BF16])
        o2 = mm(f"l{i}_ffd", f, w("ffd"), "nn", F32)
        if last:
            (x2,) = rows(f"l{i}_ln2", f_ln_last, r, tm_n, nbc_n, [whole(x1), whole(o2)],
                         [m["g2"], _typed(s["ln2_g"]), _typed(s["ln2_b"])], [(r, d, F32, 0)], [])
            hb = None
        else:
            nx = mods[i + 1]
            x2, hb = rows(f"l{i}_ln2", f_ln_mod, r, tm_n, nbc_n, [whole(x1), whole(o2)],
                          [m["g2"], _typed(s["ln2_g"]), _typed(s["ln2_b"]), nx["sc1"], nx["sh1"]],
                          [(r, d, F32, 0), (r, d, BF16, 0)], [])
        sv.update(p=p, gpre=gpre, q=q, k=k, v=v, ys=ys, ts=ts, mg=mg, o=o, x1=x1, h2b=h2b, af=af, bf=bf, f=f, o2=o2)
        saved.append(sv)
        x = x2

    lat = jnp.concatenate([jnp.zeros((1, 1, 128), F32), jnp.ones((1, 1, 128), F32)], axis=0)

    def f_loss(xb, tb, msk):
        diff = (xb - tb) * msk[:, 0:1]
        part = jnp.sum(jnp.mean(jnp.square(diff), axis=-1, keepdims=True), axis=0, keepdims=True)
        return diff * (1.0 / d), jnp.broadcast_to(part, (1, 128))

    dx_direct, loss_acc = rows("loss", f_loss, r, tm_n, nbc_n, [whole(x), whole(target, nbc_n)], [lat],
                               [(r, d, F32, 0)], [128])
    loss = 0.5 * loss_acc[1, 0, 0]

    dmods = [dict() for _ in range(n_layers)]
    dsp = [dict() for _ in range(n_layers)]
    dh = None

    def small_done(j):
        ds = dict(dsp[j])
        for nm in ("ln1_g", "ln1_b", "ln2_g", "ln2_b", "b_gate", "q_norm_g", "k_norm_g"):
            ds[nm] = ds[nm][0, 0] + ds[nm][1, 0]
        for nm in ("pool_scale", "sgu_ln_g", "sgu_ln_b"):
            ds[nm] = ds[nm].reshape(-1)
        ds["sgu_b"] = ds["sgu_b"].reshape(N_GROUPS, SGU_CHUNK)
        comm.small_ready(j, jnp.concatenate([dmods[j][nm][:, 0, :] for nm in MOD_NAMES], axis=-1), ds)

    for i in reversed(range(n_layers)):
        last = i == n_layers - 1
        w, s, m, sv = functools.partial(comm.weight, i), sp[i], mods[i], saved[i]
        dm, dw, ds = dmods[i], {}, dsp[i]
        ln2 = [m["g2"], _typed(s["ln2_g"]), _typed(s["ln2_b"])]
        if last:
            res = rows(f"l{i}_ln2_bwd", _vjp_fn(f_ln_last, 2, 1), r, tm_n, nbc_n,
                       [whole(sv["x1"]), whole(sv["o2"]), whole(dx_direct)], ln2,
                       [(r, d, F32, 0), (r, d, BF16, 0)], [d, d, d])
            dx1, do2, dm["g2"], dlg, dlb = res
        else:
            nx = mods[i + 1]
            res = rows(f"l{i}_ln2_bwd", _vjp_fn(f_ln_mod, 2, 2), r, tm_n, nbc_n,
                       [whole(sv["x1"]), whole(sv["o2"]), whole(dx_direct), whole(dh)],
                       ln2 + [nx["sc1"], nx["sh1"]],
                       [(r, d, F32, 0), (r, d, BF16, 0)], [d, d, d, d, d])
            dx1, do2, dm["g2"], dlg, dlb, dmods[i + 1]["sc1"], dmods[i + 1]["sh1"] = res
            small_done(i + 1)
        ds["ln2_g"], ds["ln2_b"] = dlg, dlb
        dab, dbb = mm_fused(f"l{i}_dF", do2, [w("ffd")],
                            lambda prods, tiles: jax.vjp(_f_swiglu, *tiles)[1](prods[0]), [sv["af"], sv["bf"]],
                            [BF16, BF16])
        comm.grads(i, {"ffd": mm(f"l{i}_dWffd", sv["f"], do2, "tn", BF16)})
        comm.grads(i, {"ffg_t": mm(f"l{i}_dWffg", dab, sv["h2b"], "tn", BF16)})
        comm.grads(i, {"ffu_t": mm(f"l{i}_dWffu", dbb, sv["h2b"], "tn", BF16)})
        dh2 = mm(f"l{i}_dh2a", dab, w("ffg_t"), "nn", F32)
        dh2 = mm(f"l{i}_dh2b", dbb, w("ffu_t"), "nn", F32, acc=dh2)
        res = rows(f"l{i}_ln1_bwd", _vjp_fn(f_ln_mod, 2, 2), r, tm_n, nbc_n,
                   [whole(sv["x"]), whole(sv["o"]), whole(dx1), whole(dh2)],
                   [m["g1"], _typed(s["ln1_g"]), _typed(s["ln1_b"]), m["sc2"], m["sh2"]],
                   [(r, d, F32, 0), (r, d, BF16, 0)], [d, d, d, d, d])
        dx_direct, do, dm["g1"], ds["ln1_g"], ds["ln1_b"], dm["sc2"], dm["sh2"] = res
        dmg = mm(f"l{i}_dMg", do, w("o"), "nt", F32)
        comm.grads(i, {"o": mm(f"l{i}_dWo", sv["mg"], do, "tn", BF16)})
        res = rows(f"l{i}_merge_bwd", _vjp_fn(_f_gate, 5, 1), r, tm_w, nbc_w,
                   [whole(sv["gpre"])] + [whole(t) for t in sv["ts"]] + [whole(dmg)], [_typed(s["b_gate"])],
                   [(r, N_BRANCH * d, BF16, 0)] + [(r, d, BF16, 0)] * N_BRANCH, [N_BRANCH * d])
        dgb, dts, ds["b_gate"] = res[0], res[1:1 + N_BRANCH], res[1 + N_BRANCH]
        comm.grads(i, {"gate_t": mm(f"l{i}_dWgate", dgb, sv["hb"], "tn", BF16)})
        dys = [mm(f"l{i}_dY{kk}", dts[kk], w(f"br{kk}"), "nn", F32) for kk in range(N_BRANCH)]
        for kk in range(N_BRANCH):
            comm.grads(i, {f"br{kk}": mm(f"l{i}_dWbr{kk}", dts[kk], sv["ys"][kk], "tn", BF16)})
        dq, dk, dv = attn_bwd(f"l{i}_attn_bwd", sv["q"], sv["k"], sv["v"], dys[0], rc, not last)
        res = rows(f"l{i}_prep_bwd", _vjp_fn(_f_prep, 3, 3, keep=(0, 3, 4)), r, tm_n, nbc_n,
                   [(sv["p"], 0, OFF_POOL, 0), whole(cos), whole(sin), whole(dq), whole(dk), whole(dv)],
                   [_typed(s["q_norm_g"]), _typed(s["k_norm_g"])],
                   [(r, OFF_POOL, BF16, 0)], [HEAD_DIM, HEAD_DIM])
        dp_qkv, ds["q_norm_g"], ds["k_norm_g"] = res
        dp_pool, ds["pool_w"], ds["pool_scale"] = _pool_bwd(f"l{i}_pool_bwd", sv["p"], s["pool_w"], s["pool_scale"],
                                                            dys[1], rc)
        dp_sgu, ds["sgu_ln_g"], ds["sgu_ln_b"], ds["sgu_w"], ds["sgu_b"] = _sgu_bwd(
            f"l{i}_sgu_bwd", sv["p"], s["sgu_ln_g"], s["sgu_ln_b"], s["sgu_w"], s["sgu_b"], dys[2])
        dp_cb, dp_cc, dp_cx, dcw = _conv_bwd(f"l{i}_conv_bwd", sv["p"], s["conv_w"], dys[3], rc)
        ds["conv_w"] = dcw[0:3]
        dpb = jnp.concatenate([dp_qkv, dp_pool, dp_sgu, dp_cb, dp_cc, dp_cx], axis=1)
        comm.grads(i, {"in_t": mm(f"l{i}_dWin", dpb, sv["hb"], "tn", BF16)})
        dh = mm(f"l{i}_dhb_a", dpb, w("in_t"), "nn", F32)
        dh = mm(f"l{i}_dhb_b", dgb, w("gate_t"), "nn", F32, acc=dh)

    def f_mod_bwd(xb, ddir, dhb, sc, sh):
        _, vjp = jax.vjp(_f_mod, xb, sc, sh)
        dxb, dsc, dsh = vjp(dhb)
        return dxb + ddir, dsc, dsh

    grad_x, dmods[0]["sc1"], dmods[0]["sh1"] = rows(
        "mod_in_bwd", f_mod_bwd, r, tm_n, nbc_n, [whole(xin), whole(dx_direct), whole(dh)],
        [mods[0]["sc1"], mods[0]["sh1"]], [(r - rc, d, F32, nbc_n)], [d, d])
    small_done(0)
    return loss, grad_x


def _direct_rider(src):
    def peers():
        mx, my, mc = [lax.axis_index(a) for a in MESH_AXES]
        out = []
        for kk in range(1, N_DEV):
            px = 1 - mx if kk & 4 else mx
            py = 1 - my if kk & 2 else my
            pc = 1 - mc if kk & 1 else mc
            out.append(((px, py, pc), 4 * px + 2 * py + pc))
        return 4 * mx + 2 * my + mc, out

    def start(ins, outs, sems):
        send, recv, loc = sems
        me, others = peers()
        pltpu.make_async_copy(ins[0], outs[0].at[me], loc.at[0]).start()
        for j, (peer, _) in enumerate(others):
            _rcopy(ins[0], outs[0].at[me], send, recv, j, peer).start()

    def finish(ins, outs, sems):
        send, recv, loc = sems
        me, others = peers()
        for j, (peer, peer_l) in enumerate(others):
            cp = _rcopy(ins[0], outs[0].at[peer_l], send, recv, j, peer)
            cp.wait_recv()
            cp.wait_send()
        pltpu.make_async_copy(ins[0], outs[0].at[me], loc.at[0]).wait()

    return _Rider([src], [jax.ShapeDtypeStruct((N_DEV,) + src.shape, src.dtype)],
                  _sem_scratch(N_DEV - 1, N_DEV - 1, 1), start, finish)


def _mesh_place():
    mx, my, mc = [lax.axis_index(a) for a in MESH_AXES]
    chips = [(1 - mx, my), (mx, 1 - my), (1 - mx, 1 - my)]

    def lid(px, py, pc):
        return 4 * px + 2 * py + pc

    return (mx, my, mc), (mx, my, 1 - mc), chips, lid


def _rcopy(src, dst, send_sems, recv_sems, k, to):
    return pltpu.make_async_remote_copy(src_ref=src, dst_ref=dst, send_sem=send_sems.at[k], recv_sem=recv_sems.at[k],
                                        device_id=to, device_id_type=pl.DeviceIdType.MESH)


def _sem_scratch(*sizes):
    return [pltpu.SemaphoreType.DMA((s,)) for s in sizes]


def _gather_rider(src, rows, buf=None):
    r0, r1 = rows
    win = pl.ds(r0, r1 - r0)

    def start(ins, outs, sems):
        send, recv, loc = sems
        (mx, my, mc), sib, chips, lid = _mesh_place()
        mine, dst = ins[0].at[win], outs[0].at[lid(mx, my, mc), win]
        pltpu.make_async_copy(mine, dst, loc.at[0]).start()
        _rcopy(mine, dst, send, recv, 0, sib).start()
        for j, chip in enumerate(chips):
            _rcopy(mine, dst, send, recv, 1 + j, (*chip, mc)).start()

    def finish(ins, outs, sems):
        send, recv, loc = sems
        (mx, my, mc), sib, chips, lid = _mesh_place()
        mine, dst = ins[0].at[win], outs[0].at[lid(mx, my, mc), win]
        for j, chip in enumerate(chips):
            blk = outs[0].at[lid(*chip, mc), win]
            _rcopy(mine, blk, send, recv, 1 + j, (*chip, mc)).wait_recv()
            _rcopy(blk, blk, send, recv, 4 + j, sib).start()
        _rcopy(mine, outs[0].at[lid(*sib), win], send, recv, 0, sib).wait_recv()
        for j, chip in enumerate(chips):
            _rcopy(mine, outs[0].at[lid(*chip, 1 - mc), win], send, recv, 4 + j, sib).wait_recv()
        for t in range(7):
            _rcopy(mine, dst, send, recv, t, sib).wait_send()
        pltpu.make_async_copy(mine, dst, loc.at[0]).wait()

    out_shape = jax.ShapeDtypeStruct((N_DEV,) + src.shape, src.dtype)
    if buf is None:
        return _Rider([src], [out_shape], _sem_scratch(7, 7, 1), start, finish)
    return _Rider([src, buf], [out_shape], _sem_scratch(7, 7, 1), start, finish, aliases={1: 0})


def _sibling_rider(part):
    def start(ins, outs, sems):
        send, recv = sems
        (mx, my, mc), sib, chips, lid = _mesh_place()
        for t, slab in enumerate([lid(*sib)] + [lid(*chip, 1 - mc) for chip in chips]):
            _rcopy(ins[0].at[slab], outs[0].at[t], send, recv, t, sib).start()

    def finish(ins, outs, sems):
        send, recv = sems
        _, sib, _, _ = _mesh_place()
        for t in range(4):
            cp = _rcopy(ins[0].at[0], outs[0].at[t], send, recv, t, sib)
            cp.wait_recv()
            cp.wait_send()

    return _Rider([part], [jax.ShapeDtypeStruct((4,) + part.shape[1:], part.dtype)], _sem_scratch(4, 4), start, finish)


def _chips_rider(pair, rows, buf=None):
    r0, r1 = rows
    win = pl.ds(r0, r1 - r0)

    def start(ins, outs, sems):
        send, recv = sems
        (mx, my, mc), sib, chips, lid = _mesh_place()
        for j, chip in enumerate(chips):
            _rcopy(ins[0].at[j, win], outs[0].at[j, win], send, recv, j, (*chip, mc)).start()

    def finish(ins, outs, sems):
        send, recv = sems
        (mx, my, mc), sib, chips, lid = _mesh_place()
        for j, chip in enumerate(chips):
            cp = _rcopy(ins[0].at[j, win], outs[0].at[j, win], send, recv, j, (*chip, mc))
            cp.wait_recv()
            cp.wait_send()

    out_shape = jax.ShapeDtypeStruct(pair.shape, pair.dtype)
    if buf is None:
        return _Rider([pair], [out_shape], _sem_scratch(3, 3), start, finish)
    return _Rider([pair, buf], [out_shape], _sem_scratch(3, 3), start, finish, aliases={1: 0})


def _run_rider(name, rider):
    n_in, n_out = len(rider.inputs), len(rider.out_shapes)

    def body(*refs):
        ins, outs, sems = refs[:n_in], refs[n_in:n_in + n_out], refs[n_in + n_out:]
        rider.start(ins, outs, sems)
        rider.finish(ins, outs, sems)

    any_spec = pl.BlockSpec(memory_space=pl.ANY)
    res = pl.pallas_call(body, name=name, in_specs=[any_spec] * n_in, out_specs=[any_spec] * n_out,
                         out_shape=rider.out_shapes, scratch_shapes=rider.scratch,
                         input_output_aliases=rider.aliases)(*rider.inputs)
    return list(res)


def _slab_ids():
    (mx, my, mc), _, chips, lid = _mesh_place()
    return jnp.stack([lid(*chip, mc) for chip in chips] + [lid(mx, my, mc)]).astype(jnp.int32)


def _pair_sum(name, part, rsib, ids):
    _, n, k = part.shape
    tr = _row_tile(n, 512, 16)

    def body(ids_ref, p_ref, r_ref, o_ref):
        o_ref[...] = (p_ref[...].astype(F32) + r_ref[...].astype(F32)).astype(o_ref.dtype)

    grid_spec = pltpu.PrefetchScalarGridSpec(
        num_scalar_prefetch=1, grid=(3, n // tr),
        in_specs=[pl.BlockSpec((None, tr, k), lambda j, i, ids: (ids[j], i, 0)),
                  pl.BlockSpec((None, tr, k), lambda j, i, ids: (1 + j, i, 0))],
        out_specs=pl.BlockSpec((None, tr, k), lambda j, i, ids: (j, i, 0)))
    return pl.pallas_call(body, name=name, grid_spec=grid_spec, out_shape=jax.ShapeDtypeStruct((3, n, k), part.dtype),
                          compiler_params=_cparams("parallel", "parallel"))(ids, part, rsib)


def _sum5(name, part, rsib, rici, ids, layer, stacked, rider=None):
    _, n, k = part.shape
    tr = _row_tile(n, 512, 16)
    first = isinstance(stacked, int)

    def body(ids_ref, p_ref, r_ref, c_ref, *rest):
        acc = p_ref[...].astype(F32) + r_ref[...].astype(F32)
        for j in range(3):
            acc = acc + c_ref[j].astype(F32)
        rest[-1][...] = acc

    in_specs = [pl.BlockSpec((None, tr, k), lambda i, ids: (ids[3], i, 0)),
                pl.BlockSpec((None, tr, k), lambda i, ids: (0, i, 0)),
                pl.BlockSpec((3, tr, k), lambda i, ids: (0, i, 0))] + ([] if first else [pl.BlockSpec(memory_space=pl.ANY)])
    n_layers = stacked if first else stacked.shape[0]
    outs, r_outs = _pcall(name, body, (n // tr,), in_specs, [pl.BlockSpec((None, tr, k), lambda i, ids: (layer, i, 0))],
                          [jax.ShapeDtypeStruct((n_layers, n, k), F32)],
                          (part, rsib, rici) + (() if first else (stacked,)), ("parallel",), rider=rider, prefetch=ids,
                          aliases={} if first else {3: 0})
    return outs[0] if rider is None else (outs[0], r_outs)


W_KEYS = ("in_t", "br0", "br1", "br2", "br3", "gate_t", "o", "ffg_t", "ffu_t", "ffd")
SUMS_TRANSPOSED_LATER = ("gate_t", "br0", "br1", "br2", "br3")


CARRIER_US = {"mod_in": 12, "in": 55, "gate": 95, "prep": 19, "attn": 112, "br0": 15, "merge": 50, "o": 25, "ln1": 27,
              "ffgu": 135, "ffd": 73, "ln2": 27, "loss": 20, "ln2_bwd": 44, "dF": 80,
              "dWffd": 64, "dh2a": 75, "dh2b": 75, "dWffg": 64, "dWffu": 64, "ln1_bwd": 44,
              "dMg": 25, "dWo": 25, "merge_bwd": 80, "dY0": 14, "dWbr0": 15, "attn_bwd": 195, "prep_bwd": 28,
              "dWin": 54, "dWgate": 95, "dhb_a": 64, "dhb_b": 115}
ICI_US_PER_MIB = 45.0
D2D_US_PER_MIB = 6.8
MIN_CHUNK_US = 10.0
CARRIER_FILL = 1.15


class _Comm:
    def __init__(self, wsrc):
        self.wsrc = wsrc
        self.n_layers = len(wsrc)
        self.queue = []
        self.riding = {}
        self.buf, self.left = {}, {}
        self.part, self.rsib, self.pair = {}, {}, {}
        self.ids = _slab_ids()
        for i in range(self.n_layers):
            for k in W_KEYS:
                self._push_chunks("gather", ("w", i, k), wsrc[i][k].shape, wsrc[i][k].dtype)

    def _push_chunks(self, kind, item, shape, dtype):
        n, k = shape[-2], shape[-1]
        us = n * k * jnp.dtype(dtype).itemsize / 2 ** 20 * ICI_US_PER_MIB
        pieces = max(1, int(us // MIN_CHUNK_US))
        while n % (16 * pieces):
            pieces -= 1
        step = n // pieces
        self.left[item] = pieces
        for c in range(pieces):
            self.queue.append(dict(kind=kind, item=item, rows=(c * step, (c + 1) * step), us=us / pieces))

    @staticmethod
    def _merge(units, u):
        v = units[-1] if units else None
        if not (v and v["item"] == u["item"] and v["kind"] == u["kind"] and u["rows"] and v["rows"][1] == u["rows"][0]):
            return False
        v.update(rows=(v["rows"][0], u["rows"][1]), us=v["us"] + u["us"], count=v.get("count", 1) + u.get("count", 1))
        return True

    def _unit_rider(self, u):
        item = u["item"]
        if u["kind"] == "gather":
            src = self.wsrc[item[1]][item[2]] if item[0] == "w" else self.part[item]
            return _gather_rider(src, u["rows"], self.buf.get(item))
        if u["kind"] == "sibling":
            return _sibling_rider(self.part[item])
        return _chips_rider(self.pair[item], u["rows"], self.buf.get(item))

    def _done(self, u, out):
        item = u["item"]
        if u["kind"] == "sibling":
            self.rsib[item] = out
            self.pair[item] = _pair_sum(f"pair_l{item[1]}_{item[2]}", self.part[item], out, self.ids)
            self._push_chunks("chips", item, self.pair[item].shape, self.pair[item].dtype)
            return
        self.buf[item] = out
        self.left[item] -= u.get("count", 1)

    def _send(self, name, units, call):
        outs = call(_compose([self._unit_rider(u) for u in units]))
        for u, o in zip(units, outs):
            self._done(u, o)

    def exchange(self, name, src, budget_us):
        units = self._take(budget_us)
        outs = _run_rider(name, _compose([_direct_rider(src)] + [self._unit_rider(u) for u in units]))
        for u, o in zip(units, outs[1:]):
            self._done(u, o)
        return outs[0]

    def rider(self, name, budget_us=None):
        budget = CARRIER_US.get(name.split("_", 1)[1] if name[0] == "l" and name[1].isdigit() else name, 0) \
            if budget_us is None else budget_us
        units = self._take(budget)
        if not units:
            return None
        self.riding[name] = units
        return _compose([self._unit_rider(u) for u in units])

    def _take(self, budget):
        units, used = [], 0.0
        while self.queue and used + self.queue[0]["us"] <= CARRIER_FILL * budget:
            u = self.queue[0]
            if not self._merge(units, u):
                if any(v["item"] == u["item"] for v in units):
                    break
                units.append(dict(u))
            used += u["us"]
            del self.queue[0]
        return units

    def deliver(self, name, outs):
        for u, o in zip(self.riding.pop(name), outs):
            self._done(u, o)

    def _flush(self, item, kinds):
        hits = [p for p, u in enumerate(self.queue) if u["item"] == item and u["kind"] in kinds]
        if not hits:
            return
        prefix = self.queue[:hits[-1] + 1]
        del self.queue[:hits[-1] + 1]
        units = []
        for u in prefix:
            if not self._merge(units, u):
                units.append(dict(u))
        tag = "_".join(str(t) for t in item) + "_" + kinds[0]
        batches = [[]]
        for u in units:
            if any(v["item"] == u["item"] for v in batches[-1]):
                batches.append([])
            batches[-1].append(u)
        for b, batch in enumerate(batches):
            self._send(None, batch, functools.partial(_run_rider, f"alone_{tag}_{b}"))

    def begin(self):
        self._flush(("w", 0, "in_t"), ("gather",))

    def weight(self, i, k):
        item = ("w", i, k)
        self._flush(item, ("gather",))
        o = self.buf[item]
        return o.reshape(-1, o.shape[-1])

    def grads(self, i, group):
        for k, g in group.items():
            item = ("g", i, k)
            self.part[item] = g.reshape(N_DEV, g.shape[0] // N_DEV, g.shape[1])
            us = g.size // N_DEV * g.dtype.itemsize / 2 ** 20 * D2D_US_PER_MIB
            self.queue.insert(0, dict(kind="sibling", item=item, rows=None, us=us))

    def total(self, k):
        out = self.n_layers
        for i in range(self.n_layers):
            item = ("g", i, k)
            self._flush(item, ("sibling",))
            self._flush(item, ("chips",))
            name = f"sum_l{i}_{k}"
            rider = self.rider(name, budget_us=self.part[item][0].size / 1.06e5) if k in SUMS_TRANSPOSED_LATER else None
            out = _sum5(name, self.part[item], self.rsib[item], self.buf[item], self.ids, i, out, rider=rider)
            if rider is not None:
                out, r_outs = out
                self.deliver(name, r_outs)
        return out

    def small_ready(self, i, dmod, ds):
        parts = [dmod[0], dmod[1]] + [ds[nm] for nm in LAYER_SMALL + ("conv_w",)]
        self.small_shapes = [p.shape for p in parts]
        self.gather_small(f"lat{i}", _pack([dmod[1]]))
        self.gather_small(f"small{i}", _pack(parts))

    def gather_small(self, name, arr):
        item = ("s", name)
        self.part[item] = arr
        waiting, self.queue = self.queue, []
        self._push_chunks("gather", item, arr.shape, arr.dtype)
        self.queue += waiting

    def gathered(self, name):
        item = ("s", name)
        self._flush(item, ("gather",))
        return self.buf[item]


def _row_tile(n, pref, mult):
    best = None
    t = mult
    while t <= min(n, pref):
        if n % t == 0:
            best = t
        t += mult
    return best if best is not None else n


def _sum8(name, slabs):
    _, n, k = slabs.shape
    tr = _row_tile(n, 128, 16)

    def body(s_ref, o_ref):
        acc = s_ref[0].astype(F32)
        for j in range(1, N_DEV):
            acc = acc + s_ref[j].astype(F32)
        o_ref[...] = acc

    return pl.pallas_call(
        body, name=name, grid=(n // tr,),
        in_specs=[pl.BlockSpec((N_DEV, tr, k), lambda i: (0, i, 0))],
        out_specs=pl.BlockSpec((tr, k), lambda i: (i, 0)),
        out_shape=jax.ShapeDtypeStruct((n, k), F32),
        compiler_params=_cparams("parallel"),
    )(slabs)


def _adamw(name, w, g, m, v, rider=None):
    n, k = w.shape[-2:]
    tr = _row_tile(n, 256, 8)

    def body(w_ref, g_ref, m_ref, v_ref, d_ref, m2_ref, v2_ref):
        gv = g_ref[...]
        m2 = ADAM_B1 * m_ref[...] + (1.0 - ADAM_B1) * gv
        v2 = ADAM_B2 * v_ref[...] + (1.0 - ADAM_B2) * jnp.square(gv)
        m_hat = m2 / (1.0 - ADAM_B1 ** ADAM_STEP)
        v_hat = v2 / (1.0 - ADAM_B2 ** ADAM_STEP)
        d_ref[...] = -ADAM_LR * (m_hat / (jnp.sqrt(v_hat) + ADAM_EPS) + ADAM_WD * w_ref[...])
        m2_ref[...] = m2
        v2_ref[...] = v2

    if w.ndim == 2:
        grid, spec = (n // tr,), pl.BlockSpec((tr, k), lambda i: (i, 0))
    else:
        grid, spec = (w.shape[0], n // tr), pl.BlockSpec((None, tr, k), lambda l, i: (l, i, 0))
    outs, r_outs = _pcall(name, body, grid, [spec] * 4, [spec] * 3, [jax.ShapeDtypeStruct(w.shape, F32)] * 3,
                          (w, g, m, v), ("parallel",) * len(grid), rider=rider)
    return outs if rider is None else (outs, r_outs)


def _pack(arrs):
    flat = jnp.concatenate([a.reshape(-1).astype(F32) for a in arrs])
    pad = (-flat.shape[0]) % 2048
    if pad:
        flat = jnp.concatenate([flat, jnp.zeros((pad,), F32)])
    return flat.reshape(-1, 128)


def _unpack(packed, shapes):
    flat = packed.reshape(-1)
    out, off = [], 0
    for shp in shapes:
        size = math.prod(shp)
        out.append(flat[off:off + size].reshape(shp))
        off += size
    return out


WEIGHT_NAMES = ("c_ctx", "w_ada", "b_ada", "w_in", "q_norm_g", "k_norm_g", "pool_w", "pool_scale", "sgu_ln_g",
                "sgu_ln_b", "sgu_w", "sgu_b", "conv_w", "w_br_attn", "w_br_pool", "w_br_sgu", "w_br_conv", "w_gate",
                "b_gate", "w_o", "ln1_g", "ln1_b", "w_ff_gate", "w_ff_up", "w_ff_down", "ln2_g", "ln2_b")
COL_SHARDED = {"w_in": "in_t", "w_gate": "gate_t", "w_ff_gate": "ffg_t", "w_ff_up": "ffu_t",
               "w_br_attn": "br0", "w_br_pool": "br1", "w_br_sgu": "br2", "w_br_conv": "br3"}
ROW_SHARDED = {"w_o": "o", "w_ff_down": "ffd"}
LAYER_SMALL = ("q_norm_g", "k_norm_g", "pool_w", "pool_scale", "sgu_ln_g", "sgu_ln_b", "sgu_w", "sgu_b", "b_gate",
               "ln1_g", "ln1_b", "ln2_g", "ln2_b")
SMALL_ORDER = ("c_ctx", "b_ada") + LAYER_SMALL + ("conv_w",)


def _train_step(a):
    n_layers, d = a["w_in"].shape[0], a["x"].shape[-1]
    rc = a["ctx"].shape[1]
    alpha = (2 * n_layers) ** 0.25
    mx, my, mc = [lax.axis_index(ax) for ax in MESH_AXES]
    me = 4 * mx + 2 * my + mc
    ada_w = a["w_ada"].shape[-1]
    cw_loc = a["conv_w"].shape[-1]

    comm = _Comm([{**{key: jnp.swapaxes(a[nm], 1, 2)[i].astype(BF16) for nm, key in COL_SHARDED.items()},
                   **{key: a[nm][i].astype(BF16) for nm, key in ROW_SHARDED.items()}} for i in range(n_layers)])

    def carried(name, *args, budget_us, **kw):
        rider = comm.rider(name, budget_us=budget_us)
        if rider is None:
            return _mm(name, *args, **kw)
        res, r_outs = _mm(name, *args, rider=rider, **kw)
        comm.deliver(name, r_outs)
        return res

    n_c, n_cw = d, n_layers * 3 * cw_loc
    got = comm.exchange("gather_cond", _pack([a["c"], a["conv_w"]]), budget_us=12).reshape(N_DEV, -1)
    c_all = got[:, :n_c]
    conv_w = got[:, n_c:n_c + n_cw].reshape(N_DEV, n_layers, 3, cw_loc).transpose(1, 2, 0, 3).reshape(n_layers, 3, -1)
    cond = jnp.concatenate([c_all, a["c_ctx"][None], jnp.zeros((16 - N_DEV - 1, d), F32)], axis=0)
    sil, sil_vjp = jax.vjp(jax.nn.silu, cond)
    sil = sil.astype(BF16)

    mod_cols = jnp.concatenate([carried(f"ada{i}", sil, a["w_ada"][i], "nn", F32, budget_us=20)
                                for i in range(n_layers)], axis=0)
    got = comm.exchange("gather_mod", mod_cols, budget_us=22)
    mod_all = got.reshape(N_DEV, n_layers, 16, ada_w).transpose(1, 2, 0, 3).reshape(n_layers, 16, -1)
    mod_all = mod_all + a["b_ada"][:, None, :]
    mod = jnp.stack([mod_all[:, N_DEV], lax.dynamic_index_in_dim(mod_all, me, axis=1, keepdims=False)], axis=1)
    comm.begin()
    sp = [{nm: a[nm][i] for nm in LAYER_SMALL} for i in range(n_layers)]
    for i in range(n_layers):
        sp[i]["conv_w"] = conv_w[i]

    xin = jnp.concatenate([a["ctx"][0], a["x"][0]], axis=0)
    loss_l, grad_x = _local_step(xin, a["loss_target"][0], mod, comm, sp, rc, alpha)
    loss = lax.psum(loss_l, MESH_AXES)
    grads = {}

    def transposed_home(nm):
        return nm in COL_SHARDED and a[nm].shape[-1] % 128 != 0

    delta, new_m, new_v = {}, {}, {}

    def adamw(nm):
        name = f"adamw_{nm}"
        there = transposed_home(nm)
        view = (lambda t: jnp.swapaxes(t, 1, 2)) if there else (lambda t: t)
        if nm in COL_SHARDED:
            g = comm.total(COL_SHARDED[nm])
            grads[nm] = jnp.swapaxes(g, 1, 2)
            g = g if there else grads[nm]
        elif nm in ROW_SHARDED:
            g = grads[nm] = comm.total(ROW_SHARDED[nm])
        else:
            g = grads[nm]
        res = _adamw(name, view(a[nm]), g, view(a["m_" + nm]), view(a["v_" + nm]))
        delta[nm], new_m[nm], new_v[nm] = [view(t) for t in res]

    for nm in ("w_ff_down", "w_ff_gate", "w_ff_up", "w_o", "w_br_attn", "w_br_pool", "w_br_sgu", "w_br_conv"):
        adamw(nm)

    tots = [_unpack(_sum8(f"sum_small{i}", comm.gathered(f"small{i}")), comm.small_shapes) for i in range(n_layers)]
    dmod_c, dmod_lat_sum = jnp.stack([t[0] for t in tots]), jnp.stack([t[1] for t in tots])
    for j, nm in enumerate(LAYER_SMALL + ("conv_w",)):
        grads[nm] = jnp.stack([t[2 + j] for t in tots])
    grads["conv_w"] = lax.dynamic_slice_in_dim(grads["conv_w"], me * cw_loc, cw_loc, axis=2)
    grads["b_ada"] = dmod_c + dmod_lat_sum
    dmod_lat_all = jnp.stack([comm.gathered(f"lat{i}").reshape(N_DEV, -1)[:, :6 * d] for i in range(n_layers)])
    dm_rows = jnp.concatenate([dmod_lat_all, dmod_c[:, None, :],
                               jnp.zeros((n_layers, 16 - N_DEV - 1, 6 * d), F32)], axis=1)
    dm_cols = lax.dynamic_slice_in_dim(dm_rows, me * ada_w, ada_w, axis=2).astype(BF16)
    grads["w_ada"] = jnp.stack([carried(f"dWada{i}", sil, dm_cols[i], "tn", F32, budget_us=20)
                                for i in range(n_layers)])
    dsil = None
    for i in range(n_layers):
        dsil = carried(f"dsil{i}", dm_cols[i], a["w_ada"][i], "nt", F32, acc=dsil, budget_us=10)
    dsil = _sum8("sum_dsil", comm.exchange("gather_dsil", dsil, budget_us=17))
    grads["c_ctx"] = sil_vjp(dsil)[0][N_DEV]

    for nm in ("w_ada", "w_in", "w_gate"):
        adamw(nm)
    shapes = [a[nm].shape for nm in SMALL_ORDER]
    res = _adamw("adamw_small", _pack([a[nm] for nm in SMALL_ORDER]), _pack([grads[nm] for nm in SMALL_ORDER]),
                 _pack([a["m_" + nm] for nm in SMALL_ORDER]), _pack([a["v_" + nm] for nm in SMALL_ORDER]))
    for tree, packed in zip((delta, new_m, new_v), res):
        for nm, t in zip(SMALL_ORDER, _unpack(packed, shapes)):
            tree[nm] = t
    return (loss, grad_x[None], *[grads[nm] for nm in WEIGHT_NAMES], *[delta[nm] for nm in WEIGHT_NAMES],
            *[new_m[nm] for nm in WEIGHT_NAMES], *[new_v[nm] for nm in WEIGHT_NAMES])


def kernel(x, c, ctx, c_ctx, w_ada, b_ada, w_in, q_norm_g, k_norm_g, pool_w, pool_scale, sgu_ln_g, sgu_ln_b, sgu_w, sgu_b, conv_w, w_br_attn, w_br_pool, w_br_sgu, w_br_conv, w_gate, b_gate, w_o, ln1_g, ln1_b, w_ff_gate, w_ff_up, w_ff_down, ln2_g, ln2_b, loss_target, m_c_ctx, m_w_ada, m_b_ada, m_w_in, m_q_norm_g, m_k_norm_g, m_pool_w, m_pool_scale, m_sgu_ln_g, m_sgu_ln_b, m_sgu_w, m_sgu_b, m_conv_w, m_w_br_attn, m_w_br_pool, m_w_br_sgu, m_w_br_conv, m_w_gate, m_b_gate, m_w_o, m_ln1_g, m_ln1_b, m_w_ff_gate, m_w_ff_up, m_w_ff_down, m_ln2_g, m_ln2_b, v_c_ctx, v_w_ada, v_b_ada, v_w_in, v_q_norm_g, v_k_norm_g, v_pool_w, v_pool_scale, v_sgu_ln_g, v_sgu_ln_b, v_sgu_w, v_sgu_b, v_conv_w, v_w_br_attn, v_w_br_pool, v_w_br_sgu, v_w_br_conv, v_w_gate, v_b_gate, v_w_o, v_ln1_g, v_ln1_b, v_w_ff_gate, v_w_ff_up, v_w_ff_down, v_ln2_g, v_ln2_b):
    names = list(WEIGHT_NAMES)
    args = dict(zip(
        ["x", "c", "ctx"] + names + ["loss_target"] + ["m_" + n for n in names] + ["v_" + n for n in names],
        (x, c, ctx, c_ctx, w_ada, b_ada, w_in, q_norm_g, k_norm_g, pool_w, pool_scale, sgu_ln_g, sgu_ln_b, sgu_w, sgu_b, conv_w, w_br_attn, w_br_pool, w_br_sgu, w_br_conv, w_gate, b_gate, w_o, ln1_g, ln1_b, w_ff_gate, w_ff_up, w_ff_down, ln2_g, ln2_b, loss_target, m_c_ctx, m_w_ada, m_b_ada, m_w_in, m_q_norm_g, m_k_norm_g, m_pool_w, m_pool_scale, m_sgu_ln_g, m_sgu_ln_b, m_sgu_w, m_sgu_b, m_conv_w, m_w_br_attn, m_w_br_pool, m_w_br_sgu, m_w_br_conv, m_w_gate, m_b_gate, m_w_o, m_ln1_g, m_ln1_b, m_w_ff_gate, m_w_ff_up, m_w_ff_down, m_ln2_g, m_ln2_b, v_c_ctx, v_w_ada, v_b_ada, v_w_in, v_q_norm_g, v_k_norm_g, v_pool_w, v_pool_scale, v_sgu_ln_g, v_sgu_ln_b, v_sgu_w, v_sgu_b, v_conv_w, v_w_br_attn, v_w_br_pool, v_w_br_sgu, v_w_br_conv, v_w_gate, v_b_gate, v_w_o, v_ln1_g, v_ln1_b, v_w_ff_gate, v_w_ff_up, v_w_ff_down, v_ln2_g, v_ln2_b)))
    return _train_step(args)
```

```python
import functools
import math

import jax
import jax.numpy as jnp
from jax import lax
from jax.experimental import pallas as pl
from jax.experimental.pallas import tpu as pltpu

F32 = jnp.float32
BF16 = jnp.bfloat16

N_DEV = 8
MESH_AXES = ("x", "y", "c")
V7X_VMEM_LIMIT_BYTES = 56 * 1024 * 1024

GRID_W = 64
HEAD_DIM = 128
N_HEADS = 8
N_KV_HEADS = 2
KV_GROUP = N_HEADS // N_KV_HEADS
Q_W = N_HEADS * HEAD_DIM
KV_W = N_KV_HEADS * HEAD_DIM
ROPE_THETA = 10000.0
ROPE_AXIS_DIM = HEAD_DIM // 2
POOL_WINDOWS = (2, 4, 8, 16)
GC = 128
N_GROUPS = 4
BR_W = N_GROUPS * GC
SGU_CHUNK = 128
N_BRANCH = 4
LN_EPS = 1e-5
RMS_EPS = 1e-6
OFF_K = Q_W
OFF_V = OFF_K + KV_W
OFF_POOL = OFF_V + KV_W
OFF_U = OFF_POOL + BR_W
OFF_VG = OFF_U + BR_W
OFF_CB = OFF_VG + BR_W
OFF_CC = OFF_CB + BR_W
OFF_CX = OFF_CC + BR_W
IN_W = OFF_CX + BR_W
ATT_SCALE = HEAD_DIM ** -0.5

ADAM_LR = 0.001
ADAM_B1 = 0.9
ADAM_B2 = 0.999
ADAM_EPS = 1e-08
ADAM_WD = 0.01
ADAM_STEP = 10

_NT = (((1,), (1,)), ((), ()))
_NN = (((1,), (0,)), ((), ()))
_TN = (((0,), (0,)), ((), ()))
_DIMS = {"nt": _NT, "nn": _NN, "tn": _TN}


def _cparams(*sem):
    return pltpu.CompilerParams(dimension_semantics=sem, vmem_limit_bytes=V7X_VMEM_LIMIT_BYTES)


def _tile(dim, pref):
    best = None
    t = 128
    while t <= min(dim, pref):
        if dim % t == 0:
            best = t
        t += 128
    return best if best is not None else dim


def _dot(a, b, dims):
    return lax.dot_general(a.astype(BF16), b.astype(BF16), dims, preferred_element_type=F32)


class _Rider:
    def __init__(self, inputs, out_shapes, scratch, start, finish, aliases=None):
        self.inputs, self.out_shapes, self.scratch = list(inputs), list(out_shapes), list(scratch)
        self.start, self.finish = start, finish
        self.aliases = dict(aliases or {})


def _compose(riders):
    inputs, outs, scratch, aliases, spans = [], [], [], {}, []
    for rd in riders:
        i0, o0, s0 = len(inputs), len(outs), len(scratch)
        aliases.update({i0 + p: o0 + q for p, q in rd.aliases.items()})
        inputs += rd.inputs
        outs += rd.out_shapes
        scratch += rd.scratch
        spans.append((slice(i0, len(inputs)), slice(o0, len(outs)), slice(s0, len(scratch))))

    def start(ins, os, sems):
        for rd, (si, so, ss) in zip(riders, spans):
            rd.start(ins[si], os[so], sems[ss])

    def finish(ins, os, sems):
        for rd, (si, so, ss) in zip(riders, spans):
            rd.finish(ins[si], os[so], sems[ss])

    return _Rider(inputs, outs, scratch, start, finish, aliases)


def _pcall(name, body, grid, in_specs, out_specs, out_shape, args, sem, scratch=(), rider=None, prefetch=None,
           aliases=None):
    in_specs, out_specs, out_shape, scratch = list(in_specs), list(out_specs), list(out_shape), list(scratch)
    n_pre = 0 if prefetch is None else 1
    n_in, n_out, n_scr = len(in_specs), len(out_specs), len(scratch)
    r_in, r_out = (len(rider.inputs), len(rider.out_shapes)) if rider is not None else (0, 0)
    any_spec = pl.BlockSpec(memory_space=pl.ANY)
    io_aliases = {n_pre + p: q for p, q in (aliases or {}).items()}
    if rider is not None:
        io_aliases.update({n_pre + n_in + p: n_out + q for p, q in rider.aliases.items()})
        in_specs, out_specs = in_specs + [any_spec] * r_in, out_specs + [any_spec] * r_out
        out_shape, scratch = out_shape + rider.out_shapes, scratch + rider.scratch
        args, sem = (*args, *rider.inputs), ["arbitrary"] * len(grid)

    def wrapped(*refs):
        pre, refs = refs[:n_pre], refs[n_pre:]
        if rider is None:
            return body(*pre, *refs)
        ins, refs = refs[:n_in], refs[n_in:]
        r_ins, refs = refs[:r_in], refs[r_in:]
        outs, refs = refs[:n_out], refs[n_out:]
        r_outs, refs = refs[:r_out], refs[r_out:]
        scr, r_scr = refs[:n_scr], refs[n_scr:]
        first = functools.reduce(jnp.logical_and, [pl.program_id(ax) == 0 for ax in range(len(grid))])
        last = functools.reduce(jnp.logical_and, [pl.program_id(ax) == grid[ax] - 1 for ax in range(len(grid))])

        @pl.when(first)
        def _():
            rider.start(r_ins, r_outs, r_scr)

        body(*pre, *ins, *outs, *scr)

        @pl.when(last)
        def _():
            rider.finish(r_ins, r_outs, r_scr)

    if prefetch is None:
        res = pl.pallas_call(wrapped, name=name, grid=grid, in_specs=in_specs, out_specs=out_specs, out_shape=out_shape,
                             scratch_shapes=scratch, input_output_aliases=io_aliases,
                             compiler_params=_cparams(*sem))(*args)
    else:
        grid_spec = pltpu.PrefetchScalarGridSpec(num_scalar_prefetch=1, grid=grid, in_specs=in_specs,
                                                 out_specs=out_specs, scratch_shapes=scratch)
        res = pl.pallas_call(wrapped, name=name, grid_spec=grid_spec, out_shape=out_shape,
                             input_output_aliases=io_aliases, compiler_params=_cparams(*sem))(prefetch, *args)
    return list(res[:n_out]), list(res[n_out:])


V7X_MM_VMEM_BUDGET = 40 * 1024 * 1024


def _mm_plan(form, m, n, k, a_size, b_size, o_size, has_acc):
    tn = n if (form == "tn" and n <= 2048) else _tile(n, 512)
    rows = sorted({m} | {t for t in range(128, m, 128) if m % t == 0}, reverse=True)
    for tk, min_tm in ((k, 384), (k if k <= 2816 else _tile(k, 2816), 0)):
        nk = k // tk
        for tm in rows:
            need = 2 * (tm * tk * a_size + tn * tk * b_size + tm * tn * o_size) + tm * tn * 4 * (2 if nk > 1 else 1)
            need += 2 * tm * tn * 4 if has_acc else 0
            if need <= V7X_MM_VMEM_BUDGET and tm >= min(min_tm, m):
                return tm, tn, tk
    return _tile(m, 128), tn, tk


def _mm(name, a, b, form, out_dtype, acc=None, rider=None):
    if form == "nt":
        (m, k), (n, k2) = a.shape, b.shape
    elif form == "nn":
        (m, k), (k2, n) = a.shape, b.shape
    else:
        (k, m), (k2, n) = a.shape, b.shape
    assert k == k2, (name, a.shape, b.shape)
    has_acc = acc is not None
    tm, tn, tk = _mm_plan(form, m, n, k, a.dtype.itemsize, b.dtype.itemsize, jnp.dtype(out_dtype).itemsize, has_acc)
    nk = k // tk
    a_spec = {"nt": pl.BlockSpec((tm, tk), lambda i, j, kk: (i, kk)),
              "nn": pl.BlockSpec((tm, tk), lambda i, j, kk: (i, kk)),
              "tn": pl.BlockSpec((tk, tm), lambda i, j, kk: (kk, i))}[form]
    b_spec = {"nt": pl.BlockSpec((tn, tk), lambda i, j, kk: (j, kk)),
              "nn": pl.BlockSpec((tk, tn), lambda i, j, kk: (kk, j)),
              "tn": pl.BlockSpec((tk, tn), lambda i, j, kk: (kk, j))}[form]
    o_spec = pl.BlockSpec((tm, tn), lambda i, j, kk: (i, j))
    dims = _DIMS[form]

    def body(*refs):
        a_ref, b_ref = refs[0], refs[1]
        c_ref = refs[2] if has_acc else None
        o_ref = refs[3] if has_acc else refs[2]

        def finish(r):
            if has_acc:
                r = r + c_ref[...]
            o_ref[...] = r.astype(o_ref.dtype)

        if nk == 1:
            finish(_dot(a_ref[...], b_ref[...], dims))
            return
        acc_ref = refs[-1]
        kk = pl.program_id(2)

        @pl.when(kk == 0)
        def _():
            acc_ref[...] = _dot(a_ref[...], b_ref[...], dims)

        @pl.when(kk > 0)
        def _():
            acc_ref[...] += _dot(a_ref[...], b_ref[...], dims)

        @pl.when(kk == nk - 1)
        def _():
            finish(acc_ref[...])

    in_specs = [a_spec, b_spec] + ([o_spec] if has_acc else [])
    args = (a, b) + ((acc,) if has_acc else ())
    outs, r_outs = _pcall(name, body, (m // tm, n // tn, nk), in_specs, [o_spec],
                          [jax.ShapeDtypeStruct((m, n), out_dtype)], args, ("parallel", "parallel", "arbitrary"),
                          scratch=[pltpu.VMEM((tm, tn), F32)] if nk > 1 else [], rider=rider)
    return outs[0] if rider is None else (outs[0], r_outs)


def _mm_fused(name, a, bs, epilogue, tile_ins, out_dtypes, rider=None):
    (m, k), (n, _) = a.shape, bs[0].shape
    tn = _tile(n, 512)
    tm = None
    for cand in sorted({m} | {t for t in range(128, m, 128) if m % t == 0}, reverse=True):
        per_tile = sum(t.dtype.itemsize for t in tile_ins) + sum(jnp.dtype(d).itemsize for d in out_dtypes)
        need = 2 * (cand * k * a.dtype.itemsize + len(bs) * tn * k * bs[0].dtype.itemsize + cand * tn * per_tile)
        need += (len(bs) + 2) * cand * tn * 4
        if need <= V7X_MM_VMEM_BUDGET:
            tm = cand
            break
    assert tm is not None, name
    n_b, n_t = len(bs), len(tile_ins)
    tile = pl.BlockSpec((tm, tn), lambda i, j: (i, j))

    def body(a_ref, *refs):
        prods = [_dot(a_ref[...], r[...], _NT) for r in refs[:n_b]]
        outs = epilogue(prods, [r[...].astype(F32) for r in refs[n_b:n_b + n_t]])
        for r, o in zip(refs[n_b + n_t:], outs):
            r[...] = o.astype(r.dtype)

    outs, r_outs = _pcall(name, body, (m // tm, n // tn),
                          [pl.BlockSpec((tm, k), lambda i, j: (i, 0))] + [pl.BlockSpec((tn, k), lambda i, j: (j, 0))] * n_b
                          + [tile] * n_t, [tile] * len(out_dtypes),
                          [jax.ShapeDtypeStruct((m, n), d) for d in out_dtypes], (a, *bs, *tile_ins),
                          ("parallel", "parallel"), rider=rider)
    return outs if rider is None else (outs, r_outs)


def _rows(name, fn, n_rows, tm, nbc, row_ins, type_ins, row_outs, acc_outs, rider=None):
    n_ri, n_ti, n_ro, n_ao = len(row_ins), len(type_ins), len(row_outs), len(acc_outs)

    def row_map(i, cb, roff):
        return (jnp.maximum(i - roff, 0), cb)

    def type_map(i):
        return (jnp.where(i >= nbc, 1, 0), 0, 0)

    in_specs, args = [], []
    for arr, cb, width, roff in row_ins:
        in_specs.append(pl.BlockSpec((tm, width), functools.partial(row_map, cb=cb, roff=roff)))
        args.append(arr)
    def shared_map(i):
        return (0, 0, 0)

    for arr in type_ins:
        in_specs.append(pl.BlockSpec((None, 1, arr.shape[-1]), type_map if arr.shape[0] == 2 else shared_map))
        args.append(arr)
    out_shape, out_specs = [], []
    for total, width, dtype, roff in row_outs:
        out_shape.append(jax.ShapeDtypeStruct((total, width), dtype))
        out_specs.append(pl.BlockSpec((tm, width), functools.partial(row_map, cb=0, roff=roff)))
    acc_shared = [isinstance(w, tuple) for w in acc_outs]
    for width in acc_outs:
        if isinstance(width, tuple):
            out_shape.append(jax.ShapeDtypeStruct((1, 1, width[0]), F32))
            out_specs.append(pl.BlockSpec((None, 1, width[0]), shared_map))
        else:
            out_shape.append(jax.ShapeDtypeStruct((2, 1, width), F32))
            out_specs.append(pl.BlockSpec((None, 1, width), type_map))
    n_in = n_ri + n_ti

    def body(*refs):
        i = pl.program_id(0)
        outs = fn(*[r[...].astype(F32) for r in refs[:n_in]])
        if not isinstance(outs, (tuple, list)):
            outs = (outs,)
        assert len(outs) == n_ro + n_ao, (name, len(outs))
        for r, o in zip(refs[n_in:n_in + n_ro], outs[:n_ro]):
            r[...] = o.astype(r.dtype)
        if n_ao:
            for r, o, shared in zip(refs[n_in + n_ro:], outs[n_ro:], acc_shared):
                first = i == 0 if shared else jnp.logical_or(i == 0, i == nbc)
                o = jnp.broadcast_to(o.astype(F32), r.shape)

                @pl.when(first)
                def _(r=r, o=o):
                    r[...] = o

                @pl.when(jnp.logical_not(first))
                def _(r=r, o=o):
                    r[...] += o

    outs, r_outs = _pcall(name, body, (n_rows // tm,), in_specs, out_specs, out_shape, args, ("arbitrary",),
                          rider=rider)
    return outs if rider is None else (outs, r_outs)


def _vjp_fn(f, n_row, n_cot, keep=None):
    def g(*args):
        prim = args[:n_row] + args[n_row + n_cot:]
        cots = args[n_row:n_row + n_cot]
        out, vjp = jax.vjp(f, *prim)
        grads = vjp(tuple(cots) if isinstance(out, (tuple, list)) else cots[0])
        return grads if keep is None else tuple(grads[j] for j in keep)
    return g


def _typed(v):
    return v.reshape(1, 1, -1)


def _ln(x, g, b):
    mu = jnp.mean(x, axis=-1, keepdims=True)
    var = jnp.mean(jnp.square(x - mu), axis=-1, keepdims=True)
    return (x - mu) * lax.rsqrt(var + LN_EPS) * g + b


def _f_mod(x, sc, sh):
    return x * (1.0 + sc) + sh


def _make_f_ln(alpha, with_mod):
    def f(x, o, gate, lng, lnb, *mod):
        xn = _ln(alpha * x + gate * o, lng, lnb)
        if with_mod:
            sc, sh = mod
            return xn, xn * (1.0 + sc) + sh
        return xn
    return f


@jax.custom_vjp
def _rot(y):
    lane = lax.broadcasted_iota(jnp.int32, y.shape, 1)
    return jnp.where(lane % 64 < 32, pltpu.roll(y, 96, axis=1), pltpu.roll(y, 32, axis=1))


_rot.defvjp(lambda y: (_rot(y), None), lambda _, g: (_rot(g),))


def _f_prep(p, cos, sin, qg, kg):
    def head(xh, g):
        ms = jnp.mean(jnp.square(xh), axis=-1, keepdims=True)
        y = xh * lax.rsqrt(ms + RMS_EPS) * g
        return y * cos + _rot(y) * sin
    q = jnp.concatenate([head(p[:, h * HEAD_DIM:(h + 1) * HEAD_DIM], qg) for h in range(N_HEADS)], axis=1)
    k = jnp.concatenate([head(p[:, OFF_K + h * HEAD_DIM:OFF_K + (h + 1) * HEAD_DIM], kg)
                         for h in range(N_KV_HEADS)], axis=1)
    return q, k, p[:, OFF_V:OFF_POOL]


def _f_gate(g, t0, t1, t2, t3, b):
    d = t0.shape[-1]
    ts = (t0, t1, t2, t3)
    terms = [jax.nn.sigmoid(g[:, k * d:(k + 1) * d] + b[:, k * d:(k + 1) * d]) * ts[k] for k in range(N_BRANCH)]
    return terms[0] + terms[1] + terms[2] + terms[3]


def _f_swiglu(a, b):
    return jax.nn.silu(a) * b


def _softmax(raw):
    e = jnp.exp2((raw - jnp.max(raw, axis=-1, keepdims=True)) * (ATT_SCALE * math.log2(math.e)))
    return e / jnp.sum(e, axis=-1, keepdims=True)


def _attn_fwd(name, q, k, v, rc, ctx_queries, tq=256, rider=None):
    r = q.shape[0]
    assert rc % tq == 0 and r % tq == 0
    nqc = rc // tq

    def body(q_ref, k_ref, v_ref, o_ref):
        qi = pl.program_id(1)

        def attend(nk):
            raw = _dot(q_ref[...], k_ref[0:nk, :], _NT)
            e = jnp.exp2((raw - jnp.max(raw, axis=-1, keepdims=True)) * (ATT_SCALE * math.log2(math.e)))
            o = _dot(e, v_ref[0:nk, :], _NN) / jnp.sum(e, axis=-1, keepdims=True)
            o_ref[...] = o.astype(o_ref.dtype)

        @pl.when(qi < nqc)
        def _():
            if ctx_queries:
                attend(rc)
            else:
                o_ref[...] = jnp.zeros_like(o_ref)

        @pl.when(qi >= nqc)
        def _():
            attend(r)

    outs, r_outs = _pcall(
        name, body, (N_HEADS, r // tq),
        [pl.BlockSpec((tq, HEAD_DIM), lambda h, i: (i, h)),
         pl.BlockSpec((r, HEAD_DIM), lambda h, i: (0, h // KV_GROUP)),
         pl.BlockSpec((r, HEAD_DIM), lambda h, i: (0, h // KV_GROUP))],
        [pl.BlockSpec((tq, HEAD_DIM), lambda h, i: (i, h))],
        [jax.ShapeDtypeStruct((r, Q_W), BF16)], (q, k, v), ("parallel", "parallel"), rider=rider)
    return outs[0] if rider is None else (outs[0], r_outs)


def _attn_bwd(name, q, k, v, do, rc, ctx_queries, tq=256, rider=None):
    r = q.shape[0]
    nqc = rc // tq

    def body(q_ref, k_ref, v_ref, do_ref, dq_ref, dk_ref, dv_ref):
        g, qi = pl.program_id(1), pl.program_id(2)

        @pl.when(jnp.logical_and(g == 0, qi == 0))
        def _():
            dk_ref[...] = jnp.zeros_like(dk_ref)
            dv_ref[...] = jnp.zeros_like(dv_ref)

        def grad(nk):
            qb, kb, vb = q_ref[...], k_ref[0:nk, :], v_ref[0:nk, :]
            dob = do_ref[...].astype(BF16)
            p = _softmax(_dot(qb, kb, _NT))
            dv_ref[0:nk, :] += _dot(p, dob, _TN)
            dp = _dot(dob, vb, _NT)
            ds = p * (dp - jnp.sum(dp * p, axis=-1, keepdims=True)) * ATT_SCALE
            dq_ref[...] = _dot(ds, kb, _NN)
            dk_ref[0:nk, :] += _dot(ds, qb, _TN)

        @pl.when(qi < nqc)
        def _():
            if ctx_queries:
                grad(rc)
            else:
                dq_ref[...] = jnp.zeros_like(dq_ref)

        @pl.when(qi >= nqc)
        def _():
            grad(r)

    def qmap(kv, g, i):
        return (i, kv * KV_GROUP + g)

    def kvmap(kv, g, i):
        return (0, kv)

    outs, r_outs = _pcall(
        name, body, (N_KV_HEADS, KV_GROUP, r // tq),
        [pl.BlockSpec((tq, HEAD_DIM), qmap), pl.BlockSpec((r, HEAD_DIM), kvmap),
         pl.BlockSpec((r, HEAD_DIM), kvmap), pl.BlockSpec((tq, HEAD_DIM), qmap)],
        [pl.BlockSpec((tq, HEAD_DIM), qmap), pl.BlockSpec((r, HEAD_DIM), kvmap), pl.BlockSpec((r, HEAD_DIM), kvmap)],
        [jax.ShapeDtypeStruct((r, Q_W), F32), jax.ShapeDtypeStruct((r, KV_W), F32),
         jax.ShapeDtypeStruct((r, KV_W), F32)],
        (q, k, v, do), ("arbitrary", "arbitrary", "arbitrary"), rider=rider)
    return outs if rider is None else (outs, r_outs)


def _segments(shape, rc):
    t = lax.broadcasted_iota(jnp.int32, shape, 0)
    lo = jnp.where(t < rc, 0, rc)
    hi = jnp.where(t < rc, rc, shape[0])
    return t, lo, hi


def _shifted(x, o, t, lo, hi):
    n = x.shape[0]
    sh = pltpu.roll(x, (-o) % n, axis=0)
    return jnp.where(jnp.logical_and(t + o >= lo, t + o < hi), sh, 0.0)


def _winsum(x, left, right, t, lo, hi):
    acc = x
    for o in range(-left, right + 1):
        if o != 0:
            acc = acc + _shifted(x, o, t, lo, hi)
    return acc


def _pool_parts(z, g, t, lo, hi):
    w = POOL_WINDOWS[g]
    left = w // 2
    right = w - 1 - left
    count = (jnp.minimum(t + right + 1, hi) - jnp.maximum(t - left, lo)).astype(F32)
    return _winsum(z, left, right, t, lo, hi) / count - z, count, left, right


def _pool_fwd(name, p, pool_w, pool_scale, rc):
    r = p.shape[0]

    def body(z_ref, w_ref, s_ref, y_ref):
        t, lo, hi = _segments((r, GC), rc)
        for g in range(N_GROUPS):
            cols = slice(g * GC, (g + 1) * GC)
            d, _, _, _ = _pool_parts(z_ref[:, cols], g, t, lo, hi)
            y_ref[:, cols] = (_dot(d, w_ref[g], _NN) * s_ref[:, cols]).astype(y_ref.dtype)

    return pl.pallas_call(
        body, name=name, grid=(1,),
        in_specs=[pl.BlockSpec((r, BR_W), lambda i: (0, OFF_POOL // BR_W)),
                  pl.BlockSpec((N_GROUPS, GC, GC), lambda i: (0, 0, 0)),
                  pl.BlockSpec((1, BR_W), lambda i: (0, 0))],
        out_specs=pl.BlockSpec((r, BR_W), lambda i: (0, 0)),
        out_shape=jax.ShapeDtypeStruct((r, BR_W), BF16),
        compiler_params=_cparams("arbitrary"),
    )(p, pool_w, pool_scale.reshape(1, BR_W))


def _pool_bwd(name, p, pool_w, pool_scale, dy, rc):
    r = p.shape[0]

    def body(z_ref, w_ref, s_ref, dy_ref, dz_ref, dw_ref, ds_ref):
        t, lo, hi = _segments((r, GC), rc)
        for g in range(N_GROUPS):
            cols = slice(g * GC, (g + 1) * GC)
            d, count, left, right = _pool_parts(z_ref[:, cols], g, t, lo, hi)
            dyg = dy_ref[:, cols]
            ds_ref[:, cols] = jnp.sum(dyg * _dot(d, w_ref[g], _NN), axis=0, keepdims=True)
            dlin = dyg * s_ref[:, cols]
            dw_ref[g] = _dot(d, dlin, _TN)
            dd = _dot(dlin, w_ref[g], _NT)
            dz_ref[:, cols] = (_winsum(dd / count, right, left, t, lo, hi) - dd).astype(dz_ref.dtype)

    return pl.pallas_call(
        body, name=name, grid=(1,),
        in_specs=[pl.BlockSpec((r, BR_W), lambda i: (0, OFF_POOL // BR_W)),
                  pl.BlockSpec((N_GROUPS, GC, GC), lambda i: (0, 0, 0)),
                  pl.BlockSpec((1, BR_W), lambda i: (0, 0)),
                  pl.BlockSpec((r, BR_W), lambda i: (0, 0))],
        out_specs=[pl.BlockSpec((r, BR_W), lambda i: (0, 0)),
                   pl.BlockSpec((N_GROUPS, GC, GC), lambda i: (0, 0, 0)),
                   pl.BlockSpec((1, BR_W), lambda i: (0, 0))],
        out_shape=[jax.ShapeDtypeStruct((r, BR_W), BF16), jax.ShapeDtypeStruct((N_GROUPS, GC, GC), F32),
                   jax.ShapeDtypeStruct((1, BR_W), F32)],
        compiler_params=_cparams("arbitrary"),
    )(p, pool_w, pool_scale.reshape(1, BR_W), dy)


def _f_sgu_v(pvg, lng, lnb):
    return _ln(jax.nn.gelu(pvg), lng, lnb)


def _sgu_fwd(name, p, ln_g, ln_b, sgu_w, sgu_b):
    r = p.shape[0]

    def body(pu_ref, pv_ref, g_ref, b_ref, w_ref, sb_ref, y_ref):
        vn = _f_sgu_v(pv_ref[...], g_ref[...], b_ref[...])
        u = jax.nn.gelu(pu_ref[...])
        for g in range(N_GROUPS):
            cols = slice(g * GC, (g + 1) * GC)
            s = _dot(w_ref[g], vn[:, cols], _NN) + sb_ref[g]
            y_ref[:, cols] = (u[:, cols] * s).astype(y_ref.dtype)

    return pl.pallas_call(
        body, name=name, grid=(r // SGU_CHUNK,),
        in_specs=[pl.BlockSpec((SGU_CHUNK, BR_W), lambda i: (i, OFF_U // BR_W)),
                  pl.BlockSpec((SGU_CHUNK, BR_W), lambda i: (i, OFF_VG // BR_W)),
                  pl.BlockSpec((1, BR_W), lambda i: (0, 0)), pl.BlockSpec((1, BR_W), lambda i: (0, 0)),
                  pl.BlockSpec((N_GROUPS, GC, GC), lambda i: (0, 0, 0)),
                  pl.BlockSpec((N_GROUPS, SGU_CHUNK, 1), lambda i: (0, 0, 0))],
        out_specs=pl.BlockSpec((SGU_CHUNK, BR_W), lambda i: (i, 0)),
        out_shape=jax.ShapeDtypeStruct((r, BR_W), BF16),
        compiler_params=_cparams("parallel"),
    )(p, p, ln_g.reshape(1, BR_W), ln_b.reshape(1, BR_W), sgu_w, sgu_b.reshape(N_GROUPS, SGU_CHUNK, 1))


def _sgu_bwd(name, p, ln_g, ln_b, sgu_w, sgu_b, dy):
    r = p.shape[0]

    def body(pu_ref, pv_ref, g_ref, b_ref, w_ref, sb_ref, dy_ref, dp_ref, dg_ref, db_ref, dw_ref, dsb_ref):
        i = pl.program_id(0)

        @pl.when(i == 0)
        def _():
            for ref in (dg_ref, db_ref, dw_ref, dsb_ref):
                ref[...] = jnp.zeros_like(ref)

        vn, vjp_v = jax.vjp(_f_sgu_v, pv_ref[...], g_ref[...], b_ref[...])
        u, vjp_u = jax.vjp(jax.nn.gelu, pu_ref[...])
        dy = dy_ref[...]
        du, dvn = [], []
        for g in range(N_GROUPS):
            cols = slice(g * GC, (g + 1) * GC)
            s = _dot(w_ref[g], vn[:, cols], _NN) + sb_ref[g]
            du.append(dy[:, cols] * s)
            ds = dy[:, cols] * u[:, cols]
            dsb_ref[g] += jnp.sum(ds, axis=1, keepdims=True)
            dw_ref[g] += _dot(ds, vn[:, cols], _NT)
            dvn.append(_dot(w_ref[g], ds, _TN))
        (dpu,) = vjp_u(jnp.concatenate(du, axis=1))
        dpv, dg, db = vjp_v(jnp.concatenate(dvn, axis=1))
        dp_ref[:, 0:BR_W] = dpu.astype(dp_ref.dtype)
        dp_ref[:, BR_W:2 * BR_W] = dpv.astype(dp_ref.dtype)
        dg_ref[...] += dg
        db_ref[...] += db

    vec = pl.BlockSpec((1, BR_W), lambda i: (0, 0))
    wsp = pl.BlockSpec((N_GROUPS, GC, GC), lambda i: (0, 0, 0))
    bsp = pl.BlockSpec((N_GROUPS, SGU_CHUNK, 1), lambda i: (0, 0, 0))
    return pl.pallas_call(
        body, name=name, grid=(r // SGU_CHUNK,),
        in_specs=[pl.BlockSpec((SGU_CHUNK, BR_W), lambda i: (i, OFF_U // BR_W)),
                  pl.BlockSpec((SGU_CHUNK, BR_W), lambda i: (i, OFF_VG // BR_W)),
                  vec, vec, wsp, bsp, pl.BlockSpec((SGU_CHUNK, BR_W), lambda i: (i, 0))],
        out_specs=[pl.BlockSpec((SGU_CHUNK, 2 * BR_W), lambda i: (i, 0)), vec, vec, wsp, bsp],
        out_shape=[jax.ShapeDtypeStruct((r, 2 * BR_W), BF16), jax.ShapeDtypeStruct((1, BR_W), F32),
                   jax.ShapeDtypeStruct((1, BR_W), F32), jax.ShapeDtypeStruct((N_GROUPS, GC, GC), F32),
                   jax.ShapeDtypeStruct((N_GROUPS, SGU_CHUNK, 1), F32)],
        compiler_params=_cparams("arbitrary"),
    )(p, p, ln_g.reshape(1, BR_W), ln_b.reshape(1, BR_W), sgu_w, sgu_b.reshape(N_GROUPS, SGU_CHUNK, 1), dy)


def _conv_w8(conv_w):
    return jnp.concatenate([conv_w, jnp.zeros((8 - conv_w.shape[0], conv_w.shape[1]), F32)], axis=0)


def _conv_fwd(name, p, conv_w, rc):
    r = p.shape[0]

    def body(cb_ref, cc_ref, cx_ref, w_ref, y_ref):
        t, lo, hi = _segments((r, GC), rc)
        z = cc_ref[...] * cx_ref[...]
        w = w_ref[...]
        c = _shifted(z, -1, t, lo, hi) * w[0:1] + z * w[1:2] + _shifted(z, 1, t, lo, hi) * w[2:3]
        y_ref[...] = (cb_ref[...] * c).astype(y_ref.dtype)

    nb = OFF_CB // GC
    return pl.pallas_call(
        body, name=name, grid=(N_GROUPS,),
        in_specs=[pl.BlockSpec((r, GC), lambda j: (0, nb + j)),
                  pl.BlockSpec((r, GC), lambda j: (0, nb + N_GROUPS + j)),
                  pl.BlockSpec((r, GC), lambda j: (0, nb + 2 * N_GROUPS + j)),
                  pl.BlockSpec((8, GC), lambda j: (0, j))],
        out_specs=pl.BlockSpec((r, GC), lambda j: (0, j)),
        out_shape=jax.ShapeDtypeStruct((r, BR_W), BF16),
        compiler_params=_cparams("parallel"),
    )(p, p, p, _conv_w8(conv_w))


def _conv_bwd(name, p, conv_w, dy, rc):
    r = p.shape[0]

    def body(cb_ref, cc_ref, cx_ref, w_ref, dy_ref, dcb_ref, dcc_ref, dcx_ref, dw_ref):
        t, lo, hi = _segments((r, GC), rc)
        cc, cx, w, dy = cc_ref[...], cx_ref[...], w_ref[...], dy_ref[...]
        z = cc * cx
        zp, zn = _shifted(z, -1, t, lo, hi), _shifted(z, 1, t, lo, hi)
        dcb_ref[...] = (dy * (zp * w[0:1] + z * w[1:2] + zn * w[2:3])).astype(dcb_ref.dtype)
        dc = dy * cb_ref[...]
        dw_ref[...] = jnp.concatenate(
            [jnp.sum(dc * zp, axis=0, keepdims=True), jnp.sum(dc * z, axis=0, keepdims=True),
             jnp.sum(dc * zn, axis=0, keepdims=True), jnp.zeros((5, GC), F32)], axis=0)
        dz = dc * w[1:2] + _shifted(dc, 1, t, lo, hi) * w[0:1] + _shifted(dc, -1, t, lo, hi) * w[2:3]
        dcc_ref[...] = (dz * cx).astype(dcc_ref.dtype)
        dcx_ref[...] = (dz * cc).astype(dcx_ref.dtype)

    nb = OFF_CB // GC
    return pl.pallas_call(
        body, name=name, grid=(N_GROUPS,),
        in_specs=[pl.BlockSpec((r, GC), lambda j: (0, nb + j)),
                  pl.BlockSpec((r, GC), lambda j: (0, nb + N_GROUPS + j)),
                  pl.BlockSpec((r, GC), lambda j: (0, nb + 2 * N_GROUPS + j)),
                  pl.BlockSpec((8, GC), lambda j: (0, j)),
                  pl.BlockSpec((r, GC), lambda j: (0, j))],
        out_specs=[pl.BlockSpec((r, GC), lambda j: (0, j))] * 3 + [pl.BlockSpec((8, GC), lambda j: (0, j))],
        out_shape=[jax.ShapeDtypeStruct((r, BR_W), BF16)] * 3 + [jax.ShapeDtypeStruct((8, BR_W), F32)],
        compiler_params=_cparams("parallel"),
    )(p, p, p, _conv_w8(conv_w), dy)


def _rope_tables(rc, n):
    rows = n // GRID_W
    row = jnp.repeat(jnp.arange(rows), GRID_W).astype(F32)
    col = jnp.tile(jnp.arange(GRID_W), rows).astype(F32)
    inv = ROPE_THETA ** (-jnp.arange(0, ROPE_AXIS_DIM, 2, dtype=F32) / ROPE_AXIS_DIM)
    ang_r, ang_c = row[:, None] * inv, col[:, None] * inv
    cos = jnp.concatenate([jnp.cos(ang_r), jnp.cos(ang_r), jnp.cos(ang_c), jnp.cos(ang_c)], axis=1)
    sin = jnp.concatenate([-jnp.sin(ang_r), jnp.sin(ang_r), -jnp.sin(ang_c), jnp.sin(ang_c)], axis=1)
    cos = jnp.concatenate([jnp.ones((rc, HEAD_DIM), F32), cos], axis=0)
    sin = jnp.concatenate([jnp.zeros((rc, HEAD_DIM), F32), sin], axis=0)
    return cos, sin


MOD_NAMES = ("sh1", "sc1", "g1", "sh2", "sc2", "g2")


def _local_step(xin, target, mod, comm, sp, rc, alpha):
    def carrying(fn):
        def call(name, *args, **kw):
            rider = comm.rider(name)
            if rider is None:
                return fn(name, *args, **kw)
            res, r_outs = fn(name, *args, rider=rider, **kw)
            comm.deliver(name, r_outs)
            return res
        return call

    mm, rows, attn_fwd, attn_bwd = carrying(_mm), carrying(_rows), carrying(_attn_fwd), carrying(_attn_bwd)
    mm_fused = carrying(_mm_fused)
    r, d = xin.shape
    n_layers = mod.shape[0]
    tm_n, tm_w = 256, 128
    nbc_n, nbc_w = rc // tm_n, rc // tm_w
    cos, sin = _rope_tables(rc, r - rc)
    mp = mod.reshape(n_layers, 2, 6, 1, d)
    mods = [{nm: mp[i, :, j] for j, nm in enumerate(MOD_NAMES)} for i in range(n_layers)]
    f_ln_mod, f_ln_last = _make_f_ln(alpha, True), _make_f_ln(alpha, False)

    def whole(arr, roff=0):
        return (arr, 0, arr.shape[1], roff)

    (hb,) = rows("mod_in", _f_mod, r, tm_n, nbc_n, [whole(xin)], [mods[0]["sc1"], mods[0]["sh1"]],
                 [(r, d, BF16, 0)], [])
    saved = []
    x = xin
    for i in range(n_layers):
        last = i == n_layers - 1
        w, s, m = functools.partial(comm.weight, i), sp[i], mods[i]
        sv = {"x": x, "hb": hb}
        p = mm(f"l{i}_in", hb, w("in_t"), "nt", F32)
        q, k, v = rows(f"l{i}_prep", _f_prep, r, tm_n, nbc_n,
                       [(p, 0, OFF_POOL, 0), whole(cos), whole(sin)], [_typed(s["q_norm_g"]), _typed(s["k_norm_g"])],
                       [(r, Q_W, BF16, 0), (r, KV_W, BF16, 0), (r, KV_W, BF16, 0)], [])
        ys = [attn_fwd(f"l{i}_attn", q, k, v, rc, not last),
              _pool_fwd(f"l{i}_pool", p, s["pool_w"], s["pool_scale"], rc),
              _sgu_fwd(f"l{i}_sgu", p, s["sgu_ln_g"], s["sgu_ln_b"], s["sgu_w"], s["sgu_b"]),
              _conv_fwd(f"l{i}_conv", p, s["conv_w"], rc)]
        ts = [mm(f"l{i}_br{kk}", ys[kk], w(f"br{kk}"), "nt", BF16) for kk in range(N_BRANCH)]
        gpre = mm(f"l{i}_gate", hb, w("gate_t"), "nt", BF16)
        (mg,) = rows(f"l{i}_merge", _f_gate, r, tm_w, nbc_w, [whole(gpre)] + [whole(t) for t in ts],
                     [_typed(s["b_gate"])], [(r, d, BF16, 0)], [])
        o = mm(f"l{i}_o", mg, w("o"), "nn", F32)
        x1, h2b = rows(f"l{i}_ln1", f_ln_mod, r, tm_n, nbc_n, [whole(x), whole(o)],
                       [m["g1"], _typed(s["ln1_g"]), _typed(s["ln1_b"]), m["sc2"], m["sh2"]],
                       [(r, d, F32, 0), (r, d, BF16, 0)], [])
        af, bf, f = mm_fused(f"l{i}_ffgu", h2b, [w("ffg_t"), w("ffu_t")],
                             lambda prods, _: (prods[0], prods[1], _f_swiglu(prods[0], prods[1])), [], [BF16, BF16, BF16])
        o2 = mm(f"l{i}_ffd", f, w("ffd"), "nn", F32)
        if last:
            (x2,) = rows(f"l{i}_ln2", f_ln_last, r, tm_n, nbc_n, [whole(x1), whole(o2)],
                         [m["g2"], _typed(s["ln2_g"]), _typed(s["ln2_b"])], [(r, d, F32, 0)], [])
            hb = None
        else:
            nx = mods[i + 1]
            x2, hb = rows(f"l{i}_ln2", f_ln_mod, r, tm_n, nbc_n, [whole(x1), whole(o2)],
                          [m["g2"], _typed(s["ln2_g"]), _typed(s["ln2_b"]), nx["sc1"], nx["sh1"]],
                          [(r, d, F32, 0), (r, d, BF16, 0)], [])
        sv.update(p=p, gpre=gpre, q=q, k=k, v=v, ys=ys, ts=ts, mg=mg, o=o, x1=x1, h2b=h2b, af=af, bf=bf, f=f, o2=o2)
        saved.append(sv)
        x = x2

    lat = jnp.concatenate([jnp.zeros((1, 1, 128), F32), jnp.ones((1, 1, 128), F32)], axis=0)

    def f_loss(xb, tb, msk):
        diff = (xb - tb) * msk[:, 0:1]
        part = jnp.sum(jnp.mean(jnp.square(diff), axis=-1, keepdims=True), axis=0, keepdims=True)
        return diff * (1.0 / d), jnp.broadcast_to(part, (1, 128))

    dx_direct, loss_acc = rows("loss", f_loss, r, tm_n, nbc_n, [whole(x), whole(target, nbc_n)], [lat],
                               [(r, d, F32, 0)], [128])
    loss = 0.5 * loss_acc[1, 0, 0]

    dmods = [dict() for _ in range(n_layers)]
    dsp = [dict() for _ in range(n_layers)]
    dh = None

    def small_done(j):
        ds = dict(dsp[j])
        for nm in ("ln1_g", "ln1_b", "ln2_g", "ln2_b", "b_gate", "q_norm_g", "k_norm_g"):
            ds[nm] = ds[nm][0, 0]
        for nm in ("pool_scale", "sgu_ln_g", "sgu_ln_b"):
            ds[nm] = ds[nm].reshape(-1)
        ds["sgu_b"] = ds["sgu_b"].reshape(N_GROUPS, SGU_CHUNK)
        comm.small_ready(j, jnp.concatenate([dmods[j][nm][:, 0, :] for nm in MOD_NAMES], axis=-1), ds)

    for i in reversed(range(n_layers)):
        last = i == n_layers - 1
        w, s, m, sv = functools.partial(comm.weight, i), sp[i], mods[i], saved[i]
        dm, dw, ds = dmods[i], {}, dsp[i]
        ln2 = [m["g2"], _typed(s["ln2_g"]), _typed(s["ln2_b"])]
        if last:
            res = rows(f"l{i}_ln2_bwd", _vjp_fn(f_ln_last, 2, 1), r, tm_n, nbc_n,
                       [whole(sv["x1"]), whole(sv["o2"]), whole(dx_direct)], ln2,
                       [(r, d, F32, 0), (r, d, BF16, 0)], [d, (d,), (d,)])
            dx1, do2, dm["g2"], dlg, dlb = res
        else:
            nx = mods[i + 1]
            res = rows(f"l{i}_ln2_bwd", _vjp_fn(f_ln_mod, 2, 2), r, tm_n, nbc_n,
                       [whole(sv["x1"]), whole(sv["o2"]), whole(dx_direct), whole(dh)],
                       ln2 + [nx["sc1"], nx["sh1"]],
                       [(r, d, F32, 0), (r, d, BF16, 0)], [d, (d,), (d,), d, d])
            dx1, do2, dm["g2"], dlg, dlb, dmods[i + 1]["sc1"], dmods[i + 1]["sh1"] = res
            small_done(i + 1)
        ds["ln2_g"], ds["ln2_b"] = dlg, dlb
        dab, dbb = mm_fused(f"l{i}_dF", do2, [w("ffd")],
                            lambda prods, tiles: jax.vjp(_f_swiglu, *tiles)[1](prods[0]), [sv["af"], sv["bf"]],
                            [BF16, BF16])
        comm.grads(i, {"ffd": mm(f"l{i}_dWffd", sv["f"], do2, "tn", BF16)})
        comm.grads(i, {"ffg_t": mm(f"l{i}_dWffg", dab, sv["h2b"], "tn", BF16)})
        comm.grads(i, {"ffu_t": mm(f"l{i}_dWffu", dbb, sv["h2b"], "tn", BF16)})
        dh2 = mm(f"l{i}_dh2a", dab, w("ffg_t"), "nn", F32)
        dh2 = mm(f"l{i}_dh2b", dbb, w("ffu_t"), "nn", F32, acc=dh2)
        res = rows(f"l{i}_ln1_bwd", _vjp_fn(f_ln_mod, 2, 2), r, tm_n, nbc_n,
                   [whole(sv["x"]), whole(sv["o"]), whole(dx1), whole(dh2)],
                   [m["g1"], _typed(s["ln1_g"]), _typed(s["ln1_b"]), m["sc2"], m["sh2"]],
                   [(r, d, F32, 0), (r, d, BF16, 0)], [d, (d,), (d,), d, d])
        dx_direct, do, dm["g1"], ds["ln1_g"], ds["ln1_b"], dm["sc2"], dm["sh2"] = res
        dmg = mm(f"l{i}_dMg", do, w("o"), "nt", F32)
        comm.grads(i, {"o": mm(f"l{i}_dWo", sv["mg"], do, "tn", BF16)})
        res = rows(f"l{i}_merge_bwd", _vjp_fn(_f_gate, 5, 1), r, tm_w, nbc_w,
                   [whole(sv["gpre"])] + [whole(t) for t in sv["ts"]] + [whole(dmg)], [_typed(s["b_gate"])],
                   [(r, N_BRANCH * d, BF16, 0)] + [(r, d, BF16, 0)] * N_BRANCH, [(N_BRANCH * d,)])
        dgb, dts, ds["b_gate"] = res[0], res[1:1 + N_BRANCH], res[1 + N_BRANCH]
        comm.grads(i, {"gate_t": mm(f"l{i}_dWgate", dgb, sv["hb"], "tn", BF16)})
        dys = [mm(f"l{i}_dY{kk}", dts[kk], w(f"br{kk}"), "nn", F32) for kk in range(N_BRANCH)]
        for kk in range(N_BRANCH):
            comm.grads(i, {f"br{kk}": mm(f"l{i}_dWbr{kk}", dts[kk], sv["ys"][kk], "tn", BF16)})
        dq, dk, dv = attn_bwd(f"l{i}_attn_bwd", sv["q"], sv["k"], sv["v"], dys[0], rc, not last)
        res = rows(f"l{i}_prep_bwd", _vjp_fn(_f_prep, 3, 3, keep=(0, 3, 4)), r, tm_n, nbc_n,
                   [(sv["p"], 0, OFF_POOL, 0), whole(cos), whole(sin), whole(dq), whole(dk), whole(dv)],
                   [_typed(s["q_norm_g"]), _typed(s["k_norm_g"])],
                   [(r, OFF_POOL, BF16, 0)], [(HEAD_DIM,), (HEAD_DIM,)])
        dp_qkv, ds["q_norm_g"], ds["k_norm_g"] = res
        dp_pool, ds["pool_w"], ds["pool_scale"] = _pool_bwd(f"l{i}_pool_bwd", sv["p"], s["pool_w"], s["pool_scale"],
                                                            dys[1], rc)
        dp_sgu, ds["sgu_ln_g"], ds["sgu_ln_b"], ds["sgu_w"], ds["sgu_b"] = _sgu_bwd(
            f"l{i}_sgu_bwd", sv["p"], s["sgu_ln_g"], s["sgu_ln_b"], s["sgu_w"], s["sgu_b"], dys[2])
        dp_cb, dp_cc, dp_cx, dcw = _conv_bwd(f"l{i}_conv_bwd", sv["p"], s["conv_w"], dys[3], rc)
        ds["conv_w"] = dcw[0:3]
        dpb = jnp.concatenate([dp_qkv, dp_pool, dp_sgu, dp_cb, dp_cc, dp_cx], axis=1)
        comm.grads(i, {"in_t": mm(f"l{i}_dWin", dpb, sv["hb"], "tn", BF16)})
        dh = mm(f"l{i}_dhb_a", dpb, w("in_t"), "nn", F32)
        dh = mm(f"l{i}_dhb_b", dgb, w("gate_t"), "nn", F32, acc=dh)

    def f_mod_bwd(xb, ddir, dhb, sc, sh):
        _, vjp = jax.vjp(_f_mod, xb, sc, sh)
        dxb, dsc, dsh = vjp(dhb)
        return dxb + ddir, dsc, dsh

    grad_x, dmods[0]["sc1"], dmods[0]["sh1"] = rows(
        "mod_in_bwd", f_mod_bwd, r, tm_n, nbc_n, [whole(xin), whole(dx_direct), whole(dh)],
        [mods[0]["sc1"], mods[0]["sh1"]], [(r - rc, d, F32, nbc_n)], [d, d])
    small_done(0)
    return loss, grad_x


def _direct_rider(src):
    def peers():
        mx, my, mc = [lax.axis_index(a) for a in MESH_AXES]
        out = []
        for kk in range(1, N_DEV):
            px = 1 - mx if kk & 4 else mx
            py = 1 - my if kk & 2 else my
            pc = 1 - mc if kk & 1 else mc
            out.append(((px, py, pc), 4 * px + 2 * py + pc))
        return 4 * mx + 2 * my + mc, out

    def start(ins, outs, sems):
        send, recv, loc = sems
        me, others = peers()
        pltpu.make_async_copy(ins[0], outs[0].at[me], loc.at[0]).start()
        for j, (peer, _) in enumerate(others):
            _rcopy(ins[0], outs[0].at[me], send, recv, j, peer).start()

    def finish(ins, outs, sems):
        send, recv, loc = sems
        me, others = peers()
        for j, (peer, peer_l) in enumerate(others):
            cp = _rcopy(ins[0], outs[0].at[peer_l], send, recv, j, peer)
            cp.wait_recv()
            cp.wait_send()
        pltpu.make_async_copy(ins[0], outs[0].at[me], loc.at[0]).wait()

    return _Rider([src], [jax.ShapeDtypeStruct((N_DEV,) + src.shape, src.dtype)],
                  _sem_scratch(N_DEV - 1, N_DEV - 1, 1), start, finish)


def _mesh_place():
    mx, my, mc = [lax.axis_index(a) for a in MESH_AXES]
    chips = [(1 - mx, my), (mx, 1 - my), (1 - mx, 1 - my)]

    def lid(px, py, pc):
        return 4 * px + 2 * py + pc

    return (mx, my, mc), (mx, my, 1 - mc), chips, lid


def _rcopy(src, dst, send_sems, recv_sems, k, to):
    return pltpu.make_async_remote_copy(src_ref=src, dst_ref=dst, send_sem=send_sems.at[k], recv_sem=recv_sems.at[k],
                                        device_id=to, device_id_type=pl.DeviceIdType.MESH)


def _sem_scratch(*sizes):
    return [pltpu.SemaphoreType.DMA((s,)) for s in sizes]


def _gather_rider(src, rows, buf=None):
    r0, r1 = rows
    win = pl.ds(r0, r1 - r0)

    def start(ins, outs, sems):
        send, recv, loc = sems
        (mx, my, mc), sib, chips, lid = _mesh_place()
        mine, dst = ins[0].at[win], outs[0].at[lid(mx, my, mc), win]
        pltpu.make_async_copy(mine, dst, loc.at[0]).start()
        _rcopy(mine, dst, send, recv, 0, sib).start()
        for j, chip in enumerate(chips):
            _rcopy(mine, dst, send, recv, 1 + j, (*chip, mc)).start()

    def finish(ins, outs, sems):
        send, recv, loc = sems
        (mx, my, mc), sib, chips, lid = _mesh_place()
        mine, dst = ins[0].at[win], outs[0].at[lid(mx, my, mc), win]
        for j, chip in enumerate(chips):
            blk = outs[0].at[lid(*chip, mc), win]
            _rcopy(mine, blk, send, recv, 1 + j, (*chip, mc)).wait_recv()
            _rcopy(blk, blk, send, recv, 4 + j, sib).start()
        _rcopy(mine, outs[0].at[lid(*sib), win], send, recv, 0, sib).wait_recv()
        for j, chip in enumerate(chips):
            _rcopy(mine, outs[0].at[lid(*chip, 1 - mc), win], send, recv, 4 + j, sib).wait_recv()
        for t in range(7):
            _rcopy(mine, dst, send, recv, t, sib).wait_send()
        pltpu.make_async_copy(mine, dst, loc.at[0]).wait()

    out_shape = jax.ShapeDtypeStruct((N_DEV,) + src.shape, src.dtype)
    if buf is None:
        return _Rider([src], [out_shape], _sem_scratch(7, 7, 1), start, finish)
    return _Rider([src, buf], [out_shape], _sem_scratch(7, 7, 1), start, finish, aliases={1: 0})


def _sibling_rider(part):
    def start(ins, outs, sems):
        send, recv = sems
        (mx, my, mc), sib, chips, lid = _mesh_place()
        for t, slab in enumerate([lid(*sib)] + [lid(*chip, 1 - mc) for chip in chips]):
            _rcopy(ins[0].at[slab], outs[0].at[t], send, recv, t, sib).start()

    def finish(ins, outs, sems):
        send, recv = sems
        _, sib, _, _ = _mesh_place()
        for t in range(4):
            cp = _rcopy(ins[0].at[0], outs[0].at[t], send, recv, t, sib)
            cp.wait_recv()
            cp.wait_send()

    return _Rider([part], [jax.ShapeDtypeStruct((4,) + part.shape[1:], part.dtype)], _sem_scratch(4, 4), start, finish)


def _chips_rider(pair, rows, buf=None):
    r0, r1 = rows
    win = pl.ds(r0, r1 - r0)

    def start(ins, outs, sems):
        send, recv = sems
        (mx, my, mc), sib, chips, lid = _mesh_place()
        for j, chip in enumerate(chips):
            _rcopy(ins[0].at[j, win], outs[0].at[j, win], send, recv, j, (*chip, mc)).start()

    def finish(ins, outs, sems):
        send, recv = sems
        (mx, my, mc), sib, chips, lid = _mesh_place()
        for j, chip in enumerate(chips):
            cp = _rcopy(ins[0].at[j, win], outs[0].at[j, win], send, recv, j, (*chip, mc))
            cp.wait_recv()
            cp.wait_send()

    out_shape = jax.ShapeDtypeStruct(pair.shape, pair.dtype)
    if buf is None:
        return _Rider([pair], [out_shape], _sem_scratch(3, 3), start, finish)
    return _Rider([pair, buf], [out_shape], _sem_scratch(3, 3), start, finish, aliases={1: 0})


def _run_rider(name, rider):
    n_in, n_out = len(rider.inputs), len(rider.out_shapes)

    def body(*refs):
        ins, outs, sems = refs[:n_in], refs[n_in:n_in + n_out], refs[n_in + n_out:]
        rider.start(ins, outs, sems)
        rider.finish(ins, outs, sems)

    any_spec = pl.BlockSpec(memory_space=pl.ANY)
    res = pl.pallas_call(body, name=name, in_specs=[any_spec] * n_in, out_specs=[any_spec] * n_out,
                         out_shape=rider.out_shapes, scratch_shapes=rider.scratch,
                         input_output_aliases=rider.aliases)(*rider.inputs)
    return list(res)


def _slab_ids():
    (mx, my, mc), _, chips, lid = _mesh_place()
    return jnp.stack([lid(*chip, mc) for chip in chips] + [lid(mx, my, mc)]).astype(jnp.int32)


def _pair_sum(name, part, rsib, ids):
    _, n, k = part.shape
    tr = _row_tile(n, 512, 16)

    def body(ids_ref, p_ref, r_ref, o_ref):
        o_ref[...] = (p_ref[...].astype(F32) + r_ref[...].astype(F32)).astype(o_ref.dtype)

    grid_spec = pltpu.PrefetchScalarGridSpec(
        num_scalar_prefetch=1, grid=(3, n // tr),
        in_specs=[pl.BlockSpec((None, tr, k), lambda j, i, ids: (ids[j], i, 0)),
                  pl.BlockSpec((None, tr, k), lambda j, i, ids: (1 + j, i, 0))],
        out_specs=pl.BlockSpec((None, tr, k), lambda j, i, ids: (j, i, 0)))
    return pl.pallas_call(body, name=name, grid_spec=grid_spec, out_shape=jax.ShapeDtypeStruct((3, n, k), part.dtype),
                          compiler_params=_cparams("parallel", "parallel"))(ids, part, rsib)


def _sum5(name, part, rsib, rici, ids, layer, stacked, rider=None):
    _, n, k = part.shape
    tr = _row_tile(n, 512, 16)
    first = isinstance(stacked, int)

    def body(ids_ref, p_ref, r_ref, c_ref, *rest):
        acc = p_ref[...].astype(F32) + r_ref[...].astype(F32)
        for j in range(3):
            acc = acc + c_ref[j].astype(F32)
        rest[-1][...] = acc

    in_specs = [pl.BlockSpec((None, tr, k), lambda i, ids: (ids[3], i, 0)),
                pl.BlockSpec((None, tr, k), lambda i, ids: (0, i, 0)),
                pl.BlockSpec((3, tr, k), lambda i, ids: (0, i, 0))] + ([] if first else [pl.BlockSpec(memory_space=pl.ANY)])
    n_layers = stacked if first else stacked.shape[0]
    outs, r_outs = _pcall(name, body, (n // tr,), in_specs, [pl.BlockSpec((None, tr, k), lambda i, ids: (layer, i, 0))],
                          [jax.ShapeDtypeStruct((n_layers, n, k), F32)],
                          (part, rsib, rici) + (() if first else (stacked,)), ("parallel",), rider=rider, prefetch=ids,
                          aliases={} if first else {3: 0})
    return outs[0] if rider is None else (outs[0], r_outs)


W_KEYS = ("in_t", "br0", "br1", "br2", "br3", "gate_t", "o", "ffg_t", "ffu_t", "ffd")
SUMS_TRANSPOSED_LATER = ("gate_t", "br0", "br1", "br2", "br3")


CARRIER_US = {"mod_in": 12, "in": 55, "gate": 95, "prep": 19, "attn": 112, "br0": 15, "merge": 50, "o": 25, "ln1": 27,
              "ffgu": 135, "ffd": 73, "ln2": 27, "loss": 20, "ln2_bwd": 44, "dF": 80,
              "dWffd": 64, "dh2a": 75, "dh2b": 75, "dWffg": 64, "dWffu": 64, "ln1_bwd": 44,
              "dMg": 25, "dWo": 25, "merge_bwd": 80, "dY0": 14, "dWbr0": 15, "attn_bwd": 195, "prep_bwd": 28,
              "dWin": 54, "dWgate": 95, "dhb_a": 64, "dhb_b": 115}
ICI_US_PER_MIB = 45.0
D2D_US_PER_MIB = 6.8
MIN_CHUNK_US = 10.0
CARRIER_FILL = 1.15


class _Comm:
    def __init__(self, wsrc):
        self.wsrc = wsrc
        self.n_layers = len(wsrc)
        self.queue = []
        self.riding = {}
        self.buf, self.left = {}, {}
        self.part, self.rsib, self.pair = {}, {}, {}
        self.ids = _slab_ids()
        for i in range(self.n_layers):
            for k in W_KEYS:
                self._push_chunks("gather", ("w", i, k), wsrc[i][k].shape, wsrc[i][k].dtype)

    def _push_chunks(self, kind, item, shape, dtype):
        n, k = shape[-2], shape[-1]
        us = n * k * jnp.dtype(dtype).itemsize / 2 ** 20 * ICI_US_PER_MIB
        pieces = max(1, int(us // MIN_CHUNK_US))
        while n % (16 * pieces):
            pieces -= 1
        step = n // pieces
        self.left[item] = pieces
        for c in range(pieces):
            self.queue.append(dict(kind=kind, item=item, rows=(c * step, (c + 1) * step), us=us / pieces))

    @staticmethod
    def _merge(units, u):
        v = units[-1] if units else None
        if not (v and v["item"] == u["item"] and v["kind"] == u["kind"] and u["rows"] and v["rows"][1] == u["rows"][0]):
            return False
        v.update(rows=(v["rows"][0], u["rows"][1]), us=v["us"] + u["us"], count=v.get("count", 1) + u.get("count", 1))
        return True

    def _unit_rider(self, u):
        item = u["item"]
        if u["kind"] == "gather":
            src = self.wsrc[item[1]][item[2]] if item[0] == "w" else self.part[item]
            return _gather_rider(src, u["rows"], self.buf.get(item))
        if u["kind"] == "sibling":
            return _sibling_rider(self.part[item])
        return _chips_rider(self.pair[item], u["rows"], self.buf.get(item))

    def _done(self, u, out):
        item = u["item"]
        if u["kind"] == "sibling":
            self.rsib[item] = out
            self.pair[item] = _pair_sum(f"pair_l{item[1]}_{item[2]}", self.part[item], out, self.ids)
            self._push_chunks("chips", item, self.pair[item].shape, self.pair[item].dtype)
            return
        self.buf[item] = out
        self.left[item] -= u.get("count", 1)

    def _send(self, name, units, call):
        outs = call(_compose([self._unit_rider(u) for u in units]))
        for u, o in zip(units, outs):
            self._done(u, o)

    def exchange(self, name, src, budget_us):
        units = self._take(budget_us)
        outs = _run_rider(name, _compose([_direct_rider(src)] + [self._unit_rider(u) for u in units]))
        for u, o in zip(units, outs[1:]):
            self._done(u, o)
        return outs[0]

    def rider(self, name, budget_us=None):
        budget = CARRIER_US.get(name.split("_", 1)[1] if name[0] == "l" and name[1].isdigit() else name, 0) \
            if budget_us is None else budget_us
        units = self._take(budget)
        if not units:
            return None
        self.riding[name] = units
        return _compose([self._unit_rider(u) for u in units])

    def _take(self, budget):
        units, used = [], 0.0
        while self.queue and used + self.queue[0]["us"] <= CARRIER_FILL * budget:
            u = self.queue[0]
            if not self._merge(units, u):
                if any(v["item"] == u["item"] for v in units):
                    break
                units.append(dict(u))
            used += u["us"]
            del self.queue[0]
        return units

    def deliver(self, name, outs):
        for u, o in zip(self.riding.pop(name), outs):
            self._done(u, o)

    def _flush(self, item, kinds):
        hits = [p for p, u in enumerate(self.queue) if u["item"] == item and u["kind"] in kinds]
        if not hits:
            return
        prefix = self.queue[:hits[-1] + 1]
        del self.queue[:hits[-1] + 1]
        units = []
        for u in prefix:
            if not self._merge(units, u):
                units.append(dict(u))
        tag = "_".join(str(t) for t in item) + "_" + kinds[0]
        batches = [[]]
        for u in units:
            if any(v["item"] == u["item"] for v in batches[-1]):
                batches.append([])
            batches[-1].append(u)
        for b, batch in enumerate(batches):
            self._send(None, batch, functools.partial(_run_rider, f"alone_{tag}_{b}"))

    def begin(self):
        self._flush(("w", 0, "in_t"), ("gather",))

    def weight(self, i, k):
        item = ("w", i, k)
        self._flush(item, ("gather",))
        o = self.buf[item]
        return o.reshape(-1, o.shape[-1])

    def grads(self, i, group):
        for k, g in group.items():
            item = ("g", i, k)
            self.part[item] = g.reshape(N_DEV, g.shape[0] // N_DEV, g.shape[1])
            us = g.size // N_DEV * g.dtype.itemsize / 2 ** 20 * D2D_US_PER_MIB
            self.queue.insert(0, dict(kind="sibling", item=item, rows=None, us=us))

    def total(self, k):
        out = self.n_layers
        for i in range(self.n_layers):
            item = ("g", i, k)
            self._flush(item, ("sibling",))
            self._flush(item, ("chips",))
            name = f"sum_l{i}_{k}"
            rider = self.rider(name, budget_us=self.part[item][0].size / 1.06e5) if k in SUMS_TRANSPOSED_LATER else None
            out = _sum5(name, self.part[item], self.rsib[item], self.buf[item], self.ids, i, out, rider=rider)
            if rider is not None:
                out, r_outs = out
                self.deliver(name, r_outs)
        return out

    def small_ready(self, i, dmod, ds):
        parts = [dmod[0], dmod[1]] + [ds[nm] for nm in LAYER_SMALL + ("conv_w",)]
        self.small_shapes = [p.shape for p in parts]
        self.gather_small(f"lat{i}", _pack([dmod[1]]))
        self.gather_small(f"small{i}", _pack(parts))

    def gather_small(self, name, arr):
        item = ("s", name)
        self.part[item] = arr
        waiting, self.queue = self.queue, []
        self._push_chunks("gather", item, arr.shape, arr.dtype)
        self.queue += waiting

    def gathered(self, name):
        item = ("s", name)
        self._flush(item, ("gather",))
        return self.buf[item]


def _row_tile(n, pref, mult):
    best = None
    t = mult
    while t <= min(n, pref):
        if n % t == 0:
            best = t
        t += mult
    return best if best is not None else n


def _sum8(name, slabs):
    _, n, k = slabs.shape
    tr = _row_tile(n, 128, 16)

    def body(s_ref, o_ref):
        acc = s_ref[0].astype(F32)
        for j in range(1, N_DEV):
            acc = acc + s_ref[j].astype(F32)
        o_ref[...] = acc

    return pl.pallas_call(
        body, name=name, grid=(n // tr,),
        in_specs=[pl.BlockSpec((N_DEV, tr, k), lambda i: (0, i, 0))],
        out_specs=pl.BlockSpec((tr, k), lambda i: (i, 0)),
        out_shape=jax.ShapeDtypeStruct((n, k), F32),
        compiler_params=_cparams("parallel"),
    )(slabs)


def _adamw(name, w, g, m, v, rider=None):
    n, k = w.shape[-2:]
    tr = _row_tile(n, 256, 8)

    def body(w_ref, g_ref, m_ref, v_ref, d_ref, m2_ref, v2_ref):
        gv = g_ref[...]
        m2 = ADAM_B1 * m_ref[...] + (1.0 - ADAM_B1) * gv
        v2 = ADAM_B2 * v_ref[...] + (1.0 - ADAM_B2) * jnp.square(gv)
        m_hat = m2 / (1.0 - ADAM_B1 ** ADAM_STEP)
        v_hat = v2 / (1.0 - ADAM_B2 ** ADAM_STEP)
        d_ref[...] = -ADAM_LR * (m_hat / (jnp.sqrt(v_hat) + ADAM_EPS) + ADAM_WD * w_ref[...])
        m2_ref[...] = m2
        v2_ref[...] = v2

    if w.ndim == 2:
        grid, spec = (n // tr,), pl.BlockSpec((tr, k), lambda i: (i, 0))
    else:
        grid, spec = (w.shape[0], n // tr), pl.BlockSpec((None, tr, k), lambda l, i: (l, i, 0))
    outs, r_outs = _pcall(name, body, grid, [spec] * 4, [spec] * 3, [jax.ShapeDtypeStruct(w.shape, F32)] * 3,
                          (w, g, m, v), ("parallel",) * len(grid), rider=rider)
    return outs if rider is None else (outs, r_outs)


def _pack(arrs):
    flat = jnp.concatenate([a.reshape(-1).astype(F32) for a in arrs])
    pad = (-flat.shape[0]) % 2048
    if pad:
        flat = jnp.concatenate([flat, jnp.zeros((pad,), F32)])
    return flat.reshape(-1, 128)


def _unpack(packed, shapes):
    flat = packed.reshape(-1)
    out, off = [], 0
    for shp in shapes:
        size = math.prod(shp)
        out.append(flat[off:off + size].reshape(shp))
        off += size
    return out


WEIGHT_NAMES = ("c_ctx", "w_ada", "b_ada", "w_in", "q_norm_g", "k_norm_g", "pool_w", "pool_scale", "sgu_ln_g",
                "sgu_ln_b", "sgu_w", "sgu_b", "conv_w", "w_br_attn", "w_br_pool", "w_br_sgu", "w_br_conv", "w_gate",
                "b_gate", "w_o", "ln1_g", "ln1_b", "w_ff_gate", "w_ff_up", "w_ff_down", "ln2_g", "ln2_b")
COL_SHARDED = {"w_in": "in_t", "w_gate": "gate_t", "w_ff_gate": "ffg_t", "w_ff_up": "ffu_t",
               "w_br_attn": "br0", "w_br_pool": "br1", "w_br_sgu": "br2", "w_br_conv": "br3"}
ROW_SHARDED = {"w_o": "o", "w_ff_down": "ffd"}
LAYER_SMALL = ("q_norm_g", "k_norm_g", "pool_w", "pool_scale", "sgu_ln_g", "sgu_ln_b", "sgu_w", "sgu_b", "b_gate",
               "ln1_g", "ln1_b", "ln2_g", "ln2_b")
SMALL_ORDER = ("c_ctx", "b_ada") + LAYER_SMALL + ("conv_w",)


def _train_step(a):
    n_layers, d = a["w_in"].shape[0], a["x"].shape[-1]
    rc = a["ctx"].shape[1]
    alpha = (2 * n_layers) ** 0.25
    mx, my, mc = [lax.axis_index(ax) for ax in MESH_AXES]
    me = 4 * mx + 2 * my + mc
    ada_w = a["w_ada"].shape[-1]
    cw_loc = a["conv_w"].shape[-1]

    comm = _Comm([{**{key: jnp.swapaxes(a[nm], 1, 2)[i].astype(BF16) for nm, key in COL_SHARDED.items()},
                   **{key: a[nm][i].astype(BF16) for nm, key in ROW_SHARDED.items()}} for i in range(n_layers)])

    def carried(name, *args, budget_us, **kw):
        rider = comm.rider(name, budget_us=budget_us)
        if rider is None:
            return _mm(name, *args, **kw)
        res, r_outs = _mm(name, *args, rider=rider, **kw)
        comm.deliver(name, r_outs)
        return res

    n_c, n_cw = d, n_layers * 3 * cw_loc
    got = comm.exchange("gather_cond", _pack([a["c"], a["conv_w"]]), budget_us=12).reshape(N_DEV, -1)
    c_all = got[:, :n_c]
    conv_w = got[:, n_c:n_c + n_cw].reshape(N_DEV, n_layers, 3, cw_loc).transpose(1, 2, 0, 3).reshape(n_layers, 3, -1)
    cond = jnp.concatenate([c_all, a["c_ctx"][None], jnp.zeros((16 - N_DEV - 1, d), F32)], axis=0)
    sil, sil_vjp = jax.vjp(jax.nn.silu, cond)
    sil = sil.astype(BF16)

    mod_cols = jnp.concatenate([carried(f"ada{i}", sil, a["w_ada"][i], "nn", F32, budget_us=20)
                                for i in range(n_layers)], axis=0)
    got = comm.exchange("gather_mod", mod_cols, budget_us=22)
    mod_all = got.reshape(N_DEV, n_layers, 16, ada_w).transpose(1, 2, 0, 3).reshape(n_layers, 16, -1)
    mod_all = mod_all + a["b_ada"][:, None, :]
    mod = jnp.stack([mod_all[:, N_DEV], lax.dynamic_index_in_dim(mod_all, me, axis=1, keepdims=False)], axis=1)
    comm.begin()
    sp = [{nm: a[nm][i] for nm in LAYER_SMALL} for i in range(n_layers)]
    for i in range(n_layers):
        sp[i]["conv_w"] = conv_w[i]

    xin = jnp.concatenate([a["ctx"][0], a["x"][0]], axis=0)
    loss_l, grad_x = _local_step(xin, a["loss_target"][0], mod, comm, sp, rc, alpha)
    loss = lax.psum(loss_l, MESH_AXES)
    grads = {}

    def transposed_home(nm):
        return nm in COL_SHARDED and a[nm].shape[-1] % 128 != 0

    delta, new_m, new_v = {}, {}, {}

    def adamw(nm):
        name = f"adamw_{nm}"
        there = transposed_home(nm)
        view = (lambda t: jnp.swapaxes(t, 1, 2)) if there else (lambda t: t)
        if nm in COL_SHARDED:
            g = comm.total(COL_SHARDED[nm])
            grads[nm] = jnp.swapaxes(g, 1, 2)
            g = g if there else grads[nm]
        elif nm in ROW_SHARDED:
            g = grads[nm] = comm.total(ROW_SHARDED[nm])
        else:
            g = grads[nm]
        res = _adamw(name, view(a[nm]), g, view(a["m_" + nm]), view(a["v_" + nm]))
        delta[nm], new_m[nm], new_v[nm] = [view(t) for t in res]

    for nm in ("w_ff_down", "w_ff_gate", "w_ff_up", "w_o", "w_br_attn", "w_br_pool", "w_br_sgu", "w_br_conv"):
        adamw(nm)

    tots = [_unpack(_sum8(f"sum_small{i}", comm.gathered(f"small{i}")), comm.small_shapes) for i in range(n_layers)]
    dmod_c, dmod_lat_sum = jnp.stack([t[0] for t in tots]), jnp.stack([t[1] for t in tots])
    for j, nm in enumerate(LAYER_SMALL + ("conv_w",)):
        grads[nm] = jnp.stack([t[2 + j] for t in tots])
    grads["conv_w"] = lax.dynamic_slice_in_dim(grads["conv_w"], me * cw_loc, cw_loc, axis=2)
    grads["b_ada"] = dmod_c + dmod_lat_sum
    dmod_lat_all = jnp.stack([comm.gathered(f"lat{i}").reshape(N_DEV, -1)[:, :6 * d] for i in range(n_layers)])
    dm_rows = jnp.concatenate([dmod_lat_all, dmod_c[:, None, :],
                               jnp.zeros((n_layers, 16 - N_DEV - 1, 6 * d), F32)], axis=1)
    dm_cols = lax.dynamic_slice_in_dim(dm_rows, me * ada_w, ada_w, axis=2).astype(BF16)
    grads["w_ada"] = jnp.stack([carried(f"dWada{i}", sil, dm_cols[i], "tn", F32, budget_us=20)
                                for i in range(n_layers)])
    dsil = None
    for i in range(n_layers):
        dsil = carried(f"dsil{i}", dm_cols[i], a["w_ada"][i], "nt", F32, acc=dsil, budget_us=10)
    dsil = _sum8("sum_dsil", comm.exchange("gather_dsil", dsil, budget_us=17))
    grads["c_ctx"] = sil_vjp(dsil)[0][N_DEV]

    for nm in ("w_ada", "w_in", "w_gate"):
        adamw(nm)
    shapes = [a[nm].shape for nm in SMALL_ORDER]
    res = _adamw("adamw_small", _pack([a[nm] for nm in SMALL_ORDER]), _pack([grads[nm] for nm in SMALL_ORDER]),
                 _pack([a["m_" + nm] for nm in SMALL_ORDER]), _pack([a["v_" + nm] for nm in SMALL_ORDER]))
    for tree, packed in zip((delta, new_m, new_v), res):
        for nm, t in zip(SMALL_ORDER, _unpack(packed, shapes)):
            tree[nm] = t
    return (loss, grad_x[None], *[grads[nm] for nm in WEIGHT_NAMES], *[delta[nm] for nm in WEIGHT_NAMES],
            *[new_m[nm] for nm in WEIGHT_NAMES], *[new_v[nm] for nm in WEIGHT_NAMES])


def kernel(x, c, ctx, c_ctx, w_ada, b_ada, w_in, q_norm_g, k_norm_g, pool_w, pool_scale, sgu_ln_g, sgu_ln_b, sgu_w, sgu_b, conv_w, w_br_attn, w_br_pool, w_br_sgu, w_br_conv, w_gate, b_gate, w_o, ln1_g, ln1_b, w_ff_gate, w_ff_up, w_ff_down, ln2_g, ln2_b, loss_target, m_c_ctx, m_w_ada, m_b_ada, m_w_in, m_q_norm_g, m_k_norm_g, m_pool_w, m_pool_scale, m_sgu_ln_g, m_sgu_ln_b, m_sgu_w, m_sgu_b, m_conv_w, m_w_br_attn, m_w_br_pool, m_w_br_sgu, m_w_br_conv, m_w_gate, m_b_gate, m_w_o, m_ln1_g, m_ln1_b, m_w_ff_gate, m_w_ff_up, m_w_ff_down, m_ln2_g, m_ln2_b, v_c_ctx, v_w_ada, v_b_ada, v_w_in, v_q_norm_g, v_k_norm_g, v_pool_w, v_pool_scale, v_sgu_ln_g, v_sgu_ln_b, v_sgu_w, v_sgu_b, v_conv_w, v_w_br_attn, v_w_br_pool, v_w_br_sgu, v_w_br_conv, v_w_gate, v_b_gate, v_w_o, v_ln1_g, v_ln1_b, v_w_ff_gate, v_w_ff_up, v_w_ff_down, v_ln2_g, v_ln2_b):
    names = list(WEIGHT_NAMES)
    args = dict(zip(
        ["x", "c", "ctx"] + names + ["loss_target"] + ["m_" + n for n in names] + ["v_" + n for n in names],
        (x, c, ctx, c_ctx, w_ada, b_ada, w_in, q_norm_g, k_norm_g, pool_w, pool_scale, sgu_ln_g, sgu_ln_b, sgu_w, sgu_b, conv_w, w_br_attn, w_br_pool, w_br_sgu, w_br_conv, w_gate, b_gate, w_o, ln1_g, ln1_b, w_ff_gate, w_ff_up, w_ff_down, ln2_g, ln2_b, loss_target, m_c_ctx, m_w_ada, m_b_ada, m_w_in, m_q_norm_g, m_k_norm_g, m_pool_w, m_pool_scale, m_sgu_ln_g, m_sgu_ln_b, m_sgu_w, m_sgu_b, m_conv_w, m_w_br_attn, m_w_br_pool, m_w_br_sgu, m_w_br_conv, m_w_gate, m_b_gate, m_w_o, m_ln1_g, m_ln1_b, m_w_ff_gate, m_w_ff_up, m_w_ff_down, m_ln2_g, m_ln2_b, v_c_ctx, v_w_ada, v_b_ada, v_w_in, v_q_norm_g, v_k_norm_g, v_pool_w, v_pool_scale, v_sgu_ln_g, v_sgu_ln_b, v_sgu_w, v_sgu_b, v_conv_w, v_w_br_attn, v_w_br_pool, v_w_br_sgu, v_w_br_conv, v_w_gate, v_b_gate, v_w_o, v_ln1_g, v_ln1_b, v_w_ff_gate, v_w_ff_up, v_w_ff_down, v_ln2_g, v_ln2_b)))
    return _train_step(args)
```

```python
import functools
import math

import jax
import jax.numpy as jnp
from jax import lax
from jax.experimental import pallas as pl
from jax.experimental.pallas import tpu as pltpu

F32 = jnp.float32
BF16 = jnp.bfloat16

N_DEV = 8
MESH_AXES = ("x", "y", "c")
V7X_VMEM_LIMIT_BYTES = 56 * 1024 * 1024

GRID_W = 64
HEAD_DIM = 128
N_HEADS = 8
N_KV_HEADS = 2
KV_GROUP = N_HEADS // N_KV_HEADS
Q_W = N_HEADS * HEAD_DIM
KV_W = N_KV_HEADS * HEAD_DIM
ROPE_THETA = 10000.0
ROPE_AXIS_DIM = HEAD_DIM // 2
POOL_WINDOWS = (2, 4, 8, 16)
GC = 128
N_GROUPS = 4
BR_W = N_GROUPS * GC
SGU_CHUNK = 128
N_BRANCH = 4
LN_EPS = 1e-5
RMS_EPS = 1e-6
OFF_K = Q_W
OFF_V = OFF_K + KV_W
OFF_POOL = OFF_V + KV_W
OFF_U = OFF_POOL + BR_W
OFF_VG = OFF_U + BR_W
OFF_CB = OFF_VG + BR_W
OFF_CC = OFF_CB + BR_W
OFF_CX = OFF_CC + BR_W
IN_W = OFF_CX + BR_W
ATT_SCALE = HEAD_DIM ** -0.5

ADAM_LR = 0.001
ADAM_B1 = 0.9
ADAM_B2 = 0.999
ADAM_EPS = 1e-08
ADAM_WD = 0.01
ADAM_STEP = 10

_NT = (((1,), (1,)), ((), ()))
_NN = (((1,), (0,)), ((), ()))
_TN = (((0,), (0,)), ((), ()))
_DIMS = {"nt": _NT, "nn": _NN, "tn": _TN}


def _cparams(*sem):
    return pltpu.CompilerParams(dimension_semantics=sem, vmem_limit_bytes=V7X_VMEM_LIMIT_BYTES)


def _tile(dim, pref):
    best = None
    t = 128
    while t <= min(dim, pref):
        if dim % t == 0:
            best = t
        t += 128
    return best if best is not None else dim


def _dot(a, b, dims):
    return lax.dot_general(a.astype(BF16), b.astype(BF16), dims, preferred_element_type=F32)


class _Rider:
    def __init__(self, inputs, out_shapes, scratch, start, finish, aliases=None):
        self.inputs, self.out_shapes, self.scratch = list(inputs), list(out_shapes), list(scratch)
        self.start, self.finish = start, finish
        self.aliases = dict(aliases or {})


def _compose(riders):
    inputs, outs, scratch, aliases, spans = [], [], [], {}, []
    for rd in riders:
        i0, o0, s0 = len(inputs), len(outs), len(scratch)
        aliases.update({i0 + p: o0 + q for p, q in rd.aliases.items()})
        inputs += rd.inputs
        outs += rd.out_shapes
        scratch += rd.scratch
        spans.append((slice(i0, len(inputs)), slice(o0, len(outs)), slice(s0, len(scratch))))

    def start(ins, os, sems):
        for rd, (si, so, ss) in zip(riders, spans):
            rd.start(ins[si], os[so], sems[ss])

    def finish(ins, os, sems):
        for rd, (si, so, ss) in zip(riders, spans):
            rd.finish(ins[si], os[so], sems[ss])

    return _Rider(inputs, outs, scratch, start, finish, aliases)


def _pcall(name, body, grid, in_specs, out_specs, out_shape, args, sem, scratch=(), rider=None, prefetch=None,
           aliases=None):
    in_specs, out_specs, out_shape, scratch = list(in_specs), list(out_specs), list(out_shape), list(scratch)
    n_pre = 0 if prefetch is None else 1
    n_in, n_out, n_scr = len(in_specs), len(out_specs), len(scratch)
    r_in, r_out = (len(rider.inputs), len(rider.out_shapes)) if rider is not None else (0, 0)
    any_spec = pl.BlockSpec(memory_space=pl.ANY)
    io_aliases = {n_pre + p: q for p, q in (aliases or {}).items()}
    if rider is not None:
        io_aliases.update({n_pre + n_in + p: n_out + q for p, q in rider.aliases.items()})
        in_specs, out_specs = in_specs + [any_spec] * r_in, out_specs + [any_spec] * r_out
        out_shape, scratch = out_shape + rider.out_shapes, scratch + rider.scratch
        args, sem = (*args, *rider.inputs), ["arbitrary"] * len(grid)

    def wrapped(*refs):
        pre, refs = refs[:n_pre], refs[n_pre:]
        if rider is None:
            return body(*pre, *refs)
        ins, refs = refs[:n_in], refs[n_in:]
        r_ins, refs = refs[:r_in], refs[r_in:]
        outs, refs = refs[:n_out], refs[n_out:]
        r_outs, refs = refs[:r_out], refs[r_out:]
        scr, r_scr = refs[:n_scr], refs[n_scr:]
        first = functools.reduce(jnp.logical_and, [pl.program_id(ax) == 0 for ax in range(len(grid))])
        last = functools.reduce(jnp.logical_and, [pl.program_id(ax) == grid[ax] - 1 for ax in range(len(grid))])

        @pl.when(first)
        def _():
            rider.start(r_ins, r_outs, r_scr)

        body(*pre, *ins, *outs, *scr)

        @pl.when(last)
        def _():
            rider.finish(r_ins, r_outs, r_scr)

    if prefetch is None:
        res = pl.pallas_call(wrapped, name=name, grid=grid, in_specs=in_specs, out_specs=out_specs, out_shape=out_shape,
                             scratch_shapes=scratch, input_output_aliases=io_aliases,
                             compiler_params=_cparams(*sem))(*args)
    else:
        grid_spec = pltpu.PrefetchScalarGridSpec(num_scalar_prefetch=1, grid=grid, in_specs=in_specs,
                                                 out_specs=out_specs, scratch_shapes=scratch)
        res = pl.pallas_call(wrapped, name=name, grid_spec=grid_spec, out_shape=out_shape,
                             input_output_aliases=io_aliases, compiler_params=_cparams(*sem))(prefetch, *args)
    return list(res[:n_out]), list(res[n_out:])


V7X_MM_VMEM_BUDGET = 40 * 1024 * 1024


def _mm_plan(form, m, n, k, a_size, b_size, o_size, has_acc):
    tn = n if (form == "tn" and n <= 2048) else _tile(n, 512)
    rows = sorted({m} | {t for t in range(128, m, 128) if m % t == 0}, reverse=True)
    for tk, min_tm in ((k, 384), (k if k <= 2816 else _tile(k, 2816), 0)):
        nk = k // tk
        for tm in rows:
            need = 2 * (tm * tk * a_size + tn * tk * b_size + tm * tn * o_size) + tm * tn * 4 * (2 if nk > 1 else 1)
            need += 2 * tm * tn * 4 if has_acc else 0
            if need <= V7X_MM_VMEM_BUDGET and tm >= min(min_tm, m):
                return tm, tn, tk
    return _tile(m, 128), tn, tk


def _mm(name, a, b, form, out_dtype, acc=None, rider=None):
    if form == "nt":
        (m, k), (n, k2) = a.shape, b.shape
    elif form == "nn":
        (m, k), (k2, n) = a.shape, b.shape
    else:
        (k, m), (k2, n) = a.shape, b.shape
    assert k == k2, (name, a.shape, b.shape)
    has_acc = acc is not None
    tm, tn, tk = _mm_plan(form, m, n, k, a.dtype.itemsize, b.dtype.itemsize, jnp.dtype(out_dtype).itemsize, has_acc)
    nk = k // tk
    a_spec = {"nt": pl.BlockSpec((tm, tk), lambda i, j, kk: (i, kk)),
              "nn": pl.BlockSpec((tm, tk), lambda i, j, kk: (i, kk)),
              "tn": pl.BlockSpec((tk, tm), lambda i, j, kk: (kk, i))}[form]
    b_spec = {"nt": pl.BlockSpec((tn, tk), lambda i, j, kk: (j, kk)),
              "nn": pl.BlockSpec((tk, tn), lambda i, j, kk: (kk, j)),
              "tn": pl.BlockSpec((tk, tn), lambda i, j, kk: (kk, j))}[form]
    o_spec = pl.BlockSpec((tm, tn), lambda i, j, kk: (i, j))
    dims = _DIMS[form]

    def body(*refs):
        a_ref, b_ref = refs[0], refs[1]
        c_ref = refs[2] if has_acc else None
        o_ref = refs[3] if has_acc else refs[2]

        def finish(r):
            if has_acc:
                r = r + c_ref[...]
            o_ref[...] = r.astype(o_ref.dtype)

        if nk == 1:
            finish(_dot(a_ref[...], b_ref[...], dims))
            return
        acc_ref = refs[-1]
        kk = pl.program_id(2)

        @pl.when(kk == 0)
        def _():
            acc_ref[...] = _dot(a_ref[...], b_ref[...], dims)

        @pl.when(kk > 0)
        def _():
            acc_ref[...] += _dot(a_ref[...], b_ref[...], dims)

        @pl.when(kk == nk - 1)
        def _():
            finish(acc_ref[...])

    in_specs = [a_spec, b_spec] + ([o_spec] if has_acc else [])
    args = (a, b) + ((acc,) if has_acc else ())
    outs, r_outs = _pcall(name, body, (m // tm, n // tn, nk), in_specs, [o_spec],
                          [jax.ShapeDtypeStruct((m, n), out_dtype)], args, ("parallel", "parallel", "arbitrary"),
                          scratch=[pltpu.VMEM((tm, tn), F32)] if nk > 1 else [], rider=rider)
    return outs[0] if rider is None else (outs[0], r_outs)


def _mm_fused(name, a, bs, epilogue, tile_ins, out_dtypes, rider=None):
    (m, k), (n, _) = a.shape, bs[0].shape
    tn = _tile(n, 512)
    tm = None
    for cand in sorted({m} | {t for t in range(128, m, 128) if m % t == 0}, reverse=True):
        per_tile = sum(t.dtype.itemsize for t in tile_ins) + sum(jnp.dtype(d).itemsize for d in out_dtypes)
        need = 2 * (cand * k * a.dtype.itemsize + len(bs) * tn * k * bs[0].dtype.itemsize + cand * tn * per_tile)
        need += (len(bs) + 2) * cand * tn * 4
        if need <= V7X_MM_VMEM_BUDGET:
            tm = cand
            break
    assert tm is not None, name
    n_b, n_t = len(bs), len(tile_ins)
    tile = pl.BlockSpec((tm, tn), lambda i, j: (i, j))

    def body(a_ref, *refs):
        prods = [_dot(a_ref[...], r[...], _NT) for r in refs[:n_b]]
        outs = epilogue(prods, [r[...].astype(F32) for r in refs[n_b:n_b + n_t]])
        for r, o in zip(refs[n_b + n_t:], outs):
            r[...] = o.astype(r.dtype)

    outs, r_outs = _pcall(name, body, (m // tm, n // tn),
                          [pl.BlockSpec((tm, k), lambda i, j: (i, 0))] + [pl.BlockSpec((tn, k), lambda i, j: (j, 0))] * n_b
                          + [tile] * n_t, [tile] * len(out_dtypes),
                          [jax.ShapeDtypeStruct((m, n), d) for d in out_dtypes], (a, *bs, *tile_ins),
                          ("parallel", "parallel"), rider=rider)
    return outs if rider is None else (outs, r_outs)


def _rows(name, fn, n_rows, tm, nbc, row_ins, type_ins, row_outs, acc_outs, rider=None):
    n_ri, n_ti, n_ro, n_ao = len(row_ins), len(type_ins), len(row_outs), len(acc_outs)

    def row_map(i, cb, roff):
        return (jnp.maximum(i - roff, 0), cb)

    def type_map(i):
        return (jnp.where(i >= nbc, 1, 0), 0, 0)

    in_specs, args = [], []
    for arr, cb, width, roff in row_ins:
        in_specs.append(pl.BlockSpec((tm, width), functools.partial(row_map, cb=cb, roff=roff)))
        args.append(arr)
    def shared_map(i):
        return (0, 0, 0)

    for arr in type_ins:
        in_specs.append(pl.BlockSpec((None, 1, arr.shape[-1]), type_map if arr.shape[0] == 2 else shared_map))
        args.append(arr)
    out_shape, out_specs = [], []
    for total, width, dtype, roff in row_outs:
        out_shape.append(jax.ShapeDtypeStruct((total, width), dtype))
        out_specs.append(pl.BlockSpec((tm, width), functools.partial(row_map, cb=0, roff=roff)))
    acc_shared = [isinstance(w, tuple) for w in acc_outs]
    for width in acc_outs:
        if isinstance(width, tuple):
            out_shape.append(jax.ShapeDtypeStruct((1, 1, width[0]), F32))
            out_specs.append(pl.BlockSpec((None, 1, width[0]), shared_map))
        else:
            out_shape.append(jax.ShapeDtypeStruct((2, 1, width), F32))
            out_specs.append(pl.BlockSpec((None, 1, width), type_map))
    n_in = n_ri + n_ti

    def body(*refs):
        i = pl.program_id(0)
        outs = fn(*[r[...].astype(F32) for r in refs[:n_in]])
        if not isinstance(outs, (tuple, list)):
            outs = (outs,)
        assert len(outs) == n_ro + n_ao, (name, len(outs))
        for r, o in zip(refs[n_in:n_in + n_ro], outs[:n_ro]):
            r[...] = o.astype(r.dtype)
        if n_ao:
            for r, o, shared in zip(refs[n_in + n_ro:], outs[n_ro:], acc_shared):
                first = i == 0 if shared else jnp.logical_or(i == 0, i == nbc)
                o = jnp.broadcast_to(o.astype(F32), r.shape)

                @pl.when(first)
                def _(r=r, o=o):
                    r[...] = o

                @pl.when(jnp.logical_not(first))
                def _(r=r, o=o):
                    r[...] += o

    outs, r_outs = _pcall(name, body, (n_rows // tm,), in_specs, out_specs, out_shape, args, ("arbitrary",),
                          rider=rider)
    return outs if rider is None else (outs, r_outs)


def _vjp_fn(f, n_row, n_cot, keep=None):
    def g(*args):
        prim = args[:n_row] + args[n_row + n_cot:]
        cots = args[n_row:n_row + n_cot]
        out, vjp = jax.vjp(f, *prim)
        grads = vjp(tuple(cots) if isinstance(out, (tuple, list)) else cots[0])
        return grads if keep is None else tuple(grads[j] for j in keep)
    return g


def _typed(v):
    return v.reshape(1, 1, -1)


def _ln(x, g, b):
    mu = jnp.mean(x, axis=-1, keepdims=True)
    var = jnp.mean(jnp.square(x - mu), axis=-1, keepdims=True)
    return (x - mu) * lax.rsqrt(var + LN_EPS) * g + b


def _f_mod(x, sc, sh):
    return x * (1.0 + sc) + sh


def _make_f_ln(alpha, with_mod):
    def f(x, o, gate, lng, lnb, *mod):
        xn = _ln(alpha * x + gate * o, lng, lnb)
        if with_mod:
            sc, sh = mod
            return xn, xn * (1.0 + sc) + sh
        return xn
    return f


@jax.custom_vjp
def _rot(y):
    lane = lax.broadcasted_iota(jnp.int32, y.shape, 1)
    return jnp.where(lane % 64 < 32, pltpu.roll(y, 96, axis=1), pltpu.roll(y, 32, axis=1))


_rot.defvjp(lambda y: (_rot(y), None), lambda _, g: (_rot(g),))


def _f_prep(p, cos, sin, qg, kg):
    def head(xh, g):
        ms = jnp.mean(jnp.square(xh), axis=-1, keepdims=True)
        y = xh * lax.rsqrt(ms + RMS_EPS) * g
        return y * cos + _rot(y) * sin
    q = jnp.concatenate([head(p[:, h * HEAD_DIM:(h + 1) * HEAD_DIM], qg) for h in range(N_HEADS)], axis=1)
    k = jnp.concatenate([head(p[:, OFF_K + h * HEAD_DIM:OFF_K + (h + 1) * HEAD_DIM], kg)
                         for h in range(N_KV_HEADS)], axis=1)
    return q, k, p[:, OFF_V:OFF_POOL]


def _f_gate(g, t0, t1, t2, t3, b):
    d = t0.shape[-1]
    ts = (t0, t1, t2, t3)
    terms = [jax.nn.sigmoid(g[:, k * d:(k + 1) * d] + b[:, k * d:(k + 1) * d]) * ts[k] for k in range(N_BRANCH)]
    return terms[0] + terms[1] + terms[2] + terms[3]


def _f_swiglu(a, b):
    return jax.nn.silu(a) * b


def _softmax(raw):
    e = jnp.exp2((raw - jnp.max(raw, axis=-1, keepdims=True)) * (ATT_SCALE * math.log2(math.e)))
    return e / jnp.sum(e, axis=-1, keepdims=True)


def _attn_fwd(name, q, k, v, rc, ctx_queries, tq=256, rider=None):
    r = q.shape[0]
    assert rc % tq == 0 and r % tq == 0
    nqc = rc // tq

    def body(q_ref, k_ref, v_ref, o_ref):
        qi = pl.program_id(1)

        def attend(nk):
            raw = _dot(q_ref[...], k_ref[0:nk, :], _NT)
            e = jnp.exp2((raw - jnp.max(raw, axis=-1, keepdims=True)) * (ATT_SCALE * math.log2(math.e)))
            o = _dot(e, v_ref[0:nk, :], _NN) / jnp.sum(e, axis=-1, keepdims=True)
            o_ref[...] = o.astype(o_ref.dtype)

        @pl.when(qi < nqc)
        def _():
            if ctx_queries:
                attend(rc)
            else:
                o_ref[...] = jnp.zeros_like(o_ref)

        @pl.when(qi >= nqc)
        def _():
            attend(r)

    outs, r_outs = _pcall(
        name, body, (N_HEADS, r // tq),
        [pl.BlockSpec((tq, HEAD_DIM), lambda h, i: (i, h)),
         pl.BlockSpec((r, HEAD_DIM), lambda h, i: (0, h // KV_GROUP)),
         pl.BlockSpec((r, HEAD_DIM), lambda h, i: (0, h // KV_GROUP))],
        [pl.BlockSpec((tq, HEAD_DIM), lambda h, i: (i, h))],
        [jax.ShapeDtypeStruct((r, Q_W), BF16)], (q, k, v), ("parallel", "parallel"), rider=rider)
    return outs[0] if rider is None else (outs[0], r_outs)


def _attn_bwd(name, q, k, v, do, rc, ctx_queries, tq=256, rider=None):
    r = q.shape[0]
    nqc = rc // tq

    def body(q_ref, k_ref, v_ref, do_ref, dq_ref, dk_ref, dv_ref):
        g, qi = pl.program_id(1), pl.program_id(2)

        @pl.when(jnp.logical_and(g == 0, qi == 0))
        def _():
            dk_ref[...] = jnp.zeros_like(dk_ref)
            dv_ref[...] = jnp.zeros_like(dv_ref)

        def grad(nk):
            qb, kb, vb = q_ref[...], k_ref[0:nk, :], v_ref[0:nk, :]
            dob = do_ref[...].astype(BF16)
            p = _softmax(_dot(qb, kb, _NT))
            dv_ref[0:nk, :] += _dot(p, dob, _TN)
            dp = _dot(dob, vb, _NT)
            ds = p * (dp - jnp.sum(dp * p, axis=-1, keepdims=True)) * ATT_SCALE
            dq_ref[...] = _dot(ds, kb, _NN)
            dk_ref[0:nk, :] += _dot(ds, qb, _TN)

        @pl.when(qi < nqc)
        def _():
            if ctx_queries:
                grad(rc)
            else:
                dq_ref[...] = jnp.zeros_like(dq_ref)

        @pl.when(qi >= nqc)
        def _():
            grad(r)

    def qmap(kv, g, i):
        return (i, kv * KV_GROUP + g)

    def kvmap(kv, g, i):
        return (0, kv)

    outs, r_outs = _pcall(
        name, body, (N_KV_HEADS, KV_GROUP, r // tq),
        [pl.BlockSpec((tq, HEAD_DIM), qmap), pl.BlockSpec((r, HEAD_DIM), kvmap),
         pl.BlockSpec((r, HEAD_DIM), kvmap), pl.BlockSpec((tq, HEAD_DIM), qmap)],
        [pl.BlockSpec((tq, HEAD_DIM), qmap), pl.BlockSpec((r, HEAD_DIM), kvmap), pl.BlockSpec((r, HEAD_DIM), kvmap)],
        [jax.ShapeDtypeStruct((r, Q_W), F32), jax.ShapeDtypeStruct((r, KV_W), F32),
         jax.ShapeDtypeStruct((r, KV_W), F32)],
        (q, k, v, do), ("arbitrary", "arbitrary", "arbitrary"), rider=rider)
    return outs if rider is None else (outs, r_outs)


def _segments(shape, rc):
    t = lax.broadcasted_iota(jnp.int32, shape, 0)
    lo = jnp.where(t < rc, 0, rc)
    hi = jnp.where(t < rc, rc, shape[0])
    return t, lo, hi


def _shifted(x, o, t, lo, hi):
    n = x.shape[0]
    sh = pltpu.roll(x, (-o) % n, axis=0)
    return jnp.where(jnp.logical_and(t + o >= lo, t + o < hi), sh, 0.0)


def _winsum(x, left, right, t, lo, hi):
    acc = x
    for o in range(-left, right + 1):
        if o != 0:
            acc = acc + _shifted(x, o, t, lo, hi)
    return acc


def _pool_parts(z, g, t, lo, hi):
    w = POOL_WINDOWS[g]
    left = w // 2
    right = w - 1 - left
    count = (jnp.minimum(t + right + 1, hi) - jnp.maximum(t - left, lo)).astype(F32)
    return _winsum(z, left, right, t, lo, hi) / count - z, count, left, right


def _pool_fwd(name, p, pool_w, pool_scale, rc):
    r = p.shape[0]

    def body(z_ref, w_ref, s_ref, y_ref):
        t, lo, hi = _segments((r, GC), rc)
        for g in range(N_GROUPS):
            cols = slice(g * GC, (g + 1) * GC)
            d, _, _, _ = _pool_parts(z_ref[:, cols], g, t, lo, hi)
            y_ref[:, cols] = (_dot(d, w_ref[g], _NN) * s_ref[:, cols]).astype(y_ref.dtype)

    return pl.pallas_call(
        body, name=name, grid=(1,),
        in_specs=[pl.BlockSpec((r, BR_W), lambda i: (0, OFF_POOL // BR_W)),
                  pl.BlockSpec((N_GROUPS, GC, GC), lambda i: (0, 0, 0)),
                  pl.BlockSpec((1, BR_W), lambda i: (0, 0))],
        out_specs=pl.BlockSpec((r, BR_W), lambda i: (0, 0)),
        out_shape=jax.ShapeDtypeStruct((r, BR_W), BF16),
        compiler_params=_cparams("arbitrary"),
    )(p, pool_w, pool_scale.reshape(1, BR_W))


def _pool_bwd(name, p, pool_w, pool_scale, dy, rc):
    r = p.shape[0]

    def body(z_ref, w_ref, s_ref, dy_ref, dz_ref, dw_ref, ds_ref):
        t, lo, hi = _segments((r, GC), rc)
        for g in range(N_GROUPS):
            cols = slice(g * GC, (g + 1) * GC)
            d, count, left, right = _pool_parts(z_ref[:, cols], g, t, lo, hi)
            dyg = dy_ref[:, cols]
            ds_ref[:, cols] = jnp.sum(dyg * _dot(d, w_ref[g], _NN), axis=0, keepdims=True)
            dlin = dyg * s_ref[:, cols]
            dw_ref[g] = _dot(d, dlin, _TN)
            dd = _dot(dlin, w_ref[g], _NT)
            dz_ref[:, cols] = (_winsum(dd / count, right, left, t, lo, hi) - dd).astype(dz_ref.dtype)

    return pl.pallas_call(
        body, name=name, grid=(1,),
        in_specs=[pl.BlockSpec((r, BR_W), lambda i: (0, OFF_POOL // BR_W)),
                  pl.BlockSpec((N_GROUPS, GC, GC), lambda i: (0, 0, 0)),
                  pl.BlockSpec((1, BR_W), lambda i: (0, 0)),
                  pl.BlockSpec((r, BR_W), lambda i: (0, 0))],
        out_specs=[pl.BlockSpec((r, BR_W), lambda i: (0, 0)),
                   pl.BlockSpec((N_GROUPS, GC, GC), lambda i: (0, 0, 0)),
                   pl.BlockSpec((1, BR_W), lambda i: (0, 0))],
        out_shape=[jax.ShapeDtypeStruct((r, BR_W), BF16), jax.ShapeDtypeStruct((N_GROUPS, GC, GC), F32),
                   jax.ShapeDtypeStruct((1, BR_W), F32)],
        compiler_params=_cparams("arbitrary"),
    )(p, pool_w, pool_scale.reshape(1, BR_W), dy)


def _f_sgu_v(pvg, lng, lnb):
    return _ln(jax.nn.gelu(pvg), lng, lnb)


def _sgu_fwd(name, p, ln_g, ln_b, sgu_w, sgu_b):
    r = p.shape[0]

    def body(pu_ref, pv_ref, g_ref, b_ref, w_ref, sb_ref, y_ref):
        vn = _f_sgu_v(pv_ref[...], g_ref[...], b_ref[...])
        u = jax.nn.gelu(pu_ref[...])
        for g in range(N_GROUPS):
            cols = slice(g * GC, (g + 1) * GC)
            s = _dot(w_ref[g], vn[:, cols], _NN) + sb_ref[g]
            y_ref[:, cols] = (u[:, cols] * s).astype(y_ref.dtype)

    return pl.pallas_call(
        body, name=name, grid=(r // SGU_CHUNK,),
        in_specs=[pl.BlockSpec((SGU_CHUNK, BR_W), lambda i: (i, OFF_U // BR_W)),
                  pl.BlockSpec((SGU_CHUNK, BR_W), lambda i: (i, OFF_VG // BR_W)),
                  pl.BlockSpec((1, BR_W), lambda i: (0, 0)), pl.BlockSpec((1, BR_W), lambda i: (0, 0)),
                  pl.BlockSpec((N_GROUPS, GC, GC), lambda i: (0, 0, 0)),
                  pl.BlockSpec((N_GROUPS, SGU_CHUNK, 1), lambda i: (0, 0, 0))],
        out_specs=pl.BlockSpec((SGU_CHUNK, BR_W), lambda i: (i, 0)),
        out_shape=jax.ShapeDtypeStruct((r, BR_W), BF16),
        compiler_params=_cparams("parallel"),
    )(p, p, ln_g.reshape(1, BR_W), ln_b.reshape(1, BR_W), sgu_w, sgu_b.reshape(N_GROUPS, SGU_CHUNK, 1))


def _sgu_bwd(name, p, ln_g, ln_b, sgu_w, sgu_b, dy):
    r = p.shape[0]

    def body(pu_ref, pv_ref, g_ref, b_ref, w_ref, sb_ref, dy_ref, dp_ref, dg_ref, db_ref, dw_ref, dsb_ref):
        i = pl.program_id(0)

        @pl.when(i == 0)
        def _():
            for ref in (dg_ref, db_ref, dw_ref, dsb_ref):
                ref[...] = jnp.zeros_like(ref)

        vn, vjp_v = jax.vjp(_f_sgu_v, pv_ref[...], g_ref[...], b_ref[...])
        u, vjp_u = jax.vjp(jax.nn.gelu, pu_ref[...])
        dy = dy_ref[...]
        du, dvn = [], []
        for g in range(N_GROUPS):
            cols = slice(g * GC, (g + 1) * GC)
            s = _dot(w_ref[g], vn[:, cols], _NN) + sb_ref[g]
            du.append(dy[:, cols] * s)
            ds = dy[:, cols] * u[:, cols]
            dsb_ref[g] += jnp.sum(ds, axis=1, keepdims=True)
            dw_ref[g] += _dot(ds, vn[:, cols], _NT)
            dvn.append(_dot(w_ref[g], ds, _TN))
        (dpu,) = vjp_u(jnp.concatenate(du, axis=1))
        dpv, dg, db = vjp_v(jnp.concatenate(dvn, axis=1))
        dp_ref[:, 0:BR_W] = dpu.astype(dp_ref.dtype)
        dp_ref[:, BR_W:2 * BR_W] = dpv.astype(dp_ref.dtype)
        dg_ref[...] += dg
        db_ref[...] += db

    vec = pl.BlockSpec((1, BR_W), lambda i: (0, 0))
    wsp = pl.BlockSpec((N_GROUPS, GC, GC), lambda i: (0, 0, 0))
    bsp = pl.BlockSpec((N_GROUPS, SGU_CHUNK, 1), lambda i: (0, 0, 0))
    return pl.pallas_call(
        body, name=name, grid=(r // SGU_CHUNK,),
        in_specs=[pl.BlockSpec((SGU_CHUNK, BR_W), lambda i: (i, OFF_U // BR_W)),
                  pl.BlockSpec((SGU_CHUNK, BR_W), lambda i: (i, OFF_VG // BR_W)),
                  vec, vec, wsp, bsp, pl.BlockSpec((SGU_CHUNK, BR_W), lambda i: (i, 0))],
        out_specs=[pl.BlockSpec((SGU_CHUNK, 2 * BR_W), lambda i: (i, 0)), vec, vec, wsp, bsp],
        out_shape=[jax.ShapeDtypeStruct((r, 2 * BR_W), BF16), jax.ShapeDtypeStruct((1, BR_W), F32),
                   jax.ShapeDtypeStruct((1, BR_W), F32), jax.ShapeDtypeStruct((N_GROUPS, GC, GC), F32),
                   jax.ShapeDtypeStruct((N_GROUPS, SGU_CHUNK, 1), F32)],
        compiler_params=_cparams("arbitrary"),
    )(p, p, ln_g.reshape(1, BR_W), ln_b.reshape(1, BR_W), sgu_w, sgu_b.reshape(N_GROUPS, SGU_CHUNK, 1), dy)


def _conv_w8(conv_w):
    return jnp.concatenate([conv_w, jnp.zeros((8 - conv_w.shape[0], conv_w.shape[1]), F32)], axis=0)


def _conv_fwd(name, p, conv_w, rc):
    r = p.shape[0]

    def body(cb_ref, cc_ref, cx_ref, w_ref, y_ref):
        t, lo, hi = _segments((r, GC), rc)
        z = cc_ref[...] * cx_ref[...]
        w = w_ref[...]
        c = _shifted(z, -1, t, lo, hi) * w[0:1] + z * w[1:2] + _shifted(z, 1, t, lo, hi) * w[2:3]
        y_ref[...] = (cb_ref[...] * c).astype(y_ref.dtype)

    nb = OFF_CB // GC
    return pl.pallas_call(
        body, name=name, grid=(N_GROUPS,),
        in_specs=[pl.BlockSpec((r, GC), lambda j: (0, nb + j)),
                  pl.BlockSpec((r, GC), lambda j: (0, nb + N_GROUPS + j)),
                  pl.BlockSpec((r, GC), lambda j: (0, nb + 2 * N_GROUPS + j)),
                  pl.BlockSpec((8, GC), lambda j: (0, j))],
        out_specs=pl.BlockSpec((r, GC), lambda j: (0, j)),
        out_shape=jax.ShapeDtypeStruct((r, BR_W), BF16),
        compiler_params=_cparams("parallel"),
    )(p, p, p, _conv_w8(conv_w))


def _conv_bwd(name, p, conv_w, dy, rc):
    r = p.shape[0]

    def body(cb_ref, cc_ref, cx_ref, w_ref, dy_ref, dcb_ref, dcc_ref, dcx_ref, dw_ref):
        t, lo, hi = _segments((r, GC), rc)
        cc, cx, w, dy = cc_ref[...], cx_ref[...], w_ref[...], dy_ref[...]
        z = cc * cx
        zp, zn = _shifted(z, -1, t, lo, hi), _shifted(z, 1, t, lo, hi)
        dcb_ref[...] = (dy * (zp * w[0:1] + z * w[1:2] + zn * w[2:3])).astype(dcb_ref.dtype)
        dc = dy * cb_ref[...]
        dw_ref[...] = jnp.concatenate(
            [jnp.sum(dc * zp, axis=0, keepdims=True), jnp.sum(dc * z, axis=0, keepdims=True),
             jnp.sum(dc * zn, axis=0, keepdims=True), jnp.zeros((5, GC), F32)], axis=0)
        dz = dc * w[1:2] + _shifted(dc, 1, t, lo, hi) * w[0:1] + _shifted(dc, -1, t, lo, hi) * w[2:3]
        dcc_ref[...] = (dz * cx).astype(dcc_ref.dtype)
        dcx_ref[...] = (dz * cc).astype(dcx_ref.dtype)

    nb = OFF_CB // GC
    return pl.pallas_call(
        body, name=name, grid=(N_GROUPS,),
        in_specs=[pl.BlockSpec((r, GC), lambda j: (0, nb + j)),
                  pl.BlockSpec((r, GC), lambda j: (0, nb + N_GROUPS + j)),
                  pl.BlockSpec((r, GC), lambda j: (0, nb + 2 * N_GROUPS + j)),
                  pl.BlockSpec((8, GC), lambda j: (0, j)),
                  pl.BlockSpec((r, GC), lambda j: (0, j))],
        out_specs=[pl.BlockSpec((r, GC), lambda j: (0, j))] * 3 + [pl.BlockSpec((8, GC), lambda j: (0, j))],
        out_shape=[jax.ShapeDtypeStruct((r, BR_W), BF16)] * 3 + [jax.ShapeDtypeStruct((8, BR_W), F32)],
        compiler_params=_cparams("parallel"),
    )(p, p, p, _conv_w8(conv_w), dy)


def _rope_tables(rc, n):
    rows = n // GRID_W
    row = jnp.repeat(jnp.arange(rows), GRID_W).astype(F32)
    col = jnp.tile(jnp.arange(GRID_W), rows).astype(F32)
    inv = ROPE_THETA ** (-jnp.arange(0, ROPE_AXIS_DIM, 2, dtype=F32) / ROPE_AXIS_DIM)
    ang_r, ang_c = row[:, None] * inv, col[:, None] * inv
    cos = jnp.concatenate([jnp.cos(ang_r), jnp.cos(ang_r), jnp.cos(ang_c), jnp.cos(ang_c)], axis=1)
    sin = jnp.concatenate([-jnp.sin(ang_r), jnp.sin(ang_r), -jnp.sin(ang_c), jnp.sin(ang_c)], axis=1)
    cos = jnp.concatenate([jnp.ones((rc, HEAD_DIM), F32), cos], axis=0)
    sin = jnp.concatenate([jnp.zeros((rc, HEAD_DIM), F32), sin], axis=0)
    return cos, sin


MOD_NAMES = ("sh1", "sc1", "g1", "sh2", "sc2", "g2")


def _local_step(xin, target, mod, comm, sp, rc, alpha):
    def carrying(fn):
        def call(name, *args, **kw):
            rider = comm.rider(name)
            if rider is None:
                return fn(name, *args, **kw)
            res, r_outs = fn(name, *args, rider=rider, **kw)
            comm.deliver(name, r_outs)
            return res
        return call

    mm, rows, attn_fwd, attn_bwd = carrying(_mm), carrying(_rows), carrying(_attn_fwd), carrying(_attn_bwd)
    mm_fused = carrying(_mm_fused)
    r, d = xin.shape
    n_layers = mod.shape[0]
    tm_n, tm_w = 256, 128
    nbc_n, nbc_w = rc // tm_n, rc // tm_w
    cos, sin = _rope_tables(rc, r - rc)
    mp = mod.reshape(n_layers, 2, 6, 1, d)
    mods = [{nm: mp[i, :, j] for j, nm in enumerate(MOD_NAMES)} for i in range(n_layers)]
    f_ln_mod, f_ln_last = _make_f_ln(alpha, True), _make_f_ln(alpha, False)

    def whole(arr, roff=0):
        return (arr, 0, arr.shape[1], roff)

    (hb,) = rows("mod_in", _f_mod, r, tm_n, nbc_n, [whole(xin)], [mods[0]["sc1"], mods[0]["sh1"]],
                 [(r, d, BF16, 0)], [])
    saved = []
    x = xin
    for i in range(n_layers):
        last = i == n_layers - 1
        w, s, m = functools.partial(comm.weight, i), sp[i], mods[i]
        sv = {"x": x, "hb": hb}
        p = mm(f"l{i}_in", hb, w("in_t"), "nt", F32)
        q, k, v = rows(f"l{i}_prep", _f_prep, r, tm_n, nbc_n,
                       [(p, 0, OFF_POOL, 0), whole(cos), whole(sin)], [_typed(s["q_norm_g"]), _typed(s["k_norm_g"])],
                       [(r, Q_W, BF16, 0), (r, KV_W, BF16, 0), (r, KV_W, BF16, 0)], [])
        ys = [attn_fwd(f"l{i}_attn", q, k, v, rc, not last),
              _pool_fwd(f"l{i}_pool", p, s["pool_w"], s["pool_scale"], rc),
              _sgu_fwd(f"l{i}_sgu", p, s["sgu_ln_g"], s["sgu_ln_b"], s["sgu_w"], s["sgu_b"]),
              _conv_fwd(f"l{i}_conv", p, s["conv_w"], rc)]
        ts = [mm(f"l{i}_br{kk}", ys[kk], w(f"br{kk}"), "nt", BF16) for kk in range(N_BRANCH)]
        gpre = mm(f"l{i}_gate", hb, w("gate_t"), "nt", BF16)
        (mg,) = rows(f"l{i}_merge", _f_gate, r, tm_w, nbc_w, [whole(gpre)] + [whole(t) for t in ts],
                     [_typed(s["b_gate"])], [(r, d, BF16, 0)], [])
        o = mm(f"l{i}_o", mg, w("o"), "nn", F32)
        x1, h2b = rows(f"l{i}_ln1", f_ln_mod, r, tm_n, nbc_n, [whole(x), whole(o)],
                       [m["g1"], _typed(s["ln1_g"]), _typed(s["ln1_b"]), m["sc2"], m["sh2"]],
                       [(r, d, F32, 0), (r, d, BF16, 0)], [])
        af, bf, f = mm_fused(f"l{i}_ffgu", h2b, [w("ffg_t"), w("ffu_t")],
                             lambda prods, _: (prods[0], prods[1], _f_swiglu(prods[0], prods[1])), [], [BF16, BF16, BF16])
        o2 = mm(f"l{i}_ffd", f, w("ffd"), "nn", F32)
        if last:
            (x2,) = rows(f"l{i}_ln2", f_ln_last, r, tm_n, nbc_n, [whole(x1), whole(o2)],
                         [m["g2"], _typed(s["ln2_g"]), _typed(s["ln2_b"])], [(r, d, F32, 0)], [])
            hb = None
        else:
            nx = mods[i + 1]
            x2, hb = rows(f"l{i}_ln2", f_ln_mod, r, tm_n, nbc_n, [whole(x1), whole(o2)],
                          [m["g2"], _typed(s["ln2_g"]), _typed(s["ln2_b"]), nx["sc1"], nx["sh1"]],
                          [(r, d, F32, 0), (r, d, BF16, 0)], [])
        sv.update(p=p, gpre=gpre, q=q, k=k, v=v, ys=ys, ts=ts, mg=mg, o=o, x1=x1, h2b=h2b, af=af, bf=bf, f=f, o2=o2)
        saved.append(sv)
        x = x2

    lat = jnp.concatenate([jnp.zeros((1, 1, 128), F32), jnp.ones((1, 1, 128), F32)], axis=0)

    def f_loss(xb, tb, msk):
        diff = (xb - tb) * msk[:, 0:1]
        part = jnp.sum(jnp.mean(jnp.square(diff), axis=-1, keepdims=True), axis=0, keepdims=True)
        return diff * (1.0 / d), jnp.broadcast_to(part, (1, 128))

    dx_direct, loss_acc = rows("loss", f_loss, r, tm_n, nbc_n, [whole(x), whole(target, nbc_n)], [lat],
                               [(r, d, F32, 0)], [128])
    loss = 0.5 * loss_acc[1, 0, 0]

    dmods = [dict() for _ in range(n_layers)]
    dsp = [dict() for _ in range(n_layers)]
    dh = None

    def small_done(j):
        ds = dict(dsp[j])
        for nm in ("ln1_g", "ln1_b", "ln2_g", "ln2_b", "b_gate", "q_norm_g", "k_norm_g"):
            ds[nm] = ds[nm][0, 0]
        for nm in ("pool_scale", "sgu_ln_g", "sgu_ln_b"):
            ds[nm] = ds[nm].reshape(-1)
        ds["sgu_b"] = ds["sgu_b"].reshape(N_GROUPS, SGU_CHUNK)
        comm.small_ready(j, jnp.concatenate([dmods[j][nm][:, 0, :] for nm in MOD_NAMES], axis=-1), ds)

    for i in reversed(range(n_layers)):
        last = i == n_layers - 1
        w, s, m, sv = functools.partial(comm.weight, i), sp[i], mods[i], saved[i]
        dm, dw, ds = dmods[i], {}, dsp[i]
        ln2 = [m["g2"], _typed(s["ln2_g"]), _typed(s["ln2_b"])]
        if last:
            res = rows(f"l{i}_ln2_bwd", _vjp_fn(f_ln_last, 2, 1), r, tm_n, nbc_n,
                       [whole(sv["x1"]), whole(sv["o2"]), whole(dx_direct)], ln2,
                       [(r, d, F32, 0), (r, d, BF16, 0)], [d, (d,), (d,)])
            dx1, do2, dm["g2"], dlg, dlb = res
        else:
            nx = mods[i + 1]
            res = rows(f"l{i}_ln2_bwd", _vjp_fn(f_ln_mod, 2, 2), r, tm_n, nbc_n,
                       [whole(sv["x1"]), whole(sv["o2"]), whole(dx_direct), whole(dh)],
                       ln2 + [nx["sc1"], nx["sh1"]],
                       [(r, d, F32, 0), (r, d, BF16, 0)], [d, (d,), (d,), d, d])
            dx1, do2, dm["g2"], dlg, dlb, dmods[i + 1]["sc1"], dmods[i + 1]["sh1"] = res
            small_done(i + 1)
        ds["ln2_g"], ds["ln2_b"] = dlg, dlb
        dab, dbb = mm_fused(f"l{i}_dF", do2, [w("ffd")],
                            lambda prods, tiles: jax.vjp(_f_swiglu, *tiles)[1](prods[0]), [sv["af"], sv["bf"]],
                            [BF16, BF16])
        comm.grads(i, {"ffd": mm(f"l{i}_dWffd", sv["f"], do2, "tn", BF16)})
        comm.grads(i, {"ffg_t": mm(f"l{i}_dWffg", dab, sv["h2b"], "tn", BF16)})
        comm.grads(i, {"ffu_t": mm(f"l{i}_dWffu", dbb, sv["h2b"], "tn", BF16)})
        dh2 = mm(f"l{i}_dh2a", dab, w("ffg_t"), "nn", F32)
        dh2 = mm(f"l{i}_dh2b", dbb, w("ffu_t"), "nn", F32, acc=dh2)
        res = rows(f"l{i}_ln1_bwd", _vjp_fn(f_ln_mod, 2, 2), r, tm_n, nbc_n,
                   [whole(sv["x"]), whole(sv["o"]), whole(dx1), whole(dh2)],
                   [m["g1"], _typed(s["ln1_g"]), _typed(s["ln1_b"]), m["sc2"], m["sh2"]],
                   [(r, d, F32, 0), (r, d, BF16, 0)], [d, (d,), (d,), d, d])
        dx_direct, do, dm["g1"], ds["ln1_g"], ds["ln1_b"], dm["sc2"], dm["sh2"] = res
        dmg = mm(f"l{i}_dMg", do, w("o"), "nt", F32)
        comm.grads(i, {"o": mm(f"l{i}_dWo", sv["mg"], do, "tn", BF16)})
        res = rows(f"l{i}_merge_bwd", _vjp_fn(_f_gate, 5, 1), r, tm_w, nbc_w,
                   [whole(sv["gpre"])] + [whole(t) for t in sv["ts"]] + [whole(dmg)], [_typed(s["b_gate"])],
                   [(r, N_BRANCH * d, BF16, 0)] + [(r, d, BF16, 0)] * N_BRANCH, [(N_BRANCH * d,)])
        dgb, dts, ds["b_gate"] = res[0], res[1:1 + N_BRANCH], res[1 + N_BRANCH]
        comm.grads(i, {"gate_t": mm(f"l{i}_dWgate", dgb, sv["hb"], "tn", BF16)})
        dys = [mm(f"l{i}_dY{kk}", dts[kk], w(f"br{kk}"), "nn", F32) for kk in range(N_BRANCH)]
        for kk in range(N_BRANCH):
            comm.grads(i, {f"br{kk}": mm(f"l{i}_dWbr{kk}", dts[kk], sv["ys"][kk], "tn", BF16)})
        dq, dk, dv = attn_bwd(f"l{i}_attn_bwd", sv["q"], sv["k"], sv["v"], dys[0], rc, not last)
        res = rows(f"l{i}_prep_bwd", _vjp_fn(_f_prep, 3, 3, keep=(0, 3, 4)), r, tm_n, nbc_n,
                   [(sv["p"], 0, OFF_POOL, 0), whole(cos), whole(sin), whole(dq), whole(dk), whole(dv)],
                   [_typed(s["q_norm_g"]), _typed(s["k_norm_g"])],
                   [(r, OFF_POOL, BF16, 0)], [(HEAD_DIM,), (HEAD_DIM,)])
        dp_qkv, ds["q_norm_g"], ds["k_norm_g"] = res
        dp_pool, ds["pool_w"], ds["pool_scale"] = _pool_bwd(f"l{i}_pool_bwd", sv["p"], s["pool_w"], s["pool_scale"],
                                                            dys[1], rc)
        dp_sgu, ds["sgu_ln_g"], ds["sgu_ln_b"], ds["sgu_w"], ds["sgu_b"] = _sgu_bwd(
            f"l{i}_sgu_bwd", sv["p"], s["sgu_ln_g"], s["sgu_ln_b"], s["sgu_w"], s["sgu_b"], dys[2])
        dp_cb, dp_cc, dp_cx, dcw = _conv_bwd(f"l{i}_conv_bwd", sv["p"], s["conv_w"], dys[3], rc)
        ds["conv_w"] = dcw[0:3]
        dpb = jnp.concatenate([dp_qkv, dp_pool, dp_sgu, dp_cb, dp_cc, dp_cx], axis=1)
        comm.grads(i, {"in_t": mm(f"l{i}_dWin", dpb, sv["hb"], "tn", BF16)})
        dh = mm(f"l{i}_dhb_a", dpb, w("in_t"), "nn", F32)
        dh = mm(f"l{i}_dhb_b", dgb, w("gate_t"), "nn", F32, acc=dh)

    def f_mod_bwd(xb, ddir, dhb, sc, sh):
        _, vjp = jax.vjp(_f_mod, xb, sc, sh)
        dxb, dsc, dsh = vjp(dhb)
        return dxb + ddir, dsc, dsh

    grad_x, dmods[0]["sc1"], dmods[0]["sh1"] = rows(
        "mod_in_bwd", f_mod_bwd, r, tm_n, nbc_n, [whole(xin), whole(dx_direct), whole(dh)],
        [mods[0]["sc1"], mods[0]["sh1"]], [(r - rc, d, F32, nbc_n)], [d, d])
    small_done(0)
    return loss, grad_x


def _direct_rider(src):
    def peers():
        mx, my, mc = [lax.axis_index(a) for a in MESH_AXES]
        out = []
        for kk in range(1, N_DEV):
            px = 1 - mx if kk & 4 else mx
            py = 1 - my if kk & 2 else my
            pc = 1 - mc if kk & 1 else mc
            out.append(((px, py, pc), 4 * px + 2 * py + pc))
        return 4 * mx + 2 * my + mc, out

    def start(ins, outs, sems):
        send, recv, loc = sems
        me, others = peers()
        pltpu.make_async_copy(ins[0], outs[0].at[me], loc.at[0]).start()
        for j, (peer, _) in enumerate(others):
            _rcopy(ins[0], outs[0].at[me], send, recv, j, peer).start()

    def finish(ins, outs, sems):
        send, recv, loc = sems
        me, others = peers()
        for j, (peer, peer_l) in enumerate(others):
            cp = _rcopy(ins[0], outs[0].at[peer_l], send, recv, j, peer)
            cp.wait_recv()
            cp.wait_send()
        pltpu.make_async_copy(ins[0], outs[0].at[me], loc.at[0]).wait()

    return _Rider([src], [jax.ShapeDtypeStruct((N_DEV,) + src.shape, src.dtype)],
                  _sem_scratch(N_DEV - 1, N_DEV - 1, 1), start, finish)


def _mesh_place():
    mx, my, mc = [lax.axis_index(a) for a in MESH_AXES]
    chips = [(1 - mx, my), (mx, 1 - my), (1 - mx, 1 - my)]

    def lid(px, py, pc):
        return 4 * px + 2 * py + pc

    return (mx, my, mc), (mx, my, 1 - mc), chips, lid


def _rcopy(src, dst, send_sems, recv_sems, k, to):
    return pltpu.make_async_remote_copy(src_ref=src, dst_ref=dst, send_sem=send_sems.at[k], recv_sem=recv_sems.at[k],
                                        device_id=to, device_id_type=pl.DeviceIdType.MESH)


def _sem_scratch(*sizes):
    return [pltpu.SemaphoreType.DMA((s,)) for s in sizes]


def _gather_rider(src, rows1, rows2, buf=None):
    def place():
        (mx, my, mc), sib, chips, lid = _mesh_place()
        xn, yn, dg = [(*chip, mc) for chip in chips]
        return lid(mx, my, mc), sib, xn, yn, dg, lid

    def halves(rows):
        r0, r1 = rows
        mid = r0 + (r1 - r0) // 32 * 16
        return pl.ds(r0, mid - r0), pl.ds(mid, r1 - mid), pl.ds(r0, r1 - r0)

    n_src = 1 if rows1 is not None else 0

    def start(ins, outs, sems):
        send, recv, loc = sems
        me, sib, xn, yn, dg, lid = place()
        if rows1 is not None:
            win = pl.ds(rows1[0], rows1[1] - rows1[0])
            mine, dst = ins[0].at[win], outs[0].at[me, win]
            pltpu.make_async_copy(mine, dst, loc.at[0]).start()
            for t, to in enumerate((sib, xn, yn)):
                _rcopy(mine, dst, send, recv, t, to).start()
        if rows2 is not None:
            top, bot, win = halves(rows2)
            xb, yb = outs[0].at[lid(*xn)], outs[0].at[lid(*yn)]
            _rcopy(xb.at[top], xb.at[top], send, recv, 3, yn).start()
            _rcopy(yb.at[bot], yb.at[bot], send, recv, 4, xn).start()
            _rcopy(xb.at[win], xb.at[win], send, recv, 5, sib).start()
            _rcopy(yb.at[win], yb.at[win], send, recv, 6, sib).start()

    def finish(ins, outs, sems):
        send, recv, loc = sems
        me, sib, xn, yn, dg, lid = place()
        if rows2 is not None:
            top, bot, win = halves(rows2)
            db = outs[0].at[lid(*dg)]
            _rcopy(db.at[top], db.at[top], send, recv, 3, yn).wait_recv()
            _rcopy(db.at[bot], db.at[bot], send, recv, 4, xn).wait_recv()
            _rcopy(db.at[win], db.at[win], send, recv, 7, sib).start()
            for t, dev in ((5, xn), (6, yn), (7, dg)):
                blk = outs[0].at[lid(dev[0], dev[1], 1 - dev[2]), win]
                _rcopy(blk, blk, send, recv, t, sib).wait_recv()
            for t, part in ((3, top), (4, bot), (5, win), (6, win), (7, win)):
                _rcopy(db.at[part], db.at[part], send, recv, t, sib).wait_send()
        if rows1 is not None:
            win = pl.ds(rows1[0], rows1[1] - rows1[0])
            mine, dst = ins[0].at[win], outs[0].at[me, win]
            for t, dev in enumerate((sib, xn, yn)):
                cp = _rcopy(mine, outs[0].at[lid(*dev), win], send, recv, t, dev)
                cp.wait_recv()
                cp.wait_send()
            pltpu.make_async_copy(mine, dst, loc.at[0]).wait()

    out_shape = jax.ShapeDtypeStruct((N_DEV,) + src.shape, src.dtype)
    inputs = ([src] if n_src else []) + ([buf] if buf is not None else [])
    return _Rider(inputs, [out_shape], _sem_scratch(8, 8, 1), start, finish,
                  aliases={n_src: 0} if buf is not None else {})


def _sibling_rider(part):
    def start(ins, outs, sems):
        send, recv = sems
        (mx, my, mc), sib, chips, lid = _mesh_place()
        for t, slab in enumerate([lid(*sib)] + [lid(*chip, 1 - mc) for chip in chips]):
            _rcopy(ins[0].at[slab], outs[0].at[t], send, recv, t, sib).start()

    def finish(ins, outs, sems):
        send, recv = sems
        _, sib, _, _ = _mesh_place()
        for t in range(4):
            cp = _rcopy(ins[0].at[0], outs[0].at[t], send, recv, t, sib)
            cp.wait_recv()
            cp.wait_send()

    return _Rider([part], [jax.ShapeDtypeStruct((4,) + part.shape[1:], part.dtype)], _sem_scratch(4, 4), start, finish)


def _chips_rider(pair, rows, buf=None):
    r0, r1 = rows
    win = pl.ds(r0, r1 - r0)

    def start(ins, outs, sems):
        send, recv = sems
        (mx, my, mc), sib, chips, lid = _mesh_place()
        for j, chip in enumerate(chips):
            _rcopy(ins[0].at[j, win], outs[0].at[j, win], send, recv, j, (*chip, mc)).start()

    def finish(ins, outs, sems):
        send, recv = sems
        (mx, my, mc), sib, chips, lid = _mesh_place()
        for j, chip in enumerate(chips):
            cp = _rcopy(ins[0].at[j, win], outs[0].at[j, win], send, recv, j, (*chip, mc))
            cp.wait_recv()
            cp.wait_send()

    out_shape = jax.ShapeDtypeStruct(pair.shape, pair.dtype)
    if buf is None:
        return _Rider([pair], [out_shape], _sem_scratch(3, 3), start, finish)
    return _Rider([pair, buf], [out_shape], _sem_scratch(3, 3), start, finish, aliases={1: 0})


def _run_rider(name, rider):
    n_in, n_out = len(rider.inputs), len(rider.out_shapes)

    def body(*refs):
        ins, outs, sems = refs[:n_in], refs[n_in:n_in + n_out], refs[n_in + n_out:]
        rider.start(ins, outs, sems)
        rider.finish(ins, outs, sems)

    any_spec = pl.BlockSpec(memory_space=pl.ANY)
    res = pl.pallas_call(body, name=name, in_specs=[any_spec] * n_in, out_specs=[any_spec] * n_out,
                         out_shape=rider.out_shapes, scratch_shapes=rider.scratch,
                         input_output_aliases=rider.aliases)(*rider.inputs)
    return list(res)


def _slab_ids():
    (mx, my, mc), _, chips, lid = _mesh_place()
    return jnp.stack([lid(*chip, mc) for chip in chips] + [lid(mx, my, mc)]).astype(jnp.int32)


def _pair_sum(name, part, rsib, ids):
    _, n, k = part.shape
    tr = _row_tile(n, 512, 16)

    def body(ids_ref, p_ref, r_ref, o_ref):
        o_ref[...] = (p_ref[...].astype(F32) + r_ref[...].astype(F32)).astype(o_ref.dtype)

    grid_spec = pltpu.PrefetchScalarGridSpec(
        num_scalar_prefetch=1, grid=(3, n // tr),
        in_specs=[pl.BlockSpec((None, tr, k), lambda j, i, ids: (ids[j], i, 0)),
                  pl.BlockSpec((None, tr, k), lambda j, i, ids: (1 + j, i, 0))],
        out_specs=pl.BlockSpec((None, tr, k), lambda j, i, ids: (j, i, 0)))
    return pl.pallas_call(body, name=name, grid_spec=grid_spec, out_shape=jax.ShapeDtypeStruct((3, n, k), part.dtype),
                          compiler_params=_cparams("parallel", "parallel"))(ids, part, rsib)


def _sum5(name, part, rsib, rici, ids, layer, stacked, rider=None):
    _, n, k = part.shape
    tr = _row_tile(n, 512, 16)
    first = isinstance(stacked, int)

    def body(ids_ref, p_ref, r_ref, c_ref, *rest):
        acc = p_ref[...].astype(F32) + r_ref[...].astype(F32)
        for j in range(3):
            acc = acc + c_ref[j].astype(F32)
        rest[-1][...] = acc

    in_specs = [pl.BlockSpec((None, tr, k), lambda i, ids: (ids[3], i, 0)),
                pl.BlockSpec((None, tr, k), lambda i, ids: (0, i, 0)),
                pl.BlockSpec((3, tr, k), lambda i, ids: (0, i, 0))] + ([] if first else [pl.BlockSpec(memory_space=pl.ANY)])
    n_layers = stacked if first else stacked.shape[0]
    outs, r_outs = _pcall(name, body, (n // tr,), in_specs, [pl.BlockSpec((None, tr, k), lambda i, ids: (layer, i, 0))],
                          [jax.ShapeDtypeStruct((n_layers, n, k), F32)],
                          (part, rsib, rici) + (() if first else (stacked,)), ("parallel",), rider=rider, prefetch=ids,
                          aliases={} if first else {3: 0})
    return outs[0] if rider is None else (outs[0], r_outs)


W_KEYS = ("in_t", "br0", "br1", "br2", "br3", "gate_t", "o", "ffg_t", "ffu_t", "ffd")
SUMS_TRANSPOSED_LATER = ("gate_t", "br0", "br1", "br2", "br3")


CARRIER_US = {"mod_in": 12, "in": 55, "gate": 95, "prep": 19, "attn": 112, "br0": 15, "merge": 50, "o": 25, "ln1": 27,
              "ffgu": 135, "ffd": 73, "ln2": 27, "loss": 20, "ln2_bwd": 44, "dF": 80,
              "dWffd": 64, "dh2a": 75, "dh2b": 75, "dWffg": 64, "dWffu": 64, "ln1_bwd": 44,
              "dMg": 25, "dWo": 25, "merge_bwd": 80, "dY0": 14, "dWbr0": 15, "attn_bwd": 195, "prep_bwd": 28,
              "dWin": 54, "dWgate": 95, "dhb_a": 64, "dhb_b": 115}
ICI_US_PER_MIB = 45.0
GATHER_US_PER_MIB = 22.5
RELAY_US_PER_MIB = 12.0
D2D_US_PER_MIB = 6.8
MIN_CHUNK_US = 10.0
CARRIER_FILL = 1.15


class _Comm:
    def __init__(self, wsrc):
        self.wsrc = wsrc
        self.n_layers = len(wsrc)
        self.queue = []
        self.riding = {}
        self.n_alone = 0
        self.buf, self.left = {}, {}
        self.part, self.rsib, self.pair = {}, {}, {}
        self.ids = _slab_ids()
        for i in range(self.n_layers):
            for k in W_KEYS:
                self._push_chunks("gather", ("w", i, k), wsrc[i][k].shape, wsrc[i][k].dtype)

    def _push_chunks(self, kind, item, shape, dtype):
        n, k = shape[-2], shape[-1]
        mib = n * k * jnp.dtype(dtype).itemsize / 2 ** 20
        pieces = max(1, int(mib * ICI_US_PER_MIB // MIN_CHUNK_US))
        while n % (16 * pieces):
            pieces -= 1
        step = n // pieces
        self.left[item] = n
        us = mib * (GATHER_US_PER_MIB if kind == "gather" else ICI_US_PER_MIB) / pieces
        for c in range(pieces):
            self.queue.append(dict(kind=kind, item=item, rows=(c * step, (c + 1) * step), rows2=None, us=us))

    @staticmethod
    def _merge(units, u):
        def joined(a, b):
            if a is None or b is None:
                return True, a or b
            return a[1] == b[0], (a[0], b[1])

        for v in units:
            if v["item"] == u["item"] and v["kind"] == u["kind"] and u["kind"] != "sibling":
                ok1, rows = joined(v["rows"], u["rows"])
                ok2, rows2 = joined(v["rows2"], u["rows2"])
                if ok1 and ok2:
                    v.update(rows=rows, rows2=rows2, us=v["us"] + u["us"])
                    return True
        return False

    def _unit_rider(self, u):
        item = u["item"]
        if u["kind"] == "gather":
            src = self.wsrc[item[1]][item[2]] if item[0] == "w" else self.part[item]
            return _gather_rider(src, u["rows"], u["rows2"], self.buf.get(item))
        if u["kind"] == "sibling":
            return _sibling_rider(self.part[item])
        return _chips_rider(self.pair[item], u["rows"], self.buf.get(item))

    def _done(self, u, out):
        item = u["item"]
        if u["kind"] == "sibling":
            self.rsib[item] = out
            self.pair[item] = _pair_sum(f"pair_l{item[1]}_{item[2]}", self.part[item], out, self.ids)
            self._push_chunks("chips", item, self.pair[item].shape, self.pair[item].dtype)
            return
        self.buf[item] = out
        if u["kind"] == "gather":
            if u["rows"] is not None:
                rows = u["rows"]
                mib = (rows[1] - rows[0]) * out.shape[-1] * out.dtype.itemsize / 2 ** 20
                self.queue.insert(0, dict(kind="gather", item=item, rows=None, rows2=rows, us=mib * RELAY_US_PER_MIB))
            if u["rows2"] is not None:
                self.left[item] -= u["rows2"][1] - u["rows2"][0]

    def _send(self, name, units, call):
        outs = call(_compose([self._unit_rider(u) for u in units]))
        for u, o in zip(units, outs):
            self._done(u, o)

    def exchange(self, name, src, budget_us):
        units = self._take(budget_us)
        outs = _run_rider(name, _compose([_direct_rider(src)] + [self._unit_rider(u) for u in units]))
        for u, o in zip(units, outs[1:]):
            self._done(u, o)
        return outs[0]

    def rider(self, name, budget_us=None):
        budget = CARRIER_US.get(name.split("_", 1)[1] if name[0] == "l" and name[1].isdigit() else name, 0) \
            if budget_us is None else budget_us
        units = self._take(budget)
        if not units:
            return None
        self.riding[name] = units
        return _compose([self._unit_rider(u) for u in units])

    def _take(self, budget):
        units, used = [], 0.0
        while self.queue and used + self.queue[0]["us"] <= CARRIER_FILL * budget:
            u = self.queue[0]
            if not self._merge(units, u):
                if any(v["item"] == u["item"] for v in units):
                    break
                units.append(dict(u))
            used += u["us"]
            del self.queue[0]
        return units

    def deliver(self, name, outs):
        for u, o in zip(self.riding.pop(name), outs):
            self._done(u, o)

    def _flush(self, item, kinds):
        hits = [p for p, u in enumerate(self.queue) if u["item"] == item and u["kind"] in kinds]
        if not hits:
            return
        prefix = self.queue[:hits[-1] + 1]
        del self.queue[:hits[-1] + 1]
        units = []
        for u in prefix:
            if not self._merge(units, u):
                units.append(dict(u))
        tag = "_".join(str(t) for t in item) + "_" + kinds[0]
        batches = [[]]
        for u in units:
            if any(v["item"] == u["item"] for v in batches[-1]):
                batches.append([])
            batches[-1].append(u)
        for batch in batches:
            self.n_alone += 1
            self._send(None, batch, functools.partial(_run_rider, f"alone{self.n_alone}_{tag}"))

    def begin(self):
        self._flush(("w", 0, "in_t"), ("gather",))

    def _finish_gather(self, item):
        while self.left[item] > 0:
            assert any(u["item"] == item for u in self.queue), item
            self._flush(item, ("gather",))
        return self.buf[item]

    def weight(self, i, k):
        o = self._finish_gather(("w", i, k))
        return o.reshape(-1, o.shape[-1])

    def grads(self, i, group):
        for k, g in group.items():
            item = ("g", i, k)
            self.part[item] = g.reshape(N_DEV, g.shape[0] // N_DEV, g.shape[1])
            us = g.size // N_DEV * g.dtype.itemsize / 2 ** 20 * D2D_US_PER_MIB
            self.queue.insert(0, dict(kind="sibling", item=item, rows=None, us=us))

    def total(self, k):
        out = self.n_layers
        for i in range(self.n_layers):
            item = ("g", i, k)
            self._flush(item, ("sibling",))
            self._flush(item, ("chips",))
            name = f"sum_l{i}_{k}"
            rider = self.rider(name, budget_us=self.part[item][0].size / 1.06e5) if k in SUMS_TRANSPOSED_LATER else None
            out = _sum5(name, self.part[item], self.rsib[item], self.buf[item], self.ids, i, out, rider=rider)
            if rider is not None:
                out, r_outs = out
                self.deliver(name, r_outs)
        return out

    def small_ready(self, i, dmod, ds):
        parts = [dmod[0], dmod[1]] + [ds[nm] for nm in LAYER_SMALL + ("conv_w",)]
        self.small_shapes = [p.shape for p in parts]
        self.gather_small(f"lat{i}", _pack([dmod[1]]))
        self.gather_small(f"small{i}", _pack(parts))

    def gather_small(self, name, arr):
        item = ("s", name)
        self.part[item] = arr
        waiting, self.queue = self.queue, []
        self._push_chunks("gather", item, arr.shape, arr.dtype)
        self.queue += waiting

    def gathered(self, name):
        return self._finish_gather(("s", name))


def _row_tile(n, pref, mult):
    best = None
    t = mult
    while t <= min(n, pref):
        if n % t == 0:
            best = t
        t += mult
    return best if best is not None else n


def _sum8(name, slabs):
    _, n, k = slabs.shape
    tr = _row_tile(n, 128, 16)

    def body(s_ref, o_ref):
        acc = s_ref[0].astype(F32)
        for j in range(1, N_DEV):
            acc = acc + s_ref[j].astype(F32)
        o_ref[...] = acc

    return pl.pallas_call(
        body, name=name, grid=(n // tr,),
        in_specs=[pl.BlockSpec((N_DEV, tr, k), lambda i: (0, i, 0))],
        out_specs=pl.BlockSpec((tr, k), lambda i: (i, 0)),
        out_shape=jax.ShapeDtypeStruct((n, k), F32),
        compiler_params=_cparams("parallel"),
    )(slabs)


def _adamw(name, w, g, m, v, rider=None):
    n, k = w.shape[-2:]
    tr = _row_tile(n, 256, 8)

    def body(w_ref, g_ref, m_ref, v_ref, d_ref, m2_ref, v2_ref):
        gv = g_ref[...]
        m2 = ADAM_B1 * m_ref[...] + (1.0 - ADAM_B1) * gv
        v2 = ADAM_B2 * v_ref[...] + (1.0 - ADAM_B2) * jnp.square(gv)
        m_hat = m2 / (1.0 - ADAM_B1 ** ADAM_STEP)
        v_hat = v2 / (1.0 - ADAM_B2 ** ADAM_STEP)
        d_ref[...] = -ADAM_LR * (m_hat / (jnp.sqrt(v_hat) + ADAM_EPS) + ADAM_WD * w_ref[...])
        m2_ref[...] = m2
        v2_ref[...] = v2

    if w.ndim == 2:
        grid, spec = (n // tr,), pl.BlockSpec((tr, k), lambda i: (i, 0))
    else:
        grid, spec = (w.shape[0], n // tr), pl.BlockSpec((None, tr, k), lambda l, i: (l, i, 0))
    outs, r_outs = _pcall(name, body, grid, [spec] * 4, [spec] * 3, [jax.ShapeDtypeStruct(w.shape, F32)] * 3,
                          (w, g, m, v), ("parallel",) * len(grid), rider=rider)
    return outs if rider is None else (outs, r_outs)


def _pack(arrs):
    flat = jnp.concatenate([a.reshape(-1).astype(F32) for a in arrs])
    pad = (-flat.shape[0]) % 2048
    if pad:
        flat = jnp.concatenate([flat, jnp.zeros((pad,), F32)])
    return flat.reshape(-1, 128)


def _unpack(packed, shapes):
    flat = packed.reshape(-1)
    out, off = [], 0
    for shp in shapes:
        size = math.prod(shp)
        out.append(flat[off:off + size].reshape(shp))
        off += size
    return out


WEIGHT_NAMES = ("c_ctx", "w_ada", "b_ada", "w_in", "q_norm_g", "k_norm_g", "pool_w", "pool_scale", "sgu_ln_g",
                "sgu_ln_b", "sgu_w", "sgu_b", "conv_w", "w_br_attn", "w_br_pool", "w_br_sgu", "w_br_conv", "w_gate",
                "b_gate", "w_o", "ln1_g", "ln1_b", "w_ff_gate", "w_ff_up", "w_ff_down", "ln2_g", "ln2_b")
COL_SHARDED = {"w_in": "in_t", "w_gate": "gate_t", "w_ff_gate": "ffg_t", "w_ff_up": "ffu_t",
               "w_br_attn": "br0", "w_br_pool": "br1", "w_br_sgu": "br2", "w_br_conv": "br3"}
ROW_SHARDED = {"w_o": "o", "w_ff_down": "ffd"}
LAYER_SMALL = ("q_norm_g", "k_norm_g", "pool_w", "pool_scale", "sgu_ln_g", "sgu_ln_b", "sgu_w", "sgu_b", "b_gate",
               "ln1_g", "ln1_b", "ln2_g", "ln2_b")
SMALL_ORDER = ("c_ctx", "b_ada") + LAYER_SMALL + ("conv_w",)


def _train_step(a):
    n_layers, d = a["w_in"].shape[0], a["x"].shape[-1]
    rc = a["ctx"].shape[1]
    alpha = (2 * n_layers) ** 0.25
    mx, my, mc = [lax.axis_index(ax) for ax in MESH_AXES]
    me = 4 * mx + 2 * my + mc
    ada_w = a["w_ada"].shape[-1]
    cw_loc = a["conv_w"].shape[-1]

    comm = _Comm([{**{key: jnp.swapaxes(a[nm], 1, 2)[i].astype(BF16) for nm, key in COL_SHARDED.items()},
                   **{key: a[nm][i].astype(BF16) for nm, key in ROW_SHARDED.items()}} for i in range(n_layers)])

    def carried(name, *args, budget_us, **kw):
        rider = comm.rider(name, budget_us=budget_us)
        if rider is None:
            return _mm(name, *args, **kw)
        res, r_outs = _mm(name, *args, rider=rider, **kw)
        comm.deliver(name, r_outs)
        return res

    n_c, n_cw = d, n_layers * 3 * cw_loc
    got = comm.exchange("gather_cond", _pack([a["c"], a["conv_w"]]), budget_us=12).reshape(N_DEV, -1)
    c_all = got[:, :n_c]
    conv_w = got[:, n_c:n_c + n_cw].reshape(N_DEV, n_layers, 3, cw_loc).transpose(1, 2, 0, 3).reshape(n_layers, 3, -1)
    cond = jnp.concatenate([c_all, a["c_ctx"][None], jnp.zeros((16 - N_DEV - 1, d), F32)], axis=0)
    sil, sil_vjp = jax.vjp(jax.nn.silu, cond)
    sil = sil.astype(BF16)

    mod_cols = jnp.concatenate([carried(f"ada{i}", sil, a["w_ada"][i], "nn", F32, budget_us=20)
                                for i in range(n_layers)], axis=0)
    got = comm.exchange("gather_mod", mod_cols, budget_us=22)
    mod_all = got.reshape(N_DEV, n_layers, 16, ada_w).transpose(1, 2, 0, 3).reshape(n_layers, 16, -1)
    mod_all = mod_all + a["b_ada"][:, None, :]
    mod = jnp.stack([mod_all[:, N_DEV], lax.dynamic_index_in_dim(mod_all, me, axis=1, keepdims=False)], axis=1)
    comm.begin()
    sp = [{nm: a[nm][i] for nm in LAYER_SMALL} for i in range(n_layers)]
    for i in range(n_layers):
        sp[i]["conv_w"] = conv_w[i]

    xin = jnp.concatenate([a["ctx"][0], a["x"][0]], axis=0)
    loss_l, grad_x = _local_step(xin, a["loss_target"][0], mod, comm, sp, rc, alpha)
    loss = lax.psum(loss_l, MESH_AXES)
    grads = {}

    def transposed_home(nm):
        return nm in COL_SHARDED and a[nm].shape[-1] % 128 != 0

    delta, new_m, new_v = {}, {}, {}

    def adamw(nm):
        name = f"adamw_{nm}"
        there = transposed_home(nm)
        view = (lambda t: jnp.swapaxes(t, 1, 2)) if there else (lambda t: t)
        if nm in COL_SHARDED:
            g = comm.total(COL_SHARDED[nm])
            grads[nm] = jnp.swapaxes(g, 1, 2)
            g = g if there else grads[nm]
        elif nm in ROW_SHARDED:
            g = grads[nm] = comm.total(ROW_SHARDED[nm])
        else:
            g = grads[nm]
        res = _adamw(name, view(a[nm]), g, view(a["m_" + nm]), view(a["v_" + nm]))
        delta[nm], new_m[nm], new_v[nm] = [view(t) for t in res]

    for nm in ("w_ff_down", "w_ff_gate", "w_ff_up", "w_o", "w_br_attn", "w_br_pool", "w_br_sgu", "w_br_conv"):
        adamw(nm)

    tots = [_unpack(_sum8(f"sum_small{i}", comm.gathered(f"small{i}")), comm.small_shapes) for i in range(n_layers)]
    dmod_c, dmod_lat_sum = jnp.stack([t[0] for t in tots]), jnp.stack([t[1] for t in tots])
    for j, nm in enumerate(LAYER_SMALL + ("conv_w",)):
        grads[nm] = jnp.stack([t[2 + j] for t in tots])
    grads["conv_w"] = lax.dynamic_slice_in_dim(grads["conv_w"], me * cw_loc, cw_loc, axis=2)
    grads["b_ada"] = dmod_c + dmod_lat_sum
    dmod_lat_all = jnp.stack([comm.gathered(f"lat{i}").reshape(N_DEV, -1)[:, :6 * d] for i in range(n_layers)])
    dm_rows = jnp.concatenate([dmod_lat_all, dmod_c[:, None, :],
                               jnp.zeros((n_layers, 16 - N_DEV - 1, 6 * d), F32)], axis=1)
    dm_cols = lax.dynamic_slice_in_dim(dm_rows, me * ada_w, ada_w, axis=2).astype(BF16)
    grads["w_ada"] = jnp.stack([carried(f"dWada{i}", sil, dm_cols[i], "tn", F32, budget_us=20)
                                for i in range(n_layers)])
    dsil = None
    for i in range(n_layers):
        dsil = carried(f"dsil{i}", dm_cols[i], a["w_ada"][i], "nt", F32, acc=dsil, budget_us=10)
    dsil = _sum8("sum_dsil", comm.exchange("gather_dsil", dsil, budget_us=17))
    grads["c_ctx"] = sil_vjp(dsil)[0][N_DEV]

    for nm in ("w_ada", "w_in", "w_gate"):
        adamw(nm)
    shapes = [a[nm].shape for nm in SMALL_ORDER]
    res = _adamw("adamw_small", _pack([a[nm] for nm in SMALL_ORDER]), _pack([grads[nm] for nm in SMALL_ORDER]),
                 _pack([a["m_" + nm] for nm in SMALL_ORDER]), _pack([a["v_" + nm] for nm in SMALL_ORDER]))
    for tree, packed in zip((delta, new_m, new_v), res):
        for nm, t in zip(SMALL_ORDER, _unpack(packed, shapes)):
            tree[nm] = t
    return (loss, grad_x[None], *[grads[nm] for nm in WEIGHT_NAMES], *[delta[nm] for nm in WEIGHT_NAMES],
            *[new_m[nm] for nm in WEIGHT_NAMES], *[new_v[nm] for nm in WEIGHT_NAMES])


def kernel(x, c, ctx, c_ctx, w_ada, b_ada, w_in, q_norm_g, k_norm_g, pool_w, pool_scale, sgu_ln_g, sgu_ln_b, sgu_w, sgu_b, conv_w, w_br_attn, w_br_pool, w_br_sgu, w_br_conv, w_gate, b_gate, w_o, ln1_g, ln1_b, w_ff_gate, w_ff_up, w_ff_down, ln2_g, ln2_b, loss_target, m_c_ctx, m_w_ada, m_b_ada, m_w_in, m_q_norm_g, m_k_norm_g, m_pool_w, m_pool_scale, m_sgu_ln_g, m_sgu_ln_b, m_sgu_w, m_sgu_b, m_conv_w, m_w_br_attn, m_w_br_pool, m_w_br_sgu, m_w_br_conv, m_w_gate, m_b_gate, m_w_o, m_ln1_g, m_ln1_b, m_w_ff_gate, m_w_ff_up, m_w_ff_down, m_ln2_g, m_ln2_b, v_c_ctx, v_w_ada, v_b_ada, v_w_in, v_q_norm_g, v_k_norm_g, v_pool_w, v_pool_scale, v_sgu_ln_g, v_sgu_ln_b, v_sgu_w, v_sgu_b, v_conv_w, v_w_br_attn, v_w_br_pool, v_w_br_sgu, v_w_br_conv, v_w_gate, v_b_gate, v_w_o, v_ln1_g, v_ln1_b, v_w_ff_gate, v_w_ff_up, v_w_ff_down, v_ln2_g, v_ln2_b):
    names = list(WEIGHT_NAMES)
    args = dict(zip(
        ["x", "c", "ctx"] + names + ["loss_target"] + ["m_" + n for n in names] + ["v_" + n for n in names],
        (x, c, ctx, c_ctx, w_ada, b_ada, w_in, q_norm_g, k_norm_g, pool_w, pool_scale, sgu_ln_g, sgu_ln_b, sgu_w, sgu_b, conv_w, w_br_attn, w_br_pool, w_br_sgu, w_br_conv, w_gate, b_gate, w_o, ln1_g, ln1_b, w_ff_gate, w_ff_up, w_ff_down, ln2_g, ln2_b, loss_target, m_c_ctx, m_w_ada, m_b_ada, m_w_in, m_q_norm_g, m_k_norm_g, m_pool_w, m_pool_scale, m_sgu_ln_g, m_sgu_ln_b, m_sgu_w, m_sgu_b, m_conv_w, m_w_br_attn, m_w_br_pool, m_w_br_sgu, m_w_br_conv, m_w_gate, m_b_gate, m_w_o, m_ln1_g, m_ln1_b, m_w_ff_gate, m_w_ff_up, m_w_ff_down, m_ln2_g, m_ln2_b, v_c_ctx, v_w_ada, v_b_ada, v_w_in, v_q_norm_g, v_k_norm_g, v_pool_w, v_pool_scale, v_sgu_ln_g, v_sgu_ln_b, v_sgu_w, v_sgu_b, v_conv_w, v_w_br_attn, v_w_br_pool, v_w_br_sgu, v_w_br_conv, v_w_gate, v_b_gate, v_w_o, v_ln1_g, v_ln1_b, v_w_ff_gate, v_w_ff_up, v_w_ff_down, v_ln2_g, v_ln2_b)))
    return _train_step(args)
```

```python
import functools
import math

import jax
import jax.numpy as jnp
from jax import lax
from jax.experimental import pallas as pl
from jax.experimental.pallas import tpu as pltpu

F32 = jnp.float32
BF16 = jnp.bfloat16

N_DEV = 8
MESH_AXES = ("x", "y", "c")
V7X_VMEM_LIMIT_BYTES = 56 * 1024 * 1024

GRID_W = 64
HEAD_DIM = 128
N_HEADS = 8
N_KV_HEADS = 2
KV_GROUP = N_HEADS // N_KV_HEADS
Q_W = N_HEADS * HEAD_DIM
KV_W = N_KV_HEADS * HEAD_DIM
ROPE_THETA = 10000.0
ROPE_AXIS_DIM = HEAD_DIM // 2
POOL_WINDOWS = (2, 4, 8, 16)
GC = 128
N_GROUPS = 4
BR_W = N_GROUPS * GC
SGU_CHUNK = 128
N_BRANCH = 4
LN_EPS = 1e-5
RMS_EPS = 1e-6
OFF_K = Q_W
OFF_V = OFF_K + KV_W
OFF_POOL = OFF_V + KV_W
OFF_U = OFF_POOL + BR_W
OFF_VG = OFF_U + BR_W
OFF_CB = OFF_VG + BR_W
OFF_CC = OFF_CB + BR_W
OFF_CX = OFF_CC + BR_W
IN_W = OFF_CX + BR_W
ATT_SCALE = HEAD_DIM ** -0.5

ADAM_LR = 0.001
ADAM_B1 = 0.9
ADAM_B2 = 0.999
ADAM_EPS = 1e-08
ADAM_WD = 0.01
ADAM_STEP = 10

_NT = (((1,), (1,)), ((), ()))
_NN = (((1,), (0,)), ((), ()))
_TN = (((0,), (0,)), ((), ()))
_DIMS = {"nt": _NT, "nn": _NN, "tn": _TN}


def _cparams(*sem):
    return pltpu.CompilerParams(dimension_semantics=sem, vmem_limit_bytes=V7X_VMEM_LIMIT_BYTES)


def _tile(dim, pref):
    best = None
    t = 128
    while t <= min(dim, pref):
        if dim % t == 0:
            best = t
        t += 128
    return best if best is not None else dim


def _dot(a, b, dims):
    return lax.dot_general(a.astype(BF16), b.astype(BF16), dims, preferred_element_type=F32)


class _Rider:
    def __init__(self, inputs, out_shapes, scratch, start, finish, aliases=None):
        self.inputs, self.out_shapes, self.scratch = list(inputs), list(out_shapes), list(scratch)
        self.start, self.finish = start, finish
        self.aliases = dict(aliases or {})


def _compose(riders):
    inputs, outs, scratch, aliases, spans = [], [], [], {}, []
    for rd in riders:
        i0, o0, s0 = len(inputs), len(outs), len(scratch)
        aliases.update({i0 + p: o0 + q for p, q in rd.aliases.items()})
        inputs += rd.inputs
        outs += rd.out_shapes
        scratch += rd.scratch
        spans.append((slice(i0, len(inputs)), slice(o0, len(outs)), slice(s0, len(scratch))))

    def start(ins, os, sems):
        for rd, (si, so, ss) in zip(riders, spans):
            rd.start(ins[si], os[so], sems[ss])

    def finish(ins, os, sems):
        for rd, (si, so, ss) in zip(riders, spans):
            rd.finish(ins[si], os[so], sems[ss])

    return _Rider(inputs, outs, scratch, start, finish, aliases)


def _pcall(name, body, grid, in_specs, out_specs, out_shape, args, sem, scratch=(), rider=None, prefetch=None,
           aliases=None):
    in_specs, out_specs, out_shape, scratch = list(in_specs), list(out_specs), list(out_shape), list(scratch)
    n_pre = 0 if prefetch is None else 1
    n_in, n_out, n_scr = len(in_specs), len(out_specs), len(scratch)
    r_in, r_out = (len(rider.inputs), len(rider.out_shapes)) if rider is not None else (0, 0)
    any_spec = pl.BlockSpec(memory_space=pl.ANY)
    io_aliases = {n_pre + p: q for p, q in (aliases or {}).items()}
    if rider is not None:
        io_aliases.update({n_pre + n_in + p: n_out + q for p, q in rider.aliases.items()})
        in_specs, out_specs = in_specs + [any_spec] * r_in, out_specs + [any_spec] * r_out
        out_shape, scratch = out_shape + rider.out_shapes, scratch + rider.scratch
        args, sem = (*args, *rider.inputs), ["arbitrary"] * len(grid)

    def wrapped(*refs):
        pre, refs = refs[:n_pre], refs[n_pre:]
        if rider is None:
            return body(*pre, *refs)
        ins, refs = refs[:n_in], refs[n_in:]
        r_ins, refs = refs[:r_in], refs[r_in:]
        outs, refs = refs[:n_out], refs[n_out:]
        r_outs, refs = refs[:r_out], refs[r_out:]
        scr, r_scr = refs[:n_scr], refs[n_scr:]
        first = functools.reduce(jnp.logical_and, [pl.program_id(ax) == 0 for ax in range(len(grid))])
        last = functools.reduce(jnp.logical_and, [pl.program_id(ax) == grid[ax] - 1 for ax in range(len(grid))])

        @pl.when(first)
        def _():
            rider.start(r_ins, r_outs, r_scr)

        body(*pre, *ins, *outs, *scr)

        @pl.when(last)
        def _():
            rider.finish(r_ins, r_outs, r_scr)

    if prefetch is None:
        res = pl.pallas_call(wrapped, name=name, grid=grid, in_specs=in_specs, out_specs=out_specs, out_shape=out_shape,
                             scratch_shapes=scratch, input_output_aliases=io_aliases,
                             compiler_params=_cparams(*sem))(*args)
    else:
        grid_spec = pltpu.PrefetchScalarGridSpec(num_scalar_prefetch=1, grid=grid, in_specs=in_specs,
                                                 out_specs=out_specs, scratch_shapes=scratch)
        res = pl.pallas_call(wrapped, name=name, grid_spec=grid_spec, out_shape=out_shape,
                             input_output_aliases=io_aliases, compiler_params=_cparams(*sem))(prefetch, *args)
    return list(res[:n_out]), list(res[n_out:])


V7X_MM_VMEM_BUDGET = 40 * 1024 * 1024


def _mm_plan(form, m, n, k, a_size, b_size, o_size, has_acc):
    tn = n if (form == "tn" and n <= 2048) else _tile(n, 512)
    rows = sorted({m} | {t for t in range(128, m, 128) if m % t == 0}, reverse=True)
    for tk, min_tm in ((k, 384), (k if k <= 2816 else _tile(k, 2816), 0)):
        nk = k // tk
        for tm in rows:
            need = 2 * (tm * tk * a_size + tn * tk * b_size + tm * tn * o_size) + tm * tn * 4 * (2 if nk > 1 else 1)
            need += 2 * tm * tn * 4 if has_acc else 0
            if need <= V7X_MM_VMEM_BUDGET and tm >= min(min_tm, m):
                return tm, tn, tk
    return _tile(m, 128), tn, tk


def _mm(name, a, b, form, out_dtype, acc=None, rider=None):
    if form == "nt":
        (m, k), (n, k2) = a.shape, b.shape
    elif form == "nn":
        (m, k), (k2, n) = a.shape, b.shape
    else:
        (k, m), (k2, n) = a.shape, b.shape
    assert k == k2, (name, a.shape, b.shape)
    has_acc = acc is not None
    tm, tn, tk = _mm_plan(form, m, n, k, a.dtype.itemsize, b.dtype.itemsize, jnp.dtype(out_dtype).itemsize, has_acc)
    nk = k // tk
    a_spec = {"nt": pl.BlockSpec((tm, tk), lambda i, j, kk: (i, kk)),
              "nn": pl.BlockSpec((tm, tk), lambda i, j, kk: (i, kk)),
              "tn": pl.BlockSpec((tk, tm), lambda i, j, kk: (kk, i))}[form]
    b_spec = {"nt": pl.BlockSpec((tn, tk), lambda i, j, kk: (j, kk)),
              "nn": pl.BlockSpec((tk, tn), lambda i, j, kk: (kk, j)),
              "tn": pl.BlockSpec((tk, tn), lambda i, j, kk: (kk, j))}[form]
    o_spec = pl.BlockSpec((tm, tn), lambda i, j, kk: (i, j))
    dims = _DIMS[form]

    def body(*refs):
        a_ref, b_ref = refs[0], refs[1]
        c_ref = refs[2] if has_acc else None
        o_ref = refs[3] if has_acc else refs[2]

        def finish(r):
            if has_acc:
                r = r + c_ref[...]
            o_ref[...] = r.astype(o_ref.dtype)

        if nk == 1:
            finish(_dot(a_ref[...], b_ref[...], dims))
            return
        acc_ref = refs[-1]
        kk = pl.program_id(2)

        @pl.when(kk == 0)
        def _():
            acc_ref[...] = _dot(a_ref[...], b_ref[...], dims)

        @pl.when(kk > 0)
        def _():
            acc_ref[...] += _dot(a_ref[...], b_ref[...], dims)

        @pl.when(kk == nk - 1)
        def _():
            finish(acc_ref[...])

    in_specs = [a_spec, b_spec] + ([o_spec] if has_acc else [])
    args = (a, b) + ((acc,) if has_acc else ())
    outs, r_outs = _pcall(name, body, (m // tm, n // tn, nk), in_specs, [o_spec],
                          [jax.ShapeDtypeStruct((m, n), out_dtype)], args, ("parallel", "parallel", "arbitrary"),
                          scratch=[pltpu.VMEM((tm, tn), F32)] if nk > 1 else [], rider=rider)
    return outs[0] if rider is None else (outs[0], r_outs)


def _mm_fused(name, a, bs, epilogue, tile_ins, out_dtypes, rider=None):
    (m, k), (n, _) = a.shape, bs[0].shape
    tn = _tile(n, 512)
    tm = None
    for cand in sorted({m} | {t for t in range(128, m, 128) if m % t == 0}, reverse=True):
        per_tile = sum(t.dtype.itemsize for t in tile_ins) + sum(jnp.dtype(d).itemsize for d in out_dtypes)
        need = 2 * (cand * k * a.dtype.itemsize + len(bs) * tn * k * bs[0].dtype.itemsize + cand * tn * per_tile)
        need += (len(bs) + 2) * cand * tn * 4
        if need <= V7X_MM_VMEM_BUDGET:
            tm = cand
            break
    assert tm is not None, name
    n_b, n_t = len(bs), len(tile_ins)
    tile = pl.BlockSpec((tm, tn), lambda i, j: (i, j))

    def body(a_ref, *refs):
        prods = [_dot(a_ref[...], r[...], _NT) for r in refs[:n_b]]
        outs = epilogue(prods, [r[...].astype(F32) for r in refs[n_b:n_b + n_t]])
        for r, o in zip(refs[n_b + n_t:], outs):
            r[...] = o.astype(r.dtype)

    outs, r_outs = _pcall(name, body, (m // tm, n // tn),
                          [pl.BlockSpec((tm, k), lambda i, j: (i, 0))] + [pl.BlockSpec((tn, k), lambda i, j: (j, 0))] * n_b
                          + [tile] * n_t, [tile] * len(out_dtypes),
                          [jax.ShapeDtypeStruct((m, n), d) for d in out_dtypes], (a, *bs, *tile_ins),
                          ("parallel", "parallel"), rider=rider)
    return outs if rider is None else (outs, r_outs)


def _rows(name, fn, n_rows, tm, nbc, row_ins, type_ins, row_outs, acc_outs, rider=None):
    n_ri, n_ti, n_ro, n_ao = len(row_ins), len(type_ins), len(row_outs), len(acc_outs)

    def row_map(i, cb, roff):
        return (jnp.maximum(i - roff, 0), cb)

    def type_map(i):
        return (jnp.where(i >= nbc, 1, 0), 0, 0)

    in_specs, args = [], []
    for arr, cb, width, roff in row_ins:
        in_specs.append(pl.BlockSpec((tm, width), functools.partial(row_map, cb=cb, roff=roff)))
        args.append(arr)
    def shared_map(i):
        return (0, 0, 0)

    for arr in type_ins:
        in_specs.append(pl.BlockSpec((None, 1, arr.shape[-1]), type_map if arr.shape[0] == 2 else shared_map))
        args.append(arr)
    out_shape, out_specs = [], []
    for total, width, dtype, roff in row_outs:
        out_shape.append(jax.ShapeDtypeStruct((total, width), dtype))
        out_specs.append(pl.BlockSpec((tm, width), functools.partial(row_map, cb=0, roff=roff)))
    acc_shared = [isinstance(w, tuple) for w in acc_outs]
    for width in acc_outs:
        if isinstance(width, tuple):
            out_shape.append(jax.ShapeDtypeStruct((1, 1, width[0]), F32))
            out_specs.append(pl.BlockSpec((None, 1, width[0]), shared_map))
        else:
            out_shape.append(jax.ShapeDtypeStruct((2, 1, width), F32))
            out_specs.append(pl.BlockSpec((None, 1, width), type_map))
    n_in = n_ri + n_ti

    def body(*refs):
        i = pl.program_id(0)
        outs = fn(*[r[...].astype(F32) for r in refs[:n_in]])
        if not isinstance(outs, (tuple, list)):
            outs = (outs,)
        assert len(outs) == n_ro + n_ao, (name, len(outs))
        for r, o in zip(refs[n_in:n_in + n_ro], outs[:n_ro]):
            r[...] = o.astype(r.dtype)
        if n_ao:
            for r, o, shared in zip(refs[n_in + n_ro:], outs[n_ro:], acc_shared):
                first = i == 0 if shared else jnp.logical_or(i == 0, i == nbc)
                o = jnp.broadcast_to(o.astype(F32), r.shape)

                @pl.when(first)
                def _(r=r, o=o):
                    r[...] = o

                @pl.when(jnp.logical_not(first))
                def _(r=r, o=o):
                    r[...] += o

    outs, r_outs = _pcall(name, body, (n_rows // tm,), in_specs, out_specs, out_shape, args, ("arbitrary",),
                          rider=rider)
    return outs if rider is None else (outs, r_outs)


def _vjp_fn(f, n_row, n_cot, keep=None):
    def g(*args):
        prim = args[:n_row] + args[n_row + n_cot:]
        cots = args[n_row:n_row + n_cot]
        out, vjp = jax.vjp(f, *prim)
        grads = vjp(tuple(cots) if isinstance(out, (tuple, list)) else cots[0])
        return grads if keep is None else tuple(grads[j] for j in keep)
    return g


def _typed(v):
    return v.reshape(1, 1, -1)


def _ln(x, g, b):
    mu = jnp.mean(x, axis=-1, keepdims=True)
    var = jnp.mean(jnp.square(x - mu), axis=-1, keepdims=True)
    return (x - mu) * lax.rsqrt(var + LN_EPS) * g + b


def _f_mod(x, sc, sh):
    return x * (1.0 + sc) + sh


def _make_f_ln(alpha, with_mod):
    def f(x, o, gate, lng, lnb, *mod):
        xn = _ln(alpha * x + gate * o, lng, lnb)
        if with_mod:
            sc, sh = mod
            return xn, xn * (1.0 + sc) + sh
        return xn
    return f


@jax.custom_vjp
def _rot(y):
    lane = lax.broadcasted_iota(jnp.int32, y.shape, 1)
    return jnp.where(lane % 64 < 32, pltpu.roll(y, 96, axis=1), pltpu.roll(y, 32, axis=1))


_rot.defvjp(lambda y: (_rot(y), None), lambda _, g: (_rot(g),))


def _f_prep(p, cos, sin, qg, kg):
    def head(xh, g):
        ms = jnp.mean(jnp.square(xh), axis=-1, keepdims=True)
        y = xh * lax.rsqrt(ms + RMS_EPS) * g
        return y * cos + _rot(y) * sin
    q = jnp.concatenate([head(p[:, h * HEAD_DIM:(h + 1) * HEAD_DIM], qg) for h in range(N_HEADS)], axis=1)
    k = jnp.concatenate([head(p[:, OFF_K + h * HEAD_DIM:OFF_K + (h + 1) * HEAD_DIM], kg)
                         for h in range(N_KV_HEADS)], axis=1)
    return q, k, p[:, OFF_V:OFF_POOL]


def _f_gate(g, t0, t1, t2, t3, b):
    d = t0.shape[-1]
    ts = (t0, t1, t2, t3)
    terms = [jax.nn.sigmoid(g[:, k * d:(k + 1) * d] + b[:, k * d:(k + 1) * d]) * ts[k] for k in range(N_BRANCH)]
    return terms[0] + terms[1] + terms[2] + terms[3]


def _f_swiglu(a, b):
    return jax.nn.silu(a) * b


def _softmax(raw):
    e = jnp.exp2((raw - jnp.max(raw, axis=-1, keepdims=True)) * (ATT_SCALE * math.log2(math.e)))
    return e / jnp.sum(e, axis=-1, keepdims=True)


def _attn_fwd(name, q, k, v, rc, ctx_queries, tq=256, rider=None):
    r = q.shape[0]
    assert rc % tq == 0 and r % tq == 0
    nqc = rc // tq

    def body(q_ref, k_ref, v_ref, o_ref, p_ref):
        qi = pl.program_id(1)

        def attend(nk):
            p = _softmax(_dot(q_ref[...], k_ref[0:nk, :], _NT)).astype(BF16)
            p_ref[:, 0:nk] = p
            o_ref[...] = _dot(p, v_ref[0:nk, :], _NN).astype(o_ref.dtype)

        @pl.when(qi < nqc)
        def _():
            if ctx_queries:
                attend(rc)
            else:
                o_ref[...] = jnp.zeros_like(o_ref)

        @pl.when(qi >= nqc)
        def _():
            attend(r)

    outs, r_outs = _pcall(
        name, body, (N_HEADS, r // tq),
        [pl.BlockSpec((tq, HEAD_DIM), lambda h, i: (i, h)),
         pl.BlockSpec((r, HEAD_DIM), lambda h, i: (0, h // KV_GROUP)),
         pl.BlockSpec((r, HEAD_DIM), lambda h, i: (0, h // KV_GROUP))],
        [pl.BlockSpec((tq, HEAD_DIM), lambda h, i: (i, h)), pl.BlockSpec((None, tq, r), lambda h, i: (h, i, 0))],
        [jax.ShapeDtypeStruct((r, Q_W), BF16), jax.ShapeDtypeStruct((N_HEADS, r, r), BF16)], (q, k, v),
        ("parallel", "parallel"), rider=rider)
    return outs if rider is None else (outs, r_outs)


def _attn_bwd(name, q, k, v, pw, do, rc, ctx_queries, tq=256, rider=None):
    r = q.shape[0]
    nqc = rc // tq

    def body(q_ref, k_ref, v_ref, p_ref, do_ref, dq_ref, dk_ref, dv_ref):
        g, qi = pl.program_id(1), pl.program_id(2)

        @pl.when(jnp.logical_and(g == 0, qi == 0))
        def _():
            dk_ref[...] = jnp.zeros_like(dk_ref)
            dv_ref[...] = jnp.zeros_like(dv_ref)

        def grad(nk):
            qb, kb, vb = q_ref[...], k_ref[0:nk, :], v_ref[0:nk, :]
            dob = do_ref[...].astype(BF16)
            p = p_ref[:, 0:nk].astype(F32)
            dv_ref[0:nk, :] += _dot(p, dob, _TN)
            dp = _dot(dob, vb, _NT)
            ds = p * (dp - jnp.sum(dp * p, axis=-1, keepdims=True)) * ATT_SCALE
            dq_ref[...] = _dot(ds, kb, _NN)
            dk_ref[0:nk, :] += _dot(ds, qb, _TN)

        @pl.when(qi < nqc)
        def _():
            if ctx_queries:
                grad(rc)
            else:
                dq_ref[...] = jnp.zeros_like(dq_ref)

        @pl.when(qi >= nqc)
        def _():
            grad(r)

    def qmap(kv, g, i):
        return (i, kv * KV_GROUP + g)

    def kvmap(kv, g, i):
        return (0, kv)

    outs, r_outs = _pcall(
        name, body, (N_KV_HEADS, KV_GROUP, r // tq),
        [pl.BlockSpec((tq, HEAD_DIM), qmap), pl.BlockSpec((r, HEAD_DIM), kvmap), pl.BlockSpec((r, HEAD_DIM), kvmap),
         pl.BlockSpec((None, tq, r), lambda kv, g, i: (kv * KV_GROUP + g, i, 0)), pl.BlockSpec((tq, HEAD_DIM), qmap)],
        [pl.BlockSpec((tq, HEAD_DIM), qmap), pl.BlockSpec((r, HEAD_DIM), kvmap), pl.BlockSpec((r, HEAD_DIM), kvmap)],
        [jax.ShapeDtypeStruct((r, Q_W), F32), jax.ShapeDtypeStruct((r, KV_W), F32),
         jax.ShapeDtypeStruct((r, KV_W), F32)],
        (q, k, v, pw, do), ("arbitrary", "arbitrary", "arbitrary"), rider=rider)
    return outs if rider is None else (outs, r_outs)


def _segments(shape, rc):
    t = lax.broadcasted_iota(jnp.int32, shape, 0)
    lo = jnp.where(t < rc, 0, rc)
    hi = jnp.where(t < rc, rc, shape[0])
    return t, lo, hi


def _shifted(x, o, t, lo, hi):
    n = x.shape[0]
    sh = pltpu.roll(x, (-o) % n, axis=0)
    return jnp.where(jnp.logical_and(t + o >= lo, t + o < hi), sh, 0.0)


def _winsum(x, left, right, t, lo, hi):
    acc = x
    for o in range(-left, right + 1):
        if o != 0:
            acc = acc + _shifted(x, o, t, lo, hi)
    return acc


def _pool_parts(z, g, t, lo, hi):
    w = POOL_WINDOWS[g]
    left = w // 2
    right = w - 1 - left
    count = (jnp.minimum(t + right + 1, hi) - jnp.maximum(t - left, lo)).astype(F32)
    return _winsum(z, left, right, t, lo, hi) / count - z, count, left, right


def _pool_fwd(name, p, pool_w, pool_scale, rc):
    r = p.shape[0]

    def body(z_ref, w_ref, s_ref, y_ref):
        t, lo, hi = _segments((r, GC), rc)
        for g in range(N_GROUPS):
            cols = slice(g * GC, (g + 1) * GC)
            d, _, _, _ = _pool_parts(z_ref[:, cols], g, t, lo, hi)
            y_ref[:, cols] = (_dot(d, w_ref[g], _NN) * s_ref[:, cols]).astype(y_ref.dtype)

    return pl.pallas_call(
        body, name=name, grid=(1,),
        in_specs=[pl.BlockSpec((r, BR_W), lambda i: (0, OFF_POOL // BR_W)),
                  pl.BlockSpec((N_GROUPS, GC, GC), lambda i: (0, 0, 0)),
                  pl.BlockSpec((1, BR_W), lambda i: (0, 0))],
        out_specs=pl.BlockSpec((r, BR_W), lambda i: (0, 0)),
        out_shape=jax.ShapeDtypeStruct((r, BR_W), BF16),
        compiler_params=_cparams("arbitrary"),
    )(p, pool_w, pool_scale.reshape(1, BR_W))


def _pool_bwd(name, p, pool_w, pool_scale, dy, rc):
    r = p.shape[0]

    def body(z_ref, w_ref, s_ref, dy_ref, dz_ref, dw_ref, ds_ref):
        t, lo, hi = _segments((r, GC), rc)
        for g in range(N_GROUPS):
            cols = slice(g * GC, (g + 1) * GC)
            d, count, left, right = _pool_parts(z_ref[:, cols], g, t, lo, hi)
            dyg = dy_ref[:, cols]
            ds_ref[:, cols] = jnp.sum(dyg * _dot(d, w_ref[g], _NN), axis=0, keepdims=True)
            dlin = dyg * s_ref[:, cols]
            dw_ref[g] = _dot(d, dlin, _TN)
            dd = _dot(dlin, w_ref[g], _NT)
            dz_ref[:, cols] = (_winsum(dd / count, right, left, t, lo, hi) - dd).astype(dz_ref.dtype)

    return pl.pallas_call(
        body, name=name, grid=(1,),
        in_specs=[pl.BlockSpec((r, BR_W), lambda i: (0, OFF_POOL // BR_W)),
                  pl.BlockSpec((N_GROUPS, GC, GC), lambda i: (0, 0, 0)),
                  pl.BlockSpec((1, BR_W), lambda i: (0, 0)),
                  pl.BlockSpec((r, BR_W), lambda i: (0, 0))],
        out_specs=[pl.BlockSpec((r, BR_W), lambda i: (0, 0)),
                   pl.BlockSpec((N_GROUPS, GC, GC), lambda i: (0, 0, 0)),
                   pl.BlockSpec((1, BR_W), lambda i: (0, 0))],
        out_shape=[jax.ShapeDtypeStruct((r, BR_W), BF16), jax.ShapeDtypeStruct((N_GROUPS, GC, GC), F32),
                   jax.ShapeDtypeStruct((1, BR_W), F32)],
        compiler_params=_cparams("arbitrary"),
    )(p, pool_w, pool_scale.reshape(1, BR_W), dy)


def _f_sgu_v(pvg, lng, lnb):
    return _ln(jax.nn.gelu(pvg), lng, lnb)


def _sgu_fwd(name, p, ln_g, ln_b, sgu_w, sgu_b):
    r = p.shape[0]

    def body(pu_ref, pv_ref, g_ref, b_ref, w_ref, sb_ref, y_ref):
        vn = _f_sgu_v(pv_ref[...], g_ref[...], b_ref[...])
        u = jax.nn.gelu(pu_ref[...])
        for g in range(N_GROUPS):
            cols = slice(g * GC, (g + 1) * GC)
            s = _dot(w_ref[g], vn[:, cols], _NN) + sb_ref[g]
            y_ref[:, cols] = (u[:, cols] * s).astype(y_ref.dtype)

    return pl.pallas_call(
        body, name=name, grid=(r // SGU_CHUNK,),
        in_specs=[pl.BlockSpec((SGU_CHUNK, BR_W), lambda i: (i, OFF_U // BR_W)),
                  pl.BlockSpec((SGU_CHUNK, BR_W), lambda i: (i, OFF_VG // BR_W)),
                  pl.BlockSpec((1, BR_W), lambda i: (0, 0)), pl.BlockSpec((1, BR_W), lambda i: (0, 0)),
                  pl.BlockSpec((N_GROUPS, GC, GC), lambda i: (0, 0, 0)),
                  pl.BlockSpec((N_GROUPS, SGU_CHUNK, 1), lambda i: (0, 0, 0))],
        out_specs=pl.BlockSpec((SGU_CHUNK, BR_W), lambda i: (i, 0)),
        out_shape=jax.ShapeDtypeStruct((r, BR_W), BF16),
        compiler_params=_cparams("parallel"),
    )(p, p, ln_g.reshape(1, BR_W), ln_b.reshape(1, BR_W), sgu_w, sgu_b.reshape(N_GROUPS, SGU_CHUNK, 1))


def _sgu_bwd(name, p, ln_g, ln_b, sgu_w, sgu_b, dy):
    r = p.shape[0]

    def body(pu_ref, pv_ref, g_ref, b_ref, w_ref, sb_ref, dy_ref, dp_ref, dg_ref, db_ref, dw_ref, dsb_ref):
        i = pl.program_id(0)

        @pl.when(i == 0)
        def _():
            for ref in (dg_ref, db_ref, dw_ref, dsb_ref):
                ref[...] = jnp.zeros_like(ref)

        vn, vjp_v = jax.vjp(_f_sgu_v, pv_ref[...], g_ref[...], b_ref[...])
        u, vjp_u = jax.vjp(jax.nn.gelu, pu_ref[...])
        dy = dy_ref[...]
        du, dvn = [], []
        for g in range(N_GROUPS):
            cols = slice(g * GC, (g + 1) * GC)
            s = _dot(w_ref[g], vn[:, cols], _NN) + sb_ref[g]
            du.append(dy[:, cols] * s)
            ds = dy[:, cols] * u[:, cols]
            dsb_ref[g] += jnp.sum(ds, axis=1, keepdims=True)
            dw_ref[g] += _dot(ds, vn[:, cols], _NT)
            dvn.append(_dot(w_ref[g], ds, _TN))
        (dpu,) = vjp_u(jnp.concatenate(du, axis=1))
        dpv, dg, db = vjp_v(jnp.concatenate(dvn, axis=1))
        dp_ref[:, 0:BR_W] = dpu.astype(dp_ref.dtype)
        dp_ref[:, BR_W:2 * BR_W] = dpv.astype(dp_ref.dtype)
        dg_ref[...] += dg
        db_ref[...] += db

    vec = pl.BlockSpec((1, BR_W), lambda i: (0, 0))
    wsp = pl.BlockSpec((N_GROUPS, GC, GC), lambda i: (0, 0, 0))
    bsp = pl.BlockSpec((N_GROUPS, SGU_CHUNK, 1), lambda i: (0, 0, 0))
    return pl.pallas_call(
        body, name=name, grid=(r // SGU_CHUNK,),
        in_specs=[pl.BlockSpec((SGU_CHUNK, BR_W), lambda i: (i, OFF_U // BR_W)),
                  pl.BlockSpec((SGU_CHUNK, BR_W), lambda i: (i, OFF_VG // BR_W)),
                  vec, vec, wsp, bsp, pl.BlockSpec((SGU_CHUNK, BR_W), lambda i: (i, 0))],
        out_specs=[pl.BlockSpec((SGU_CHUNK, 2 * BR_W), lambda i: (i, 0)), vec, vec, wsp, bsp],
        out_shape=[jax.ShapeDtypeStruct((r, 2 * BR_W), BF16), jax.ShapeDtypeStruct((1, BR_W), F32),
                   jax.ShapeDtypeStruct((1, BR_W), F32), jax.ShapeDtypeStruct((N_GROUPS, GC, GC), F32),
                   jax.ShapeDtypeStruct((N_GROUPS, SGU_CHUNK, 1), F32)],
        compiler_params=_cparams("arbitrary"),
    )(p, p, ln_g.reshape(1, BR_W), ln_b.reshape(1, BR_W), sgu_w, sgu_b.reshape(N_GROUPS, SGU_CHUNK, 1), dy)


def _conv_w8(conv_w):
    return jnp.concatenate([conv_w, jnp.zeros((8 - conv_w.shape[0], conv_w.shape[1]), F32)], axis=0)


def _conv_fwd(name, p, conv_w, rc):
    r = p.shape[0]

    def body(cb_ref, cc_ref, cx_ref, w_ref, y_ref):
        t, lo, hi = _segments((r, GC), rc)
        z = cc_ref[...] * cx_ref[...]
        w = w_ref[...]
        c = _shifted(z, -1, t, lo, hi) * w[0:1] + z * w[1:2] + _shifted(z, 1, t, lo, hi) * w[2:3]
        y_ref[...] = (cb_ref[...] * c).astype(y_ref.dtype)

    nb = OFF_CB // GC
    return pl.pallas_call(
        body, name=name, grid=(N_GROUPS,),
        in_specs=[pl.BlockSpec((r, GC), lambda j: (0, nb + j)),
                  pl.BlockSpec((r, GC), lambda j: (0, nb + N_GROUPS + j)),
                  pl.BlockSpec((r, GC), lambda j: (0, nb + 2 * N_GROUPS + j)),
                  pl.BlockSpec((8, GC), lambda j: (0, j))],
        out_specs=pl.BlockSpec((r, GC), lambda j: (0, j)),
        out_shape=jax.ShapeDtypeStruct((r, BR_W), BF16),
        compiler_params=_cparams("parallel"),
    )(p, p, p, _conv_w8(conv_w))


def _conv_bwd(name, p, conv_w, dy, rc):
    r = p.shape[0]

    def body(cb_ref, cc_ref, cx_ref, w_ref, dy_ref, dcb_ref, dcc_ref, dcx_ref, dw_ref):
        t, lo, hi = _segments((r, GC), rc)
        cc, cx, w, dy = cc_ref[...], cx_ref[...], w_ref[...], dy_ref[...]
        z = cc * cx
        zp, zn = _shifted(z, -1, t, lo, hi), _shifted(z, 1, t, lo, hi)
        dcb_ref[...] = (dy * (zp * w[0:1] + z * w[1:2] + zn * w[2:3])).astype(dcb_ref.dtype)
        dc = dy * cb_ref[...]
        dw_ref[...] = jnp.concatenate(
            [jnp.sum(dc * zp, axis=0, keepdims=True), jnp.sum(dc * z, axis=0, keepdims=True),
             jnp.sum(dc * zn, axis=0, keepdims=True), jnp.zeros((5, GC), F32)], axis=0)
        dz = dc * w[1:2] + _shifted(dc, 1, t, lo, hi) * w[0:1] + _shifted(dc, -1, t, lo, hi) * w[2:3]
        dcc_ref[...] = (dz * cx).astype(dcc_ref.dtype)
        dcx_ref[...] = (dz * cc).astype(dcx_ref.dtype)

    nb = OFF_CB // GC
    return pl.pallas_call(
        body, name=name, grid=(N_GROUPS,),
        in_specs=[pl.BlockSpec((r, GC), lambda j: (0, nb + j)),
                  pl.BlockSpec((r, GC), lambda j: (0, nb + N_GROUPS + j)),
                  pl.BlockSpec((r, GC), lambda j: (0, nb + 2 * N_GROUPS + j)),
                  pl.BlockSpec((8, GC), lambda j: (0, j)),
                  pl.BlockSpec((r, GC), lambda j: (0, j))],
        out_specs=[pl.BlockSpec((r, GC), lambda j: (0, j))] * 3 + [pl.BlockSpec((8, GC), lambda j: (0, j))],
        out_shape=[jax.ShapeDtypeStruct((r, BR_W), BF16)] * 3 + [jax.ShapeDtypeStruct((8, BR_W), F32)],
        compiler_params=_cparams("parallel"),
    )(p, p, p, _conv_w8(conv_w), dy)


def _rope_tables(rc, n):
    rows = n // GRID_W
    row = jnp.repeat(jnp.arange(rows), GRID_W).astype(F32)
    col = jnp.tile(jnp.arange(GRID_W), rows).astype(F32)
    inv = ROPE_THETA ** (-jnp.arange(0, ROPE_AXIS_DIM, 2, dtype=F32) / ROPE_AXIS_DIM)
    ang_r, ang_c = row[:, None] * inv, col[:, None] * inv
    cos = jnp.concatenate([jnp.cos(ang_r), jnp.cos(ang_r), jnp.cos(ang_c), jnp.cos(ang_c)], axis=1)
    sin = jnp.concatenate([-jnp.sin(ang_r), jnp.sin(ang_r), -jnp.sin(ang_c), jnp.sin(ang_c)], axis=1)
    cos = jnp.concatenate([jnp.ones((rc, HEAD_DIM), F32), cos], axis=0)
    sin = jnp.concatenate([jnp.zeros((rc, HEAD_DIM), F32), sin], axis=0)
    return cos, sin


MOD_NAMES = ("sh1", "sc1", "g1", "sh2", "sc2", "g2")


def _local_step(xin, target, mod, comm, sp, rc, alpha):
    def carrying(fn):
        def call(name, *args, **kw):
            rider = comm.rider(name)
            if rider is None:
                return fn(name, *args, **kw)
            res, r_outs = fn(name, *args, rider=rider, **kw)
            comm.deliver(name, r_outs)
            return res
        return call

    mm, rows, attn_fwd, attn_bwd = carrying(_mm), carrying(_rows), carrying(_attn_fwd), carrying(_attn_bwd)
    mm_fused = carrying(_mm_fused)
    r, d = xin.shape
    n_layers = mod.shape[0]
    tm_n, tm_w = 256, 128
    nbc_n, nbc_w = rc // tm_n, rc // tm_w
    cos, sin = _rope_tables(rc, r - rc)
    mp = mod.reshape(n_layers, 2, 6, 1, d)
    mods = [{nm: mp[i, :, j] for j, nm in enumerate(MOD_NAMES)} for i in range(n_layers)]
    f_ln_mod, f_ln_last = _make_f_ln(alpha, True), _make_f_ln(alpha, False)

    def whole(arr, roff=0):
        return (arr, 0, arr.shape[1], roff)

    (hb,) = rows("mod_in", _f_mod, r, tm_n, nbc_n, [whole(xin)], [mods[0]["sc1"], mods[0]["sh1"]],
                 [(r, d, BF16, 0)], [])
    saved = []
    x = xin
    for i in range(n_layers):
        last = i == n_layers - 1
        w, s, m = functools.partial(comm.weight, i), sp[i], mods[i]
        sv = {"x": x, "hb": hb}
        p = mm(f"l{i}_in", hb, w("in_t"), "nt", F32)
        q, k, v = rows(f"l{i}_prep", _f_prep, r, tm_n, nbc_n,
                       [(p, 0, OFF_POOL, 0), whole(cos), whole(sin)], [_typed(s["q_norm_g"]), _typed(s["k_norm_g"])],
                       [(r, Q_W, BF16, 0), (r, KV_W, BF16, 0), (r, KV_W, BF16, 0)], [])
        att, sv["pw"] = attn_fwd(f"l{i}_attn", q, k, v, rc, not last)
        ys = [att,
              _pool_fwd(f"l{i}_pool", p, s["pool_w"], s["pool_scale"], rc),
              _sgu_fwd(f"l{i}_sgu", p, s["sgu_ln_g"], s["sgu_ln_b"], s["sgu_w"], s["sgu_b"]),
              _conv_fwd(f"l{i}_conv", p, s["conv_w"], rc)]
        ts = [mm(f"l{i}_br{kk}", ys[kk], w(f"br{kk}"), "nt", BF16) for kk in range(N_BRANCH)]
        gpre = mm(f"l{i}_gate", hb, w("gate_t"), "nt", BF16)
        (mg,) = rows(f"l{i}_merge", _f_gate, r, tm_w, nbc_w, [whole(gpre)] + [whole(t) for t in ts],
                     [_typed(s["b_gate"])], [(r, d, BF16, 0)], [])
        o = mm(f"l{i}_o", mg, w("o"), "nn", F32)
        x1, h2b = rows(f"l{i}_ln1", f_ln_mod, r, tm_n, nbc_n, [whole(x), whole(o)],
                       [m["g1"], _typed(s["ln1_g"]), _typed(s["ln1_b"]), m["sc2"], m["sh2"]],
                       [(r, d, F32, 0), (r, d, BF16, 0)], [])
        af, bf, f = mm_fused(f"l{i}_ffgu", h2b, [w("ffg_t"), w("ffu_t")],
                             lambda prods, _: (prods[0], prods[1], _f_swiglu(prods[0], prods[1])), [], [BF16, BF16, BF16])
        o2 = mm(f"l{i}_ffd", f, w("ffd"), "nn", F32)
        if last:
            (x2,) = rows(f"l{i}_ln2", f_ln_last, r, tm_n, nbc_n, [whole(x1), whole(o2)],
                         [m["g2"], _typed(s["ln2_g"]), _typed(s["ln2_b"])], [(r, d, F32, 0)], [])
            hb = None
        else:
            nx = mods[i + 1]
            x2, hb = rows(f"l{i}_ln2", f_ln_mod, r, tm_n, nbc_n, [whole(x1), whole(o2)],
                          [m["g2"], _typed(s["ln2_g"]), _typed(s["ln2_b"]), nx["sc1"], nx["sh1"]],
                          [(r, d, F32, 0), (r, d, BF16, 0)], [])
        sv.update(p=p, gpre=gpre, q=q, k=k, v=v, ys=ys, ts=ts, mg=mg, o=o, x1=x1, h2b=h2b, af=af, bf=bf, f=f, o2=o2)
        saved.append(sv)
        x = x2

    lat = jnp.concatenate([jnp.zeros((1, 1, 128), F32), jnp.ones((1, 1, 128), F32)], axis=0)

    def f_loss(xb, tb, msk):
        diff = (xb - tb) * msk[:, 0:1]
        part = jnp.sum(jnp.mean(jnp.square(diff), axis=-1, keepdims=True), axis=0, keepdims=True)
        return diff * (1.0 / d), jnp.broadcast_to(part, (1, 128))

    dx_direct, loss_acc = rows("loss", f_loss, r, tm_n, nbc_n, [whole(x), whole(target, nbc_n)], [lat],
                               [(r, d, F32, 0)], [128])
    loss = 0.5 * loss_acc[1, 0, 0]

    dmods = [dict() for _ in range(n_layers)]
    dsp = [dict() for _ in range(n_layers)]
    dh = None

    def small_done(j):
        ds = dict(dsp[j])
        for nm in ("ln1_g", "ln1_b", "ln2_g", "ln2_b", "b_gate", "q_norm_g", "k_norm_g"):
            ds[nm] = ds[nm][0, 0]
        for nm in ("pool_scale", "sgu_ln_g", "sgu_ln_b"):
            ds[nm] = ds[nm].reshape(-1)
        ds["sgu_b"] = ds["sgu_b"].reshape(N_GROUPS, SGU_CHUNK)
        comm.small_ready(j, jnp.concatenate([dmods[j][nm][:, 0, :] for nm in MOD_NAMES], axis=-1), ds)

    for i in reversed(range(n_layers)):
        last = i == n_layers - 1
        w, s, m, sv = functools.partial(comm.weight, i), sp[i], mods[i], saved[i]
        dm, dw, ds = dmods[i], {}, dsp[i]
        ln2 = [m["g2"], _typed(s["ln2_g"]), _typed(s["ln2_b"])]
        if last:
            res = rows(f"l{i}_ln2_bwd", _vjp_fn(f_ln_last, 2, 1), r, tm_n, nbc_n,
                       [whole(sv["x1"]), whole(sv["o2"]), whole(dx_direct)], ln2,
                       [(r, d, F32, 0), (r, d, BF16, 0)], [d, (d,), (d,)])
            dx1, do2, dm["g2"], dlg, dlb = res
        else:
            nx = mods[i + 1]
            res = rows(f"l{i}_ln2_bwd", _vjp_fn(f_ln_mod, 2, 2), r, tm_n, nbc_n,
                       [whole(sv["x1"]), whole(sv["o2"]), whole(dx_direct), whole(dh)],
                       ln2 + [nx["sc1"], nx["sh1"]],
                       [(r, d, F32, 0), (r, d, BF16, 0)], [d, (d,), (d,), d, d])
            dx1, do2, dm["g2"], dlg, dlb, dmods[i + 1]["sc1"], dmods[i + 1]["sh1"] = res
            small_done(i + 1)
        ds["ln2_g"], ds["ln2_b"] = dlg, dlb
        dab, dbb = mm_fused(f"l{i}_dF", do2, [w("ffd")],
                            lambda prods, tiles: jax.vjp(_f_swiglu, *tiles)[1](prods[0]), [sv["af"], sv["bf"]],
                            [BF16, BF16])
        comm.grads(i, {"ffd": mm(f"l{i}_dWffd", sv["f"], do2, "tn", BF16)})
        comm.grads(i, {"ffg_t": mm(f"l{i}_dWffg", dab, sv["h2b"], "tn", BF16)})
        comm.grads(i, {"ffu_t": mm(f"l{i}_dWffu", dbb, sv["h2b"], "tn", BF16)})
        dh2 = mm(f"l{i}_dh2a", dab, w("ffg_t"), "nn", F32)
        dh2 = mm(f"l{i}_dh2b", dbb, w("ffu_t"), "nn", F32, acc=dh2)
        res = rows(f"l{i}_ln1_bwd", _vjp_fn(f_ln_mod, 2, 2), r, tm_n, nbc_n,
                   [whole(sv["x"]), whole(sv["o"]), whole(dx1), whole(dh2)],
                   [m["g1"], _typed(s["ln1_g"]), _typed(s["ln1_b"]), m["sc2"], m["sh2"]],
                   [(r, d, F32, 0), (r, d, BF16, 0)], [d, (d,), (d,), d, d])
        dx_direct, do, dm["g1"], ds["ln1_g"], ds["ln1_b"], dm["sc2"], dm["sh2"] = res
        dmg = mm(f"l{i}_dMg", do, w("o"), "nt", F32)
        comm.grads(i, {"o": mm(f"l{i}_dWo", sv["mg"], do, "tn", BF16)})
        res = rows(f"l{i}_merge_bwd", _vjp_fn(_f_gate, 5, 1), r, tm_w, nbc_w,
                   [whole(sv["gpre"])] + [whole(t) for t in sv["ts"]] + [whole(dmg)], [_typed(s["b_gate"])],
                   [(r, N_BRANCH * d, BF16, 0)] + [(r, d, BF16, 0)] * N_BRANCH, [(N_BRANCH * d,)])
        dgb, dts, ds["b_gate"] = res[0], res[1:1 + N_BRANCH], res[1 + N_BRANCH]
        comm.grads(i, {"gate_t": mm(f"l{i}_dWgate", dgb, sv["hb"], "tn", BF16)})
        dys = [mm(f"l{i}_dY{kk}", dts[kk], w(f"br{kk}"), "nn", F32) for kk in range(N_BRANCH)]
        for kk in range(N_BRANCH):
            comm.grads(i, {f"br{kk}": mm(f"l{i}_dWbr{kk}", dts[kk], sv["ys"][kk], "tn", BF16)})
        dq, dk, dv = attn_bwd(f"l{i}_attn_bwd", sv["q"], sv["k"], sv["v"], sv["pw"], dys[0], rc, not last)
        res = rows(f"l{i}_prep_bwd", _vjp_fn(_f_prep, 3, 3, keep=(0, 3, 4)), r, tm_n, nbc_n,
                   [(sv["p"], 0, OFF_POOL, 0), whole(cos), whole(sin), whole(dq), whole(dk), whole(dv)],
                   [_typed(s["q_norm_g"]), _typed(s["k_norm_g"])],
                   [(r, OFF_POOL, BF16, 0)], [(HEAD_DIM,), (HEAD_DIM,)])
        dp_qkv, ds["q_norm_g"], ds["k_norm_g"] = res
        dp_pool, ds["pool_w"], ds["pool_scale"] = _pool_bwd(f"l{i}_pool_bwd", sv["p"], s["pool_w"], s["pool_scale"],
                                                            dys[1], rc)
        dp_sgu, ds["sgu_ln_g"], ds["sgu_ln_b"], ds["sgu_w"], ds["sgu_b"] = _sgu_bwd(
            f"l{i}_sgu_bwd", sv["p"], s["sgu_ln_g"], s["sgu_ln_b"], s["sgu_w"], s["sgu_b"], dys[2])
        dp_cb, dp_cc, dp_cx, dcw = _conv_bwd(f"l{i}_conv_bwd", sv["p"], s["conv_w"], dys[3], rc)
        ds["conv_w"] = dcw[0:3]
        dpb = jnp.concatenate([dp_qkv, dp_pool, dp_sgu, dp_cb, dp_cc, dp_cx], axis=1)
        comm.grads(i, {"in_t": mm(f"l{i}_dWin", dpb, sv["hb"], "tn", BF16)})
        dh = mm(f"l{i}_dhb_a", dpb, w("in_t"), "nn", F32)
        dh = mm(f"l{i}_dhb_b", dgb, w("gate_t"), "nn", F32, acc=dh)

    def f_mod_bwd(xb, ddir, dhb, sc, sh):
        _, vjp = jax.vjp(_f_mod, xb, sc, sh)
        dxb, dsc, dsh = vjp(dhb)
        return dxb + ddir, dsc, dsh

    grad_x, dmods[0]["sc1"], dmods[0]["sh1"] = rows(
        "mod_in_bwd", f_mod_bwd, r, tm_n, nbc_n, [whole(xin), whole(dx_direct), whole(dh)],
        [mods[0]["sc1"], mods[0]["sh1"]], [(r - rc, d, F32, nbc_n)], [d, d])
    small_done(0)
    return loss, grad_x


def _direct_rider(src):
    def peers():
        mx, my, mc = [lax.axis_index(a) for a in MESH_AXES]
        out = []
        for kk in range(1, N_DEV):
            px = 1 - mx if kk & 4 else mx
            py = 1 - my if kk & 2 else my
            pc = 1 - mc if kk & 1 else mc
            out.append(((px, py, pc), 4 * px + 2 * py + pc))
        return 4 * mx + 2 * my + mc, out

    def start(ins, outs, sems):
        send, recv, loc = sems
        me, others = peers()
        pltpu.make_async_copy(ins[0], outs[0].at[me], loc.at[0]).start()
        for j, (peer, _) in enumerate(others):
            _rcopy(ins[0], outs[0].at[me], send, recv, j, peer).start()

    def finish(ins, outs, sems):
        send, recv, loc = sems
        me, others = peers()
        for j, (peer, peer_l) in enumerate(others):
            cp = _rcopy(ins[0], outs[0].at[peer_l], send, recv, j, peer)
            cp.wait_recv()
            cp.wait_send()
        pltpu.make_async_copy(ins[0], outs[0].at[me], loc.at[0]).wait()

    return _Rider([src], [jax.ShapeDtypeStruct((N_DEV,) + src.shape, src.dtype)],
                  _sem_scratch(N_DEV - 1, N_DEV - 1, 1), start, finish)


def _mesh_place():
    mx, my, mc = [lax.axis_index(a) for a in MESH_AXES]
    chips = [(1 - mx, my), (mx, 1 - my), (1 - mx, 1 - my)]

    def lid(px, py, pc):
        return 4 * px + 2 * py + pc

    return (mx, my, mc), (mx, my, 1 - mc), chips, lid


def _rcopy(src, dst, send_sems, recv_sems, k, to):
    return pltpu.make_async_remote_copy(src_ref=src, dst_ref=dst, send_sem=send_sems.at[k], recv_sem=recv_sems.at[k],
                                        device_id=to, device_id_type=pl.DeviceIdType.MESH)


def _sem_scratch(*sizes):
    return [pltpu.SemaphoreType.DMA((s,)) for s in sizes]


def _gather_rider(src, rows1, rows2, buf=None):
    def place():
        (mx, my, mc), sib, chips, lid = _mesh_place()
        xn, yn, dg = [(*chip, mc) for chip in chips]
        return lid(mx, my, mc), sib, xn, yn, dg, lid

    def halves(rows):
        r0, r1 = rows
        mid = r0 + (r1 - r0) // 32 * 16
        return pl.ds(r0, mid - r0), pl.ds(mid, r1 - mid), pl.ds(r0, r1 - r0)

    n_src = 1 if rows1 is not None else 0

    def start(ins, outs, sems):
        send, recv, loc = sems
        me, sib, xn, yn, dg, lid = place()
        if rows1 is not None:
            win = pl.ds(rows1[0], rows1[1] - rows1[0])
            mine, dst = ins[0].at[win], outs[0].at[me, win]
            pltpu.make_async_copy(mine, dst, loc.at[0]).start()
            for t, to in enumerate((sib, xn, yn)):
                _rcopy(mine, dst, send, recv, t, to).start()
        if rows2 is not None:
            top, bot, win = halves(rows2)
            xb, yb = outs[0].at[lid(*xn)], outs[0].at[lid(*yn)]
            _rcopy(xb.at[top], xb.at[top], send, recv, 3, yn).start()
            _rcopy(yb.at[bot], yb.at[bot], send, recv, 4, xn).start()
            _rcopy(xb.at[win], xb.at[win], send, recv, 5, sib).start()
            _rcopy(yb.at[win], yb.at[win], send, recv, 6, sib).start()

    def finish(ins, outs, sems):
        send, recv, loc = sems
        me, sib, xn, yn, dg, lid = place()
        if rows2 is not None:
            top, bot, win = halves(rows2)
            db = outs[0].at[lid(*dg)]
            _rcopy(db.at[top], db.at[top], send, recv, 3, yn).wait_recv()
            _rcopy(db.at[bot], db.at[bot], send, recv, 4, xn).wait_recv()
            _rcopy(db.at[win], db.at[win], send, recv, 7, sib).start()
            for t, dev in ((5, xn), (6, yn), (7, dg)):
                blk = outs[0].at[lid(dev[0], dev[1], 1 - dev[2]), win]
                _rcopy(blk, blk, send, recv, t, sib).wait_recv()
            for t, part in ((3, top), (4, bot), (5, win), (6, win), (7, win)):
                _rcopy(db.at[part], db.at[part], send, recv, t, sib).wait_send()
        if rows1 is not None:
            win = pl.ds(rows1[0], rows1[1] - rows1[0])
            mine, dst = ins[0].at[win], outs[0].at[me, win]
            for t, dev in enumerate((sib, xn, yn)):
                cp = _rcopy(mine, outs[0].at[lid(*dev), win], send, recv, t, dev)
                cp.wait_recv()
                cp.wait_send()
            pltpu.make_async_copy(mine, dst, loc.at[0]).wait()

    out_shape = jax.ShapeDtypeStruct((N_DEV,) + src.shape, src.dtype)
    inputs = ([src] if n_src else []) + ([buf] if buf is not None else [])
    return _Rider(inputs, [out_shape], _sem_scratch(8, 8, 1), start, finish,
                  aliases={n_src: 0} if buf is not None else {})


def _sibling_rider(part):
    def start(ins, outs, sems):
        send, recv = sems
        (mx, my, mc), sib, chips, lid = _mesh_place()
        for t, slab in enumerate([lid(*sib)] + [lid(*chip, 1 - mc) for chip in chips]):
            _rcopy(ins[0].at[slab], outs[0].at[t], send, recv, t, sib).start()

    def finish(ins, outs, sems):
        send, recv = sems
        _, sib, _, _ = _mesh_place()
        for t in range(4):
            cp = _rcopy(ins[0].at[0], outs[0].at[t], send, recv, t, sib)
            cp.wait_recv()
            cp.wait_send()

    return _Rider([part], [jax.ShapeDtypeStruct((4,) + part.shape[1:], part.dtype)], _sem_scratch(4, 4), start, finish)


def _chips_rider(pair, rows, buf=None):
    r0, r1 = rows
    win = pl.ds(r0, r1 - r0)

    def start(ins, outs, sems):
        send, recv = sems
        (mx, my, mc), sib, chips, lid = _mesh_place()
        for j, chip in enumerate(chips):
            _rcopy(ins[0].at[j, win], outs[0].at[j, win], send, recv, j, (*chip, mc)).start()

    def finish(ins, outs, sems):
        send, recv = sems
        (mx, my, mc), sib, chips, lid = _mesh_place()
        for j, chip in enumerate(chips):
            cp = _rcopy(ins[0].at[j, win], outs[0].at[j, win], send, recv, j, (*chip, mc))
            cp.wait_recv()
            cp.wait_send()

    out_shape = jax.ShapeDtypeStruct(pair.shape, pair.dtype)
    if buf is None:
        return _Rider([pair], [out_shape], _sem_scratch(3, 3), start, finish)
    return _Rider([pair, buf], [out_shape], _sem_scratch(3, 3), start, finish, aliases={1: 0})


def _run_rider(name, rider):
    n_in, n_out = len(rider.inputs), len(rider.out_shapes)

    def body(*refs):
        ins, outs, sems = refs[:n_in], refs[n_in:n_in + n_out], refs[n_in + n_out:]
        rider.start(ins, outs, sems)
        rider.finish(ins, outs, sems)

    any_spec = pl.BlockSpec(memory_space=pl.ANY)
    res = pl.pallas_call(body, name=name, in_specs=[any_spec] * n_in, out_specs=[any_spec] * n_out,
                         out_shape=rider.out_shapes, scratch_shapes=rider.scratch,
                         input_output_aliases=rider.aliases)(*rider.inputs)
    return list(res)


def _slab_ids():
    (mx, my, mc), _, chips, lid = _mesh_place()
    return jnp.stack([lid(*chip, mc) for chip in chips] + [lid(mx, my, mc)]).astype(jnp.int32)


def _pair_sum(name, part, rsib, ids):
    _, n, k = part.shape
    tr = _row_tile(n, 512, 16)

    def body(ids_ref, p_ref, r_ref, o_ref):
        o_ref[...] = (p_ref[...].astype(F32) + r_ref[...].astype(F32)).astype(o_ref.dtype)

    grid_spec = pltpu.PrefetchScalarGridSpec(
        num_scalar_prefetch=1, grid=(3, n // tr),
        in_specs=[pl.BlockSpec((None, tr, k), lambda j, i, ids: (ids[j], i, 0)),
                  pl.BlockSpec((None, tr, k), lambda j, i, ids: (1 + j, i, 0))],
        out_specs=pl.BlockSpec((None, tr, k), lambda j, i, ids: (j, i, 0)))
    return pl.pallas_call(body, name=name, grid_spec=grid_spec, out_shape=jax.ShapeDtypeStruct((3, n, k), part.dtype),
                          compiler_params=_cparams("parallel", "parallel"))(ids, part, rsib)


def _sum5(name, part, rsib, rici, ids, layer, stacked, rider=None):
    _, n, k = part.shape
    tr = _row_tile(n, 512, 16)
    first = isinstance(stacked, int)

    def body(ids_ref, p_ref, r_ref, c_ref, *rest):
        acc = p_ref[...].astype(F32) + r_ref[...].astype(F32)
        for j in range(3):
            acc = acc + c_ref[j].astype(F32)
        rest[-1][...] = acc

    in_specs = [pl.BlockSpec((None, tr, k), lambda i, ids: (ids[3], i, 0)),
                pl.BlockSpec((None, tr, k), lambda i, ids: (0, i, 0)),
                pl.BlockSpec((3, tr, k), lambda i, ids: (0, i, 0))] + ([] if first else [pl.BlockSpec(memory_space=pl.ANY)])
    n_layers = stacked if first else stacked.shape[0]
    outs, r_outs = _pcall(name, body, (n // tr,), in_specs, [pl.BlockSpec((None, tr, k), lambda i, ids: (layer, i, 0))],
                          [jax.ShapeDtypeStruct((n_layers, n, k), F32)],
                          (part, rsib, rici) + (() if first else (stacked,)), ("parallel",), rider=rider, prefetch=ids,
                          aliases={} if first else {3: 0})
    return outs[0] if rider is None else (outs[0], r_outs)


W_KEYS = ("in_t", "br0", "br1", "br2", "br3", "gate_t", "o", "ffg_t", "ffu_t", "ffd")
SUMS_TRANSPOSED_LATER = ("gate_t", "br0", "br1", "br2", "br3")


CARRIER_US = {"mod_in": 12, "in": 55, "gate": 95, "prep": 19, "attn": 125,"br0": 15, "merge": 50, "o": 25, "ln1": 27,
              "ffgu": 135, "ffd": 73, "ln2": 27, "loss": 20, "ln2_bwd": 44, "dF": 80,
              "dWffd": 64, "dh2a": 75, "dh2b": 75, "dWffg": 64, "dWffu": 64, "ln1_bwd": 44,
              "dMg": 25, "dWo": 25, "merge_bwd": 80, "dY0": 14, "dWbr0": 15, "attn_bwd": 140,"prep_bwd": 28,
              "dWin": 54, "dWgate": 95, "dhb_a": 64, "dhb_b": 115}
ICI_US_PER_MIB = 45.0
GATHER_US_PER_MIB = 30.0
RELAY_US_PER_MIB = 15.0
D2D_US_PER_MIB = 6.8
MIN_CHUNK_US = 10.0
CARRIER_FILL = 1.15


class _Comm:
    def __init__(self, wsrc):
        self.wsrc = wsrc
        self.n_layers = len(wsrc)
        self.queue = []
        self.riding = {}
        self.n_alone = 0
        self.buf, self.left = {}, {}
        self.part, self.rsib, self.pair = {}, {}, {}
        self.ids = _slab_ids()
        for i in range(self.n_layers):
            for k in W_KEYS:
                self._push_chunks("gather", ("w", i, k), wsrc[i][k].shape, wsrc[i][k].dtype)

    def _push_chunks(self, kind, item, shape, dtype):
        n, k = shape[-2], shape[-1]
        mib = n * k * jnp.dtype(dtype).itemsize / 2 ** 20
        pieces = max(1, int(mib * ICI_US_PER_MIB // MIN_CHUNK_US))
        while n % (16 * pieces):
            pieces -= 1
        step = n // pieces
        self.left[item] = n
        us = mib * (GATHER_US_PER_MIB if kind == "gather" else ICI_US_PER_MIB) / pieces
        for c in range(pieces):
            self.queue.append(dict(kind=kind, item=item, rows=(c * step, (c + 1) * step), rows2=None, us=us))

    @staticmethod
    def _merge(units, u):
        def joined(a, b):
            if a is None or b is None:
                return True, a or b
            return a[1] == b[0], (a[0], b[1])

        for v in units:
            if v["item"] == u["item"] and v["kind"] == u["kind"] and u["kind"] != "sibling":
                ok1, rows = joined(v["rows"], u["rows"])
                ok2, rows2 = joined(v["rows2"], u["rows2"])
                if ok1 and ok2:
                    v.update(rows=rows, rows2=rows2, us=v["us"] + u["us"])
                    return True
        return False

    def _unit_rider(self, u):
        item = u["item"]
        if u["kind"] == "gather":
            src = self.wsrc[item[1]][item[2]] if item[0] == "w" else self.part[item]
            return _gather_rider(src, u["rows"], u["rows2"], self.buf.get(item))
        if u["kind"] == "sibling":
            return _sibling_rider(self.part[item])
        return _chips_rider(self.pair[item], u["rows"], self.buf.get(item))

    def _done(self, u, out):
        item = u["item"]
        if u["kind"] == "sibling":
            self.rsib[item] = out
            self.pair[item] = _pair_sum(f"pair_l{item[1]}_{item[2]}", self.part[item], out, self.ids)
            self._push_chunks("chips", item, self.pair[item].shape, self.pair[item].dtype)
            return
        self.buf[item] = out
        if u["kind"] == "gather":
            if u["rows"] is not None:
                rows = u["rows"]
                mib = (rows[1] - rows[0]) * out.shape[-1] * out.dtype.itemsize / 2 ** 20
                self.queue.insert(0, dict(kind="gather", item=item, rows=None, rows2=rows, us=mib * RELAY_US_PER_MIB))
            if u["rows2"] is not None:
                self.left[item] -= u["rows2"][1] - u["rows2"][0]

    def _send(self, name, units, call):
        outs = call(_compose([self._unit_rider(u) for u in units]))
        for u, o in zip(units, outs):
            self._done(u, o)

    def exchange(self, name, src, budget_us):
        units = self._take(budget_us)
        outs = _run_rider(name, _compose([_direct_rider(src)] + [self._unit_rider(u) for u in units]))
        for u, o in zip(units, outs[1:]):
            self._done(u, o)
        return outs[0]

    def rider(self, name, budget_us=None):
        budget = CARRIER_US.get(name.split("_", 1)[1] if name[0] == "l" and name[1].isdigit() else name, 0) \
            if budget_us is None else budget_us
        units = self._take(budget)
        if not units:
            return None
        self.riding[name] = units
        return _compose([self._unit_rider(u) for u in units])

    def _take(self, budget):
        units, used = [], 0.0
        while self.queue and used + self.queue[0]["us"] <= CARRIER_FILL * budget:
            u = self.queue[0]
            if not self._merge(units, u):
                if any(v["item"] == u["item"] for v in units):
                    break
                units.append(dict(u))
            used += u["us"]
            del self.queue[0]
        return units

    def deliver(self, name, outs):
        for u, o in zip(self.riding.pop(name), outs):
            self._done(u, o)

    def _flush(self, item, kinds):
        hits = [p for p, u in enumerate(self.queue) if u["item"] == item and u["kind"] in kinds]
        if not hits:
            return
        prefix = self.queue[:hits[-1] + 1]
        del self.queue[:hits[-1] + 1]
        units = []
        for u in prefix:
            if not self._merge(units, u):
                units.append(dict(u))
        tag = "_".join(str(t) for t in item) + "_" + kinds[0]
        batches = [[]]
        for u in units:
            if any(v["item"] == u["item"] for v in batches[-1]):
                batches.append([])
            batches[-1].append(u)
        for batch in batches:
            self.n_alone += 1
            self._send(None, batch, functools.partial(_run_rider, f"alone{self.n_alone}_{tag}"))

    def begin(self):
        self._flush(("w", 0, "in_t"), ("gather",))

    def _finish_gather(self, item):
        while self.left[item] > 0:
            assert any(u["item"] == item for u in self.queue), item
            self._flush(item, ("gather",))
        return self.buf[item]

    def weight(self, i, k):
        o = self._finish_gather(("w", i, k))
        return o.reshape(-1, o.shape[-1])

    def grads(self, i, group):
        for k, g in group.items():
            item = ("g", i, k)
            self.part[item] = g.reshape(N_DEV, g.shape[0] // N_DEV, g.shape[1])
            us = g.size // N_DEV * g.dtype.itemsize / 2 ** 20 * D2D_US_PER_MIB
            self.queue.insert(0, dict(kind="sibling", item=item, rows=None, us=us))

    def total(self, k):
        out = self.n_layers
        for i in range(self.n_layers):
            item = ("g", i, k)
            self._flush(item, ("sibling",))
            self._flush(item, ("chips",))
            name = f"sum_l{i}_{k}"
            rider = self.rider(name, budget_us=self.part[item][0].size / 1.06e5) if k in SUMS_TRANSPOSED_LATER else None
            out = _sum5(name, self.part[item], self.rsib[item], self.buf[item], self.ids, i, out, rider=rider)
            if rider is not None:
                out, r_outs = out
                self.deliver(name, r_outs)
        return out

    def small_ready(self, i, dmod, ds):
        parts = [dmod[0], dmod[1]] + [ds[nm] for nm in LAYER_SMALL + ("conv_w",)]
        self.small_shapes = [p.shape for p in parts]
        self.gather_small(f"lat{i}", _pack([dmod[1]]))
        self.gather_small(f"small{i}", _pack(parts))

    def gather_small(self, name, arr):
        item = ("s", name)
        self.part[item] = arr
        waiting, self.queue = self.queue, []
        self._push_chunks("gather", item, arr.shape, arr.dtype)
        self.queue += waiting

    def gathered(self, name):
        return self._finish_gather(("s", name))


def _row_tile(n, pref, mult):
    best = None
    t = mult
    while t <= min(n, pref):
        if n % t == 0:
            best = t
        t += mult
    return best if best is not None else n


def _sum8(name, slabs):
    _, n, k = slabs.shape
    tr = _row_tile(n, 128, 16)

    def body(s_ref, o_ref):
        acc = s_ref[0].astype(F32)
        for j in range(1, N_DEV):
            acc = acc + s_ref[j].astype(F32)
        o_ref[...] = acc

    return pl.pallas_call(
        body, name=name, grid=(n // tr,),
        in_specs=[pl.BlockSpec((N_DEV, tr, k), lambda i: (0, i, 0))],
        out_specs=pl.BlockSpec((tr, k), lambda i: (i, 0)),
        out_shape=jax.ShapeDtypeStruct((n, k), F32),
        compiler_params=_cparams("parallel"),
    )(slabs)


def _adamw(name, w, g, m, v, rider=None):
    n, k = w.shape[-2:]
    tr = _row_tile(n, 256, 8)

    def body(w_ref, g_ref, m_ref, v_ref, d_ref, m2_ref, v2_ref):
        gv = g_ref[...]
        m2 = ADAM_B1 * m_ref[...] + (1.0 - ADAM_B1) * gv
        v2 = ADAM_B2 * v_ref[...] + (1.0 - ADAM_B2) * jnp.square(gv)
        m_hat = m2 / (1.0 - ADAM_B1 ** ADAM_STEP)
        v_hat = v2 / (1.0 - ADAM_B2 ** ADAM_STEP)
        d_ref[...] = -ADAM_LR * (m_hat / (jnp.sqrt(v_hat) + ADAM_EPS) + ADAM_WD * w_ref[...])
        m2_ref[...] = m2
        v2_ref[...] = v2

    if w.ndim == 2:
        grid, spec = (n // tr,), pl.BlockSpec((tr, k), lambda i: (i, 0))
    else:
        grid, spec = (w.shape[0], n // tr), pl.BlockSpec((None, tr, k), lambda l, i: (l, i, 0))
    outs, r_outs = _pcall(name, body, grid, [spec] * 4, [spec] * 3, [jax.ShapeDtypeStruct(w.shape, F32)] * 3,
                          (w, g, m, v), ("parallel",) * len(grid), rider=rider)
    return outs if rider is None else (outs, r_outs)


def _pack(arrs):
    flat = jnp.concatenate([a.reshape(-1).astype(F32) for a in arrs])
    pad = (-flat.shape[0]) % 2048
    if pad:
        flat = jnp.concatenate([flat, jnp.zeros((pad,), F32)])
    return flat.reshape(-1, 128)


def _unpack(packed, shapes):
    flat = packed.reshape(-1)
    out, off = [], 0
    for shp in shapes:
        size = math.prod(shp)
        out.append(flat[off:off + size].reshape(shp))
        off += size
    return out


WEIGHT_NAMES = ("c_ctx", "w_ada", "b_ada", "w_in", "q_norm_g", "k_norm_g", "pool_w", "pool_scale", "sgu_ln_g",
                "sgu_ln_b", "sgu_w", "sgu_b", "conv_w", "w_br_attn", "w_br_pool", "w_br_sgu", "w_br_conv", "w_gate",
                "b_gate", "w_o", "ln1_g", "ln1_b", "w_ff_gate", "w_ff_up", "w_ff_down", "ln2_g", "ln2_b")
COL_SHARDED = {"w_in": "in_t", "w_gate": "gate_t", "w_ff_gate": "ffg_t", "w_ff_up": "ffu_t",
               "w_br_attn": "br0", "w_br_pool": "br1", "w_br_sgu": "br2", "w_br_conv": "br3"}
ROW_SHARDED = {"w_o": "o", "w_ff_down": "ffd"}
LAYER_SMALL = ("q_norm_g", "k_norm_g", "pool_w", "pool_scale", "sgu_ln_g", "sgu_ln_b", "sgu_w", "sgu_b", "b_gate",
               "ln1_g", "ln1_b", "ln2_g", "ln2_b")
SMALL_ORDER = ("c_ctx", "b_ada") + LAYER_SMALL + ("conv_w",)


def _train_step(a):
    n_layers, d = a["w_in"].shape[0], a["x"].shape[-1]
    rc = a["ctx"].shape[1]
    alpha = (2 * n_layers) ** 0.25
    mx, my, mc = [lax.axis_index(ax) for ax in MESH_AXES]
    me = 4 * mx + 2 * my + mc
    ada_w = a["w_ada"].shape[-1]
    cw_loc = a["conv_w"].shape[-1]

    comm = _Comm([{**{key: jnp.swapaxes(a[nm], 1, 2)[i].astype(BF16) for nm, key in COL_SHARDED.items()},
                   **{key: a[nm][i].astype(BF16) for nm, key in ROW_SHARDED.items()}} for i in range(n_layers)])

    def carried(name, *args, budget_us, **kw):
        rider = comm.rider(name, budget_us=budget_us)
        if rider is None:
            return _mm(name, *args, **kw)
        res, r_outs = _mm(name, *args, rider=rider, **kw)
        comm.deliver(name, r_outs)
        return res

    n_c, n_cw = d, n_layers * 3 * cw_loc
    got = comm.exchange("gather_cond", _pack([a["c"], a["conv_w"]]), budget_us=12).reshape(N_DEV, -1)
    c_all = got[:, :n_c]
    conv_w = got[:, n_c:n_c + n_cw].reshape(N_DEV, n_layers, 3, cw_loc).transpose(1, 2, 0, 3).reshape(n_layers, 3, -1)
    cond = jnp.concatenate([c_all, a["c_ctx"][None], jnp.zeros((16 - N_DEV - 1, d), F32)], axis=0)
    sil, sil_vjp = jax.vjp(jax.nn.silu, cond)
    sil = sil.astype(BF16)

    mod_cols = jnp.concatenate([carried(f"ada{i}", sil, a["w_ada"][i], "nn", F32, budget_us=20)
                                for i in range(n_layers)], axis=0)
    got = comm.exchange("gather_mod", mod_cols, budget_us=22)
    mod_all = got.reshape(N_DEV, n_layers, 16, ada_w).transpose(1, 2, 0, 3).reshape(n_layers, 16, -1)
    mod_all = mod_all + a["b_ada"][:, None, :]
    mod = jnp.stack([mod_all[:, N_DEV], lax.dynamic_index_in_dim(mod_all, me, axis=1, keepdims=False)], axis=1)
    comm.begin()
    sp = [{nm: a[nm][i] for nm in LAYER_SMALL} for i in range(n_layers)]
    for i in range(n_layers):
        sp[i]["conv_w"] = conv_w[i]

    xin = jnp.concatenate([a["ctx"][0], a["x"][0]], axis=0)
    loss_l, grad_x = _local_step(xin, a["loss_target"][0], mod, comm, sp, rc, alpha)
    loss = lax.psum(loss_l, MESH_AXES)
    grads = {}

    def transposed_home(nm):
        return nm in COL_SHARDED and a[nm].shape[-1] % 128 != 0

    delta, new_m, new_v = {}, {}, {}

    def adamw(nm):
        name = f"adamw_{nm}"
        there = transposed_home(nm)
        view = (lambda t: jnp.swapaxes(t, 1, 2)) if there else (lambda t: t)
        if nm in COL_SHARDED:
            g = comm.total(COL_SHARDED[nm])
            grads[nm] = jnp.swapaxes(g, 1, 2)
            g = g if there else grads[nm]
        elif nm in ROW_SHARDED:
            g = grads[nm] = comm.total(ROW_SHARDED[nm])
        else:
            g = grads[nm]
        res = _adamw(name, view(a[nm]), g, view(a["m_" + nm]), view(a["v_" + nm]))
        delta[nm], new_m[nm], new_v[nm] = [view(t) for t in res]

    for nm in ("w_ff_down", "w_ff_gate", "w_ff_up", "w_o", "w_br_attn", "w_br_pool", "w_br_sgu", "w_br_conv"):
        adamw(nm)

    tots = [_unpack(_sum8(f"sum_small{i}", comm.gathered(f"small{i}")), comm.small_shapes) for i in range(n_layers)]
    dmod_c, dmod_lat_sum = jnp.stack([t[0] for t in tots]), jnp.stack([t[1] for t in tots])
    for j, nm in enumerate(LAYER_SMALL + ("conv_w",)):
        grads[nm] = jnp.stack([t[2 + j] for t in tots])
    grads["conv_w"] = lax.dynamic_slice_in_dim(grads["conv_w"], me * cw_loc, cw_loc, axis=2)
    grads["b_ada"] = dmod_c + dmod_lat_sum
    dmod_lat_all = jnp.stack([comm.gathered(f"lat{i}").reshape(N_DEV, -1)[:, :6 * d] for i in range(n_layers)])
    dm_rows = jnp.concatenate([dmod_lat_all, dmod_c[:, None, :],
                               jnp.zeros((n_layers, 16 - N_DEV - 1, 6 * d), F32)], axis=1)
    dm_cols = lax.dynamic_slice_in_dim(dm_rows, me * ada_w, ada_w, axis=2).astype(BF16)
    grads["w_ada"] = jnp.stack([carried(f"dWada{i}", sil, dm_cols[i], "tn", F32, budget_us=20)
                                for i in range(n_layers)])
    dsil = None
    for i in range(n_layers):
        dsil = carried(f"dsil{i}", dm_cols[i], a["w_ada"][i], "nt", F32, acc=dsil, budget_us=10)
    dsil = _sum8("sum_dsil", comm.exchange("gather_dsil", dsil, budget_us=17))
    grads["c_ctx"] = sil_vjp(dsil)[0][N_DEV]

    for nm in ("w_ada", "w_in", "w_gate"):
        adamw(nm)
    shapes = [a[nm].shape for nm in SMALL_ORDER]
    res = _adamw("adamw_small", _pack([a[nm] for nm in SMALL_ORDER]), _pack([grads[nm] for nm in SMALL_ORDER]),
                 _pack([a["m_" + nm] for nm in SMALL_ORDER]), _pack([a["v_" + nm] for nm in SMALL_ORDER]))
    for tree, packed in zip((delta, new_m, new_v), res):
        for nm, t in zip(SMALL_ORDER, _unpack(packed, shapes)):
            tree[nm] = t
    return (loss, grad_x[None], *[grads[nm] for nm in WEIGHT_NAMES], *[delta[nm] for nm in WEIGHT_NAMES],
            *[new_m[nm] for nm in WEIGHT_NAMES], *[new_v[nm] for nm in WEIGHT_NAMES])


def kernel(x, c, ctx, c_ctx, w_ada, b_ada, w_in, q_norm_g, k_norm_g, pool_w, pool_scale, sgu_ln_g, sgu_ln_b, sgu_w, sgu_b, conv_w, w_br_attn, w_br_pool, w_br_sgu, w_br_conv, w_gate, b_gate, w_o, ln1_g, ln1_b, w_ff_gate, w_ff_up, w_ff_down, ln2_g, ln2_b, loss_target, m_c_ctx, m_w_ada, m_b_ada, m_w_in, m_q_norm_g, m_k_norm_g, m_pool_w, m_pool_scale, m_sgu_ln_g, m_sgu_ln_b, m_sgu_w, m_sgu_b, m_conv_w, m_w_br_attn, m_w_br_pool, m_w_br_sgu, m_w_br_conv, m_w_gate, m_b_gate, m_w_o, m_ln1_g, m_ln1_b, m_w_ff_gate, m_w_ff_up, m_w_ff_down, m_ln2_g, m_ln2_b, v_c_ctx, v_w_ada, v_b_ada, v_w_in, v_q_norm_g, v_k_norm_g, v_pool_w, v_pool_scale, v_sgu_ln_g, v_sgu_ln_b, v_sgu_w, v_sgu_b, v_conv_w, v_w_br_attn, v_w_br_pool, v_w_br_sgu, v_w_br_conv, v_w_gate, v_b_gate, v_w_o, v_ln1_g, v_ln1_b, v_w_ff_gate, v_w_ff_up, v_w_ff_down, v_ln2_g, v_ln2_b):
    names = list(WEIGHT_NAMES)
    args = dict(zip(
        ["x", "c", "ctx"] + names + ["loss_target"] + ["m_" + n for n in names] + ["v_" + n for n in names],
        (x, c, ctx, c_ctx, w_ada, b_ada, w_in, q_norm_g, k_norm_g, pool_w, pool_scale, sgu_ln_g, sgu_ln_b, sgu_w, sgu_b, conv_w, w_br_attn, w_br_pool, w_br_sgu, w_br_conv, w_gate, b_gate, w_o, ln1_g, ln1_b, w_ff_gate, w_ff_up, w_ff_down, ln2_g, ln2_b, loss_target, m_c_ctx, m_w_ada, m_b_ada, m_w_in, m_q_norm_g, m_k_norm_g, m_pool_w, m_pool_scale, m_sgu_ln_g, m_sgu_ln_b, m_sgu_w, m_sgu_b, m_conv_w, m_w_br_attn, m_w_br_pool, m_w_br_sgu, m_w_br_conv, m_w_gate, m_b_gate, m_w_o, m_ln1_g, m_ln1_b, m_w_ff_gate, m_w_ff_up, m_w_ff_down, m_ln2_g, m_ln2_b, v_c_ctx, v_w_ada, v_b_ada, v_w_in, v_q_norm_g, v_k_norm_g, v_pool_w, v_pool_scale, v_sgu_ln_g, v_sgu_ln_b, v_sgu_w, v_sgu_b, v_conv_w, v_w_br_attn, v_w_br_pool, v_w_br_sgu, v_w_br_conv, v_w_gate, v_b_gate, v_w_o, v_ln1_g, v_ln1_b, v_w_ff_gate, v_w_ff_up, v_w_ff_down, v_ln2_g, v_ln2_b)))
    return _train_step(args)
```

```python
import functools
import math

import jax
import jax.numpy as jnp
from jax import lax
from jax.experimental import pallas as pl
from jax.experimental.pallas import tpu as pltpu

F32 = jnp.float32
BF16 = jnp.bfloat16

N_DEV = 8
MESH_AXES = ("x", "y", "c")
V7X_VMEM_LIMIT_BYTES = 56 * 1024 * 1024

GRID_W = 64
HEAD_DIM = 128
N_HEADS = 8
N_KV_HEADS = 2
KV_GROUP = N_HEADS // N_KV_HEADS
Q_W = N_HEADS * HEAD_DIM
KV_W = N_KV_HEADS * HEAD_DIM
ROPE_THETA = 10000.0
ROPE_AXIS_DIM = HEAD_DIM // 2
POOL_WINDOWS = (2, 4, 8, 16)
GC = 128
N_GROUPS = 4
BR_W = N_GROUPS * GC
SGU_CHUNK = 128
N_BRANCH = 4
LN_EPS = 1e-5
RMS_EPS = 1e-6
OFF_K = Q_W
OFF_V = OFF_K + KV_W
OFF_POOL = OFF_V + KV_W
OFF_U = OFF_POOL + BR_W
OFF_VG = OFF_U + BR_W
OFF_CB = OFF_VG + BR_W
OFF_CC = OFF_CB + BR_W
OFF_CX = OFF_CC + BR_W
IN_W = OFF_CX + BR_W
ATT_SCALE = HEAD_DIM ** -0.5

ADAM_LR = 0.001
ADAM_B1 = 0.9
ADAM_B2 = 0.999
ADAM_EPS = 1e-08
ADAM_WD = 0.01
ADAM_STEP = 10

_NT = (((1,), (1,)), ((), ()))
_NN = (((1,), (0,)), ((), ()))
_TN = (((0,), (0,)), ((), ()))
_DIMS = {"nt": _NT, "nn": _NN, "tn": _TN}


def _cparams(*sem):
    return pltpu.CompilerParams(dimension_semantics=sem, vmem_limit_bytes=V7X_VMEM_LIMIT_BYTES)


def _tile(dim, pref):
    best = None
    t = 128
    while t <= min(dim, pref):
        if dim % t == 0:
            best = t
        t += 128
    return best if best is not None else dim


def _dot(a, b, dims):
    return lax.dot_general(a.astype(BF16), b.astype(BF16), dims, preferred_element_type=F32)


class _Rider:
    def __init__(self, inputs, out_shapes, scratch, start, finish, aliases=None):
        self.inputs, self.out_shapes, self.scratch = list(inputs), list(out_shapes), list(scratch)
        self.start, self.finish = start, finish
        self.aliases = dict(aliases or {})


def _compose(riders):
    inputs, outs, scratch, aliases, spans = [], [], [], {}, []
    for rd in riders:
        i0, o0, s0 = len(inputs), len(outs), len(scratch)
        aliases.update({i0 + p: o0 + q for p, q in rd.aliases.items()})
        inputs += rd.inputs
        outs += rd.out_shapes
        scratch += rd.scratch
        spans.append((slice(i0, len(inputs)), slice(o0, len(outs)), slice(s0, len(scratch))))

    def start(ins, os, sems):
        for rd, (si, so, ss) in zip(riders, spans):
            rd.start(ins[si], os[so], sems[ss])

    def finish(ins, os, sems):
        for rd, (si, so, ss) in zip(riders, spans):
            rd.finish(ins[si], os[so], sems[ss])

    return _Rider(inputs, outs, scratch, start, finish, aliases)


def _pcall(name, body, grid, in_specs, out_specs, out_shape, args, sem, scratch=(), rider=None, prefetch=None,
           aliases=None):
    in_specs, out_specs, out_shape, scratch = list(in_specs), list(out_specs), list(out_shape), list(scratch)
    n_pre = 0 if prefetch is None else 1
    n_in, n_out, n_scr = len(in_specs), len(out_specs), len(scratch)
    r_in, r_out = (len(rider.inputs), len(rider.out_shapes)) if rider is not None else (0, 0)
    any_spec = pl.BlockSpec(memory_space=pl.ANY)
    io_aliases = {n_pre + p: q for p, q in (aliases or {}).items()}
    if rider is not None:
        io_aliases.update({n_pre + n_in + p: n_out + q for p, q in rider.aliases.items()})
        in_specs, out_specs = in_specs + [any_spec] * r_in, out_specs + [any_spec] * r_out
        out_shape, scratch = out_shape + rider.out_shapes, scratch + rider.scratch
        args, sem = (*args, *rider.inputs), ["arbitrary"] * len(grid)

    def wrapped(*refs):
        pre, refs = refs[:n_pre], refs[n_pre:]
        if rider is None:
            return body(*pre, *refs)
        ins, refs = refs[:n_in], refs[n_in:]
        r_ins, refs = refs[:r_in], refs[r_in:]
        outs, refs = refs[:n_out], refs[n_out:]
        r_outs, refs = refs[:r_out], refs[r_out:]
        scr, r_scr = refs[:n_scr], refs[n_scr:]
        first = functools.reduce(jnp.logical_and, [pl.program_id(ax) == 0 for ax in range(len(grid))])
        last = functools.reduce(jnp.logical_and, [pl.program_id(ax) == grid[ax] - 1 for ax in range(len(grid))])

        @pl.when(first)
        def _():
            rider.start(r_ins, r_outs, r_scr)

        body(*pre, *ins, *outs, *scr)

        @pl.when(last)
        def _():
            rider.finish(r_ins, r_outs, r_scr)

    if prefetch is None:
        res = pl.pallas_call(wrapped, name=name, grid=grid, in_specs=in_specs, out_specs=out_specs, out_shape=out_shape,
                             scratch_shapes=scratch, input_output_aliases=io_aliases,
                             compiler_params=_cparams(*sem))(*args)
    else:
        grid_spec = pltpu.PrefetchScalarGridSpec(num_scalar_prefetch=1, grid=grid, in_specs=in_specs,
                                                 out_specs=out_specs, scratch_shapes=scratch)
        res = pl.pallas_call(wrapped, name=name, grid_spec=grid_spec, out_shape=out_shape,
                             input_output_aliases=io_aliases, compiler_params=_cparams(*sem))(prefetch, *args)
    return list(res[:n_out]), list(res[n_out:])


V7X_MM_VMEM_BUDGET = 40 * 1024 * 1024


def _mm_plan(form, m, n, k, a_size, b_size, o_size, has_acc):
    tn = n if (form == "tn" and n <= 2048) else _tile(n, 512)
    rows = sorted({m} | {t for t in range(128, m, 128) if m % t == 0}, reverse=True)
    for tk, min_tm in ((k, 384), (k if k <= 2816 else _tile(k, 2816), 0)):
        nk = k // tk
        for tm in rows:
            need = 2 * (tm * tk * a_size + tn * tk * b_size + tm * tn * o_size) + tm * tn * 4 * (2 if nk > 1 else 1)
            need += 2 * tm * tn * 4 if has_acc else 0
            if need <= V7X_MM_VMEM_BUDGET and tm >= min(min_tm, m):
                return tm, tn, tk
    return _tile(m, 128), tn, tk


def _mm(name, a, b, form, out_dtype, acc=None, rider=None):
    if form == "nt":
        (m, k), (n, k2) = a.shape, b.shape
    elif form == "nn":
        (m, k), (k2, n) = a.shape, b.shape
    else:
        (k, m), (k2, n) = a.shape, b.shape
    assert k == k2, (name, a.shape, b.shape)
    has_acc = acc is not None
    tm, tn, tk = _mm_plan(form, m, n, k, a.dtype.itemsize, b.dtype.itemsize, jnp.dtype(out_dtype).itemsize, has_acc)
    nk = k // tk
    a_spec = {"nt": pl.BlockSpec((tm, tk), lambda i, j, kk: (i, kk)),
              "nn": pl.BlockSpec((tm, tk), lambda i, j, kk: (i, kk)),
              "tn": pl.BlockSpec((tk, tm), lambda i, j, kk: (kk, i))}[form]
    b_spec = {"nt": pl.BlockSpec((tn, tk), lambda i, j, kk: (j, kk)),
              "nn": pl.BlockSpec((tk, tn), lambda i, j, kk: (kk, j)),
              "tn": pl.BlockSpec((tk, tn), lambda i, j, kk: (kk, j))}[form]
    o_spec = pl.BlockSpec((tm, tn), lambda i, j, kk: (i, j))
    dims = _DIMS[form]

    def body(*refs):
        a_ref, b_ref = refs[0], refs[1]
        c_ref = refs[2] if has_acc else None
        o_ref = refs[3] if has_acc else refs[2]

        def finish(r):
            if has_acc:
                r = r + c_ref[...]
            o_ref[...] = r.astype(o_ref.dtype)

        if nk == 1:
            finish(_dot(a_ref[...], b_ref[...], dims))
            return
        acc_ref = refs[-1]
        kk = pl.program_id(2)

        @pl.when(kk == 0)
        def _():
            acc_ref[...] = _dot(a_ref[...], b_ref[...], dims)

        @pl.when(kk > 0)
        def _():
            acc_ref[...] += _dot(a_ref[...], b_ref[...], dims)

        @pl.when(kk == nk - 1)
        def _():
            finish(acc_ref[...])

    in_specs = [a_spec, b_spec] + ([o_spec] if has_acc else [])
    args = (a, b) + ((acc,) if has_acc else ())
    outs, r_outs = _pcall(name, body, (m // tm, n // tn, nk), in_specs, [o_spec],
                          [jax.ShapeDtypeStruct((m, n), out_dtype)], args, ("parallel", "parallel", "arbitrary"),
                          scratch=[pltpu.VMEM((tm, tn), F32)] if nk > 1 else [], rider=rider)
    return outs[0] if rider is None else (outs[0], r_outs)


def _mm_fused(name, a, bs, epilogue, tile_ins, out_dtypes, rider=None):
    (m, k), (n, _) = a.shape, bs[0].shape
    tn = _tile(n, 512)
    tm = None
    for cand in sorted({m} | {t for t in range(128, m, 128) if m % t == 0}, reverse=True):
        per_tile = sum(t.dtype.itemsize for t in tile_ins) + sum(jnp.dtype(d).itemsize for d in out_dtypes)
        need = 2 * (cand * k * a.dtype.itemsize + len(bs) * tn * k * bs[0].dtype.itemsize + cand * tn * per_tile)
        need += (len(bs) + 2) * cand * tn * 4
        if need <= V7X_MM_VMEM_BUDGET:
            tm = cand
            break
    assert tm is not None, name
    n_b, n_t = len(bs), len(tile_ins)
    tile = pl.BlockSpec((tm, tn), lambda i, j: (i, j))

    def body(a_ref, *refs):
        prods = [_dot(a_ref[...], r[...], _NT) for r in refs[:n_b]]
        outs = epilogue(prods, [r[...].astype(F32) for r in refs[n_b:n_b + n_t]])
        for r, o in zip(refs[n_b + n_t:], outs):
            r[...] = o.astype(r.dtype)

    outs, r_outs = _pcall(name, body, (m // tm, n // tn),
                          [pl.BlockSpec((tm, k), lambda i, j: (i, 0))] + [pl.BlockSpec((tn, k), lambda i, j: (j, 0))] * n_b
                          + [tile] * n_t, [tile] * len(out_dtypes),
                          [jax.ShapeDtypeStruct((m, n), d) for d in out_dtypes], (a, *bs, *tile_ins),
                          ("parallel", "parallel"), rider=rider)
    return outs if rider is None else (outs, r_outs)


def _rows(name, fn, n_rows, tm, nbc, row_ins, type_ins, row_outs, acc_outs, rider=None):
    n_ri, n_ti, n_ro, n_ao = len(row_ins), len(type_ins), len(row_outs), len(acc_outs)

    def row_map(i, cb, roff):
        return (jnp.maximum(i - roff, 0), cb)

    def type_map(i):
        return (jnp.where(i >= nbc, 1, 0), 0, 0)

    in_specs, args = [], []
    for arr, cb, width, roff in row_ins:
        in_specs.append(pl.BlockSpec((tm, width), functools.partial(row_map, cb=cb, roff=roff)))
        args.append(arr)
    def shared_map(i):
        return (0, 0, 0)

    for arr in type_ins:
        in_specs.append(pl.BlockSpec((None, 1, arr.shape[-1]), type_map if arr.shape[0] == 2 else shared_map))
        args.append(arr)
    out_shape, out_specs = [], []
    for total, width, dtype, roff in row_outs:
        out_shape.append(jax.ShapeDtypeStruct((total, width), dtype))
        out_specs.append(pl.BlockSpec((tm, width), functools.partial(row_map, cb=0, roff=roff)))
    acc_shared = [isinstance(w, tuple) for w in acc_outs]
    for width in acc_outs:
        if isinstance(width, tuple):
            out_shape.append(jax.ShapeDtypeStruct((1, 1, width[0]), F32))
            out_specs.append(pl.BlockSpec((None, 1, width[0]), shared_map))
        else:
            out_shape.append(jax.ShapeDtypeStruct((2, 1, width), F32))
            out_specs.append(pl.BlockSpec((None, 1, width), type_map))
    n_in = n_ri + n_ti

    def body(*refs):
        i = pl.program_id(0)
        outs = fn(*[r[...].astype(F32) for r in refs[:n_in]])
        if not isinstance(outs, (tuple, list)):
            outs = (outs,)
        assert len(outs) == n_ro + n_ao, (name, len(outs))
        for r, o in zip(refs[n_in:n_in + n_ro], outs[:n_ro]):
            r[...] = o.astype(r.dtype)
        if n_ao:
            for r, o, shared in zip(refs[n_in + n_ro:], outs[n_ro:], acc_shared):
                first = i == 0 if shared else jnp.logical_or(i == 0, i == nbc)
                o = jnp.broadcast_to(o.astype(F32), r.shape)

                @pl.when(first)
                def _(r=r, o=o):
                    r[...] = o

                @pl.when(jnp.logical_not(first))
                def _(r=r, o=o):
                    r[...] += o

    outs, r_outs = _pcall(name, body, (n_rows // tm,), in_specs, out_specs, out_shape, args, ("arbitrary",),
                          rider=rider)
    return outs if rider is None else (outs, r_outs)


def _vjp_fn(f, n_row, n_cot, keep=None):
    def g(*args):
        prim = args[:n_row] + args[n_row + n_cot:]
        cots = args[n_row:n_row + n_cot]
        out, vjp = jax.vjp(f, *prim)
        grads = vjp(tuple(cots) if isinstance(out, (tuple, list)) else cots[0])
        return grads if keep is None else tuple(grads[j] for j in keep)
    return g


def _typed(v):
    return v.reshape(1, 1, -1)


def _ln(x, g, b):
    mu = jnp.mean(x, axis=-1, keepdims=True)
    var = jnp.mean(jnp.square(x - mu), axis=-1, keepdims=True)
    return (x - mu) * lax.rsqrt(var + LN_EPS) * g + b


def _f_mod(x, sc, sh):
    return x * (1.0 + sc) + sh


def _make_f_ln(alpha, with_mod):
    def f(x, o, gate, lng, lnb, *mod):
        xn = _ln(alpha * x + gate * o, lng, lnb)
        if with_mod:
            sc, sh = mod
            return xn, xn * (1.0 + sc) + sh
        return xn
    return f


@jax.custom_vjp
def _rot(y):
    lane = lax.broadcasted_iota(jnp.int32, y.shape, 1)
    return jnp.where(lane % 64 < 32, pltpu.roll(y, 96, axis=1), pltpu.roll(y, 32, axis=1))


_rot.defvjp(lambda y: (_rot(y), None), lambda _, g: (_rot(g),))


def _f_prep(p, cos, sin, qg, kg):
    def head(xh, g):
        ms = jnp.mean(jnp.square(xh), axis=-1, keepdims=True)
        y = xh * lax.rsqrt(ms + RMS_EPS) * g
        return y * cos + _rot(y) * sin
    q = jnp.concatenate([head(p[:, h * HEAD_DIM:(h + 1) * HEAD_DIM], qg) for h in range(N_HEADS)], axis=1)
    k = jnp.concatenate([head(p[:, OFF_K + h * HEAD_DIM:OFF_K + (h + 1) * HEAD_DIM], kg)
                         for h in range(N_KV_HEADS)], axis=1)
    return q, k, p[:, OFF_V:OFF_POOL]


def _f_gate(g, t0, t1, t2, t3, b):
    d = t0.shape[-1]
    ts = (t0, t1, t2, t3)
    terms = [jax.nn.sigmoid(g[:, k * d:(k + 1) * d] + b[:, k * d:(k + 1) * d]) * ts[k] for k in range(N_BRANCH)]
    return terms[0] + terms[1] + terms[2] + terms[3]


def _f_swiglu(a, b):
    return jax.nn.silu(a) * b


def _softmax(raw):
    e = jnp.exp2((raw - jnp.max(raw, axis=-1, keepdims=True)) * (ATT_SCALE * math.log2(math.e)))
    return e / jnp.sum(e, axis=-1, keepdims=True)


def _attn_fwd(name, q, k, v, rc, ctx_queries, tq=256, rider=None):
    r = q.shape[0]
    assert rc % tq == 0 and r % tq == 0
    nqc = rc // tq

    def body(q_ref, k_ref, v_ref, o_ref, p_ref):
        qi = pl.program_id(1)

        def attend(nk):
            p = _softmax(_dot(q_ref[...], k_ref[0:nk, :], _NT)).astype(BF16)
            p_ref[:, 0:nk] = p
            o_ref[...] = _dot(p, v_ref[0:nk, :], _NN).astype(o_ref.dtype)

        @pl.when(qi < nqc)
        def _():
            if ctx_queries:
                attend(rc)
            else:
                o_ref[...] = jnp.zeros_like(o_ref)

        @pl.when(qi >= nqc)
        def _():
            attend(r)

    outs, r_outs = _pcall(
        name, body, (N_HEADS, r // tq),
        [pl.BlockSpec((tq, HEAD_DIM), lambda h, i: (i, h)),
         pl.BlockSpec((r, HEAD_DIM), lambda h, i: (0, h // KV_GROUP)),
         pl.BlockSpec((r, HEAD_DIM), lambda h, i: (0, h // KV_GROUP))],
        [pl.BlockSpec((tq, HEAD_DIM), lambda h, i: (i, h)), pl.BlockSpec((None, tq, r), lambda h, i: (h, i, 0))],
        [jax.ShapeDtypeStruct((r, Q_W), BF16), jax.ShapeDtypeStruct((N_HEADS, r, r), BF16)], (q, k, v),
        ("parallel", "parallel"), rider=rider)
    return outs if rider is None else (outs, r_outs)


def _attn_bwd(name, q, k, v, pw, do, rc, ctx_queries, tq=256, rider=None):
    r = q.shape[0]
    nqc = rc // tq

    def body(q_ref, k_ref, v_ref, p_ref, do_ref, dq_ref, dk_ref, dv_ref):
        g, qi = pl.program_id(1), pl.program_id(2)

        @pl.when(jnp.logical_and(g == 0, qi == 0))
        def _():
            dk_ref[...] = jnp.zeros_like(dk_ref)
            dv_ref[...] = jnp.zeros_like(dv_ref)

        def grad(nk):
            qb, kb, vb = q_ref[...], k_ref[0:nk, :], v_ref[0:nk, :]
            dob = do_ref[...].astype(BF16)
            p = p_ref[:, 0:nk].astype(F32)
            dv_ref[0:nk, :] += _dot(p, dob, _TN)
            dp = _dot(dob, vb, _NT)
            ds = p * (dp - jnp.sum(dp * p, axis=-1, keepdims=True))
            dq_ref[...] = _dot(ds, kb, _NN) * ATT_SCALE
            dk_ref[0:nk, :] += _dot(ds, qb, _TN) * ATT_SCALE

        @pl.when(qi < nqc)
        def _():
            if ctx_queries:
                grad(rc)
            else:
                dq_ref[...] = jnp.zeros_like(dq_ref)

        @pl.when(qi >= nqc)
        def _():
            grad(r)

    def qmap(kv, g, i):
        return (i, kv * KV_GROUP + g)

    def kvmap(kv, g, i):
        return (0, kv)

    outs, r_outs = _pcall(
        name, body, (N_KV_HEADS, KV_GROUP, r // tq),
        [pl.BlockSpec((tq, HEAD_DIM), qmap), pl.BlockSpec((r, HEAD_DIM), kvmap), pl.BlockSpec((r, HEAD_DIM), kvmap),
         pl.BlockSpec((None, tq, r), lambda kv, g, i: (kv * KV_GROUP + g, i, 0)), pl.BlockSpec((tq, HEAD_DIM), qmap)],
        [pl.BlockSpec((tq, HEAD_DIM), qmap), pl.BlockSpec((r, HEAD_DIM), kvmap), pl.BlockSpec((r, HEAD_DIM), kvmap)],
        [jax.ShapeDtypeStruct((r, Q_W), F32), jax.ShapeDtypeStruct((r, KV_W), F32),
         jax.ShapeDtypeStruct((r, KV_W), F32)],
        (q, k, v, pw, do), ("arbitrary", "arbitrary", "arbitrary"), rider=rider)
    return outs if rider is None else (outs, r_outs)


def _segments(shape, rc):
    t = lax.broadcasted_iota(jnp.int32, shape, 0)
    lo = jnp.where(t < rc, 0, rc)
    hi = jnp.where(t < rc, rc, shape[0])
    return t, lo, hi


def _shifted(x, o, t, lo, hi):
    n = x.shape[0]
    sh = pltpu.roll(x, (-o) % n, axis=0)
    return jnp.where(jnp.logical_and(t + o >= lo, t + o < hi), sh, 0.0)


def _winsum(x, left, right, t, lo, hi):
    acc = x
    for o in range(-left, right + 1):
        if o != 0:
            acc = acc + _shifted(x, o, t, lo, hi)
    return acc


def _pool_parts(z, g, t, lo, hi):
    w = POOL_WINDOWS[g]
    left = w // 2
    right = w - 1 - left
    count = (jnp.minimum(t + right + 1, hi) - jnp.maximum(t - left, lo)).astype(F32)
    return _winsum(z, left, right, t, lo, hi) / count - z, count, left, right


def _pool_fwd(name, p, pool_w, pool_scale, rc):
    r = p.shape[0]

    def body(z_ref, w_ref, s_ref, y_ref):
        t, lo, hi = _segments((r, GC), rc)
        for g in range(N_GROUPS):
            cols = slice(g * GC, (g + 1) * GC)
            d, _, _, _ = _pool_parts(z_ref[:, cols], g, t, lo, hi)
            y_ref[:, cols] = (_dot(d, w_ref[g], _NN) * s_ref[:, cols]).astype(y_ref.dtype)

    return pl.pallas_call(
        body, name=name, grid=(1,),
        in_specs=[pl.BlockSpec((r, BR_W), lambda i: (0, OFF_POOL // BR_W)),
                  pl.BlockSpec((N_GROUPS, GC, GC), lambda i: (0, 0, 0)),
                  pl.BlockSpec((1, BR_W), lambda i: (0, 0))],
        out_specs=pl.BlockSpec((r, BR_W), lambda i: (0, 0)),
        out_shape=jax.ShapeDtypeStruct((r, BR_W), BF16),
        compiler_params=_cparams("arbitrary"),
    )(p, pool_w, pool_scale.reshape(1, BR_W))


def _pool_bwd(name, p, pool_w, pool_scale, dy, rc):
    r = p.shape[0]

    def body(z_ref, w_ref, s_ref, dy_ref, dz_ref, dw_ref, ds_ref):
        t, lo, hi = _segments((r, GC), rc)
        for g in range(N_GROUPS):
            cols = slice(g * GC, (g + 1) * GC)
            d, count, left, right = _pool_parts(z_ref[:, cols], g, t, lo, hi)
            dyg = dy_ref[:, cols]
            ds_ref[:, cols] = jnp.sum(dyg * _dot(d, w_ref[g], _NN), axis=0, keepdims=True)
            dlin = dyg * s_ref[:, cols]
            dw_ref[g] = _dot(d, dlin, _TN)
            dd = _dot(dlin, w_ref[g], _NT)
            dz_ref[:, cols] = (_winsum(dd / count, right, left, t, lo, hi) - dd).astype(dz_ref.dtype)

    return pl.pallas_call(
        body, name=name, grid=(1,),
        in_specs=[pl.BlockSpec((r, BR_W), lambda i: (0, OFF_POOL // BR_W)),
                  pl.BlockSpec((N_GROUPS, GC, GC), lambda i: (0, 0, 0)),
                  pl.BlockSpec((1, BR_W), lambda i: (0, 0)),
                  pl.BlockSpec((r, BR_W), lambda i: (0, 0))],
        out_specs=[pl.BlockSpec((r, BR_W), lambda i: (0, 0)),
                   pl.BlockSpec((N_GROUPS, GC, GC), lambda i: (0, 0, 0)),
                   pl.BlockSpec((1, BR_W), lambda i: (0, 0))],
        out_shape=[jax.ShapeDtypeStruct((r, BR_W), BF16), jax.ShapeDtypeStruct((N_GROUPS, GC, GC), F32),
                   jax.ShapeDtypeStruct((1, BR_W), F32)],
        compiler_params=_cparams("arbitrary"),
    )(p, pool_w, pool_scale.reshape(1, BR_W), dy)


def _f_sgu_v(pvg, lng, lnb):
    return _ln(jax.nn.gelu(pvg), lng, lnb)


def _sgu_fwd(name, p, ln_g, ln_b, sgu_w, sgu_b):
    r = p.shape[0]

    def body(pu_ref, pv_ref, g_ref, b_ref, w_ref, sb_ref, y_ref):
        vn = _f_sgu_v(pv_ref[...], g_ref[...], b_ref[...])
        u = jax.nn.gelu(pu_ref[...])
        for g in range(N_GROUPS):
            cols = slice(g * GC, (g + 1) * GC)
            s = _dot(w_ref[g], vn[:, cols], _NN) + sb_ref[g]
            y_ref[:, cols] = (u[:, cols] * s).astype(y_ref.dtype)

    return pl.pallas_call(
        body, name=name, grid=(r // SGU_CHUNK,),
        in_specs=[pl.BlockSpec((SGU_CHUNK, BR_W), lambda i: (i, OFF_U // BR_W)),
                  pl.BlockSpec((SGU_CHUNK, BR_W), lambda i: (i, OFF_VG // BR_W)),
                  pl.BlockSpec((1, BR_W), lambda i: (0, 0)), pl.BlockSpec((1, BR_W), lambda i: (0, 0)),
                  pl.BlockSpec((N_GROUPS, GC, GC), lambda i: (0, 0, 0)),
                  pl.BlockSpec((N_GROUPS, SGU_CHUNK, 1), lambda i: (0, 0, 0))],
        out_specs=pl.BlockSpec((SGU_CHUNK, BR_W), lambda i: (i, 0)),
        out_shape=jax.ShapeDtypeStruct((r, BR_W), BF16),
        compiler_params=_cparams("parallel"),
    )(p, p, ln_g.reshape(1, BR_W), ln_b.reshape(1, BR_W), sgu_w, sgu_b.reshape(N_GROUPS, SGU_CHUNK, 1))


def _sgu_bwd(name, p, ln_g, ln_b, sgu_w, sgu_b, dy):
    r = p.shape[0]

    def body(pu_ref, pv_ref, g_ref, b_ref, w_ref, sb_ref, dy_ref, dp_ref, dg_ref, db_ref, dw_ref, dsb_ref):
        i = pl.program_id(0)

        @pl.when(i == 0)
        def _():
            for ref in (dg_ref, db_ref, dw_ref, dsb_ref):
                ref[...] = jnp.zeros_like(ref)

        vn, vjp_v = jax.vjp(_f_sgu_v, pv_ref[...], g_ref[...], b_ref[...])
        u, vjp_u = jax.vjp(jax.nn.gelu, pu_ref[...])
        dy = dy_ref[...]
        du, dvn = [], []
        for g in range(N_GROUPS):
            cols = slice(g * GC, (g + 1) * GC)
            s = _dot(w_ref[g], vn[:, cols], _NN) + sb_ref[g]
            du.append(dy[:, cols] * s)
            ds = dy[:, cols] * u[:, cols]
            dsb_ref[g] += jnp.sum(ds, axis=1, keepdims=True)
            dw_ref[g] += _dot(ds, vn[:, cols], _NT)
            dvn.append(_dot(w_ref[g], ds, _TN))
        (dpu,) = vjp_u(jnp.concatenate(du, axis=1))
        dpv, dg, db = vjp_v(jnp.concatenate(dvn, axis=1))
        dp_ref[:, 0:BR_W] = dpu.astype(dp_ref.dtype)
        dp_ref[:, BR_W:2 * BR_W] = dpv.astype(dp_ref.dtype)
        dg_ref[...] += dg
        db_ref[...] += db

    vec = pl.BlockSpec((1, BR_W), lambda i: (0, 0))
    wsp = pl.BlockSpec((N_GROUPS, GC, GC), lambda i: (0, 0, 0))
    bsp = pl.BlockSpec((N_GROUPS, SGU_CHUNK, 1), lambda i: (0, 0, 0))
    return pl.pallas_call(
        body, name=name, grid=(r // SGU_CHUNK,),
        in_specs=[pl.BlockSpec((SGU_CHUNK, BR_W), lambda i: (i, OFF_U // BR_W)),
                  pl.BlockSpec((SGU_CHUNK, BR_W), lambda i: (i, OFF_VG // BR_W)),
                  vec, vec, wsp, bsp, pl.BlockSpec((SGU_CHUNK, BR_W), lambda i: (i, 0))],
        out_specs=[pl.BlockSpec((SGU_CHUNK, 2 * BR_W), lambda i: (i, 0)), vec, vec, wsp, bsp],
        out_shape=[jax.ShapeDtypeStruct((r, 2 * BR_W), BF16), jax.ShapeDtypeStruct((1, BR_W), F32),
                   jax.ShapeDtypeStruct((1, BR_W), F32), jax.ShapeDtypeStruct((N_GROUPS, GC, GC), F32),
                   jax.ShapeDtypeStruct((N_GROUPS, SGU_CHUNK, 1), F32)],
        compiler_params=_cparams("arbitrary"),
    )(p, p, ln_g.reshape(1, BR_W), ln_b.reshape(1, BR_W), sgu_w, sgu_b.reshape(N_GROUPS, SGU_CHUNK, 1), dy)


def _conv_w8(conv_w):
    return jnp.concatenate([conv_w, jnp.zeros((8 - conv_w.shape[0], conv_w.shape[1]), F32)], axis=0)


def _conv_fwd(name, p, conv_w, rc):
    r = p.shape[0]

    def body(cb_ref, cc_ref, cx_ref, w_ref, y_ref):
        t, lo, hi = _segments((r, GC), rc)
        z = cc_ref[...] * cx_ref[...]
        w = w_ref[...]
        c = _shifted(z, -1, t, lo, hi) * w[0:1] + z * w[1:2] + _shifted(z, 1, t, lo, hi) * w[2:3]
        y_ref[...] = (cb_ref[...] * c).astype(y_ref.dtype)

    nb = OFF_CB // GC
    return pl.pallas_call(
        body, name=name, grid=(N_GROUPS,),
        in_specs=[pl.BlockSpec((r, GC), lambda j: (0, nb + j)),
                  pl.BlockSpec((r, GC), lambda j: (0, nb + N_GROUPS + j)),
                  pl.BlockSpec((r, GC), lambda j: (0, nb + 2 * N_GROUPS + j)),
                  pl.BlockSpec((8, GC), lambda j: (0, j))],
        out_specs=pl.BlockSpec((r, GC), lambda j: (0, j)),
        out_shape=jax.ShapeDtypeStruct((r, BR_W), BF16),
        compiler_params=_cparams("parallel"),
    )(p, p, p, _conv_w8(conv_w))


def _conv_bwd(name, p, conv_w, dy, rc):
    r = p.shape[0]

    def body(cb_ref, cc_ref, cx_ref, w_ref, dy_ref, dcb_ref, dcc_ref, dcx_ref, dw_ref):
        t, lo, hi = _segments((r, GC), rc)
        cc, cx, w, dy = cc_ref[...], cx_ref[...], w_ref[...], dy_ref[...]
        z = cc * cx
        zp, zn = _shifted(z, -1, t, lo, hi), _shifted(z, 1, t, lo, hi)
        dcb_ref[...] = (dy * (zp * w[0:1] + z * w[1:2] + zn * w[2:3])).astype(dcb_ref.dtype)
        dc = dy * cb_ref[...]
        dw_ref[...] = jnp.concatenate(
            [jnp.sum(dc * zp, axis=0, keepdims=True), jnp.sum(dc * z, axis=0, keepdims=True),
             jnp.sum(dc * zn, axis=0, keepdims=True), jnp.zeros((5, GC), F32)], axis=0)
        dz = dc * w[1:2] + _shifted(dc, 1, t, lo, hi) * w[0:1] + _shifted(dc, -1, t, lo, hi) * w[2:3]
        dcc_ref[...] = (dz * cx).astype(dcc_ref.dtype)
        dcx_ref[...] = (dz * cc).astype(dcx_ref.dtype)

    nb = OFF_CB // GC
    return pl.pallas_call(
        body, name=name, grid=(N_GROUPS,),
        in_specs=[pl.BlockSpec((r, GC), lambda j: (0, nb + j)),
                  pl.BlockSpec((r, GC), lambda j: (0, nb + N_GROUPS + j)),
                  pl.BlockSpec((r, GC), lambda j: (0, nb + 2 * N_GROUPS + j)),
                  pl.BlockSpec((8, GC), lambda j: (0, j)),
                  pl.BlockSpec((r, GC), lambda j: (0, j))],
        out_specs=[pl.BlockSpec((r, GC), lambda j: (0, j))] * 3 + [pl.BlockSpec((8, GC), lambda j: (0, j))],
        out_shape=[jax.ShapeDtypeStruct((r, BR_W), BF16)] * 3 + [jax.ShapeDtypeStruct((8, BR_W), F32)],
        compiler_params=_cparams("parallel"),
    )(p, p, p, _conv_w8(conv_w), dy)


def _rope_tables(rc, n):
    rows = n // GRID_W
    row = jnp.repeat(jnp.arange(rows), GRID_W).astype(F32)
    col = jnp.tile(jnp.arange(GRID_W), rows).astype(F32)
    inv = ROPE_THETA ** (-jnp.arange(0, ROPE_AXIS_DIM, 2, dtype=F32) / ROPE_AXIS_DIM)
    ang_r, ang_c = row[:, None] * inv, col[:, None] * inv
    cos = jnp.concatenate([jnp.cos(ang_r), jnp.cos(ang_r), jnp.cos(ang_c), jnp.cos(ang_c)], axis=1)
    sin = jnp.concatenate([-jnp.sin(ang_r), jnp.sin(ang_r), -jnp.sin(ang_c), jnp.sin(ang_c)], axis=1)
    cos = jnp.concatenate([jnp.ones((rc, HEAD_DIM), F32), cos], axis=0)
    sin = jnp.concatenate([jnp.zeros((rc, HEAD_DIM), F32), sin], axis=0)
    return cos, sin


MOD_NAMES = ("sh1", "sc1", "g1", "sh2", "sc2", "g2")


def _local_step(xin, target, mod, comm, sp, rc, alpha):
    def carrying(fn):
        def call(name, *args, **kw):
            rider = comm.rider(name)
            if rider is None:
                return fn(name, *args, **kw)
            res, r_outs = fn(name, *args, rider=rider, **kw)
            comm.deliver(name, r_outs)
            return res
        return call

    mm, rows, attn_fwd, attn_bwd = carrying(_mm), carrying(_rows), carrying(_attn_fwd), carrying(_attn_bwd)
    mm_fused = carrying(_mm_fused)
    r, d = xin.shape
    n_layers = mod.shape[0]
    tm_n, tm_w = 256, 128
    nbc_n, nbc_w = rc // tm_n, rc // tm_w
    cos, sin = _rope_tables(rc, r - rc)
    mp = mod.reshape(n_layers, 2, 6, 1, d)
    mods = [{nm: mp[i, :, j] for j, nm in enumerate(MOD_NAMES)} for i in range(n_layers)]
    f_ln_mod, f_ln_last = _make_f_ln(alpha, True), _make_f_ln(alpha, False)

    def whole(arr, roff=0):
        return (arr, 0, arr.shape[1], roff)

    (hb,) = rows("mod_in", _f_mod, r, tm_n, nbc_n, [whole(xin)], [mods[0]["sc1"], mods[0]["sh1"]],
                 [(r, d, BF16, 0)], [])
    saved = []
    x = xin
    for i in range(n_layers):
        last = i == n_layers - 1
        w, s, m = functools.partial(comm.weight, i), sp[i], mods[i]
        sv = {"x": x, "hb": hb}
        p = mm(f"l{i}_in", hb, w("in_t"), "nt", F32)
        q, k, v = rows(f"l{i}_prep", _f_prep, r, tm_n, nbc_n,
                       [(p, 0, OFF_POOL, 0), whole(cos), whole(sin)], [_typed(s["q_norm_g"]), _typed(s["k_norm_g"])],
                       [(r, Q_W, BF16, 0), (r, KV_W, BF16, 0), (r, KV_W, BF16, 0)], [])
        att, sv["pw"] = attn_fwd(f"l{i}_attn", q, k, v, rc, not last)
        ys = [att,
              _pool_fwd(f"l{i}_pool", p, s["pool_w"], s["pool_scale"], rc),
              _sgu_fwd(f"l{i}_sgu", p, s["sgu_ln_g"], s["sgu_ln_b"], s["sgu_w"], s["sgu_b"]),
              _conv_fwd(f"l{i}_conv", p, s["conv_w"], rc)]
        ts = [mm(f"l{i}_br{kk}", ys[kk], w(f"br{kk}"), "nt", BF16) for kk in range(N_BRANCH)]
        gpre = mm(f"l{i}_gate", hb, w("gate_t"), "nt", BF16)
        (mg,) = rows(f"l{i}_merge", _f_gate, r, tm_w, nbc_w, [whole(gpre)] + [whole(t) for t in ts],
                     [_typed(s["b_gate"])], [(r, d, BF16, 0)], [])
        o = mm(f"l{i}_o", mg, w("o"), "nn", F32)
        x1, h2b = rows(f"l{i}_ln1", f_ln_mod, r, tm_n, nbc_n, [whole(x), whole(o)],
                       [m["g1"], _typed(s["ln1_g"]), _typed(s["ln1_b"]), m["sc2"], m["sh2"]],
                       [(r, d, F32, 0), (r, d, BF16, 0)], [])
        af, bf, f = mm_fused(f"l{i}_ffgu", h2b, [w("ffg_t"), w("ffu_t")],
                             lambda prods, _: (prods[0], prods[1], _f_swiglu(prods[0], prods[1])), [], [BF16, BF16, BF16])
        o2 = mm(f"l{i}_ffd", f, w("ffd"), "nn", F32)
        if last:
            (x2,) = rows(f"l{i}_ln2", f_ln_last, r, tm_n, nbc_n, [whole(x1), whole(o2)],
                         [m["g2"], _typed(s["ln2_g"]), _typed(s["ln2_b"])], [(r, d, F32, 0)], [])
            hb = None
        else:
            nx = mods[i + 1]
            x2, hb = rows(f"l{i}_ln2", f_ln_mod, r, tm_n, nbc_n, [whole(x1), whole(o2)],
                          [m["g2"], _typed(s["ln2_g"]), _typed(s["ln2_b"]), nx["sc1"], nx["sh1"]],
                          [(r, d, F32, 0), (r, d, BF16, 0)], [])
        sv.update(p=p, gpre=gpre, q=q, k=k, v=v, ys=ys, ts=ts, mg=mg, o=o, x1=x1, h2b=h2b, af=af, bf=bf, f=f, o2=o2)
        saved.append(sv)
        x = x2

    lat = jnp.concatenate([jnp.zeros((1, 1, 128), F32), jnp.ones((1, 1, 128), F32)], axis=0)

    def f_loss(xb, tb, msk):
        diff = (xb - tb) * msk[:, 0:1]
        part = jnp.sum(jnp.mean(jnp.square(diff), axis=-1, keepdims=True), axis=0, keepdims=True)
        return diff * (1.0 / d), jnp.broadcast_to(part, (1, 128))

    dx_direct, loss_acc = rows("loss", f_loss, r, tm_n, nbc_n, [whole(x), whole(target, nbc_n)], [lat],
                               [(r, d, F32, 0)], [128])
    loss = 0.5 * loss_acc[1, 0, 0]

    dmods = [dict() for _ in range(n_layers)]
    dsp = [dict() for _ in range(n_layers)]
    dh = None

    def small_done(j):
        ds = dict(dsp[j])
        for nm in ("ln1_g", "ln1_b", "ln2_g", "ln2_b", "b_gate", "q_norm_g", "k_norm_g"):
            ds[nm] = ds[nm][0, 0]
        for nm in ("pool_scale", "sgu_ln_g", "sgu_ln_b"):
            ds[nm] = ds[nm].reshape(-1)
        ds["sgu_b"] = ds["sgu_b"].reshape(N_GROUPS, SGU_CHUNK)
        comm.small_ready(j, jnp.concatenate([dmods[j][nm][:, 0, :] for nm in MOD_NAMES], axis=-1), ds)

    for i in reversed(range(n_layers)):
        last = i == n_layers - 1
        w, s, m, sv = functools.partial(comm.weight, i), sp[i], mods[i], saved[i]
        dm, dw, ds = dmods[i], {}, dsp[i]
        ln2 = [m["g2"], _typed(s["ln2_g"]), _typed(s["ln2_b"])]
        if last:
            res = rows(f"l{i}_ln2_bwd", _vjp_fn(f_ln_last, 2, 1), r, tm_n, nbc_n,
                       [whole(sv["x1"]), whole(sv["o2"]), whole(dx_direct)], ln2,
                       [(r, d, F32, 0), (r, d, BF16, 0)], [d, (d,), (d,)])
            dx1, do2, dm["g2"], dlg, dlb = res
        else:
            nx = mods[i + 1]
            res = rows(f"l{i}_ln2_bwd", _vjp_fn(f_ln_mod, 2, 2), r, tm_n, nbc_n,
                       [whole(sv["x1"]), whole(sv["o2"]), whole(dx_direct), whole(dh)],
                       ln2 + [nx["sc1"], nx["sh1"]],
                       [(r, d, F32, 0), (r, d, BF16, 0)], [d, (d,), (d,), d, d])
            dx1, do2, dm["g2"], dlg, dlb, dmods[i + 1]["sc1"], dmods[i + 1]["sh1"] = res
            small_done(i + 1)
        ds["ln2_g"], ds["ln2_b"] = dlg, dlb
        dab, dbb = mm_fused(f"l{i}_dF", do2, [w("ffd")],
                            lambda prods, tiles: jax.vjp(_f_swiglu, *tiles)[1](prods[0]), [sv["af"], sv["bf"]],
                            [BF16, BF16])
        comm.grads(i, {"ffd": mm(f"l{i}_dWffd", sv["f"], do2, "tn", BF16)})
        comm.grads(i, {"ffg_t": mm(f"l{i}_dWffg", dab, sv["h2b"], "tn", BF16)})
        comm.grads(i, {"ffu_t": mm(f"l{i}_dWffu", dbb, sv["h2b"], "tn", BF16)})
        dh2 = mm(f"l{i}_dh2a", dab, w("ffg_t"), "nn", F32)
        dh2 = mm(f"l{i}_dh2b", dbb, w("ffu_t"), "nn", F32, acc=dh2)
        res = rows(f"l{i}_ln1_bwd", _vjp_fn(f_ln_mod, 2, 2), r, tm_n, nbc_n,
                   [whole(sv["x"]), whole(sv["o"]), whole(dx1), whole(dh2)],
                   [m["g1"], _typed(s["ln1_g"]), _typed(s["ln1_b"]), m["sc2"], m["sh2"]],
                   [(r, d, F32, 0), (r, d, BF16, 0)], [d, (d,), (d,), d, d])
        dx_direct, do, dm["g1"], ds["ln1_g"], ds["ln1_b"], dm["sc2"], dm["sh2"] = res
        dmg = mm(f"l{i}_dMg", do, w("o"), "nt", F32)
        comm.grads(i, {"o": mm(f"l{i}_dWo", sv["mg"], do, "tn", BF16)})
        res = rows(f"l{i}_merge_bwd", _vjp_fn(_f_gate, 5, 1), r, tm_w, nbc_w,
                   [whole(sv["gpre"])] + [whole(t) for t in sv["ts"]] + [whole(dmg)], [_typed(s["b_gate"])],
                   [(r, N_BRANCH * d, BF16, 0)] + [(r, d, BF16, 0)] * N_BRANCH, [(N_BRANCH * d,)])
        dgb, dts, ds["b_gate"] = res[0], res[1:1 + N_BRANCH], res[1 + N_BRANCH]
        comm.grads(i, {"gate_t": mm(f"l{i}_dWgate", dgb, sv["hb"], "tn", BF16)})
        dys = [mm(f"l{i}_dY{kk}", dts[kk], w(f"br{kk}"), "nn", F32) for kk in range(N_BRANCH)]
        for kk in range(N_BRANCH):
            comm.grads(i, {f"br{kk}": mm(f"l{i}_dWbr{kk}", dts[kk], sv["ys"][kk], "tn", BF16)})
        dq, dk, dv = attn_bwd(f"l{i}_attn_bwd", sv["q"], sv["k"], sv["v"], sv["pw"], dys[0], rc, not last)
        res = rows(f"l{i}_prep_bwd", _vjp_fn(_f_prep, 3, 3, keep=(0, 3, 4)), r, tm_n, nbc_n,
                   [(sv["p"], 0, OFF_POOL, 0), whole(cos), whole(sin), whole(dq), whole(dk), whole(dv)],
                   [_typed(s["q_norm_g"]), _typed(s["k_norm_g"])],
                   [(r, OFF_POOL, BF16, 0)], [(HEAD_DIM,), (HEAD_DIM,)])
        dp_qkv, ds["q_norm_g"], ds["k_norm_g"] = res
        dp_pool, ds["pool_w"], ds["pool_scale"] = _pool_bwd(f"l{i}_pool_bwd", sv["p"], s["pool_w"], s["pool_scale"],
                                                            dys[1], rc)
        dp_sgu, ds["sgu_ln_g"], ds["sgu_ln_b"], ds["sgu_w"], ds["sgu_b"] = _sgu_bwd(
            f"l{i}_sgu_bwd", sv["p"], s["sgu_ln_g"], s["sgu_ln_b"], s["sgu_w"], s["sgu_b"], dys[2])
        dp_cb, dp_cc, dp_cx, dcw = _conv_bwd(f"l{i}_conv_bwd", sv["p"], s["conv_w"], dys[3], rc)
        ds["conv_w"] = dcw[0:3]
        dpb = jnp.concatenate([dp_qkv, dp_pool, dp_sgu, dp_cb, dp_cc, dp_cx], axis=1)
        comm.grads(i, {"in_t": mm(f"l{i}_dWin", dpb, sv["hb"], "tn", BF16)})
        dh = mm(f"l{i}_dhb_a", dpb, w("in_t"), "nn", F32)
        dh = mm(f"l{i}_dhb_b", dgb, w("gate_t"), "nn", F32, acc=dh)

    def f_mod_bwd(xb, ddir, dhb, sc, sh):
        _, vjp = jax.vjp(_f_mod, xb, sc, sh)
        dxb, dsc, dsh = vjp(dhb)
        return dxb + ddir, dsc, dsh

    grad_x, dmods[0]["sc1"], dmods[0]["sh1"] = rows(
        "mod_in_bwd", f_mod_bwd, r, tm_n, nbc_n, [whole(xin), whole(dx_direct), whole(dh)],
        [mods[0]["sc1"], mods[0]["sh1"]], [(r - rc, d, F32, nbc_n)], [d, d])
    small_done(0)
    return loss, grad_x


def _direct_rider(src):
    def peers():
        mx, my, mc = [lax.axis_index(a) for a in MESH_AXES]
        out = []
        for kk in range(1, N_DEV):
            px = 1 - mx if kk & 4 else mx
            py = 1 - my if kk & 2 else my
            pc = 1 - mc if kk & 1 else mc
            out.append(((px, py, pc), 4 * px + 2 * py + pc))
        return 4 * mx + 2 * my + mc, out

    def start(ins, outs, sems):
        send, recv, loc = sems
        me, others = peers()
        pltpu.make_async_copy(ins[0], outs[0].at[me], loc.at[0]).start()
        for j, (peer, _) in enumerate(others):
            _rcopy(ins[0], outs[0].at[me], send, recv, j, peer).start()

    def finish(ins, outs, sems):
        send, recv, loc = sems
        me, others = peers()
        for j, (peer, peer_l) in enumerate(others):
            cp = _rcopy(ins[0], outs[0].at[peer_l], send, recv, j, peer)
            cp.wait_recv()
            cp.wait_send()
        pltpu.make_async_copy(ins[0], outs[0].at[me], loc.at[0]).wait()

    return _Rider([src], [jax.ShapeDtypeStruct((N_DEV,) + src.shape, src.dtype)],
                  _sem_scratch(N_DEV - 1, N_DEV - 1, 1), start, finish)


def _mesh_place():
    mx, my, mc = [lax.axis_index(a) for a in MESH_AXES]
    chips = [(1 - mx, my), (mx, 1 - my), (1 - mx, 1 - my)]

    def lid(px, py, pc):
        return 4 * px + 2 * py + pc

    return (mx, my, mc), (mx, my, 1 - mc), chips, lid


def _rcopy(src, dst, send_sems, recv_sems, k, to):
    return pltpu.make_async_remote_copy(src_ref=src, dst_ref=dst, send_sem=send_sems.at[k], recv_sem=recv_sems.at[k],
                                        device_id=to, device_id_type=pl.DeviceIdType.MESH)


def _sem_scratch(*sizes):
    return [pltpu.SemaphoreType.DMA((s,)) for s in sizes]


def _gather_rider(src, rows1, rows2, buf=None):
    def place():
        (mx, my, mc), sib, chips, lid = _mesh_place()
        xn, yn, dg = [(*chip, mc) for chip in chips]
        return lid(mx, my, mc), sib, xn, yn, dg, lid

    def halves(rows):
        r0, r1 = rows
        mid = r0 + (r1 - r0) // 32 * 16
        return pl.ds(r0, mid - r0), pl.ds(mid, r1 - mid), pl.ds(r0, r1 - r0)

    n_src = 1 if rows1 is not None else 0

    def start(ins, outs, sems):
        send, recv, loc = sems
        me, sib, xn, yn, dg, lid = place()
        if rows1 is not None:
            win = pl.ds(rows1[0], rows1[1] - rows1[0])
            mine, dst = ins[0].at[win], outs[0].at[me, win]
            pltpu.make_async_copy(mine, dst, loc.at[0]).start()
            for t, to in enumerate((sib, xn, yn)):
                _rcopy(mine, dst, send, recv, t, to).start()
        if rows2 is not None:
            top, bot, win = halves(rows2)
            xb, yb = outs[0].at[lid(*xn)], outs[0].at[lid(*yn)]
            _rcopy(xb.at[top], xb.at[top], send, recv, 3, yn).start()
            _rcopy(yb.at[bot], yb.at[bot], send, recv, 4, xn).start()
            _rcopy(xb.at[win], xb.at[win], send, recv, 5, sib).start()
            _rcopy(yb.at[win], yb.at[win], send, recv, 6, sib).start()

    def finish(ins, outs, sems):
        send, recv, loc = sems
        me, sib, xn, yn, dg, lid = place()
        if rows2 is not None:
            top, bot, win = halves(rows2)
            db = outs[0].at[lid(*dg)]
            _rcopy(db.at[top], db.at[top], send, recv, 3, yn).wait_recv()
            _rcopy(db.at[bot], db.at[bot], send, recv, 4, xn).wait_recv()
            _rcopy(db.at[win], db.at[win], send, recv, 7, sib).start()
            for t, dev in ((5, xn), (6, yn), (7, dg)):
                blk = outs[0].at[lid(dev[0], dev[1], 1 - dev[2]), win]
                _rcopy(blk, blk, send, recv, t, sib).wait_recv()
            for t, part in ((3, top), (4, bot), (5, win), (6, win), (7, win)):
                _rcopy(db.at[part], db.at[part], send, recv, t, sib).wait_send()
        if rows1 is not None:
            win = pl.ds(rows1[0], rows1[1] - rows1[0])
            mine, dst = ins[0].at[win], outs[0].at[me, win]
            for t, dev in enumerate((sib, xn, yn)):
                cp = _rcopy(mine, outs[0].at[lid(*dev), win], send, recv, t, dev)
                cp.wait_recv()
                cp.wait_send()
            pltpu.make_async_copy(mine, dst, loc.at[0]).wait()

    out_shape = jax.ShapeDtypeStruct((N_DEV,) + src.shape, src.dtype)
    inputs = ([src] if n_src else []) + ([buf] if buf is not None else [])
    return _Rider(inputs, [out_shape], _sem_scratch(8, 8, 1), start, finish,
                  aliases={n_src: 0} if buf is not None else {})


def _sibling_rider(part):
    def start(ins, outs, sems):
        send, recv = sems
        (mx, my, mc), sib, chips, lid = _mesh_place()
        for t, slab in enumerate([lid(*sib)] + [lid(*chip, 1 - mc) for chip in chips]):
            _rcopy(ins[0].at[slab], outs[0].at[t], send, recv, t, sib).start()

    def finish(ins, outs, sems):
        send, recv = sems
        _, sib, _, _ = _mesh_place()
        for t in range(4):
            cp = _rcopy(ins[0].at[0], outs[0].at[t], send, recv, t, sib)
            cp.wait_recv()
            cp.wait_send()

    return _Rider([part], [jax.ShapeDtypeStruct((4,) + part.shape[1:], part.dtype)], _sem_scratch(4, 4), start, finish)


def _chips_rider(pair, rows, buf=None):
    r0, r1 = rows
    win = pl.ds(r0, r1 - r0)

    def start(ins, outs, sems):
        send, recv = sems
        (mx, my, mc), sib, chips, lid = _mesh_place()
        for j, chip in enumerate(chips):
            _rcopy(ins[0].at[j, win], outs[0].at[j, win], send, recv, j, (*chip, mc)).start()

    def finish(ins, outs, sems):
        send, recv = sems
        (mx, my, mc), sib, chips, lid = _mesh_place()
        for j, chip in enumerate(chips):
            cp = _rcopy(ins[0].at[j, win], outs[0].at[j, win], send, recv, j, (*chip, mc))
            cp.wait_recv()
            cp.wait_send()

    out_shape = jax.ShapeDtypeStruct(pair.shape, pair.dtype)
    if buf is None:
        return _Rider([pair], [out_shape], _sem_scratch(3, 3), start, finish)
    return _Rider([pair, buf], [out_shape], _sem_scratch(3, 3), start, finish, aliases={1: 0})


def _run_rider(name, rider):
    n_in, n_out = len(rider.inputs), len(rider.out_shapes)

    def body(*refs):
        ins, outs, sems = refs[:n_in], refs[n_in:n_in + n_out], refs[n_in + n_out:]
        rider.start(ins, outs, sems)
        rider.finish(ins, outs, sems)

    any_spec = pl.BlockSpec(memory_space=pl.ANY)
    res = pl.pallas_call(body, name=name, in_specs=[any_spec] * n_in, out_specs=[any_spec] * n_out,
                         out_shape=rider.out_shapes, scratch_shapes=rider.scratch,
                         input_output_aliases=rider.aliases)(*rider.inputs)
    return list(res)


def _slab_ids():
    (mx, my, mc), _, chips, lid = _mesh_place()
    return jnp.stack([lid(*chip, mc) for chip in chips] + [lid(mx, my, mc)]).astype(jnp.int32)


def _pair_sum(name, part, rsib, ids):
    _, n, k = part.shape
    tr = _row_tile(n, 512, 16)

    def body(ids_ref, p_ref, r_ref, o_ref):
        o_ref[...] = (p_ref[...].astype(F32) + r_ref[...].astype(F32)).astype(o_ref.dtype)

    grid_spec = pltpu.PrefetchScalarGridSpec(
        num_scalar_prefetch=1, grid=(3, n // tr),
        in_specs=[pl.BlockSpec((None, tr, k), lambda j, i, ids: (ids[j], i, 0)),
                  pl.BlockSpec((None, tr, k), lambda j, i, ids: (1 + j, i, 0))],
        out_specs=pl.BlockSpec((None, tr, k), lambda j, i, ids: (j, i, 0)))
    return pl.pallas_call(body, name=name, grid_spec=grid_spec, out_shape=jax.ShapeDtypeStruct((3, n, k), part.dtype),
                          compiler_params=_cparams("parallel", "parallel"))(ids, part, rsib)


def _sum5(name, part, rsib, rici, ids, layer, stacked, rider=None):
    _, n, k = part.shape
    tr = _row_tile(n, 512, 16)
    first = isinstance(stacked, int)

    def body(ids_ref, p_ref, r_ref, c_ref, *rest):
        acc = p_ref[...].astype(F32) + r_ref[...].astype(F32)
        for j in range(3):
            acc = acc + c_ref[j].astype(F32)
        rest[-1][...] = acc

    in_specs = [pl.BlockSpec((None, tr, k), lambda i, ids: (ids[3], i, 0)),
                pl.BlockSpec((None, tr, k), lambda i, ids: (0, i, 0)),
                pl.BlockSpec((3, tr, k), lambda i, ids: (0, i, 0))] + ([] if first else [pl.BlockSpec(memory_space=pl.ANY)])
    n_layers = stacked if first else stacked.shape[0]
    outs, r_outs = _pcall(name, body, (n // tr,), in_specs, [pl.BlockSpec((None, tr, k), lambda i, ids: (layer, i, 0))],
                          [jax.ShapeDtypeStruct((n_layers, n, k), F32)],
                          (part, rsib, rici) + (() if first else (stacked,)), ("parallel",), rider=rider, prefetch=ids,
                          aliases={} if first else {3: 0})
    return outs[0] if rider is None else (outs[0], r_outs)


W_KEYS = ("in_t", "br0", "br1", "br2", "br3", "gate_t", "o", "ffg_t", "ffu_t", "ffd")
SUMS_TRANSPOSED_LATER = ("gate_t", "br0", "br1", "br2", "br3")


CARRIER_US = {"mod_in": 12, "in": 55, "gate": 95, "prep": 19, "attn": 125,"br0": 15, "merge": 50, "o": 25, "ln1": 27,
              "ffgu": 135, "ffd": 73, "ln2": 27, "loss": 20, "ln2_bwd": 44, "dF": 80,
              "dWffd": 64, "dh2a": 75, "dh2b": 75, "dWffg": 64, "dWffu": 64, "ln1_bwd": 44,
              "dMg": 25, "dWo": 25, "merge_bwd": 80, "dY0": 14, "dWbr0": 15, "attn_bwd": 140,"prep_bwd": 28,
              "dWin": 54, "dWgate": 95, "dhb_a": 64, "dhb_b": 115}
ICI_US_PER_MIB = 45.0
GATHER_US_PER_MIB = 30.0
RELAY_US_PER_MIB = 15.0
D2D_US_PER_MIB = 6.8
MIN_CHUNK_US = 10.0
CARRIER_FILL = 1.4


class _Comm:
    def __init__(self, wsrc):
        self.wsrc = wsrc
        self.n_layers = len(wsrc)
        self.queue = []
        self.riding = {}
        self.n_alone = 0
        self.buf, self.left = {}, {}
        self.part, self.rsib, self.pair = {}, {}, {}
        self.ids = _slab_ids()
        for i in range(self.n_layers):
            for k in W_KEYS:
                self._push_chunks("gather", ("w", i, k), wsrc[i][k].shape, wsrc[i][k].dtype)

    def _push_chunks(self, kind, item, shape, dtype):
        n, k = shape[-2], shape[-1]
        mib = n * k * jnp.dtype(dtype).itemsize / 2 ** 20
        pieces = max(1, int(mib * ICI_US_PER_MIB // MIN_CHUNK_US))
        while n % (16 * pieces):
            pieces -= 1
        step = n // pieces
        self.left[item] = n
        us = mib * (GATHER_US_PER_MIB if kind == "gather" else ICI_US_PER_MIB) / pieces
        for c in range(pieces):
            self.queue.append(dict(kind=kind, item=item, rows=(c * step, (c + 1) * step), rows2=None, us=us))

    @staticmethod
    def _merge(units, u):
        def joined(a, b):
            if a is None or b is None:
                return True, a or b
            return a[1] == b[0], (a[0], b[1])

        for v in units:
            if v["item"] == u["item"] and v["kind"] == u["kind"] and u["kind"] != "sibling":
                ok1, rows = joined(v["rows"], u["rows"])
                ok2, rows2 = joined(v["rows2"], u["rows2"])
                if ok1 and ok2:
                    v.update(rows=rows, rows2=rows2, us=v["us"] + u["us"])
                    return True
        return False

    def _unit_rider(self, u):
        item = u["item"]
        if u["kind"] == "gather":
            src = self.wsrc[item[1]][item[2]] if item[0] == "w" else self.part[item]
            return _gather_rider(src, u["rows"], u["rows2"], self.buf.get(item))
        if u["kind"] == "sibling":
            return _sibling_rider(self.part[item])
        return _chips_rider(self.pair[item], u["rows"], self.buf.get(item))

    def _done(self, u, out):
        item = u["item"]
        if u["kind"] == "sibling":
            self.rsib[item] = out
            self.pair[item] = _pair_sum(f"pair_l{item[1]}_{item[2]}", self.part[item], out, self.ids)
            self._push_chunks("chips", item, self.pair[item].shape, self.pair[item].dtype)
            return
        self.buf[item] = out
        if u["kind"] == "gather":
            if u["rows"] is not None:
                rows = u["rows"]
                mib = (rows[1] - rows[0]) * out.shape[-1] * out.dtype.itemsize / 2 ** 20
                self.queue.insert(0, dict(kind="gather", item=item, rows=None, rows2=rows, us=mib * RELAY_US_PER_MIB))
            if u["rows2"] is not None:
                self.left[item] -= u["rows2"][1] - u["rows2"][0]

    def _send(self, name, units, call):
        outs = call(_compose([self._unit_rider(u) for u in units]))
        for u, o in zip(units, outs):
            self._done(u, o)

    def exchange(self, name, src, budget_us):
        units = self._take(budget_us)
        outs = _run_rider(name, _compose([_direct_rider(src)] + [self._unit_rider(u) for u in units]))
        for u, o in zip(units, outs[1:]):
            self._done(u, o)
        return outs[0]

    def rider(self, name, budget_us=None):
        budget = CARRIER_US.get(name.split("_", 1)[1] if name[0] == "l" and name[1].isdigit() else name, 0) \
            if budget_us is None else budget_us
        units = self._take(budget)
        if not units:
            return None
        self.riding[name] = units
        return _compose([self._unit_rider(u) for u in units])

    def _take(self, budget):
        units, used = [], 0.0
        while self.queue and used + self.queue[0]["us"] <= CARRIER_FILL * budget:
            u = self.queue[0]
            if not self._merge(units, u):
                if any(v["item"] == u["item"] for v in units):
                    break
                units.append(dict(u))
            used += u["us"]
            del self.queue[0]
        return units

    def deliver(self, name, outs):
        for u, o in zip(self.riding.pop(name), outs):
            self._done(u, o)

    def _flush(self, item, kinds):
        hits = [p for p, u in enumerate(self.queue) if u["item"] == item and u["kind"] in kinds]
        if not hits:
            return
        prefix = self.queue[:hits[-1] + 1]
        del self.queue[:hits[-1] + 1]
        units = []
        for u in prefix:
            if not self._merge(units, u):
                units.append(dict(u))
        tag = "_".join(str(t) for t in item) + "_" + kinds[0]
        batches = [[]]
        for u in units:
            if any(v["item"] == u["item"] for v in batches[-1]):
                batches.append([])
            batches[-1].append(u)
        for batch in batches:
            self.n_alone += 1
            self._send(None, batch, functools.partial(_run_rider, f"alone{self.n_alone}_{tag}"))

    def begin(self):
        self._flush(("w", 0, "in_t"), ("gather",))

    def _finish_gather(self, item):
        while self.left[item] > 0:
            assert any(u["item"] == item for u in self.queue), item
            self._flush(item, ("gather",))
        return self.buf[item]

    def weight(self, i, k):
        o = self._finish_gather(("w", i, k))
        return o.reshape(-1, o.shape[-1])

    def grads(self, i, group):
        for k, g in group.items():
            item = ("g", i, k)
            self.part[item] = g.reshape(N_DEV, g.shape[0] // N_DEV, g.shape[1])
            us = g.size // N_DEV * g.dtype.itemsize / 2 ** 20 * D2D_US_PER_MIB
            self.queue.insert(0, dict(kind="sibling", item=item, rows=None, us=us))

    def total(self, k):
        out = self.n_layers
        for i in range(self.n_layers):
            item = ("g", i, k)
            self._flush(item, ("sibling",))
            self._flush(item, ("chips",))
            name = f"sum_l{i}_{k}"
            rider = self.rider(name, budget_us=self.part[item][0].size / 1.06e5) if k in SUMS_TRANSPOSED_LATER else None
            out = _sum5(name, self.part[item], self.rsib[item], self.buf[item], self.ids, i, out, rider=rider)
            if rider is not None:
                out, r_outs = out
                self.deliver(name, r_outs)
        return out

    def small_ready(self, i, dmod, ds):
        parts = [dmod[0], dmod[1]] + [ds[nm] for nm in LAYER_SMALL + ("conv_w",)]
        self.small_shapes = [p.shape for p in parts]
        self.gather_small(f"lat{i}", _pack([dmod[1]]))
        self.gather_small(f"small{i}", _pack(parts))

    def gather_small(self, name, arr):
        item = ("s", name)
        self.part[item] = arr
        waiting, self.queue = self.queue, []
        self._push_chunks("gather", item, arr.shape, arr.dtype)
        self.queue += waiting

    def gathered(self, name):
        return self._finish_gather(("s", name))


def _row_tile(n, pref, mult):
    best = None
    t = mult
    while t <= min(n, pref):
        if n % t == 0:
            best = t
        t += mult
    return best if best is not None else n


def _sum8(name, slabs):
    _, n, k = slabs.shape
    tr = _row_tile(n, 128, 16)

    def body(s_ref, o_ref):
        acc = s_ref[0].astype(F32)
        for j in range(1, N_DEV):
            acc = acc + s_ref[j].astype(F32)
        o_ref[...] = acc

    return pl.pallas_call(
        body, name=name, grid=(n // tr,),
        in_specs=[pl.BlockSpec((N_DEV, tr, k), lambda i: (0, i, 0))],
        out_specs=pl.BlockSpec((tr, k), lambda i: (i, 0)),
        out_shape=jax.ShapeDtypeStruct((n, k), F32),
        compiler_params=_cparams("parallel"),
    )(slabs)


def _adamw(name, w, g, m, v, rider=None):
    n, k = w.shape[-2:]
    tr = _row_tile(n, 256, 8)

    def body(w_ref, g_ref, m_ref, v_ref, d_ref, m2_ref, v2_ref):
        gv = g_ref[...]
        m2 = ADAM_B1 * m_ref[...] + (1.0 - ADAM_B1) * gv
        v2 = ADAM_B2 * v_ref[...] + (1.0 - ADAM_B2) * jnp.square(gv)
        m_hat = m2 / (1.0 - ADAM_B1 ** ADAM_STEP)
        v_hat = v2 / (1.0 - ADAM_B2 ** ADAM_STEP)
        d_ref[...] = -ADAM_LR * (m_hat / (jnp.sqrt(v_hat) + ADAM_EPS) + ADAM_WD * w_ref[...])
        m2_ref[...] = m2
        v2_ref[...] = v2

    if w.ndim == 2:
        grid, spec = (n // tr,), pl.BlockSpec((tr, k), lambda i: (i, 0))
    else:
        grid, spec = (w.shape[0], n // tr), pl.BlockSpec((None, tr, k), lambda l, i: (l, i, 0))
    outs, r_outs = _pcall(name, body, grid, [spec] * 4, [spec] * 3, [jax.ShapeDtypeStruct(w.shape, F32)] * 3,
                          (w, g, m, v), ("parallel",) * len(grid), rider=rider)
    return outs if rider is None else (outs, r_outs)


def _pack(arrs):
    flat = jnp.concatenate([a.reshape(-1).astype(F32) for a in arrs])
    pad = (-flat.shape[0]) % 2048
    if pad:
        flat = jnp.concatenate([flat, jnp.zeros((pad,), F32)])
    return flat.reshape(-1, 128)


def _unpack(packed, shapes):
    flat = packed.reshape(-1)
    out, off = [], 0
    for shp in shapes:
        size = math.prod(shp)
        out.append(flat[off:off + size].reshape(shp))
        off += size
    return out


WEIGHT_NAMES = ("c_ctx", "w_ada", "b_ada", "w_in", "q_norm_g", "k_norm_g", "pool_w", "pool_scale", "sgu_ln_g",
                "sgu_ln_b", "sgu_w", "sgu_b", "conv_w", "w_br_attn", "w_br_pool", "w_br_sgu", "w_br_conv", "w_gate",
                "b_gate", "w_o", "ln1_g", "ln1_b", "w_ff_gate", "w_ff_up", "w_ff_down", "ln2_g", "ln2_b")
COL_SHARDED = {"w_in": "in_t", "w_gate": "gate_t", "w_ff_gate": "ffg_t", "w_ff_up": "ffu_t",
               "w_br_attn": "br0", "w_br_pool": "br1", "w_br_sgu": "br2", "w_br_conv": "br3"}
ROW_SHARDED = {"w_o": "o", "w_ff_down": "ffd"}
LAYER_SMALL = ("q_norm_g", "k_norm_g", "pool_w", "pool_scale", "sgu_ln_g", "sgu_ln_b", "sgu_w", "sgu_b", "b_gate",
               "ln1_g", "ln1_b", "ln2_g", "ln2_b")
SMALL_ORDER = ("c_ctx", "b_ada") + LAYER_SMALL + ("conv_w",)


def _train_step(a):
    n_layers, d = a["w_in"].shape[0], a["x"].shape[-1]
    rc = a["ctx"].shape[1]
    alpha = (2 * n_layers) ** 0.25
    mx, my, mc = [lax.axis_index(ax) for ax in MESH_AXES]
    me = 4 * mx + 2 * my + mc
    ada_w = a["w_ada"].shape[-1]
    cw_loc = a["conv_w"].shape[-1]

    comm = _Comm([{**{key: jnp.swapaxes(a[nm], 1, 2)[i].astype(BF16) for nm, key in COL_SHARDED.items()},
                   **{key: a[nm][i].astype(BF16) for nm, key in ROW_SHARDED.items()}} for i in range(n_layers)])

    def carried(name, *args, budget_us, **kw):
        rider = comm.rider(name, budget_us=budget_us)
        if rider is None:
            return _mm(name, *args, **kw)
        res, r_outs = _mm(name, *args, rider=rider, **kw)
        comm.deliver(name, r_outs)
        return res

    n_c, n_cw = d, n_layers * 3 * cw_loc
    got = comm.exchange("gather_cond", _pack([a["c"], a["conv_w"]]), budget_us=12).reshape(N_DEV, -1)
    c_all = got[:, :n_c]
    conv_w = got[:, n_c:n_c + n_cw].reshape(N_DEV, n_layers, 3, cw_loc).transpose(1, 2, 0, 3).reshape(n_layers, 3, -1)
    cond = jnp.concatenate([c_all, a["c_ctx"][None], jnp.zeros((16 - N_DEV - 1, d), F32)], axis=0)
    sil, sil_vjp = jax.vjp(jax.nn.silu, cond)
    sil = sil.astype(BF16)

    mod_cols = jnp.concatenate([carried(f"ada{i}", sil, a["w_ada"][i], "nn", F32, budget_us=20)
                                for i in range(n_layers)], axis=0)
    got = comm.exchange("gather_mod", mod_cols, budget_us=22)
    mod_all = got.reshape(N_DEV, n_layers, 16, ada_w).transpose(1, 2, 0, 3).reshape(n_layers, 16, -1)
    mod_all = mod_all + a["b_ada"][:, None, :]
    mod = jnp.stack([mod_all[:, N_DEV], lax.dynamic_index_in_dim(mod_all, me, axis=1, keepdims=False)], axis=1)
    comm.begin()
    sp = [{nm: a[nm][i] for nm in LAYER_SMALL} for i in range(n_layers)]
    for i in range(n_layers):
        sp[i]["conv_w"] = conv_w[i]

    xin = jnp.concatenate([a["ctx"][0], a["x"][0]], axis=0)
    loss_l, grad_x = _local_step(xin, a["loss_target"][0], mod, comm, sp, rc, alpha)
    loss = lax.psum(loss_l, MESH_AXES)
    grads = {}

    def transposed_home(nm):
        return nm in COL_SHARDED and a[nm].shape[-1] % 128 != 0

    delta, new_m, new_v = {}, {}, {}

    def adamw(nm):
        name = f"adamw_{nm}"
        there = transposed_home(nm)
        view = (lambda t: jnp.swapaxes(t, 1, 2)) if there else (lambda t: t)
        if nm in COL_SHARDED:
            g = comm.total(COL_SHARDED[nm])
            grads[nm] = jnp.swapaxes(g, 1, 2)
            g = g if there else grads[nm]
        elif nm in ROW_SHARDED:
            g = grads[nm] = comm.total(ROW_SHARDED[nm])
        else:
            g = grads[nm]
        res = _adamw(name, view(a[nm]), g, view(a["m_" + nm]), view(a["v_" + nm]))
        delta[nm], new_m[nm], new_v[nm] = [view(t) for t in res]

    for nm in ("w_ff_down", "w_ff_gate", "w_ff_up", "w_o", "w_br_attn", "w_br_pool", "w_br_sgu", "w_br_conv"):
        adamw(nm)

    tots = [_unpack(_sum8(f"sum_small{i}", comm.gathered(f"small{i}")), comm.small_shapes) for i in range(n_layers)]
    dmod_c, dmod_lat_sum = jnp.stack([t[0] for t in tots]), jnp.stack([t[1] for t in tots])
    for j, nm in enumerate(LAYER_SMALL + ("conv_w",)):
        grads[nm] = jnp.stack([t[2 + j] for t in tots])
    grads["conv_w"] = lax.dynamic_slice_in_dim(grads["conv_w"], me * cw_loc, cw_loc, axis=2)
    grads["b_ada"] = dmod_c + dmod_lat_sum
    dmod_lat_all = jnp.stack([comm.gathered(f"lat{i}").reshape(N_DEV, -1)[:, :6 * d] for i in range(n_layers)])
    dm_rows = jnp.concatenate([dmod_lat_all, dmod_c[:, None, :],
                               jnp.zeros((n_layers, 16 - N_DEV - 1, 6 * d), F32)], axis=1)
    dm_cols = lax.dynamic_slice_in_dim(dm_rows, me * ada_w, ada_w, axis=2).astype(BF16)
    grads["w_ada"] = jnp.stack([carried(f"dWada{i}", sil, dm_cols[i], "tn", F32, budget_us=20)
                                for i in range(n_layers)])
    dsil = None
    for i in range(n_layers):
        dsil = carried(f"dsil{i}", dm_cols[i], a["w_ada"][i], "nt", F32, acc=dsil, budget_us=10)
    dsil = _sum8("sum_dsil", comm.exchange("gather_dsil", dsil, budget_us=17))
    grads["c_ctx"] = sil_vjp(dsil)[0][N_DEV]

    for nm in ("w_ada", "w_in", "w_gate"):
        adamw(nm)
    shapes = [a[nm].shape for nm in SMALL_ORDER]
    res = _adamw("adamw_small", _pack([a[nm] for nm in SMALL_ORDER]), _pack([grads[nm] for nm in SMALL_ORDER]),
                 _pack([a["m_" + nm] for nm in SMALL_ORDER]), _pack([a["v_" + nm] for nm in SMALL_ORDER]))
    for tree, packed in zip((delta, new_m, new_v), res):
        for nm, t in zip(SMALL_ORDER, _unpack(packed, shapes)):
            tree[nm] = t
    return (loss, grad_x[None], *[grads[nm] for nm in WEIGHT_NAMES], *[delta[nm] for nm in WEIGHT_NAMES],
            *[new_m[nm] for nm in WEIGHT_NAMES], *[new_v[nm] for nm in WEIGHT_NAMES])


def kernel(x, c, ctx, c_ctx, w_ada, b_ada, w_in, q_norm_g, k_norm_g, pool_w, pool_scale, sgu_ln_g, sgu_ln_b, sgu_w, sgu_b, conv_w, w_br_attn, w_br_pool, w_br_sgu, w_br_conv, w_gate, b_gate, w_o, ln1_g, ln1_b, w_ff_gate, w_ff_up, w_ff_down, ln2_g, ln2_b, loss_target, m_c_ctx, m_w_ada, m_b_ada, m_w_in, m_q_norm_g, m_k_norm_g, m_pool_w, m_pool_scale, m_sgu_ln_g, m_sgu_ln_b, m_sgu_w, m_sgu_b, m_conv_w, m_w_br_attn, m_w_br_pool, m_w_br_sgu, m_w_br_conv, m_w_gate, m_b_gate, m_w_o, m_ln1_g, m_ln1_b, m_w_ff_gate, m_w_ff_up, m_w_ff_down, m_ln2_g, m_ln2_b, v_c_ctx, v_w_ada, v_b_ada, v_w_in, v_q_norm_g, v_k_norm_g, v_pool_w, v_pool_scale, v_sgu_ln_g, v_sgu_ln_b, v_sgu_w, v_sgu_b, v_conv_w, v_w_br_attn, v_w_br_pool, v_w_br_sgu, v_w_br_conv, v_w_gate, v_b_gate, v_w_o, v_ln1_g, v_ln1_b, v_w_ff_gate, v_w_ff_up, v_w_ff_down, v_ln2_g, v_ln2_b):
    names = list(WEIGHT_NAMES)
    args = dict(zip(
        ["x", "c", "ctx"] + names + ["loss_target"] + ["m_" + n for n in names] + ["v_" + n for n in names],
        (x, c, ctx, c_ctx, w_ada, b_ada, w_in, q_norm_g, k_norm_g, pool_w, pool_scale, sgu_ln_g, sgu_ln_b, sgu_w, sgu_b, conv_w, w_br_attn, w_br_pool, w_br_sgu, w_br_conv, w_gate, b_gate, w_o, ln1_g, ln1_b, w_ff_gate, w_ff_up, w_ff_down, ln2_g, ln2_b, loss_target, m_c_ctx, m_w_ada, m_b_ada, m_w_in, m_q_norm_g, m_k_norm_g, m_pool_w, m_pool_scale, m_sgu_ln_g, m_sgu_ln_b, m_sgu_w, m_sgu_b, m_conv_w, m_w_br_attn, m_w_br_pool, m_w_br_sgu, m_w_br_conv, m_w_gate, m_b_gate, m_w_o, m_ln1_g, m_ln1_b, m_w_ff_gate, m_w_ff_up, m_w_ff_down, m_ln2_g, m_ln2_b, v_c_ctx, v_w_ada, v_b_ada, v_w_in, v_q_norm_g, v_k_norm_g, v_pool_w, v_pool_scale, v_sgu_ln_g, v_sgu_ln_b, v_sgu_w, v_sgu_b, v_conv_w, v_w_br_attn, v_w_br_pool, v_w_br_sgu, v_w_br_conv, v_w_gate, v_b_gate, v_w_o, v_ln1_g, v_ln1_b, v_w_ff_gate, v_w_ff_up, v_w_ff_down, v_ln2_g, v_ln2_b)))
    return _train_step(args)
```

```python
import functools
import math

import jax
import jax.numpy as jnp
from jax import lax
from jax.experimental import pallas as pl
from jax.experimental.pallas import tpu as pltpu

F32 = jnp.float32
BF16 = jnp.bfloat16

N_DEV = 8
MESH_AXES = ("x", "y", "c")
V7X_VMEM_LIMIT_BYTES = 56 * 1024 * 1024

GRID_W = 64
HEAD_DIM = 128
N_HEADS = 8
N_KV_HEADS = 2
KV_GROUP = N_HEADS // N_KV_HEADS
Q_W = N_HEADS * HEAD_DIM
KV_W = N_KV_HEADS * HEAD_DIM
ROPE_THETA = 10000.0
ROPE_AXIS_DIM = HEAD_DIM // 2
POOL_WINDOWS = (2, 4, 8, 16)
GC = 128
N_GROUPS = 4
BR_W = N_GROUPS * GC
SGU_CHUNK = 128
N_BRANCH = 4
LN_EPS = 1e-5
RMS_EPS = 1e-6
OFF_K = Q_W
OFF_V = OFF_K + KV_W
OFF_POOL = OFF_V + KV_W
OFF_U = OFF_POOL + BR_W
OFF_VG = OFF_U + BR_W
OFF_CB = OFF_VG + BR_W
OFF_CC = OFF_CB + BR_W
OFF_CX = OFF_CC + BR_W
IN_W = OFF_CX + BR_W
ATT_SCALE = HEAD_DIM ** -0.5

ADAM_LR = 0.001
ADAM_B1 = 0.9
ADAM_B2 = 0.999
ADAM_EPS = 1e-08
ADAM_WD = 0.01
ADAM_STEP = 10

_NT = (((1,), (1,)), ((), ()))
_NN = (((1,), (0,)), ((), ()))
_TN = (((0,), (0,)), ((), ()))
_DIMS = {"nt": _NT, "nn": _NN, "tn": _TN}


def _cparams(*sem):
    return pltpu.CompilerParams(dimension_semantics=sem, vmem_limit_bytes=V7X_VMEM_LIMIT_BYTES)


def _tile(dim, pref):
    best = None
    t = 128
    while t <= min(dim, pref):
        if dim % t == 0:
            best = t
        t += 128
    return best if best is not None else dim


def _dot(a, b, dims):
    return lax.dot_general(a.astype(BF16), b.astype(BF16), dims, preferred_element_type=F32)


class _Rider:
    def __init__(self, inputs, out_shapes, scratch, start, finish, aliases=None):
        self.inputs, self.out_shapes, self.scratch = list(inputs), list(out_shapes), list(scratch)
        self.start, self.finish = start, finish
        self.aliases = dict(aliases or {})


def _compose(riders):
    inputs, outs, scratch, aliases, spans = [], [], [], {}, []
    for rd in riders:
        i0, o0, s0 = len(inputs), len(outs), len(scratch)
        aliases.update({i0 + p: o0 + q for p, q in rd.aliases.items()})
        inputs += rd.inputs
        outs += rd.out_shapes
        scratch += rd.scratch
        spans.append((slice(i0, len(inputs)), slice(o0, len(outs)), slice(s0, len(scratch))))

    def start(ins, os, sems):
        for rd, (si, so, ss) in zip(riders, spans):
            rd.start(ins[si], os[so], sems[ss])

    def finish(ins, os, sems):
        for rd, (si, so, ss) in zip(riders, spans):
            rd.finish(ins[si], os[so], sems[ss])

    return _Rider(inputs, outs, scratch, start, finish, aliases)


def _pcall(name, body, grid, in_specs, out_specs, out_shape, args, sem, scratch=(), rider=None, prefetch=None,
           aliases=None):
    in_specs, out_specs, out_shape, scratch = list(in_specs), list(out_specs), list(out_shape), list(scratch)
    n_pre = 0 if prefetch is None else 1
    n_in, n_out, n_scr = len(in_specs), len(out_specs), len(scratch)
    r_in, r_out = (len(rider.inputs), len(rider.out_shapes)) if rider is not None else (0, 0)
    any_spec = pl.BlockSpec(memory_space=pl.ANY)
    io_aliases = {n_pre + p: q for p, q in (aliases or {}).items()}
    if rider is not None:
        io_aliases.update({n_pre + n_in + p: n_out + q for p, q in rider.aliases.items()})
        in_specs, out_specs = in_specs + [any_spec] * r_in, out_specs + [any_spec] * r_out
        out_shape, scratch = out_shape + rider.out_shapes, scratch + rider.scratch
        args, sem = (*args, *rider.inputs), ["arbitrary"] * len(grid)

    def wrapped(*refs):
        pre, refs = refs[:n_pre], refs[n_pre:]
        if rider is None:
            return body(*pre, *refs)
        ins, refs = refs[:n_in], refs[n_in:]
        r_ins, refs = refs[:r_in], refs[r_in:]
        outs, refs = refs[:n_out], refs[n_out:]
        r_outs, refs = refs[:r_out], refs[r_out:]
        scr, r_scr = refs[:n_scr], refs[n_scr:]
        first = functools.reduce(jnp.logical_and, [pl.program_id(ax) == 0 for ax in range(len(grid))])
        last = functools.reduce(jnp.logical_and, [pl.program_id(ax) == grid[ax] - 1 for ax in range(len(grid))])

        @pl.when(first)
        def _():
            rider.start(r_ins, r_outs, r_scr)

        body(*pre, *ins, *outs, *scr)

        @pl.when(last)
        def _():
            rider.finish(r_ins, r_outs, r_scr)

    if prefetch is None:
        res = pl.pallas_call(wrapped, name=name, grid=grid, in_specs=in_specs, out_specs=out_specs, out_shape=out_shape,
                             scratch_shapes=scratch, input_output_aliases=io_aliases,
                             compiler_params=_cparams(*sem))(*args)
    else:
        grid_spec = pltpu.PrefetchScalarGridSpec(num_scalar_prefetch=1, grid=grid, in_specs=in_specs,
                                                 out_specs=out_specs, scratch_shapes=scratch)
        res = pl.pallas_call(wrapped, name=name, grid_spec=grid_spec, out_shape=out_shape,
                             input_output_aliases=io_aliases, compiler_params=_cparams(*sem))(prefetch, *args)
    return list(res[:n_out]), list(res[n_out:])


V7X_MM_VMEM_BUDGET = 40 * 1024 * 1024


def _mm_plan(form, m, n, k, a_size, b_size, o_size, has_acc):
    tn = n if (form == "tn" and n <= 2048) else _tile(n, 512)
    rows = sorted({m} | {t for t in range(128, m, 128) if m % t == 0}, reverse=True)
    for tk, min_tm in ((k, 384), (k if k <= 2816 else _tile(k, 2816), 0)):
        nk = k // tk
        for tm in rows:
            need = 2 * (tm * tk * a_size + tn * tk * b_size + tm * tn * o_size) + tm * tn * 4 * (2 if nk > 1 else 1)
            need += 2 * tm * tn * 4 if has_acc else 0
            if need <= V7X_MM_VMEM_BUDGET and tm >= min(min_tm, m):
                return tm, tn, tk
    return _tile(m, 128), tn, tk


def _mm(name, a, b, form, out_dtype, acc=None, rider=None):
    if form == "nt":
        (m, k), (n, k2) = a.shape, b.shape
    elif form == "nn":
        (m, k), (k2, n) = a.shape, b.shape
    else:
        (k, m), (k2, n) = a.shape, b.shape
    assert k == k2, (name, a.shape, b.shape)
    has_acc = acc is not None
    tm, tn, tk = _mm_plan(form, m, n, k, a.dtype.itemsize, b.dtype.itemsize, jnp.dtype(out_dtype).itemsize, has_acc)
    nk = k // tk
    a_spec = {"nt": pl.BlockSpec((tm, tk), lambda i, j, kk: (i, kk)),
              "nn": pl.BlockSpec((tm, tk), lambda i, j, kk: (i, kk)),
              "tn": pl.BlockSpec((tk, tm), lambda i, j, kk: (kk, i))}[form]
    b_spec = {"nt": pl.BlockSpec((tn, tk), lambda i, j, kk: (j, kk)),
              "nn": pl.BlockSpec((tk, tn), lambda i, j, kk: (kk, j)),
              "tn": pl.BlockSpec((tk, tn), lambda i, j, kk: (kk, j))}[form]
    o_spec = pl.BlockSpec((tm, tn), lambda i, j, kk: (i, j))
    dims = _DIMS[form]

    def body(*refs):
        a_ref, b_ref = refs[0], refs[1]
        c_ref = refs[2] if has_acc else None
        o_ref = refs[3] if has_acc else refs[2]

        def finish(r):
            if has_acc:
                r = r + c_ref[...]
            o_ref[...] = r.astype(o_ref.dtype)

        if nk == 1:
            finish(_dot(a_ref[...], b_ref[...], dims))
            return
        acc_ref = refs[-1]
        kk = pl.program_id(2)

        @pl.when(kk == 0)
        def _():
            acc_ref[...] = _dot(a_ref[...], b_ref[...], dims)

        @pl.when(kk > 0)
        def _():
            acc_ref[...] += _dot(a_ref[...], b_ref[...], dims)

        @pl.when(kk == nk - 1)
        def _():
            finish(acc_ref[...])

    in_specs = [a_spec, b_spec] + ([o_spec] if has_acc else [])
    args = (a, b) + ((acc,) if has_acc else ())
    outs, r_outs = _pcall(name, body, (m // tm, n // tn, nk), in_specs, [o_spec],
                          [jax.ShapeDtypeStruct((m, n), out_dtype)], args, ("parallel", "parallel", "arbitrary"),
                          scratch=[pltpu.VMEM((tm, tn), F32)] if nk > 1 else [], rider=rider)
    return outs[0] if rider is None else (outs[0], r_outs)


def _mm_fused(name, a, bs, epilogue, tile_ins, out_dtypes, rider=None):
    (m, k), (n, _) = a.shape, bs[0].shape
    tn = _tile(n, 512)
    tm = None
    for cand in sorted({m} | {t for t in range(128, m, 128) if m % t == 0}, reverse=True):
        per_tile = sum(t.dtype.itemsize for t in tile_ins) + sum(jnp.dtype(d).itemsize for d in out_dtypes)
        need = 2 * (cand * k * a.dtype.itemsize + len(bs) * tn * k * bs[0].dtype.itemsize + cand * tn * per_tile)
        need += (len(bs) + 2) * cand * tn * 4
        if need <= V7X_MM_VMEM_BUDGET:
            tm = cand
            break
    assert tm is not None, name
    n_b, n_t = len(bs), len(tile_ins)
    tile = pl.BlockSpec((tm, tn), lambda i, j: (i, j))

    def body(a_ref, *refs):
        prods = [_dot(a_ref[...], r[...], _NT) for r in refs[:n_b]]
        outs = epilogue(prods, [r[...].astype(F32) for r in refs[n_b:n_b + n_t]])
        for r, o in zip(refs[n_b + n_t:], outs):
            r[...] = o.astype(r.dtype)

    outs, r_outs = _pcall(name, body, (m // tm, n // tn),
                          [pl.BlockSpec((tm, k), lambda i, j: (i, 0))] + [pl.BlockSpec((tn, k), lambda i, j: (j, 0))] * n_b
                          + [tile] * n_t, [tile] * len(out_dtypes),
                          [jax.ShapeDtypeStruct((m, n), d) for d in out_dtypes], (a, *bs, *tile_ins),
                          ("parallel", "parallel"), rider=rider)
    return outs if rider is None else (outs, r_outs)


def _rows(name, fn, n_rows, tm, nbc, row_ins, type_ins, row_outs, acc_outs, rider=None):
    n_ri, n_ti, n_ro, n_ao = len(row_ins), len(type_ins), len(row_outs), len(acc_outs)

    def row_map(i, cb, roff):
        return (jnp.maximum(i - roff, 0), cb)

    def type_map(i):
        return (jnp.where(i >= nbc, 1, 0), 0, 0)

    in_specs, args = [], []
    for arr, cb, width, roff in row_ins:
        in_specs.append(pl.BlockSpec((tm, width), functools.partial(row_map, cb=cb, roff=roff)))
        args.append(arr)
    def shared_map(i):
        return (0, 0, 0)

    for arr in type_ins:
        in_specs.append(pl.BlockSpec((None, 1, arr.shape[-1]), type_map if arr.shape[0] == 2 else shared_map))
        args.append(arr)
    out_shape, out_specs = [], []
    for total, width, dtype, roff in row_outs:
        out_shape.append(jax.ShapeDtypeStruct((total, width), dtype))
        out_specs.append(pl.BlockSpec((tm, width), functools.partial(row_map, cb=0, roff=roff)))
    acc_shared = [isinstance(w, tuple) for w in acc_outs]
    for width in acc_outs:
        if isinstance(width, tuple):
            out_shape.append(jax.ShapeDtypeStruct((1, 1, width[0]), F32))
            out_specs.append(pl.BlockSpec((None, 1, width[0]), shared_map))
        else:
            out_shape.append(jax.ShapeDtypeStruct((2, 1, width), F32))
            out_specs.append(pl.BlockSpec((None, 1, width), type_map))
    n_in = n_ri + n_ti

    def body(*refs):
        i = pl.program_id(0)
        outs = fn(*[r[...].astype(F32) for r in refs[:n_in]])
        if not isinstance(outs, (tuple, list)):
            outs = (outs,)
        assert len(outs) == n_ro + n_ao, (name, len(outs))
        for r, o in zip(refs[n_in:n_in + n_ro], outs[:n_ro]):
            r[...] = o.astype(r.dtype)
        if n_ao:
            for r, o, shared in zip(refs[n_in + n_ro:], outs[n_ro:], acc_shared):
                first = i == 0 if shared else jnp.logical_or(i == 0, i == nbc)
                o = jnp.broadcast_to(o.astype(F32), r.shape)

                @pl.when(first)
                def _(r=r, o=o):
                    r[...] = o

                @pl.when(jnp.logical_not(first))
                def _(r=r, o=o):
                    r[...] += o

    outs, r_outs = _pcall(name, body, (n_rows // tm,), in_specs, out_specs, out_shape, args, ("arbitrary",),
                          rider=rider)
    return outs if rider is None else (outs, r_outs)


def _vjp_fn(f, n_row, n_cot, keep=None):
    def g(*args):
        prim = args[:n_row] + args[n_row + n_cot:]
        cots = args[n_row:n_row + n_cot]
        out, vjp = jax.vjp(f, *prim)
        grads = vjp(tuple(cots) if isinstance(out, (tuple, list)) else cots[0])
        return grads if keep is None else tuple(grads[j] for j in keep)
    return g


def _typed(v):
    return v.reshape(1, 1, -1)


def _ln(x, g, b):
    mu = jnp.mean(x, axis=-1, keepdims=True)
    var = jnp.mean(jnp.square(x - mu), axis=-1, keepdims=True)
    return (x - mu) * lax.rsqrt(var + LN_EPS) * g + b


def _f_mod(x, sc, sh):
    return x * (1.0 + sc) + sh


def _make_f_ln(alpha, with_mod):
    def f(x, o, gate, lng, lnb, *mod):
        xn = _ln(alpha * x + gate * o, lng, lnb)
        if with_mod:
            sc, sh = mod
            return xn, xn * (1.0 + sc) + sh
        return xn
    return f


@jax.custom_vjp
def _rot(y):
    lane = lax.broadcasted_iota(jnp.int32, y.shape, 1)
    return jnp.where(lane % 64 < 32, pltpu.roll(y, 96, axis=1), pltpu.roll(y, 32, axis=1))


_rot.defvjp(lambda y: (_rot(y), None), lambda _, g: (_rot(g),))


def _f_prep(p, cos, sin, qg, kg):
    def head(xh, g):
        ms = jnp.mean(jnp.square(xh), axis=-1, keepdims=True)
        y = xh * lax.rsqrt(ms + RMS_EPS) * g
        return y * cos + _rot(y) * sin
    q = jnp.concatenate([head(p[:, h * HEAD_DIM:(h + 1) * HEAD_DIM], qg) for h in range(N_HEADS)], axis=1)
    k = jnp.concatenate([head(p[:, OFF_K + h * HEAD_DIM:OFF_K + (h + 1) * HEAD_DIM], kg)
                         for h in range(N_KV_HEADS)], axis=1)
    return q, k, p[:, OFF_V:OFF_POOL]


def _f_gate(g, t0, t1, t2, t3, b):
    d = t0.shape[-1]
    ts = (t0, t1, t2, t3)
    terms = [jax.nn.sigmoid(g[:, k * d:(k + 1) * d] + b[:, k * d:(k + 1) * d]) * ts[k] for k in range(N_BRANCH)]
    return terms[0] + terms[1] + terms[2] + terms[3]


def _f_swiglu(a, b):
    return jax.nn.silu(a) * b


def _softmax(raw):
    e = jnp.exp2((raw - jnp.max(raw, axis=-1, keepdims=True)) * (ATT_SCALE * math.log2(math.e)))
    return e / jnp.sum(e, axis=-1, keepdims=True)


def _attn_fwd(name, q, k, v, rc, ctx_queries, tq=256, rider=None):
    r = q.shape[0]
    assert rc % tq == 0 and r % tq == 0
    nqc = rc // tq

    def body(q_ref, k_ref, v_ref, o_ref, p_ref):
        qi = pl.program_id(1)

        def attend(nk):
            p = _softmax(_dot(q_ref[...], k_ref[0:nk, :], _NT)).astype(BF16)
            p_ref[:, 0:nk] = p
            o_ref[...] = _dot(p, v_ref[0:nk, :], _NN).astype(o_ref.dtype)

        @pl.when(qi < nqc)
        def _():
            if ctx_queries:
                attend(rc)
            else:
                o_ref[...] = jnp.zeros_like(o_ref)

        @pl.when(qi >= nqc)
        def _():
            attend(r)

    outs, r_outs = _pcall(
        name, body, (N_HEADS, r // tq),
        [pl.BlockSpec((tq, HEAD_DIM), lambda h, i: (i, h)),
         pl.BlockSpec((r, HEAD_DIM), lambda h, i: (0, h // KV_GROUP)),
         pl.BlockSpec((r, HEAD_DIM), lambda h, i: (0, h // KV_GROUP))],
        [pl.BlockSpec((tq, HEAD_DIM), lambda h, i: (i, h)), pl.BlockSpec((None, tq, r), lambda h, i: (h, i, 0))],
        [jax.ShapeDtypeStruct((r, Q_W), BF16), jax.ShapeDtypeStruct((N_HEADS, r, r), BF16)], (q, k, v),
        ("parallel", "parallel"), rider=rider)
    return outs if rider is None else (outs, r_outs)


def _attn_bwd(name, q, k, v, pw, do, rc, ctx_queries, tq=256, rider=None):
    r = q.shape[0]
    nqc = rc // tq

    def body(q_ref, k_ref, v_ref, p_ref, do_ref, dq_ref, dk_ref, dv_ref):
        g, qi = pl.program_id(1), pl.program_id(2)

        @pl.when(jnp.logical_and(g == 0, qi == 0))
        def _():
            dk_ref[...] = jnp.zeros_like(dk_ref)
            dv_ref[...] = jnp.zeros_like(dv_ref)

        def grad(nk):
            qb, kb, vb = q_ref[...], k_ref[0:nk, :], v_ref[0:nk, :]
            dob = do_ref[...].astype(BF16)
            p = p_ref[:, 0:nk].astype(F32)
            dv_ref[0:nk, :] += _dot(p, dob, _TN)
            dp = _dot(dob, vb, _NT)
            ds = p * (dp - jnp.sum(dp * p, axis=-1, keepdims=True))
            dq_ref[...] = _dot(ds, kb, _NN) * ATT_SCALE
            dk_ref[0:nk, :] += _dot(ds, qb, _TN) * ATT_SCALE

        @pl.when(qi < nqc)
        def _():
            if ctx_queries:
                grad(rc)
            else:
                dq_ref[...] = jnp.zeros_like(dq_ref)

        @pl.when(qi >= nqc)
        def _():
            grad(r)

    def qmap(kv, g, i):
        return (i, kv * KV_GROUP + g)

    def kvmap(kv, g, i):
        return (0, kv)

    outs, r_outs = _pcall(
        name, body, (N_KV_HEADS, KV_GROUP, r // tq),
        [pl.BlockSpec((tq, HEAD_DIM), qmap), pl.BlockSpec((r, HEAD_DIM), kvmap), pl.BlockSpec((r, HEAD_DIM), kvmap),
         pl.BlockSpec((None, tq, r), lambda kv, g, i: (kv * KV_GROUP + g, i, 0)), pl.BlockSpec((tq, HEAD_DIM), qmap)],
        [pl.BlockSpec((tq, HEAD_DIM), qmap), pl.BlockSpec((r, HEAD_DIM), kvmap), pl.BlockSpec((r, HEAD_DIM), kvmap)],
        [jax.ShapeDtypeStruct((r, Q_W), F32), jax.ShapeDtypeStruct((r, KV_W), F32),
         jax.ShapeDtypeStruct((r, KV_W), F32)],
        (q, k, v, pw, do), ("arbitrary", "arbitrary", "arbitrary"), rider=rider)
    return outs if rider is None else (outs, r_outs)


def _segments(shape, rc):
    t = lax.broadcasted_iota(jnp.int32, shape, 0)
    lo = jnp.where(t < rc, 0, rc)
    hi = jnp.where(t < rc, rc, shape[0])
    return t, lo, hi


def _shifted(x, o, t, lo, hi):
    n = x.shape[0]
    sh = pltpu.roll(x, (-o) % n, axis=0)
    return jnp.where(jnp.logical_and(t + o >= lo, t + o < hi), sh, 0.0)


def _winsum(x, left, right, t, lo, hi):
    acc = x
    for o in range(-left, right + 1):
        if o != 0:
            acc = acc + _shifted(x, o, t, lo, hi)
    return acc


def _pool_parts(z, g, t, lo, hi):
    w = POOL_WINDOWS[g]
    left = w // 2
    right = w - 1 - left
    count = (jnp.minimum(t + right + 1, hi) - jnp.maximum(t - left, lo)).astype(F32)
    return _winsum(z, left, right, t, lo, hi) / count - z, count, left, right


def _pool_fwd(name, p, pool_w, pool_scale, rc):
    r = p.shape[0]

    def body(z_ref, w_ref, s_ref, y_ref):
        t, lo, hi = _segments((r, GC), rc)
        for g in range(N_GROUPS):
            cols = slice(g * GC, (g + 1) * GC)
            d, _, _, _ = _pool_parts(z_ref[:, cols], g, t, lo, hi)
            y_ref[:, cols] = (_dot(d, w_ref[g], _NN) * s_ref[:, cols]).astype(y_ref.dtype)

    return pl.pallas_call(
        body, name=name, grid=(1,),
        in_specs=[pl.BlockSpec((r, BR_W), lambda i: (0, OFF_POOL // BR_W)),
                  pl.BlockSpec((N_GROUPS, GC, GC), lambda i: (0, 0, 0)),
                  pl.BlockSpec((1, BR_W), lambda i: (0, 0))],
        out_specs=pl.BlockSpec((r, BR_W), lambda i: (0, 0)),
        out_shape=jax.ShapeDtypeStruct((r, BR_W), BF16),
        compiler_params=_cparams("arbitrary"),
    )(p, pool_w, pool_scale.reshape(1, BR_W))


def _pool_bwd(name, p, pool_w, pool_scale, dy, rc):
    r = p.shape[0]

    def body(z_ref, w_ref, s_ref, dy_ref, dz_ref, dw_ref, ds_ref):
        t, lo, hi = _segments((r, GC), rc)
        for g in range(N_GROUPS):
            cols = slice(g * GC, (g + 1) * GC)
            d, count, left, right = _pool_parts(z_ref[:, cols], g, t, lo, hi)
            dyg = dy_ref[:, cols]
            ds_ref[:, cols] = jnp.sum(dyg * _dot(d, w_ref[g], _NN), axis=0, keepdims=True)
            dlin = dyg * s_ref[:, cols]
            dw_ref[g] = _dot(d, dlin, _TN)
            dd = _dot(dlin, w_ref[g], _NT)
            dz_ref[:, cols] = (_winsum(dd / count, right, left, t, lo, hi) - dd).astype(dz_ref.dtype)

    return pl.pallas_call(
        body, name=name, grid=(1,),
        in_specs=[pl.BlockSpec((r, BR_W), lambda i: (0, OFF_POOL // BR_W)),
                  pl.BlockSpec((N_GROUPS, GC, GC), lambda i: (0, 0, 0)),
                  pl.BlockSpec((1, BR_W), lambda i: (0, 0)),
                  pl.BlockSpec((r, BR_W), lambda i: (0, 0))],
        out_specs=[pl.BlockSpec((r, BR_W), lambda i: (0, 0)),
                   pl.BlockSpec((N_GROUPS, GC, GC), lambda i: (0, 0, 0)),
                   pl.BlockSpec((1, BR_W), lambda i: (0, 0))],
        out_shape=[jax.ShapeDtypeStruct((r, BR_W), BF16), jax.ShapeDtypeStruct((N_GROUPS, GC, GC), F32),
                   jax.ShapeDtypeStruct((1, BR_W), F32)],
        compiler_params=_cparams("arbitrary"),
    )(p, pool_w, pool_scale.reshape(1, BR_W), dy)


def _f_sgu_v(pvg, lng, lnb):
    return _ln(jax.nn.gelu(pvg), lng, lnb)


def _sgu_fwd(name, p, ln_g, ln_b, sgu_w, sgu_b):
    r = p.shape[0]

    def body(pu_ref, pv_ref, g_ref, b_ref, w_ref, sb_ref, y_ref):
        vn = _f_sgu_v(pv_ref[...], g_ref[...], b_ref[...])
        u = jax.nn.gelu(pu_ref[...])
        for g in range(N_GROUPS):
            cols = slice(g * GC, (g + 1) * GC)
            s = _dot(w_ref[g], vn[:, cols], _NN) + sb_ref[g]
            y_ref[:, cols] = (u[:, cols] * s).astype(y_ref.dtype)

    return pl.pallas_call(
        body, name=name, grid=(r // SGU_CHUNK,),
        in_specs=[pl.BlockSpec((SGU_CHUNK, BR_W), lambda i: (i, OFF_U // BR_W)),
                  pl.BlockSpec((SGU_CHUNK, BR_W), lambda i: (i, OFF_VG // BR_W)),
                  pl.BlockSpec((1, BR_W), lambda i: (0, 0)), pl.BlockSpec((1, BR_W), lambda i: (0, 0)),
                  pl.BlockSpec((N_GROUPS, GC, GC), lambda i: (0, 0, 0)),
                  pl.BlockSpec((N_GROUPS, SGU_CHUNK, 1), lambda i: (0, 0, 0))],
        out_specs=pl.BlockSpec((SGU_CHUNK, BR_W), lambda i: (i, 0)),
        out_shape=jax.ShapeDtypeStruct((r, BR_W), BF16),
        compiler_params=_cparams("parallel"),
    )(p, p, ln_g.reshape(1, BR_W), ln_b.reshape(1, BR_W), sgu_w, sgu_b.reshape(N_GROUPS, SGU_CHUNK, 1))


def _sgu_bwd(name, p, ln_g, ln_b, sgu_w, sgu_b, dy):
    r = p.shape[0]

    def body(pu_ref, pv_ref, g_ref, b_ref, w_ref, sb_ref, dy_ref, dp_ref, dg_ref, db_ref, dw_ref, dsb_ref):
        i = pl.program_id(0)

        @pl.when(i == 0)
        def _():
            for ref in (dg_ref, db_ref, dw_ref, dsb_ref):
                ref[...] = jnp.zeros_like(ref)

        vn, vjp_v = jax.vjp(_f_sgu_v, pv_ref[...], g_ref[...], b_ref[...])
        u, vjp_u = jax.vjp(jax.nn.gelu, pu_ref[...])
        dy = dy_ref[...]
        du, dvn = [], []
        for g in range(N_GROUPS):
            cols = slice(g * GC, (g + 1) * GC)
            s = _dot(w_ref[g], vn[:, cols], _NN) + sb_ref[g]
            du.append(dy[:, cols] * s)
            ds = dy[:, cols] * u[:, cols]
            dsb_ref[g] += jnp.sum(ds, axis=1, keepdims=True)
            dw_ref[g] += _dot(ds, vn[:, cols], _NT)
            dvn.append(_dot(w_ref[g], ds, _TN))
        (dpu,) = vjp_u(jnp.concatenate(du, axis=1))
        dpv, dg, db = vjp_v(jnp.concatenate(dvn, axis=1))
        dp_ref[:, 0:BR_W] = dpu.astype(dp_ref.dtype)
        dp_ref[:, BR_W:2 * BR_W] = dpv.astype(dp_ref.dtype)
        dg_ref[...] += dg
        db_ref[...] += db

    vec = pl.BlockSpec((1, BR_W), lambda i: (0, 0))
    wsp = pl.BlockSpec((N_GROUPS, GC, GC), lambda i: (0, 0, 0))
    bsp = pl.BlockSpec((N_GROUPS, SGU_CHUNK, 1), lambda i: (0, 0, 0))
    return pl.pallas_call(
        body, name=name, grid=(r // SGU_CHUNK,),
        in_specs=[pl.BlockSpec((SGU_CHUNK, BR_W), lambda i: (i, OFF_U // BR_W)),
                  pl.BlockSpec((SGU_CHUNK, BR_W), lambda i: (i, OFF_VG // BR_W)),
                  vec, vec, wsp, bsp, pl.BlockSpec((SGU_CHUNK, BR_W), lambda i: (i, 0))],
        out_specs=[pl.BlockSpec((SGU_CHUNK, 2 * BR_W), lambda i: (i, 0)), vec, vec, wsp, bsp],
        out_shape=[jax.ShapeDtypeStruct((r, 2 * BR_W), BF16), jax.ShapeDtypeStruct((1, BR_W), F32),
                   jax.ShapeDtypeStruct((1, BR_W), F32), jax.ShapeDtypeStruct((N_GROUPS, GC, GC), F32),
                   jax.ShapeDtypeStruct((N_GROUPS, SGU_CHUNK, 1), F32)],
        compiler_params=_cparams("arbitrary"),
    )(p, p, ln_g.reshape(1, BR_W), ln_b.reshape(1, BR_W), sgu_w, sgu_b.reshape(N_GROUPS, SGU_CHUNK, 1), dy)


def _conv_w8(conv_w):
    return jnp.concatenate([conv_w, jnp.zeros((8 - conv_w.shape[0], conv_w.shape[1]), F32)], axis=0)


def _conv_fwd(name, p, conv_w, rc):
    r = p.shape[0]

    def body(cb_ref, cc_ref, cx_ref, w_ref, y_ref):
        t, lo, hi = _segments((r, GC), rc)
        z = cc_ref[...] * cx_ref[...]
        w = w_ref[...]
        c = _shifted(z, -1, t, lo, hi) * w[0:1] + z * w[1:2] + _shifted(z, 1, t, lo, hi) * w[2:3]
        y_ref[...] = (cb_ref[...] * c).astype(y_ref.dtype)

    nb = OFF_CB // GC
    return pl.pallas_call(
        body, name=name, grid=(N_GROUPS,),
        in_specs=[pl.BlockSpec((r, GC), lambda j: (0, nb + j)),
                  pl.BlockSpec((r, GC), lambda j: (0, nb + N_GROUPS + j)),
                  pl.BlockSpec((r, GC), lambda j: (0, nb + 2 * N_GROUPS + j)),
                  pl.BlockSpec((8, GC), lambda j: (0, j))],
        out_specs=pl.BlockSpec((r, GC), lambda j: (0, j)),
        out_shape=jax.ShapeDtypeStruct((r, BR_W), BF16),
        compiler_params=_cparams("parallel"),
    )(p, p, p, _conv_w8(conv_w))


def _conv_bwd(name, p, conv_w, dy, rc):
    r = p.shape[0]

    def body(cb_ref, cc_ref, cx_ref, w_ref, dy_ref, dcb_ref, dcc_ref, dcx_ref, dw_ref):
        t, lo, hi = _segments((r, GC), rc)
        cc, cx, w, dy = cc_ref[...], cx_ref[...], w_ref[...], dy_ref[...]
        z = cc * cx
        zp, zn = _shifted(z, -1, t, lo, hi), _shifted(z, 1, t, lo, hi)
        dcb_ref[...] = (dy * (zp * w[0:1] + z * w[1:2] + zn * w[2:3])).astype(dcb_ref.dtype)
        dc = dy * cb_ref[...]
        dw_ref[...] = jnp.concatenate(
            [jnp.sum(dc * zp, axis=0, keepdims=True), jnp.sum(dc * z, axis=0, keepdims=True),
             jnp.sum(dc * zn, axis=0, keepdims=True), jnp.zeros((5, GC), F32)], axis=0)
        dz = dc * w[1:2] + _shifted(dc, 1, t, lo, hi) * w[0:1] + _shifted(dc, -1, t, lo, hi) * w[2:3]
        dcc_ref[...] = (dz * cx).astype(dcc_ref.dtype)
        dcx_ref[...] = (dz * cc).astype(dcx_ref.dtype)

    nb = OFF_CB // GC
    return pl.pallas_call(
        body, name=name, grid=(N_GROUPS,),
        in_specs=[pl.BlockSpec((r, GC), lambda j: (0, nb + j)),
                  pl.BlockSpec((r, GC), lambda j: (0, nb + N_GROUPS + j)),
                  pl.BlockSpec((r, GC), lambda j: (0, nb + 2 * N_GROUPS + j)),
                  pl.BlockSpec((8, GC), lambda j: (0, j)),
                  pl.BlockSpec((r, GC), lambda j: (0, j))],
        out_specs=[pl.BlockSpec((r, GC), lambda j: (0, j))] * 3 + [pl.BlockSpec((8, GC), lambda j: (0, j))],
        out_shape=[jax.ShapeDtypeStruct((r, BR_W), BF16)] * 3 + [jax.ShapeDtypeStruct((8, BR_W), F32)],
        compiler_params=_cparams("parallel"),
    )(p, p, p, _conv_w8(conv_w), dy)


def _rope_tables(rc, n):
    rows = n // GRID_W
    row = jnp.repeat(jnp.arange(rows), GRID_W).astype(F32)
    col = jnp.tile(jnp.arange(GRID_W), rows).astype(F32)
    inv = ROPE_THETA ** (-jnp.arange(0, ROPE_AXIS_DIM, 2, dtype=F32) / ROPE_AXIS_DIM)
    ang_r, ang_c = row[:, None] * inv, col[:, None] * inv
    cos = jnp.concatenate([jnp.cos(ang_r), jnp.cos(ang_r), jnp.cos(ang_c), jnp.cos(ang_c)], axis=1)
    sin = jnp.concatenate([-jnp.sin(ang_r), jnp.sin(ang_r), -jnp.sin(ang_c), jnp.sin(ang_c)], axis=1)
    cos = jnp.concatenate([jnp.ones((rc, HEAD_DIM), F32), cos], axis=0)
    sin = jnp.concatenate([jnp.zeros((rc, HEAD_DIM), F32), sin], axis=0)
    return cos, sin


MOD_NAMES = ("sh1", "sc1", "g1", "sh2", "sc2", "g2")


def _local_step(xin, target, mod, comm, sp, rc, alpha):
    def carrying(fn):
        def call(name, *args, **kw):
            rider = comm.rider(name)
            if rider is None:
                return fn(name, *args, **kw)
            res, r_outs = fn(name, *args, rider=rider, **kw)
            comm.deliver(name, r_outs)
            return res
        return call

    mm, rows, attn_fwd, attn_bwd = carrying(_mm), carrying(_rows), carrying(_attn_fwd), carrying(_attn_bwd)
    mm_fused = carrying(_mm_fused)
    r, d = xin.shape
    n_layers = mod.shape[0]
    tm_n, tm_w = 256, 128
    nbc_n, nbc_w = rc // tm_n, rc // tm_w
    cos, sin = _rope_tables(rc, r - rc)
    mp = mod.reshape(n_layers, 2, 6, 1, d)
    mods = [{nm: mp[i, :, j] for j, nm in enumerate(MOD_NAMES)} for i in range(n_layers)]
    f_ln_mod, f_ln_last = _make_f_ln(alpha, True), _make_f_ln(alpha, False)

    def whole(arr, roff=0):
        return (arr, 0, arr.shape[1], roff)

    (hb,) = rows("mod_in", _f_mod, r, tm_n, nbc_n, [whole(xin)], [mods[0]["sc1"], mods[0]["sh1"]],
                 [(r, d, BF16, 0)], [])
    saved = []
    x = xin
    for i in range(n_layers):
        last = i == n_layers - 1
        w, s, m = functools.partial(comm.weight, i), sp[i], mods[i]
        sv = {"x": x, "hb": hb}
        p = mm(f"l{i}_in", hb, w("in_t"), "nt", F32)
        q, k, v = rows(f"l{i}_prep", _f_prep, r, tm_n, nbc_n,
                       [(p, 0, OFF_POOL, 0), whole(cos), whole(sin)], [_typed(s["q_norm_g"]), _typed(s["k_norm_g"])],
                       [(r, Q_W, BF16, 0), (r, KV_W, BF16, 0), (r, KV_W, BF16, 0)], [])
        att, sv["pw"] = attn_fwd(f"l{i}_attn", q, k, v, rc, not last)
        ys = [att,
              _pool_fwd(f"l{i}_pool", p, s["pool_w"], s["pool_scale"], rc),
              _sgu_fwd(f"l{i}_sgu", p, s["sgu_ln_g"], s["sgu_ln_b"], s["sgu_w"], s["sgu_b"]),
              _conv_fwd(f"l{i}_conv", p, s["conv_w"], rc)]
        ts = [mm(f"l{i}_br{kk}", ys[kk], w(f"br{kk}"), "nt", BF16) for kk in range(N_BRANCH)]
        gpre = mm(f"l{i}_gate", hb, w("gate_t"), "nt", BF16)
        (mg,) = rows(f"l{i}_merge", _f_gate, r, tm_w, nbc_w, [whole(gpre)] + [whole(t) for t in ts],
                     [_typed(s["b_gate"])], [(r, d, BF16, 0)], [])
        o = mm(f"l{i}_o", mg, w("o"), "nn", F32)
        x1, h2b = rows(f"l{i}_ln1", f_ln_mod, r, tm_n, nbc_n, [whole(x), whole(o)],
                       [m["g1"], _typed(s["ln1_g"]), _typed(s["ln1_b"]), m["sc2"], m["sh2"]],
                       [(r, d, F32, 0), (r, d, BF16, 0)], [])
        af, bf, f = mm_fused(f"l{i}_ffgu", h2b, [w("ffg_t"), w("ffu_t")],
                             lambda prods, _: (prods[0], prods[1], _f_swiglu(prods[0], prods[1])), [], [BF16, BF16, BF16])
        o2 = mm(f"l{i}_ffd", f, w("ffd"), "nn", F32)
        if last:
            (x2,) = rows(f"l{i}_ln2", f_ln_last, r, tm_n, nbc_n, [whole(x1), whole(o2)],
                         [m["g2"], _typed(s["ln2_g"]), _typed(s["ln2_b"])], [(r, d, F32, 0)], [])
            hb = None
        else:
            nx = mods[i + 1]
            x2, hb = rows(f"l{i}_ln2", f_ln_mod, r, tm_n, nbc_n, [whole(x1), whole(o2)],
                          [m["g2"], _typed(s["ln2_g"]), _typed(s["ln2_b"]), nx["sc1"], nx["sh1"]],
                          [(r, d, F32, 0), (r, d, BF16, 0)], [])
        sv.update(p=p, gpre=gpre, q=q, k=k, v=v, ys=ys, ts=ts, mg=mg, o=o, x1=x1, h2b=h2b, af=af, bf=bf, f=f, o2=o2)
        saved.append(sv)
        x = x2

    lat = jnp.concatenate([jnp.zeros((1, 1, 128), F32), jnp.ones((1, 1, 128), F32)], axis=0)

    def f_loss(xb, tb, msk):
        diff = (xb - tb) * msk[:, 0:1]
        part = jnp.sum(jnp.mean(jnp.square(diff), axis=-1, keepdims=True), axis=0, keepdims=True)
        return diff * (1.0 / d), jnp.broadcast_to(part, (1, 128))

    dx_direct, loss_acc = rows("loss", f_loss, r, tm_n, nbc_n, [whole(x), whole(target, nbc_n)], [lat],
                               [(r, d, F32, 0)], [128])
    loss = 0.5 * loss_acc[1, 0, 0]

    dmods = [dict() for _ in range(n_layers)]
    dsp = [dict() for _ in range(n_layers)]
    dh = None

    def small_done(j):
        ds = dict(dsp[j])
        for nm in ("ln1_g", "ln1_b", "ln2_g", "ln2_b", "b_gate", "q_norm_g", "k_norm_g"):
            ds[nm] = ds[nm][0, 0]
        for nm in ("pool_scale", "sgu_ln_g", "sgu_ln_b"):
            ds[nm] = ds[nm].reshape(-1)
        ds["sgu_b"] = ds["sgu_b"].reshape(N_GROUPS, SGU_CHUNK)
        comm.small_ready(j, jnp.concatenate([dmods[j][nm][:, 0, :] for nm in MOD_NAMES], axis=-1), ds)

    for i in reversed(range(n_layers)):
        last = i == n_layers - 1
        w, s, m, sv = functools.partial(comm.weight, i), sp[i], mods[i], saved[i]
        dm, dw, ds = dmods[i], {}, dsp[i]
        ln2 = [m["g2"], _typed(s["ln2_g"]), _typed(s["ln2_b"])]
        if last:
            res = rows(f"l{i}_ln2_bwd", _vjp_fn(f_ln_last, 2, 1), r, tm_n, nbc_n,
                       [whole(sv["x1"]), whole(sv["o2"]), whole(dx_direct)], ln2,
                       [(r, d, F32, 0), (r, d, BF16, 0)], [d, (d,), (d,)])
            dx1, do2, dm["g2"], dlg, dlb = res
        else:
            nx = mods[i + 1]
            res = rows(f"l{i}_ln2_bwd", _vjp_fn(f_ln_mod, 2, 2), r, tm_n, nbc_n,
                       [whole(sv["x1"]), whole(sv["o2"]), whole(dx_direct), whole(dh)],
                       ln2 + [nx["sc1"], nx["sh1"]],
                       [(r, d, F32, 0), (r, d, BF16, 0)], [d, (d,), (d,), d, d])
            dx1, do2, dm["g2"], dlg, dlb, dmods[i + 1]["sc1"], dmods[i + 1]["sh1"] = res
            small_done(i + 1)
        ds["ln2_g"], ds["ln2_b"] = dlg, dlb
        dab, dbb = mm_fused(f"l{i}_dF", do2, [w("ffd")],
                            lambda prods, tiles: jax.vjp(_f_swiglu, *tiles)[1](prods[0]), [sv["af"], sv["bf"]],
                            [BF16, BF16])
        comm.grads(i, {"ffd": mm(f"l{i}_dWffd", sv["f"], do2, "tn", BF16)})
        comm.grads(i, {"ffg_t": mm(f"l{i}_dWffg", dab, sv["h2b"], "tn", BF16)})
        comm.grads(i, {"ffu_t": mm(f"l{i}_dWffu", dbb, sv["h2b"], "tn", BF16)})
        dh2 = mm(f"l{i}_dh2a", dab, w("ffg_t"), "nn", F32)
        dh2 = mm(f"l{i}_dh2b", dbb, w("ffu_t"), "nn", F32, acc=dh2)
        res = rows(f"l{i}_ln1_bwd", _vjp_fn(f_ln_mod, 2, 2), r, tm_n, nbc_n,
                   [whole(sv["x"]), whole(sv["o"]), whole(dx1), whole(dh2)],
                   [m["g1"], _typed(s["ln1_g"]), _typed(s["ln1_b"]), m["sc2"], m["sh2"]],
                   [(r, d, F32, 0), (r, d, BF16, 0)], [d, (d,), (d,), d, d])
        dx_direct, do, dm["g1"], ds["ln1_g"], ds["ln1_b"], dm["sc2"], dm["sh2"] = res
        dmg = mm(f"l{i}_dMg", do, w("o"), "nt", F32)
        comm.grads(i, {"o": mm(f"l{i}_dWo", sv["mg"], do, "tn", BF16)})
        res = rows(f"l{i}_merge_bwd", _vjp_fn(_f_gate, 5, 1), r, tm_w, nbc_w,
                   [whole(sv["gpre"])] + [whole(t) for t in sv["ts"]] + [whole(dmg)], [_typed(s["b_gate"])],
                   [(r, N_BRANCH * d, BF16, 0)] + [(r, d, BF16, 0)] * N_BRANCH, [(N_BRANCH * d,)])
        dgb, dts, ds["b_gate"] = res[0], res[1:1 + N_BRANCH], res[1 + N_BRANCH]
        comm.grads(i, {"gate_t": mm(f"l{i}_dWgate", dgb, sv["hb"], "tn", BF16)})
        dys = [mm(f"l{i}_dY{kk}", dts[kk], w(f"br{kk}"), "nn", F32) for kk in range(N_BRANCH)]
        for kk in range(N_BRANCH):
            comm.grads(i, {f"br{kk}": mm(f"l{i}_dWbr{kk}", dts[kk], sv["ys"][kk], "tn", BF16)})
        dq, dk, dv = attn_bwd(f"l{i}_attn_bwd", sv["q"], sv["k"], sv["v"], sv["pw"], dys[0], rc, not last)
        res = rows(f"l{i}_prep_bwd", _vjp_fn(_f_prep, 3, 3, keep=(0, 3, 4)), r, tm_n, nbc_n,
                   [(sv["p"], 0, OFF_POOL, 0), whole(cos), whole(sin), whole(dq), whole(dk), whole(dv)],
                   [_typed(s["q_norm_g"]), _typed(s["k_norm_g"])],
                   [(r, OFF_POOL, BF16, 0)], [(HEAD_DIM,), (HEAD_DIM,)])
        dp_qkv, ds["q_norm_g"], ds["k_norm_g"] = res
        dp_pool, ds["pool_w"], ds["pool_scale"] = _pool_bwd(f"l{i}_pool_bwd", sv["p"], s["pool_w"], s["pool_scale"],
                                                            dys[1], rc)
        dp_sgu, ds["sgu_ln_g"], ds["sgu_ln_b"], ds["sgu_w"], ds["sgu_b"] = _sgu_bwd(
            f"l{i}_sgu_bwd", sv["p"], s["sgu_ln_g"], s["sgu_ln_b"], s["sgu_w"], s["sgu_b"], dys[2])
        dp_cb, dp_cc, dp_cx, dcw = _conv_bwd(f"l{i}_conv_bwd", sv["p"], s["conv_w"], dys[3], rc)
        ds["conv_w"] = dcw[0:3]
        dpb = jnp.concatenate([dp_qkv, dp_pool, dp_sgu, dp_cb, dp_cc, dp_cx], axis=1)
        comm.grads(i, {"in_t": mm(f"l{i}_dWin", dpb, sv["hb"], "tn", BF16)})
        dh = mm(f"l{i}_dhb_a", dpb, w("in_t"), "nn", F32)
        dh = mm(f"l{i}_dhb_b", dgb, w("gate_t"), "nn", F32, acc=dh)

    def f_mod_bwd(xb, ddir, dhb, sc, sh):
        _, vjp = jax.vjp(_f_mod, xb, sc, sh)
        dxb, dsc, dsh = vjp(dhb)
        return dxb + ddir, dsc, dsh

    grad_x, dmods[0]["sc1"], dmods[0]["sh1"] = rows(
        "mod_in_bwd", f_mod_bwd, r, tm_n, nbc_n, [whole(xin), whole(dx_direct), whole(dh)],
        [mods[0]["sc1"], mods[0]["sh1"]], [(r - rc, d, F32, nbc_n)], [d, d])
    small_done(0)
    return loss, grad_x


def _direct_rider(src):
    def peers():
        mx, my, mc = [lax.axis_index(a) for a in MESH_AXES]
        out = []
        for kk in range(1, N_DEV):
            px = 1 - mx if kk & 4 else mx
            py = 1 - my if kk & 2 else my
            pc = 1 - mc if kk & 1 else mc
            out.append(((px, py, pc), 4 * px + 2 * py + pc))
        return 4 * mx + 2 * my + mc, out

    def start(ins, outs, sems):
        send, recv, loc = sems
        me, others = peers()
        pltpu.make_async_copy(ins[0], outs[0].at[me], loc.at[0]).start()
        for j, (peer, _) in enumerate(others):
            _rcopy(ins[0], outs[0].at[me], send, recv, j, peer).start()

    def finish(ins, outs, sems):
        send, recv, loc = sems
        me, others = peers()
        for j, (peer, peer_l) in enumerate(others):
            cp = _rcopy(ins[0], outs[0].at[peer_l], send, recv, j, peer)
            cp.wait_recv()
            cp.wait_send()
        pltpu.make_async_copy(ins[0], outs[0].at[me], loc.at[0]).wait()

    return _Rider([src], [jax.ShapeDtypeStruct((N_DEV,) + src.shape, src.dtype)],
                  _sem_scratch(N_DEV - 1, N_DEV - 1, 1), start, finish)


def _mesh_place():
    mx, my, mc = [lax.axis_index(a) for a in MESH_AXES]
    chips = [(1 - mx, my), (mx, 1 - my), (1 - mx, 1 - my)]

    def lid(px, py, pc):
        return 4 * px + 2 * py + pc

    return (mx, my, mc), (mx, my, 1 - mc), chips, lid


def _rcopy(src, dst, send_sems, recv_sems, k, to):
    return pltpu.make_async_remote_copy(src_ref=src, dst_ref=dst, send_sem=send_sems.at[k], recv_sem=recv_sems.at[k],
                                        device_id=to, device_id_type=pl.DeviceIdType.MESH)


def _sem_scratch(*sizes):
    return [pltpu.SemaphoreType.DMA((s,)) for s in sizes]


def _gather_rider(src, rows1, rows2, buf=None):
    def place():
        (mx, my, mc), sib, chips, lid = _mesh_place()
        xn, yn, dg = [(*chip, mc) for chip in chips]
        return lid(mx, my, mc), sib, xn, yn, dg, lid

    def halves(rows):
        r0, r1 = rows
        mid = r0 + (r1 - r0) // 32 * 16
        return pl.ds(r0, mid - r0), pl.ds(mid, r1 - mid), pl.ds(r0, r1 - r0)

    n_src = 1 if rows1 is not None else 0

    def start(ins, outs, sems):
        send, recv, loc = sems
        me, sib, xn, yn, dg, lid = place()
        if rows1 is not None:
            win = pl.ds(rows1[0], rows1[1] - rows1[0])
            mine, dst = ins[0].at[win], outs[0].at[me, win]
            pltpu.make_async_copy(mine, dst, loc.at[0]).start()
            for t, to in enumerate((sib, xn, yn)):
                _rcopy(mine, dst, send, recv, t, to).start()
        if rows2 is not None:
            top, bot, win = halves(rows2)
            xb, yb = outs[0].at[lid(*xn)], outs[0].at[lid(*yn)]
            _rcopy(xb.at[top], xb.at[top], send, recv, 3, yn).start()
            _rcopy(yb.at[bot], yb.at[bot], send, recv, 4, xn).start()
            _rcopy(xb.at[win], xb.at[win], send, recv, 5, sib).start()
            _rcopy(yb.at[win], yb.at[win], send, recv, 6, sib).start()

    def finish(ins, outs, sems):
        send, recv, loc = sems
        me, sib, xn, yn, dg, lid = place()
        if rows2 is not None:
            top, bot, win = halves(rows2)
            db = outs[0].at[lid(*dg)]
            _rcopy(db.at[top], db.at[top], send, recv, 3, yn).wait_recv()
            _rcopy(db.at[bot], db.at[bot], send, recv, 4, xn).wait_recv()
            _rcopy(db.at[win], db.at[win], send, recv, 7, sib).start()
            for t, dev in ((5, xn), (6, yn), (7, dg)):
                blk = outs[0].at[lid(dev[0], dev[1], 1 - dev[2]), win]
                _rcopy(blk, blk, send, recv, t, sib).wait_recv()
            for t, part in ((3, top), (4, bot), (5, win), (6, win), (7, win)):
                _rcopy(db.at[part], db.at[part], send, recv, t, sib).wait_send()
        if rows1 is not None:
            win = pl.ds(rows1[0], rows1[1] - rows1[0])
            mine, dst = ins[0].at[win], outs[0].at[me, win]
            for t, dev in enumerate((sib, xn, yn)):
                cp = _rcopy(mine, outs[0].at[lid(*dev), win], send, recv, t, dev)
                cp.wait_recv()
                cp.wait_send()
            pltpu.make_async_copy(mine, dst, loc.at[0]).wait()

    out_shape = jax.ShapeDtypeStruct((N_DEV,) + src.shape, src.dtype)
    inputs = ([src] if n_src else []) + ([buf] if buf is not None else [])
    return _Rider(inputs, [out_shape], _sem_scratch(8, 8, 1), start, finish,
                  aliases={n_src: 0} if buf is not None else {})


def _sibling_rider(part):
    def start(ins, outs, sems):
        send, recv = sems
        (mx, my, mc), sib, chips, lid = _mesh_place()
        for t, slab in enumerate([lid(*sib)] + [lid(*chip, 1 - mc) for chip in chips]):
            _rcopy(ins[0].at[slab], outs[0].at[t], send, recv, t, sib).start()

    def finish(ins, outs, sems):
        send, recv = sems
        _, sib, _, _ = _mesh_place()
        for t in range(4):
            cp = _rcopy(ins[0].at[0], outs[0].at[t], send, recv, t, sib)
            cp.wait_recv()
            cp.wait_send()

    return _Rider([part], [jax.ShapeDtypeStruct((4,) + part.shape[1:], part.dtype)], _sem_scratch(4, 4), start, finish)


def _chips_rider(pair, rows, buf=None):
    r0, r1 = rows
    win = pl.ds(r0, r1 - r0)

    def start(ins, outs, sems):
        send, recv = sems
        (mx, my, mc), sib, chips, lid = _mesh_place()
        for j, chip in enumerate(chips):
            _rcopy(ins[0].at[j, win], outs[0].at[j, win], send, recv, j, (*chip, mc)).start()

    def finish(ins, outs, sems):
        send, recv = sems
        (mx, my, mc), sib, chips, lid = _mesh_place()
        for j, chip in enumerate(chips):
            cp = _rcopy(ins[0].at[j, win], outs[0].at[j, win], send, recv, j, (*chip, mc))
            cp.wait_recv()
            cp.wait_send()

    out_shape = jax.ShapeDtypeStruct(pair.shape, pair.dtype)
    if buf is None:
        return _Rider([pair], [out_shape], _sem_scratch(3, 3), start, finish)
    return _Rider([pair, buf], [out_shape], _sem_scratch(3, 3), start, finish, aliases={1: 0})


def _run_rider(name, rider):
    n_in, n_out = len(rider.inputs), len(rider.out_shapes)

    def body(*refs):
        ins, outs, sems = refs[:n_in], refs[n_in:n_in + n_out], refs[n_in + n_out:]
        rider.start(ins, outs, sems)
        rider.finish(ins, outs, sems)

    any_spec = pl.BlockSpec(memory_space=pl.ANY)
    res = pl.pallas_call(body, name=name, in_specs=[any_spec] * n_in, out_specs=[any_spec] * n_out,
                         out_shape=rider.out_shapes, scratch_shapes=rider.scratch,
                         input_output_aliases=rider.aliases)(*rider.inputs)
    return list(res)


def _slab_ids():
    (mx, my, mc), _, chips, lid = _mesh_place()
    return jnp.stack([lid(*chip, mc) for chip in chips] + [lid(mx, my, mc)]).astype(jnp.int32)


def _pair_sum(name, part, rsib, ids):
    _, n, k = part.shape
    tr = _row_tile(n, 512, 16)

    def body(ids_ref, p_ref, r_ref, o_ref):
        o_ref[...] = (p_ref[...].astype(F32) + r_ref[...].astype(F32)).astype(o_ref.dtype)

    grid_spec = pltpu.PrefetchScalarGridSpec(
        num_scalar_prefetch=1, grid=(3, n // tr),
        in_specs=[pl.BlockSpec((None, tr, k), lambda j, i, ids: (ids[j], i, 0)),
                  pl.BlockSpec((None, tr, k), lambda j, i, ids: (1 + j, i, 0))],
        out_specs=pl.BlockSpec((None, tr, k), lambda j, i, ids: (j, i, 0)))
    return pl.pallas_call(body, name=name, grid_spec=grid_spec, out_shape=jax.ShapeDtypeStruct((3, n, k), part.dtype),
                          compiler_params=_cparams("parallel", "parallel"))(ids, part, rsib)


def _sum5(name, part, rsib, rici, ids, layer, stacked, rider=None):
    _, n, k = part.shape
    tr = _row_tile(n, 512, 16)
    first = isinstance(stacked, int)

    def body(ids_ref, p_ref, r_ref, c_ref, *rest):
        acc = p_ref[...].astype(F32) + r_ref[...].astype(F32)
        for j in range(3):
            acc = acc + c_ref[j].astype(F32)
        rest[-1][...] = acc

    in_specs = [pl.BlockSpec((None, tr, k), lambda i, ids: (ids[3], i, 0)),
                pl.BlockSpec((None, tr, k), lambda i, ids: (0, i, 0)),
                pl.BlockSpec((3, tr, k), lambda i, ids: (0, i, 0))] + ([] if first else [pl.BlockSpec(memory_space=pl.ANY)])
    n_layers = stacked if first else stacked.shape[0]
    outs, r_outs = _pcall(name, body, (n // tr,), in_specs, [pl.BlockSpec((None, tr, k), lambda i, ids: (layer, i, 0))],
                          [jax.ShapeDtypeStruct((n_layers, n, k), F32)],
                          (part, rsib, rici) + (() if first else (stacked,)), ("parallel",), rider=rider, prefetch=ids,
                          aliases={} if first else {3: 0})
    return outs[0] if rider is None else (outs[0], r_outs)


W_KEYS = ("in_t", "br0", "br1", "br2", "br3", "gate_t", "o", "ffg_t", "ffu_t", "ffd")
SUMS_TRANSPOSED_LATER = ("gate_t", "br0", "br1", "br2", "br3")


CARRIER_US = {"mod_in": 12, "in": 55, "gate": 95, "prep": 19, "attn": 125,"br0": 15, "merge": 50, "o": 25, "ln1": 27,
              "ffgu": 135, "ffd": 73, "ln2": 27, "loss": 20, "ln2_bwd": 44, "dF": 80,
              "dWffd": 64, "dh2a": 75, "dh2b": 75, "dWffg": 64, "dWffu": 64, "ln1_bwd": 44,
              "dMg": 25, "dWo": 25, "merge_bwd": 80, "dY0": 14, "dWbr0": 15, "attn_bwd": 140,"prep_bwd": 28,
              "dWin": 54, "dWgate": 95, "dhb_a": 64, "dhb_b": 115}
ICI_US_PER_MIB = 45.0
GATHER_US_PER_MIB = 30.0
RELAY_US_PER_MIB = 15.0
D2D_US_PER_MIB = 6.8
MIN_CHUNK_US = 10.0
CARRIER_FILL_FORWARD = 1.15
CARRIER_FILL_BACKWARD = 0.95


class _Comm:
    def __init__(self, wsrc):
        self.wsrc = wsrc
        self.n_layers = len(wsrc)
        self.queue = []
        self.riding = {}
        self.n_alone = 0
        self.buf, self.left = {}, {}
        self.part, self.rsib, self.pair = {}, {}, {}
        self.ids = _slab_ids()
        for i in range(self.n_layers):
            for k in W_KEYS:
                self._push_chunks("gather", ("w", i, k), wsrc[i][k].shape, wsrc[i][k].dtype)

    def _push_chunks(self, kind, item, shape, dtype):
        n, k = shape[-2], shape[-1]
        mib = n * k * jnp.dtype(dtype).itemsize / 2 ** 20
        pieces = max(1, int(mib * ICI_US_PER_MIB // MIN_CHUNK_US))
        while n % (16 * pieces):
            pieces -= 1
        step = n // pieces
        self.left[item] = n
        us = mib * (GATHER_US_PER_MIB if kind == "gather" else ICI_US_PER_MIB) / pieces
        for c in range(pieces):
            self.queue.append(dict(kind=kind, item=item, rows=(c * step, (c + 1) * step), rows2=None, us=us))

    @staticmethod
    def _merge(units, u):
        def joined(a, b):
            if a is None or b is None:
                return True, a or b
            return a[1] == b[0], (a[0], b[1])

        for v in units:
            if v["item"] == u["item"] and v["kind"] == u["kind"] and u["kind"] != "sibling":
                ok1, rows = joined(v["rows"], u["rows"])
                ok2, rows2 = joined(v["rows2"], u["rows2"])
                if ok1 and ok2:
                    v.update(rows=rows, rows2=rows2, us=v["us"] + u["us"])
                    return True
        return False

    def _unit_rider(self, u):
        item = u["item"]
        if u["kind"] == "gather":
            src = self.wsrc[item[1]][item[2]] if item[0] == "w" else self.part[item]
            return _gather_rider(src, u["rows"], u["rows2"], self.buf.get(item))
        if u["kind"] == "sibling":
            return _sibling_rider(self.part[item])
        return _chips_rider(self.pair[item], u["rows"], self.buf.get(item))

    def _done(self, u, out):
        item = u["item"]
        if u["kind"] == "sibling":
            self.rsib[item] = out
            self.pair[item] = _pair_sum(f"pair_l{item[1]}_{item[2]}", self.part[item], out, self.ids)
            self._push_chunks("chips", item, self.pair[item].shape, self.pair[item].dtype)
            return
        self.buf[item] = out
        if u["kind"] == "gather":
            if u["rows"] is not None:
                rows = u["rows"]
                mib = (rows[1] - rows[0]) * out.shape[-1] * out.dtype.itemsize / 2 ** 20
                self.queue.insert(0, dict(kind="gather", item=item, rows=None, rows2=rows, us=mib * RELAY_US_PER_MIB))
            if u["rows2"] is not None:
                self.left[item] -= u["rows2"][1] - u["rows2"][0]

    def _send(self, name, units, call):
        outs = call(_compose([self._unit_rider(u) for u in units]))
        for u, o in zip(units, outs):
            self._done(u, o)

    def exchange(self, name, src, budget_us):
        units = self._take(budget_us)
        outs = _run_rider(name, _compose([_direct_rider(src)] + [self._unit_rider(u) for u in units]))
        for u, o in zip(units, outs[1:]):
            self._done(u, o)
        return outs[0]

    def rider(self, name, budget_us=None):
        budget = CARRIER_US.get(name.split("_", 1)[1] if name[0] == "l" and name[1].isdigit() else name, 0) \
            if budget_us is None else budget_us
        units = self._take(budget)
        if not units:
            return None
        self.riding[name] = units
        return _compose([self._unit_rider(u) for u in units])

    def _take(self, budget):
        units, used = [], 0.0
        forward = bool(self.queue) and self.queue[0]["item"][0] == "w"
        fill = CARRIER_FILL_FORWARD if forward else CARRIER_FILL_BACKWARD
        while self.queue and used + self.queue[0]["us"] <= fill * budget:
            u = self.queue[0]
            if not self._merge(units, u):
                if any(v["item"] == u["item"] for v in units):
                    break
                units.append(dict(u))
            used += u["us"]
            del self.queue[0]
        return units

    def deliver(self, name, outs):
        for u, o in zip(self.riding.pop(name), outs):
            self._done(u, o)

    def _flush(self, item, kinds):
        hits = [p for p, u in enumerate(self.queue) if u["item"] == item and u["kind"] in kinds]
        if not hits:
            return
        prefix = self.queue[:hits[-1] + 1]
        del self.queue[:hits[-1] + 1]
        units = []
        for u in prefix:
            if not self._merge(units, u):
                units.append(dict(u))
        tag = "_".join(str(t) for t in item) + "_" + kinds[0]
        batches = [[]]
        for u in units:
            if any(v["item"] == u["item"] for v in batches[-1]):
                batches.append([])
            batches[-1].append(u)
        for batch in batches:
            self.n_alone += 1
            self._send(None, batch, functools.partial(_run_rider, f"alone{self.n_alone}_{tag}"))

    def begin(self):
        self._flush(("w", 0, "in_t"), ("gather",))

    def _finish_gather(self, item):
        while self.left[item] > 0:
            assert any(u["item"] == item for u in self.queue), item
            self._flush(item, ("gather",))
        return self.buf[item]

    def weight(self, i, k):
        o = self._finish_gather(("w", i, k))
        return o.reshape(-1, o.shape[-1])

    def grads(self, i, group):
        for k, g in group.items():
            item = ("g", i, k)
            self.part[item] = g.reshape(N_DEV, g.shape[0] // N_DEV, g.shape[1])
            us = g.size // N_DEV * g.dtype.itemsize / 2 ** 20 * D2D_US_PER_MIB
            self.queue.insert(0, dict(kind="sibling", item=item, rows=None, us=us))

    def total(self, k):
        out = self.n_layers
        for i in range(self.n_layers):
            item = ("g", i, k)
            self._flush(item, ("sibling",))
            self._flush(item, ("chips",))
            name = f"sum_l{i}_{k}"
            rider = self.rider(name, budget_us=self.part[item][0].size / 1.06e5) if k in SUMS_TRANSPOSED_LATER else None
            out = _sum5(name, self.part[item], self.rsib[item], self.buf[item], self.ids, i, out, rider=rider)
            if rider is not None:
                out, r_outs = out
                self.deliver(name, r_outs)
        return out

    def small_ready(self, i, dmod, ds):
        parts = [dmod[0], dmod[1]] + [ds[nm] for nm in LAYER_SMALL + ("conv_w",)]
        self.small_shapes = [p.shape for p in parts]
        self.gather_small(f"lat{i}", _pack([dmod[1]]))
        self.gather_small(f"small{i}", _pack(parts))

    def gather_small(self, name, arr):
        item = ("s", name)
        self.part[item] = arr
        waiting, self.queue = self.queue, []
        self._push_chunks("gather", item, arr.shape, arr.dtype)
        self.queue += waiting

    def gathered(self, name):
        return self._finish_gather(("s", name))


def _row_tile(n, pref, mult):
    best = None
    t = mult
    while t <= min(n, pref):
        if n % t == 0:
            best = t
        t += mult
    return best if best is not None else n


def _sum8(name, slabs):
    _, n, k = slabs.shape
    tr = _row_tile(n, 128, 16)

    def body(s_ref, o_ref):
        acc = s_ref[0].astype(F32)
        for j in range(1, N_DEV):
            acc = acc + s_ref[j].astype(F32)
        o_ref[...] = acc

    return pl.pallas_call(
        body, name=name, grid=(n // tr,),
        in_specs=[pl.BlockSpec((N_DEV, tr, k), lambda i: (0, i, 0))],
        out_specs=pl.BlockSpec((tr, k), lambda i: (i, 0)),
        out_shape=jax.ShapeDtypeStruct((n, k), F32),
        compiler_params=_cparams("parallel"),
    )(slabs)


def _adamw(name, w, g, m, v, rider=None):
    n, k = w.shape[-2:]
    tr = _row_tile(n, 256, 8)

    def body(w_ref, g_ref, m_ref, v_ref, d_ref, m2_ref, v2_ref):
        gv = g_ref[...]
        m2 = ADAM_B1 * m_ref[...] + (1.0 - ADAM_B1) * gv
        v2 = ADAM_B2 * v_ref[...] + (1.0 - ADAM_B2) * jnp.square(gv)
        m_hat = m2 / (1.0 - ADAM_B1 ** ADAM_STEP)
        v_hat = v2 / (1.0 - ADAM_B2 ** ADAM_STEP)
        d_ref[...] = -ADAM_LR * (m_hat / (jnp.sqrt(v_hat) + ADAM_EPS) + ADAM_WD * w_ref[...])
        m2_ref[...] = m2
        v2_ref[...] = v2

    if w.ndim == 2:
        grid, spec = (n // tr,), pl.BlockSpec((tr, k), lambda i: (i, 0))
    else:
        grid, spec = (w.shape[0], n // tr), pl.BlockSpec((None, tr, k), lambda l, i: (l, i, 0))
    outs, r_outs = _pcall(name, body, grid, [spec] * 4, [spec] * 3, [jax.ShapeDtypeStruct(w.shape, F32)] * 3,
                          (w, g, m, v), ("parallel",) * len(grid), rider=rider)
    return outs if rider is None else (outs, r_outs)


def _pack(arrs):
    flat = jnp.concatenate([a.reshape(-1).astype(F32) for a in arrs])
    pad = (-flat.shape[0]) % 2048
    if pad:
        flat = jnp.concatenate([flat, jnp.zeros((pad,), F32)])
    return flat.reshape(-1, 128)


def _unpack(packed, shapes):
    flat = packed.reshape(-1)
    out, off = [], 0
    for shp in shapes:
        size = math.prod(shp)
        out.append(flat[off:off + size].reshape(shp))
        off += size
    return out


WEIGHT_NAMES = ("c_ctx", "w_ada", "b_ada", "w_in", "q_norm_g", "k_norm_g", "pool_w", "pool_scale", "sgu_ln_g",
                "sgu_ln_b", "sgu_w", "sgu_b", "conv_w", "w_br_attn", "w_br_pool", "w_br_sgu", "w_br_conv", "w_gate",
                "b_gate", "w_o", "ln1_g", "ln1_b", "w_ff_gate", "w_ff_up", "w_ff_down", "ln2_g", "ln2_b")
COL_SHARDED = {"w_in": "in_t", "w_gate": "gate_t", "w_ff_gate": "ffg_t", "w_ff_up": "ffu_t",
               "w_br_attn": "br0", "w_br_pool": "br1", "w_br_sgu": "br2", "w_br_conv": "br3"}
ROW_SHARDED = {"w_o": "o", "w_ff_down": "ffd"}
LAYER_SMALL = ("q_norm_g", "k_norm_g", "pool_w", "pool_scale", "sgu_ln_g", "sgu_ln_b", "sgu_w", "sgu_b", "b_gate",
               "ln1_g", "ln1_b", "ln2_g", "ln2_b")
SMALL_ORDER = ("c_ctx", "b_ada") + LAYER_SMALL + ("conv_w",)


def _train_step(a):
    n_layers, d = a["w_in"].shape[0], a["x"].shape[-1]
    rc = a["ctx"].shape[1]
    alpha = (2 * n_layers) ** 0.25
    mx, my, mc = [lax.axis_index(ax) for ax in MESH_AXES]
    me = 4 * mx + 2 * my + mc
    ada_w = a["w_ada"].shape[-1]
    cw_loc = a["conv_w"].shape[-1]

    comm = _Comm([{**{key: jnp.swapaxes(a[nm], 1, 2)[i].astype(BF16) for nm, key in COL_SHARDED.items()},
                   **{key: a[nm][i].astype(BF16) for nm, key in ROW_SHARDED.items()}} for i in range(n_layers)])

    def carried(name, *args, budget_us, **kw):
        rider = comm.rider(name, budget_us=budget_us)
        if rider is None:
            return _mm(name, *args, **kw)
        res, r_outs = _mm(name, *args, rider=rider, **kw)
        comm.deliver(name, r_outs)
        return res

    n_c, n_cw = d, n_layers * 3 * cw_loc
    got = comm.exchange("gather_cond", _pack([a["c"], a["conv_w"]]), budget_us=12).reshape(N_DEV, -1)
    c_all = got[:, :n_c]
    conv_w = got[:, n_c:n_c + n_cw].reshape(N_DEV, n_layers, 3, cw_loc).transpose(1, 2, 0, 3).reshape(n_layers, 3, -1)
    cond = jnp.concatenate([c_all, a["c_ctx"][None], jnp.zeros((16 - N_DEV - 1, d), F32)], axis=0)
    sil, sil_vjp = jax.vjp(jax.nn.silu, cond)
    sil = sil.astype(BF16)

    mod_cols = jnp.concatenate([carried(f"ada{i}", sil, a["w_ada"][i], "nn", F32, budget_us=20)
                                for i in range(n_layers)], axis=0)
    got = comm.exchange("gather_mod", mod_cols, budget_us=22)
    mod_all = got.reshape(N_DEV, n_layers, 16, ada_w).transpose(1, 2, 0, 3).reshape(n_layers, 16, -1)
    mod_all = mod_all + a["b_ada"][:, None, :]
    mod = jnp.stack([mod_all[:, N_DEV], lax.dynamic_index_in_dim(mod_all, me, axis=1, keepdims=False)], axis=1)
    comm.begin()
    sp = [{nm: a[nm][i] for nm in LAYER_SMALL} for i in range(n_layers)]
    for i in range(n_layers):
        sp[i]["conv_w"] = conv_w[i]

    xin = jnp.concatenate([a["ctx"][0], a["x"][0]], axis=0)
    loss_l, grad_x = _local_step(xin, a["loss_target"][0], mod, comm, sp, rc, alpha)
    loss = lax.psum(loss_l, MESH_AXES)
    grads = {}

    def transposed_home(nm):
        return nm in COL_SHARDED and a[nm].shape[-1] % 128 != 0

    delta, new_m, new_v = {}, {}, {}

    def adamw(nm):
        name = f"adamw_{nm}"
        there = transposed_home(nm)
        view = (lambda t: jnp.swapaxes(t, 1, 2)) if there else (lambda t: t)
        if nm in COL_SHARDED:
            g = comm.total(COL_SHARDED[nm])
            grads[nm] = jnp.swapaxes(g, 1, 2)
            g = g if there else grads[nm]
        elif nm in ROW_SHARDED:
            g = grads[nm] = comm.total(ROW_SHARDED[nm])
        else:
            g = grads[nm]
        res = _adamw(name, view(a[nm]), g, view(a["m_" + nm]), view(a["v_" + nm]))
        delta[nm], new_m[nm], new_v[nm] = [view(t) for t in res]

    for nm in ("w_ff_down", "w_ff_gate", "w_ff_up", "w_o", "w_br_attn", "w_br_pool", "w_br_sgu", "w_br_conv"):
        adamw(nm)

    tots = [_unpack(_sum8(f"sum_small{i}", comm.gathered(f"small{i}")), comm.small_shapes) for i in range(n_layers)]
    dmod_c, dmod_lat_sum = jnp.stack([t[0] for t in tots]), jnp.stack([t[1] for t in tots])
    for j, nm in enumerate(LAYER_SMALL + ("conv_w",)):
        grads[nm] = jnp.stack([t[2 + j] for t in tots])
    grads["conv_w"] = lax.dynamic_slice_in_dim(grads["conv_w"], me * cw_loc, cw_loc, axis=2)
    grads["b_ada"] = dmod_c + dmod_lat_sum
    dmod_lat_all = jnp.stack([comm.gathered(f"lat{i}").reshape(N_DEV, -1)[:, :6 * d] for i in range(n_layers)])
    dm_rows = jnp.concatenate([dmod_lat_all, dmod_c[:, None, :],
                               jnp.zeros((n_layers, 16 - N_DEV - 1, 6 * d), F32)], axis=1)
    dm_cols = lax.dynamic_slice_in_dim(dm_rows, me * ada_w, ada_w, axis=2).astype(BF16)
    grads["w_ada"] = jnp.stack([carried(f"dWada{i}", sil, dm_cols[i], "tn", F32, budget_us=20)
                                for i in range(n_layers)])
    dsil = None
    for i in range(n_layers):
        dsil = carried(f"dsil{i}", dm_cols[i], a["w_ada"][i], "nt", F32, acc=dsil, budget_us=10)
    dsil = _sum8("sum_dsil", comm.exchange("gather_dsil", dsil, budget_us=17))
    grads["c_ctx"] = sil_vjp(dsil)[0][N_DEV]

    for nm in ("w_ada", "w_in", "w_gate"):
        adamw(nm)
    shapes = [a[nm].shape for nm in SMALL_ORDER]
    res = _adamw("adamw_small", _pack([a[nm] for nm in SMALL_ORDER]), _pack([grads[nm] for nm in SMALL_ORDER]),
                 _pack([a["m_" + nm] for nm in SMALL_ORDER]), _pack([a["v_" + nm] for nm in SMALL_ORDER]))
    for tree, packed in zip((delta, new_m, new_v), res):
        for nm, t in zip(SMALL_ORDER, _unpack(packed, shapes)):
            tree[nm] = t
    return (loss, grad_x[None], *[grads[nm] for nm in WEIGHT_NAMES], *[delta[nm] for nm in WEIGHT_NAMES],
            *[new_m[nm] for nm in WEIGHT_NAMES], *[new_v[nm] for nm in WEIGHT_NAMES])


def kernel(x, c, ctx, c_ctx, w_ada, b_ada, w_in, q_norm_g, k_norm_g, pool_w, pool_scale, sgu_ln_g, sgu_ln_b, sgu_w, sgu_b, conv_w, w_br_attn, w_br_pool, w_br_sgu, w_br_conv, w_gate, b_gate, w_o, ln1_g, ln1_b, w_ff_gate, w_ff_up, w_ff_down, ln2_g, ln2_b, loss_target, m_c_ctx, m_w_ada, m_b_ada, m_w_in, m_q_norm_g, m_k_norm_g, m_pool_w, m_pool_scale, m_sgu_ln_g, m_sgu_ln_b, m_sgu_w, m_sgu_b, m_conv_w, m_w_br_attn, m_w_br_pool, m_w_br_sgu, m_w_br_conv, m_w_gate, m_b_gate, m_w_o, m_ln1_g, m_ln1_b, m_w_ff_gate, m_w_ff_up, m_w_ff_down, m_ln2_g, m_ln2_b, v_c_ctx, v_w_ada, v_b_ada, v_w_in, v_q_norm_g, v_k_norm_g, v_pool_w, v_pool_scale, v_sgu_ln_g, v_sgu_ln_b, v_sgu_w, v_sgu_b, v_conv_w, v_w_br_attn, v_w_br_pool, v_w_br_sgu, v_w_br_conv, v_w_gate, v_b_gate, v_w_o, v_ln1_g, v_ln1_b, v_w_ff_gate, v_w_ff_up, v_w_ff_down, v_ln2_g, v_ln2_b):
    names = list(WEIGHT_NAMES)
    args = dict(zip(
        ["x", "c", "ctx"] + names + ["loss_target"] + ["m_" + n for n in names] + ["v_" + n for n in names],
        (x, c, ctx, c_ctx, w_ada, b_ada, w_in, q_norm_g, k_norm_g, pool_w, pool_scale, sgu_ln_g, sgu_ln_b, sgu_w, sgu_b, conv_w, w_br_attn, w_br_pool, w_br_sgu, w_br_conv, w_gate, b_gate, w_o, ln1_g, ln1_b, w_ff_gate, w_ff_up, w_ff_down, ln2_g, ln2_b, loss_target, m_c_ctx, m_w_ada, m_b_ada, m_w_in, m_q_norm_g, m_k_norm_g, m_pool_w, m_pool_scale, m_sgu_ln_g, m_sgu_ln_b, m_sgu_w, m_sgu_b, m_conv_w, m_w_br_attn, m_w_br_pool, m_w_br_sgu, m_w_br_conv, m_w_gate, m_b_gate, m_w_o, m_ln1_g, m_ln1_b, m_w_ff_gate, m_w_ff_up, m_w_ff_down, m_ln2_g, m_ln2_b, v_c_ctx, v_w_ada, v_b_ada, v_w_in, v_q_norm_g, v_k_norm_g, v_pool_w, v_pool_scale, v_sgu_ln_g, v_sgu_ln_b, v_sgu_w, v_sgu_b, v_conv_w, v_w_br_attn, v_w_br_pool, v_w_br_sgu, v_w_br_conv, v_w_gate, v_b_gate, v_w_o, v_ln1_g, v_ln1_b, v_w_ff_gate, v_w_ff_up, v_w_ff_down, v_ln2_g, v_ln2_b)))
    return _train_step(args)
```

```python
import functools
import math

import jax
import jax.numpy as jnp
from jax import lax
from jax.experimental import pallas as pl
from jax.experimental.pallas import tpu as pltpu

F32 = jnp.float32
BF16 = jnp.bfloat16

N_DEV = 8
MESH_AXES = ("x", "y", "c")
V7X_VMEM_LIMIT_BYTES = 56 * 1024 * 1024

GRID_W = 64
HEAD_DIM = 128
N_HEADS = 8
N_KV_HEADS = 2
KV_GROUP = N_HEADS // N_KV_HEADS
Q_W = N_HEADS * HEAD_DIM
KV_W = N_KV_HEADS * HEAD_DIM
ROPE_THETA = 10000.0
ROPE_AXIS_DIM = HEAD_DIM // 2
POOL_WINDOWS = (2, 4, 8, 16)
GC = 128
N_GROUPS = 4
BR_W = N_GROUPS * GC
SGU_CHUNK = 128
N_BRANCH = 4
LN_EPS = 1e-5
RMS_EPS = 1e-6
OFF_K = Q_W
OFF_V = OFF_K + KV_W
OFF_POOL = OFF_V + KV_W
OFF_U = OFF_POOL + BR_W
OFF_VG = OFF_U + BR_W
OFF_CB = OFF_VG + BR_W
OFF_CC = OFF_CB + BR_W
OFF_CX = OFF_CC + BR_W
IN_W = OFF_CX + BR_W
ATT_SCALE = HEAD_DIM ** -0.5

ADAM_LR = 0.001
ADAM_B1 = 0.9
ADAM_B2 = 0.999
ADAM_EPS = 1e-08
ADAM_WD = 0.01
ADAM_STEP = 10

_NT = (((1,), (1,)), ((), ()))
_NN = (((1,), (0,)), ((), ()))
_TN = (((0,), (0,)), ((), ()))
_DIMS = {"nt": _NT, "nn": _NN, "tn": _TN}


def _cparams(*sem):
    return pltpu.CompilerParams(dimension_semantics=sem, vmem_limit_bytes=V7X_VMEM_LIMIT_BYTES)


def _tile(dim, pref):
    best = None
    t = 128
    while t <= min(dim, pref):
        if dim % t == 0:
            best = t
        t += 128
    return best if best is not None else dim


def _dot(a, b, dims):
    return lax.dot_general(a.astype(BF16), b.astype(BF16), dims, preferred_element_type=F32)


class _Rider:
    def __init__(self, inputs, out_shapes, scratch, start, finish, aliases=None):
        self.inputs, self.out_shapes, self.scratch = list(inputs), list(out_shapes), list(scratch)
        self.start, self.finish = start, finish
        self.aliases = dict(aliases or {})


def _compose(riders):
    inputs, outs, scratch, aliases, spans = [], [], [], {}, []
    for rd in riders:
        i0, o0, s0 = len(inputs), len(outs), len(scratch)
        aliases.update({i0 + p: o0 + q for p, q in rd.aliases.items()})
        inputs += rd.inputs
        outs += rd.out_shapes
        scratch += rd.scratch
        spans.append((slice(i0, len(inputs)), slice(o0, len(outs)), slice(s0, len(scratch))))

    def start(ins, os, sems):
        for rd, (si, so, ss) in zip(riders, spans):
            rd.start(ins[si], os[so], sems[ss])

    def finish(ins, os, sems):
        for rd, (si, so, ss) in zip(riders, spans):
            rd.finish(ins[si], os[so], sems[ss])

    return _Rider(inputs, outs, scratch, start, finish, aliases)


def _pcall(name, body, grid, in_specs, out_specs, out_shape, args, sem, scratch=(), rider=None, prefetch=None,
           aliases=None):
    in_specs, out_specs, out_shape, scratch = list(in_specs), list(out_specs), list(out_shape), list(scratch)
    n_pre = 0 if prefetch is None else 1
    n_in, n_out, n_scr = len(in_specs), len(out_specs), len(scratch)
    r_in, r_out = (len(rider.inputs), len(rider.out_shapes)) if rider is not None else (0, 0)
    any_spec = pl.BlockSpec(memory_space=pl.ANY)
    io_aliases = {n_pre + p: q for p, q in (aliases or {}).items()}
    if rider is not None:
        io_aliases.update({n_pre + n_in + p: n_out + q for p, q in rider.aliases.items()})
        in_specs, out_specs = in_specs + [any_spec] * r_in, out_specs + [any_spec] * r_out
        out_shape, scratch = out_shape + rider.out_shapes, scratch + rider.scratch
        args, sem = (*args, *rider.inputs), ["arbitrary"] * len(grid)

    def wrapped(*refs):
        pre, refs = refs[:n_pre], refs[n_pre:]
        if rider is None:
            return body(*pre, *refs)
        ins, refs = refs[:n_in], refs[n_in:]
        r_ins, refs = refs[:r_in], refs[r_in:]
        outs, refs = refs[:n_out], refs[n_out:]
        r_outs, refs = refs[:r_out], refs[r_out:]
        scr, r_scr = refs[:n_scr], refs[n_scr:]
        first = functools.reduce(jnp.logical_and, [pl.program_id(ax) == 0 for ax in range(len(grid))])
        last = functools.reduce(jnp.logical_and, [pl.program_id(ax) == grid[ax] - 1 for ax in range(len(grid))])

        @pl.when(first)
        def _():
            rider.start(r_ins, r_outs, r_scr)

        body(*pre, *ins, *outs, *scr)

        @pl.when(last)
        def _():
            rider.finish(r_ins, r_outs, r_scr)

    if prefetch is None:
        res = pl.pallas_call(wrapped, name=name, grid=grid, in_specs=in_specs, out_specs=out_specs, out_shape=out_shape,
                             scratch_shapes=scratch, input_output_aliases=io_aliases,
                             compiler_params=_cparams(*sem))(*args)
    else:
        grid_spec = pltpu.PrefetchScalarGridSpec(num_scalar_prefetch=1, grid=grid, in_specs=in_specs,
                                                 out_specs=out_specs, scratch_shapes=scratch)
        res = pl.pallas_call(wrapped, name=name, grid_spec=grid_spec, out_shape=out_shape,
                             input_output_aliases=io_aliases, compiler_params=_cparams(*sem))(prefetch, *args)
    return list(res[:n_out]), list(res[n_out:])


V7X_MM_VMEM_BUDGET = 40 * 1024 * 1024


def _mm_plan(form, m, n, k, a_size, b_size, o_size, has_acc):
    tn = n if (form == "tn" and n <= 2048) else _tile(n, 512)
    rows = sorted({m} | {t for t in range(128, m, 128) if m % t == 0}, reverse=True)
    for tk, min_tm in ((k, 384), (k if k <= 2816 else _tile(k, 2816), 0)):
        nk = k // tk
        for tm in rows:
            need = 2 * (tm * tk * a_size + tn * tk * b_size + tm * tn * o_size) + tm * tn * 4 * (2 if nk > 1 else 1)
            need += 2 * tm * tn * 4 if has_acc else 0
            if need <= V7X_MM_VMEM_BUDGET and tm >= min(min_tm, m):
                return tm, tn, tk
    return _tile(m, 128), tn, tk


def _mm(name, a, b, form, out_dtype, acc=None, rider=None):
    if form == "nt":
        (m, k), (n, k2) = a.shape, b.shape
    elif form == "nn":
        (m, k), (k2, n) = a.shape, b.shape
    else:
        (k, m), (k2, n) = a.shape, b.shape
    assert k == k2, (name, a.shape, b.shape)
    has_acc = acc is not None
    tm, tn, tk = _mm_plan(form, m, n, k, a.dtype.itemsize, b.dtype.itemsize, jnp.dtype(out_dtype).itemsize, has_acc)
    nk = k // tk
    a_spec = {"nt": pl.BlockSpec((tm, tk), lambda i, j, kk: (i, kk)),
              "nn": pl.BlockSpec((tm, tk), lambda i, j, kk: (i, kk)),
              "tn": pl.BlockSpec((tk, tm), lambda i, j, kk: (kk, i))}[form]
    b_spec = {"nt": pl.BlockSpec((tn, tk), lambda i, j, kk: (j, kk)),
              "nn": pl.BlockSpec((tk, tn), lambda i, j, kk: (kk, j)),
              "tn": pl.BlockSpec((tk, tn), lambda i, j, kk: (kk, j))}[form]
    o_spec = pl.BlockSpec((tm, tn), lambda i, j, kk: (i, j))
    dims = _DIMS[form]

    def body(*refs):
        a_ref, b_ref = refs[0], refs[1]
        c_ref = refs[2] if has_acc else None
        o_ref = refs[3] if has_acc else refs[2]

        def finish(r):
            if has_acc:
                r = r + c_ref[...]
            o_ref[...] = r.astype(o_ref.dtype)

        if nk == 1:
            finish(_dot(a_ref[...], b_ref[...], dims))
            return
        acc_ref = refs[-1]
        kk = pl.program_id(2)

        @pl.when(kk == 0)
        def _():
            acc_ref[...] = _dot(a_ref[...], b_ref[...], dims)

        @pl.when(kk > 0)
        def _():
            acc_ref[...] += _dot(a_ref[...], b_ref[...], dims)

        @pl.when(kk == nk - 1)
        def _():
            finish(acc_ref[...])

    in_specs = [a_spec, b_spec] + ([o_spec] if has_acc else [])
    args = (a, b) + ((acc,) if has_acc else ())
    outs, r_outs = _pcall(name, body, (m // tm, n // tn, nk), in_specs, [o_spec],
                          [jax.ShapeDtypeStruct((m, n), out_dtype)], args, ("parallel", "parallel", "arbitrary"),
                          scratch=[pltpu.VMEM((tm, tn), F32)] if nk > 1 else [], rider=rider)
    return outs[0] if rider is None else (outs[0], r_outs)


def _mm_fused(name, a, bs, epilogue, tile_ins, out_dtypes, rider=None):
    (m, k), (n, _) = a.shape, bs[0].shape
    tn = _tile(n, 512)
    tm = None
    for cand in sorted({m} | {t for t in range(128, m, 128) if m % t == 0}, reverse=True):
        per_tile = sum(t.dtype.itemsize for t in tile_ins) + sum(jnp.dtype(d).itemsize for d in out_dtypes)
        need = 2 * (cand * k * a.dtype.itemsize + len(bs) * tn * k * bs[0].dtype.itemsize + cand * tn * per_tile)
        need += (len(bs) + 2) * cand * tn * 4
        if need <= V7X_MM_VMEM_BUDGET:
            tm = cand
            break
    assert tm is not None, name
    n_b, n_t = len(bs), len(tile_ins)
    tile = pl.BlockSpec((tm, tn), lambda i, j: (i, j))

    def body(a_ref, *refs):
        prods = [_dot(a_ref[...], r[...], _NT) for r in refs[:n_b]]
        outs = epilogue(prods, [r[...].astype(F32) for r in refs[n_b:n_b + n_t]])
        for r, o in zip(refs[n_b + n_t:], outs):
            r[...] = o.astype(r.dtype)

    outs, r_outs = _pcall(name, body, (m // tm, n // tn),
                          [pl.BlockSpec((tm, k), lambda i, j: (i, 0))] + [pl.BlockSpec((tn, k), lambda i, j: (j, 0))] * n_b
                          + [tile] * n_t, [tile] * len(out_dtypes),
                          [jax.ShapeDtypeStruct((m, n), d) for d in out_dtypes], (a, *bs, *tile_ins),
                          ("parallel", "parallel"), rider=rider)
    return outs if rider is None else (outs, r_outs)


def _rows(name, fn, n_rows, tm, nbc, row_ins, type_ins, row_outs, acc_outs, rider=None):
    n_ri, n_ti, n_ro, n_ao = len(row_ins), len(type_ins), len(row_outs), len(acc_outs)

    def row_map(i, cb, roff):
        return (jnp.maximum(i - roff, 0), cb)

    def type_map(i):
        return (jnp.where(i >= nbc, 1, 0), 0, 0)

    in_specs, args = [], []
    for arr, cb, width, roff in row_ins:
        in_specs.append(pl.BlockSpec((tm, width), functools.partial(row_map, cb=cb, roff=roff)))
        args.append(arr)
    def shared_map(i):
        return (0, 0, 0)

    for arr in type_ins:
        in_specs.append(pl.BlockSpec((None, 1, arr.shape[-1]), type_map if arr.shape[0] == 2 else shared_map))
        args.append(arr)
    out_shape, out_specs = [], []
    for total, width, dtype, roff in row_outs:
        out_shape.append(jax.ShapeDtypeStruct((total, width), dtype))
        out_specs.append(pl.BlockSpec((tm, width), functools.partial(row_map, cb=0, roff=roff)))
    acc_shared = [isinstance(w, tuple) for w in acc_outs]
    for width in acc_outs:
        if isinstance(width, tuple):
            out_shape.append(jax.ShapeDtypeStruct((1, 1, width[0]), F32))
            out_specs.append(pl.BlockSpec((None, 1, width[0]), shared_map))
        else:
            out_shape.append(jax.ShapeDtypeStruct((2, 1, width), F32))
            out_specs.append(pl.BlockSpec((None, 1, width), type_map))
    n_in = n_ri + n_ti

    def body(*refs):
        i = pl.program_id(0)
        outs = fn(*[r[...].astype(F32) for r in refs[:n_in]])
        if not isinstance(outs, (tuple, list)):
            outs = (outs,)
        assert len(outs) == n_ro + n_ao, (name, len(outs))
        for r, o in zip(refs[n_in:n_in + n_ro], outs[:n_ro]):
            r[...] = o.astype(r.dtype)
        if n_ao:
            for r, o, shared in zip(refs[n_in + n_ro:], outs[n_ro:], acc_shared):
                first = i == 0 if shared else jnp.logical_or(i == 0, i == nbc)
                o = jnp.broadcast_to(o.astype(F32), r.shape)

                @pl.when(first)
                def _(r=r, o=o):
                    r[...] = o

                @pl.when(jnp.logical_not(first))
                def _(r=r, o=o):
                    r[...] += o

    outs, r_outs = _pcall(name, body, (n_rows // tm,), in_specs, out_specs, out_shape, args, ("arbitrary",),
                          rider=rider)
    return outs if rider is None else (outs, r_outs)


def _vjp_fn(f, n_row, n_cot, keep=None):
    def g(*args):
        prim = args[:n_row] + args[n_row + n_cot:]
        cots = args[n_row:n_row + n_cot]
        out, vjp = jax.vjp(f, *prim)
        grads = vjp(tuple(cots) if isinstance(out, (tuple, list)) else cots[0])
        return grads if keep is None else tuple(grads[j] for j in keep)
    return g


def _typed(v):
    return v.reshape(1, 1, -1)


def _ln(x, g, b):
    mu = jnp.mean(x, axis=-1, keepdims=True)
    var = jnp.mean(jnp.square(x - mu), axis=-1, keepdims=True)
    return (x - mu) * lax.rsqrt(var + LN_EPS) * g + b


def _f_mod(x, sc, sh):
    return x * (1.0 + sc) + sh


def _make_f_ln(alpha, with_mod):
    def f(x, o, gate, lng, lnb, *mod):
        xn = _ln(alpha * x + gate * o, lng, lnb)
        if with_mod:
            sc, sh = mod
            return xn, xn * (1.0 + sc) + sh
        return xn
    return f


@jax.custom_vjp
def _rot(y):
    lane = lax.broadcasted_iota(jnp.int32, y.shape, 1)
    return jnp.where(lane % 64 < 32, pltpu.roll(y, 96, axis=1), pltpu.roll(y, 32, axis=1))


_rot.defvjp(lambda y: (_rot(y), None), lambda _, g: (_rot(g),))


def _f_prep(p, cos, sin, qg, kg):
    def head(xh, g):
        ms = jnp.mean(jnp.square(xh), axis=-1, keepdims=True)
        y = xh * lax.rsqrt(ms + RMS_EPS) * g
        return y * cos + _rot(y) * sin
    q = jnp.concatenate([head(p[:, h * HEAD_DIM:(h + 1) * HEAD_DIM], qg) for h in range(N_HEADS)], axis=1)
    k = jnp.concatenate([head(p[:, OFF_K + h * HEAD_DIM:OFF_K + (h + 1) * HEAD_DIM], kg)
                         for h in range(N_KV_HEADS)], axis=1)
    return q, k, p[:, OFF_V:OFF_POOL]


def _f_gate(g, t0, t1, t2, t3, b):
    d = t0.shape[-1]
    ts = (t0, t1, t2, t3)
    terms = [jax.nn.sigmoid(g[:, k * d:(k + 1) * d] + b[:, k * d:(k + 1) * d]) * ts[k] for k in range(N_BRANCH)]
    return terms[0] + terms[1] + terms[2] + terms[3]


def _f_swiglu(a, b):
    return jax.nn.silu(a) * b


def _softmax(raw):
    e = jnp.exp2((raw - jnp.max(raw, axis=-1, keepdims=True)) * (ATT_SCALE * math.log2(math.e)))
    return e / jnp.sum(e, axis=-1, keepdims=True)


def _attn_fwd(name, q, k, v, rc, ctx_queries, tq=256, rider=None):
    r = q.shape[0]
    assert rc % tq == 0 and r % tq == 0
    nqc = rc // tq

    def body(q_ref, k_ref, v_ref, o_ref, p_ref):
        qi = pl.program_id(1)

        def attend(nk):
            p = _softmax(_dot(q_ref[...], k_ref[0:nk, :], _NT)).astype(BF16)
            p_ref[:, 0:nk] = p
            o_ref[...] = _dot(p, v_ref[0:nk, :], _NN).astype(o_ref.dtype)

        @pl.when(qi < nqc)
        def _():
            if ctx_queries:
                attend(rc)
            else:
                o_ref[...] = jnp.zeros_like(o_ref)

        @pl.when(qi >= nqc)
        def _():
            attend(r)

    outs, r_outs = _pcall(
        name, body, (N_HEADS, r // tq),
        [pl.BlockSpec((tq, HEAD_DIM), lambda h, i: (i, h)),
         pl.BlockSpec((r, HEAD_DIM), lambda h, i: (0, h // KV_GROUP)),
         pl.BlockSpec((r, HEAD_DIM), lambda h, i: (0, h // KV_GROUP))],
        [pl.BlockSpec((tq, HEAD_DIM), lambda h, i: (i, h)), pl.BlockSpec((None, tq, r), lambda h, i: (h, i, 0))],
        [jax.ShapeDtypeStruct((r, Q_W), BF16), jax.ShapeDtypeStruct((N_HEADS, r, r), BF16)], (q, k, v),
        ("parallel", "parallel"), rider=rider)
    return outs if rider is None else (outs, r_outs)


def _attn_bwd(name, q, k, v, pw, do, rc, ctx_queries, tq=256, rider=None):
    r = q.shape[0]
    nqc = rc // tq

    def body(q_ref, k_ref, v_ref, p_ref, do_ref, dq_ref, dk_ref, dv_ref):
        g, qi = pl.program_id(1), pl.program_id(2)

        @pl.when(jnp.logical_and(g == 0, qi == 0))
        def _():
            dk_ref[...] = jnp.zeros_like(dk_ref)
            dv_ref[...] = jnp.zeros_like(dv_ref)

        def grad(nk):
            qb, kb, vb = q_ref[...], k_ref[0:nk, :], v_ref[0:nk, :]
            dob = do_ref[...].astype(BF16)
            p = p_ref[:, 0:nk].astype(F32)
            dv_ref[0:nk, :] += _dot(p, dob, _TN)
            dp = _dot(dob, vb, _NT)
            ds = p * (dp - jnp.sum(dp * p, axis=-1, keepdims=True))
            dq_ref[...] = _dot(ds, kb, _NN) * ATT_SCALE
            dk_ref[0:nk, :] += _dot(ds, qb, _TN) * ATT_SCALE

        @pl.when(qi < nqc)
        def _():
            if ctx_queries:
                grad(rc)
            else:
                dq_ref[...] = jnp.zeros_like(dq_ref)

        @pl.when(qi >= nqc)
        def _():
            grad(r)

    def qmap(kv, g, i):
        return (i, kv * KV_GROUP + g)

    def kvmap(kv, g, i):
        return (0, kv)

    outs, r_outs = _pcall(
        name, body, (N_KV_HEADS, KV_GROUP, r // tq),
        [pl.BlockSpec((tq, HEAD_DIM), qmap), pl.BlockSpec((r, HEAD_DIM), kvmap), pl.BlockSpec((r, HEAD_DIM), kvmap),
         pl.BlockSpec((None, tq, r), lambda kv, g, i: (kv * KV_GROUP + g, i, 0)), pl.BlockSpec((tq, HEAD_DIM), qmap)],
        [pl.BlockSpec((tq, HEAD_DIM), qmap), pl.BlockSpec((r, HEAD_DIM), kvmap), pl.BlockSpec((r, HEAD_DIM), kvmap)],
        [jax.ShapeDtypeStruct((r, Q_W), F32), jax.ShapeDtypeStruct((r, KV_W), F32),
         jax.ShapeDtypeStruct((r, KV_W), F32)],
        (q, k, v, pw, do), ("arbitrary", "arbitrary", "arbitrary"), rider=rider)
    return outs if rider is None else (outs, r_outs)


def _segments(shape, rc):
    t = lax.broadcasted_iota(jnp.int32, shape, 0)
    lo = jnp.where(t < rc, 0, rc)
    hi = jnp.where(t < rc, rc, shape[0])
    return t, lo, hi


def _shifted(x, o, t, lo, hi):
    n = x.shape[0]
    sh = pltpu.roll(x, (-o) % n, axis=0)
    return jnp.where(jnp.logical_and(t + o >= lo, t + o < hi), sh, 0.0)


def _winsum(x, left, right, t, lo, hi):
    acc = x
    for o in range(-left, right + 1):
        if o != 0:
            acc = acc + _shifted(x, o, t, lo, hi)
    return acc


def _pool_parts(z, g, t, lo, hi):
    w = POOL_WINDOWS[g]
    left = w // 2
    right = w - 1 - left
    count = (jnp.minimum(t + right + 1, hi) - jnp.maximum(t - left, lo)).astype(F32)
    return _winsum(z, left, right, t, lo, hi) / count - z, count, left, right


def _pool_fwd(name, p, pool_w, pool_scale, rc):
    r = p.shape[0]

    def body(z_ref, w_ref, s_ref, y_ref):
        t, lo, hi = _segments((r, GC), rc)
        for g in range(N_GROUPS):
            cols = slice(g * GC, (g + 1) * GC)
            d, _, _, _ = _pool_parts(z_ref[:, cols], g, t, lo, hi)
            y_ref[:, cols] = (_dot(d, w_ref[g], _NN) * s_ref[:, cols]).astype(y_ref.dtype)

    return pl.pallas_call(
        body, name=name, grid=(1,),
        in_specs=[pl.BlockSpec((r, BR_W), lambda i: (0, OFF_POOL // BR_W)),
                  pl.BlockSpec((N_GROUPS, GC, GC), lambda i: (0, 0, 0)),
                  pl.BlockSpec((1, BR_W), lambda i: (0, 0))],
        out_specs=pl.BlockSpec((r, BR_W), lambda i: (0, 0)),
        out_shape=jax.ShapeDtypeStruct((r, BR_W), BF16),
        compiler_params=_cparams("arbitrary"),
    )(p, pool_w, pool_scale.reshape(1, BR_W))


def _pool_bwd(name, p, pool_w, pool_scale, dy, rc):
    r = p.shape[0]

    def body(z_ref, w_ref, s_ref, dy_ref, dz_ref, dw_ref, ds_ref):
        t, lo, hi = _segments((r, GC), rc)
        for g in range(N_GROUPS):
            cols = slice(g * GC, (g + 1) * GC)
            d, count, left, right = _pool_parts(z_ref[:, cols], g, t, lo, hi)
            dyg = dy_ref[:, cols]
            ds_ref[:, cols] = jnp.sum(dyg * _dot(d, w_ref[g], _NN), axis=0, keepdims=True)
            dlin = dyg * s_ref[:, cols]
            dw_ref[g] = _dot(d, dlin, _TN)
            dd = _dot(dlin, w_ref[g], _NT)
            dz_ref[:, cols] = (_winsum(dd / count, right, left, t, lo, hi) - dd).astype(dz_ref.dtype)

    return pl.pallas_call(
        body, name=name, grid=(1,),
        in_specs=[pl.BlockSpec((r, BR_W), lambda i: (0, OFF_POOL // BR_W)),
                  pl.BlockSpec((N_GROUPS, GC, GC), lambda i: (0, 0, 0)),
                  pl.BlockSpec((1, BR_W), lambda i: (0, 0)),
                  pl.BlockSpec((r, BR_W), lambda i: (0, 0))],
        out_specs=[pl.BlockSpec((r, BR_W), lambda i: (0, 0)),
                   pl.BlockSpec((N_GROUPS, GC, GC), lambda i: (0, 0, 0)),
                   pl.BlockSpec((1, BR_W), lambda i: (0, 0))],
        out_shape=[jax.ShapeDtypeStruct((r, BR_W), BF16), jax.ShapeDtypeStruct((N_GROUPS, GC, GC), F32),
                   jax.ShapeDtypeStruct((1, BR_W), F32)],
        compiler_params=_cparams("arbitrary"),
    )(p, pool_w, pool_scale.reshape(1, BR_W), dy)


def _f_sgu_v(pvg, lng, lnb):
    return _ln(jax.nn.gelu(pvg), lng, lnb)


def _sgu_fwd(name, p, ln_g, ln_b, sgu_w, sgu_b):
    r = p.shape[0]

    def body(pu_ref, pv_ref, g_ref, b_ref, w_ref, sb_ref, y_ref):
        vn = _f_sgu_v(pv_ref[...], g_ref[...], b_ref[...])
        u = jax.nn.gelu(pu_ref[...])
        for g in range(N_GROUPS):
            cols = slice(g * GC, (g + 1) * GC)
            s = _dot(w_ref[g], vn[:, cols], _NN) + sb_ref[g]
            y_ref[:, cols] = (u[:, cols] * s).astype(y_ref.dtype)

    return pl.pallas_call(
        body, name=name, grid=(r // SGU_CHUNK,),
        in_specs=[pl.BlockSpec((SGU_CHUNK, BR_W), lambda i: (i, OFF_U // BR_W)),
                  pl.BlockSpec((SGU_CHUNK, BR_W), lambda i: (i, OFF_VG // BR_W)),
                  pl.BlockSpec((1, BR_W), lambda i: (0, 0)), pl.BlockSpec((1, BR_W), lambda i: (0, 0)),
                  pl.BlockSpec((N_GROUPS, GC, GC), lambda i: (0, 0, 0)),
                  pl.BlockSpec((N_GROUPS, SGU_CHUNK, 1), lambda i: (0, 0, 0))],
        out_specs=pl.BlockSpec((SGU_CHUNK, BR_W), lambda i: (i, 0)),
        out_shape=jax.ShapeDtypeStruct((r, BR_W), BF16),
        compiler_params=_cparams("parallel"),
    )(p, p, ln_g.reshape(1, BR_W), ln_b.reshape(1, BR_W), sgu_w, sgu_b.reshape(N_GROUPS, SGU_CHUNK, 1))


def _sgu_bwd(name, p, ln_g, ln_b, sgu_w, sgu_b, dy):
    r = p.shape[0]

    def body(pu_ref, pv_ref, g_ref, b_ref, w_ref, sb_ref, dy_ref, dp_ref, dg_ref, db_ref, dw_ref, dsb_ref):
        i = pl.program_id(0)

        @pl.when(i == 0)
        def _():
            for ref in (dg_ref, db_ref, dw_ref, dsb_ref):
                ref[...] = jnp.zeros_like(ref)

        vn, vjp_v = jax.vjp(_f_sgu_v, pv_ref[...], g_ref[...], b_ref[...])
        u, vjp_u = jax.vjp(jax.nn.gelu, pu_ref[...])
        dy = dy_ref[...]
        du, dvn = [], []
        for g in range(N_GROUPS):
            cols = slice(g * GC, (g + 1) * GC)
            s = _dot(w_ref[g], vn[:, cols], _NN) + sb_ref[g]
            du.append(dy[:, cols] * s)
            ds = dy[:, cols] * u[:, cols]
            dsb_ref[g] += jnp.sum(ds, axis=1, keepdims=True)
            dw_ref[g] += _dot(ds, vn[:, cols], _NT)
            dvn.append(_dot(w_ref[g], ds, _TN))
        (dpu,) = vjp_u(jnp.concatenate(du, axis=1))
        dpv, dg, db = vjp_v(jnp.concatenate(dvn, axis=1))
        dp_ref[:, 0:BR_W] = dpu.astype(dp_ref.dtype)
        dp_ref[:, BR_W:2 * BR_W] = dpv.astype(dp_ref.dtype)
        dg_ref[...] += dg
        db_ref[...] += db

    vec = pl.BlockSpec((1, BR_W), lambda i: (0, 0))
    wsp = pl.BlockSpec((N_GROUPS, GC, GC), lambda i: (0, 0, 0))
    bsp = pl.BlockSpec((N_GROUPS, SGU_CHUNK, 1), lambda i: (0, 0, 0))
    return pl.pallas_call(
        body, name=name, grid=(r // SGU_CHUNK,),
        in_specs=[pl.BlockSpec((SGU_CHUNK, BR_W), lambda i: (i, OFF_U // BR_W)),
                  pl.BlockSpec((SGU_CHUNK, BR_W), lambda i: (i, OFF_VG // BR_W)),
                  vec, vec, wsp, bsp, pl.BlockSpec((SGU_CHUNK, BR_W), lambda i: (i, 0))],
        out_specs=[pl.BlockSpec((SGU_CHUNK, 2 * BR_W), lambda i: (i, 0)), vec, vec, wsp, bsp],
        out_shape=[jax.ShapeDtypeStruct((r, 2 * BR_W), BF16), jax.ShapeDtypeStruct((1, BR_W), F32),
                   jax.ShapeDtypeStruct((1, BR_W), F32), jax.ShapeDtypeStruct((N_GROUPS, GC, GC), F32),
                   jax.ShapeDtypeStruct((N_GROUPS, SGU_CHUNK, 1), F32)],
        compiler_params=_cparams("arbitrary"),
    )(p, p, ln_g.reshape(1, BR_W), ln_b.reshape(1, BR_W), sgu_w, sgu_b.reshape(N_GROUPS, SGU_CHUNK, 1), dy)


def _conv_w8(conv_w):
    return jnp.concatenate([conv_w, jnp.zeros((8 - conv_w.shape[0], conv_w.shape[1]), F32)], axis=0)


def _conv_fwd(name, p, conv_w, rc):
    r = p.shape[0]

    def body(cb_ref, cc_ref, cx_ref, w_ref, y_ref):
        t, lo, hi = _segments((r, GC), rc)
        z = cc_ref[...] * cx_ref[...]
        w = w_ref[...]
        c = _shifted(z, -1, t, lo, hi) * w[0:1] + z * w[1:2] + _shifted(z, 1, t, lo, hi) * w[2:3]
        y_ref[...] = (cb_ref[...] * c).astype(y_ref.dtype)

    nb = OFF_CB // GC
    return pl.pallas_call(
        body, name=name, grid=(N_GROUPS,),
        in_specs=[pl.BlockSpec((r, GC), lambda j: (0, nb + j)),
                  pl.BlockSpec((r, GC), lambda j: (0, nb + N_GROUPS + j)),
                  pl.BlockSpec((r, GC), lambda j: (0, nb + 2 * N_GROUPS + j)),
                  pl.BlockSpec((8, GC), lambda j: (0, j))],
        out_specs=pl.BlockSpec((r, GC), lambda j: (0, j)),
        out_shape=jax.ShapeDtypeStruct((r, BR_W), BF16),
        compiler_params=_cparams("parallel"),
    )(p, p, p, _conv_w8(conv_w))


def _conv_bwd(name, p, conv_w, dy, rc):
    r = p.shape[0]

    def body(cb_ref, cc_ref, cx_ref, w_ref, dy_ref, dcb_ref, dcc_ref, dcx_ref, dw_ref):
        t, lo, hi = _segments((r, GC), rc)
        cc, cx, w, dy = cc_ref[...], cx_ref[...], w_ref[...], dy_ref[...]
        z = cc * cx
        zp, zn = _shifted(z, -1, t, lo, hi), _shifted(z, 1, t, lo, hi)
        dcb_ref[...] = (dy * (zp * w[0:1] + z * w[1:2] + zn * w[2:3])).astype(dcb_ref.dtype)
        dc = dy * cb_ref[...]
        dw_ref[...] = jnp.concatenate(
            [jnp.sum(dc * zp, axis=0, keepdims=True), jnp.sum(dc * z, axis=0, keepdims=True),
             jnp.sum(dc * zn, axis=0, keepdims=True), jnp.zeros((5, GC), F32)], axis=0)
        dz = dc * w[1:2] + _shifted(dc, 1, t, lo, hi) * w[0:1] + _shifted(dc, -1, t, lo, hi) * w[2:3]
        dcc_ref[...] = (dz * cx).astype(dcc_ref.dtype)
        dcx_ref[...] = (dz * cc).astype(dcx_ref.dtype)

    nb = OFF_CB // GC
    return pl.pallas_call(
        body, name=name, grid=(N_GROUPS,),
        in_specs=[pl.BlockSpec((r, GC), lambda j: (0, nb + j)),
                  pl.BlockSpec((r, GC), lambda j: (0, nb + N_GROUPS + j)),
                  pl.BlockSpec((r, GC), lambda j: (0, nb + 2 * N_GROUPS + j)),
                  pl.BlockSpec((8, GC), lambda j: (0, j)),
                  pl.BlockSpec((r, GC), lambda j: (0, j))],
        out_specs=[pl.BlockSpec((r, GC), lambda j: (0, j))] * 3 + [pl.BlockSpec((8, GC), lambda j: (0, j))],
        out_shape=[jax.ShapeDtypeStruct((r, BR_W), BF16)] * 3 + [jax.ShapeDtypeStruct((8, BR_W), F32)],
        compiler_params=_cparams("parallel"),
    )(p, p, p, _conv_w8(conv_w), dy)


def _rope_tables(rc, n):
    rows = n // GRID_W
    row = jnp.repeat(jnp.arange(rows), GRID_W).astype(F32)
    col = jnp.tile(jnp.arange(GRID_W), rows).astype(F32)
    inv = ROPE_THETA ** (-jnp.arange(0, ROPE_AXIS_DIM, 2, dtype=F32) / ROPE_AXIS_DIM)
    ang_r, ang_c = row[:, None] * inv, col[:, None] * inv
    cos = jnp.concatenate([jnp.cos(ang_r), jnp.cos(ang_r), jnp.cos(ang_c), jnp.cos(ang_c)], axis=1)
    sin = jnp.concatenate([-jnp.sin(ang_r), jnp.sin(ang_r), -jnp.sin(ang_c), jnp.sin(ang_c)], axis=1)
    cos = jnp.concatenate([jnp.ones((rc, HEAD_DIM), F32), cos], axis=0)
    sin = jnp.concatenate([jnp.zeros((rc, HEAD_DIM), F32), sin], axis=0)
    return cos, sin


MOD_NAMES = ("sh1", "sc1", "g1", "sh2", "sc2", "g2")


def _local_step(xin, target, mod, comm, sp, rc, alpha):
    def carrying(fn):
        def call(name, *args, **kw):
            rider = comm.rider(name)
            if rider is None:
                return fn(name, *args, **kw)
            res, r_outs = fn(name, *args, rider=rider, **kw)
            comm.deliver(name, r_outs)
            return res
        return call

    mm, rows, attn_fwd, attn_bwd = carrying(_mm), carrying(_rows), carrying(_attn_fwd), carrying(_attn_bwd)
    mm_fused = carrying(_mm_fused)
    r, d = xin.shape
    n_layers = mod.shape[0]
    tm_n, tm_w = 256, 128
    nbc_n, nbc_w = rc // tm_n, rc // tm_w
    cos, sin = _rope_tables(rc, r - rc)
    mp = mod.reshape(n_layers, 2, 6, 1, d)
    mods = [{nm: mp[i, :, j] for j, nm in enumerate(MOD_NAMES)} for i in range(n_layers)]
    f_ln_mod, f_ln_last = _make_f_ln(alpha, True), _make_f_ln(alpha, False)

    def whole(arr, roff=0):
        return (arr, 0, arr.shape[1], roff)

    (hb,) = rows("mod_in", _f_mod, r, tm_n, nbc_n, [whole(xin)], [mods[0]["sc1"], mods[0]["sh1"]],
                 [(r, d, BF16, 0)], [])
    saved = []
    x = xin
    for i in range(n_layers):
        last = i == n_layers - 1
        w, s, m = functools.partial(comm.weight, i), sp[i], mods[i]
        sv = {"x": x, "hb": hb}
        p = mm(f"l{i}_in", hb, w("in_t"), "nt", F32)
        q, k, v = rows(f"l{i}_prep", _f_prep, r, tm_n, nbc_n,
                       [(p, 0, OFF_POOL, 0), whole(cos), whole(sin)], [_typed(s["q_norm_g"]), _typed(s["k_norm_g"])],
                       [(r, Q_W, BF16, 0), (r, KV_W, BF16, 0), (r, KV_W, BF16, 0)], [])
        att, sv["pw"] = attn_fwd(f"l{i}_attn", q, k, v, rc, not last)
        ys = [att,
              _pool_fwd(f"l{i}_pool", p, s["pool_w"], s["pool_scale"], rc),
              _sgu_fwd(f"l{i}_sgu", p, s["sgu_ln_g"], s["sgu_ln_b"], s["sgu_w"], s["sgu_b"]),
              _conv_fwd(f"l{i}_conv", p, s["conv_w"], rc)]
        ts = [mm(f"l{i}_br{kk}", ys[kk], w(f"br{kk}"), "nt", BF16) for kk in range(N_BRANCH)]
        gpre = mm(f"l{i}_gate", hb, w("gate_t"), "nt", BF16)
        (mg,) = rows(f"l{i}_merge", _f_gate, r, tm_w, nbc_w, [whole(gpre)] + [whole(t) for t in ts],
                     [_typed(s["b_gate"])], [(r, d, BF16, 0)], [])
        o = mm(f"l{i}_o", mg, w("o"), "nn", F32)
        x1, h2b = rows(f"l{i}_ln1", f_ln_mod, r, tm_n, nbc_n, [whole(x), whole(o)],
                       [m["g1"], _typed(s["ln1_g"]), _typed(s["ln1_b"]), m["sc2"], m["sh2"]],
                       [(r, d, F32, 0), (r, d, BF16, 0)], [])
        af, bf, f = mm_fused(f"l{i}_ffgu", h2b, [w("ffg_t"), w("ffu_t")],
                             lambda prods, _: (prods[0], prods[1], _f_swiglu(prods[0], prods[1])), [], [BF16, BF16, BF16])
        o2 = mm(f"l{i}_ffd", f, w("ffd"), "nn", F32)
        if last:
            (x2,) = rows(f"l{i}_ln2", f_ln_last, r, tm_n, nbc_n, [whole(x1), whole(o2)],
                         [m["g2"], _typed(s["ln2_g"]), _typed(s["ln2_b"])], [(r, d, F32, 0)], [])
            hb = None
        else:
            nx = mods[i + 1]
            x2, hb = rows(f"l{i}_ln2", f_ln_mod, r, tm_n, nbc_n, [whole(x1), whole(o2)],
                          [m["g2"], _typed(s["ln2_g"]), _typed(s["ln2_b"]), nx["sc1"], nx["sh1"]],
                          [(r, d, F32, 0), (r, d, BF16, 0)], [])
        sv.update(p=p, gpre=gpre, q=q, k=k, v=v, ys=ys, ts=ts, mg=mg, o=o, x1=x1, h2b=h2b, af=af, bf=bf, f=f, o2=o2)
        saved.append(sv)
        x = x2

    lat = jnp.concatenate([jnp.zeros((1, 1, 128), F32), jnp.ones((1, 1, 128), F32)], axis=0)

    def f_loss(xb, tb, msk):
        diff = (xb - tb) * msk[:, 0:1]
        part = jnp.sum(jnp.mean(jnp.square(diff), axis=-1, keepdims=True), axis=0, keepdims=True)
        return diff * (1.0 / d), jnp.broadcast_to(part, (1, 128))

    dx_direct, loss_acc = rows("loss", f_loss, r, tm_n, nbc_n, [whole(x), whole(target, nbc_n)], [lat],
                               [(r, d, F32, 0)], [128])
    loss = 0.5 * loss_acc[1, 0, 0]

    dmods = [dict() for _ in range(n_layers)]
    dsp = [dict() for _ in range(n_layers)]
    dh = None

    def small_done(j):
        ds = dict(dsp[j])
        for nm in ("ln1_g", "ln1_b", "ln2_g", "ln2_b", "b_gate", "q_norm_g", "k_norm_g"):
            ds[nm] = ds[nm][0, 0]
        for nm in ("pool_scale", "sgu_ln_g", "sgu_ln_b"):
            ds[nm] = ds[nm].reshape(-1)
        ds["sgu_b"] = ds["sgu_b"].reshape(N_GROUPS, SGU_CHUNK)
        comm.small_ready(j, jnp.concatenate([dmods[j][nm][:, 0, :] for nm in MOD_NAMES], axis=-1), ds)

    for i in reversed(range(n_layers)):
        last = i == n_layers - 1
        w, s, m, sv = functools.partial(comm.weight, i), sp[i], mods[i], saved[i]
        dm, dw, ds = dmods[i], {}, dsp[i]
        ln2 = [m["g2"], _typed(s["ln2_g"]), _typed(s["ln2_b"])]
        if last:
            res = rows(f"l{i}_ln2_bwd", _vjp_fn(f_ln_last, 2, 1), r, tm_n, nbc_n,
                       [whole(sv["x1"]), whole(sv["o2"]), whole(dx_direct)], ln2,
                       [(r, d, F32, 0), (r, d, BF16, 0)], [d, (d,), (d,)])
            dx1, do2, dm["g2"], dlg, dlb = res
        else:
            nx = mods[i + 1]
            res = rows(f"l{i}_ln2_bwd", _vjp_fn(f_ln_mod, 2, 2), r, tm_n, nbc_n,
                       [whole(sv["x1"]), whole(sv["o2"]), whole(dx_direct), whole(dh)],
                       ln2 + [nx["sc1"], nx["sh1"]],
                       [(r, d, F32, 0), (r, d, BF16, 0)], [d, (d,), (d,), d, d])
            dx1, do2, dm["g2"], dlg, dlb, dmods[i + 1]["sc1"], dmods[i + 1]["sh1"] = res
            small_done(i + 1)
        ds["ln2_g"], ds["ln2_b"] = dlg, dlb
        dab, dbb = mm_fused(f"l{i}_dF", do2, [w("ffd")],
                            lambda prods, tiles: jax.vjp(_f_swiglu, *tiles)[1](prods[0]), [sv["af"], sv["bf"]],
                            [BF16, BF16])
        comm.grads(i, {"ffd": mm(f"l{i}_dWffd", sv["f"], do2, "tn", BF16)})
        comm.grads(i, {"ffg_t": mm(f"l{i}_dWffg", dab, sv["h2b"], "tn", BF16)})
        comm.grads(i, {"ffu_t": mm(f"l{i}_dWffu", dbb, sv["h2b"], "tn", BF16)})
        dh2 = mm(f"l{i}_dh2a", dab, w("ffg_t"), "nn", F32)
        dh2 = mm(f"l{i}_dh2b", dbb, w("ffu_t"), "nn", F32, acc=dh2)
        res = rows(f"l{i}_ln1_bwd", _vjp_fn(f_ln_mod, 2, 2), r, tm_n, nbc_n,
                   [whole(sv["x"]), whole(sv["o"]), whole(dx1), whole(dh2)],
                   [m["g1"], _typed(s["ln1_g"]), _typed(s["ln1_b"]), m["sc2"], m["sh2"]],
                   [(r, d, F32, 0), (r, d, BF16, 0)], [d, (d,), (d,), d, d])
        dx_direct, do, dm["g1"], ds["ln1_g"], ds["ln1_b"], dm["sc2"], dm["sh2"] = res
        dmg = mm(f"l{i}_dMg", do, w("o"), "nt", F32)
        comm.grads(i, {"o": mm(f"l{i}_dWo", sv["mg"], do, "tn", BF16)})
        res = rows(f"l{i}_merge_bwd", _vjp_fn(_f_gate, 5, 1), r, tm_w, nbc_w,
                   [whole(sv["gpre"])] + [whole(t) for t in sv["ts"]] + [whole(dmg)], [_typed(s["b_gate"])],
                   [(r, N_BRANCH * d, BF16, 0)] + [(r, d, BF16, 0)] * N_BRANCH, [(N_BRANCH * d,)])
        dgb, dts, ds["b_gate"] = res[0], res[1:1 + N_BRANCH], res[1 + N_BRANCH]
        comm.grads(i, {"gate_t": mm(f"l{i}_dWgate", dgb, sv["hb"], "tn", BF16)})
        dys = [mm(f"l{i}_dY{kk}", dts[kk], w(f"br{kk}"), "nn", F32) for kk in range(N_BRANCH)]
        for kk in range(N_BRANCH):
            comm.grads(i, {f"br{kk}": mm(f"l{i}_dWbr{kk}", dts[kk], sv["ys"][kk], "tn", BF16)})
        dq, dk, dv = attn_bwd(f"l{i}_attn_bwd", sv["q"], sv["k"], sv["v"], sv["pw"], dys[0], rc, not last)
        res = rows(f"l{i}_prep_bwd", _vjp_fn(_f_prep, 3, 3, keep=(0, 3, 4)), r, tm_n, nbc_n,
                   [(sv["p"], 0, OFF_POOL, 0), whole(cos), whole(sin), whole(dq), whole(dk), whole(dv)],
                   [_typed(s["q_norm_g"]), _typed(s["k_norm_g"])],
                   [(r, OFF_POOL, BF16, 0)], [(HEAD_DIM,), (HEAD_DIM,)])
        dp_qkv, ds["q_norm_g"], ds["k_norm_g"] = res
        dp_pool, ds["pool_w"], ds["pool_scale"] = _pool_bwd(f"l{i}_pool_bwd", sv["p"], s["pool_w"], s["pool_scale"],
                                                            dys[1], rc)
        dp_sgu, ds["sgu_ln_g"], ds["sgu_ln_b"], ds["sgu_w"], ds["sgu_b"] = _sgu_bwd(
            f"l{i}_sgu_bwd", sv["p"], s["sgu_ln_g"], s["sgu_ln_b"], s["sgu_w"], s["sgu_b"], dys[2])
        dp_cb, dp_cc, dp_cx, dcw = _conv_bwd(f"l{i}_conv_bwd", sv["p"], s["conv_w"], dys[3], rc)
        ds["conv_w"] = dcw[0:3]
        dpb = jnp.concatenate([dp_qkv, dp_pool, dp_sgu, dp_cb, dp_cc, dp_cx], axis=1)
        comm.grads(i, {"in_t": mm(f"l{i}_dWin", dpb, sv["hb"], "tn", BF16)})
        dh = mm(f"l{i}_dhb_a", dpb, w("in_t"), "nn", F32)
        dh = mm(f"l{i}_dhb_b", dgb, w("gate_t"), "nn", F32, acc=dh)

    def f_mod_bwd(xb, ddir, dhb, sc, sh):
        _, vjp = jax.vjp(_f_mod, xb, sc, sh)
        dxb, dsc, dsh = vjp(dhb)
        return dxb + ddir, dsc, dsh

    grad_x, dmods[0]["sc1"], dmods[0]["sh1"] = rows(
        "mod_in_bwd", f_mod_bwd, r, tm_n, nbc_n, [whole(xin), whole(dx_direct), whole(dh)],
        [mods[0]["sc1"], mods[0]["sh1"]], [(r - rc, d, F32, nbc_n)], [d, d])
    small_done(0)
    return loss, grad_x


def _direct_rider(src):
    def peers():
        mx, my, mc = [lax.axis_index(a) for a in MESH_AXES]
        out = []
        for kk in range(1, N_DEV):
            px = 1 - mx if kk & 4 else mx
            py = 1 - my if kk & 2 else my
            pc = 1 - mc if kk & 1 else mc
            out.append(((px, py, pc), 4 * px + 2 * py + pc))
        return 4 * mx + 2 * my + mc, out

    def start(ins, outs, sems):
        send, recv, loc = sems
        me, others = peers()
        pltpu.make_async_copy(ins[0], outs[0].at[me], loc.at[0]).start()
        for j, (peer, _) in enumerate(others):
            _rcopy(ins[0], outs[0].at[me], send, recv, j, peer).start()

    def finish(ins, outs, sems):
        send, recv, loc = sems
        me, others = peers()
        for j, (peer, peer_l) in enumerate(others):
            cp = _rcopy(ins[0], outs[0].at[peer_l], send, recv, j, peer)
            cp.wait_recv()
            cp.wait_send()
        pltpu.make_async_copy(ins[0], outs[0].at[me], loc.at[0]).wait()

    return _Rider([src], [jax.ShapeDtypeStruct((N_DEV,) + src.shape, src.dtype)],
                  _sem_scratch(N_DEV - 1, N_DEV - 1, 1), start, finish)


def _mesh_place():
    mx, my, mc = [lax.axis_index(a) for a in MESH_AXES]
    chips = [(1 - mx, my), (mx, 1 - my), (1 - mx, 1 - my)]

    def lid(px, py, pc):
        return 4 * px + 2 * py + pc

    return (mx, my, mc), (mx, my, 1 - mc), chips, lid


def _rcopy(src, dst, send_sems, recv_sems, k, to):
    return pltpu.make_async_remote_copy(src_ref=src, dst_ref=dst, send_sem=send_sems.at[k], recv_sem=recv_sems.at[k],
                                        device_id=to, device_id_type=pl.DeviceIdType.MESH)


def _sem_scratch(*sizes):
    return [pltpu.SemaphoreType.DMA((s,)) for s in sizes]


def _gather_rider(src, rows1, rows2, buf=None):
    def place():
        (mx, my, mc), sib, chips, lid = _mesh_place()
        xn, yn, dg = [(*chip, mc) for chip in chips]
        return lid(mx, my, mc), sib, xn, yn, dg, lid

    def halves(rows):
        r0, r1 = rows
        mid = r0 + (r1 - r0) // 32 * 16
        return pl.ds(r0, mid - r0), pl.ds(mid, r1 - mid), pl.ds(r0, r1 - r0)

    n_src = 1 if rows1 is not None else 0

    def start(ins, outs, sems):
        send, recv, loc = sems
        me, sib, xn, yn, dg, lid = place()
        if rows1 is not None:
            win = pl.ds(rows1[0], rows1[1] - rows1[0])
            mine, dst = ins[0].at[win], outs[0].at[me, win]
            pltpu.make_async_copy(mine, dst, loc.at[0]).start()
            for t, to in enumerate((sib, xn, yn)):
                _rcopy(mine, dst, send, recv, t, to).start()
        if rows2 is not None:
            top, bot, win = halves(rows2)
            xb, yb = outs[0].at[lid(*xn)], outs[0].at[lid(*yn)]
            _rcopy(xb.at[top], xb.at[top], send, recv, 3, yn).start()
            _rcopy(yb.at[bot], yb.at[bot], send, recv, 4, xn).start()
            _rcopy(xb.at[win], xb.at[win], send, recv, 5, sib).start()
            _rcopy(yb.at[win], yb.at[win], send, recv, 6, sib).start()

    def finish(ins, outs, sems):
        send, recv, loc = sems
        me, sib, xn, yn, dg, lid = place()
        if rows2 is not None:
            top, bot, win = halves(rows2)
            db = outs[0].at[lid(*dg)]
            _rcopy(db.at[top], db.at[top], send, recv, 3, yn).wait_recv()
            _rcopy(db.at[bot], db.at[bot], send, recv, 4, xn).wait_recv()
            _rcopy(db.at[win], db.at[win], send, recv, 7, sib).start()
            for t, dev in ((5, xn), (6, yn), (7, dg)):
                blk = outs[0].at[lid(dev[0], dev[1], 1 - dev[2]), win]
                _rcopy(blk, blk, send, recv, t, sib).wait_recv()
            for t, part in ((3, top), (4, bot), (5, win), (6, win), (7, win)):
                _rcopy(db.at[part], db.at[part], send, recv, t, sib).wait_send()
        if rows1 is not None:
            win = pl.ds(rows1[0], rows1[1] - rows1[0])
            mine, dst = ins[0].at[win], outs[0].at[me, win]
            for t, dev in enumerate((sib, xn, yn)):
                cp = _rcopy(mine, outs[0].at[lid(*dev), win], send, recv, t, dev)
                cp.wait_recv()
                cp.wait_send()
            pltpu.make_async_copy(mine, dst, loc.at[0]).wait()

    out_shape = jax.ShapeDtypeStruct((N_DEV,) + src.shape, src.dtype)
    inputs = ([src] if n_src else []) + ([buf] if buf is not None else [])
    return _Rider(inputs, [out_shape], _sem_scratch(8, 8, 1), start, finish,
                  aliases={n_src: 0} if buf is not None else {})


def _sibling_rider(part):
    def start(ins, outs, sems):
        send, recv = sems
        (mx, my, mc), sib, chips, lid = _mesh_place()
        for t, slab in enumerate([lid(*sib)] + [lid(*chip, 1 - mc) for chip in chips]):
            _rcopy(ins[0].at[slab], outs[0].at[t], send, recv, t, sib).start()

    def finish(ins, outs, sems):
        send, recv = sems
        _, sib, _, _ = _mesh_place()
        for t in range(4):
            cp = _rcopy(ins[0].at[0], outs[0].at[t], send, recv, t, sib)
            cp.wait_recv()
            cp.wait_send()

    return _Rider([part], [jax.ShapeDtypeStruct((4,) + part.shape[1:], part.dtype)], _sem_scratch(4, 4), start, finish)


def _chips_rider(pair, rows, buf=None):
    r0, r1 = rows
    win = pl.ds(r0, r1 - r0)

    def start(ins, outs, sems):
        send, recv = sems
        (mx, my, mc), sib, chips, lid = _mesh_place()
        for j, chip in enumerate(chips):
            _rcopy(ins[0].at[j, win], outs[0].at[j, win], send, recv, j, (*chip, mc)).start()

    def finish(ins, outs, sems):
        send, recv = sems
        (mx, my, mc), sib, chips, lid = _mesh_place()
        for j, chip in enumerate(chips):
            cp = _rcopy(ins[0].at[j, win], outs[0].at[j, win], send, recv, j, (*chip, mc))
            cp.wait_recv()
            cp.wait_send()

    out_shape = jax.ShapeDtypeStruct(pair.shape, pair.dtype)
    if buf is None:
        return _Rider([pair], [out_shape], _sem_scratch(3, 3), start, finish)
    return _Rider([pair, buf], [out_shape], _sem_scratch(3, 3), start, finish, aliases={1: 0})


def _run_rider(name, rider):
    n_in, n_out = len(rider.inputs), len(rider.out_shapes)

    def body(*refs):
        ins, outs, sems = refs[:n_in], refs[n_in:n_in + n_out], refs[n_in + n_out:]
        rider.start(ins, outs, sems)
        rider.finish(ins, outs, sems)

    any_spec = pl.BlockSpec(memory_space=pl.ANY)
    res = pl.pallas_call(body, name=name, in_specs=[any_spec] * n_in, out_specs=[any_spec] * n_out,
                         out_shape=rider.out_shapes, scratch_shapes=rider.scratch,
                         input_output_aliases=rider.aliases)(*rider.inputs)
    return list(res)


def _slab_ids():
    (mx, my, mc), _, chips, lid = _mesh_place()
    return jnp.stack([lid(*chip, mc) for chip in chips] + [lid(mx, my, mc)]).astype(jnp.int32)


def _pair_sum(name, part, rsib, ids):
    _, n, k = part.shape
    tr = _row_tile(n, 512, 16)

    def body(ids_ref, p_ref, r_ref, o_ref):
        o_ref[...] = (p_ref[...].astype(F32) + r_ref[...].astype(F32)).astype(o_ref.dtype)

    grid_spec = pltpu.PrefetchScalarGridSpec(
        num_scalar_prefetch=1, grid=(3, n // tr),
        in_specs=[pl.BlockSpec((None, tr, k), lambda j, i, ids: (ids[j], i, 0)),
                  pl.BlockSpec((None, tr, k), lambda j, i, ids: (1 + j, i, 0))],
        out_specs=pl.BlockSpec((None, tr, k), lambda j, i, ids: (j, i, 0)))
    return pl.pallas_call(body, name=name, grid_spec=grid_spec, out_shape=jax.ShapeDtypeStruct((3, n, k), part.dtype),
                          compiler_params=_cparams("parallel", "parallel"))(ids, part, rsib)


def _sum5(name, part, rsib, rici, ids, layer, stacked, rider=None):
    _, n, k = part.shape
    tr = _row_tile(n, 512, 16)
    first = isinstance(stacked, int)

    def body(ids_ref, p_ref, r_ref, c_ref, *rest):
        acc = p_ref[...].astype(F32) + r_ref[...].astype(F32)
        for j in range(3):
            acc = acc + c_ref[j].astype(F32)
        rest[-1][...] = acc

    in_specs = [pl.BlockSpec((None, tr, k), lambda i, ids: (ids[3], i, 0)),
                pl.BlockSpec((None, tr, k), lambda i, ids: (0, i, 0)),
                pl.BlockSpec((3, tr, k), lambda i, ids: (0, i, 0))] + ([] if first else [pl.BlockSpec(memory_space=pl.ANY)])
    n_layers = stacked if first else stacked.shape[0]
    outs, r_outs = _pcall(name, body, (n // tr,), in_specs, [pl.BlockSpec((None, tr, k), lambda i, ids: (layer, i, 0))],
                          [jax.ShapeDtypeStruct((n_layers, n, k), F32)],
                          (part, rsib, rici) + (() if first else (stacked,)), ("parallel",), rider=rider, prefetch=ids,
                          aliases={} if first else {3: 0})
    return outs[0] if rider is None else (outs[0], r_outs)


W_KEYS = ("in_t", "br0", "br1", "br2", "br3", "gate_t", "o", "ffg_t", "ffu_t", "ffd")
SUMS_TRANSPOSED_LATER = ("gate_t", "br0", "br1", "br2", "br3")


CARRIER_US = {"mod_in": 12, "in": 55, "gate": 95, "prep": 19, "attn": 125,"br0": 15, "merge": 50, "o": 25, "ln1": 27,
              "ffgu": 135, "ffd": 73, "ln2": 27, "loss": 20, "ln2_bwd": 44, "dF": 80,
              "dWffd": 64, "dh2a": 75, "dh2b": 75, "dWffg": 64, "dWffu": 64, "ln1_bwd": 44,
              "dMg": 25, "dWo": 25, "merge_bwd": 80, "dY0": 14, "dWbr0": 15, "attn_bwd": 140,"prep_bwd": 28,
              "dWin": 54, "dWgate": 95, "dhb_a": 64, "dhb_b": 115}
ICI_US_PER_MIB = 45.0
GATHER_US_PER_MIB = 30.0
RELAY_US_PER_MIB = 15.0
D2D_US_PER_MIB = 6.8
MIN_CHUNK_US = 10.0
CARRIER_FILL_FORWARD = 1.15
CARRIER_FILL_BACKWARD = 0.8


class _Comm:
    def __init__(self, wsrc):
        self.wsrc = wsrc
        self.n_layers = len(wsrc)
        self.queue = []
        self.riding = {}
        self.n_alone = 0
        self.buf, self.left = {}, {}
        self.part, self.rsib, self.pair = {}, {}, {}
        self.ids = _slab_ids()
        for i in range(self.n_layers):
            for k in W_KEYS:
                self._push_chunks("gather", ("w", i, k), wsrc[i][k].shape, wsrc[i][k].dtype)

    def _push_chunks(self, kind, item, shape, dtype):
        n, k = shape[-2], shape[-1]
        mib = n * k * jnp.dtype(dtype).itemsize / 2 ** 20
        pieces = max(1, int(mib * ICI_US_PER_MIB // MIN_CHUNK_US))
        while n % (16 * pieces):
            pieces -= 1
        step = n // pieces
        self.left[item] = n
        us = mib * (GATHER_US_PER_MIB if kind == "gather" else ICI_US_PER_MIB) / pieces
        for c in range(pieces):
            self.queue.append(dict(kind=kind, item=item, rows=(c * step, (c + 1) * step), rows2=None, us=us))

    @staticmethod
    def _merge(units, u):
        def joined(a, b):
            if a is None or b is None:
                return True, a or b
            return a[1] == b[0], (a[0], b[1])

        for v in units:
            if v["item"] == u["item"] and v["kind"] == u["kind"] and u["kind"] != "sibling":
                ok1, rows = joined(v["rows"], u["rows"])
                ok2, rows2 = joined(v["rows2"], u["rows2"])
                if ok1 and ok2:
                    v.update(rows=rows, rows2=rows2, us=v["us"] + u["us"])
                    return True
        return False

    def _unit_rider(self, u):
        item = u["item"]
        if u["kind"] == "gather":
            src = self.wsrc[item[1]][item[2]] if item[0] == "w" else self.part[item]
            return _gather_rider(src, u["rows"], u["rows2"], self.buf.get(item))
        if u["kind"] == "sibling":
            return _sibling_rider(self.part[item])
        return _chips_rider(self.pair[item], u["rows"], self.buf.get(item))

    def _done(self, u, out):
        item = u["item"]
        if u["kind"] == "sibling":
            self.rsib[item] = out
            self.pair[item] = _pair_sum(f"pair_l{item[1]}_{item[2]}", self.part[item], out, self.ids)
            self._push_chunks("chips", item, self.pair[item].shape, self.pair[item].dtype)
            return
        self.buf[item] = out
        if u["kind"] == "gather":
            if u["rows"] is not None:
                rows = u["rows"]
                mib = (rows[1] - rows[0]) * out.shape[-1] * out.dtype.itemsize / 2 ** 20
                self.queue.insert(0, dict(kind="gather", item=item, rows=None, rows2=rows, us=mib * RELAY_US_PER_MIB))
            if u["rows2"] is not None:
                self.left[item] -= u["rows2"][1] - u["rows2"][0]

    def _send(self, name, units, call):
        outs = call(_compose([self._unit_rider(u) for u in units]))
        for u, o in zip(units, outs):
            self._done(u, o)

    def exchange(self, name, src, budget_us):
        units = self._take(budget_us)
        outs = _run_rider(name, _compose([_direct_rider(src)] + [self._unit_rider(u) for u in units]))
        for u, o in zip(units, outs[1:]):
            self._done(u, o)
        return outs[0]

    def rider(self, name, budget_us=None):
        budget = CARRIER_US.get(name.split("_", 1)[1] if name[0] == "l" and name[1].isdigit() else name, 0) \
            if budget_us is None else budget_us
        units = self._take(budget)
        if not units:
            return None
        self.riding[name] = units
        return _compose([self._unit_rider(u) for u in units])

    def _take(self, budget):
        units, used = [], 0.0
        forward = bool(self.queue) and self.queue[0]["item"][0] == "w"
        fill = CARRIER_FILL_FORWARD if forward else CARRIER_FILL_BACKWARD
        while self.queue and used + self.queue[0]["us"] <= fill * budget:
            u = self.queue[0]
            if not self._merge(units, u):
                if any(v["item"] == u["item"] for v in units):
                    break
                units.append(dict(u))
            used += u["us"]
            del self.queue[0]
        return units

    def deliver(self, name, outs):
        for u, o in zip(self.riding.pop(name), outs):
            self._done(u, o)

    def _flush(self, item, kinds):
        hits = [p for p, u in enumerate(self.queue) if u["item"] == item and u["kind"] in kinds]
        if not hits:
            return
        prefix = self.queue[:hits[-1] + 1]
        del self.queue[:hits[-1] + 1]
        units = []
        for u in prefix:
            if not self._merge(units, u):
                units.append(dict(u))
        tag = "_".join(str(t) for t in item) + "_" + kinds[0]
        batches = [[]]
        for u in units:
            if any(v["item"] == u["item"] for v in batches[-1]):
                batches.append([])
            batches[-1].append(u)
        for batch in batches:
            self.n_alone += 1
            self._send(None, batch, functools.partial(_run_rider, f"alone{self.n_alone}_{tag}"))

    def begin(self):
        self._flush(("w", 0, "in_t"), ("gather",))

    def _finish_gather(self, item):
        while self.left[item] > 0:
            assert any(u["item"] == item for u in self.queue), item
            self._flush(item, ("gather",))
        return self.buf[item]

    def weight(self, i, k):
        o = self._finish_gather(("w", i, k))
        return o.reshape(-1, o.shape[-1])

    def grads(self, i, group):
        for k, g in group.items():
            item = ("g", i, k)
            self.part[item] = g.reshape(N_DEV, g.shape[0] // N_DEV, g.shape[1])
            us = g.size // N_DEV * g.dtype.itemsize / 2 ** 20 * D2D_US_PER_MIB
            self.queue.insert(0, dict(kind="sibling", item=item, rows=None, us=us))

    def total(self, k):
        out = self.n_layers
        for i in range(self.n_layers):
            item = ("g", i, k)
            self._flush(item, ("sibling",))
            self._flush(item, ("chips",))
            name = f"sum_l{i}_{k}"
            rider = self.rider(name, budget_us=self.part[item][0].size / 1.06e5) if k in SUMS_TRANSPOSED_LATER else None
            out = _sum5(name, self.part[item], self.rsib[item], self.buf[item], self.ids, i, out, rider=rider)
            if rider is not None:
                out, r_outs = out
                self.deliver(name, r_outs)
        return out

    def small_ready(self, i, dmod, ds):
        parts = [dmod[0], dmod[1]] + [ds[nm] for nm in LAYER_SMALL + ("conv_w",)]
        self.small_shapes = [p.shape for p in parts]
        self.gather_small(f"lat{i}", _pack([dmod[1]]))
        self.gather_small(f"small{i}", _pack(parts))

    def gather_small(self, name, arr):
        item = ("s", name)
        self.part[item] = arr
        waiting, self.queue = self.queue, []
        self._push_chunks("gather", item, arr.shape, arr.dtype)
        self.queue += waiting

    def gathered(self, name):
        return self._finish_gather(("s", name))


def _row_tile(n, pref, mult):
    best = None
    t = mult
    while t <= min(n, pref):
        if n % t == 0:
            best = t
        t += mult
    return best if best is not None else n


def _sum8(name, slabs):
    _, n, k = slabs.shape
    tr = _row_tile(n, 128, 16)

    def body(s_ref, o_ref):
        acc = s_ref[0].astype(F32)
        for j in range(1, N_DEV):
            acc = acc + s_ref[j].astype(F32)
        o_ref[...] = acc

    return pl.pallas_call(
        body, name=name, grid=(n // tr,),
        in_specs=[pl.BlockSpec((N_DEV, tr, k), lambda i: (0, i, 0))],
        out_specs=pl.BlockSpec((tr, k), lambda i: (i, 0)),
        out_shape=jax.ShapeDtypeStruct((n, k), F32),
        compiler_params=_cparams("parallel"),
    )(slabs)


def _adamw(name, w, g, m, v, rider=None):
    n, k = w.shape[-2:]
    tr = _row_tile(n, 256, 8)

    def body(w_ref, g_ref, m_ref, v_ref, d_ref, m2_ref, v2_ref):
        gv = g_ref[...]
        m2 = ADAM_B1 * m_ref[...] + (1.0 - ADAM_B1) * gv
        v2 = ADAM_B2 * v_ref[...] + (1.0 - ADAM_B2) * jnp.square(gv)
        m_hat = m2 / (1.0 - ADAM_B1 ** ADAM_STEP)
        v_hat = v2 / (1.0 - ADAM_B2 ** ADAM_STEP)
        d_ref[...] = -ADAM_LR * (m_hat / (jnp.sqrt(v_hat) + ADAM_EPS) + ADAM_WD * w_ref[...])
        m2_ref[...] = m2
        v2_ref[...] = v2

    if w.ndim == 2:
        grid, spec = (n // tr,), pl.BlockSpec((tr, k), lambda i: (i, 0))
    else:
        grid, spec = (w.shape[0], n // tr), pl.BlockSpec((None, tr, k), lambda l, i: (l, i, 0))
    outs, r_outs = _pcall(name, body, grid, [spec] * 4, [spec] * 3, [jax.ShapeDtypeStruct(w.shape, F32)] * 3,
                          (w, g, m, v), ("parallel",) * len(grid), rider=rider)
    return outs if rider is None else (outs, r_outs)


def _pack(arrs):
    flat = jnp.concatenate([a.reshape(-1).astype(F32) for a in arrs])
    pad = (-flat.shape[0]) % 2048
    if pad:
        flat = jnp.concatenate([flat, jnp.zeros((pad,), F32)])
    return flat.reshape(-1, 128)


def _unpack(packed, shapes):
    flat = packed.reshape(-1)
    out, off = [], 0
    for shp in shapes:
        size = math.prod(shp)
        out.append(flat[off:off + size].reshape(shp))
        off += size
    return out


WEIGHT_NAMES = ("c_ctx", "w_ada", "b_ada", "w_in", "q_norm_g", "k_norm_g", "pool_w", "pool_scale", "sgu_ln_g",
                "sgu_ln_b", "sgu_w", "sgu_b", "conv_w", "w_br_attn", "w_br_pool", "w_br_sgu", "w_br_conv", "w_gate",
                "b_gate", "w_o", "ln1_g", "ln1_b", "w_ff_gate", "w_ff_up", "w_ff_down", "ln2_g", "ln2_b")
COL_SHARDED = {"w_in": "in_t", "w_gate": "gate_t", "w_ff_gate": "ffg_t", "w_ff_up": "ffu_t",
               "w_br_attn": "br0", "w_br_pool": "br1", "w_br_sgu": "br2", "w_br_conv": "br3"}
ROW_SHARDED = {"w_o": "o", "w_ff_down": "ffd"}
LAYER_SMALL = ("q_norm_g", "k_norm_g", "pool_w", "pool_scale", "sgu_ln_g", "sgu_ln_b", "sgu_w", "sgu_b", "b_gate",
               "ln1_g", "ln1_b", "ln2_g", "ln2_b")
SMALL_ORDER = ("c_ctx", "b_ada") + LAYER_SMALL + ("conv_w",)


def _train_step(a):
    n_layers, d = a["w_in"].shape[0], a["x"].shape[-1]
    rc = a["ctx"].shape[1]
    alpha = (2 * n_layers) ** 0.25
    mx, my, mc = [lax.axis_index(ax) for ax in MESH_AXES]
    me = 4 * mx + 2 * my + mc
    ada_w = a["w_ada"].shape[-1]
    cw_loc = a["conv_w"].shape[-1]

    comm = _Comm([{**{key: jnp.swapaxes(a[nm], 1, 2)[i].astype(BF16) for nm, key in COL_SHARDED.items()},
                   **{key: a[nm][i].astype(BF16) for nm, key in ROW_SHARDED.items()}} for i in range(n_layers)])

    def carried(name, *args, budget_us, **kw):
        rider = comm.rider(name, budget_us=budget_us)
        if rider is None:
            return _mm(name, *args, **kw)
        res, r_outs = _mm(name, *args, rider=rider, **kw)
        comm.deliver(name, r_outs)
        return res

    n_c, n_cw = d, n_layers * 3 * cw_loc
    got = comm.exchange("gather_cond", _pack([a["c"], a["conv_w"]]), budget_us=12).reshape(N_DEV, -1)
    c_all = got[:, :n_c]
    conv_w = got[:, n_c:n_c + n_cw].reshape(N_DEV, n_layers, 3, cw_loc).transpose(1, 2, 0, 3).reshape(n_layers, 3, -1)
    cond = jnp.concatenate([c_all, a["c_ctx"][None], jnp.zeros((16 - N_DEV - 1, d), F32)], axis=0)
    sil, sil_vjp = jax.vjp(jax.nn.silu, cond)
    sil = sil.astype(BF16)

    mod_cols = jnp.concatenate([carried(f"ada{i}", sil, a["w_ada"][i], "nn", F32, budget_us=20)
                                for i in range(n_layers)], axis=0)
    got = comm.exchange("gather_mod", mod_cols, budget_us=22)
    mod_all = got.reshape(N_DEV, n_layers, 16, ada_w).transpose(1, 2, 0, 3).reshape(n_layers, 16, -1)
    mod_all = mod_all + a["b_ada"][:, None, :]
    mod = jnp.stack([mod_all[:, N_DEV], lax.dynamic_index_in_dim(mod_all, me, axis=1, keepdims=False)], axis=1)
    comm.begin()
    sp = [{nm: a[nm][i] for nm in LAYER_SMALL} for i in range(n_layers)]
    for i in range(n_layers):
        sp[i]["conv_w"] = conv_w[i]

    xin = jnp.concatenate([a["ctx"][0], a["x"][0]], axis=0)
    loss_l, grad_x = _local_step(xin, a["loss_target"][0], mod, comm, sp, rc, alpha)
    loss = lax.psum(loss_l, MESH_AXES)
    grads = {}

    def transposed_home(nm):
        return nm in COL_SHARDED and a[nm].shape[-1] % 128 != 0

    delta, new_m, new_v = {}, {}, {}

    def adamw(nm):
        name = f"adamw_{nm}"
        there = transposed_home(nm)
        view = (lambda t: jnp.swapaxes(t, 1, 2)) if there else (lambda t: t)
        if nm in COL_SHARDED:
            g = comm.total(COL_SHARDED[nm])
            grads[nm] = jnp.swapaxes(g, 1, 2)
            g = g if there else grads[nm]
        elif nm in ROW_SHARDED:
            g = grads[nm] = comm.total(ROW_SHARDED[nm])
        else:
            g = grads[nm]
        res = _adamw(name, view(a[nm]), g, view(a["m_" + nm]), view(a["v_" + nm]))
        delta[nm], new_m[nm], new_v[nm] = [view(t) for t in res]

    for nm in ("w_ff_down", "w_ff_gate", "w_ff_up", "w_o", "w_br_attn", "w_br_pool", "w_br_sgu", "w_br_conv"):
        adamw(nm)

    tots = [_unpack(_sum8(f"sum_small{i}", comm.gathered(f"small{i}")), comm.small_shapes) for i in range(n_layers)]
    dmod_c, dmod_lat_sum = jnp.stack([t[0] for t in tots]), jnp.stack([t[1] for t in tots])
    for j, nm in enumerate(LAYER_SMALL + ("conv_w",)):
        grads[nm] = jnp.stack([t[2 + j] for t in tots])
    grads["conv_w"] = lax.dynamic_slice_in_dim(grads["conv_w"], me * cw_loc, cw_loc, axis=2)
    grads["b_ada"] = dmod_c + dmod_lat_sum
    dmod_lat_all = jnp.stack([comm.gathered(f"lat{i}").reshape(N_DEV, -1)[:, :6 * d] for i in range(n_layers)])
    dm_rows = jnp.concatenate([dmod_lat_all, dmod_c[:, None, :],
                               jnp.zeros((n_layers, 16 - N_DEV - 1, 6 * d), F32)], axis=1)
    dm_cols = lax.dynamic_slice_in_dim(dm_rows, me * ada_w, ada_w, axis=2).astype(BF16)
    grads["w_ada"] = jnp.stack([carried(f"dWada{i}", sil, dm_cols[i], "tn", F32, budget_us=20)
                                for i in range(n_layers)])
    dsil = None
    for i in range(n_layers):
        dsil = carried(f"dsil{i}", dm_cols[i], a["w_ada"][i], "nt", F32, acc=dsil, budget_us=10)
    dsil = _sum8("sum_dsil", comm.exchange("gather_dsil", dsil, budget_us=17))
    grads["c_ctx"] = sil_vjp(dsil)[0][N_DEV]

    for nm in ("w_ada", "w_in", "w_gate"):
        adamw(nm)
    shapes = [a[nm].shape for nm in SMALL_ORDER]
    res = _adamw("adamw_small", _pack([a[nm] for nm in SMALL_ORDER]), _pack([grads[nm] for nm in SMALL_ORDER]),
                 _pack([a["m_" + nm] for nm in SMALL_ORDER]), _pack([a["v_" + nm] for nm in SMALL_ORDER]))
    for tree, packed in zip((delta, new_m, new_v), res):
        for nm, t in zip(SMALL_ORDER, _unpack(packed, shapes)):
            tree[nm] = t
    return (loss, grad_x[None], *[grads[nm] for nm in WEIGHT_NAMES], *[delta[nm] for nm in WEIGHT_NAMES],
            *[new_m[nm] for nm in WEIGHT_NAMES], *[new_v[nm] for nm in WEIGHT_NAMES])


def kernel(x, c, ctx, c_ctx, w_ada, b_ada, w_in, q_norm_g, k_norm_g, pool_w, pool_scale, sgu_ln_g, sgu_ln_b, sgu_w, sgu_b, conv_w, w_br_attn, w_br_pool, w_br_sgu, w_br_conv, w_gate, b_gate, w_o, ln1_g, ln1_b, w_ff_gate, w_ff_up, w_ff_down, ln2_g, ln2_b, loss_target, m_c_ctx, m_w_ada, m_b_ada, m_w_in, m_q_norm_g, m_k_norm_g, m_pool_w, m_pool_scale, m_sgu_ln_g, m_sgu_ln_b, m_sgu_w, m_sgu_b, m_conv_w, m_w_br_attn, m_w_br_pool, m_w_br_sgu, m_w_br_conv, m_w_gate, m_b_gate, m_w_o, m_ln1_g, m_ln1_b, m_w_ff_gate, m_w_ff_up, m_w_ff_down, m_ln2_g, m_ln2_b, v_c_ctx, v_w_ada, v_b_ada, v_w_in, v_q_norm_g, v_k_norm_g, v_pool_w, v_pool_scale, v_sgu_ln_g, v_sgu_ln_b, v_sgu_w, v_sgu_b, v_conv_w, v_w_br_attn, v_w_br_pool, v_w_br_sgu, v_w_br_conv, v_w_gate, v_b_gate, v_w_o, v_ln1_g, v_ln1_b, v_w_ff_gate, v_w_ff_up, v_w_ff_down, v_ln2_g, v_ln2_b):
    names = list(WEIGHT_NAMES)
    args = dict(zip(
        ["x", "c", "ctx"] + names + ["loss_target"] + ["m_" + n for n in names] + ["v_" + n for n in names],
        (x, c, ctx, c_ctx, w_ada, b_ada, w_in, q_norm_g, k_norm_g, pool_w, pool_scale, sgu_ln_g, sgu_ln_b, sgu_w, sgu_b, conv_w, w_br_attn, w_br_pool, w_br_sgu, w_br_conv, w_gate, b_gate, w_o, ln1_g, ln1_b, w_ff_gate, w_ff_up, w_ff_down, ln2_g, ln2_b, loss_target, m_c_ctx, m_w_ada, m_b_ada, m_w_in, m_q_norm_g, m_k_norm_g, m_pool_w, m_pool_scale, m_sgu_ln_g, m_sgu_ln_b, m_sgu_w, m_sgu_b, m_conv_w, m_w_br_attn, m_w_br_pool, m_w_br_sgu, m_w_br_conv, m_w_gate, m_b_gate, m_w_o, m_ln1_g, m_ln1_b, m_w_ff_gate, m_w_ff_up, m_w_ff_down, m_ln2_g, m_ln2_b, v_c_ctx, v_w_ada, v_b_ada, v_w_in, v_q_norm_g, v_k_norm_g, v_pool_w, v_pool_scale, v_sgu_ln_g, v_sgu_ln_b, v_sgu_w, v_sgu_b, v_conv_w, v_w_br_attn, v_w_br_pool, v_w_br_sgu, v_w_br_conv, v_w_gate, v_b_gate, v_w_o, v_ln1_g, v_ln1_b, v_w_ff_gate, v_w_ff_up, v_w_ff_down, v_ln2_g, v_ln2_b)))
    return _train_step(args)
```

```python
import functools
import math

import jax
import jax.numpy as jnp
from jax import lax
from jax.experimental import pallas as pl
from jax.experimental.pallas import tpu as pltpu

F32 = jnp.float32
BF16 = jnp.bfloat16

N_DEV = 8
MESH_AXES = ("x", "y", "c")
V7X_VMEM_LIMIT_BYTES = 56 * 1024 * 1024

GRID_W = 64
HEAD_DIM = 128
N_HEADS = 8
N_KV_HEADS = 2
KV_GROUP = N_HEADS // N_KV_HEADS
Q_W = N_HEADS * HEAD_DIM
KV_W = N_KV_HEADS * HEAD_DIM
ROPE_THETA = 10000.0
ROPE_AXIS_DIM = HEAD_DIM // 2
POOL_WINDOWS = (2, 4, 8, 16)
GC = 128
N_GROUPS = 4
BR_W = N_GROUPS * GC
SGU_CHUNK = 128
N_BRANCH = 4
LN_EPS = 1e-5
RMS_EPS = 1e-6
OFF_K = Q_W
OFF_V = OFF_K + KV_W
OFF_POOL = OFF_V + KV_W
OFF_U = OFF_POOL + BR_W
OFF_VG = OFF_U + BR_W
OFF_CB = OFF_VG + BR_W
OFF_CC = OFF_CB + BR_W
OFF_CX = OFF_CC + BR_W
IN_W = OFF_CX + BR_W
ATT_SCALE = HEAD_DIM ** -0.5

ADAM_LR = 0.001
ADAM_B1 = 0.9
ADAM_B2 = 0.999
ADAM_EPS = 1e-08
ADAM_WD = 0.01
ADAM_STEP = 10

_NT = (((1,), (1,)), ((), ()))
_NN = (((1,), (0,)), ((), ()))
_TN = (((0,), (0,)), ((), ()))
_DIMS = {"nt": _NT, "nn": _NN, "tn": _TN}


def _cparams(*sem):
    return pltpu.CompilerParams(dimension_semantics=sem, vmem_limit_bytes=V7X_VMEM_LIMIT_BYTES)


def _tile(dim, pref):
    best = None
    t = 128
    while t <= min(dim, pref):
        if dim % t == 0:
            best = t
        t += 128
    return best if best is not None else dim


def _dot(a, b, dims):
    return lax.dot_general(a.astype(BF16), b.astype(BF16), dims, preferred_element_type=F32)


class _Rider:
    def __init__(self, inputs, out_shapes, scratch, start, finish, aliases=None):
        self.inputs, self.out_shapes, self.scratch = list(inputs), list(out_shapes), list(scratch)
        self.start, self.finish = start, finish
        self.aliases = dict(aliases or {})


def _compose(riders):
    inputs, outs, scratch, aliases, spans = [], [], [], {}, []
    for rd in riders:
        i0, o0, s0 = len(inputs), len(outs), len(scratch)
        aliases.update({i0 + p: o0 + q for p, q in rd.aliases.items()})
        inputs += rd.inputs
        outs += rd.out_shapes
        scratch += rd.scratch
        spans.append((slice(i0, len(inputs)), slice(o0, len(outs)), slice(s0, len(scratch))))

    def start(ins, os, sems):
        for rd, (si, so, ss) in zip(riders, spans):
            rd.start(ins[si], os[so], sems[ss])

    def finish(ins, os, sems):
        for rd, (si, so, ss) in zip(riders, spans):
            rd.finish(ins[si], os[so], sems[ss])

    return _Rider(inputs, outs, scratch, start, finish, aliases)


def _pcall(name, body, grid, in_specs, out_specs, out_shape, args, sem, scratch=(), rider=None, prefetch=None,
           aliases=None):
    in_specs, out_specs, out_shape, scratch = list(in_specs), list(out_specs), list(out_shape), list(scratch)
    n_pre = 0 if prefetch is None else 1
    n_in, n_out, n_scr = len(in_specs), len(out_specs), len(scratch)
    r_in, r_out = (len(rider.inputs), len(rider.out_shapes)) if rider is not None else (0, 0)
    any_spec = pl.BlockSpec(memory_space=pl.ANY)
    io_aliases = {n_pre + p: q for p, q in (aliases or {}).items()}
    if rider is not None:
        io_aliases.update({n_pre + n_in + p: n_out + q for p, q in rider.aliases.items()})
        in_specs, out_specs = in_specs + [any_spec] * r_in, out_specs + [any_spec] * r_out
        out_shape, scratch = out_shape + rider.out_shapes, scratch + rider.scratch
        args, sem = (*args, *rider.inputs), ["arbitrary"] * len(grid)

    def wrapped(*refs):
        pre, refs = refs[:n_pre], refs[n_pre:]
        if rider is None:
            return body(*pre, *refs)
        ins, refs = refs[:n_in], refs[n_in:]
        r_ins, refs = refs[:r_in], refs[r_in:]
        outs, refs = refs[:n_out], refs[n_out:]
        r_outs, refs = refs[:r_out], refs[r_out:]
        scr, r_scr = refs[:n_scr], refs[n_scr:]
        first = functools.reduce(jnp.logical_and, [pl.program_id(ax) == 0 for ax in range(len(grid))])
        last = functools.reduce(jnp.logical_and, [pl.program_id(ax) == grid[ax] - 1 for ax in range(len(grid))])

        @pl.when(first)
        def _():
            rider.start(r_ins, r_outs, r_scr)

        body(*pre, *ins, *outs, *scr)

        @pl.when(last)
        def _():
            rider.finish(r_ins, r_outs, r_scr)

    if prefetch is None:
        res = pl.pallas_call(wrapped, name=name, grid=grid, in_specs=in_specs, out_specs=out_specs, out_shape=out_shape,
                             scratch_shapes=scratch, input_output_aliases=io_aliases,
                             compiler_params=_cparams(*sem))(*args)
    else:
        grid_spec = pltpu.PrefetchScalarGridSpec(num_scalar_prefetch=1, grid=grid, in_specs=in_specs,
                                                 out_specs=out_specs, scratch_shapes=scratch)
        res = pl.pallas_call(wrapped, name=name, grid_spec=grid_spec, out_shape=out_shape,
                             input_output_aliases=io_aliases, compiler_params=_cparams(*sem))(prefetch, *args)
    return list(res[:n_out]), list(res[n_out:])


V7X_MM_VMEM_BUDGET = 40 * 1024 * 1024


def _mm_plan(form, m, n, k, a_size, b_size, o_size, has_acc):
    tn = n if (form == "tn" and n <= 2048) else _tile(n, 512)
    rows = sorted({m} | {t for t in range(128, m, 128) if m % t == 0}, reverse=True)
    for tk, min_tm in ((k, 384), (k if k <= 2816 else _tile(k, 2816), 0)):
        nk = k // tk
        for tm in rows:
            need = 2 * (tm * tk * a_size + tn * tk * b_size + tm * tn * o_size) + tm * tn * 4 * (2 if nk > 1 else 1)
            need += 2 * tm * tn * 4 if has_acc else 0
            if need <= V7X_MM_VMEM_BUDGET and tm >= min(min_tm, m):
                return tm, tn, tk
    return _tile(m, 128), tn, tk


def _mm(name, a, b, form, out_dtype, acc=None, rider=None):
    if form == "nt":
        (m, k), (n, k2) = a.shape, b.shape
    elif form == "nn":
        (m, k), (k2, n) = a.shape, b.shape
    else:
        (k, m), (k2, n) = a.shape, b.shape
    assert k == k2, (name, a.shape, b.shape)
    has_acc = acc is not None
    tm, tn, tk = _mm_plan(form, m, n, k, a.dtype.itemsize, b.dtype.itemsize, jnp.dtype(out_dtype).itemsize, has_acc)
    nk = k // tk
    a_spec = {"nt": pl.BlockSpec((tm, tk), lambda i, j, kk: (i, kk)),
              "nn": pl.BlockSpec((tm, tk), lambda i, j, kk: (i, kk)),
              "tn": pl.BlockSpec((tk, tm), lambda i, j, kk: (kk, i))}[form]
    b_spec = {"nt": pl.BlockSpec((tn, tk), lambda i, j, kk: (j, kk)),
              "nn": pl.BlockSpec((tk, tn), lambda i, j, kk: (kk, j)),
              "tn": pl.BlockSpec((tk, tn), lambda i, j, kk: (kk, j))}[form]
    o_spec = pl.BlockSpec((tm, tn), lambda i, j, kk: (i, j))
    dims = _DIMS[form]

    def body(*refs):
        a_ref, b_ref = refs[0], refs[1]
        c_ref = refs[2] if has_acc else None
        o_ref = refs[3] if has_acc else refs[2]

        def finish(r):
            if has_acc:
                r = r + c_ref[...]
            o_ref[...] = r.astype(o_ref.dtype)

        if nk == 1:
            finish(_dot(a_ref[...], b_ref[...], dims))
            return
        acc_ref = refs[-1]
        kk = pl.program_id(2)

        @pl.when(kk == 0)
        def _():
            acc_ref[...] = _dot(a_ref[...], b_ref[...], dims)

        @pl.when(kk > 0)
        def _():
            acc_ref[...] += _dot(a_ref[...], b_ref[...], dims)

        @pl.when(kk == nk - 1)
        def _():
            finish(acc_ref[...])

    in_specs = [a_spec, b_spec] + ([o_spec] if has_acc else [])
    args = (a, b) + ((acc,) if has_acc else ())
    outs, r_outs = _pcall(name, body, (m // tm, n // tn, nk), in_specs, [o_spec],
                          [jax.ShapeDtypeStruct((m, n), out_dtype)], args, ("parallel", "parallel", "arbitrary"),
                          scratch=[pltpu.VMEM((tm, tn), F32)] if nk > 1 else [], rider=rider)
    return outs[0] if rider is None else (outs[0], r_outs)


def _mm_fused(name, a, bs, epilogue, tile_ins, out_dtypes, rider=None):
    (m, k), (n, _) = a.shape, bs[0].shape
    tn = _tile(n, 512)
    tm = None
    for cand in sorted({m} | {t for t in range(128, m, 128) if m % t == 0}, reverse=True):
        per_tile = sum(t.dtype.itemsize for t in tile_ins) + sum(jnp.dtype(d).itemsize for d in out_dtypes)
        need = 2 * (cand * k * a.dtype.itemsize + len(bs) * tn * k * bs[0].dtype.itemsize + cand * tn * per_tile)
        need += (len(bs) + 2) * cand * tn * 4
        if need <= V7X_MM_VMEM_BUDGET:
            tm = cand
            break
    assert tm is not None, name
    n_b, n_t = len(bs), len(tile_ins)
    tile = pl.BlockSpec((tm, tn), lambda i, j: (i, j))

    def body(a_ref, *refs):
        prods = [_dot(a_ref[...], r[...], _NT) for r in refs[:n_b]]
        outs = epilogue(prods, [r[...].astype(F32) for r in refs[n_b:n_b + n_t]])
        for r, o in zip(refs[n_b + n_t:], outs):
            r[...] = o.astype(r.dtype)

    outs, r_outs = _pcall(name, body, (m // tm, n // tn),
                          [pl.BlockSpec((tm, k), lambda i, j: (i, 0))] + [pl.BlockSpec((tn, k), lambda i, j: (j, 0))] * n_b
                          + [tile] * n_t, [tile] * len(out_dtypes),
                          [jax.ShapeDtypeStruct((m, n), d) for d in out_dtypes], (a, *bs, *tile_ins),
                          ("parallel", "parallel"), rider=rider)
    return outs if rider is None else (outs, r_outs)


def _rows(name, fn, n_rows, tm, nbc, row_ins, type_ins, row_outs, acc_outs, rider=None):
    n_ri, n_ti, n_ro, n_ao = len(row_ins), len(type_ins), len(row_outs), len(acc_outs)

    def row_map(i, cb, roff):
        return (jnp.maximum(i - roff, 0), cb)

    def type_map(i):
        return (jnp.where(i >= nbc, 1, 0), 0, 0)

    in_specs, args = [], []
    for arr, cb, width, roff in row_ins:
        in_specs.append(pl.BlockSpec((tm, width), functools.partial(row_map, cb=cb, roff=roff)))
        args.append(arr)
    def shared_map(i):
        return (0, 0, 0)

    for arr in type_ins:
        in_specs.append(pl.BlockSpec((None, 1, arr.shape[-1]), type_map if arr.shape[0] == 2 else shared_map))
        args.append(arr)
    out_shape, out_specs = [], []
    for total, width, dtype, roff in row_outs:
        out_shape.append(jax.ShapeDtypeStruct((total, width), dtype))
        out_specs.append(pl.BlockSpec((tm, width), functools.partial(row_map, cb=0, roff=roff)))
    acc_shared = [isinstance(w, tuple) for w in acc_outs]
    for width in acc_outs:
        if isinstance(width, tuple):
            out_shape.append(jax.ShapeDtypeStruct((1, 1, width[0]), F32))
            out_specs.append(pl.BlockSpec((None, 1, width[0]), shared_map))
        else:
            out_shape.append(jax.ShapeDtypeStruct((2, 1, width), F32))
            out_specs.append(pl.BlockSpec((None, 1, width), type_map))
    n_in = n_ri + n_ti

    def body(*refs):
        i = pl.program_id(0)
        outs = fn(*[r[...].astype(F32) for r in refs[:n_in]])
        if not isinstance(outs, (tuple, list)):
            outs = (outs,)
        assert len(outs) == n_ro + n_ao, (name, len(outs))
        for r, o in zip(refs[n_in:n_in + n_ro], outs[:n_ro]):
            r[...] = o.astype(r.dtype)
        if n_ao:
            for r, o, shared in zip(refs[n_in + n_ro:], outs[n_ro:], acc_shared):
                first = i == 0 if shared else jnp.logical_or(i == 0, i == nbc)
                o = jnp.broadcast_to(o.astype(F32), r.shape)

                @pl.when(first)
                def _(r=r, o=o):
                    r[...] = o

                @pl.when(jnp.logical_not(first))
                def _(r=r, o=o):
                    r[...] += o

    outs, r_outs = _pcall(name, body, (n_rows // tm,), in_specs, out_specs, out_shape, args, ("arbitrary",),
                          rider=rider)
    return outs if rider is None else (outs, r_outs)


def _vjp_fn(f, n_row, n_cot, keep=None):
    def g(*args):
        prim = args[:n_row] + args[n_row + n_cot:]
        cots = args[n_row:n_row + n_cot]
        out, vjp = jax.vjp(f, *prim)
        grads = vjp(tuple(cots) if isinstance(out, (tuple, list)) else cots[0])
        return grads if keep is None else tuple(grads[j] for j in keep)
    return g


def _typed(v):
    return v.reshape(1, 1, -1)


def _ln(x, g, b):
    mu = jnp.mean(x, axis=-1, keepdims=True)
    var = jnp.mean(jnp.square(x - mu), axis=-1, keepdims=True)
    return (x - mu) * lax.rsqrt(var + LN_EPS) * g + b


def _f_mod(x, sc, sh):
    return x * (1.0 + sc) + sh


def _make_f_ln(alpha, with_mod):
    def f(x, o, gate, lng, lnb, *mod):
        xn = _ln(alpha * x + gate * o, lng, lnb)
        if with_mod:
            sc, sh = mod
            return xn, xn * (1.0 + sc) + sh
        return xn
    return f


@jax.custom_vjp
def _rot(y):
    lane = lax.broadcasted_iota(jnp.int32, y.shape, 1)
    return jnp.where(lane % 64 < 32, pltpu.roll(y, 96, axis=1), pltpu.roll(y, 32, axis=1))


_rot.defvjp(lambda y: (_rot(y), None), lambda _, g: (_rot(g),))


def _f_prep(p, cos, sin, qg, kg):
    def head(xh, g):
        ms = jnp.mean(jnp.square(xh), axis=-1, keepdims=True)
        y = xh * lax.rsqrt(ms + RMS_EPS) * g
        return y * cos + _rot(y) * sin
    q = jnp.concatenate([head(p[:, h * HEAD_DIM:(h + 1) * HEAD_DIM], qg) for h in range(N_HEADS)], axis=1)
    k = jnp.concatenate([head(p[:, OFF_K + h * HEAD_DIM:OFF_K + (h + 1) * HEAD_DIM], kg)
                         for h in range(N_KV_HEADS)], axis=1)
    return q, k, p[:, OFF_V:OFF_POOL]


def _f_gate(g, t0, t1, t2, t3, b):
    d = t0.shape[-1]
    ts = (t0, t1, t2, t3)
    terms = [jax.nn.sigmoid(g[:, k * d:(k + 1) * d] + b[:, k * d:(k + 1) * d]) * ts[k] for k in range(N_BRANCH)]
    return terms[0] + terms[1] + terms[2] + terms[3]


def _f_swiglu(a, b):
    return jax.nn.silu(a) * b


def _softmax(raw):
    e = jnp.exp2((raw - jnp.max(raw, axis=-1, keepdims=True)) * (ATT_SCALE * math.log2(math.e)))
    return e / jnp.sum(e, axis=-1, keepdims=True)


def _attn_fwd(name, q, k, v, rc, ctx_queries, tq=256, rider=None):
    r = q.shape[0]
    assert rc % tq == 0 and r % tq == 0
    nqc = rc // tq

    def body(q_ref, k_ref, v_ref, o_ref, p_ref):
        qi = pl.program_id(1)

        def attend(nk):
            p = _softmax(_dot(q_ref[...], k_ref[0:nk, :], _NT)).astype(BF16)
            p_ref[:, 0:nk] = p
            o_ref[...] = _dot(p, v_ref[0:nk, :], _NN).astype(o_ref.dtype)

        @pl.when(qi < nqc)
        def _():
            if ctx_queries:
                attend(rc)
            else:
                o_ref[...] = jnp.zeros_like(o_ref)

        @pl.when(qi >= nqc)
        def _():
            attend(r)

    outs, r_outs = _pcall(
        name, body, (N_HEADS, r // tq),
        [pl.BlockSpec((tq, HEAD_DIM), lambda h, i: (i, h)),
         pl.BlockSpec((r, HEAD_DIM), lambda h, i: (0, h // KV_GROUP)),
         pl.BlockSpec((r, HEAD_DIM), lambda h, i: (0, h // KV_GROUP))],
        [pl.BlockSpec((tq, HEAD_DIM), lambda h, i: (i, h)), pl.BlockSpec((None, tq, r), lambda h, i: (h, i, 0))],
        [jax.ShapeDtypeStruct((r, Q_W), BF16), jax.ShapeDtypeStruct((N_HEADS, r, r), BF16)], (q, k, v),
        ("parallel", "parallel"), rider=rider)
    return outs if rider is None else (outs, r_outs)


def _attn_bwd(name, q, k, v, pw, do, rc, ctx_queries, tq=256, rider=None):
    r = q.shape[0]
    nqc = rc // tq

    def body(q_ref, k_ref, v_ref, p_ref, do_ref, dq_ref, dk_ref, dv_ref):
        g, qi = pl.program_id(1), pl.program_id(2)

        @pl.when(jnp.logical_and(g == 0, qi == 0))
        def _():
            dk_ref[...] = jnp.zeros_like(dk_ref)
            dv_ref[...] = jnp.zeros_like(dv_ref)

        def grad(nk):
            qb, kb, vb = q_ref[...], k_ref[0:nk, :], v_ref[0:nk, :]
            dob = do_ref[...].astype(BF16)
            p = p_ref[:, 0:nk].astype(F32)
            dv_ref[0:nk, :] += _dot(p, dob, _TN)
            dp = _dot(dob, vb, _NT)
            ds = p * (dp - jnp.sum(dp * p, axis=-1, keepdims=True))
            dq_ref[...] = _dot(ds, kb, _NN) * ATT_SCALE
            dk_ref[0:nk, :] += _dot(ds, qb, _TN) * ATT_SCALE

        @pl.when(qi < nqc)
        def _():
            if ctx_queries:
                grad(rc)
            else:
                dq_ref[...] = jnp.zeros_like(dq_ref)

        @pl.when(qi >= nqc)
        def _():
            grad(r)

    def qmap(kv, g, i):
        return (i, kv * KV_GROUP + g)

    def kvmap(kv, g, i):
        return (0, kv)

    outs, r_outs = _pcall(
        name, body, (N_KV_HEADS, KV_GROUP, r // tq),
        [pl.BlockSpec((tq, HEAD_DIM), qmap), pl.BlockSpec((r, HEAD_DIM), kvmap), pl.BlockSpec((r, HEAD_DIM), kvmap),
         pl.BlockSpec((None, tq, r), lambda kv, g, i: (kv * KV_GROUP + g, i, 0)), pl.BlockSpec((tq, HEAD_DIM), qmap)],
        [pl.BlockSpec((tq, HEAD_DIM), qmap), pl.BlockSpec((r, HEAD_DIM), kvmap), pl.BlockSpec((r, HEAD_DIM), kvmap)],
        [jax.ShapeDtypeStruct((r, Q_W), F32), jax.ShapeDtypeStruct((r, KV_W), F32),
         jax.ShapeDtypeStruct((r, KV_W), F32)],
        (q, k, v, pw, do), ("arbitrary", "arbitrary", "arbitrary"), rider=rider)
    return outs if rider is None else (outs, r_outs)


def _segments(shape, rc):
    t = lax.broadcasted_iota(jnp.int32, shape, 0)
    lo = jnp.where(t < rc, 0, rc)
    hi = jnp.where(t < rc, rc, shape[0])
    return t, lo, hi


def _shifted(x, o, t, lo, hi):
    n = x.shape[0]
    sh = pltpu.roll(x, (-o) % n, axis=0)
    return jnp.where(jnp.logical_and(t + o >= lo, t + o < hi), sh, 0.0)


def _winsum(x, left, right, t, lo, hi):
    acc = x
    for o in range(-left, right + 1):
        if o != 0:
            acc = acc + _shifted(x, o, t, lo, hi)
    return acc


def _pool_parts(z, g, t, lo, hi):
    w = POOL_WINDOWS[g]
    left = w // 2
    right = w - 1 - left
    count = (jnp.minimum(t + right + 1, hi) - jnp.maximum(t - left, lo)).astype(F32)
    return _winsum(z, left, right, t, lo, hi) / count - z, count, left, right


def _pool_fwd(name, p, pool_w, pool_scale, rc):
    r = p.shape[0]

    def body(z_ref, w_ref, s_ref, y_ref):
        t, lo, hi = _segments((r, GC), rc)
        for g in range(N_GROUPS):
            cols = slice(g * GC, (g + 1) * GC)
            d, _, _, _ = _pool_parts(z_ref[:, cols], g, t, lo, hi)
            y_ref[:, cols] = (_dot(d, w_ref[g], _NN) * s_ref[:, cols]).astype(y_ref.dtype)

    return pl.pallas_call(
        body, name=name, grid=(1,),
        in_specs=[pl.BlockSpec((r, BR_W), lambda i: (0, OFF_POOL // BR_W)),
                  pl.BlockSpec((N_GROUPS, GC, GC), lambda i: (0, 0, 0)),
                  pl.BlockSpec((1, BR_W), lambda i: (0, 0))],
        out_specs=pl.BlockSpec((r, BR_W), lambda i: (0, 0)),
        out_shape=jax.ShapeDtypeStruct((r, BR_W), BF16),
        compiler_params=_cparams("arbitrary"),
    )(p, pool_w, pool_scale.reshape(1, BR_W))


def _pool_bwd(name, p, pool_w, pool_scale, dy, rc):
    r = p.shape[0]

    def body(z_ref, w_ref, s_ref, dy_ref, dz_ref, dw_ref, ds_ref):
        t, lo, hi = _segments((r, GC), rc)
        for g in range(N_GROUPS):
            cols = slice(g * GC, (g + 1) * GC)
            d, count, left, right = _pool_parts(z_ref[:, cols], g, t, lo, hi)
            dyg = dy_ref[:, cols]
            ds_ref[:, cols] = jnp.sum(dyg * _dot(d, w_ref[g], _NN), axis=0, keepdims=True)
            dlin = dyg * s_ref[:, cols]
            dw_ref[g] = _dot(d, dlin, _TN)
            dd = _dot(dlin, w_ref[g], _NT)
            dz_ref[:, cols] = (_winsum(dd / count, right, left, t, lo, hi) - dd).astype(dz_ref.dtype)

    return pl.pallas_call(
        body, name=name, grid=(1,),
        in_specs=[pl.BlockSpec((r, BR_W), lambda i: (0, OFF_POOL // BR_W)),
                  pl.BlockSpec((N_GROUPS, GC, GC), lambda i: (0, 0, 0)),
                  pl.BlockSpec((1, BR_W), lambda i: (0, 0)),
                  pl.BlockSpec((r, BR_W), lambda i: (0, 0))],
        out_specs=[pl.BlockSpec((r, BR_W), lambda i: (0, 0)),
                   pl.BlockSpec((N_GROUPS, GC, GC), lambda i: (0, 0, 0)),
                   pl.BlockSpec((1, BR_W), lambda i: (0, 0))],
        out_shape=[jax.ShapeDtypeStruct((r, BR_W), BF16), jax.ShapeDtypeStruct((N_GROUPS, GC, GC), F32),
                   jax.ShapeDtypeStruct((1, BR_W), F32)],
        compiler_params=_cparams("arbitrary"),
    )(p, pool_w, pool_scale.reshape(1, BR_W), dy)


def _f_sgu_v(pvg, lng, lnb):
    return _ln(jax.nn.gelu(pvg), lng, lnb)


def _sgu_fwd(name, p, ln_g, ln_b, sgu_w, sgu_b):
    r = p.shape[0]

    def body(pu_ref, pv_ref, g_ref, b_ref, w_ref, sb_ref, y_ref):
        vn = _f_sgu_v(pv_ref[...], g_ref[...], b_ref[...])
        u = jax.nn.gelu(pu_ref[...])
        for g in range(N_GROUPS):
            cols = slice(g * GC, (g + 1) * GC)
            s = _dot(w_ref[g], vn[:, cols], _NN) + sb_ref[g]
            y_ref[:, cols] = (u[:, cols] * s).astype(y_ref.dtype)

    return pl.pallas_call(
        body, name=name, grid=(r // SGU_CHUNK,),
        in_specs=[pl.BlockSpec((SGU_CHUNK, BR_W), lambda i: (i, OFF_U // BR_W)),
                  pl.BlockSpec((SGU_CHUNK, BR_W), lambda i: (i, OFF_VG // BR_W)),
                  pl.BlockSpec((1, BR_W), lambda i: (0, 0)), pl.BlockSpec((1, BR_W), lambda i: (0, 0)),
                  pl.BlockSpec((N_GROUPS, GC, GC), lambda i: (0, 0, 0)),
                  pl.BlockSpec((N_GROUPS, SGU_CHUNK, 1), lambda i: (0, 0, 0))],
        out_specs=pl.BlockSpec((SGU_CHUNK, BR_W), lambda i: (i, 0)),
        out_shape=jax.ShapeDtypeStruct((r, BR_W), BF16),
        compiler_params=_cparams("parallel"),
    )(p, p, ln_g.reshape(1, BR_W), ln_b.reshape(1, BR_W), sgu_w, sgu_b.reshape(N_GROUPS, SGU_CHUNK, 1))


def _sgu_bwd(name, p, ln_g, ln_b, sgu_w, sgu_b, dy):
    r = p.shape[0]

    def body(pu_ref, pv_ref, g_ref, b_ref, w_ref, sb_ref, dy_ref, dp_ref, dg_ref, db_ref, dw_ref, dsb_ref):
        i = pl.program_id(0)

        @pl.when(i == 0)
        def _():
            for ref in (dg_ref, db_ref, dw_ref, dsb_ref):
                ref[...] = jnp.zeros_like(ref)

        vn, vjp_v = jax.vjp(_f_sgu_v, pv_ref[...], g_ref[...], b_ref[...])
        u, vjp_u = jax.vjp(jax.nn.gelu, pu_ref[...])
        dy = dy_ref[...]
        du, dvn = [], []
        for g in range(N_GROUPS):
            cols = slice(g * GC, (g + 1) * GC)
            s = _dot(w_ref[g], vn[:, cols], _NN) + sb_ref[g]
            du.append(dy[:, cols] * s)
            ds = dy[:, cols] * u[:, cols]
            dsb_ref[g] += jnp.sum(ds, axis=1, keepdims=True)
            dw_ref[g] += _dot(ds, vn[:, cols], _NT)
            dvn.append(_dot(w_ref[g], ds, _TN))
        (dpu,) = vjp_u(jnp.concatenate(du, axis=1))
        dpv, dg, db = vjp_v(jnp.concatenate(dvn, axis=1))
        dp_ref[:, 0:BR_W] = dpu.astype(dp_ref.dtype)
        dp_ref[:, BR_W:2 * BR_W] = dpv.astype(dp_ref.dtype)
        dg_ref[...] += dg
        db_ref[...] += db

    vec = pl.BlockSpec((1, BR_W), lambda i: (0, 0))
    wsp = pl.BlockSpec((N_GROUPS, GC, GC), lambda i: (0, 0, 0))
    bsp = pl.BlockSpec((N_GROUPS, SGU_CHUNK, 1), lambda i: (0, 0, 0))
    return pl.pallas_call(
        body, name=name, grid=(r // SGU_CHUNK,),
        in_specs=[pl.BlockSpec((SGU_CHUNK, BR_W), lambda i: (i, OFF_U // BR_W)),
                  pl.BlockSpec((SGU_CHUNK, BR_W), lambda i: (i, OFF_VG // BR_W)),
                  vec, vec, wsp, bsp, pl.BlockSpec((SGU_CHUNK, BR_W), lambda i: (i, 0))],
        out_specs=[pl.BlockSpec((SGU_CHUNK, 2 * BR_W), lambda i: (i, 0)), vec, vec, wsp, bsp],
        out_shape=[jax.ShapeDtypeStruct((r, 2 * BR_W), BF16), jax.ShapeDtypeStruct((1, BR_W), F32),
                   jax.ShapeDtypeStruct((1, BR_W), F32), jax.ShapeDtypeStruct((N_GROUPS, GC, GC), F32),
                   jax.ShapeDtypeStruct((N_GROUPS, SGU_CHUNK, 1), F32)],
        compiler_params=_cparams("arbitrary"),
    )(p, p, ln_g.reshape(1, BR_W), ln_b.reshape(1, BR_W), sgu_w, sgu_b.reshape(N_GROUPS, SGU_CHUNK, 1), dy)


def _conv_w8(conv_w):
    return jnp.concatenate([conv_w, jnp.zeros((8 - conv_w.shape[0], conv_w.shape[1]), F32)], axis=0)


def _conv_fwd(name, p, conv_w, rc):
    r = p.shape[0]

    def body(cb_ref, cc_ref, cx_ref, w_ref, y_ref):
        t, lo, hi = _segments((r, GC), rc)
        z = cc_ref[...] * cx_ref[...]
        w = w_ref[...]
        c = _shifted(z, -1, t, lo, hi) * w[0:1] + z * w[1:2] + _shifted(z, 1, t, lo, hi) * w[2:3]
        y_ref[...] = (cb_ref[...] * c).astype(y_ref.dtype)

    nb = OFF_CB // GC
    return pl.pallas_call(
        body, name=name, grid=(N_GROUPS,),
        in_specs=[pl.BlockSpec((r, GC), lambda j: (0, nb + j)),
                  pl.BlockSpec((r, GC), lambda j: (0, nb + N_GROUPS + j)),
                  pl.BlockSpec((r, GC), lambda j: (0, nb + 2 * N_GROUPS + j)),
                  pl.BlockSpec((8, GC), lambda j: (0, j))],
        out_specs=pl.BlockSpec((r, GC), lambda j: (0, j)),
        out_shape=jax.ShapeDtypeStruct((r, BR_W), BF16),
        compiler_params=_cparams("parallel"),
    )(p, p, p, _conv_w8(conv_w))


def _conv_bwd(name, p, conv_w, dy, rc):
    r = p.shape[0]

    def body(cb_ref, cc_ref, cx_ref, w_ref, dy_ref, dcb_ref, dcc_ref, dcx_ref, dw_ref):
        t, lo, hi = _segments((r, GC), rc)
        cc, cx, w, dy = cc_ref[...], cx_ref[...], w_ref[...], dy_ref[...]
        z = cc * cx
        zp, zn = _shifted(z, -1, t, lo, hi), _shifted(z, 1, t, lo, hi)
        dcb_ref[...] = (dy * (zp * w[0:1] + z * w[1:2] + zn * w[2:3])).astype(dcb_ref.dtype)
        dc = dy * cb_ref[...]
        dw_ref[...] = jnp.concatenate(
            [jnp.sum(dc * zp, axis=0, keepdims=True), jnp.sum(dc * z, axis=0, keepdims=True),
             jnp.sum(dc * zn, axis=0, keepdims=True), jnp.zeros((5, GC), F32)], axis=0)
        dz = dc * w[1:2] + _shifted(dc, 1, t, lo, hi) * w[0:1] + _shifted(dc, -1, t, lo, hi) * w[2:3]
        dcc_ref[...] = (dz * cx).astype(dcc_ref.dtype)
        dcx_ref[...] = (dz * cc).astype(dcx_ref.dtype)

    nb = OFF_CB // GC
    return pl.pallas_call(
        body, name=name, grid=(N_GROUPS,),
        in_specs=[pl.BlockSpec((r, GC), lambda j: (0, nb + j)),
                  pl.BlockSpec((r, GC), lambda j: (0, nb + N_GROUPS + j)),
                  pl.BlockSpec((r, GC), lambda j: (0, nb + 2 * N_GROUPS + j)),
                  pl.BlockSpec((8, GC), lambda j: (0, j)),
                  pl.BlockSpec((r, GC), lambda j: (0, j))],
        out_specs=[pl.BlockSpec((r, GC), lambda j: (0, j))] * 3 + [pl.BlockSpec((8, GC), lambda j: (0, j))],
        out_shape=[jax.ShapeDtypeStruct((r, BR_W), BF16)] * 3 + [jax.ShapeDtypeStruct((8, BR_W), F32)],
        compiler_params=_cparams("parallel"),
    )(p, p, p, _conv_w8(conv_w), dy)


def _rope_tables(rc, n):
    rows = n // GRID_W
    row = jnp.repeat(jnp.arange(rows), GRID_W).astype(F32)
    col = jnp.tile(jnp.arange(GRID_W), rows).astype(F32)
    inv = ROPE_THETA ** (-jnp.arange(0, ROPE_AXIS_DIM, 2, dtype=F32) / ROPE_AXIS_DIM)
    ang_r, ang_c = row[:, None] * inv, col[:, None] * inv
    cos = jnp.concatenate([jnp.cos(ang_r), jnp.cos(ang_r), jnp.cos(ang_c), jnp.cos(ang_c)], axis=1)
    sin = jnp.concatenate([-jnp.sin(ang_r), jnp.sin(ang_r), -jnp.sin(ang_c), jnp.sin(ang_c)], axis=1)
    cos = jnp.concatenate([jnp.ones((rc, HEAD_DIM), F32), cos], axis=0)
    sin = jnp.concatenate([jnp.zeros((rc, HEAD_DIM), F32), sin], axis=0)
    return cos, sin


MOD_NAMES = ("sh1", "sc1", "g1", "sh2", "sc2", "g2")


def _local_step(xin, target, mod, comm, sp, rc, alpha):
    def carrying(fn):
        def call(name, *args, **kw):
            rider = comm.rider(name)
            if rider is None:
                return fn(name, *args, **kw)
            res, r_outs = fn(name, *args, rider=rider, **kw)
            comm.deliver(name, r_outs)
            return res
        return call

    mm, rows, attn_fwd, attn_bwd = carrying(_mm), carrying(_rows), carrying(_attn_fwd), carrying(_attn_bwd)
    mm_fused = carrying(_mm_fused)
    r, d = xin.shape
    n_layers = mod.shape[0]
    tm_n, tm_w = 256, 128
    nbc_n, nbc_w = rc // tm_n, rc // tm_w
    cos, sin = _rope_tables(rc, r - rc)
    mp = mod.reshape(n_layers, 2, 6, 1, d)
    mods = [{nm: mp[i, :, j] for j, nm in enumerate(MOD_NAMES)} for i in range(n_layers)]
    f_ln_mod, f_ln_last = _make_f_ln(alpha, True), _make_f_ln(alpha, False)

    def whole(arr, roff=0):
        return (arr, 0, arr.shape[1], roff)

    (hb,) = rows("mod_in", _f_mod, r, tm_n, nbc_n, [whole(xin)], [mods[0]["sc1"], mods[0]["sh1"]],
                 [(r, d, BF16, 0)], [])
    saved = []
    x = xin
    for i in range(n_layers):
        last = i == n_layers - 1
        w, s, m = functools.partial(comm.weight, i), sp[i], mods[i]
        sv = {"x": x, "hb": hb}
        p = mm(f"l{i}_in", hb, w("in_t"), "nt", F32)
        q, k, v = rows(f"l{i}_prep", _f_prep, r, tm_n, nbc_n,
                       [(p, 0, OFF_POOL, 0), whole(cos), whole(sin)], [_typed(s["q_norm_g"]), _typed(s["k_norm_g"])],
                       [(r, Q_W, BF16, 0), (r, KV_W, BF16, 0), (r, KV_W, BF16, 0)], [])
        att, sv["pw"] = attn_fwd(f"l{i}_attn", q, k, v, rc, not last)
        ys = [att,
              _pool_fwd(f"l{i}_pool", p, s["pool_w"], s["pool_scale"], rc),
              _sgu_fwd(f"l{i}_sgu", p, s["sgu_ln_g"], s["sgu_ln_b"], s["sgu_w"], s["sgu_b"]),
              _conv_fwd(f"l{i}_conv", p, s["conv_w"], rc)]
        ts = [mm(f"l{i}_br{kk}", ys[kk], w(f"br{kk}"), "nt", BF16) for kk in range(N_BRANCH)]
        gpre = mm(f"l{i}_gate", hb, w("gate_t"), "nt", BF16)
        (mg,) = rows(f"l{i}_merge", _f_gate, r, tm_w, nbc_w, [whole(gpre)] + [whole(t) for t in ts],
                     [_typed(s["b_gate"])], [(r, d, BF16, 0)], [])
        o = mm(f"l{i}_o", mg, w("o"), "nn", F32)
        x1, h2b = rows(f"l{i}_ln1", f_ln_mod, r, tm_n, nbc_n, [whole(x), whole(o)],
                       [m["g1"], _typed(s["ln1_g"]), _typed(s["ln1_b"]), m["sc2"], m["sh2"]],
                       [(r, d, F32, 0), (r, d, BF16, 0)], [])
        af, bf, f = mm_fused(f"l{i}_ffgu", h2b, [w("ffg_t"), w("ffu_t")],
                             lambda prods, _: (prods[0], prods[1], _f_swiglu(prods[0], prods[1])), [], [BF16, BF16, BF16])
        o2 = mm(f"l{i}_ffd", f, w("ffd"), "nn", F32)
        if last:
            (x2,) = rows(f"l{i}_ln2", f_ln_last, r, tm_n, nbc_n, [whole(x1), whole(o2)],
                         [m["g2"], _typed(s["ln2_g"]), _typed(s["ln2_b"])], [(r, d, F32, 0)], [])
            hb = None
        else:
            nx = mods[i + 1]
            x2, hb = rows(f"l{i}_ln2", f_ln_mod, r, tm_n, nbc_n, [whole(x1), whole(o2)],
                          [m["g2"], _typed(s["ln2_g"]), _typed(s["ln2_b"]), nx["sc1"], nx["sh1"]],
                          [(r, d, F32, 0), (r, d, BF16, 0)], [])
        sv.update(p=p, gpre=gpre, q=q, k=k, v=v, ys=ys, ts=ts, mg=mg, o=o, x1=x1, h2b=h2b, af=af, bf=bf, f=f, o2=o2)
        saved.append(sv)
        x = x2

    lat = jnp.concatenate([jnp.zeros((1, 1, 128), F32), jnp.ones((1, 1, 128), F32)], axis=0)

    def f_loss(xb, tb, msk):
        diff = (xb - tb) * msk[:, 0:1]
        part = jnp.sum(jnp.mean(jnp.square(diff), axis=-1, keepdims=True), axis=0, keepdims=True)
        return diff * (1.0 / d), jnp.broadcast_to(part, (1, 128))

    dx_direct, loss_acc = rows("loss", f_loss, r, tm_n, nbc_n, [whole(x), whole(target, nbc_n)], [lat],
                               [(r, d, F32, 0)], [128])
    loss = 0.5 * loss_acc[1, 0, 0]

    dmods = [dict() for _ in range(n_layers)]
    dsp = [dict() for _ in range(n_layers)]
    dh = None

    def small_done(j):
        ds = dict(dsp[j])
        for nm in ("ln1_g", "ln1_b", "ln2_g", "ln2_b", "b_gate", "q_norm_g", "k_norm_g"):
            ds[nm] = ds[nm][0, 0]
        for nm in ("pool_scale", "sgu_ln_g", "sgu_ln_b"):
            ds[nm] = ds[nm].reshape(-1)
        ds["sgu_b"] = ds["sgu_b"].reshape(N_GROUPS, SGU_CHUNK)
        comm.small_ready(j, jnp.concatenate([dmods[j][nm][:, 0, :] for nm in MOD_NAMES], axis=-1), ds)

    for i in reversed(range(n_layers)):
        last = i == n_layers - 1
        w, s, m, sv = functools.partial(comm.weight, i), sp[i], mods[i], saved[i]
        dm, dw, ds = dmods[i], {}, dsp[i]
        ln2 = [m["g2"], _typed(s["ln2_g"]), _typed(s["ln2_b"])]
        if last:
            res = rows(f"l{i}_ln2_bwd", _vjp_fn(f_ln_last, 2, 1), r, tm_n, nbc_n,
                       [whole(sv["x1"]), whole(sv["o2"]), whole(dx_direct)], ln2,
                       [(r, d, F32, 0), (r, d, BF16, 0)], [d, (d,), (d,)])
            dx1, do2, dm["g2"], dlg, dlb = res
        else:
            nx = mods[i + 1]
            res = rows(f"l{i}_ln2_bwd", _vjp_fn(f_ln_mod, 2, 2), r, tm_n, nbc_n,
                       [whole(sv["x1"]), whole(sv["o2"]), whole(dx_direct), whole(dh)],
                       ln2 + [nx["sc1"], nx["sh1"]],
                       [(r, d, F32, 0), (r, d, BF16, 0)], [d, (d,), (d,), d, d])
            dx1, do2, dm["g2"], dlg, dlb, dmods[i + 1]["sc1"], dmods[i + 1]["sh1"] = res
            small_done(i + 1)
        ds["ln2_g"], ds["ln2_b"] = dlg, dlb
        dab, dbb = mm_fused(f"l{i}_dF", do2, [w("ffd")],
                            lambda prods, tiles: jax.vjp(_f_swiglu, *tiles)[1](prods[0]), [sv["af"], sv["bf"]],
                            [BF16, BF16])
        comm.grads(i, {"ffd": mm(f"l{i}_dWffd", sv["f"], do2, "tn", BF16)})
        comm.grads(i, {"ffg_t": mm(f"l{i}_dWffg", dab, sv["h2b"], "tn", BF16)})
        comm.grads(i, {"ffu_t": mm(f"l{i}_dWffu", dbb, sv["h2b"], "tn", BF16)})
        dh2 = mm(f"l{i}_dh2a", dab, w("ffg_t"), "nn", F32)
        dh2 = mm(f"l{i}_dh2b", dbb, w("ffu_t"), "nn", F32, acc=dh2)
        res = rows(f"l{i}_ln1_bwd", _vjp_fn(f_ln_mod, 2, 2), r, tm_n, nbc_n,
                   [whole(sv["x"]), whole(sv["o"]), whole(dx1), whole(dh2)],
                   [m["g1"], _typed(s["ln1_g"]), _typed(s["ln1_b"]), m["sc2"], m["sh2"]],
                   [(r, d, F32, 0), (r, d, BF16, 0)], [d, (d,), (d,), d, d])
        dx_direct, do, dm["g1"], ds["ln1_g"], ds["ln1_b"], dm["sc2"], dm["sh2"] = res
        dmg = mm(f"l{i}_dMg", do, w("o"), "nt", F32)
        comm.grads(i, {"o": mm(f"l{i}_dWo", sv["mg"], do, "tn", BF16)})
        res = rows(f"l{i}_merge_bwd", _vjp_fn(_f_gate, 5, 1), r, tm_w, nbc_w,
                   [whole(sv["gpre"])] + [whole(t) for t in sv["ts"]] + [whole(dmg)], [_typed(s["b_gate"])],
                   [(r, N_BRANCH * d, BF16, 0)] + [(r, d, BF16, 0)] * N_BRANCH, [(N_BRANCH * d,)])
        dgb, dts, ds["b_gate"] = res[0], res[1:1 + N_BRANCH], res[1 + N_BRANCH]
        comm.grads(i, {"gate_t": mm(f"l{i}_dWgate", dgb, sv["hb"], "tn", BF16)})
        dys = [mm(f"l{i}_dY{kk}", dts[kk], w(f"br{kk}"), "nn", F32) for kk in range(N_BRANCH)]
        for kk in range(N_BRANCH):
            comm.grads(i, {f"br{kk}": mm(f"l{i}_dWbr{kk}", dts[kk], sv["ys"][kk], "tn", BF16)})
        dq, dk, dv = attn_bwd(f"l{i}_attn_bwd", sv["q"], sv["k"], sv["v"], sv["pw"], dys[0], rc, not last)
        res = rows(f"l{i}_prep_bwd", _vjp_fn(_f_prep, 3, 3, keep=(0, 3, 4)), r, tm_n, nbc_n,
                   [(sv["p"], 0, OFF_POOL, 0), whole(cos), whole(sin), whole(dq), whole(dk), whole(dv)],
                   [_typed(s["q_norm_g"]), _typed(s["k_norm_g"])],
                   [(r, OFF_POOL, BF16, 0)], [(HEAD_DIM,), (HEAD_DIM,)])
        dp_qkv, ds["q_norm_g"], ds["k_norm_g"] = res
        dp_pool, ds["pool_w"], ds["pool_scale"] = _pool_bwd(f"l{i}_pool_bwd", sv["p"], s["pool_w"], s["pool_scale"],
                                                            dys[1], rc)
        dp_sgu, ds["sgu_ln_g"], ds["sgu_ln_b"], ds["sgu_w"], ds["sgu_b"] = _sgu_bwd(
            f"l{i}_sgu_bwd", sv["p"], s["sgu_ln_g"], s["sgu_ln_b"], s["sgu_w"], s["sgu_b"], dys[2])
        dp_cb, dp_cc, dp_cx, dcw = _conv_bwd(f"l{i}_conv_bwd", sv["p"], s["conv_w"], dys[3], rc)
        ds["conv_w"] = dcw[0:3]
        dpb = jnp.concatenate([dp_qkv, dp_pool, dp_sgu, dp_cb, dp_cc, dp_cx], axis=1)
        comm.grads(i, {"in_t": mm(f"l{i}_dWin", dpb, sv["hb"], "tn", BF16)})
        dh = mm(f"l{i}_dhb_a", dpb, w("in_t"), "nn", F32)
        dh = mm(f"l{i}_dhb_b", dgb, w("gate_t"), "nn", F32, acc=dh)

    def f_mod_bwd(xb, ddir, dhb, sc, sh):
        _, vjp = jax.vjp(_f_mod, xb, sc, sh)
        dxb, dsc, dsh = vjp(dhb)
        return dxb + ddir, dsc, dsh

    grad_x, dmods[0]["sc1"], dmods[0]["sh1"] = rows(
        "mod_in_bwd", f_mod_bwd, r, tm_n, nbc_n, [whole(xin), whole(dx_direct), whole(dh)],
        [mods[0]["sc1"], mods[0]["sh1"]], [(r - rc, d, F32, nbc_n)], [d, d])
    small_done(0)
    return loss, grad_x


def _direct_rider(src):
    def peers():
        mx, my, mc = [lax.axis_index(a) for a in MESH_AXES]
        out = []
        for kk in range(1, N_DEV):
            px = 1 - mx if kk & 4 else mx
            py = 1 - my if kk & 2 else my
            pc = 1 - mc if kk & 1 else mc
            out.append(((px, py, pc), 4 * px + 2 * py + pc))
        return 4 * mx + 2 * my + mc, out

    def start(ins, outs, sems):
        send, recv, loc = sems
        me, others = peers()
        pltpu.make_async_copy(ins[0], outs[0].at[me], loc.at[0]).start()
        for j, (peer, _) in enumerate(others):
            _rcopy(ins[0], outs[0].at[me], send, recv, j, peer).start()

    def finish(ins, outs, sems):
        send, recv, loc = sems
        me, others = peers()
        for j, (peer, peer_l) in enumerate(others):
            cp = _rcopy(ins[0], outs[0].at[peer_l], send, recv, j, peer)
            cp.wait_recv()
            cp.wait_send()
        pltpu.make_async_copy(ins[0], outs[0].at[me], loc.at[0]).wait()

    return _Rider([src], [jax.ShapeDtypeStruct((N_DEV,) + src.shape, src.dtype)],
                  _sem_scratch(N_DEV - 1, N_DEV - 1, 1), start, finish)


def _mesh_place():
    mx, my, mc = [lax.axis_index(a) for a in MESH_AXES]
    chips = [(1 - mx, my), (mx, 1 - my), (1 - mx, 1 - my)]

    def lid(px, py, pc):
        return 4 * px + 2 * py + pc

    return (mx, my, mc), (mx, my, 1 - mc), chips, lid


def _rcopy(src, dst, send_sems, recv_sems, k, to):
    return pltpu.make_async_remote_copy(src_ref=src, dst_ref=dst, send_sem=send_sems.at[k], recv_sem=recv_sems.at[k],
                                        device_id=to, device_id_type=pl.DeviceIdType.MESH)


def _sem_scratch(*sizes):
    return [pltpu.SemaphoreType.DMA((s,)) for s in sizes]


def _gather_rider(src, rows1, rows2, buf=None):
    def place():
        (mx, my, mc), sib, chips, lid = _mesh_place()
        xn, yn, dg = [(*chip, mc) for chip in chips]
        return lid(mx, my, mc), sib, xn, yn, dg, lid

    def halves(rows):
        r0, r1 = rows
        mid = r0 + (r1 - r0) // 32 * 16
        return pl.ds(r0, mid - r0), pl.ds(mid, r1 - mid), pl.ds(r0, r1 - r0)

    n_src = 1 if rows1 is not None else 0

    def start(ins, outs, sems):
        send, recv, loc = sems
        me, sib, xn, yn, dg, lid = place()
        if rows1 is not None:
            win = pl.ds(rows1[0], rows1[1] - rows1[0])
            mine, dst = ins[0].at[win], outs[0].at[me, win]
            pltpu.make_async_copy(mine, dst, loc.at[0]).start()
            for t, to in enumerate((sib, xn, yn)):
                _rcopy(mine, dst, send, recv, t, to).start()
        if rows2 is not None:
            top, bot, win = halves(rows2)
            xb, yb = outs[0].at[lid(*xn)], outs[0].at[lid(*yn)]
            _rcopy(xb.at[top], xb.at[top], send, recv, 3, yn).start()
            _rcopy(yb.at[bot], yb.at[bot], send, recv, 4, xn).start()
            _rcopy(xb.at[win], xb.at[win], send, recv, 5, sib).start()
            _rcopy(yb.at[win], yb.at[win], send, recv, 6, sib).start()

    def finish(ins, outs, sems):
        send, recv, loc = sems
        me, sib, xn, yn, dg, lid = place()
        if rows2 is not None:
            top, bot, win = halves(rows2)
            db = outs[0].at[lid(*dg)]
            _rcopy(db.at[top], db.at[top], send, recv, 3, yn).wait_recv()
            _rcopy(db.at[bot], db.at[bot], send, recv, 4, xn).wait_recv()
            _rcopy(db.at[win], db.at[win], send, recv, 7, sib).start()
            for t, dev in ((5, xn), (6, yn), (7, dg)):
                blk = outs[0].at[lid(dev[0], dev[1], 1 - dev[2]), win]
                _rcopy(blk, blk, send, recv, t, sib).wait_recv()
            for t, part in ((3, top), (4, bot), (5, win), (6, win), (7, win)):
                _rcopy(db.at[part], db.at[part], send, recv, t, sib).wait_send()
        if rows1 is not None:
            win = pl.ds(rows1[0], rows1[1] - rows1[0])
            mine, dst = ins[0].at[win], outs[0].at[me, win]
            for t, dev in enumerate((sib, xn, yn)):
                cp = _rcopy(mine, outs[0].at[lid(*dev), win], send, recv, t, dev)
                cp.wait_recv()
                cp.wait_send()
            pltpu.make_async_copy(mine, dst, loc.at[0]).wait()

    out_shape = jax.ShapeDtypeStruct((N_DEV,) + src.shape, src.dtype)
    inputs = ([src] if n_src else []) + ([buf] if buf is not None else [])
    return _Rider(inputs, [out_shape], _sem_scratch(8, 8, 1), start, finish,
                  aliases={n_src: 0} if buf is not None else {})


def _sibling_rider(part):
    def start(ins, outs, sems):
        send, recv = sems
        (mx, my, mc), sib, chips, lid = _mesh_place()
        for t, slab in enumerate([lid(*sib)] + [lid(*chip, 1 - mc) for chip in chips]):
            _rcopy(ins[0].at[slab], outs[0].at[t], send, recv, t, sib).start()

    def finish(ins, outs, sems):
        send, recv = sems
        _, sib, _, _ = _mesh_place()
        for t in range(4):
            cp = _rcopy(ins[0].at[0], outs[0].at[t], send, recv, t, sib)
            cp.wait_recv()
            cp.wait_send()

    return _Rider([part], [jax.ShapeDtypeStruct((4,) + part.shape[1:], part.dtype)], _sem_scratch(4, 4), start, finish)


def _chips_rider(pair, rows, buf=None):
    r0, r1 = rows
    win = pl.ds(r0, r1 - r0)

    def start(ins, outs, sems):
        send, recv = sems
        (mx, my, mc), sib, chips, lid = _mesh_place()
        for j, chip in enumerate(chips):
            _rcopy(ins[0].at[j, win], outs[0].at[j, win], send, recv, j, (*chip, mc)).start()

    def finish(ins, outs, sems):
        send, recv = sems
        (mx, my, mc), sib, chips, lid = _mesh_place()
        for j, chip in enumerate(chips):
            cp = _rcopy(ins[0].at[j, win], outs[0].at[j, win], send, recv, j, (*chip, mc))
            cp.wait_recv()
            cp.wait_send()

    out_shape = jax.ShapeDtypeStruct(pair.shape, pair.dtype)
    if buf is None:
        return _Rider([pair], [out_shape], _sem_scratch(3, 3), start, finish)
    return _Rider([pair, buf], [out_shape], _sem_scratch(3, 3), start, finish, aliases={1: 0})


def _run_rider(name, rider):
    n_in, n_out = len(rider.inputs), len(rider.out_shapes)

    def body(*refs):
        ins, outs, sems = refs[:n_in], refs[n_in:n_in + n_out], refs[n_in + n_out:]
        rider.start(ins, outs, sems)
        rider.finish(ins, outs, sems)

    any_spec = pl.BlockSpec(memory_space=pl.ANY)
    res = pl.pallas_call(body, name=name, in_specs=[any_spec] * n_in, out_specs=[any_spec] * n_out,
                         out_shape=rider.out_shapes, scratch_shapes=rider.scratch,
                         input_output_aliases=rider.aliases)(*rider.inputs)
    return list(res)


def _slab_ids():
    (mx, my, mc), _, chips, lid = _mesh_place()
    return jnp.stack([lid(*chip, mc) for chip in chips] + [lid(mx, my, mc)]).astype(jnp.int32)


def _pair_sum(name, part, rsib, ids):
    _, n, k = part.shape
    tr = _row_tile(n, 512, 16)

    def body(ids_ref, p_ref, r_ref, o_ref):
        o_ref[...] = (p_ref[...].astype(F32) + r_ref[...].astype(F32)).astype(o_ref.dtype)

    grid_spec = pltpu.PrefetchScalarGridSpec(
        num_scalar_prefetch=1, grid=(3, n // tr),
        in_specs=[pl.BlockSpec((None, tr, k), lambda j, i, ids: (ids[j], i, 0)),
                  pl.BlockSpec((None, tr, k), lambda j, i, ids: (1 + j, i, 0))],
        out_specs=pl.BlockSpec((None, tr, k), lambda j, i, ids: (j, i, 0)))
    return pl.pallas_call(body, name=name, grid_spec=grid_spec, out_shape=jax.ShapeDtypeStruct((3, n, k), part.dtype),
                          compiler_params=_cparams("parallel", "parallel"))(ids, part, rsib)


def _sum5(name, part, rsib, rici, ids, layer, stacked, rider=None):
    _, n, k = part.shape
    tr = _row_tile(n, 512, 16)
    first = isinstance(stacked, int)

    def body(ids_ref, p_ref, r_ref, c_ref, *rest):
        acc = p_ref[...].astype(F32) + r_ref[...].astype(F32)
        for j in range(3):
            acc = acc + c_ref[j].astype(F32)
        rest[-1][...] = acc

    in_specs = [pl.BlockSpec((None, tr, k), lambda i, ids: (ids[3], i, 0)),
                pl.BlockSpec((None, tr, k), lambda i, ids: (0, i, 0)),
                pl.BlockSpec((3, tr, k), lambda i, ids: (0, i, 0))] + ([] if first else [pl.BlockSpec(memory_space=pl.ANY)])
    n_layers = stacked if first else stacked.shape[0]
    outs, r_outs = _pcall(name, body, (n // tr,), in_specs, [pl.BlockSpec((None, tr, k), lambda i, ids: (layer, i, 0))],
                          [jax.ShapeDtypeStruct((n_layers, n, k), F32)],
                          (part, rsib, rici) + (() if first else (stacked,)), ("parallel",), rider=rider, prefetch=ids,
                          aliases={} if first else {3: 0})
    return outs[0] if rider is None else (outs[0], r_outs)


W_KEYS = ("in_t", "br0", "br1", "br2", "br3", "gate_t", "o", "ffg_t", "ffu_t", "ffd")
SUMS_TRANSPOSED_LATER = ("gate_t", "br0", "br1", "br2", "br3")


CARRIER_US = {"mod_in": 12, "in": 55, "gate": 95, "prep": 19, "attn": 125,"br0": 15, "merge": 50, "o": 25, "ln1": 27,
              "ffgu": 135, "ffd": 73, "ln2": 27, "loss": 20, "ln2_bwd": 44, "dF": 80,
              "dWffd": 64, "dh2a": 75, "dh2b": 75, "dWffg": 64, "dWffu": 64, "ln1_bwd": 44,
              "dMg": 25, "dWo": 25, "merge_bwd": 80, "dY0": 14, "dWbr0": 15, "attn_bwd": 140,"prep_bwd": 28,
              "dWin": 54, "dWgate": 95, "dhb_a": 64, "dhb_b": 115}
ICI_US_PER_MIB = 45.0
GATHER_US_PER_MIB = 30.0
RELAY_US_PER_MIB = 15.0
D2D_US_PER_MIB = 6.8
MIN_CHUNK_US = 10.0
CARRIER_FILL_FORWARD = 1.4
CARRIER_FILL_BACKWARD = 0.95


class _Comm:
    def __init__(self, wsrc):
        self.wsrc = wsrc
        self.n_layers = len(wsrc)
        self.queue = []
        self.riding = {}
        self.n_alone = 0
        self.buf, self.left = {}, {}
        self.part, self.rsib, self.pair = {}, {}, {}
        self.ids = _slab_ids()
        for i in range(self.n_layers):
            for k in W_KEYS:
                self._push_chunks("gather", ("w", i, k), wsrc[i][k].shape, wsrc[i][k].dtype)

    def _push_chunks(self, kind, item, shape, dtype):
        n, k = shape[-2], shape[-1]
        mib = n * k * jnp.dtype(dtype).itemsize / 2 ** 20
        pieces = max(1, int(mib * ICI_US_PER_MIB // MIN_CHUNK_US))
        while n % (16 * pieces):
            pieces -= 1
        step = n // pieces
        self.left[item] = n
        us = mib * (GATHER_US_PER_MIB if kind == "gather" else ICI_US_PER_MIB) / pieces
        for c in range(pieces):
            self.queue.append(dict(kind=kind, item=item, rows=(c * step, (c + 1) * step), rows2=None, us=us))

    @staticmethod
    def _merge(units, u):
        def joined(a, b):
            if a is None or b is None:
                return True, a or b
            return a[1] == b[0], (a[0], b[1])

        for v in units:
            if v["item"] == u["item"] and v["kind"] == u["kind"] and u["kind"] != "sibling":
                ok1, rows = joined(v["rows"], u["rows"])
                ok2, rows2 = joined(v["rows2"], u["rows2"])
                if ok1 and ok2:
                    v.update(rows=rows, rows2=rows2, us=v["us"] + u["us"])
                    return True
        return False

    def _unit_rider(self, u):
        item = u["item"]
        if u["kind"] == "gather":
            src = self.wsrc[item[1]][item[2]] if item[0] == "w" else self.part[item]
            return _gather_rider(src, u["rows"], u["rows2"], self.buf.get(item))
        if u["kind"] == "sibling":
            return _sibling_rider(self.part[item])
        return _chips_rider(self.pair[item], u["rows"], self.buf.get(item))

    def _done(self, u, out):
        item = u["item"]
        if u["kind"] == "sibling":
            self.rsib[item] = out
            self.pair[item] = _pair_sum(f"pair_l{item[1]}_{item[2]}", self.part[item], out, self.ids)
            self._push_chunks("chips", item, self.pair[item].shape, self.pair[item].dtype)
            return
        self.buf[item] = out
        if u["kind"] == "gather":
            if u["rows"] is not None:
                rows = u["rows"]
                mib = (rows[1] - rows[0]) * out.shape[-1] * out.dtype.itemsize / 2 ** 20
                self.queue.insert(0, dict(kind="gather", item=item, rows=None, rows2=rows, us=mib * RELAY_US_PER_MIB))
            if u["rows2"] is not None:
                self.left[item] -= u["rows2"][1] - u["rows2"][0]

    def _send(self, name, units, call):
        outs = call(_compose([self._unit_rider(u) for u in units]))
        for u, o in zip(units, outs):
            self._done(u, o)

    def exchange(self, name, src, budget_us):
        units = self._take(budget_us)
        outs = _run_rider(name, _compose([_direct_rider(src)] + [self._unit_rider(u) for u in units]))
        for u, o in zip(units, outs[1:]):
            self._done(u, o)
        return outs[0]

    def rider(self, name, budget_us=None):
        budget = CARRIER_US.get(name.split("_", 1)[1] if name[0] == "l" and name[1].isdigit() else name, 0) \
            if budget_us is None else budget_us
        units = self._take(budget)
        if not units:
            return None
        self.riding[name] = units
        return _compose([self._unit_rider(u) for u in units])

    def _take(self, budget):
        units, used = [], 0.0
        forward = bool(self.queue) and self.queue[0]["item"][0] == "w"
        fill = CARRIER_FILL_FORWARD if forward else CARRIER_FILL_BACKWARD
        while self.queue and used + self.queue[0]["us"] <= fill * budget:
            u = self.queue[0]
            if not self._merge(units, u):
                if any(v["item"] == u["item"] for v in units):
                    break
                units.append(dict(u))
            used += u["us"]
            del self.queue[0]
        return units

    def deliver(self, name, outs):
        for u, o in zip(self.riding.pop(name), outs):
            self._done(u, o)

    def _flush(self, item, kinds):
        hits = [p for p, u in enumerate(self.queue) if u["item"] == item and u["kind"] in kinds]
        if not hits:
            return
        prefix = self.queue[:hits[-1] + 1]
        del self.queue[:hits[-1] + 1]
        units = []
        for u in prefix:
            if not self._merge(units, u):
                units.append(dict(u))
        tag = "_".join(str(t) for t in item) + "_" + kinds[0]
        batches = [[]]
        for u in units:
            if any(v["item"] == u["item"] for v in batches[-1]):
                batches.append([])
            batches[-1].append(u)
        for batch in batches:
            self.n_alone += 1
            self._send(None, batch, functools.partial(_run_rider, f"alone{self.n_alone}_{tag}"))

    def begin(self):
        self._flush(("w", 0, "in_t"), ("gather",))

    def _finish_gather(self, item):
        while self.left[item] > 0:
            assert any(u["item"] == item for u in self.queue), item
            self._flush(item, ("gather",))
        return self.buf[item]

    def weight(self, i, k):
        o = self._finish_gather(("w", i, k))
        return o.reshape(-1, o.shape[-1])

    def grads(self, i, group):
        for k, g in group.items():
            item = ("g", i, k)
            self.part[item] = g.reshape(N_DEV, g.shape[0] // N_DEV, g.shape[1])
            us = g.size // N_DEV * g.dtype.itemsize / 2 ** 20 * D2D_US_PER_MIB
            self.queue.insert(0, dict(kind="sibling", item=item, rows=None, us=us))

    def total(self, k):
        out = self.n_layers
        for i in range(self.n_layers):
            item = ("g", i, k)
            self._flush(item, ("sibling",))
            self._flush(item, ("chips",))
            name = f"sum_l{i}_{k}"
            rider = self.rider(name, budget_us=self.part[item][0].size / 1.06e5) if k in SUMS_TRANSPOSED_LATER else None
            out = _sum5(name, self.part[item], self.rsib[item], self.buf[item], self.ids, i, out, rider=rider)
            if rider is not None:
                out, r_outs = out
                self.deliver(name, r_outs)
        return out

    def small_ready(self, i, dmod, ds):
        parts = [dmod[0], dmod[1]] + [ds[nm] for nm in LAYER_SMALL + ("conv_w",)]
        self.small_shapes = [p.shape for p in parts]
        self.gather_small(f"lat{i}", _pack([dmod[1]]))
        self.gather_small(f"small{i}", _pack(parts))

    def gather_small(self, name, arr):
        item = ("s", name)
        self.part[item] = arr
        waiting, self.queue = self.queue, []
        self._push_chunks("gather", item, arr.shape, arr.dtype)
        self.queue += waiting

    def gathered(self, name):
        return self._finish_gather(("s", name))


def _row_tile(n, pref, mult):
    best = None
    t = mult
    while t <= min(n, pref):
        if n % t == 0:
            best = t
        t += mult
    return best if best is not None else n


def _sum8(name, slabs):
    _, n, k = slabs.shape
    tr = _row_tile(n, 128, 16)

    def body(s_ref, o_ref):
        acc = s_ref[0].astype(F32)
        for j in range(1, N_DEV):
            acc = acc + s_ref[j].astype(F32)
        o_ref[...] = acc

    return pl.pallas_call(
        body, name=name, grid=(n // tr,),
        in_specs=[pl.BlockSpec((N_DEV, tr, k), lambda i: (0, i, 0))],
        out_specs=pl.BlockSpec((tr, k), lambda i: (i, 0)),
        out_shape=jax.ShapeDtypeStruct((n, k), F32),
        compiler_params=_cparams("parallel"),
    )(slabs)


def _adamw(name, w, g, m, v, rider=None):
    n, k = w.shape[-2:]
    tr = _row_tile(n, 256, 8)

    def body(w_ref, g_ref, m_ref, v_ref, d_ref, m2_ref, v2_ref):
        gv = g_ref[...]
        m2 = ADAM_B1 * m_ref[...] + (1.0 - ADAM_B1) * gv
        v2 = ADAM_B2 * v_ref[...] + (1.0 - ADAM_B2) * jnp.square(gv)
        m_hat = m2 / (1.0 - ADAM_B1 ** ADAM_STEP)
        v_hat = v2 / (1.0 - ADAM_B2 ** ADAM_STEP)
        d_ref[...] = -ADAM_LR * (m_hat / (jnp.sqrt(v_hat) + ADAM_EPS) + ADAM_WD * w_ref[...])
        m2_ref[...] = m2
        v2_ref[...] = v2

    if w.ndim == 2:
        grid, spec = (n // tr,), pl.BlockSpec((tr, k), lambda i: (i, 0))
    else:
        grid, spec = (w.shape[0], n // tr), pl.BlockSpec((None, tr, k), lambda l, i: (l, i, 0))
    outs, r_outs = _pcall(name, body, grid, [spec] * 4, [spec] * 3, [jax.ShapeDtypeStruct(w.shape, F32)] * 3,
                          (w, g, m, v), ("parallel",) * len(grid), rider=rider)
    return outs if rider is None else (outs, r_outs)


def _pack(arrs):
    flat = jnp.concatenate([a.reshape(-1).astype(F32) for a in arrs])
    pad = (-flat.shape[0]) % 2048
    if pad:
        flat = jnp.concatenate([flat, jnp.zeros((pad,), F32)])
    return flat.reshape(-1, 128)


def _unpack(packed, shapes):
    flat = packed.reshape(-1)
    out, off = [], 0
    for shp in shapes:
        size = math.prod(shp)
        out.append(flat[off:off + size].reshape(shp))
        off += size
    return out


WEIGHT_NAMES = ("c_ctx", "w_ada", "b_ada", "w_in", "q_norm_g", "k_norm_g", "pool_w", "pool_scale", "sgu_ln_g",
                "sgu_ln_b", "sgu_w", "sgu_b", "conv_w", "w_br_attn", "w_br_pool", "w_br_sgu", "w_br_conv", "w_gate",
                "b_gate", "w_o", "ln1_g", "ln1_b", "w_ff_gate", "w_ff_up", "w_ff_down", "ln2_g", "ln2_b")
COL_SHARDED = {"w_in": "in_t", "w_gate": "gate_t", "w_ff_gate": "ffg_t", "w_ff_up": "ffu_t",
               "w_br_attn": "br0", "w_br_pool": "br1", "w_br_sgu": "br2", "w_br_conv": "br3"}
ROW_SHARDED = {"w_o": "o", "w_ff_down": "ffd"}
LAYER_SMALL = ("q_norm_g", "k_norm_g", "pool_w", "pool_scale", "sgu_ln_g", "sgu_ln_b", "sgu_w", "sgu_b", "b_gate",
               "ln1_g", "ln1_b", "ln2_g", "ln2_b")
SMALL_ORDER = ("c_ctx", "b_ada") + LAYER_SMALL + ("conv_w",)


def _train_step(a):
    n_layers, d = a["w_in"].shape[0], a["x"].shape[-1]
    rc = a["ctx"].shape[1]
    alpha = (2 * n_layers) ** 0.25
    mx, my, mc = [lax.axis_index(ax) for ax in MESH_AXES]
    me = 4 * mx + 2 * my + mc
    ada_w = a["w_ada"].shape[-1]
    cw_loc = a["conv_w"].shape[-1]

    comm = _Comm([{**{key: jnp.swapaxes(a[nm], 1, 2)[i].astype(BF16) for nm, key in COL_SHARDED.items()},
                   **{key: a[nm][i].astype(BF16) for nm, key in ROW_SHARDED.items()}} for i in range(n_layers)])

    def carried(name, *args, budget_us, **kw):
        rider = comm.rider(name, budget_us=budget_us)
        if rider is None:
            return _mm(name, *args, **kw)
        res, r_outs = _mm(name, *args, rider=rider, **kw)
        comm.deliver(name, r_outs)
        return res

    n_c, n_cw = d, n_layers * 3 * cw_loc
    got = comm.exchange("gather_cond", _pack([a["c"], a["conv_w"]]), budget_us=12).reshape(N_DEV, -1)
    c_all = got[:, :n_c]
    conv_w = got[:, n_c:n_c + n_cw].reshape(N_DEV, n_layers, 3, cw_loc).transpose(1, 2, 0, 3).reshape(n_layers, 3, -1)
    cond = jnp.concatenate([c_all, a["c_ctx"][None], jnp.zeros((16 - N_DEV - 1, d), F32)], axis=0)
    sil, sil_vjp = jax.vjp(jax.nn.silu, cond)
    sil = sil.astype(BF16)

    mod_cols = jnp.concatenate([carried(f"ada{i}", sil, a["w_ada"][i], "nn", F32, budget_us=20)
                                for i in range(n_layers)], axis=0)
    got = comm.exchange("gather_mod", mod_cols, budget_us=22)
    mod_all = got.reshape(N_DEV, n_layers, 16, ada_w).transpose(1, 2, 0, 3).reshape(n_layers, 16, -1)
    mod_all = mod_all + a["b_ada"][:, None, :]
    mod = jnp.stack([mod_all[:, N_DEV], lax.dynamic_index_in_dim(mod_all, me, axis=1, keepdims=False)], axis=1)
    comm.begin()
    sp = [{nm: a[nm][i] for nm in LAYER_SMALL} for i in range(n_layers)]
    for i in range(n_layers):
        sp[i]["conv_w"] = conv_w[i]

    xin = jnp.concatenate([a["ctx"][0], a["x"][0]], axis=0)
    loss_l, grad_x = _local_step(xin, a["loss_target"][0], mod, comm, sp, rc, alpha)
    loss = lax.psum(loss_l, MESH_AXES)
    grads = {}

    def transposed_home(nm):
        return nm in COL_SHARDED and a[nm].shape[-1] % 128 != 0

    delta, new_m, new_v = {}, {}, {}

    def adamw(nm):
        name = f"adamw_{nm}"
        there = transposed_home(nm)
        view = (lambda t: jnp.swapaxes(t, 1, 2)) if there else (lambda t: t)
        if nm in COL_SHARDED:
            g = comm.total(COL_SHARDED[nm])
            grads[nm] = jnp.swapaxes(g, 1, 2)
            g = g if there else grads[nm]
        elif nm in ROW_SHARDED:
            g = grads[nm] = comm.total(ROW_SHARDED[nm])
        else:
            g = grads[nm]
        res = _adamw(name, view(a[nm]), g, view(a["m_" + nm]), view(a["v_" + nm]))
        delta[nm], new_m[nm], new_v[nm] = [view(t) for t in res]

    for nm in ("w_ff_down", "w_ff_gate", "w_ff_up", "w_o", "w_br_attn", "w_br_pool", "w_br_sgu", "w_br_conv"):
        adamw(nm)

    tots = [_unpack(_sum8(f"sum_small{i}", comm.gathered(f"small{i}")), comm.small_shapes) for i in range(n_layers)]
    dmod_c, dmod_lat_sum = jnp.stack([t[0] for t in tots]), jnp.stack([t[1] for t in tots])
    for j, nm in enumerate(LAYER_SMALL + ("conv_w",)):
        grads[nm] = jnp.stack([t[2 + j] for t in tots])
    grads["conv_w"] = lax.dynamic_slice_in_dim(grads["conv_w"], me * cw_loc, cw_loc, axis=2)
    grads["b_ada"] = dmod_c + dmod_lat_sum
    dmod_lat_all = jnp.stack([comm.gathered(f"lat{i}").reshape(N_DEV, -1)[:, :6 * d] for i in range(n_layers)])
    dm_rows = jnp.concatenate([dmod_lat_all, dmod_c[:, None, :],
                               jnp.zeros((n_layers, 16 - N_DEV - 1, 6 * d), F32)], axis=1)
    dm_cols = lax.dynamic_slice_in_dim(dm_rows, me * ada_w, ada_w, axis=2).astype(BF16)
    grads["w_ada"] = jnp.stack([carried(f"dWada{i}", sil, dm_cols[i], "tn", F32, budget_us=20)
                                for i in range(n_layers)])
    dsil = None
    for i in range(n_layers):
        dsil = carried(f"dsil{i}", dm_cols[i], a["w_ada"][i], "nt", F32, acc=dsil, budget_us=10)
    dsil = _sum8("sum_dsil", comm.exchange("gather_dsil", dsil, budget_us=17))
    grads["c_ctx"] = sil_vjp(dsil)[0][N_DEV]

    for nm in ("w_ada", "w_in", "w_gate"):
        adamw(nm)
    shapes = [a[nm].shape for nm in SMALL_ORDER]
    res = _adamw("adamw_small", _pack([a[nm] for nm in SMALL_ORDER]), _pack([grads[nm] for nm in SMALL_ORDER]),
                 _pack([a["m_" + nm] for nm in SMALL_ORDER]), _pack([a["v_" + nm] for nm in SMALL_ORDER]))
    for tree, packed in zip((delta, new_m, new_v), res):
        for nm, t in zip(SMALL_ORDER, _unpack(packed, shapes)):
            tree[nm] = t
    return (loss, grad_x[None], *[grads[nm] for nm in WEIGHT_NAMES], *[delta[nm] for nm in WEIGHT_NAMES],
            *[new_m[nm] for nm in WEIGHT_NAMES], *[new_v[nm] for nm in WEIGHT_NAMES])


def kernel(x, c, ctx, c_ctx, w_ada, b_ada, w_in, q_norm_g, k_norm_g, pool_w, pool_scale, sgu_ln_g, sgu_ln_b, sgu_w, sgu_b, conv_w, w_br_attn, w_br_pool, w_br_sgu, w_br_conv, w_gate, b_gate, w_o, ln1_g, ln1_b, w_ff_gate, w_ff_up, w_ff_down, ln2_g, ln2_b, loss_target, m_c_ctx, m_w_ada, m_b_ada, m_w_in, m_q_norm_g, m_k_norm_g, m_pool_w, m_pool_scale, m_sgu_ln_g, m_sgu_ln_b, m_sgu_w, m_sgu_b, m_conv_w, m_w_br_attn, m_w_br_pool, m_w_br_sgu, m_w_br_conv, m_w_gate, m_b_gate, m_w_o, m_ln1_g, m_ln1_b, m_w_ff_gate, m_w_ff_up, m_w_ff_down, m_ln2_g, m_ln2_b, v_c_ctx, v_w_ada, v_b_ada, v_w_in, v_q_norm_g, v_k_norm_g, v_pool_w, v_pool_scale, v_sgu_ln_g, v_sgu_ln_b, v_sgu_w, v_sgu_b, v_conv_w, v_w_br_attn, v_w_br_pool, v_w_br_sgu, v_w_br_conv, v_w_gate, v_b_gate, v_w_o, v_ln1_g, v_ln1_b, v_w_ff_gate, v_w_ff_up, v_w_ff_down, v_ln2_g, v_ln2_b):
    names = list(WEIGHT_NAMES)
    args = dict(zip(
        ["x", "c", "ctx"] + names + ["loss_target"] + ["m_" + n for n in names] + ["v_" + n for n in names],
        (x, c, ctx, c_ctx, w_ada, b_ada, w_in, q_norm_g, k_norm_g, pool_w, pool_scale, sgu_ln_g, sgu_ln_b, sgu_w, sgu_b, conv_w, w_br_attn, w_br_pool, w_br_sgu, w_br_conv, w_gate, b_gate, w_o, ln1_g, ln1_b, w_ff_gate, w_ff_up, w_ff_down, ln2_g, ln2_b, loss_target, m_c_ctx, m_w_ada, m_b_ada, m_w_in, m_q_norm_g, m_k_norm_g, m_pool_w, m_pool_scale, m_sgu_ln_g, m_sgu_ln_b, m_sgu_w, m_sgu_b, m_conv_w, m_w_br_attn, m_w_br_pool, m_w_br_sgu, m_w_br_conv, m_w_gate, m_b_gate, m_w_o, m_ln1_g, m_ln1_b, m_w_ff_gate, m_w_ff_up, m_w_ff_down, m_ln2_g, m_ln2_b, v_c_ctx, v_w_ada, v_b_ada, v_w_in, v_q_norm_g, v_k_norm_g, v_pool_w, v_pool_scale, v_sgu_ln_g, v_sgu_ln_b, v_sgu_w, v_sgu_b, v_conv_w, v_w_br_attn, v_w_br_pool, v_w_br_sgu, v_w_br_conv, v_w_gate, v_b_gate, v_w_o, v_ln1_g, v_ln1_b, v_w_ff_gate, v_w_ff_up, v_w_ff_down, v_ln2_g, v_ln2_b)))
    return _train_step(args)
```

```python
import functools
import math

import jax
import jax.numpy as jnp
from jax import lax
from jax.experimental import pallas as pl
from jax.experimental.pallas import tpu as pltpu

F32 = jnp.float32
BF16 = jnp.bfloat16

N_DEV = 8
MESH_AXES = ("x", "y", "c")
V7X_VMEM_LIMIT_BYTES = 56 * 1024 * 1024

GRID_W = 64
HEAD_DIM = 128
N_HEADS = 8
N_KV_HEADS = 2
KV_GROUP = N_HEADS // N_KV_HEADS
Q_W = N_HEADS * HEAD_DIM
KV_W = N_KV_HEADS * HEAD_DIM
ROPE_THETA = 10000.0
ROPE_AXIS_DIM = HEAD_DIM // 2
POOL_WINDOWS = (2, 4, 8, 16)
GC = 128
N_GROUPS = 4
BR_W = N_GROUPS * GC
SGU_CHUNK = 128
N_BRANCH = 4
LN_EPS = 1e-5
RMS_EPS = 1e-6
OFF_K = Q_W
OFF_V = OFF_K + KV_W
OFF_POOL = OFF_V + KV_W
OFF_U = OFF_POOL + BR_W
OFF_VG = OFF_U + BR_W
OFF_CB = OFF_VG + BR_W
OFF_CC = OFF_CB + BR_W
OFF_CX = OFF_CC + BR_W
IN_W = OFF_CX + BR_W
ATT_SCALE = HEAD_DIM ** -0.5

ADAM_LR = 0.001
ADAM_B1 = 0.9
ADAM_B2 = 0.999
ADAM_EPS = 1e-08
ADAM_WD = 0.01
ADAM_STEP = 10

_NT = (((1,), (1,)), ((), ()))
_NN = (((1,), (0,)), ((), ()))
_TN = (((0,), (0,)), ((), ()))
_DIMS = {"nt": _NT, "nn": _NN, "tn": _TN}


def _cparams(*sem):
    return pltpu.CompilerParams(dimension_semantics=sem, vmem_limit_bytes=V7X_VMEM_LIMIT_BYTES)


def _tile(dim, pref):
    best = None
    t = 128
    while t <= min(dim, pref):
        if dim % t == 0:
            best = t
        t += 128
    return best if best is not None else dim


def _dot(a, b, dims):
    return lax.dot_general(a.astype(BF16), b.astype(BF16), dims, preferred_element_type=F32)


class _Rider:
    def __init__(self, inputs, out_shapes, scratch, start, finish, aliases=None):
        self.inputs, self.out_shapes, self.scratch = list(inputs), list(out_shapes), list(scratch)
        self.start, self.finish = start, finish
        self.aliases = dict(aliases or {})


def _compose(riders):
    inputs, outs, scratch, aliases, spans = [], [], [], {}, []
    for rd in riders:
        i0, o0, s0 = len(inputs), len(outs), len(scratch)
        aliases.update({i0 + p: o0 + q for p, q in rd.aliases.items()})
        inputs += rd.inputs
        outs += rd.out_shapes
        scratch += rd.scratch
        spans.append((slice(i0, len(inputs)), slice(o0, len(outs)), slice(s0, len(scratch))))

    def start(ins, os, sems):
        for rd, (si, so, ss) in zip(riders, spans):
            rd.start(ins[si], os[so], sems[ss])

    def finish(ins, os, sems):
        for rd, (si, so, ss) in zip(riders, spans):
            rd.finish(ins[si], os[so], sems[ss])

    return _Rider(inputs, outs, scratch, start, finish, aliases)


def _pcall(name, body, grid, in_specs, out_specs, out_shape, args, sem, scratch=(), rider=None, prefetch=None,
           aliases=None):
    in_specs, out_specs, out_shape, scratch = list(in_specs), list(out_specs), list(out_shape), list(scratch)
    n_pre = 0 if prefetch is None else 1
    n_in, n_out, n_scr = len(in_specs), len(out_specs), len(scratch)
    r_in, r_out = (len(rider.inputs), len(rider.out_shapes)) if rider is not None else (0, 0)
    any_spec = pl.BlockSpec(memory_space=pl.ANY)
    io_aliases = {n_pre + p: q for p, q in (aliases or {}).items()}
    if rider is not None:
        io_aliases.update({n_pre + n_in + p: n_out + q for p, q in rider.aliases.items()})
        in_specs, out_specs = in_specs + [any_spec] * r_in, out_specs + [any_spec] * r_out
        out_shape, scratch = out_shape + rider.out_shapes, scratch + rider.scratch
        args, sem = (*args, *rider.inputs), ["arbitrary"] * len(grid)

    def wrapped(*refs):
        pre, refs = refs[:n_pre], refs[n_pre:]
        if rider is None:
            return body(*pre, *refs)
        ins, refs = refs[:n_in], refs[n_in:]
        r_ins, refs = refs[:r_in], refs[r_in:]
        outs, refs = refs[:n_out], refs[n_out:]
        r_outs, refs = refs[:r_out], refs[r_out:]
        scr, r_scr = refs[:n_scr], refs[n_scr:]
        first = functools.reduce(jnp.logical_and, [pl.program_id(ax) == 0 for ax in range(len(grid))])
        last = functools.reduce(jnp.logical_and, [pl.program_id(ax) == grid[ax] - 1 for ax in range(len(grid))])

        @pl.when(first)
        def _():
            rider.start(r_ins, r_outs, r_scr)

        body(*pre, *ins, *outs, *scr)

        @pl.when(last)
        def _():
            rider.finish(r_ins, r_outs, r_scr)

    if prefetch is None:
        res = pl.pallas_call(wrapped, name=name, grid=grid, in_specs=in_specs, out_specs=out_specs, out_shape=out_shape,
                             scratch_shapes=scratch, input_output_aliases=io_aliases,
                             compiler_params=_cparams(*sem))(*args)
    else:
        grid_spec = pltpu.PrefetchScalarGridSpec(num_scalar_prefetch=1, grid=grid, in_specs=in_specs,
                                                 out_specs=out_specs, scratch_shapes=scratch)
        res = pl.pallas_call(wrapped, name=name, grid_spec=grid_spec, out_shape=out_shape,
                             input_output_aliases=io_aliases, compiler_params=_cparams(*sem))(prefetch, *args)
    return list(res[:n_out]), list(res[n_out:])


V7X_MM_VMEM_BUDGET = 40 * 1024 * 1024


def _mm_plan(form, m, n, k, a_size, b_size, o_size, has_acc):
    tn = n if (form == "tn" and n <= 2048) else _tile(n, 512)
    rows = sorted({m} | {t for t in range(128, m, 128) if m % t == 0}, reverse=True)
    for tk, min_tm in ((k, 384), (k if k <= 2816 else _tile(k, 2816), 0)):
        nk = k // tk
        for tm in rows:
            need = 2 * (tm * tk * a_size + tn * tk * b_size + tm * tn * o_size) + tm * tn * 4 * (2 if nk > 1 else 1)
            need += 2 * tm * tn * 4 if has_acc else 0
            if need <= V7X_MM_VMEM_BUDGET and tm >= min(min_tm, m):
                return tm, tn, tk
    return _tile(m, 128), tn, tk


def _mm(name, a, b, form, out_dtype, acc=None, rider=None):
    if form == "nt":
        (m, k), (n, k2) = a.shape, b.shape
    elif form == "nn":
        (m, k), (k2, n) = a.shape, b.shape
    else:
        (k, m), (k2, n) = a.shape, b.shape
    assert k == k2, (name, a.shape, b.shape)
    has_acc = acc is not None
    tm, tn, tk = _mm_plan(form, m, n, k, a.dtype.itemsize, b.dtype.itemsize, jnp.dtype(out_dtype).itemsize, has_acc)
    nk = k // tk
    a_spec = {"nt": pl.BlockSpec((tm, tk), lambda i, j, kk: (i, kk)),
              "nn": pl.BlockSpec((tm, tk), lambda i, j, kk: (i, kk)),
              "tn": pl.BlockSpec((tk, tm), lambda i, j, kk: (kk, i))}[form]
    b_spec = {"nt": pl.BlockSpec((tn, tk), lambda i, j, kk: (j, kk)),
              "nn": pl.BlockSpec((tk, tn), lambda i, j, kk: (kk, j)),
              "tn": pl.BlockSpec((tk, tn), lambda i, j, kk: (kk, j))}[form]
    o_spec = pl.BlockSpec((tm, tn), lambda i, j, kk: (i, j))
    dims = _DIMS[form]

    def body(*refs):
        a_ref, b_ref = refs[0], refs[1]
        c_ref = refs[2] if has_acc else None
        o_ref = refs[3] if has_acc else refs[2]

        def finish(r):
            if has_acc:
                r = r + c_ref[...]
            o_ref[...] = r.astype(o_ref.dtype)

        if nk == 1:
            finish(_dot(a_ref[...], b_ref[...], dims))
            return
        acc_ref = refs[-1]
        kk = pl.program_id(2)

        @pl.when(kk == 0)
        def _():
            acc_ref[...] = _dot(a_ref[...], b_ref[...], dims)

        @pl.when(kk > 0)
        def _():
            acc_ref[...] += _dot(a_ref[...], b_ref[...], dims)

        @pl.when(kk == nk - 1)
        def _():
            finish(acc_ref[...])

    in_specs = [a_spec, b_spec] + ([o_spec] if has_acc else [])
    args = (a, b) + ((acc,) if has_acc else ())
    outs, r_outs = _pcall(name, body, (m // tm, n // tn, nk), in_specs, [o_spec],
                          [jax.ShapeDtypeStruct((m, n), out_dtype)], args, ("parallel", "parallel", "arbitrary"),
                          scratch=[pltpu.VMEM((tm, tn), F32)] if nk > 1 else [], rider=rider)
    return outs[0] if rider is None else (outs[0], r_outs)


def _mm_fused(name, a, bs, epilogue, tile_ins, out_dtypes, rider=None):
    (m, k), (n, _) = a.shape, bs[0].shape
    tn = _tile(n, 512)
    tm = None
    for cand in sorted({m} | {t for t in range(128, m, 128) if m % t == 0}, reverse=True):
        per_tile = sum(t.dtype.itemsize for t in tile_ins) + sum(jnp.dtype(d).itemsize for d in out_dtypes)
        need = 2 * (cand * k * a.dtype.itemsize + len(bs) * tn * k * bs[0].dtype.itemsize + cand * tn * per_tile)
        need += (len(bs) + 2) * cand * tn * 4
        if need <= V7X_MM_VMEM_BUDGET:
            tm = cand
            break
    assert tm is not None, name
    n_b, n_t = len(bs), len(tile_ins)
    tile = pl.BlockSpec((tm, tn), lambda i, j: (i, j))

    def body(a_ref, *refs):
        prods = [_dot(a_ref[...], r[...], _NT) for r in refs[:n_b]]
        outs = epilogue(prods, [r[...].astype(F32) for r in refs[n_b:n_b + n_t]])
        for r, o in zip(refs[n_b + n_t:], outs):
            r[...] = o.astype(r.dtype)

    outs, r_outs = _pcall(name, body, (m // tm, n // tn),
                          [pl.BlockSpec((tm, k), lambda i, j: (i, 0))] + [pl.BlockSpec((tn, k), lambda i, j: (j, 0))] * n_b
                          + [tile] * n_t, [tile] * len(out_dtypes),
                          [jax.ShapeDtypeStruct((m, n), d) for d in out_dtypes], (a, *bs, *tile_ins),
                          ("parallel", "parallel"), rider=rider)
    return outs if rider is None else (outs, r_outs)


def _rows(name, fn, n_rows, tm, nbc, row_ins, type_ins, row_outs, acc_outs, rider=None):
    n_ri, n_ti, n_ro, n_ao = len(row_ins), len(type_ins), len(row_outs), len(acc_outs)

    def row_map(i, cb, roff):
        return (jnp.maximum(i - roff, 0), cb)

    def type_map(i):
        return (jnp.where(i >= nbc, 1, 0), 0, 0)

    in_specs, args = [], []
    for arr, cb, width, roff in row_ins:
        in_specs.append(pl.BlockSpec((tm, width), functools.partial(row_map, cb=cb, roff=roff)))
        args.append(arr)
    def shared_map(i):
        return (0, 0, 0)

    for arr in type_ins:
        in_specs.append(pl.BlockSpec((None, 1, arr.shape[-1]), type_map if arr.shape[0] == 2 else shared_map))
        args.append(arr)
    out_shape, out_specs = [], []
    for total, width, dtype, roff in row_outs:
        out_shape.append(jax.ShapeDtypeStruct((total, width), dtype))
        out_specs.append(pl.BlockSpec((tm, width), functools.partial(row_map, cb=0, roff=roff)))
    acc_shared = [isinstance(w, tuple) for w in acc_outs]
    for width in acc_outs:
        if isinstance(width, tuple):
            out_shape.append(jax.ShapeDtypeStruct((1, 1, width[0]), F32))
            out_specs.append(pl.BlockSpec((None, 1, width[0]), shared_map))
        else:
            out_shape.append(jax.ShapeDtypeStruct((2, 1, width), F32))
            out_specs.append(pl.BlockSpec((None, 1, width), type_map))
    n_in = n_ri + n_ti

    def body(*refs):
        i = pl.program_id(0)
        outs = fn(*[r[...].astype(F32) for r in refs[:n_in]])
        if not isinstance(outs, (tuple, list)):
            outs = (outs,)
        assert len(outs) == n_ro + n_ao, (name, len(outs))
        for r, o in zip(refs[n_in:n_in + n_ro], outs[:n_ro]):
            r[...] = o.astype(r.dtype)
        if n_ao:
            for r, o, shared in zip(refs[n_in + n_ro:], outs[n_ro:], acc_shared):
                first = i == 0 if shared else jnp.logical_or(i == 0, i == nbc)
                o = jnp.broadcast_to(o.astype(F32), r.shape)

                @pl.when(first)
                def _(r=r, o=o):
                    r[...] = o

                @pl.when(jnp.logical_not(first))
                def _(r=r, o=o):
                    r[...] += o

    outs, r_outs = _pcall(name, body, (n_rows // tm,), in_specs, out_specs, out_shape, args, ("arbitrary",),
                          rider=rider)
    return outs if rider is None else (outs, r_outs)


def _vjp_fn(f, n_row, n_cot, keep=None):
    def g(*args):
        prim = args[:n_row] + args[n_row + n_cot:]
        cots = args[n_row:n_row + n_cot]
        out, vjp = jax.vjp(f, *prim)
        grads = vjp(tuple(cots) if isinstance(out, (tuple, list)) else cots[0])
        return grads if keep is None else tuple(grads[j] for j in keep)
    return g


def _typed(v):
    return v.reshape(1, 1, -1)


def _ln(x, g, b):
    mu = jnp.mean(x, axis=-1, keepdims=True)
    var = jnp.mean(jnp.square(x - mu), axis=-1, keepdims=True)
    return (x - mu) * lax.rsqrt(var + LN_EPS) * g + b


def _f_mod(x, sc, sh):
    return x * (1.0 + sc) + sh


def _make_f_ln(alpha, with_mod):
    def f(x, o, gate, lng, lnb, *mod):
        xn = _ln(alpha * x + gate * o, lng, lnb)
        if with_mod:
            sc, sh = mod
            return xn, xn * (1.0 + sc) + sh
        return xn
    return f


@jax.custom_vjp
def _rot(y):
    lane = lax.broadcasted_iota(jnp.int32, y.shape, 1)
    return jnp.where(lane % 64 < 32, pltpu.roll(y, 96, axis=1), pltpu.roll(y, 32, axis=1))


_rot.defvjp(lambda y: (_rot(y), None), lambda _, g: (_rot(g),))


def _f_prep(p, cos, sin, qg, kg):
    def head(xh, g):
        ms = jnp.mean(jnp.square(xh), axis=-1, keepdims=True)
        y = xh * lax.rsqrt(ms + RMS_EPS) * g
        return y * cos + _rot(y) * sin
    q = jnp.concatenate([head(p[:, h * HEAD_DIM:(h + 1) * HEAD_DIM], qg) for h in range(N_HEADS)], axis=1)
    k = jnp.concatenate([head(p[:, OFF_K + h * HEAD_DIM:OFF_K + (h + 1) * HEAD_DIM], kg)
                         for h in range(N_KV_HEADS)], axis=1)
    return q, k, p[:, OFF_V:OFF_POOL]


def _f_gate(g, t0, t1, t2, t3, b):
    d = t0.shape[-1]
    ts = (t0, t1, t2, t3)
    terms = [jax.nn.sigmoid(g[:, k * d:(k + 1) * d] + b[:, k * d:(k + 1) * d]) * ts[k] for k in range(N_BRANCH)]
    return terms[0] + terms[1] + terms[2] + terms[3]


def _f_gate_bwd(g, t0, t1, t2, t3, dm, b):
    dgs, dts = [], []
    d = t0.shape[-1]
    for k, t in enumerate((t0, t1, t2, t3)):
        s = jax.nn.sigmoid(g[:, k * d:(k + 1) * d] + b[:, k * d:(k + 1) * d])
        dt = dm * s
        dts.append(dt)
        dgs.append(dt * t * (1.0 - s))
    dg = jnp.concatenate(dgs, axis=1)
    return (dg, *dts, jnp.sum(dg, axis=0, keepdims=True))


def _f_swiglu(a, b):
    return jax.nn.silu(a) * b


def _softmax(raw):
    e = jnp.exp2((raw - jnp.max(raw, axis=-1, keepdims=True)) * (ATT_SCALE * math.log2(math.e)))
    return e / jnp.sum(e, axis=-1, keepdims=True)


def _attn_fwd(name, q, k, v, rc, ctx_queries, tq=256, rider=None):
    r = q.shape[0]
    assert rc % tq == 0 and r % tq == 0
    nqc = rc // tq

    def body(q_ref, k_ref, v_ref, o_ref, p_ref):
        qi = pl.program_id(1)

        def attend(nk):
            p = _softmax(_dot(q_ref[...], k_ref[0:nk, :], _NT)).astype(BF16)
            p_ref[:, 0:nk] = p
            o_ref[...] = _dot(p, v_ref[0:nk, :], _NN).astype(o_ref.dtype)

        @pl.when(qi < nqc)
        def _():
            if ctx_queries:
                attend(rc)
            else:
                o_ref[...] = jnp.zeros_like(o_ref)

        @pl.when(qi >= nqc)
        def _():
            attend(r)

    outs, r_outs = _pcall(
        name, body, (N_HEADS, r // tq),
        [pl.BlockSpec((tq, HEAD_DIM), lambda h, i: (i, h)),
         pl.BlockSpec((r, HEAD_DIM), lambda h, i: (0, h // KV_GROUP)),
         pl.BlockSpec((r, HEAD_DIM), lambda h, i: (0, h // KV_GROUP))],
        [pl.BlockSpec((tq, HEAD_DIM), lambda h, i: (i, h)), pl.BlockSpec((None, tq, r), lambda h, i: (h, i, 0))],
        [jax.ShapeDtypeStruct((r, Q_W), BF16), jax.ShapeDtypeStruct((N_HEADS, r, r), BF16)], (q, k, v),
        ("parallel", "parallel"), rider=rider)
    return outs if rider is None else (outs, r_outs)


def _attn_bwd(name, q, k, v, pw, do, rc, ctx_queries, tq=256, rider=None):
    r = q.shape[0]
    nqc = rc // tq

    def body(q_ref, k_ref, v_ref, p_ref, do_ref, dq_ref, dk_ref, dv_ref):
        g, qi = pl.program_id(1), pl.program_id(2)

        @pl.when(jnp.logical_and(g == 0, qi == 0))
        def _():
            dk_ref[...] = jnp.zeros_like(dk_ref)
            dv_ref[...] = jnp.zeros_like(dv_ref)

        def grad(nk):
            qb, kb, vb = q_ref[...], k_ref[0:nk, :], v_ref[0:nk, :]
            dob = do_ref[...].astype(BF16)
            p = p_ref[:, 0:nk].astype(F32)
            dv_ref[0:nk, :] += _dot(p, dob, _TN)
            dp = _dot(dob, vb, _NT)
            ds = p * (dp - jnp.sum(dp * p, axis=-1, keepdims=True))
            dq_ref[...] = _dot(ds, kb, _NN) * ATT_SCALE
            dk_ref[0:nk, :] += _dot(ds, qb, _TN) * ATT_SCALE

        @pl.when(qi < nqc)
        def _():
            if ctx_queries:
                grad(rc)
            else:
                dq_ref[...] = jnp.zeros_like(dq_ref)

        @pl.when(qi >= nqc)
        def _():
            grad(r)

    def qmap(kv, g, i):
        return (i, kv * KV_GROUP + g)

    def kvmap(kv, g, i):
        return (0, kv)

    outs, r_outs = _pcall(
        name, body, (N_KV_HEADS, KV_GROUP, r // tq),
        [pl.BlockSpec((tq, HEAD_DIM), qmap), pl.BlockSpec((r, HEAD_DIM), kvmap), pl.BlockSpec((r, HEAD_DIM), kvmap),
         pl.BlockSpec((None, tq, r), lambda kv, g, i: (kv * KV_GROUP + g, i, 0)), pl.BlockSpec((tq, HEAD_DIM), qmap)],
        [pl.BlockSpec((tq, HEAD_DIM), qmap), pl.BlockSpec((r, HEAD_DIM), kvmap), pl.BlockSpec((r, HEAD_DIM), kvmap)],
        [jax.ShapeDtypeStruct((r, Q_W), F32), jax.ShapeDtypeStruct((r, KV_W), F32),
         jax.ShapeDtypeStruct((r, KV_W), F32)],
        (q, k, v, pw, do), ("arbitrary", "arbitrary", "arbitrary"), rider=rider)
    return outs if rider is None else (outs, r_outs)


def _segments(shape, rc):
    t = lax.broadcasted_iota(jnp.int32, shape, 0)
    lo = jnp.where(t < rc, 0, rc)
    hi = jnp.where(t < rc, rc, shape[0])
    return t, lo, hi


def _shifted(x, o, t, lo, hi):
    n = x.shape[0]
    sh = pltpu.roll(x, (-o) % n, axis=0)
    return jnp.where(jnp.logical_and(t + o >= lo, t + o < hi), sh, 0.0)


def _winsum(x, left, right, t, lo, hi):
    acc = x
    for o in range(-left, right + 1):
        if o != 0:
            acc = acc + _shifted(x, o, t, lo, hi)
    return acc


def _pool_parts(z, g, t, lo, hi):
    w = POOL_WINDOWS[g]
    left = w // 2
    right = w - 1 - left
    count = (jnp.minimum(t + right + 1, hi) - jnp.maximum(t - left, lo)).astype(F32)
    return _winsum(z, left, right, t, lo, hi) / count - z, count, left, right


def _pool_fwd(name, p, pool_w, pool_scale, rc):
    r = p.shape[0]

    def body(z_ref, w_ref, s_ref, y_ref):
        t, lo, hi = _segments((r, GC), rc)
        for g in range(N_GROUPS):
            cols = slice(g * GC, (g + 1) * GC)
            d, _, _, _ = _pool_parts(z_ref[:, cols], g, t, lo, hi)
            y_ref[:, cols] = (_dot(d, w_ref[g], _NN) * s_ref[:, cols]).astype(y_ref.dtype)

    return pl.pallas_call(
        body, name=name, grid=(1,),
        in_specs=[pl.BlockSpec((r, BR_W), lambda i: (0, OFF_POOL // BR_W)),
                  pl.BlockSpec((N_GROUPS, GC, GC), lambda i: (0, 0, 0)),
                  pl.BlockSpec((1, BR_W), lambda i: (0, 0))],
        out_specs=pl.BlockSpec((r, BR_W), lambda i: (0, 0)),
        out_shape=jax.ShapeDtypeStruct((r, BR_W), BF16),
        compiler_params=_cparams("arbitrary"),
    )(p, pool_w, pool_scale.reshape(1, BR_W))


def _pool_bwd(name, p, pool_w, pool_scale, dy, rc):
    r = p.shape[0]

    def body(z_ref, w_ref, s_ref, dy_ref, dz_ref, dw_ref, ds_ref):
        t, lo, hi = _segments((r, GC), rc)
        for g in range(N_GROUPS):
            cols = slice(g * GC, (g + 1) * GC)
            d, count, left, right = _pool_parts(z_ref[:, cols], g, t, lo, hi)
            dyg = dy_ref[:, cols]
            ds_ref[:, cols] = jnp.sum(dyg * _dot(d, w_ref[g], _NN), axis=0, keepdims=True)
            dlin = dyg * s_ref[:, cols]
            dw_ref[g] = _dot(d, dlin, _TN)
            dd = _dot(dlin, w_ref[g], _NT)
            dz_ref[:, cols] = (_winsum(dd / count, right, left, t, lo, hi) - dd).astype(dz_ref.dtype)

    return pl.pallas_call(
        body, name=name, grid=(1,),
        in_specs=[pl.BlockSpec((r, BR_W), lambda i: (0, OFF_POOL // BR_W)),
                  pl.BlockSpec((N_GROUPS, GC, GC), lambda i: (0, 0, 0)),
                  pl.BlockSpec((1, BR_W), lambda i: (0, 0)),
                  pl.BlockSpec((r, BR_W), lambda i: (0, 0))],
        out_specs=[pl.BlockSpec((r, BR_W), lambda i: (0, 0)),
                   pl.BlockSpec((N_GROUPS, GC, GC), lambda i: (0, 0, 0)),
                   pl.BlockSpec((1, BR_W), lambda i: (0, 0))],
        out_shape=[jax.ShapeDtypeStruct((r, BR_W), BF16), jax.ShapeDtypeStruct((N_GROUPS, GC, GC), F32),
                   jax.ShapeDtypeStruct((1, BR_W), F32)],
        compiler_params=_cparams("arbitrary"),
    )(p, pool_w, pool_scale.reshape(1, BR_W), dy)


def _f_sgu_v(pvg, lng, lnb):
    return _ln(jax.nn.gelu(pvg), lng, lnb)


def _sgu_fwd(name, p, ln_g, ln_b, sgu_w, sgu_b):
    r = p.shape[0]

    def body(pu_ref, pv_ref, g_ref, b_ref, w_ref, sb_ref, y_ref):
        vn = _f_sgu_v(pv_ref[...], g_ref[...], b_ref[...])
        u = jax.nn.gelu(pu_ref[...])
        for g in range(N_GROUPS):
            cols = slice(g * GC, (g + 1) * GC)
            s = _dot(w_ref[g], vn[:, cols], _NN) + sb_ref[g]
            y_ref[:, cols] = (u[:, cols] * s).astype(y_ref.dtype)

    return pl.pallas_call(
        body, name=name, grid=(r // SGU_CHUNK,),
        in_specs=[pl.BlockSpec((SGU_CHUNK, BR_W), lambda i: (i, OFF_U // BR_W)),
                  pl.BlockSpec((SGU_CHUNK, BR_W), lambda i: (i, OFF_VG // BR_W)),
                  pl.BlockSpec((1, BR_W), lambda i: (0, 0)), pl.BlockSpec((1, BR_W), lambda i: (0, 0)),
                  pl.BlockSpec((N_GROUPS, GC, GC), lambda i: (0, 0, 0)),
                  pl.BlockSpec((N_GROUPS, SGU_CHUNK, 1), lambda i: (0, 0, 0))],
        out_specs=pl.BlockSpec((SGU_CHUNK, BR_W), lambda i: (i, 0)),
        out_shape=jax.ShapeDtypeStruct((r, BR_W), BF16),
        compiler_params=_cparams("parallel"),
    )(p, p, ln_g.reshape(1, BR_W), ln_b.reshape(1, BR_W), sgu_w, sgu_b.reshape(N_GROUPS, SGU_CHUNK, 1))


def _sgu_bwd(name, p, ln_g, ln_b, sgu_w, sgu_b, dy):
    r = p.shape[0]

    def body(pu_ref, pv_ref, g_ref, b_ref, w_ref, sb_ref, dy_ref, dp_ref, dg_ref, db_ref, dw_ref, dsb_ref):
        i = pl.program_id(0)

        @pl.when(i == 0)
        def _():
            for ref in (dg_ref, db_ref, dw_ref, dsb_ref):
                ref[...] = jnp.zeros_like(ref)

        vn, vjp_v = jax.vjp(_f_sgu_v, pv_ref[...], g_ref[...], b_ref[...])
        u, vjp_u = jax.vjp(jax.nn.gelu, pu_ref[...])
        dy = dy_ref[...]
        du, dvn = [], []
        for g in range(N_GROUPS):
            cols = slice(g * GC, (g + 1) * GC)
            s = _dot(w_ref[g], vn[:, cols], _NN) + sb_ref[g]
            du.append(dy[:, cols] * s)
            ds = dy[:, cols] * u[:, cols]
            dsb_ref[g] += jnp.sum(ds, axis=1, keepdims=True)
            dw_ref[g] += _dot(ds, vn[:, cols], _NT)
            dvn.append(_dot(w_ref[g], ds, _TN))
        (dpu,) = vjp_u(jnp.concatenate(du, axis=1))
        dpv, dg, db = vjp_v(jnp.concatenate(dvn, axis=1))
        dp_ref[:, 0:BR_W] = dpu.astype(dp_ref.dtype)
        dp_ref[:, BR_W:2 * BR_W] = dpv.astype(dp_ref.dtype)
        dg_ref[...] += dg
        db_ref[...] += db

    vec = pl.BlockSpec((1, BR_W), lambda i: (0, 0))
    wsp = pl.BlockSpec((N_GROUPS, GC, GC), lambda i: (0, 0, 0))
    bsp = pl.BlockSpec((N_GROUPS, SGU_CHUNK, 1), lambda i: (0, 0, 0))
    return pl.pallas_call(
        body, name=name, grid=(r // SGU_CHUNK,),
        in_specs=[pl.BlockSpec((SGU_CHUNK, BR_W), lambda i: (i, OFF_U // BR_W)),
                  pl.BlockSpec((SGU_CHUNK, BR_W), lambda i: (i, OFF_VG // BR_W)),
                  vec, vec, wsp, bsp, pl.BlockSpec((SGU_CHUNK, BR_W), lambda i: (i, 0))],
        out_specs=[pl.BlockSpec((SGU_CHUNK, 2 * BR_W), lambda i: (i, 0)), vec, vec, wsp, bsp],
        out_shape=[jax.ShapeDtypeStruct((r, 2 * BR_W), BF16), jax.ShapeDtypeStruct((1, BR_W), F32),
                   jax.ShapeDtypeStruct((1, BR_W), F32), jax.ShapeDtypeStruct((N_GROUPS, GC, GC), F32),
                   jax.ShapeDtypeStruct((N_GROUPS, SGU_CHUNK, 1), F32)],
        compiler_params=_cparams("arbitrary"),
    )(p, p, ln_g.reshape(1, BR_W), ln_b.reshape(1, BR_W), sgu_w, sgu_b.reshape(N_GROUPS, SGU_CHUNK, 1), dy)


def _conv_w8(conv_w):
    return jnp.concatenate([conv_w, jnp.zeros((8 - conv_w.shape[0], conv_w.shape[1]), F32)], axis=0)


def _conv_fwd(name, p, conv_w, rc):
    r = p.shape[0]

    def body(cb_ref, cc_ref, cx_ref, w_ref, y_ref):
        t, lo, hi = _segments((r, GC), rc)
        z = cc_ref[...] * cx_ref[...]
        w = w_ref[...]
        c = _shifted(z, -1, t, lo, hi) * w[0:1] + z * w[1:2] + _shifted(z, 1, t, lo, hi) * w[2:3]
        y_ref[...] = (cb_ref[...] * c).astype(y_ref.dtype)

    nb = OFF_CB // GC
    return pl.pallas_call(
        body, name=name, grid=(N_GROUPS,),
        in_specs=[pl.BlockSpec((r, GC), lambda j: (0, nb + j)),
                  pl.BlockSpec((r, GC), lambda j: (0, nb + N_GROUPS + j)),
                  pl.BlockSpec((r, GC), lambda j: (0, nb + 2 * N_GROUPS + j)),
                  pl.BlockSpec((8, GC), lambda j: (0, j))],
        out_specs=pl.BlockSpec((r, GC), lambda j: (0, j)),
        out_shape=jax.ShapeDtypeStruct((r, BR_W), BF16),
        compiler_params=_cparams("parallel"),
    )(p, p, p, _conv_w8(conv_w))


def _conv_bwd(name, p, conv_w, dy, rc):
    r = p.shape[0]

    def body(cb_ref, cc_ref, cx_ref, w_ref, dy_ref, dcb_ref, dcc_ref, dcx_ref, dw_ref):
        t, lo, hi = _segments((r, GC), rc)
        cc, cx, w, dy = cc_ref[...], cx_ref[...], w_ref[...], dy_ref[...]
        z = cc * cx
        zp, zn = _shifted(z, -1, t, lo, hi), _shifted(z, 1, t, lo, hi)
        dcb_ref[...] = (dy * (zp * w[0:1] + z * w[1:2] + zn * w[2:3])).astype(dcb_ref.dtype)
        dc = dy * cb_ref[...]
        dw_ref[...] = jnp.concatenate(
            [jnp.sum(dc * zp, axis=0, keepdims=True), jnp.sum(dc * z, axis=0, keepdims=True),
             jnp.sum(dc * zn, axis=0, keepdims=True), jnp.zeros((5, GC), F32)], axis=0)
        dz = dc * w[1:2] + _shifted(dc, 1, t, lo, hi) * w[0:1] + _shifted(dc, -1, t, lo, hi) * w[2:3]
        dcc_ref[...] = (dz * cx).astype(dcc_ref.dtype)
        dcx_ref[...] = (dz * cc).astype(dcx_ref.dtype)

    nb = OFF_CB // GC
    return pl.pallas_call(
        body, name=name, grid=(N_GROUPS,),
        in_specs=[pl.BlockSpec((r, GC), lambda j: (0, nb + j)),
                  pl.BlockSpec((r, GC), lambda j: (0, nb + N_GROUPS + j)),
                  pl.BlockSpec((r, GC), lambda j: (0, nb + 2 * N_GROUPS + j)),
                  pl.BlockSpec((8, GC), lambda j: (0, j)),
                  pl.BlockSpec((r, GC), lambda j: (0, j))],
        out_specs=[pl.BlockSpec((r, GC), lambda j: (0, j))] * 3 + [pl.BlockSpec((8, GC), lambda j: (0, j))],
        out_shape=[jax.ShapeDtypeStruct((r, BR_W), BF16)] * 3 + [jax.ShapeDtypeStruct((8, BR_W), F32)],
        compiler_params=_cparams("parallel"),
    )(p, p, p, _conv_w8(conv_w), dy)


def _rope_tables(rc, n):
    rows = n // GRID_W
    row = jnp.repeat(jnp.arange(rows), GRID_W).astype(F32)
    col = jnp.tile(jnp.arange(GRID_W), rows).astype(F32)
    inv = ROPE_THETA ** (-jnp.arange(0, ROPE_AXIS_DIM, 2, dtype=F32) / ROPE_AXIS_DIM)
    ang_r, ang_c = row[:, None] * inv, col[:, None] * inv
    cos = jnp.concatenate([jnp.cos(ang_r), jnp.cos(ang_r), jnp.cos(ang_c), jnp.cos(ang_c)], axis=1)
    sin = jnp.concatenate([-jnp.sin(ang_r), jnp.sin(ang_r), -jnp.sin(ang_c), jnp.sin(ang_c)], axis=1)
    cos = jnp.concatenate([jnp.ones((rc, HEAD_DIM), F32), cos], axis=0)
    sin = jnp.concatenate([jnp.zeros((rc, HEAD_DIM), F32), sin], axis=0)
    return cos, sin


MOD_NAMES = ("sh1", "sc1", "g1", "sh2", "sc2", "g2")


def _local_step(xin, target, mod, comm, sp, rc, alpha):
    def carrying(fn):
        def call(name, *args, **kw):
            rider = comm.rider(name)
            if rider is None:
                return fn(name, *args, **kw)
            res, r_outs = fn(name, *args, rider=rider, **kw)
            comm.deliver(name, r_outs)
            return res
        return call

    mm, rows, attn_fwd, attn_bwd = carrying(_mm), carrying(_rows), carrying(_attn_fwd), carrying(_attn_bwd)
    mm_fused = carrying(_mm_fused)
    r, d = xin.shape
    n_layers = mod.shape[0]
    tm_n, tm_w = 256, 128
    nbc_n, nbc_w = rc // tm_n, rc // tm_w
    cos, sin = _rope_tables(rc, r - rc)
    mp = mod.reshape(n_layers, 2, 6, 1, d)
    mods = [{nm: mp[i, :, j] for j, nm in enumerate(MOD_NAMES)} for i in range(n_layers)]
    f_ln_mod, f_ln_last = _make_f_ln(alpha, True), _make_f_ln(alpha, False)

    def whole(arr, roff=0):
        return (arr, 0, arr.shape[1], roff)

    (hb,) = rows("mod_in", _f_mod, r, tm_n, nbc_n, [whole(xin)], [mods[0]["sc1"], mods[0]["sh1"]],
                 [(r, d, BF16, 0)], [])
    saved = []
    x = xin
    for i in range(n_layers):
        last = i == n_layers - 1
        w, s, m = functools.partial(comm.weight, i), sp[i], mods[i]
        sv = {"x": x, "hb": hb}
        p = mm(f"l{i}_in", hb, w("in_t"), "nt", F32)
        q, k, v = rows(f"l{i}_prep", _f_prep, r, tm_n, nbc_n,
                       [(p, 0, OFF_POOL, 0), whole(cos), whole(sin)], [_typed(s["q_norm_g"]), _typed(s["k_norm_g"])],
                       [(r, Q_W, BF16, 0), (r, KV_W, BF16, 0), (r, KV_W, BF16, 0)], [])
        att, sv["pw"] = attn_fwd(f"l{i}_attn", q, k, v, rc, not last)
        ys = [att,
              _pool_fwd(f"l{i}_pool", p, s["pool_w"], s["pool_scale"], rc),
              _sgu_fwd(f"l{i}_sgu", p, s["sgu_ln_g"], s["sgu_ln_b"], s["sgu_w"], s["sgu_b"]),
              _conv_fwd(f"l{i}_conv", p, s["conv_w"], rc)]
        ts = [mm(f"l{i}_br{kk}", ys[kk], w(f"br{kk}"), "nt", BF16) for kk in range(N_BRANCH)]
        gpre = mm(f"l{i}_gate", hb, w("gate_t"), "nt", BF16)
        (mg,) = rows(f"l{i}_merge", _f_gate, r, tm_w, nbc_w, [whole(gpre)] + [whole(t) for t in ts],
                     [_typed(s["b_gate"])], [(r, d, BF16, 0)], [])
        o = mm(f"l{i}_o", mg, w("o"), "nn", F32)
        x1, h2b = rows(f"l{i}_ln1", f_ln_mod, r, tm_n, nbc_n, [whole(x), whole(o)],
                       [m["g1"], _typed(s["ln1_g"]), _typed(s["ln1_b"]), m["sc2"], m["sh2"]],
                       [(r, d, F32, 0), (r, d, BF16, 0)], [])
        af, bf, f = mm_fused(f"l{i}_ffgu", h2b, [w("ffg_t"), w("ffu_t")],
                             lambda prods, _: (prods[0], prods[1], _f_swiglu(prods[0], prods[1])), [], [BF16, BF16, BF16])
        o2 = mm(f"l{i}_ffd", f, w("ffd"), "nn", F32)
        if last:
            (x2,) = rows(f"l{i}_ln2", f_ln_last, r, tm_n, nbc_n, [whole(x1), whole(o2)],
                         [m["g2"], _typed(s["ln2_g"]), _typed(s["ln2_b"])], [(r, d, F32, 0)], [])
            hb = None
        else:
            nx = mods[i + 1]
            x2, hb = rows(f"l{i}_ln2", f_ln_mod, r, tm_n, nbc_n, [whole(x1), whole(o2)],
                          [m["g2"], _typed(s["ln2_g"]), _typed(s["ln2_b"]), nx["sc1"], nx["sh1"]],
                          [(r, d, F32, 0), (r, d, BF16, 0)], [])
        sv.update(p=p, gpre=gpre, q=q, k=k, v=v, ys=ys, ts=ts, mg=mg, o=o, x1=x1, h2b=h2b, af=af, bf=bf, f=f, o2=o2)
        saved.append(sv)
        x = x2

    lat = jnp.concatenate([jnp.zeros((1, 1, 128), F32), jnp.ones((1, 1, 128), F32)], axis=0)

    def f_loss(xb, tb, msk):
        diff = (xb - tb) * msk[:, 0:1]
        part = jnp.sum(jnp.mean(jnp.square(diff), axis=-1, keepdims=True), axis=0, keepdims=True)
        return diff * (1.0 / d), jnp.broadcast_to(part, (1, 128))

    dx_direct, loss_acc = rows("loss", f_loss, r, tm_n, nbc_n, [whole(x), whole(target, nbc_n)], [lat],
                               [(r, d, F32, 0)], [128])
    loss = 0.5 * loss_acc[1, 0, 0]

    dmods = [dict() for _ in range(n_layers)]
    dsp = [dict() for _ in range(n_layers)]
    dh = None

    def small_done(j):
        ds = dict(dsp[j])
        for nm in ("ln1_g", "ln1_b", "ln2_g", "ln2_b", "b_gate", "q_norm_g", "k_norm_g"):
            ds[nm] = ds[nm][0, 0]
        for nm in ("pool_scale", "sgu_ln_g", "sgu_ln_b"):
            ds[nm] = ds[nm].reshape(-1)
        ds["sgu_b"] = ds["sgu_b"].reshape(N_GROUPS, SGU_CHUNK)
        comm.small_ready(j, jnp.concatenate([dmods[j][nm][:, 0, :] for nm in MOD_NAMES], axis=-1), ds)

    for i in reversed(range(n_layers)):
        last = i == n_layers - 1
        w, s, m, sv = functools.partial(comm.weight, i), sp[i], mods[i], saved[i]
        dm, dw, ds = dmods[i], {}, dsp[i]
        ln2 = [m["g2"], _typed(s["ln2_g"]), _typed(s["ln2_b"])]
        if last:
            res = rows(f"l{i}_ln2_bwd", _vjp_fn(f_ln_last, 2, 1), r, tm_n, nbc_n,
                       [whole(sv["x1"]), whole(sv["o2"]), whole(dx_direct)], ln2,
                       [(r, d, F32, 0), (r, d, BF16, 0)], [d, (d,), (d,)])
            dx1, do2, dm["g2"], dlg, dlb = res
        else:
            nx = mods[i + 1]
            res = rows(f"l{i}_ln2_bwd", _vjp_fn(f_ln_mod, 2, 2), r, tm_n, nbc_n,
                       [whole(sv["x1"]), whole(sv["o2"]), whole(dx_direct), whole(dh)],
                       ln2 + [nx["sc1"], nx["sh1"]],
                       [(r, d, F32, 0), (r, d, BF16, 0)], [d, (d,), (d,), d, d])
            dx1, do2, dm["g2"], dlg, dlb, dmods[i + 1]["sc1"], dmods[i + 1]["sh1"] = res
            small_done(i + 1)
        ds["ln2_g"], ds["ln2_b"] = dlg, dlb
        dab, dbb = mm_fused(f"l{i}_dF", do2, [w("ffd")],
                            lambda prods, tiles: jax.vjp(_f_swiglu, *tiles)[1](prods[0]), [sv["af"], sv["bf"]],
                            [BF16, BF16])
        comm.grads(i, {"ffd": mm(f"l{i}_dWffd", sv["f"], do2, "tn", BF16)})
        comm.grads(i, {"ffg_t": mm(f"l{i}_dWffg", dab, sv["h2b"], "tn", BF16)})
        comm.grads(i, {"ffu_t": mm(f"l{i}_dWffu", dbb, sv["h2b"], "tn", BF16)})
        dh2 = mm(f"l{i}_dh2a", dab, w("ffg_t"), "nn", F32)
        dh2 = mm(f"l{i}_dh2b", dbb, w("ffu_t"), "nn", F32, acc=dh2)
        res = rows(f"l{i}_ln1_bwd", _vjp_fn(f_ln_mod, 2, 2), r, tm_n, nbc_n,
                   [whole(sv["x"]), whole(sv["o"]), whole(dx1), whole(dh2)],
                   [m["g1"], _typed(s["ln1_g"]), _typed(s["ln1_b"]), m["sc2"], m["sh2"]],
                   [(r, d, F32, 0), (r, d, BF16, 0)], [d, (d,), (d,), d, d])
        dx_direct, do, dm["g1"], ds["ln1_g"], ds["ln1_b"], dm["sc2"], dm["sh2"] = res
        dmg = mm(f"l{i}_dMg", do, w("o"), "nt", F32)
        comm.grads(i, {"o": mm(f"l{i}_dWo", sv["mg"], do, "tn", BF16)})
        res = rows(f"l{i}_merge_bwd", _f_gate_bwd, r, tm_w, nbc_w,
                   [whole(sv["gpre"])] + [whole(t) for t in sv["ts"]] + [whole(dmg)], [_typed(s["b_gate"])],
                   [(r, N_BRANCH * d, BF16, 0)] + [(r, d, BF16, 0)] * N_BRANCH, [(N_BRANCH * d,)])
        dgb, dts, ds["b_gate"] = res[0], res[1:1 + N_BRANCH], res[1 + N_BRANCH]
        comm.grads(i, {"gate_t": mm(f"l{i}_dWgate", dgb, sv["hb"], "tn", BF16)})
        dys = [mm(f"l{i}_dY{kk}", dts[kk], w(f"br{kk}"), "nn", F32) for kk in range(N_BRANCH)]
        for kk in range(N_BRANCH):
            comm.grads(i, {f"br{kk}": mm(f"l{i}_dWbr{kk}", dts[kk], sv["ys"][kk], "tn", BF16)})
        dq, dk, dv = attn_bwd(f"l{i}_attn_bwd", sv["q"], sv["k"], sv["v"], sv["pw"], dys[0], rc, not last)
        res = rows(f"l{i}_prep_bwd", _vjp_fn(_f_prep, 3, 3, keep=(0, 3, 4)), r, tm_n, nbc_n,
                   [(sv["p"], 0, OFF_POOL, 0), whole(cos), whole(sin), whole(dq), whole(dk), whole(dv)],
                   [_typed(s["q_norm_g"]), _typed(s["k_norm_g"])],
                   [(r, OFF_POOL, BF16, 0)], [(HEAD_DIM,), (HEAD_DIM,)])
        dp_qkv, ds["q_norm_g"], ds["k_norm_g"] = res
        dp_pool, ds["pool_w"], ds["pool_scale"] = _pool_bwd(f"l{i}_pool_bwd", sv["p"], s["pool_w"], s["pool_scale"],
                                                            dys[1], rc)
        dp_sgu, ds["sgu_ln_g"], ds["sgu_ln_b"], ds["sgu_w"], ds["sgu_b"] = _sgu_bwd(
            f"l{i}_sgu_bwd", sv["p"], s["sgu_ln_g"], s["sgu_ln_b"], s["sgu_w"], s["sgu_b"], dys[2])
        dp_cb, dp_cc, dp_cx, dcw = _conv_bwd(f"l{i}_conv_bwd", sv["p"], s["conv_w"], dys[3], rc)
        ds["conv_w"] = dcw[0:3]
        dpb = jnp.concatenate([dp_qkv, dp_pool, dp_sgu, dp_cb, dp_cc, dp_cx], axis=1)
        comm.grads(i, {"in_t": mm(f"l{i}_dWin", dpb, sv["hb"], "tn", BF16)})
        dh = mm(f"l{i}_dhb_a", dpb, w("in_t"), "nn", F32)
        dh = mm(f"l{i}_dhb_b", dgb, w("gate_t"), "nn", F32, acc=dh)

    def f_mod_bwd(xb, ddir, dhb, sc, sh):
        _, vjp = jax.vjp(_f_mod, xb, sc, sh)
        dxb, dsc, dsh = vjp(dhb)
        return dxb + ddir, dsc, dsh

    grad_x, dmods[0]["sc1"], dmods[0]["sh1"] = rows(
        "mod_in_bwd", f_mod_bwd, r, tm_n, nbc_n, [whole(xin), whole(dx_direct), whole(dh)],
        [mods[0]["sc1"], mods[0]["sh1"]], [(r - rc, d, F32, nbc_n)], [d, d])
    small_done(0)
    return loss, grad_x


def _direct_rider(src):
    def peers():
        mx, my, mc = [lax.axis_index(a) for a in MESH_AXES]
        out = []
        for kk in range(1, N_DEV):
            px = 1 - mx if kk & 4 else mx
            py = 1 - my if kk & 2 else my
            pc = 1 - mc if kk & 1 else mc
            out.append(((px, py, pc), 4 * px + 2 * py + pc))
        return 4 * mx + 2 * my + mc, out

    def start(ins, outs, sems):
        send, recv, loc = sems
        me, others = peers()
        pltpu.make_async_copy(ins[0], outs[0].at[me], loc.at[0]).start()
        for j, (peer, _) in enumerate(others):
            _rcopy(ins[0], outs[0].at[me], send, recv, j, peer).start()

    def finish(ins, outs, sems):
        send, recv, loc = sems
        me, others = peers()
        for j, (peer, peer_l) in enumerate(others):
            cp = _rcopy(ins[0], outs[0].at[peer_l], send, recv, j, peer)
            cp.wait_recv()
            cp.wait_send()
        pltpu.make_async_copy(ins[0], outs[0].at[me], loc.at[0]).wait()

    return _Rider([src], [jax.ShapeDtypeStruct((N_DEV,) + src.shape, src.dtype)],
                  _sem_scratch(N_DEV - 1, N_DEV - 1, 1), start, finish)


def _mesh_place():
    mx, my, mc = [lax.axis_index(a) for a in MESH_AXES]
    chips = [(1 - mx, my), (mx, 1 - my), (1 - mx, 1 - my)]

    def lid(px, py, pc):
        return 4 * px + 2 * py + pc

    return (mx, my, mc), (mx, my, 1 - mc), chips, lid


def _rcopy(src, dst, send_sems, recv_sems, k, to):
    return pltpu.make_async_remote_copy(src_ref=src, dst_ref=dst, send_sem=send_sems.at[k], recv_sem=recv_sems.at[k],
                                        device_id=to, device_id_type=pl.DeviceIdType.MESH)


def _sem_scratch(*sizes):
    return [pltpu.SemaphoreType.DMA((s,)) for s in sizes]


def _gather_rider(src, rows1, rows2, buf=None):
    def place():
        (mx, my, mc), sib, chips, lid = _mesh_place()
        xn, yn, dg = [(*chip, mc) for chip in chips]
        return lid(mx, my, mc), sib, xn, yn, dg, lid

    def halves(rows):
        r0, r1 = rows
        mid = r0 + (r1 - r0) // 32 * 16
        return pl.ds(r0, mid - r0), pl.ds(mid, r1 - mid), pl.ds(r0, r1 - r0)

    n_src = 1 if rows1 is not None else 0

    def start(ins, outs, sems):
        send, recv, loc = sems
        me, sib, xn, yn, dg, lid = place()
        if rows1 is not None:
            win = pl.ds(rows1[0], rows1[1] - rows1[0])
            mine, dst = ins[0].at[win], outs[0].at[me, win]
            pltpu.make_async_copy(mine, dst, loc.at[0]).start()
            for t, to in enumerate((sib, xn, yn)):
                _rcopy(mine, dst, send, recv, t, to).start()
        if rows2 is not None:
            top, bot, win = halves(rows2)
            xb, yb = outs[0].at[lid(*xn)], outs[0].at[lid(*yn)]
            _rcopy(xb.at[top], xb.at[top], send, recv, 3, yn).start()
            _rcopy(yb.at[bot], yb.at[bot], send, recv, 4, xn).start()
            _rcopy(xb.at[win], xb.at[win], send, recv, 5, sib).start()
            _rcopy(yb.at[win], yb.at[win], send, recv, 6, sib).start()

    def finish(ins, outs, sems):
        send, recv, loc = sems
        me, sib, xn, yn, dg, lid = place()
        if rows2 is not None:
            top, bot, win = halves(rows2)
            db = outs[0].at[lid(*dg)]
            _rcopy(db.at[top], db.at[top], send, recv, 3, yn).wait_recv()
            _rcopy(db.at[bot], db.at[bot], send, recv, 4, xn).wait_recv()
            _rcopy(db.at[win], db.at[win], send, recv, 7, sib).start()
            for t, dev in ((5, xn), (6, yn), (7, dg)):
                blk = outs[0].at[lid(dev[0], dev[1], 1 - dev[2]), win]
                _rcopy(blk, blk, send, recv, t, sib).wait_recv()
            for t, part in ((3, top), (4, bot), (5, win), (6, win), (7, win)):
                _rcopy(db.at[part], db.at[part], send, recv, t, sib).wait_send()
        if rows1 is not None:
            win = pl.ds(rows1[0], rows1[1] - rows1[0])
            mine, dst = ins[0].at[win], outs[0].at[me, win]
            for t, dev in enumerate((sib, xn, yn)):
                cp = _rcopy(mine, outs[0].at[lid(*dev), win], send, recv, t, dev)
                cp.wait_recv()
                cp.wait_send()
            pltpu.make_async_copy(mine, dst, loc.at[0]).wait()

    out_shape = jax.ShapeDtypeStruct((N_DEV,) + src.shape, src.dtype)
    inputs = ([src] if n_src else []) + ([buf] if buf is not None else [])
    return _Rider(inputs, [out_shape], _sem_scratch(8, 8, 1), start, finish,
                  aliases={n_src: 0} if buf is not None else {})


def _sibling_rider(part):
    def start(ins, outs, sems):
        send, recv = sems
        (mx, my, mc), sib, chips, lid = _mesh_place()
        for t, slab in enumerate([lid(*sib)] + [lid(*chip, 1 - mc) for chip in chips]):
            _rcopy(ins[0].at[slab], outs[0].at[t], send, recv, t, sib).start()

    def finish(ins, outs, sems):
        send, recv = sems
        _, sib, _, _ = _mesh_place()
        for t in range(4):
            cp = _rcopy(ins[0].at[0], outs[0].at[t], send, recv, t, sib)
            cp.wait_recv()
            cp.wait_send()

    return _Rider([part], [jax.ShapeDtypeStruct((4,) + part.shape[1:], part.dtype)], _sem_scratch(4, 4), start, finish)


def _chips_rider(pair, rows, buf=None):
    r0, r1 = rows
    win = pl.ds(r0, r1 - r0)

    def start(ins, outs, sems):
        send, recv = sems
        (mx, my, mc), sib, chips, lid = _mesh_place()
        for j, chip in enumerate(chips):
            _rcopy(ins[0].at[j, win], outs[0].at[j, win], send, recv, j, (*chip, mc)).start()

    def finish(ins, outs, sems):
        send, recv = sems
        (mx, my, mc), sib, chips, lid = _mesh_place()
        for j, chip in enumerate(chips):
            cp = _rcopy(ins[0].at[j, win], outs[0].at[j, win], send, recv, j, (*chip, mc))
            cp.wait_recv()
            cp.wait_send()

    out_shape = jax.ShapeDtypeStruct(pair.shape, pair.dtype)
    if buf is None:
        return _Rider([pair], [out_shape], _sem_scratch(3, 3), start, finish)
    return _Rider([pair, buf], [out_shape], _sem_scratch(3, 3), start, finish, aliases={1: 0})


def _run_rider(name, rider):
    n_in, n_out = len(rider.inputs), len(rider.out_shapes)

    def body(*refs):
        ins, outs, sems = refs[:n_in], refs[n_in:n_in + n_out], refs[n_in + n_out:]
        rider.start(ins, outs, sems)
        rider.finish(ins, outs, sems)

    any_spec = pl.BlockSpec(memory_space=pl.ANY)
    res = pl.pallas_call(body, name=name, in_specs=[any_spec] * n_in, out_specs=[any_spec] * n_out,
                         out_shape=rider.out_shapes, scratch_shapes=rider.scratch,
                         input_output_aliases=rider.aliases)(*rider.inputs)
    return list(res)


def _slab_ids():
    (mx, my, mc), _, chips, lid = _mesh_place()
    return jnp.stack([lid(*chip, mc) for chip in chips] + [lid(mx, my, mc)]).astype(jnp.int32)


def _pair_sum(name, part, rsib, ids):
    _, n, k = part.shape
    tr = _row_tile(n, 512, 16)

    def body(ids_ref, p_ref, r_ref, o_ref):
        o_ref[...] = (p_ref[...].astype(F32) + r_ref[...].astype(F32)).astype(o_ref.dtype)

    grid_spec = pltpu.PrefetchScalarGridSpec(
        num_scalar_prefetch=1, grid=(3, n // tr),
        in_specs=[pl.BlockSpec((None, tr, k), lambda j, i, ids: (ids[j], i, 0)),
                  pl.BlockSpec((None, tr, k), lambda j, i, ids: (1 + j, i, 0))],
        out_specs=pl.BlockSpec((None, tr, k), lambda j, i, ids: (j, i, 0)))
    return pl.pallas_call(body, name=name, grid_spec=grid_spec, out_shape=jax.ShapeDtypeStruct((3, n, k), part.dtype),
                          compiler_params=_cparams("parallel", "parallel"))(ids, part, rsib)


def _sum5(name, part, rsib, rici, ids, layer, stacked, rider=None):
    _, n, k = part.shape
    tr = _row_tile(n, 512, 16)
    first = isinstance(stacked, int)

    def body(ids_ref, p_ref, r_ref, c_ref, *rest):
        acc = p_ref[...].astype(F32) + r_ref[...].astype(F32)
        for j in range(3):
            acc = acc + c_ref[j].astype(F32)
        rest[-1][...] = acc

    in_specs = [pl.BlockSpec((None, tr, k), lambda i, ids: (ids[3], i, 0)),
                pl.BlockSpec((None, tr, k), lambda i, ids: (0, i, 0)),
                pl.BlockSpec((3, tr, k), lambda i, ids: (0, i, 0))] + ([] if first else [pl.BlockSpec(memory_space=pl.ANY)])
    n_layers = stacked if first else stacked.shape[0]
    outs, r_outs = _pcall(name, body, (n // tr,), in_specs, [pl.BlockSpec((None, tr, k), lambda i, ids: (layer, i, 0))],
                          [jax.ShapeDtypeStruct((n_layers, n, k), F32)],
                          (part, rsib, rici) + (() if first else (stacked,)), ("parallel",), rider=rider, prefetch=ids,
                          aliases={} if first else {3: 0})
    return outs[0] if rider is None else (outs[0], r_outs)


W_KEYS = ("in_t", "br0", "br1", "br2", "br3", "gate_t", "o", "ffg_t", "ffu_t", "ffd")
SUMS_TRANSPOSED_LATER = ("gate_t", "br0", "br1", "br2", "br3")


CARRIER_US = {"mod_in": 12, "in": 55, "gate": 95, "prep": 19, "attn": 125,"br0": 15, "merge": 50, "o": 25, "ln1": 27,
              "ffgu": 135, "ffd": 73, "ln2": 27, "loss": 20, "ln2_bwd": 44, "dF": 80,
              "dWffd": 64, "dh2a": 75, "dh2b": 75, "dWffg": 64, "dWffu": 64, "ln1_bwd": 44,
              "dMg": 25, "dWo": 25, "merge_bwd": 80, "dY0": 14, "dWbr0": 15, "attn_bwd": 140,"prep_bwd": 28,
              "dWin": 54, "dWgate": 95, "dhb_a": 64, "dhb_b": 115}
ICI_US_PER_MIB = 45.0
GATHER_US_PER_MIB = 30.0
RELAY_US_PER_MIB = 15.0
D2D_US_PER_MIB = 6.8
MIN_CHUNK_US = 10.0
CARRIER_FILL_FORWARD = 1.15
CARRIER_FILL_BACKWARD = 0.95


class _Comm:
    def __init__(self, wsrc):
        self.wsrc = wsrc
        self.n_layers = len(wsrc)
        self.queue = []
        self.riding = {}
        self.n_alone = 0
        self.buf, self.left = {}, {}
        self.part, self.rsib, self.pair = {}, {}, {}
        self.ids = _slab_ids()
        for i in range(self.n_layers):
            for k in W_KEYS:
                self._push_chunks("gather", ("w", i, k), wsrc[i][k].shape, wsrc[i][k].dtype)

    def _push_chunks(self, kind, item, shape, dtype):
        n, k = shape[-2], shape[-1]
        mib = n * k * jnp.dtype(dtype).itemsize / 2 ** 20
        pieces = max(1, int(mib * ICI_US_PER_MIB // MIN_CHUNK_US))
        while n % (16 * pieces):
            pieces -= 1
        step = n // pieces
        self.left[item] = n
        us = mib * (GATHER_US_PER_MIB if kind == "gather" else ICI_US_PER_MIB) / pieces
        for c in range(pieces):
            self.queue.append(dict(kind=kind, item=item, rows=(c * step, (c + 1) * step), rows2=None, us=us))

    @staticmethod
    def _merge(units, u):
        def joined(a, b):
            if a is None or b is None:
                return True, a or b
            return a[1] == b[0], (a[0], b[1])

        for v in units:
            if v["item"] == u["item"] and v["kind"] == u["kind"] and u["kind"] != "sibling":
                ok1, rows = joined(v["rows"], u["rows"])
                ok2, rows2 = joined(v["rows2"], u["rows2"])
                if ok1 and ok2:
                    v.update(rows=rows, rows2=rows2, us=v["us"] + u["us"])
                    return True
        return False

    def _unit_rider(self, u):
        item = u["item"]
        if u["kind"] == "gather":
            src = self.wsrc[item[1]][item[2]] if item[0] == "w" else self.part[item]
            return _gather_rider(src, u["rows"], u["rows2"], self.buf.get(item))
        if u["kind"] == "sibling":
            return _sibling_rider(self.part[item])
        return _chips_rider(self.pair[item], u["rows"], self.buf.get(item))

    def _done(self, u, out):
        item = u["item"]
        if u["kind"] == "sibling":
            self.rsib[item] = out
            self.pair[item] = _pair_sum(f"pair_l{item[1]}_{item[2]}", self.part[item], out, self.ids)
            self._push_chunks("chips", item, self.pair[item].shape, self.pair[item].dtype)
            return
        self.buf[item] = out
        if u["kind"] == "gather":
            if u["rows"] is not None:
                rows = u["rows"]
                mib = (rows[1] - rows[0]) * out.shape[-1] * out.dtype.itemsize / 2 ** 20
                self.queue.insert(0, dict(kind="gather", item=item, rows=None, rows2=rows, us=mib * RELAY_US_PER_MIB))
            if u["rows2"] is not None:
                self.left[item] -= u["rows2"][1] - u["rows2"][0]

    def _send(self, name, units, call):
        outs = call(_compose([self._unit_rider(u) for u in units]))
        for u, o in zip(units, outs):
            self._done(u, o)

    def exchange(self, name, src, budget_us):
        units = self._take(budget_us)
        outs = _run_rider(name, _compose([_direct_rider(src)] + [self._unit_rider(u) for u in units]))
        for u, o in zip(units, outs[1:]):
            self._done(u, o)
        return outs[0]

    def rider(self, name, budget_us=None):
        budget = CARRIER_US.get(name.split("_", 1)[1] if name[0] == "l" and name[1].isdigit() else name, 0) \
            if budget_us is None else budget_us
        units = self._take(budget)
        if not units:
            return None
        self.riding[name] = units
        return _compose([self._unit_rider(u) for u in units])

    def _take(self, budget):
        units, used = [], 0.0
        forward = bool(self.queue) and self.queue[0]["item"][0] == "w"
        fill = CARRIER_FILL_FORWARD if forward else CARRIER_FILL_BACKWARD
        while self.queue and used + self.queue[0]["us"] <= fill * budget:
            u = self.queue[0]
            if not self._merge(units, u):
                if any(v["item"] == u["item"] for v in units):
                    break
                units.append(dict(u))
            used += u["us"]
            del self.queue[0]
        return units

    def deliver(self, name, outs):
        for u, o in zip(self.riding.pop(name), outs):
            self._done(u, o)

    def _flush(self, item, kinds):
        hits = [p for p, u in enumerate(self.queue) if u["item"] == item and u["kind"] in kinds]
        if not hits:
            return
        prefix = self.queue[:hits[-1] + 1]
        del self.queue[:hits[-1] + 1]
        units = []
        for u in prefix:
            if not self._merge(units, u):
                units.append(dict(u))
        tag = "_".join(str(t) for t in item) + "_" + kinds[0]
        batches = [[]]
        for u in units:
            if any(v["item"] == u["item"] for v in batches[-1]):
                batches.append([])
            batches[-1].append(u)
        for batch in batches:
            self.n_alone += 1
            self._send(None, batch, functools.partial(_run_rider, f"alone{self.n_alone}_{tag}"))

    def begin(self):
        self._flush(("w", 0, "in_t"), ("gather",))

    def _finish_gather(self, item):
        while self.left[item] > 0:
            assert any(u["item"] == item for u in self.queue), item
            self._flush(item, ("gather",))
        return self.buf[item]

    def weight(self, i, k):
        o = self._finish_gather(("w", i, k))
        return o.reshape(-1, o.shape[-1])

    def grads(self, i, group):
        for k, g in group.items():
            item = ("g", i, k)
            self.part[item] = g.reshape(N_DEV, g.shape[0] // N_DEV, g.shape[1])
            us = g.size // N_DEV * g.dtype.itemsize / 2 ** 20 * D2D_US_PER_MIB
            self.queue.insert(0, dict(kind="sibling", item=item, rows=None, us=us))

    def total(self, k):
        out = self.n_layers
        for i in range(self.n_layers):
            item = ("g", i, k)
            self._flush(item, ("sibling",))
            self._flush(item, ("chips",))
            name = f"sum_l{i}_{k}"
            rider = self.rider(name, budget_us=self.part[item][0].size / 1.06e5) if k in SUMS_TRANSPOSED_LATER else None
            out = _sum5(name, self.part[item], self.rsib[item], self.buf[item], self.ids, i, out, rider=rider)
            if rider is not None:
                out, r_outs = out
                self.deliver(name, r_outs)
        return out

    def small_ready(self, i, dmod, ds):
        parts = [dmod[0], dmod[1]] + [ds[nm] for nm in LAYER_SMALL + ("conv_w",)]
        self.small_shapes = [p.shape for p in parts]
        self.gather_small(f"lat{i}", _pack([dmod[1]]))
        self.gather_small(f"small{i}", _pack(parts))

    def gather_small(self, name, arr):
        item = ("s", name)
        self.part[item] = arr
        waiting, self.queue = self.queue, []
        self._push_chunks("gather", item, arr.shape, arr.dtype)
        self.queue += waiting

    def gathered(self, name):
        return self._finish_gather(("s", name))


def _row_tile(n, pref, mult):
    best = None
    t = mult
    while t <= min(n, pref):
        if n % t == 0:
            best = t
        t += mult
    return best if best is not None else n


def _sum8(name, slabs):
    _, n, k = slabs.shape
    tr = _row_tile(n, 128, 16)

    def body(s_ref, o_ref):
        acc = s_ref[0].astype(F32)
        for j in range(1, N_DEV):
            acc = acc + s_ref[j].astype(F32)
        o_ref[...] = acc

    return pl.pallas_call(
        body, name=name, grid=(n // tr,),
        in_specs=[pl.BlockSpec((N_DEV, tr, k), lambda i: (0, i, 0))],
        out_specs=pl.BlockSpec((tr, k), lambda i: (i, 0)),
        out_shape=jax.ShapeDtypeStruct((n, k), F32),
        compiler_params=_cparams("parallel"),
    )(slabs)


def _adamw(name, w, g, m, v, rider=None):
    n, k = w.shape[-2:]
    tr = _row_tile(n, 256, 8)

    def body(w_ref, g_ref, m_ref, v_ref, d_ref, m2_ref, v2_ref):
        gv = g_ref[...]
        m2 = ADAM_B1 * m_ref[...] + (1.0 - ADAM_B1) * gv
        v2 = ADAM_B2 * v_ref[...] + (1.0 - ADAM_B2) * jnp.square(gv)
        m_hat = m2 / (1.0 - ADAM_B1 ** ADAM_STEP)
        v_hat = v2 / (1.0 - ADAM_B2 ** ADAM_STEP)
        d_ref[...] = -ADAM_LR * (m_hat / (jnp.sqrt(v_hat) + ADAM_EPS) + ADAM_WD * w_ref[...])
        m2_ref[...] = m2
        v2_ref[...] = v2

    if w.ndim == 2:
        grid, spec = (n // tr,), pl.BlockSpec((tr, k), lambda i: (i, 0))
    else:
        grid, spec = (w.shape[0], n // tr), pl.BlockSpec((None, tr, k), lambda l, i: (l, i, 0))
    outs, r_outs = _pcall(name, body, grid, [spec] * 4, [spec] * 3, [jax.ShapeDtypeStruct(w.shape, F32)] * 3,
                          (w, g, m, v), ("parallel",) * len(grid), rider=rider)
    return outs if rider is None else (outs, r_outs)


def _pack(arrs):
    flat = jnp.concatenate([a.reshape(-1).astype(F32) for a in arrs])
    pad = (-flat.shape[0]) % 2048
    if pad:
        flat = jnp.concatenate([flat, jnp.zeros((pad,), F32)])
    return flat.reshape(-1, 128)


def _unpack(packed, shapes):
    flat = packed.reshape(-1)
    out, off = [], 0
    for shp in shapes:
        size = math.prod(shp)
        out.append(flat[off:off + size].reshape(shp))
        off += size
    return out


WEIGHT_NAMES = ("c_ctx", "w_ada", "b_ada", "w_in", "q_norm_g", "k_norm_g", "pool_w", "pool_scale", "sgu_ln_g",
                "sgu_ln_b", "sgu_w", "sgu_b", "conv_w", "w_br_attn", "w_br_pool", "w_br_sgu", "w_br_conv", "w_gate",
                "b_gate", "w_o", "ln1_g", "ln1_b", "w_ff_gate", "w_ff_up", "w_ff_down", "ln2_g", "ln2_b")
COL_SHARDED = {"w_in": "in_t", "w_gate": "gate_t", "w_ff_gate": "ffg_t", "w_ff_up": "ffu_t",
               "w_br_attn": "br0", "w_br_pool": "br1", "w_br_sgu": "br2", "w_br_conv": "br3"}
ROW_SHARDED = {"w_o": "o", "w_ff_down": "ffd"}
LAYER_SMALL = ("q_norm_g", "k_norm_g", "pool_w", "pool_scale", "sgu_ln_g", "sgu_ln_b", "sgu_w", "sgu_b", "b_gate",
               "ln1_g", "ln1_b", "ln2_g", "ln2_b")
SMALL_ORDER = ("c_ctx", "b_ada") + LAYER_SMALL + ("conv_w",)


def _train_step(a):
    n_layers, d = a["w_in"].shape[0], a["x"].shape[-1]
    rc = a["ctx"].shape[1]
    alpha = (2 * n_layers) ** 0.25
    mx, my, mc = [lax.axis_index(ax) for ax in MESH_AXES]
    me = 4 * mx + 2 * my + mc
    ada_w = a["w_ada"].shape[-1]
    cw_loc = a["conv_w"].shape[-1]

    comm = _Comm([{**{key: jnp.swapaxes(a[nm], 1, 2)[i].astype(BF16) for nm, key in COL_SHARDED.items()},
                   **{key: a[nm][i].astype(BF16) for nm, key in ROW_SHARDED.items()}} for i in range(n_layers)])

    def carried(name, *args, budget_us, **kw):
        rider = comm.rider(name, budget_us=budget_us)
        if rider is None:
            return _mm(name, *args, **kw)
        res, r_outs = _mm(name, *args, rider=rider, **kw)
        comm.deliver(name, r_outs)
        return res

    n_c, n_cw = d, n_layers * 3 * cw_loc
    got = comm.exchange("gather_cond", _pack([a["c"], a["conv_w"]]), budget_us=12).reshape(N_DEV, -1)
    c_all = got[:, :n_c]
    conv_w = got[:, n_c:n_c + n_cw].reshape(N_DEV, n_layers, 3, cw_loc).transpose(1, 2, 0, 3).reshape(n_layers, 3, -1)
    cond = jnp.concatenate([c_all, a["c_ctx"][None], jnp.zeros((16 - N_DEV - 1, d), F32)], axis=0)
    sil, sil_vjp = jax.vjp(jax.nn.silu, cond)
    sil = sil.astype(BF16)

    mod_cols = jnp.concatenate([carried(f"ada{i}", sil, a["w_ada"][i], "nn", F32, budget_us=20)
                                for i in range(n_layers)], axis=0)
    got = comm.exchange("gather_mod", mod_cols, budget_us=22)
    mod_all = got.reshape(N_DEV, n_layers, 16, ada_w).transpose(1, 2, 0, 3).reshape(n_layers, 16, -1)
    mod_all = mod_all + a["b_ada"][:, None, :]
    mod = jnp.stack([mod_all[:, N_DEV], lax.dynamic_index_in_dim(mod_all, me, axis=1, keepdims=False)], axis=1)
    comm.begin()
    sp = [{nm: a[nm][i] for nm in LAYER_SMALL} for i in range(n_layers)]
    for i in range(n_layers):
        sp[i]["conv_w"] = conv_w[i]

    xin = jnp.concatenate([a["ctx"][0], a["x"][0]], axis=0)
    loss_l, grad_x = _local_step(xin, a["loss_target"][0], mod, comm, sp, rc, alpha)
    loss = lax.psum(loss_l, MESH_AXES)
    grads = {}

    def transposed_home(nm):
        return nm in COL_SHARDED and a[nm].shape[-1] % 128 != 0

    delta, new_m, new_v = {}, {}, {}

    def adamw(nm):
        name = f"adamw_{nm}"
        there = transposed_home(nm)
        view = (lambda t: jnp.swapaxes(t, 1, 2)) if there else (lambda t: t)
        if nm in COL_SHARDED:
            g = comm.total(COL_SHARDED[nm])
            grads[nm] = jnp.swapaxes(g, 1, 2)
            g = g if there else grads[nm]
        elif nm in ROW_SHARDED:
            g = grads[nm] = comm.total(ROW_SHARDED[nm])
        else:
            g = grads[nm]
        res = _adamw(name, view(a[nm]), g, view(a["m_" + nm]), view(a["v_" + nm]))
        delta[nm], new_m[nm], new_v[nm] = [view(t) for t in res]

    for nm in ("w_ff_down", "w_ff_gate", "w_ff_up", "w_o", "w_br_attn", "w_br_pool", "w_br_sgu", "w_br_conv"):
        adamw(nm)

    tots = [_unpack(_sum8(f"sum_small{i}", comm.gathered(f"small{i}")), comm.small_shapes) for i in range(n_layers)]
    dmod_c, dmod_lat_sum = jnp.stack([t[0] for t in tots]), jnp.stack([t[1] for t in tots])
    for j, nm in enumerate(LAYER_SMALL + ("conv_w",)):
        grads[nm] = jnp.stack([t[2 + j] for t in tots])
    grads["conv_w"] = lax.dynamic_slice_in_dim(grads["conv_w"], me * cw_loc, cw_loc, axis=2)
    grads["b_ada"] = dmod_c + dmod_lat_sum
    dmod_lat_all = jnp.stack([comm.gathered(f"lat{i}").reshape(N_DEV, -1)[:, :6 * d] for i in range(n_layers)])
    dm_rows = jnp.concatenate([dmod_lat_all, dmod_c[:, None, :],
                               jnp.zeros((n_layers, 16 - N_DEV - 1, 6 * d), F32)], axis=1)
    dm_cols = lax.dynamic_slice_in_dim(dm_rows, me * ada_w, ada_w, axis=2).astype(BF16)
    grads["w_ada"] = jnp.stack([carried(f"dWada{i}", sil, dm_cols[i], "tn", F32, budget_us=20)
                                for i in range(n_layers)])
    dsil = None
    for i in range(n_layers):
        dsil = carried(f"dsil{i}", dm_cols[i], a["w_ada"][i], "nt", F32, acc=dsil, budget_us=10)
    dsil = _sum8("sum_dsil", comm.exchange("gather_dsil", dsil, budget_us=17))
    grads["c_ctx"] = sil_vjp(dsil)[0][N_DEV]

    for nm in ("w_ada", "w_in", "w_gate"):
        adamw(nm)
    shapes = [a[nm].shape for nm in SMALL_ORDER]
    res = _adamw("adamw_small", _pack([a[nm] for nm in SMALL_ORDER]), _pack([grads[nm] for nm in SMALL_ORDER]),
                 _pack([a["m_" + nm] for nm in SMALL_ORDER]), _pack([a["v_" + nm] for nm in SMALL_ORDER]))
    for tree, packed in zip((delta, new_m, new_v), res):
        for nm, t in zip(SMALL_ORDER, _unpack(packed, shapes)):
            tree[nm] = t
    return (loss, grad_x[None], *[grads[nm] for nm in WEIGHT_NAMES], *[delta[nm] for nm in WEIGHT_NAMES],
            *[new_m[nm] for nm in WEIGHT_NAMES], *[new_v[nm] for nm in WEIGHT_NAMES])


def kernel(x, c, ctx, c_ctx, w_ada, b_ada, w_in, q_norm_g, k_norm_g, pool_w, pool_scale, sgu_ln_g, sgu_ln_b, sgu_w, sgu_b, conv_w, w_br_attn, w_br_pool, w_br_sgu, w_br_conv, w_gate, b_gate, w_o, ln1_g, ln1_b, w_ff_gate, w_ff_up, w_ff_down, ln2_g, ln2_b, loss_target, m_c_ctx, m_w_ada, m_b_ada, m_w_in, m_q_norm_g, m_k_norm_g, m_pool_w, m_pool_scale, m_sgu_ln_g, m_sgu_ln_b, m_sgu_w, m_sgu_b, m_conv_w, m_w_br_attn, m_w_br_pool, m_w_br_sgu, m_w_br_conv, m_w_gate, m_b_gate, m_w_o, m_ln1_g, m_ln1_b, m_w_ff_gate, m_w_ff_up, m_w_ff_down, m_ln2_g, m_ln2_b, v_c_ctx, v_w_ada, v_b_ada, v_w_in, v_q_norm_g, v_k_norm_g, v_pool_w, v_pool_scale, v_sgu_ln_g, v_sgu_ln_b, v_sgu_w, v_sgu_b, v_conv_w, v_w_br_attn, v_w_br_pool, v_w_br_sgu, v_w_br_conv, v_w_gate, v_b_gate, v_w_o, v_ln1_g, v_ln1_b, v_w_ff_gate, v_w_ff_up, v_w_ff_down, v_ln2_g, v_ln2_b):
    names = list(WEIGHT_NAMES)
    args = dict(zip(
        ["x", "c", "ctx"] + names + ["loss_target"] + ["m_" + n for n in names] + ["v_" + n for n in names],
        (x, c, ctx, c_ctx, w_ada, b_ada, w_in, q_norm_g, k_norm_g, pool_w, pool_scale, sgu_ln_g, sgu_ln_b, sgu_w, sgu_b, conv_w, w_br_attn, w_br_pool, w_br_sgu, w_br_conv, w_gate, b_gate, w_o, ln1_g, ln1_b, w_ff_gate, w_ff_up, w_ff_down, ln2_g, ln2_b, loss_target, m_c_ctx, m_w_ada, m_b_ada, m_w_in, m_q_norm_g, m_k_norm_g, m_pool_w, m_pool_scale, m_sgu_ln_g, m_sgu_ln_b, m_sgu_w, m_sgu_b, m_conv_w, m_w_br_attn, m_w_br_pool, m_w_br_sgu, m_w_br_conv, m_w_gate, m_b_gate, m_w_o, m_ln1_g, m_ln1_b, m_w_ff_gate, m_w_ff_up, m_w_ff_down, m_ln2_g, m_ln2_b, v_c_ctx, v_w_ada, v_b_ada, v_w_in, v_q_norm_g, v_k_norm_g, v_pool_w, v_pool_scale, v_sgu_ln_g, v_sgu_ln_b, v_sgu_w, v_sgu_b, v_conv_w, v_w_br_attn, v_w_br_pool, v_w_br_sgu, v_w_br_conv, v_w_gate, v_b_gate, v_w_o, v_ln1_g, v_ln1_b, v_w_ff_gate, v_w_ff_up, v_w_ff_down, v_ln2_g, v_ln2_b)))
    return _train_step(args)
```
